```python
import jax, jax.numpy as jnp
from jax import lax
import numpy as np

D_MODEL = 1024
BATCH = 8
SEQ = 4096
DEPTH = 4

CHUNK = 64
N_MIXERS = 2
EPS = 1e-6
GM_BLOCK = 128
GM_HEADS = 8
GM_WIDTH = 2 * D_MODEL
GM_HEAD_DIM = GM_WIDTH // GM_HEADS
HG_EXPAND = 128
HG_HEADS = D_MODEL // HG_EXPAND
HG_KEY = HG_EXPAND
HG_VAL = D_MODEL // HG_HEADS
FFN_HIDDEN = 2816
CONV_WIDTH = 3
N_A = (DEPTH + 1) // 2
N_B = DEPTH // 2

kernel_name = "hybrid_gmlp_hgrn2_convffn_adaln"


def rms_norm(x, g):
    xf = x.astype(jnp.float32)
    y = xf * lax.rsqrt(jnp.mean(xf * xf, axis=-1, keepdims=True) + EPS)
    return (y * g.astype(jnp.float32)).astype(x.dtype)


def chunk_causal_mask(n):
    idx = jnp.arange(n) // CHUNK
    return idx[:, None] >= idx[None, :]


def spatial_gating_mixer(h, w_in, ln_g, ln_b, w_s, b_s, w_out):
    bsz, t, _ = h.shape
    z = jax.nn.gelu(h @ w_in, approximate=False)
    u, v = jnp.split(z, 2, axis=-1)
    vf = v.astype(jnp.float32)
    mu = jnp.mean(vf, axis=-1, keepdims=True)
    var = jnp.mean(jnp.square(vf - mu), axis=-1, keepdims=True)
    v = ((vf - mu) * lax.rsqrt(var + EPS) * ln_g + ln_b).astype(h.dtype)
    v = v.reshape(bsz, t // GM_BLOCK, GM_BLOCK, GM_HEADS, GM_HEAD_DIM)
    ws = jnp.where(chunk_causal_mask(GM_BLOCK)[None], w_s, 0)
    s = jnp.einsum('hnm,bcmhd->bcnhd', ws, v) + b_s.T[None, None, :, :, None]
    gated = u * s.reshape(bsz, t, GM_WIDTH)
    return gated @ w_out


def hgrn2_chunked_scan(q, k, v, logf):
    bsz, t, nh, dk = q.shape
    dv = v.shape[-1]
    nc = t // CHUNK

    def to_chunks(a):
        return a.reshape(bsz, nc, CHUNK, nh, a.shape[-1]).transpose(1, 0, 3, 2, 4)

    causal = jnp.tril(jnp.ones((CHUNK, CHUNK), dtype=bool))[:, :, None]

    def step(state, inp):
        qc, kc, vc, gc = inp
        cum = jnp.cumsum(gc, axis=-2)
        rel = cum[..., :, None, :] - cum[..., None, :, :]
        decay = jnp.exp(jnp.where(causal, rel, -jnp.inf))
        scores = jnp.einsum('bhik,bhjk,bhijk->bhij', qc, kc, decay)
        out = jnp.einsum('bhij,bhjv->bhiv', scores, vc) + \
            jnp.einsum('bhik,bhkv->bhiv', qc * jnp.exp(cum), state)
        last = cum[..., -1:, :]
        state = jnp.exp(last)[..., 0, :, None] * state + \
            jnp.einsum('bhjk,bhjv->bhkv', kc * jnp.exp(last - cum), vc)
        return state, out

    s0 = jnp.zeros((bsz, nh, dk, dv), jnp.float32)
    _, out = lax.scan(step, s0, (to_chunks(q), to_chunks(k), to_chunks(v), to_chunks(logf)))
    return out.transpose(1, 0, 3, 2, 4).reshape(bsz, t, nh, dv)


def hgrn2_mixer(h, w_in, lb, gn_g, w_out):
    bsz, t, _ = h.shape
    q, fz, i, g = jnp.split(h @ w_in, 4, axis=-1)
    f = lb + (1.0 - lb) * jax.nn.sigmoid(fz.astype(jnp.float32))
    logf = jnp.log(f)
    k = 1.0 - f
    q = jax.nn.silu(q).astype(jnp.float32)

    def heads(a):
        return a.reshape(bsz, t, HG_HEADS, -1)

    o = hgrn2_chunked_scan(heads(q), heads(k), heads(i.astype(jnp.float32)), heads(logf))
    o = o * lax.rsqrt(jnp.mean(o * o, axis=-1, keepdims=True) + EPS)
    o = (o.reshape(bsz, t, HG_HEADS * HG_VAL) * gn_g.astype(jnp.float32)).astype(h.dtype)
    return (o * jax.nn.silu(g)) @ w_out


def conv_ffn(h, w_up, conv_w, conv_b, w_down):
    t = h.shape[1]
    a = h @ w_up
    ap = jnp.pad(a, ((0, 0), (CONV_WIDTH - 1, 0), (0, 0)))
    y = conv_b
    for j in range(CONV_WIDTH):
        y = y + conv_w[j] * ap[:, j:j + t]
    gate, val = jnp.split(y, 2, axis=-1)
    return (jax.nn.gelu(gate, approximate=False) * val) @ w_down


def _fwd_setup_inputs(seed: int = 0) -> dict:
    key = jax.random.key(seed)
    ks = jax.random.split(key, 20)
    nrm = jax.random.normal
    f32 = jnp.float32
    D, F2 = D_MODEL, 2 * FFN_HIDDEN
    return {
        "x": nrm(ks[0], (BATCH, SEQ, D), f32),
        "c": nrm(ks[1], (BATCH, D), f32),
        "gm_w_in": nrm(ks[2], (N_A, D, 2 * GM_WIDTH), f32) * D ** -0.5,
        "gm_ln_g": 1.0 + 0.02 * nrm(ks[3], (N_A, GM_WIDTH), f32),
        "gm_ln_b": 0.02 * nrm(ks[4], (N_A, GM_WIDTH), f32),
        "gm_w_s": nrm(ks[5], (N_A, GM_HEADS, GM_BLOCK, GM_BLOCK), f32) * GM_BLOCK ** -0.5,
        "gm_b_s": 1.0 + 0.1 * nrm(ks[6], (N_A, GM_HEADS, GM_BLOCK), f32),
        "gm_w_out": nrm(ks[7], (N_A, GM_WIDTH, D), f32) * GM_WIDTH ** -0.5,
        "hg_w_in": nrm(ks[8], (N_B, D, 4 * D), f32) * D ** -0.5,
        "hg_lb": 0.5 * nrm(ks[9], (N_B, D), f32),
        "hg_gn_g": 1.0 + 0.02 * nrm(ks[10], (N_B, D), f32),
        "hg_w_out": nrm(ks[11], (N_B, D, D), f32) * D ** -0.5,
        "ffn_w_up": nrm(ks[12], (DEPTH, D, F2), f32) * D ** -0.5,
        "ffn_conv_w": nrm(ks[13], (DEPTH, CONV_WIDTH, F2), f32) * CONV_WIDTH ** -0.5,
        "ffn_conv_b": 0.02 * nrm(ks[14], (DEPTH, F2), f32),
        "ffn_w_down": nrm(ks[15], (DEPTH, FFN_HIDDEN, D), f32) * FFN_HIDDEN ** -0.5,
        "norm_g": 1.0 + 0.02 * nrm(ks[16], (DEPTH, 2, D), f32),
        "ada_w": nrm(ks[17], (DEPTH, D, 6 * D), f32) * D ** -0.5,
        "ada_b": 0.02 * nrm(ks[18], (DEPTH, 6 * D), f32),
        "final_g": 1.0 + 0.02 * nrm(ks[19], (D,), f32),
    }


def _fwd_reference(x, c, gm_w_in, gm_ln_g, gm_ln_b, gm_w_s, gm_b_s, gm_w_out,
              hg_w_in, hg_lb, hg_gn_g, hg_w_out,
              ffn_w_up, ffn_conv_w, ffn_conv_b, ffn_w_down,
              norm_g, ada_w, ada_b, final_g):
    lb_p = jax.nn.softmax(hg_lb.astype(jnp.float32), axis=0)
    lb_all = jnp.cumsum(lb_p, axis=0) - lb_p[0]
    cond = jax.nn.silu(c)
    for i in range(DEPTH):
        mod = (cond @ ada_w[i] + ada_b[i])[:, None, :]
        sh1, sc1, g1, sh2, sc2, g2 = jnp.split(mod, 6, axis=-1)
        h = rms_norm(x, norm_g[i, 0]) * (1.0 + sc1) + sh1
        j = i // N_MIXERS
        if i % N_MIXERS == 0:
            y = spatial_gating_mixer(h, gm_w_in[j], gm_ln_g[j], gm_ln_b[j],
                                     gm_w_s[j], gm_b_s[j], gm_w_out[j])
        else:
            y = hgrn2_mixer(h, hg_w_in[j], lb_all[j], hg_gn_g[j], hg_w_out[j])
        x = x + g1 * y
        h = rms_norm(x, norm_g[i, 1]) * (1.0 + sc2) + sh2
        x = x + g2 * conv_ffn(h, ffn_w_up[i], ffn_conv_w[i], ffn_conv_b[i], ffn_w_down[i])
    return rms_norm(x, final_g)


import jax as _jax
import jax.numpy as _jnp

TWIN_FORMAT = 'train_step'
FWD_PARAMS = ['x', 'c', 'gm_w_in', 'gm_ln_g', 'gm_ln_b', 'gm_w_s', 'gm_b_s', 'gm_w_out', 'hg_w_in', 'hg_lb', 'hg_gn_g', 'hg_w_out', 'ffn_w_up', 'ffn_conv_w', 'ffn_conv_b', 'ffn_w_down', 'norm_g', 'ada_w', 'ada_b', 'final_g']
TWIN_WEIGHTS = ['gm_w_in', 'gm_ln_g', 'gm_ln_b', 'gm_w_s', 'gm_b_s', 'gm_w_out', 'hg_w_in', 'hg_lb', 'hg_gn_g', 'hg_w_out', 'ffn_w_up', 'ffn_conv_w', 'ffn_conv_b', 'ffn_w_down', 'norm_g', 'ada_w', 'ada_b', 'final_g']
TWIN_DIFF_INPUT = 'x'
TWIN_INPUTS = ['x', 'c', 'gm_w_in', 'gm_ln_g', 'gm_ln_b', 'gm_w_s', 'gm_b_s', 'gm_w_out', 'hg_w_in', 'hg_lb', 'hg_gn_g', 'hg_w_out', 'ffn_w_up', 'ffn_conv_w', 'ffn_conv_b', 'ffn_w_down', 'norm_g', 'ada_w', 'ada_b', 'final_g', 'loss_target', 'm_gm_w_in', 'm_gm_ln_g', 'm_gm_ln_b', 'm_gm_w_s', 'm_gm_b_s', 'm_gm_w_out', 'm_hg_w_in', 'm_hg_lb', 'm_hg_gn_g', 'm_hg_w_out', 'm_ffn_w_up', 'm_ffn_conv_w', 'm_ffn_conv_b', 'm_ffn_w_down', 'm_norm_g', 'm_ada_w', 'm_ada_b', 'm_final_g', 'v_gm_w_in', 'v_gm_ln_g', 'v_gm_ln_b', 'v_gm_w_s', 'v_gm_b_s', 'v_gm_w_out', 'v_hg_w_in', 'v_hg_lb', 'v_hg_gn_g', 'v_hg_w_out', 'v_ffn_w_up', 'v_ffn_conv_w', 'v_ffn_conv_b', 'v_ffn_w_down', 'v_norm_g', 'v_ada_w', 'v_ada_b', 'v_final_g']
TWIN_OUTPUTS = ['loss', 'grad_x', 'grad_gm_w_in', 'grad_gm_ln_g', 'grad_gm_ln_b', 'grad_gm_w_s', 'grad_gm_b_s', 'grad_gm_w_out', 'grad_hg_w_in', 'grad_hg_lb', 'grad_hg_gn_g', 'grad_hg_w_out', 'grad_ffn_w_up', 'grad_ffn_conv_w', 'grad_ffn_conv_b', 'grad_ffn_w_down', 'grad_norm_g', 'grad_ada_w', 'grad_ada_b', 'grad_final_g', 'delta_gm_w_in', 'delta_gm_ln_g', 'delta_gm_ln_b', 'delta_gm_w_s', 'delta_gm_b_s', 'delta_gm_w_out', 'delta_hg_w_in', 'delta_hg_lb', 'delta_hg_gn_g', 'delta_hg_w_out', 'delta_ffn_w_up', 'delta_ffn_conv_w', 'delta_ffn_conv_b', 'delta_ffn_w_down', 'delta_norm_g', 'delta_ada_w', 'delta_ada_b', 'delta_final_g', 'new_m_gm_w_in', 'new_m_gm_ln_g', 'new_m_gm_ln_b', 'new_m_gm_w_s', 'new_m_gm_b_s', 'new_m_gm_w_out', 'new_m_hg_w_in', 'new_m_hg_lb', 'new_m_hg_gn_g', 'new_m_hg_w_out', 'new_m_ffn_w_up', 'new_m_ffn_conv_w', 'new_m_ffn_conv_b', 'new_m_ffn_w_down', 'new_m_norm_g', 'new_m_ada_w', 'new_m_ada_b', 'new_m_final_g', 'new_v_gm_w_in', 'new_v_gm_ln_g', 'new_v_gm_ln_b', 'new_v_gm_w_s', 'new_v_gm_b_s', 'new_v_gm_w_out', 'new_v_hg_w_in', 'new_v_hg_lb', 'new_v_hg_gn_g', 'new_v_hg_w_out', 'new_v_ffn_w_up', 'new_v_ffn_conv_w', 'new_v_ffn_conv_b', 'new_v_ffn_w_down', 'new_v_norm_g', 'new_v_ada_w', 'new_v_ada_b', 'new_v_final_g']
TWIN_LEAF_KINDS = {'loss': 'loss', 'grad_x': 'grad_x', 'grad_gm_w_in': 'grad_w', 'grad_gm_ln_g': 'grad_w', 'grad_gm_ln_b': 'grad_w', 'grad_gm_w_s': 'grad_w', 'grad_gm_b_s': 'grad_w', 'grad_gm_w_out': 'grad_w', 'grad_hg_w_in': 'grad_w', 'grad_hg_lb': 'grad_w', 'grad_hg_gn_g': 'grad_w', 'grad_hg_w_out': 'grad_w', 'grad_ffn_w_up': 'grad_w', 'grad_ffn_conv_w': 'grad_w', 'grad_ffn_conv_b': 'grad_w', 'grad_ffn_w_down': 'grad_w', 'grad_norm_g': 'grad_w', 'grad_ada_w': 'grad_w', 'grad_ada_b': 'grad_w', 'grad_final_g': 'grad_w', 'delta_gm_w_in': 'delta_w', 'delta_gm_ln_g': 'delta_w', 'delta_gm_ln_b': 'delta_w', 'delta_gm_w_s': 'delta_w', 'delta_gm_b_s': 'delta_w', 'delta_gm_w_out': 'delta_w', 'delta_hg_w_in': 'delta_w', 'delta_hg_lb': 'delta_w', 'delta_hg_gn_g': 'delta_w', 'delta_hg_w_out': 'delta_w', 'delta_ffn_w_up': 'delta_w', 'delta_ffn_conv_w': 'delta_w', 'delta_ffn_conv_b': 'delta_w', 'delta_ffn_w_down': 'delta_w', 'delta_norm_g': 'delta_w', 'delta_ada_w': 'delta_w', 'delta_ada_b': 'delta_w', 'delta_final_g': 'delta_w', 'new_m_gm_w_in': 'new_m', 'new_m_gm_ln_g': 'new_m', 'new_m_gm_ln_b': 'new_m', 'new_m_gm_w_s': 'new_m', 'new_m_gm_b_s': 'new_m', 'new_m_gm_w_out': 'new_m', 'new_m_hg_w_in': 'new_m', 'new_m_hg_lb': 'new_m', 'new_m_hg_gn_g': 'new_m', 'new_m_hg_w_out': 'new_m', 'new_m_ffn_w_up': 'new_m', 'new_m_ffn_conv_w': 'new_m', 'new_m_ffn_conv_b': 'new_m', 'new_m_ffn_w_down': 'new_m', 'new_m_norm_g': 'new_m', 'new_m_ada_w': 'new_m', 'new_m_ada_b': 'new_m', 'new_m_final_g': 'new_m', 'new_v_gm_w_in': 'new_v', 'new_v_gm_ln_g': 'new_v', 'new_v_gm_ln_b': 'new_v', 'new_v_gm_w_s': 'new_v', 'new_v_gm_b_s': 'new_v', 'new_v_gm_w_out': 'new_v', 'new_v_hg_w_in': 'new_v', 'new_v_hg_lb': 'new_v', 'new_v_hg_gn_g': 'new_v', 'new_v_hg_w_out': 'new_v', 'new_v_ffn_w_up': 'new_v', 'new_v_ffn_conv_w': 'new_v', 'new_v_ffn_conv_b': 'new_v', 'new_v_ffn_w_down': 'new_v', 'new_v_norm_g': 'new_v', 'new_v_ada_w': 'new_v', 'new_v_ada_b': 'new_v', 'new_v_final_g': 'new_v'}


def _forward(args):
    return _fwd_reference(*[args[k] for k in FWD_PARAMS])


def _output_shape():
    def fwd():
        inp = _fwd_setup_inputs(0)
        return _fwd_reference(*[inp[k] for k in FWD_PARAMS])
    out = _jax.eval_shape(fwd)
    return out.shape, out.dtype

N_MICROBATCH = 1
ADAM_LR = 0.001
ADAM_B1 = 0.9
ADAM_B2 = 0.999
ADAM_EPS = 1e-08
ADAM_WD = 0.01
ADAM_STEP = 10
PER_EXAMPLE_BATCH_AXIS = {'x': 0, 'c': 0, 'loss_target': 0}
SHARED_INPUTS = []
_WEIGHT_DTYPES = {'gm_w_in': _jnp.float32, 'gm_ln_g': _jnp.float32, 'gm_ln_b': _jnp.float32, 'gm_w_s': _jnp.float32, 'gm_b_s': _jnp.float32, 'gm_w_out': _jnp.float32, 'hg_w_in': _jnp.float32, 'hg_lb': _jnp.float32, 'hg_gn_g': _jnp.float32, 'hg_w_out': _jnp.float32, 'ffn_w_up': _jnp.float32, 'ffn_conv_w': _jnp.float32, 'ffn_conv_b': _jnp.float32, 'ffn_w_down': _jnp.float32, 'norm_g': _jnp.float32, 'ada_w': _jnp.float32, 'ada_b': _jnp.float32, 'final_g': _jnp.float32}
MOMENT_SCALE = {'gm_w_in': 6.849400e-02, 'gm_ln_g': 4.909576e-02, 'gm_ln_b': 4.730022e-02, 'gm_w_s': 6.934870e-02, 'gm_b_s': 8.100418e-02, 'gm_w_out': 1.100977e-01, 'hg_w_in': 4.308712e-02, 'hg_lb': 2.195702e-03, 'hg_gn_g': 5.898229e-02, 'hg_w_out': 5.959214e-02, 'ffn_w_up': 5.104587e-02, 'ffn_conv_w': 5.143608e-02, 'ffn_conv_b': 3.832921e-02, 'ffn_w_down': 8.440142e-02, 'norm_g': 1.027098e-01, 'ada_w': 6.956791e-02, 'ada_b': 1.170599e-01, 'final_g': 3.239397e+01}


def _to_microbatches(a, axis):
    t = _jnp.moveaxis(a, axis, 0)
    t = t.reshape((N_MICROBATCH, t.shape[0] // N_MICROBATCH) + t.shape[1:])
    return _jnp.moveaxis(t, 1, axis + 1)


def setup_inputs(seed: int = 0) -> dict:
    inp = _fwd_setup_inputs(seed)
    key = _jax.random.fold_in(_jax.random.key(seed), 7919)
    shape, _ = _output_shape()
    out = dict(inp)
    out["loss_target"] = _jax.random.normal(_jax.random.fold_in(key, 0), shape, _jnp.float32)
    for i, name in enumerate(TWIN_WEIGHTS):
        w = inp[name].astype(_jnp.float32)
        if MOMENT_SCALE is None:
            s = _jnp.sqrt(_jnp.mean(_jnp.square(w)) + 1e-30)
        else:
            s = MOMENT_SCALE[name]
        km, kv = _jax.random.split(_jax.random.fold_in(key, i + 1))
        out[name] = w
        out["m_" + name] = s * _jax.random.normal(km, w.shape, _jnp.float32)
        out["v_" + name] = (s * s) * _jax.random.uniform(kv, w.shape, _jnp.float32, 0.5, 1.5)
    if N_MICROBATCH > 1:
        for name, axis in PER_EXAMPLE_BATCH_AXIS.items():
            out[name] = _to_microbatches(out[name], axis)
    return {'x': out['x'], 'c': out['c'], 'gm_w_in': out['gm_w_in'], 'gm_ln_g': out['gm_ln_g'], 'gm_ln_b': out['gm_ln_b'], 'gm_w_s': out['gm_w_s'], 'gm_b_s': out['gm_b_s'], 'gm_w_out': out['gm_w_out'], 'hg_w_in': out['hg_w_in'], 'hg_lb': out['hg_lb'], 'hg_gn_g': out['hg_gn_g'], 'hg_w_out': out['hg_w_out'], 'ffn_w_up': out['ffn_w_up'], 'ffn_conv_w': out['ffn_conv_w'], 'ffn_conv_b': out['ffn_conv_b'], 'ffn_w_down': out['ffn_w_down'], 'norm_g': out['norm_g'], 'ada_w': out['ada_w'], 'ada_b': out['ada_b'], 'final_g': out['final_g'], 'loss_target': out['loss_target'], 'm_gm_w_in': out['m_gm_w_in'], 'm_gm_ln_g': out['m_gm_ln_g'], 'm_gm_ln_b': out['m_gm_ln_b'], 'm_gm_w_s': out['m_gm_w_s'], 'm_gm_b_s': out['m_gm_b_s'], 'm_gm_w_out': out['m_gm_w_out'], 'm_hg_w_in': out['m_hg_w_in'], 'm_hg_lb': out['m_hg_lb'], 'm_hg_gn_g': out['m_hg_gn_g'], 'm_hg_w_out': out['m_hg_w_out'], 'm_ffn_w_up': out['m_ffn_w_up'], 'm_ffn_conv_w': out['m_ffn_conv_w'], 'm_ffn_conv_b': out['m_ffn_conv_b'], 'm_ffn_w_down': out['m_ffn_w_down'], 'm_norm_g': out['m_norm_g'], 'm_ada_w': out['m_ada_w'], 'm_ada_b': out['m_ada_b'], 'm_final_g': out['m_final_g'], 'v_gm_w_in': out['v_gm_w_in'], 'v_gm_ln_g': out['v_gm_ln_g'], 'v_gm_ln_b': out['v_gm_ln_b'], 'v_gm_w_s': out['v_gm_w_s'], 'v_gm_b_s': out['v_gm_b_s'], 'v_gm_w_out': out['v_gm_w_out'], 'v_hg_w_in': out['v_hg_w_in'], 'v_hg_lb': out['v_hg_lb'], 'v_hg_gn_g': out['v_hg_gn_g'], 'v_hg_w_out': out['v_hg_w_out'], 'v_ffn_w_up': out['v_ffn_w_up'], 'v_ffn_conv_w': out['v_ffn_conv_w'], 'v_ffn_conv_b': out['v_ffn_conv_b'], 'v_ffn_w_down': out['v_ffn_w_down'], 'v_norm_g': out['v_norm_g'], 'v_ada_w': out['v_ada_w'], 'v_ada_b': out['v_ada_b'], 'v_final_g': out['v_final_g']}


def _loss(weights, diff, rest, loss_target):
    with _jax.named_scope("forward"):
        args = {**rest, TWIN_DIFF_INPUT: diff, **{k: w.astype(_WEIGHT_DTYPES[k]) for k, w in weights.items()}}
        y = _forward(args)
    with _jax.named_scope("loss_head"):
        err = _jnp.square(y.astype(_jnp.float32) - loss_target)
        return 0.5 * _jnp.sum(_jnp.mean(err, axis=-1)) if err.ndim else 0.5 * err


def _adamw(w, g, m, v):
    m = ADAM_B1 * m + (1.0 - ADAM_B1) * g
    v = ADAM_B2 * v + (1.0 - ADAM_B2) * _jnp.square(g)
    m_hat = m / (1.0 - ADAM_B1 ** ADAM_STEP)
    v_hat = v / (1.0 - ADAM_B2 ** ADAM_STEP)
    delta = -ADAM_LR * (m_hat / (_jnp.sqrt(v_hat) + ADAM_EPS) + ADAM_WD * w)
    return delta, m, v


def reference(x, c, gm_w_in, gm_ln_g, gm_ln_b, gm_w_s, gm_b_s, gm_w_out, hg_w_in, hg_lb, hg_gn_g, hg_w_out, ffn_w_up, ffn_conv_w, ffn_conv_b, ffn_w_down, norm_g, ada_w, ada_b, final_g, loss_target, m_gm_w_in, m_gm_ln_g, m_gm_ln_b, m_gm_w_s, m_gm_b_s, m_gm_w_out, m_hg_w_in, m_hg_lb, m_hg_gn_g, m_hg_w_out, m_ffn_w_up, m_ffn_conv_w, m_ffn_conv_b, m_ffn_w_down, m_norm_g, m_ada_w, m_ada_b, m_final_g, v_gm_w_in, v_gm_ln_g, v_gm_ln_b, v_gm_w_s, v_gm_b_s, v_gm_w_out, v_hg_w_in, v_hg_lb, v_hg_gn_g, v_hg_w_out, v_ffn_w_up, v_ffn_conv_w, v_ffn_conv_b, v_ffn_w_down, v_norm_g, v_ada_w, v_ada_b, v_final_g):
    given = dict(x=x, c=c, gm_w_in=gm_w_in, gm_ln_g=gm_ln_g, gm_ln_b=gm_ln_b, gm_w_s=gm_w_s, gm_b_s=gm_b_s, gm_w_out=gm_w_out, hg_w_in=hg_w_in, hg_lb=hg_lb, hg_gn_g=hg_gn_g, hg_w_out=hg_w_out, ffn_w_up=ffn_w_up, ffn_conv_w=ffn_conv_w, ffn_conv_b=ffn_conv_b, ffn_w_down=ffn_w_down, norm_g=norm_g, ada_w=ada_w, ada_b=ada_b, final_g=final_g, loss_target=loss_target, m_gm_w_in=m_gm_w_in, m_gm_ln_g=m_gm_ln_g, m_gm_ln_b=m_gm_ln_b, m_gm_w_s=m_gm_w_s, m_gm_b_s=m_gm_b_s, m_gm_w_out=m_gm_w_out, m_hg_w_in=m_hg_w_in, m_hg_lb=m_hg_lb, m_hg_gn_g=m_hg_gn_g, m_hg_w_out=m_hg_w_out, m_ffn_w_up=m_ffn_w_up, m_ffn_conv_w=m_ffn_conv_w, m_ffn_conv_b=m_ffn_conv_b, m_ffn_w_down=m_ffn_w_down, m_norm_g=m_norm_g, m_ada_w=m_ada_w, m_ada_b=m_ada_b, m_final_g=m_final_g, v_gm_w_in=v_gm_w_in, v_gm_ln_g=v_gm_ln_g, v_gm_ln_b=v_gm_ln_b, v_gm_w_s=v_gm_w_s, v_gm_b_s=v_gm_b_s, v_gm_w_out=v_gm_w_out, v_hg_w_in=v_hg_w_in, v_hg_lb=v_hg_lb, v_hg_gn_g=v_hg_gn_g, v_hg_w_out=v_hg_w_out, v_ffn_w_up=v_ffn_w_up, v_ffn_conv_w=v_ffn_conv_w, v_ffn_conv_b=v_ffn_conv_b, v_ffn_w_down=v_ffn_w_down, v_norm_g=v_norm_g, v_ada_w=v_ada_w, v_ada_b=v_ada_b, v_final_g=v_final_g)
    weights = {n: given[n] for n in TWIN_WEIGHTS}
    shared = {n: given[n] for n in SHARED_INPUTS}
    per_example = {n: given[n] for n in ['x', 'c']}
    grad_fn = _jax.value_and_grad(_loss, argnums=(0, 1))

    def one_microbatch(ex, loss_target):
        ex = dict(ex)
        diff = ex.pop(TWIN_DIFF_INPUT)
        return grad_fn(weights, diff, {**shared, **ex}, loss_target)

    if N_MICROBATCH == 1:
        loss, (grad_w, grad_x) = one_microbatch(per_example, given["loss_target"])
    else:
        def body(carry, xs):
            loss_sum, grad_sum = carry
            l_k, (gw_k, gx_k) = one_microbatch(xs[0], xs[1])
            with _jax.named_scope("update"):
                return (loss_sum + l_k, _jax.tree.map(_jnp.add, grad_sum, gw_k)), gx_k

        init = (_jnp.zeros((), _jnp.float32), _jax.tree.map(_jnp.zeros_like, weights))
        (loss, grad_w), grad_x = _jax.lax.scan(body, init, (per_example, given["loss_target"]))
    with _jax.named_scope("update"):
        delta_w, new_m, new_v = {}, {}, {}
        for n in TWIN_WEIGHTS:
            delta_w[n], new_m[n], new_v[n] = _adamw(weights[n], grad_w[n], given["m_" + n], given["v_" + n])
    return (loss, grad_x, *[grad_w[n] for n in TWIN_WEIGHTS], *[delta_w[n] for n in TWIN_WEIGHTS],
            *[new_m[n] for n in TWIN_WEIGHTS], *[new_v[n] for n in TWIN_WEIGHTS])
```

```python
import functools

import jax
import jax.numpy as jnp
from jax import lax
from jax.experimental import pallas as pl
from jax.experimental.pallas import tpu as pltpu

F32 = jnp.float32
BF16 = jnp.bfloat16
HI = lax.Precision.HIGHEST
MESH = pl.DeviceIdType.MESH

D_MODEL = 1024
DEPTH = 4
EPS = 1e-6
GM_WIDTH = 2048
GM_HEADS = 8
GM_BLOCK = 128
GM_HEAD_DIM = 256
CHUNK = 64
HG_HEADS = 8
HG_DIM = 128
FFN_HIDDEN = 2816
N_CHIPS = 4
N_DEV = 8

ADAM_LR = 0.001
ADAM_B1 = 0.9
ADAM_B2 = 0.999
ADAM_EPS = 1e-08
ADAM_WD = 0.01
ADAM_STEP = 10

VMEM_LIMIT_BYTES = 56 * 1024 * 1024
ROW_TILE = 256
LANES = 128

_SQRT_HALF = 0.7071067811865476
_INV_SQRT_2PI = 0.3989422804014327


def _pick(dim, prefs):
    for p in prefs:
        if dim % p == 0:
            return p
    return dim


def _params(sem):
    return pltpu.CompilerParams(dimension_semantics=sem, vmem_limit_bytes=VMEM_LIMIT_BYTES)


def _cdf(x):
    return 0.5 * (1.0 + lax.erf(x * _SQRT_HALF))


def _pdf(x):
    return jnp.exp(-0.5 * x * x) * _INV_SQRT_2PI


def _sig(x):
    return jax.nn.sigmoid(x)


def _dot(a, b, dims, prec=None):
    return lax.dot_general(a, b, (dims, ((), ())), precision=prec, preferred_element_type=F32)


NN = ((1,), (0,))
NT = ((1,), (1,))
TN = ((0,), (0,))


def _mm(a, b, mode, name, b_layer=None, out_dtype=F32, exchange_out=False):
    b2 = b.shape[-2:]
    if mode == "nn":
        (M, K), (_, N) = a.shape, b2
    elif mode == "nt":
        (M, K), (N, _) = a.shape, b2
    else:
        (K, M), (_, N) = a.shape, b2
    tm = _pick(M, (512, 256, 128))
    tn = _pick(N, (1408, 1024, 512, 256, 128))
    tk = _pick(K, (1408, 1024, 512, 256, 128))
    nk = K // tk
    dims = {"nn": NN, "nt": NT, "tn": TN}[mode]

    def body(a_ref, b_ref, o_ref, *scratch):
        part = _dot(a_ref[...].astype(BF16), b_ref[...].astype(BF16), dims)
        if nk == 1:
            o_ref[...] = part.astype(o_ref.dtype)
            return
        (acc_ref,) = scratch
        k = pl.program_id(2)

        @pl.when(k == 0)
        def _():
            acc_ref[...] = part

        @pl.when(k > 0)
        def _():
            acc_ref[...] += part

        @pl.when(k == nk - 1)
        def _():
            o_ref[...] = acc_ref[...].astype(o_ref.dtype)

    if mode == "tn":
        a_spec = pl.BlockSpec((tk, tm), lambda i, j, k: (k, i))
    else:
        a_spec = pl.BlockSpec((tm, tk), lambda i, j, k: (i, k))
    bblk = (tk, tn) if mode in ("nn", "tn") else (tn, tk)
    bidx = (lambda i, j, k: (k, j)) if mode in ("nn", "tn") else (lambda i, j, k: (j, k))
    if b_layer is None:
        b_spec = pl.BlockSpec(bblk, bidx)
    else:
        b_spec = pl.BlockSpec((None,) + bblk, lambda i, j, k: (b_layer,) + bidx(i, j, k))
    if exchange_out:
        mh, cw = M // 2, N // N_CHIPS
        assert mh % tm == 0 and cw % tn == 0
        out_shape = jax.ShapeDtypeStruct((N_CHIPS, 2, mh, cw), out_dtype)
        o_spec = pl.BlockSpec(
            (None, None, tm, tn),
            lambda i, j, k: (j // (cw // tn), i // (mh // tm), i % (mh // tm), j % (cw // tn)))
    else:
        out_shape = jax.ShapeDtypeStruct((M, N), out_dtype)
        o_spec = pl.BlockSpec((tm, tn), lambda i, j, k: (i, j))
    return pl.pallas_call(
        body, name=name, out_shape=out_shape, grid=(M // tm, N // tn, nk),
        in_specs=[a_spec, b_spec], out_specs=o_spec,
        scratch_shapes=[] if nk == 1 else [pltpu.VMEM((tm, tn), F32)],
        compiler_params=_params(("parallel", "parallel", "arbitrary")),
    )(a, b)


def _row_spec(tr, width):
    return pl.BlockSpec((tr, width), lambda i: (i, 0))


def _vec_spec(width, rows=1):
    return pl.BlockSpec((rows, width), lambda i: (0, 0))


def _norm_fwd(x, y, gate, g, sc, sh, name):
    T, Dm = x.shape
    tr = _pick(T, (ROW_TILE,))
    has_res = y is not None

    def body(*refs):
        if has_res:
            x_ref, y_ref, gate_ref, g_ref, sc_ref, sh_ref, xo_ref, h_ref = refs
            xv = x_ref[...] + gate_ref[...] * y_ref[...]
            xo_ref[...] = xv
        else:
            x_ref, g_ref, sc_ref, sh_ref, h_ref = refs
            xv = x_ref[...]
        rstd = lax.rsqrt(jnp.mean(xv * xv, axis=-1, keepdims=True) + EPS)
        h_ref[...] = ((xv * rstd * g_ref[...]) * (1.0 + sc_ref[...]) + sh_ref[...]).astype(BF16)

    row, vec = _row_spec(tr, Dm), _vec_spec(Dm)
    if has_res:
        ins, in_specs = (x, y, gate, g, sc, sh), [row, row, vec, vec, vec, vec]
        out_shape = (jax.ShapeDtypeStruct((T, Dm), F32), jax.ShapeDtypeStruct((T, Dm), BF16))
        out_specs = (row, row)
    else:
        ins, in_specs = (x, g, sc, sh), [row, vec, vec, vec]
        out_shape = jax.ShapeDtypeStruct((T, Dm), BF16)
        out_specs = row
    out = pl.pallas_call(body, name=name, out_shape=out_shape, grid=(T // tr,), in_specs=in_specs,
                         out_specs=out_specs, compiler_params=_params(("parallel",)))(*ins)
    return out if has_res else (x, out)


def _norm_bwd(x, dh, dxo, g, sc, y_prev, gate_prev, name):
    T, Dm = x.shape
    tr = _pick(T, (ROW_TILE,))
    has_prev = y_prev is not None

    def body(*refs):
        if has_prev:
            x_ref, dh_ref, dxo_ref, g_ref, sc_ref, yp_ref, gp_ref, dx_ref, dyp_ref, s1_ref, s2_ref, s3_ref = refs
        else:
            x_ref, dh_ref, dxo_ref, g_ref, sc_ref, dx_ref, s1_ref, s2_ref = refs

        @pl.when(pl.program_id(0) == 0)
        def _():
            s1_ref[...] = jnp.zeros_like(s1_ref)
            s2_ref[...] = jnp.zeros_like(s2_ref)
            if has_prev:
                s3_ref[...] = jnp.zeros_like(s3_ref)

        xv = x_ref[...]
        rstd = lax.rsqrt(jnp.mean(xv * xv, axis=-1, keepdims=True) + EPS)
        xhat = xv * rstd
        dh = dh_ref[...]
        dxhat = dh * (g_ref[...] * (1.0 + sc_ref[...]))
        dx = dxo_ref[...] + rstd * (dxhat - xhat * jnp.mean(dxhat * xhat, axis=-1, keepdims=True))
        dx_ref[...] = dx
        s1_ref[...] += jnp.sum(dh, axis=0, keepdims=True)
        s2_ref[...] += jnp.sum(dh * xhat, axis=0, keepdims=True)
        if has_prev:
            dyp_ref[...] = (gp_ref[...] * dx).astype(BF16)
            s3_ref[...] += jnp.sum(dx * yp_ref[...], axis=0, keepdims=True)

    row, vec = _row_spec(tr, Dm), _vec_spec(Dm)
    vshape = jax.ShapeDtypeStruct((1, Dm), F32)
    if has_prev:
        ins, in_specs = (x, dh, dxo, g, sc, y_prev, gate_prev), [row, row, row, vec, vec, row, vec]
        out_shape = (jax.ShapeDtypeStruct((T, Dm), F32), jax.ShapeDtypeStruct((T, Dm), BF16), vshape, vshape, vshape)
        out_specs = (row, row, vec, vec, vec)
    else:
        ins, in_specs = (x, dh, dxo, g, sc), [row, row, row, vec, vec]
        out_shape = (jax.ShapeDtypeStruct((T, Dm), F32), vshape, vshape)
        out_specs = (row, vec, vec)
    return pl.pallas_call(body, name=name, out_shape=out_shape, grid=(T // tr,), in_specs=in_specs,
                          out_specs=out_specs, compiler_params=_params(("arbitrary",)))(*ins)


def _loss_head(x, y, gate, fg, tgt, name):
    T, Dm = x.shape
    tr = _pick(T, (ROW_TILE,))
    nsteps = T // tr

    def body(x_ref, y_ref, gate_ref, fg_ref, t_ref, dx_ref, dy_ref, loss_ref, sfg_ref, sg_ref, acc_ref):
        i = pl.program_id(0)

        @pl.when(i == 0)
        def _():
            acc_ref[...] = jnp.zeros_like(acc_ref)
            sfg_ref[...] = jnp.zeros_like(sfg_ref)
            sg_ref[...] = jnp.zeros_like(sg_ref)

        yv = y_ref[...]
        xv = x_ref[...] + gate_ref[...] * yv
        rstd = lax.rsqrt(jnp.mean(xv * xv, axis=-1, keepdims=True) + EPS)
        xhat = xv * rstd
        err = xhat * fg_ref[...] - t_ref[...]
        acc_ref[...] += jnp.sum(err * err, axis=0, keepdims=True)
        dyn = err * (1.0 / Dm)
        sfg_ref[...] += jnp.sum(dyn * xhat, axis=0, keepdims=True)
        dxhat = dyn * fg_ref[...]
        dx = rstd * (dxhat - xhat * jnp.mean(dxhat * xhat, axis=-1, keepdims=True))
        dx_ref[...] = dx
        dy_ref[...] = (gate_ref[...] * dx).astype(BF16)
        sg_ref[...] += jnp.sum(dx * yv, axis=0, keepdims=True)

        @pl.when(i == nsteps - 1)
        def _():
            total = jnp.sum(acc_ref[...], axis=1, keepdims=True) * (0.5 / Dm)
            loss_ref[...] = jnp.broadcast_to(total, loss_ref.shape)

    row, vec = _row_spec(tr, Dm), _vec_spec(Dm)
    vshape = jax.ShapeDtypeStruct((1, Dm), F32)
    return pl.pallas_call(
        body, name=name, grid=(nsteps,),
        out_shape=(jax.ShapeDtypeStruct((T, Dm), F32), jax.ShapeDtypeStruct((T, Dm), BF16),
                   jax.ShapeDtypeStruct((1, LANES), F32), vshape, vshape),
        in_specs=[row, row, vec, vec, row], out_specs=(row, row, _vec_spec(LANES), vec, vec),
        scratch_shapes=[pltpu.VMEM((1, Dm), F32)], compiler_params=_params(("arbitrary",)),
    )(x, y, gate, fg, tgt)


def _spatial_mask():
    r = lax.broadcasted_iota(jnp.int32, (GM_BLOCK, GM_BLOCK), 0) // CHUNK
    c = lax.broadcasted_iota(jnp.int32, (GM_BLOCK, GM_BLOCK), 1) // CHUNK
    return r >= c


def _gm_specs(tr):
    return [_row_spec(tr, 2 * GM_WIDTH), _vec_spec(GM_WIDTH), _vec_spec(GM_WIDTH),
            pl.BlockSpec((GM_HEADS, GM_BLOCK, GM_BLOCK), lambda i: (0, 0, 0)),
            pl.BlockSpec((GM_HEADS, GM_BLOCK, 1), lambda i: (0, 0, 0))]


def _gm_mid_fwd(a, ln_g, ln_b, ws, bs3, name):
    T = a.shape[0]
    tr = _pick(T, (ROW_TILE,))
    W = GM_WIDTH

    def body(a_ref, lg_ref, lb_ref, ws_ref, bs_ref, o_ref, vn_scr):
        av = a_ref[:, W:]
        v = av * _cdf(av)
        vc = v - jnp.mean(v, axis=-1, keepdims=True)
        rstd = lax.rsqrt(jnp.mean(vc * vc, axis=-1, keepdims=True) + EPS)
        vn_scr[...] = (vc * rstd * lg_ref[...] + lb_ref[...]).astype(BF16)
        mask = _spatial_mask()
        for h in range(GM_HEADS):
            w = jnp.where(mask, ws_ref[h], 0.0).astype(BF16)
            cs = slice(h * GM_HEAD_DIM, (h + 1) * GM_HEAD_DIM)
            for blk in range(tr // GM_BLOCK):
                rs = slice(blk * GM_BLOCK, (blk + 1) * GM_BLOCK)
                s = _dot(w, vn_scr[rs, cs], NN) + bs_ref[h]
                au = a_ref[rs, cs]
                o_ref[rs, cs] = (au * _cdf(au) * s).astype(BF16)

    return pl.pallas_call(
        body, name=name, out_shape=jax.ShapeDtypeStruct((T, W), BF16), grid=(T // tr,),
        in_specs=_gm_specs(tr), out_specs=_row_spec(tr, W),
        scratch_shapes=[pltpu.VMEM((tr, W), BF16)], compiler_params=_params(("parallel",)),
    )(a, ln_g, ln_b, ws, bs3)


def _gm_mid_bwd(a, dgated, ln_g, ln_b, ws, bs3, name):
    T = a.shape[0]
    tr = _pick(T, (ROW_TILE,))
    W = GM_WIDTH
    nsteps = T // tr

    def body(a_ref, dg_ref, lg_ref, lb_ref, ws_ref, bs_ref, da_ref, dws_ref, dbs_ref, dlg_ref, dlb_ref,
             vn_scr, vhat_scr, dvn_scr, dsum_scr):
        i = pl.program_id(0)

        @pl.when(i == 0)
        def _():
            dws_ref[...] = jnp.zeros_like(dws_ref)
            dbs_ref[...] = jnp.zeros_like(dbs_ref)
            dlg_ref[...] = jnp.zeros_like(dlg_ref)
            dlb_ref[...] = jnp.zeros_like(dlb_ref)
            dsum_scr[...] = jnp.zeros_like(dsum_scr)

        av = a_ref[:, W:]
        cdf_v = _cdf(av)
        v = av * cdf_v
        vc = v - jnp.mean(v, axis=-1, keepdims=True)
        rstd = lax.rsqrt(jnp.mean(vc * vc, axis=-1, keepdims=True) + EPS)
        vhat_scr[...] = vc * rstd
        vn_scr[...] = (vhat_scr[...] * lg_ref[...] + lb_ref[...]).astype(BF16)
        mask = _spatial_mask()
        for h in range(GM_HEADS):
            w = jnp.where(mask, ws_ref[h], 0.0).astype(BF16)
            cs = slice(h * GM_HEAD_DIM, (h + 1) * GM_HEAD_DIM)
            for blk in range(tr // GM_BLOCK):
                rs = slice(blk * GM_BLOCK, (blk + 1) * GM_BLOCK)
                vnb = vn_scr[rs, cs]
                s = _dot(w, vnb, NN) + bs_ref[h]
                au = a_ref[rs, cs]
                cdf_u = _cdf(au)
                dg = dg_ref[rs, cs]
                ds = dg * (au * cdf_u)
                da_ref[rs, cs] = (dg * s * (cdf_u + au * _pdf(au))).astype(BF16)
                dsb = ds.astype(BF16)
                dvn_scr[rs, cs] = _dot(w, dsb, TN)
                dws_ref[h] += _dot(dsb, vnb, NT)
                dsum_scr[:, cs] += ds
        dvn = dvn_scr[...]
        vhat = vhat_scr[...]
        dlg_ref[...] += jnp.sum(dvn * vhat, axis=0, keepdims=True)
        dlb_ref[...] += jnp.sum(dvn, axis=0, keepdims=True)
        dvh = dvn * lg_ref[...]
        dv = rstd * (dvh - jnp.mean(dvh, axis=-1, keepdims=True)
                     - vhat * jnp.mean(dvh * vhat, axis=-1, keepdims=True))
        da_ref[:, W:] = (dv * (cdf_v + av * _pdf(av))).astype(BF16)

        @pl.when(i == nsteps - 1)
        def _():
            for h in range(GM_HEADS):
                dws_ref[h] = jnp.where(mask, dws_ref[h], 0.0)
            col_head = lax.broadcasted_iota(jnp.int32, (W, GM_BLOCK), 0) // GM_HEAD_DIM
            sel = (col_head == lax.broadcasted_iota(jnp.int32, (W, GM_BLOCK), 1)).astype(F32)
            dbs_ref[...] = _dot(dsum_scr[...], sel, NN, HI)

    vshape = jax.ShapeDtypeStruct((1, W), F32)
    return pl.pallas_call(
        body, name=name, grid=(nsteps,),
        out_shape=(jax.ShapeDtypeStruct((T, 2 * W), BF16), jax.ShapeDtypeStruct((GM_HEADS, GM_BLOCK, GM_BLOCK), F32),
                   jax.ShapeDtypeStruct((GM_BLOCK, GM_BLOCK), F32), vshape, vshape),
        in_specs=[_gm_specs(tr)[0], _row_spec(tr, W)] + _gm_specs(tr)[1:],
        out_specs=(_row_spec(tr, 2 * W), pl.BlockSpec((GM_HEADS, GM_BLOCK, GM_BLOCK), lambda i: (0, 0, 0)),
                   pl.BlockSpec((GM_BLOCK, GM_BLOCK), lambda i: (0, 0)), _vec_spec(W), _vec_spec(W)),
        scratch_shapes=[pltpu.VMEM((tr, W), BF16), pltpu.VMEM((tr, W), F32), pltpu.VMEM((tr, W), F32),
                        pltpu.VMEM((GM_BLOCK, W), F32)],
        compiler_params=_params(("arbitrary",)),
    )(a, dgated, ln_g, ln_b, ws, bs3)


SUB = 16
EXP_CLAMP = 80.0


def _tri(lower):
    r = lax.broadcasted_iota(jnp.int32, (CHUNK, CHUNK), 0)
    c = lax.broadcasted_iota(jnp.int32, (CHUNK, CHUNK), 1)
    return (r >= c) if lower else (c >= r)


def _score_masks():
    i = lax.broadcasted_iota(jnp.int32, (CHUNK, CHUNK), 0)
    j = lax.broadcasted_iota(jnp.int32, (CHUNK, CHUNK), 1)
    bi, bj = i // SUB, j // SUB
    diag = (bi == bj) & (i >= j)
    pair = (bi % 2 == 1) & (bj == bi - 1)
    half = (i >= CHUNK // 2) & (j < CHUNK // 2)
    return diag, pair, half


def _ref_selector():
    r = lax.broadcasted_iota(jnp.int32, (4 * CHUNK, CHUNK), 0)
    c = lax.broadcasted_iota(jnp.int32, (4 * CHUNK, CHUNK), 1)
    seg, i = r // CHUNK, r % CHUNK
    blk = (i // SUB) * SUB
    tgt = jnp.where(seg == 0, blk + SUB // 2 - 1,
                    jnp.where(seg == 1, blk - 1, jnp.where(seg == 2, blk + SUB - 1, CHUNK // 2 - 1)))
    return (c == tgt).astype(F32)


def _hg_gates(p_ref, lb_ref, h, sel):
    Dm = D_MODEL
    c0 = h * HG_DIM
    qr = p_ref[:, c0:c0 + HG_DIM]
    fz = p_ref[:, Dm + c0:Dm + c0 + HG_DIM]
    v = p_ref[:, 2 * Dm + c0:2 * Dm + c0 + HG_DIM]
    gt = p_ref[:, 3 * Dm + c0:3 * Dm + c0 + HG_DIM]
    lbh = lb_ref[:, c0:c0 + HG_DIM]
    sg = _sig(fz)
    f = lbh + (1.0 - lbh) * sg
    gl = jnp.log(f)
    kk = 1.0 - f
    sq = _sig(qr)
    q = qr * sq
    b = _dot(_tri(True).astype(F32), gl, NN, HI)
    refs = _dot(sel, b, NN, HI)
    r_mid, r_prev, r_end, r_half = (refs[s * CHUNK:(s + 1) * CHUNK] for s in range(4))
    bc = jnp.sum(gl, axis=0, keepdims=True)
    eqs = (jnp.exp(jnp.clip(b - r_mid, -EXP_CLAMP, EXP_CLAMP)), jnp.exp(jnp.minimum(b - r_prev, 0.0)),
           jnp.exp(jnp.minimum(b - r_half, 0.0)))
    eks = (jnp.exp(jnp.clip(r_mid - b, -EXP_CLAMP, EXP_CLAMP)), jnp.exp(jnp.minimum(r_end - b, 0.0)),
           jnp.exp(jnp.minimum(r_half - b, 0.0)))
    eb = jnp.exp(b)
    ec = jnp.exp(bc - b)
    return dict(qr=qr, v=v, gt=gt, lbh=lbh, sg=sg, f=f, kk=kk, sq=sq, q=q, eqs=eqs, eks=eks, eb=eb, ec=ec,
                e_end=jnp.exp(bc), qs=[q * e for e in eqs], ks=[kk * e for e in eks], qe=q * eb, ke=kk * ec)


def _scores(g, masks):
    a = None
    for qs, ks, m in zip(g["qs"], g["ks"], masks):
        part = jnp.where(m, _dot(qs, ks, NT, HI), 0.0)
        a = part if a is None else a + part
    return a


def _hg_scan_fwd(p, lb, gn, name):
    T = p.shape[0]
    nc = T // CHUNK
    Dm = D_MODEL

    def body(p_ref, lb_ref, gn_ref, o_ref, og_ref, so_ref, st_ref):
        @pl.when(pl.program_id(0) == 0)
        def _():
            st_ref[...] = jnp.zeros_like(st_ref)

        masks = _score_masks()
        sel = _ref_selector()
        for h in range(HG_HEADS):
            g = _hg_gates(p_ref, lb_ref, h, sel)
            cs = slice(h * HG_DIM, (h + 1) * HG_DIM)
            a = _scores(g, masks)
            st = st_ref[h]
            so_ref[0, h] = st
            o = _dot(a, g["v"], NN, HI) + _dot(g["qe"], st, NT, HI)
            st_ref[h] = st * g["e_end"] + _dot(g["v"], g["ke"], TN, HI)
            o_ref[:, cs] = o
            r = lax.rsqrt(jnp.mean(o * o, axis=-1, keepdims=True) + EPS)
            gt = g["gt"]
            og_ref[:, cs] = (((o * r) * gn_ref[:, cs]).astype(F32) * (gt * _sig(gt))).astype(BF16)

    return pl.pallas_call(
        body, name=name, grid=(nc,),
        out_shape=(jax.ShapeDtypeStruct((T, Dm), F32), jax.ShapeDtypeStruct((T, Dm), BF16),
                   jax.ShapeDtypeStruct((nc, HG_HEADS, HG_DIM, HG_DIM), F32)),
        in_specs=[_row_spec(CHUNK, 4 * Dm), _vec_spec(Dm), _vec_spec(Dm)],
        out_specs=(_row_spec(CHUNK, Dm), _row_spec(CHUNK, Dm),
                   pl.BlockSpec((1, HG_HEADS, HG_DIM, HG_DIM), lambda i: (i, 0, 0, 0))),
        scratch_shapes=[pltpu.VMEM((HG_HEADS, HG_DIM, HG_DIM), F32)],
        compiler_params=_params(("arbitrary",)),
    )(p, lb, gn)


def _hg_scan_bwd(p, lb, gn, o, dog, states, name):
    T = p.shape[0]
    nc = T // CHUNK
    Dm = D_MODEL

    def rev(i):
        return nc - 1 - i

    def body(p_ref, lb_ref, gn_ref, o_ref, dog_ref, st_in_ref, dp_ref, dlb_ref, dgn_ref, dst_ref, carry_ref):
        @pl.when(pl.program_id(0) == 0)
        def _():
            dst_ref[...] = jnp.zeros_like(dst_ref)
            carry_ref[...] = jnp.zeros_like(carry_ref)
            dlb_ref[...] = jnp.zeros_like(dlb_ref)
            dgn_ref[...] = jnp.zeros_like(dgn_ref)

        upper = _tri(False).astype(F32)
        masks = _score_masks()
        sel = _ref_selector()
        for h in range(HG_HEADS):
            g = _hg_gates(p_ref, lb_ref, h, sel)
            c0 = h * HG_DIM
            cs = slice(c0, c0 + HG_DIM)
            oh = o_ref[:, cs]
            r = lax.rsqrt(jnp.mean(oh * oh, axis=-1, keepdims=True) + EPS)
            on = oh * r
            gt = g["gt"]
            sgt = _sig(gt)
            sil = gt * sgt
            dogh = dog_ref[:, cs]
            gnh = gn_ref[:, cs]
            don = dogh * gnh * sil
            dgn_ref[:, cs] += jnp.sum(dogh * on * sil, axis=0, keepdims=True)
            dgate = dogh * on * gnh * (sgt * (1.0 + gt * (1.0 - sgt)))
            do = r * (don - on * jnp.mean(don * on, axis=-1, keepdims=True))
            a = _scores(g, masks)
            dst = dst_ref[h]
            dob = do.astype(BF16)
            da = _dot(do, g["v"], NT, HI)
            dv = _dot(a.astype(BF16), dob, TN) + _dot(g["ke"].astype(BF16), dst.astype(BF16), NT)
            dq = _dot(do, st_in_ref[0, h], NN, HI) * g["eb"]
            dk = _dot(g["v"], dst, NN, HI) * g["ec"]
            for qs, ks, eq, ek, m in zip(g["qs"], g["ks"], g["eqs"], g["eks"], masks):
                dam = jnp.where(m, da, 0.0)
                dq = dq + _dot(dam, ks, NN, HI) * eq
                dk = dk + _dot(dam, qs, TN, HI) * ek
            dst_ref[h] = dst * g["e_end"] + _dot(do, g["qe"], TN, HI)
            dgd = g["q"] * dq - g["kk"] * dk
            dgl = _dot(upper, dgd, NN, HI) + carry_ref[h]
            carry_ref[h] += jnp.sum(dgd, axis=0, keepdims=True)
            df = dgl / g["f"] - dk
            sg = g["sg"]
            dlb_ref[:, cs] += jnp.sum(df * (1.0 - sg), axis=0, keepdims=True)
            sq, qr = g["sq"], g["qr"]
            dp_ref[:, c0:c0 + HG_DIM] = (dq * (sq * (1.0 + qr * (1.0 - sq)))).astype(BF16)
            dp_ref[:, Dm + c0:Dm + c0 + HG_DIM] = (df * (1.0 - g["lbh"]) * sg * (1.0 - sg)).astype(BF16)
            dp_ref[:, 2 * Dm + c0:2 * Dm + c0 + HG_DIM] = dv.astype(BF16)
            dp_ref[:, 3 * Dm + c0:3 * Dm + c0 + HG_DIM] = dgate.astype(BF16)

    vshape = jax.ShapeDtypeStruct((1, Dm), F32)
    rrow = lambda w: pl.BlockSpec((CHUNK, w), lambda i: (rev(i), 0))
    return pl.pallas_call(
        body, name=name, grid=(nc,),
        out_shape=(jax.ShapeDtypeStruct((T, 4 * Dm), BF16), vshape, vshape),
        in_specs=[rrow(4 * Dm), _vec_spec(Dm), _vec_spec(Dm), rrow(Dm), rrow(Dm),
                  pl.BlockSpec((1, HG_HEADS, HG_DIM, HG_DIM), lambda i: (rev(i), 0, 0, 0))],
        out_specs=(rrow(4 * Dm), _vec_spec(Dm), _vec_spec(Dm)),
        scratch_shapes=[pltpu.VMEM((HG_HEADS, HG_DIM, HG_DIM), F32), pltpu.VMEM((HG_HEADS, 1, HG_DIM), F32)],
        compiler_params=_params(("arbitrary",)),
    )(p, lb, gn, o, dog, states)


def _lb_fwd(hg_lb, name):
    def body(a_ref, o_ref):
        a0, a1 = a_ref[0:1], a_ref[1:2]
        m = jnp.maximum(a0, a1)
        e0, e1 = jnp.exp(a0 - m), jnp.exp(a1 - m)
        p0, p1 = e0 / (e0 + e1), e1 / (e0 + e1)
        o_ref[0:1] = p0 - p0
        o_ref[1:2] = (p0 + p1) - p0

    return pl.pallas_call(body, name=name, out_shape=jax.ShapeDtypeStruct(hg_lb.shape, F32))(hg_lb)


def _lb_bwd(hg_lb, dlb_all, name):
    def body(a_ref, d_ref, o_ref):
        a0, a1 = a_ref[0:1], a_ref[1:2]
        m = jnp.maximum(a0, a1)
        e0, e1 = jnp.exp(a0 - m), jnp.exp(a1 - m)
        p0, p1 = e0 / (e0 + e1), e1 / (e0 + e1)
        d1 = d_ref[1:2]
        o_ref[0:1] = -p0 * p1 * d1
        o_ref[1:2] = p1 * (1.0 - p1) * d1

    return pl.pallas_call(body, name=name, out_shape=jax.ShapeDtypeStruct(hg_lb.shape, F32))(hg_lb, dlb_all)


CONV_COLS = 256


def _conv_fwd(a, w, b, name):
    T = a.shape[0]
    Fh = FFN_HIDDEN
    tr = _pick(T, (ROW_TILE,))
    cw = CONV_COLS
    hb = tr // 8

    def body(a_ref, ap_ref, w_ref, b_ref, m_ref):
        m0 = (pl.program_id(0) > 0).astype(F32)

        def conv(cc):
            x = jnp.concatenate([ap_ref[:, pl.ds(cc, cw)] * m0, a_ref[:, pl.ds(cc, cw)]], axis=0)
            wv = w_ref[:, pl.ds(cc, cw)]
            y = b_ref[:, pl.ds(cc, cw)] + wv[2:3] * x + wv[1:2] * pltpu.roll(x, 1, axis=0) \
                + wv[0:1] * pltpu.roll(x, 2, axis=0)
            return y[8:]

        def step(c, carry):
            c0 = pl.multiple_of(c * cw, cw)
            c1 = pl.multiple_of(Fh + c * cw, cw)
            yg, yv = conv(c0), conv(c1)
            m_ref[:, pl.ds(c0, cw)] = (yg * _cdf(yg) * yv).astype(BF16)
            return carry

        lax.fori_loop(0, Fh // cw, step, 0)

    return pl.pallas_call(
        body, name=name, out_shape=jax.ShapeDtypeStruct((T, Fh), BF16), grid=(T // tr,),
        in_specs=[_row_spec(tr, 2 * Fh), pl.BlockSpec((8, 2 * Fh), lambda i: (jnp.maximum(i * hb - 1, 0), 0)),
                  _vec_spec(2 * Fh, 3), _vec_spec(2 * Fh)],
        out_specs=_row_spec(tr, Fh), compiler_params=_params(("parallel",)),
    )(a, a, w, b)


def _conv_bwd(a, dm, w, b, name):
    T = a.shape[0]
    Fh = FFN_HIDDEN
    tr = _pick(T, (ROW_TILE,))
    cw = CONV_COLS
    hb = tr // 8
    nsteps = T // tr
    n = tr + 8

    def body(a_ref, ap_ref, an_ref, dm_ref, dmn_ref, w_ref, b_ref, da_ref, dw_ref, db_ref):
        i = pl.program_id(0)
        m0 = (i > 0).astype(F32)
        m1 = (i < nsteps - 1).astype(F32)

        @pl.when(i == 0)
        def _():
            dw_ref[...] = jnp.zeros_like(dw_ref)
            db_ref[...] = jnp.zeros_like(db_ref)

        def prep(cc):
            x = jnp.concatenate([ap_ref[:, pl.ds(cc, cw)] * m0, a_ref[:, pl.ds(cc, cw)],
                                 an_ref[:, pl.ds(cc, cw)] * m1], axis=0)
            wv = w_ref[:, pl.ds(cc, cw)]
            s1 = pltpu.roll(x, 1, axis=0)
            s2 = pltpu.roll(x, 2, axis=0)
            y = b_ref[:, pl.ds(cc, cw)] + wv[2:3] * x + wv[1:2] * s1 + wv[0:1] * s2
            return wv, x[8:], s1[8:], s2[8:], y[8:]

        def back(cc, dy, wv, x0, s1, s2):
            da = wv[2:3] * dy + wv[1:2] * pltpu.roll(dy, n - 1, axis=0) + wv[0:1] * pltpu.roll(dy, n - 2, axis=0)
            da_ref[:, pl.ds(cc, cw)] = da[:tr].astype(BF16)
            d = dy[:tr]
            db_ref[:, pl.ds(cc, cw)] += jnp.sum(d, axis=0, keepdims=True)
            dw_ref[2:3, pl.ds(cc, cw)] += jnp.sum(d * x0[:tr], axis=0, keepdims=True)
            dw_ref[1:2, pl.ds(cc, cw)] += jnp.sum(d * s1[:tr], axis=0, keepdims=True)
            dw_ref[0:1, pl.ds(cc, cw)] += jnp.sum(d * s2[:tr], axis=0, keepdims=True)

        def step(c, carry):
            c0 = pl.multiple_of(c * cw, cw)
            c1 = pl.multiple_of(Fh + c * cw, cw)
            dmx = jnp.concatenate([dm_ref[:, pl.ds(c0, cw)], dmn_ref[:, pl.ds(c0, cw)] * m1], axis=0)
            wg, xg, s1g, s2g, yg = prep(c0)
            wv, xv, s1v, s2v, yv = prep(c1)
            cg = _cdf(yg)
            back(c0, dmx * yv * (cg + yg * _pdf(yg)), wg, xg, s1g, s2g)
            back(c1, dmx * (yg * cg), wv, xv, s1v, s2v)
            return carry

        lax.fori_loop(0, Fh // cw, step, 0)

    prev = lambda wd: pl.BlockSpec((8, wd), lambda i: (jnp.maximum(i * hb - 1, 0), 0))
    nxt = lambda wd: pl.BlockSpec((8, wd), lambda i: (jnp.minimum((i + 1) * hb, T // 8 - 1), 0))
    return pl.pallas_call(
        body, name=name, grid=(nsteps,),
        out_shape=(jax.ShapeDtypeStruct((T, 2 * Fh), BF16), jax.ShapeDtypeStruct((3, 2 * Fh), F32),
                   jax.ShapeDtypeStruct((1, 2 * Fh), F32)),
        in_specs=[_row_spec(tr, 2 * Fh), prev(2 * Fh), nxt(2 * Fh), _row_spec(tr, Fh), nxt(Fh),
                  _vec_spec(2 * Fh, 3), _vec_spec(2 * Fh)],
        out_specs=(_row_spec(tr, 2 * Fh), _vec_spec(2 * Fh, 3), _vec_spec(2 * Fh)),
        compiler_params=_params(("arbitrary",)),
    )(a, a, a, dm, dm, w, b)


def _ada_fwd(c_all, ada_w, ada_b, name):
    L, Dm, cols = ada_w.shape
    tn = _pick(cols, (512, 256, 128))

    def body(c_ref, w_ref, b_ref, o_ref):
        cv = c_ref[...]
        cond = (cv * _sig(cv)).astype(BF16)
        o_ref[...] = _dot(cond, w_ref[...].astype(BF16), NN) + b_ref[...]

    return pl.pallas_call(
        body, name=name, out_shape=jax.ShapeDtypeStruct((L, N_DEV, cols), F32), grid=(L, cols // tn),
        in_specs=[pl.BlockSpec((N_DEV, Dm), lambda l, j: (0, 0)), pl.BlockSpec((None, Dm, tn), lambda l, j: (l, 0, j)),
                  pl.BlockSpec((None, 1, tn), lambda l, j: (l, 0, j))],
        out_specs=pl.BlockSpec((None, N_DEV, tn), lambda l, j: (l, 0, j)),
        compiler_params=_params(("parallel", "parallel")),
    )(c_all, ada_w, ada_b.reshape(L, 1, cols))


def _ada_bwd(c_all, dmod, name):
    L, _, cols = dmod.shape
    Dm = c_all.shape[1]
    tn = _pick(cols, (512, 256, 128))

    def body(c_ref, d_ref, o_ref):
        cv = c_ref[...]
        o_ref[...] = _dot(cv * _sig(cv), d_ref[...], TN, HI)

    return pl.pallas_call(
        body, name=name, out_shape=jax.ShapeDtypeStruct((L, Dm, cols), F32), grid=(L, cols // tn),
        in_specs=[pl.BlockSpec((N_DEV, Dm), lambda l, j: (0, 0)), pl.BlockSpec((None, N_DEV, tn), lambda l, j: (l, 0, j))],
        out_specs=pl.BlockSpec((None, Dm, tn), lambda l, j: (l, 0, j)),
        compiler_params=_params(("parallel", "parallel")),
    )(c_all, dmod)


def _add_own_half(g4, rb, core, name):
    S, _, rh, cw = g4.shape
    tr = _pick(rh, (256, 128, 176, 64))

    def body(core_ref, g_ref, r_ref, o_ref):
        o_ref[...] = (g_ref[...] + r_ref[...].astype(F32)).astype(BF16)

    return pl.pallas_call(
        body, name=name, out_shape=jax.ShapeDtypeStruct((S, rh, cw), BF16),
        grid_spec=pltpu.PrefetchScalarGridSpec(
            num_scalar_prefetch=1, grid=(S, rh // tr),
            in_specs=[pl.BlockSpec((None, None, tr, cw), lambda s, i, core_ref: (s, core_ref[0], i, 0)),
                      pl.BlockSpec((None, tr, cw), lambda s, i, core_ref: (s, i, 0))],
            out_specs=pl.BlockSpec((None, tr, cw), lambda s, i, core_ref: (s, i, 0))),
        compiler_params=_params(("parallel", "parallel")),
    )(core, g4, rb)


def _sum_chips(rc, name):
    S, rh, cw = rc.shape
    tr = _pick(rh, (256, 128, 176, 64))

    def body(r_ref, o_ref):
        acc = r_ref[0].astype(F32)
        for s in range(1, S):
            acc = acc + r_ref[s].astype(F32)
        o_ref[...] = acc

    return pl.pallas_call(
        body, name=name, out_shape=jax.ShapeDtypeStruct((rh, cw), F32), grid=(rh // tr,),
        in_specs=[pl.BlockSpec((S, tr, cw), lambda i: (0, i, 0))], out_specs=pl.BlockSpec((tr, cw), lambda i: (i, 0)),
        compiler_params=_params(("parallel",)),
    )(rc)


def _sum_devices(gathered, name):
    n, R, _ = gathered.shape
    tr = _pick(R, (512, 256, 128, 64, 32, 16, 8))

    def body(g_ref, o_ref):
        acc = g_ref[0]
        for d in range(1, n):
            acc = acc + g_ref[d]
        o_ref[...] = acc

    return pl.pallas_call(
        body, name=name, out_shape=jax.ShapeDtypeStruct((R, LANES), F32), grid=(R // tr,),
        in_specs=[pl.BlockSpec((n, tr, LANES), lambda i: (0, i, 0))], out_specs=pl.BlockSpec((tr, LANES), lambda i: (i, 0)),
        compiler_params=_params(("parallel",)),
    )(gathered)


def _adamw(w, g, m, v, name):
    R, C = w.shape
    tr = _pick(R, (256, 128, 64, 32, 16, 8))
    c1 = 1.0 / (1.0 - ADAM_B1 ** ADAM_STEP)
    c2 = 1.0 / (1.0 - ADAM_B2 ** ADAM_STEP)

    def body(w_ref, g_ref, m_ref, v_ref, d_ref, mo_ref, vo_ref):
        gv = g_ref[...]
        m2 = ADAM_B1 * m_ref[...] + (1.0 - ADAM_B1) * gv
        v2 = ADAM_B2 * v_ref[...] + (1.0 - ADAM_B2) * (gv * gv)
        mo_ref[...] = m2
        vo_ref[...] = v2
        d_ref[...] = -ADAM_LR * ((m2 * c1) / (jnp.sqrt(v2 * c2) + ADAM_EPS) + ADAM_WD * w_ref[...])

    spec = pl.BlockSpec((tr, C), lambda i: (i, 0))
    shp = jax.ShapeDtypeStruct((R, C), F32)
    return pl.pallas_call(body, name=name, out_shape=(shp, shp, shp), grid=(R // tr,), in_specs=[spec] * 4,
                          out_specs=(spec, spec, spec), compiler_params=_params(("parallel",)))(w, g, m, v)


ANY = pl.BlockSpec(memory_space=pl.ANY)


def _position():
    x, y, c = lax.axis_index("x"), lax.axis_index("y"), lax.axis_index("c")
    return x, y, c


def _allgather(ins, out_shapes, src_fns, dst_fns, name, in_vmem):
    n = len(ins)

    def body(*refs):
        in_refs, out_refs = refs[:n], refs[n:2 * n]
        send_sems, recv_sems, local_sems = refs[2 * n:]
        x, y, c = _position()
        me, sibling = (x, y, c), (x, y, 1 - c)
        chips = [(1 - x, y), (x, 1 - y), (1 - x, 1 - y)]

        def copy(k, j, block, to, own=False):
            dst = dst_fns[k](out_refs[k], *block)
            return pltpu.make_async_remote_copy(
                src_ref=src_fns[k](in_refs[k], c) if own else dst, dst_ref=dst,
                send_sem=send_sems.at[k, j], recv_sem=recv_sems.at[k, j], device_id=to, device_id_type=MESH)

        mine = [pltpu.make_async_copy(src_fns[k](in_refs[k], c), dst_fns[k](out_refs[k], *me), local_sems.at[k])
                for k in range(n)]
        for cp in mine:
            cp.start()
        first = []
        for k in range(n):
            first.append(copy(k, 0, me, sibling, own=True))
            first += [copy(k, 1 + j, me, (*chip, c), own=True) for j, chip in enumerate(chips)]
        for cp in first:
            cp.start()
        passed = []
        for j, chip in enumerate(chips):
            for k in range(n):
                copy(k, 1 + j, (*chip, c), me).wait_recv()
                fwd = copy(k, 4 + j, (*chip, c), sibling)
                fwd.start()
                passed.append(fwd)
        for k in range(n):
            copy(k, 0, sibling, me).wait_recv()
        for j, chip in enumerate(chips):
            for k in range(n):
                copy(k, 4 + j, (*chip, 1 - c), me).wait_recv()
        for cp in first + passed:
            cp.wait_send()
        for cp in mine:
            cp.wait()

    spec = pl.BlockSpec(memory_space=pltpu.VMEM) if in_vmem else ANY
    return pl.pallas_call(
        body, name=name, out_shape=tuple(out_shapes), in_specs=[spec] * n, out_specs=tuple([spec] * n),
        scratch_shapes=[pltpu.SemaphoreType.DMA((n, 7)), pltpu.SemaphoreType.DMA((n, 7)),
                        pltpu.SemaphoreType.DMA((n,))],
        compiler_params=pltpu.CompilerParams(vmem_limit_bytes=VMEM_LIMIT_BYTES),
    )(*ins)


def _allgather_small(payload, name):
    R = payload.shape[0]
    (out,) = _allgather(
        [payload], [jax.ShapeDtypeStruct((N_DEV, R, LANES), F32)],
        [lambda ref, c: ref], [lambda ref, px, py, pc: ref.at[4 * px + 2 * py + pc]], name, in_vmem=True)
    return out


def _gather_weights(shards, col_sharded, name):
    src_fns, dst_fns, out_shapes = [], [], []
    for sh, col in zip(shards, col_sharded):
        L, r, cw = sh.shape
        rh = r // 2
        src_fns.append(lambda ref, c, rh=rh: ref.at[:, pl.ds(c * rh, rh), :])
        if col:
            out_shapes.append(jax.ShapeDtypeStruct((L, r, N_CHIPS * cw), sh.dtype))
            dst_fns.append(lambda ref, px, py, pc, rh=rh, cw=cw:
                           ref.at[:, pl.ds(pc * rh, rh), pl.ds((2 * px + py) * cw, cw)])
        else:
            out_shapes.append(jax.ShapeDtypeStruct((L, N_CHIPS * r, cw), sh.dtype))
            dst_fns.append(lambda ref, px, py, pc, rh=rh, r=r:
                           ref.at[:, pl.ds((2 * px + py) * r + pc * rh, rh), :])
    return _allgather(list(shards), out_shapes, src_fns, dst_fns, name, in_vmem=False)


def _swap_sibling(ins, name):
    n = len(ins)

    def body(*refs):
        in_refs, out_refs = refs[:n], refs[n:2 * n]
        send_sems, recv_sems = refs[2 * n:]
        x, y, c = _position()
        copies = [pltpu.make_async_remote_copy(
            src_ref=in_refs[k], dst_ref=out_refs[k], send_sem=send_sems.at[k], recv_sem=recv_sems.at[k],
            device_id=(x, y, 1 - c), device_id_type=MESH) for k in range(n)]
        for cp in copies:
            cp.start()
        for cp in copies:
            cp.wait_recv()
        for cp in copies:
            cp.wait_send()

    return pl.pallas_call(
        body, name=name, out_shape=tuple(jax.ShapeDtypeStruct(a.shape, a.dtype) for a in ins),
        in_specs=[ANY] * n, out_specs=tuple([ANY] * n),
        scratch_shapes=[pltpu.SemaphoreType.DMA((n,)), pltpu.SemaphoreType.DMA((n,))],
        compiler_params=pltpu.CompilerParams(vmem_limit_bytes=VMEM_LIMIT_BYTES),
    )(*ins)


def _exchange_chips(ins, name):
    n = len(ins)

    def body(*refs):
        in_refs, out_refs = refs[:n], refs[n:2 * n]
        send_sems, recv_sems, local_sems = refs[2 * n:]
        x, y, c = _position()
        my_chip = 2 * x + y
        chips = [(1 - x, y), (x, 1 - y), (1 - x, 1 - y)]
        local = [pltpu.make_async_copy(in_refs[k].at[my_chip], out_refs[k].at[my_chip], local_sems.at[k])
                 for k in range(n)]
        for cp in local:
            cp.start()
        sends = []
        for j, (px, py) in enumerate(chips):
            for k in range(n):
                sends.append(pltpu.make_async_remote_copy(
                    src_ref=in_refs[k].at[2 * px + py], dst_ref=out_refs[k].at[my_chip],
                    send_sem=send_sems.at[k, j], recv_sem=recv_sems.at[k, j],
                    device_id=(px, py, c), device_id_type=MESH))
        for cp in sends:
            cp.start()
        for j, (px, py) in enumerate(chips):
            for k in range(n):
                pltpu.make_async_remote_copy(
                    src_ref=in_refs[k].at[2 * px + py], dst_ref=out_refs[k].at[2 * px + py],
                    send_sem=send_sems.at[k, j], recv_sem=recv_sems.at[k, j],
                    device_id=(px, py, c), device_id_type=MESH).wait_recv()
        for cp in sends:
            cp.wait_send()
        for cp in local:
            cp.wait()

    return pl.pallas_call(
        body, name=name, out_shape=tuple(jax.ShapeDtypeStruct(a.shape, a.dtype) for a in ins),
        in_specs=[ANY] * n, out_specs=tuple([ANY] * n),
        scratch_shapes=[pltpu.SemaphoreType.DMA((n, 3)), pltpu.SemaphoreType.DMA((n, 3)),
                        pltpu.SemaphoreType.DMA((n,))],
        compiler_params=pltpu.CompilerParams(vmem_limit_bytes=VMEM_LIMIT_BYTES),
    )(*ins)


def _join_halves(halves, groups, name):
    n = len(halves)
    where = {}
    out_shapes = []
    for gi, grp in enumerate(groups):
        rh, cw = halves[grp[0]].shape
        out_shapes.append(jax.ShapeDtypeStruct((len(grp), 2, rh, cw), F32))
        for li, k in enumerate(grp):
            where[k] = (gi, li)
    ng = len(groups)

    def body(*refs):
        in_refs, out_refs = refs[:n], refs[n:n + ng]
        send_sems, recv_sems, local_sems = refs[n + ng:]
        x, y, c = _position()
        local, sends = [], []
        for k in range(n):
            gi, li = where[k]
            local.append(pltpu.make_async_copy(in_refs[k], out_refs[gi].at[li, c], local_sems.at[k]))
            sends.append(pltpu.make_async_remote_copy(
                src_ref=in_refs[k], dst_ref=out_refs[gi].at[li, c], send_sem=send_sems.at[k],
                recv_sem=recv_sems.at[k], device_id=(x, y, 1 - c), device_id_type=MESH))
        for cp in local + sends:
            cp.start()
        for k in range(n):
            gi, li = where[k]
            pltpu.make_async_remote_copy(
                src_ref=in_refs[k], dst_ref=out_refs[gi].at[li, 1 - c], send_sem=send_sems.at[k],
                recv_sem=recv_sems.at[k], device_id=(x, y, 1 - c), device_id_type=MESH).wait_recv()
        for cp in sends:
            cp.wait_send()
        for cp in local:
            cp.wait()

    return pl.pallas_call(
        body, name=name, out_shape=tuple(out_shapes), in_specs=[ANY] * n, out_specs=tuple([ANY] * ng),
        scratch_shapes=[pltpu.SemaphoreType.DMA((n,)), pltpu.SemaphoreType.DMA((n,)), pltpu.SemaphoreType.DMA((n,))],
        compiler_params=pltpu.CompilerParams(vmem_limit_bytes=VMEM_LIMIT_BYTES),
    )(*halves)


def _vec(a):
    return a.reshape(1, -1)


def _local_step(x, tgt, mod, W, P):
    Dm = D_MODEL
    G = {k: [] for k in ("gm_w_in", "gm_w_out", "hg_w_in", "hg_w_out", "ffn_w_up", "ffn_w_down")}
    lb_all = _lb_fwd(P["hg_lb"], "lb_fwd")
    saved = []
    xs = x
    y_prev = gate_prev = None
    for i in range(DEPTH):
        m = [_vec(mod[i, j * Dm:(j + 1) * Dm]) for j in range(6)]
        sh1, sc1, g1, sh2, sc2, g2 = m
        j = i // 2
        xs, h = _norm_fwd(xs, y_prev, gate_prev, _vec(P["norm_g"][i, 0]), sc1, sh1, f"norm_fwd_a{i}")
        rec = dict(x1=xs, h1=h)
        if i % 2 == 0:
            a = _mm(h, W["gm_w_in"], "nn", f"gm_in{i}", b_layer=j)
            gated = _gm_mid_fwd(a, _vec(P["gm_ln_g"][j]), _vec(P["gm_ln_b"][j]), P["gm_w_s"][j],
                                P["gm_b_s"][j].reshape(GM_HEADS, GM_BLOCK, 1), f"gm_mid_fwd{i}")
            y1 = _mm(gated, W["gm_w_out"], "nn", f"gm_out{i}", b_layer=j)
            rec.update(a=a, act=gated)
        else:
            p = _mm(h, W["hg_w_in"], "nn", f"hg_in{i}", b_layer=j)
            o, og, states = _hg_scan_fwd(p, _vec(lb_all[j]), _vec(P["hg_gn_g"][j]), f"hg_scan_fwd{i}")
            y1 = _mm(og, W["hg_w_out"], "nn", f"hg_out{i}", b_layer=j)
            rec.update(a=p, act=og, o=o, states=states)
        rec["y1"] = y1
        xs, h2 = _norm_fwd(xs, y1, g1, _vec(P["norm_g"][i, 1]), sc2, sh2, f"norm_fwd_b{i}")
        a2 = _mm(h2, W["ffn_w_up"], "nn", f"ffn_up{i}", b_layer=i)
        mm_ = _conv_fwd(a2, P["ffn_conv_w"][i], _vec(P["ffn_conv_b"][i]), f"conv_fwd{i}")
        y2 = _mm(mm_, W["ffn_w_down"], "nn", f"ffn_down{i}", b_layer=i)
        rec.update(x2=xs, h2=h2, a2=a2, m=mm_, y2=y2, mods=m)
        saved.append(rec)
        y_prev, gate_prev = y2, g2
    dx, dy, loss, s_fg, s_gate = _loss_head(xs, y_prev, gate_prev, _vec(P["final_g"]), tgt, "loss_head")
    small = dict(final_g=s_fg, norm_g=[None] * DEPTH, dmod=[None] * DEPTH, ffn_conv_w=[None] * DEPTH,
                 ffn_conv_b=[None] * DEPTH, gm_ln_g=[None] * 2, gm_ln_b=[None] * 2, gm_w_s=[None] * 2,
                 gm_b_s=[None] * 2, hg_gn_g=[None] * 2, dlb=[None] * 2)
    for i in reversed(range(DEPTH)):
        rec = saved[i]
        sh1, sc1, g1, sh2, sc2, g2 = rec["mods"]
        j = i // 2
        d_g2 = s_gate
        dm = _mm(dy, W["ffn_w_down"], "nt", f"ffn_down_dx{i}", b_layer=i)
        G["ffn_w_down"].append(_mm(rec["m"], dy, "tn", f"ffn_down_dw{i}"))
        da2, dcw, dcb = _conv_bwd(rec["a2"], dm, P["ffn_conv_w"][i], _vec(P["ffn_conv_b"][i]), f"conv_bwd{i}")
        small["ffn_conv_w"][i], small["ffn_conv_b"][i] = dcw, dcb
        dh2 = _mm(da2, W["ffn_w_up"], "nt", f"ffn_up_dx{i}", b_layer=i)
        G["ffn_w_up"].append(_mm(rec["h2"], da2, "tn", f"ffn_up_dw{i}", exchange_out=True))
        ng2 = _vec(P["norm_g"][i, 1])
        dx, dy, s_sh2, s_x2, d_g1 = _norm_bwd(rec["x2"], dh2, dx, ng2, sc2, rec["y1"], g1, f"norm_bwd_b{i}")
        d_sc2, d_ng2 = s_x2 * ng2, s_x2 * (1.0 + sc2)
        if i % 2 == 0:
            dgated = _mm(dy, W["gm_w_out"], "nt", f"gm_out_dx{i}", b_layer=j)
            G["gm_w_out"].append(_mm(rec["act"], dy, "tn", f"gm_out_dw{i}"))
            da, dws, dbs, dlg, dlbeta = _gm_mid_bwd(
                rec["a"], dgated, _vec(P["gm_ln_g"][j]), _vec(P["gm_ln_b"][j]), P["gm_w_s"][j],
                P["gm_b_s"][j].reshape(GM_HEADS, GM_BLOCK, 1), f"gm_mid_bwd{i}")
            small["gm_w_s"][j], small["gm_b_s"][j] = dws, dbs[:, :GM_HEADS].T
            small["gm_ln_g"][j], small["gm_ln_b"][j] = dlg, dlbeta
            dh1 = _mm(da, W["gm_w_in"], "nt", f"gm_in_dx{i}", b_layer=j)
            G["gm_w_in"].append(_mm(rec["h1"], da, "tn", f"gm_in_dw{i}", exchange_out=True))
        else:
            dog = _mm(dy, W["hg_w_out"], "nt", f"hg_out_dx{i}", b_layer=j)
            G["hg_w_out"].append(_mm(rec["act"], dy, "tn", f"hg_out_dw{i}"))
            dp, dlb, dgn = _hg_scan_bwd(rec["a"], _vec(lb_all[j]), _vec(P["hg_gn_g"][j]), rec["o"], dog,
                                        rec["states"], f"hg_scan_bwd{i}")
            small["dlb"][j], small["hg_gn_g"][j] = dlb, dgn
            dh1 = _mm(dp, W["hg_w_in"], "nt", f"hg_in_dx{i}", b_layer=j)
            G["hg_w_in"].append(_mm(rec["h1"], dp, "tn", f"hg_in_dw{i}", exchange_out=True))
        ng1 = _vec(P["norm_g"][i, 0])
        if i > 0:
            prev = saved[i - 1]
            dx, dy, s_sh1, s_x1, s_gate = _norm_bwd(rec["x1"], dh1, dx, ng1, sc1, prev["y2"], prev["mods"][5],
                                                    f"norm_bwd_a{i}")
        else:
            dx, s_sh1, s_x1 = _norm_bwd(rec["x1"], dh1, dx, ng1, sc1, None, None, f"norm_bwd_a{i}")
        d_sc1, d_ng1 = s_x1 * ng1, s_x1 * (1.0 + sc1)
        small["norm_g"][i] = jnp.concatenate([d_ng1, d_ng2], axis=0)
        small["dmod"][i] = jnp.concatenate([s_sh1, d_sc1, d_g1, s_sh2, d_sc2, d_g2], axis=1)
    for k in G:
        G[k] = G[k][::-1]
    dlb_all = jnp.concatenate(small.pop("dlb"), axis=0)
    small["hg_lb"] = _lb_bwd(P["hg_lb"], dlb_all, "lb_bwd")
    return loss, dx, G, small


BIG = ("gm_w_in", "gm_w_out", "hg_w_in", "hg_w_out", "ffn_w_up", "ffn_w_down")
COL_SHARDED = dict(gm_w_in=True, gm_w_out=False, hg_w_in=True, hg_w_out=False, ffn_w_up=True, ffn_w_down=False)


def _pack(pieces):
    flat = [p.reshape(-1).astype(F32) for p in pieces]
    offs, tot = [], 0
    for f in flat:
        offs.append((tot, f.shape[0]))
        tot += f.shape[0]
    padded = -(-tot // (8 * LANES)) * (8 * LANES)
    if padded > tot:
        flat.append(jnp.zeros((padded - tot,), F32))
    return jnp.concatenate(flat).reshape(-1, LANES), offs


def _unpack(rows, offs, shapes):
    lead = rows.shape[:-2]
    flat = rows.reshape(lead + (-1,))
    return [flat[..., o:o + n].reshape(lead + tuple(s)) for (o, n), s in zip(offs, shapes)]


def _from_chips(per_dev, axis):
    per_chip = per_dev[0::2]
    return jnp.concatenate([per_chip[s] for s in range(N_CHIPS)], axis=axis)


def kernel(x, c, gm_w_in, gm_ln_g, gm_ln_b, gm_w_s, gm_b_s, gm_w_out, hg_w_in, hg_lb, hg_gn_g, hg_w_out, ffn_w_up, ffn_conv_w, ffn_conv_b, ffn_w_down, norm_g, ada_w, ada_b, final_g, loss_target, m_gm_w_in, m_gm_ln_g, m_gm_ln_b, m_gm_w_s, m_gm_b_s, m_gm_w_out, m_hg_w_in, m_hg_lb, m_hg_gn_g, m_hg_w_out, m_ffn_w_up, m_ffn_conv_w, m_ffn_conv_b, m_ffn_w_down, m_norm_g, m_ada_w, m_ada_b, m_final_g, v_gm_w_in, v_gm_ln_g, v_gm_ln_b, v_gm_w_s, v_gm_b_s, v_gm_w_out, v_hg_w_in, v_hg_lb, v_hg_gn_g, v_hg_w_out, v_ffn_w_up, v_ffn_conv_w, v_ffn_conv_b, v_ffn_w_down, v_norm_g, v_ada_w, v_ada_b, v_final_g):
    Dm = D_MODEL
    xi, yi, ci = _position()
    chip = 2 * xi + yi
    dev = 4 * xi + 2 * yi + ci
    weights = dict(gm_w_in=gm_w_in, gm_ln_g=gm_ln_g, gm_ln_b=gm_ln_b, gm_w_s=gm_w_s, gm_b_s=gm_b_s,
                   gm_w_out=gm_w_out, hg_w_in=hg_w_in, hg_lb=hg_lb, hg_gn_g=hg_gn_g, hg_w_out=hg_w_out,
                   ffn_w_up=ffn_w_up, ffn_conv_w=ffn_conv_w, ffn_conv_b=ffn_conv_b, ffn_w_down=ffn_w_down,
                   norm_g=norm_g, ada_w=ada_w, ada_b=ada_b, final_g=final_g)
    mom_m = dict(gm_w_in=m_gm_w_in, gm_ln_g=m_gm_ln_g, gm_ln_b=m_gm_ln_b, gm_w_s=m_gm_w_s, gm_b_s=m_gm_b_s,
                 gm_w_out=m_gm_w_out, hg_w_in=m_hg_w_in, hg_lb=m_hg_lb, hg_gn_g=m_hg_gn_g, hg_w_out=m_hg_w_out,
                 ffn_w_up=m_ffn_w_up, ffn_conv_w=m_ffn_conv_w, ffn_conv_b=m_ffn_conv_b, ffn_w_down=m_ffn_w_down,
                 norm_g=m_norm_g, ada_w=m_ada_w, ada_b=m_ada_b, final_g=m_final_g)
    mom_v = dict(gm_w_in=v_gm_w_in, gm_ln_g=v_gm_ln_g, gm_ln_b=v_gm_ln_b, gm_w_s=v_gm_w_s, gm_b_s=v_gm_b_s,
                 gm_w_out=v_gm_w_out, hg_w_in=v_hg_w_in, hg_lb=v_hg_lb, hg_gn_g=v_hg_gn_g, hg_w_out=v_hg_w_out,
                 ffn_w_up=v_ffn_w_up, ffn_conv_w=v_ffn_conv_w, ffn_conv_b=v_ffn_conv_b, ffn_w_down=v_ffn_w_down,
                 norm_g=v_norm_g, ada_w=v_ada_w, ada_b=v_ada_b, final_g=v_final_g)
    order = list(weights)

    pieces = [c, hg_lb, hg_gn_g, norm_g, ffn_conv_w]
    payload, offs = _pack(pieces)
    got = _allgather_small(payload, "gather_small")
    c_g, lb_g, gn_g, ng_g, cw_g = _unpack(got, offs, [p.shape for p in pieces])
    c_all = c_g.reshape(N_DEV, Dm)
    P = dict(hg_lb=_from_chips(lb_g, 1), hg_gn_g=_from_chips(gn_g, 1), norm_g=_from_chips(ng_g, 2),
             ffn_conv_w=_from_chips(cw_g, 2), gm_ln_g=gm_ln_g, gm_ln_b=gm_ln_b, gm_w_s=gm_w_s, gm_b_s=gm_b_s,
             ffn_conv_b=ffn_conv_b, final_g=final_g)

    cols = ada_w.shape[2]
    ada_b_sh = lax.dynamic_slice_in_dim(ada_b, chip * cols, cols, axis=1)
    mod_sh = _ada_fwd(c_all, ada_w, ada_b_sh, "ada_fwd")
    mod_g = _allgather_small(mod_sh.reshape(-1, LANES), "gather_mod").reshape(N_DEV, DEPTH, N_DEV, cols)
    mod_mine = lax.dynamic_index_in_dim(mod_g[0::2], dev, axis=2, keepdims=False)
    mod = jnp.transpose(mod_mine, (1, 0, 2)).reshape(DEPTH, N_CHIPS * cols)

    shards = [weights[k].astype(BF16) for k in BIG]
    full = _gather_weights(shards, [COL_SHARDED[k] for k in BIG], "gather_weights")
    W = dict(zip(BIG, full))

    loss_part, dx, G, small = _local_step(x[0], loss_target[0], mod, W, P)

    g4, names = [], []
    for k in BIG:
        for l, g in enumerate(G[k]):
            if not COL_SHARDED[k]:
                R, C = g.shape
                g = g.reshape(N_CHIPS, 2, R // (2 * N_CHIPS), C)
            g4.append(g)
            names.append(f"{k}{l}")
    other = 1 - ci
    to_sib = [lax.dynamic_index_in_dim(g, other, axis=1, keepdims=False).astype(BF16) for g in g4]
    from_sib = _swap_sibling(to_sib, "reduce_swap")
    core = jnp.reshape(ci, (1,)).astype(jnp.int32)
    chip_sums = [_add_own_half(g, r, core, f"chip_sum_{nm}") for g, r, nm in zip(g4, from_sib, names)]
    by_src = _exchange_chips(chip_sums, "reduce_exchange")
    halves = [_sum_chips(r, f"sum_chips_{nm}") for r, nm in zip(by_src, names)]
    groups, pos = [], 0
    for k in BIG:
        groups.append(list(range(pos, pos + len(G[k]))))
        pos += len(G[k])
    joined = _join_halves(halves, groups, "reduce_join")
    grads = {}
    for k, jg in zip(BIG, joined):
        L, _, rh, cw = jg.shape
        grads[k] = jg.reshape(L, 2 * rh, cw)

    sum_pieces = [loss_part[:, :1], small["final_g"], jnp.stack(small["gm_ln_g"]), jnp.stack(small["gm_ln_b"]),
                  jnp.stack(small["gm_w_s"]), jnp.stack(small["gm_b_s"]), jnp.stack(small["ffn_conv_b"]),
                  small["hg_lb"], jnp.stack(small["hg_gn_g"]), jnp.stack(small["norm_g"]),
                  jnp.stack(small["ffn_conv_w"])]
    dmod = jnp.concatenate(small["dmod"], axis=0)
    payload2, offs2 = _pack(sum_pieces + [dmod])
    got2 = _allgather_small(payload2, "gather_grads")
    dmod_all = _unpack(got2, offs2[-1:], [dmod.shape])[0]
    summed = _sum_devices(got2, "sum_devices")
    (loss_s, d_final_g, d_ln_g, d_ln_b, d_ws, d_bs, d_cb, d_lb, d_gn, d_ng, d_cw) = _unpack(
        summed, offs2[:-1], [(1,), final_g.shape, gm_ln_g.shape, gm_ln_b.shape, gm_w_s.shape, gm_b_s.shape,
                             ffn_conv_b.shape, (2, Dm), (2, Dm), (DEPTH, 2, Dm), (DEPTH, 3, 2 * FFN_HIDDEN)])
    grads.update(final_g=d_final_g, gm_ln_g=d_ln_g, gm_ln_b=d_ln_b, gm_w_s=d_ws, gm_b_s=d_bs, ffn_conv_b=d_cb)
    grads["hg_lb"] = lax.dynamic_slice_in_dim(d_lb, chip * hg_lb.shape[1], hg_lb.shape[1], axis=1)
    grads["hg_gn_g"] = lax.dynamic_slice_in_dim(d_gn, chip * hg_gn_g.shape[1], hg_gn_g.shape[1], axis=1)
    grads["norm_g"] = lax.dynamic_slice_in_dim(d_ng, chip * norm_g.shape[2], norm_g.shape[2], axis=2)
    grads["ffn_conv_w"] = lax.dynamic_slice_in_dim(d_cw, chip * ffn_conv_w.shape[2], ffn_conv_w.shape[2], axis=2)
    dmod_sh = lax.dynamic_slice_in_dim(dmod_all, chip * cols, cols, axis=2)
    grads["ada_w"] = _ada_bwd(c_all, jnp.transpose(dmod_sh, (1, 0, 2)), "ada_bwd")
    grads["ada_b"] = _sum_devices(dmod_all.reshape(N_DEV, -1, LANES), "sum_ada_b").reshape(ada_b.shape)

    deltas, new_m, new_v = {}, {}, {}
    for k in order:
        w = weights[k]
        shp = w.shape
        view = (-1, shp[-1]) if w.ndim > 1 else (8, -1)
        d, m2, v2 = _adamw(w.reshape(view), grads[k].reshape(view), mom_m[k].reshape(view), mom_v[k].reshape(view),
                           f"adamw_{k}")
        deltas[k], new_m[k], new_v[k] = d.reshape(shp), m2.reshape(shp), v2.reshape(shp)
        grads[k] = grads[k].reshape(shp)

    loss = loss_s.reshape(())
    return (loss, dx[None], *[grads[k] for k in order], *[deltas[k] for k in order],
            *[new_m[k] for k in order], *[new_v[k] for k in order])
```

```python
import functools

import jax
import jax.numpy as jnp
from jax import lax
from jax.experimental import pallas as pl
from jax.experimental.pallas import tpu as pltpu

F32 = jnp.float32
BF16 = jnp.bfloat16
HI = lax.Precision.HIGHEST
X3 = lax.Precision.HIGH
MESH = pl.DeviceIdType.MESH

D_MODEL = 1024
DEPTH = 4
EPS = 1e-6
GM_WIDTH = 2048
GM_HEADS = 8
GM_BLOCK = 128
GM_HEAD_DIM = 256
CHUNK = 64
HG_HEADS = 8
HG_DIM = 128
FFN_HIDDEN = 2816
N_CHIPS = 4
N_DEV = 8

ADAM_LR = 0.001
ADAM_B1 = 0.9
ADAM_B2 = 0.999
ADAM_EPS = 1e-08
ADAM_WD = 0.01
ADAM_STEP = 10

VMEM_LIMIT_BYTES = 56 * 1024 * 1024
ROW_TILE = 256
LANES = 128

_SQRT_HALF = 0.7071067811865476
_INV_SQRT_2PI = 0.3989422804014327


def _pick(dim, prefs):
    for p in prefs:
        if dim % p == 0:
            return p
    return dim


def _params(sem):
    return pltpu.CompilerParams(dimension_semantics=sem, vmem_limit_bytes=VMEM_LIMIT_BYTES)


def _cdf(x):
    return 0.5 * (1.0 + lax.erf(x * _SQRT_HALF))


def _pdf(x):
    return jnp.exp(-0.5 * x * x) * _INV_SQRT_2PI


def _sig(x):
    return jax.nn.sigmoid(x)


def _dot(a, b, dims, prec=None):
    return lax.dot_general(a, b, (dims, ((), ())), precision=prec, preferred_element_type=F32)


NN = ((1,), (0,))
NT = ((1,), (1,))
TN = ((0,), (0,))


def _mm(a, b, mode, name, b_layer=None, out_dtype=F32, exchange_out=False):
    b2 = b.shape[-2:]
    if mode == "nn":
        (M, K), (_, N) = a.shape, b2
    elif mode == "nt":
        (M, K), (N, _) = a.shape, b2
    else:
        (K, M), (_, N) = a.shape, b2
    tm = _pick(M, (512, 256, 128) if mode == "tn" else (1024, 512, 256, 128))
    tn = _pick(N, (1408, 1024, 512, 256, 128))
    tk = _pick(K, (1408, 1024, 512, 256, 128))
    nk = K // tk
    dims = {"nn": NN, "nt": NT, "tn": TN}[mode]

    def body(a_ref, b_ref, o_ref, *scratch):
        part = _dot(a_ref[...].astype(BF16), b_ref[...].astype(BF16), dims)
        if nk == 1:
            o_ref[...] = part.astype(o_ref.dtype)
            return
        (acc_ref,) = scratch
        k = pl.program_id(2)

        @pl.when(k == 0)
        def _():
            acc_ref[...] = part

        @pl.when(k > 0)
        def _():
            acc_ref[...] += part

        @pl.when(k == nk - 1)
        def _():
            o_ref[...] = acc_ref[...].astype(o_ref.dtype)

    if mode == "tn":
        a_spec = pl.BlockSpec((tk, tm), lambda i, j, k: (k, i))
    else:
        a_spec = pl.BlockSpec((tm, tk), lambda i, j, k: (i, k))
    bblk = (tk, tn) if mode in ("nn", "tn") else (tn, tk)
    bidx = (lambda i, j, k: (k, j)) if mode in ("nn", "tn") else (lambda i, j, k: (j, k))
    if b_layer is None:
        b_spec = pl.BlockSpec(bblk, bidx)
    else:
        b_spec = pl.BlockSpec((None,) + bblk, lambda i, j, k: (b_layer,) + bidx(i, j, k))
    if exchange_out:
        mh, cw = M // 2, N // N_CHIPS
        assert mh % tm == 0 and cw % tn == 0
        out_shape = jax.ShapeDtypeStruct((N_CHIPS, 2, mh, cw), out_dtype)
        o_spec = pl.BlockSpec(
            (None, None, tm, tn),
            lambda i, j, k: (j // (cw // tn), i // (mh // tm), i % (mh // tm), j % (cw // tn)))
    else:
        out_shape = jax.ShapeDtypeStruct((M, N), out_dtype)
        o_spec = pl.BlockSpec((tm, tn), lambda i, j, k: (i, j))
    return pl.pallas_call(
        body, name=name, out_shape=out_shape, grid=(M // tm, N // tn, nk),
        in_specs=[a_spec, b_spec], out_specs=o_spec,
        scratch_shapes=[] if nk == 1 else [pltpu.VMEM((tm, tn), F32)],
        compiler_params=_params(("parallel", "parallel", "arbitrary")),
    )(a, b)


def _row_spec(tr, width):
    return pl.BlockSpec((tr, width), lambda i: (i, 0))


def _vec_spec(width, rows=1):
    return pl.BlockSpec((rows, width), lambda i: (0, 0))


def _norm_fwd(x, y, gate, g, sc, sh, name):
    T, Dm = x.shape
    tr = _pick(T, (ROW_TILE,))
    has_res = y is not None

    def body(*refs):
        if has_res:
            x_ref, y_ref, gate_ref, g_ref, sc_ref, sh_ref, xo_ref, h_ref = refs
            xv = x_ref[...] + gate_ref[...] * y_ref[...]
            xo_ref[...] = xv
        else:
            x_ref, g_ref, sc_ref, sh_ref, h_ref = refs
            xv = x_ref[...]
        rstd = lax.rsqrt(jnp.mean(xv * xv, axis=-1, keepdims=True) + EPS)
        h_ref[...] = ((xv * rstd * g_ref[...]) * (1.0 + sc_ref[...]) + sh_ref[...]).astype(BF16)

    row, vec = _row_spec(tr, Dm), _vec_spec(Dm)
    if has_res:
        ins, in_specs = (x, y, gate, g, sc, sh), [row, row, vec, vec, vec, vec]
        out_shape = (jax.ShapeDtypeStruct((T, Dm), F32), jax.ShapeDtypeStruct((T, Dm), BF16))
        out_specs = (row, row)
    else:
        ins, in_specs = (x, g, sc, sh), [row, vec, vec, vec]
        out_shape = jax.ShapeDtypeStruct((T, Dm), BF16)
        out_specs = row
    out = pl.pallas_call(body, name=name, out_shape=out_shape, grid=(T // tr,), in_specs=in_specs,
                         out_specs=out_specs, compiler_params=_params(("parallel",)))(*ins)
    return out if has_res else (x, out)


def _norm_bwd(x, dh, dxo, g, sc, y_prev, gate_prev, name):
    T, Dm = x.shape
    tr = _pick(T, (ROW_TILE,))
    has_prev = y_prev is not None

    def body(*refs):
        if has_prev:
            x_ref, dh_ref, dxo_ref, g_ref, sc_ref, yp_ref, gp_ref, dx_ref, dyp_ref, s1_ref, s2_ref, s3_ref = refs
        else:
            x_ref, dh_ref, dxo_ref, g_ref, sc_ref, dx_ref, s1_ref, s2_ref = refs

        @pl.when(pl.program_id(0) == 0)
        def _():
            s1_ref[...] = jnp.zeros_like(s1_ref)
            s2_ref[...] = jnp.zeros_like(s2_ref)
            if has_prev:
                s3_ref[...] = jnp.zeros_like(s3_ref)

        xv = x_ref[...]
        rstd = lax.rsqrt(jnp.mean(xv * xv, axis=-1, keepdims=True) + EPS)
        xhat = xv * rstd
        dh = dh_ref[...]
        dxhat = dh * (g_ref[...] * (1.0 + sc_ref[...]))
        dx = dxo_ref[...] + rstd * (dxhat - xhat * jnp.mean(dxhat * xhat, axis=-1, keepdims=True))
        dx_ref[...] = dx
        s1_ref[...] += jnp.sum(dh, axis=0, keepdims=True)
        s2_ref[...] += jnp.sum(dh * xhat, axis=0, keepdims=True)
        if has_prev:
            dyp_ref[...] = (gp_ref[...] * dx).astype(BF16)
            s3_ref[...] += jnp.sum(dx * yp_ref[...], axis=0, keepdims=True)

    row, vec = _row_spec(tr, Dm), _vec_spec(Dm)
    vshape = jax.ShapeDtypeStruct((1, Dm), F32)
    if has_prev:
        ins, in_specs = (x, dh, dxo, g, sc, y_prev, gate_prev), [row, row, row, vec, vec, row, vec]
        out_shape = (jax.ShapeDtypeStruct((T, Dm), F32), jax.ShapeDtypeStruct((T, Dm), BF16), vshape, vshape, vshape)
        out_specs = (row, row, vec, vec, vec)
    else:
        ins, in_specs = (x, dh, dxo, g, sc), [row, row, row, vec, vec]
        out_shape = (jax.ShapeDtypeStruct((T, Dm), F32), vshape, vshape)
        out_specs = (row, vec, vec)
    return pl.pallas_call(body, name=name, out_shape=out_shape, grid=(T // tr,), in_specs=in_specs,
                          out_specs=out_specs, compiler_params=_params(("arbitrary",)))(*ins)


def _loss_head(x, y, gate, fg, tgt, name):
    T, Dm = x.shape
    tr = _pick(T, (ROW_TILE,))
    nsteps = T // tr

    def body(x_ref, y_ref, gate_ref, fg_ref, t_ref, dx_ref, dy_ref, loss_ref, sfg_ref, sg_ref, acc_ref):
        i = pl.program_id(0)

        @pl.when(i == 0)
        def _():
            acc_ref[...] = jnp.zeros_like(acc_ref)
            sfg_ref[...] = jnp.zeros_like(sfg_ref)
            sg_ref[...] = jnp.zeros_like(sg_ref)

        yv = y_ref[...]
        xv = x_ref[...] + gate_ref[...] * yv
        rstd = lax.rsqrt(jnp.mean(xv * xv, axis=-1, keepdims=True) + EPS)
        xhat = xv * rstd
        err = xhat * fg_ref[...] - t_ref[...]
        acc_ref[...] += jnp.sum(err * err, axis=0, keepdims=True)
        dyn = err * (1.0 / Dm)
        sfg_ref[...] += jnp.sum(dyn * xhat, axis=0, keepdims=True)
        dxhat = dyn * fg_ref[...]
        dx = rstd * (dxhat - xhat * jnp.mean(dxhat * xhat, axis=-1, keepdims=True))
        dx_ref[...] = dx
        dy_ref[...] = (gate_ref[...] * dx).astype(BF16)
        sg_ref[...] += jnp.sum(dx * yv, axis=0, keepdims=True)

        @pl.when(i == nsteps - 1)
        def _():
            total = jnp.sum(acc_ref[...], axis=1, keepdims=True) * (0.5 / Dm)
            loss_ref[...] = jnp.broadcast_to(total, loss_ref.shape)

    row, vec = _row_spec(tr, Dm), _vec_spec(Dm)
    vshape = jax.ShapeDtypeStruct((1, Dm), F32)
    return pl.pallas_call(
        body, name=name, grid=(nsteps,),
        out_shape=(jax.ShapeDtypeStruct((T, Dm), F32), jax.ShapeDtypeStruct((T, Dm), BF16),
                   jax.ShapeDtypeStruct((1, LANES), F32), vshape, vshape),
        in_specs=[row, row, vec, vec, row], out_specs=(row, row, _vec_spec(LANES), vec, vec),
        scratch_shapes=[pltpu.VMEM((1, Dm), F32)], compiler_params=_params(("arbitrary",)),
    )(x, y, gate, fg, tgt)


def _spatial_mask():
    r = lax.broadcasted_iota(jnp.int32, (GM_BLOCK, GM_BLOCK), 0) // CHUNK
    c = lax.broadcasted_iota(jnp.int32, (GM_BLOCK, GM_BLOCK), 1) // CHUNK
    return r >= c


def _gm_specs(tr):
    return [_row_spec(tr, 2 * GM_WIDTH), _vec_spec(GM_WIDTH), _vec_spec(GM_WIDTH),
            pl.BlockSpec((GM_HEADS, GM_BLOCK, GM_BLOCK), lambda i: (0, 0, 0)),
            pl.BlockSpec((GM_HEADS, GM_BLOCK, 1), lambda i: (0, 0, 0))]


def _gm_mid_fwd(a, ln_g, ln_b, ws, bs3, name):
    T = a.shape[0]
    tr = _pick(T, (ROW_TILE,))
    W = GM_WIDTH

    def body(a_ref, lg_ref, lb_ref, ws_ref, bs_ref, o_ref, vn_scr):
        av = a_ref[:, W:]
        v = av * _cdf(av)
        vc = v - jnp.mean(v, axis=-1, keepdims=True)
        rstd = lax.rsqrt(jnp.mean(vc * vc, axis=-1, keepdims=True) + EPS)
        vn_scr[...] = (vc * rstd * lg_ref[...] + lb_ref[...]).astype(BF16)
        mask = _spatial_mask()
        for h in range(GM_HEADS):
            w = jnp.where(mask, ws_ref[h], 0.0).astype(BF16)
            cs = slice(h * GM_HEAD_DIM, (h + 1) * GM_HEAD_DIM)
            for blk in range(tr // GM_BLOCK):
                rs = slice(blk * GM_BLOCK, (blk + 1) * GM_BLOCK)
                s = _dot(w, vn_scr[rs, cs], NN) + bs_ref[h]
                au = a_ref[rs, cs]
                o_ref[rs, cs] = (au * _cdf(au) * s).astype(BF16)

    return pl.pallas_call(
        body, name=name, out_shape=jax.ShapeDtypeStruct((T, W), BF16), grid=(T // tr,),
        in_specs=_gm_specs(tr), out_specs=_row_spec(tr, W),
        scratch_shapes=[pltpu.VMEM((tr, W), BF16)], compiler_params=_params(("parallel",)),
    )(a, ln_g, ln_b, ws, bs3)


def _gm_mid_bwd(a, dgated, ln_g, ln_b, ws, bs3, name):
    T = a.shape[0]
    tr = _pick(T, (ROW_TILE,))
    W = GM_WIDTH
    nsteps = T // tr

    def body(a_ref, dg_ref, lg_ref, lb_ref, ws_ref, bs_ref, da_ref, dws_ref, dbs_ref, dlg_ref, dlb_ref,
             vn_scr, vhat_scr, dvn_scr, dsum_scr):
        i = pl.program_id(0)

        @pl.when(i == 0)
        def _():
            dws_ref[...] = jnp.zeros_like(dws_ref)
            dbs_ref[...] = jnp.zeros_like(dbs_ref)
            dlg_ref[...] = jnp.zeros_like(dlg_ref)
            dlb_ref[...] = jnp.zeros_like(dlb_ref)
            dsum_scr[...] = jnp.zeros_like(dsum_scr)

        av = a_ref[:, W:]
        cdf_v = _cdf(av)
        v = av * cdf_v
        vc = v - jnp.mean(v, axis=-1, keepdims=True)
        rstd = lax.rsqrt(jnp.mean(vc * vc, axis=-1, keepdims=True) + EPS)
        vhat_scr[...] = vc * rstd
        vn_scr[...] = (vhat_scr[...] * lg_ref[...] + lb_ref[...]).astype(BF16)
        mask = _spatial_mask()
        for h in range(GM_HEADS):
            w = jnp.where(mask, ws_ref[h], 0.0).astype(BF16)
            cs = slice(h * GM_HEAD_DIM, (h + 1) * GM_HEAD_DIM)
            for blk in range(tr // GM_BLOCK):
                rs = slice(blk * GM_BLOCK, (blk + 1) * GM_BLOCK)
                vnb = vn_scr[rs, cs]
                s = _dot(w, vnb, NN) + bs_ref[h]
                au = a_ref[rs, cs]
                cdf_u = _cdf(au)
                dg = dg_ref[rs, cs]
                ds = dg * (au * cdf_u)
                da_ref[rs, cs] = (dg * s * (cdf_u + au * _pdf(au))).astype(BF16)
                dsb = ds.astype(BF16)
                dvn_scr[rs, cs] = _dot(w, dsb, TN)
                dws_ref[h] += _dot(dsb, vnb, NT)
                dsum_scr[:, cs] += ds
        dvn = dvn_scr[...]
        vhat = vhat_scr[...]
        dlg_ref[...] += jnp.sum(dvn * vhat, axis=0, keepdims=True)
        dlb_ref[...] += jnp.sum(dvn, axis=0, keepdims=True)
        dvh = dvn * lg_ref[...]
        dv = rstd * (dvh - jnp.mean(dvh, axis=-1, keepdims=True)
                     - vhat * jnp.mean(dvh * vhat, axis=-1, keepdims=True))
        da_ref[:, W:] = (dv * (cdf_v + av * _pdf(av))).astype(BF16)

        @pl.when(i == nsteps - 1)
        def _():
            for h in range(GM_HEADS):
                dws_ref[h] = jnp.where(mask, dws_ref[h], 0.0)
            col_head = lax.broadcasted_iota(jnp.int32, (W, GM_BLOCK), 0) // GM_HEAD_DIM
            sel = (col_head == lax.broadcasted_iota(jnp.int32, (W, GM_BLOCK), 1)).astype(F32)
            dbs_ref[...] = _dot(dsum_scr[...], sel, NN, HI)

    vshape = jax.ShapeDtypeStruct((1, W), F32)
    return pl.pallas_call(
        body, name=name, grid=(nsteps,),
        out_shape=(jax.ShapeDtypeStruct((T, 2 * W), BF16), jax.ShapeDtypeStruct((GM_HEADS, GM_BLOCK, GM_BLOCK), F32),
                   jax.ShapeDtypeStruct((GM_BLOCK, GM_BLOCK), F32), vshape, vshape),
        in_specs=[_gm_specs(tr)[0], _row_spec(tr, W)] + _gm_specs(tr)[1:],
        out_specs=(_row_spec(tr, 2 * W), pl.BlockSpec((GM_HEADS, GM_BLOCK, GM_BLOCK), lambda i: (0, 0, 0)),
                   pl.BlockSpec((GM_BLOCK, GM_BLOCK), lambda i: (0, 0)), _vec_spec(W), _vec_spec(W)),
        scratch_shapes=[pltpu.VMEM((tr, W), BF16), pltpu.VMEM((tr, W), F32), pltpu.VMEM((tr, W), F32),
                        pltpu.VMEM((GM_BLOCK, W), F32)],
        compiler_params=_params(("arbitrary",)),
    )(a, dgated, ln_g, ln_b, ws, bs3)


SUB = 16
EXP_CLAMP = 80.0


def _tri(lower):
    r = lax.broadcasted_iota(jnp.int32, (CHUNK, CHUNK), 0)
    c = lax.broadcasted_iota(jnp.int32, (CHUNK, CHUNK), 1)
    return (r >= c) if lower else (c >= r)


def _score_masks():
    i = lax.broadcasted_iota(jnp.int32, (CHUNK, CHUNK), 0)
    j = lax.broadcasted_iota(jnp.int32, (CHUNK, CHUNK), 1)
    bi, bj = i // SUB, j // SUB
    diag = (bi == bj) & (i >= j)
    pair = (bi % 2 == 1) & (bj == bi - 1)
    half = (i >= CHUNK // 2) & (j < CHUNK // 2)
    return diag, pair, half


def _ref_selector():
    r = lax.broadcasted_iota(jnp.int32, (5 * CHUNK, CHUNK), 0)
    c = lax.broadcasted_iota(jnp.int32, (5 * CHUNK, CHUNK), 1)
    seg, i = r // CHUNK, r % CHUNK
    blk = (i // SUB) * SUB
    upto = jnp.where(seg == 0, i, jnp.where(seg == 1, blk + SUB // 2 - 1, jnp.where(
        seg == 2, blk - 1, jnp.where(seg == 3, blk + SUB - 1, CHUNK // 2 - 1))))
    return (c <= upto).astype(F32)


def _hg_gates(p_ref, lb_ref, h, sel):
    Dm = D_MODEL
    c0 = h * HG_DIM
    qr = p_ref[:, c0:c0 + HG_DIM]
    fz = p_ref[:, Dm + c0:Dm + c0 + HG_DIM]
    v = p_ref[:, 2 * Dm + c0:2 * Dm + c0 + HG_DIM]
    gt = p_ref[:, 3 * Dm + c0:3 * Dm + c0 + HG_DIM]
    lbh = lb_ref[:, c0:c0 + HG_DIM]
    sg = _sig(fz)
    f = lbh + (1.0 - lbh) * sg
    gl = jnp.log(f)
    kk = 1.0 - f
    sq = _sig(qr)
    q = qr * sq
    refs = _dot(sel, gl, NN, HI)
    b, r_mid, r_prev, r_end, r_half = (refs[s * CHUNK:(s + 1) * CHUNK] for s in range(5))
    bc = jnp.sum(gl, axis=0, keepdims=True)
    eqs = (jnp.exp(jnp.clip(b - r_mid, -EXP_CLAMP, EXP_CLAMP)), jnp.exp(jnp.minimum(b - r_prev, 0.0)),
           jnp.exp(jnp.minimum(b - r_half, 0.0)))
    eks = (jnp.exp(jnp.clip(r_mid - b, -EXP_CLAMP, EXP_CLAMP)), jnp.exp(jnp.minimum(r_end - b, 0.0)),
           jnp.exp(jnp.minimum(r_half - b, 0.0)))
    eb = jnp.exp(b)
    ec = jnp.exp(bc - b)
    return dict(qr=qr, v=v, gt=gt, lbh=lbh, sg=sg, f=f, kk=kk, sq=sq, q=q, eqs=eqs, eks=eks, eb=eb, ec=ec,
                e_end=jnp.exp(bc), qs=[q * e for e in eqs], ks=[kk * e for e in eks], qe=q * eb, ke=kk * ec)


def _scores(g, masks):
    a = None
    for qs, ks, m in zip(g["qs"], g["ks"], masks):
        part = jnp.where(m, _dot(qs, ks, NT, X3), 0.0)
        a = part if a is None else a + part
    return a


def _hg_scan_fwd(p, lb, gn, name):
    T = p.shape[0]
    nc = T // CHUNK
    Dm = D_MODEL

    def body(p_ref, lb_ref, gn_ref, o_ref, og_ref, so_ref, st_ref):
        @pl.when(pl.program_id(0) == 0)
        def _():
            st_ref[...] = jnp.zeros_like(st_ref)

        masks = _score_masks()
        sel = _ref_selector()
        for h in range(HG_HEADS):
            g = _hg_gates(p_ref, lb_ref, h, sel)
            cs = slice(h * HG_DIM, (h + 1) * HG_DIM)
            a = _scores(g, masks)
            st = st_ref[h]
            so_ref[0, h] = st
            o = _dot(a, g["v"], NN, X3) + _dot(g["qe"], st, NT, X3)
            st_ref[h] = st * g["e_end"] + _dot(g["v"], g["ke"], TN, X3)
            o_ref[:, cs] = o
            r = lax.rsqrt(jnp.mean(o * o, axis=-1, keepdims=True) + EPS)
            gt = g["gt"]
            og_ref[:, cs] = (((o * r) * gn_ref[:, cs]).astype(F32) * (gt * _sig(gt))).astype(BF16)

    return pl.pallas_call(
        body, name=name, grid=(nc,),
        out_shape=(jax.ShapeDtypeStruct((T, Dm), F32), jax.ShapeDtypeStruct((T, Dm), BF16),
                   jax.ShapeDtypeStruct((nc, HG_HEADS, HG_DIM, HG_DIM), F32)),
        in_specs=[_row_spec(CHUNK, 4 * Dm), _vec_spec(Dm), _vec_spec(Dm)],
        out_specs=(_row_spec(CHUNK, Dm), _row_spec(CHUNK, Dm),
                   pl.BlockSpec((1, HG_HEADS, HG_DIM, HG_DIM), lambda i: (i, 0, 0, 0))),
        scratch_shapes=[pltpu.VMEM((HG_HEADS, HG_DIM, HG_DIM), F32)],
        compiler_params=_params(("arbitrary",)),
    )(p, lb, gn)


def _hg_scan_bwd(p, lb, gn, o, dog, states, name):
    T = p.shape[0]
    nc = T // CHUNK
    Dm = D_MODEL

    def rev(i):
        return nc - 1 - i

    def body(p_ref, lb_ref, gn_ref, o_ref, dog_ref, st_in_ref, dp_ref, dlb_ref, dgn_ref, dst_ref, carry_ref):
        @pl.when(pl.program_id(0) == 0)
        def _():
            dst_ref[...] = jnp.zeros_like(dst_ref)
            carry_ref[...] = jnp.zeros_like(carry_ref)
            dlb_ref[...] = jnp.zeros_like(dlb_ref)
            dgn_ref[...] = jnp.zeros_like(dgn_ref)

        upper = _tri(False).astype(F32)
        masks = _score_masks()
        sel = _ref_selector()
        for h in range(HG_HEADS):
            g = _hg_gates(p_ref, lb_ref, h, sel)
            c0 = h * HG_DIM
            cs = slice(c0, c0 + HG_DIM)
            oh = o_ref[:, cs]
            r = lax.rsqrt(jnp.mean(oh * oh, axis=-1, keepdims=True) + EPS)
            on = oh * r
            gt = g["gt"]
            sgt = _sig(gt)
            sil = gt * sgt
            dogh = dog_ref[:, cs]
            gnh = gn_ref[:, cs]
            don = dogh * gnh * sil
            dgn_ref[:, cs] += jnp.sum(dogh * on * sil, axis=0, keepdims=True)
            dgate = dogh * on * gnh * (sgt * (1.0 + gt * (1.0 - sgt)))
            do = r * (don - on * jnp.mean(don * on, axis=-1, keepdims=True))
            a = _scores(g, masks)
            dst = dst_ref[h]
            dob = do.astype(BF16)
            da = _dot(do, g["v"], NT, X3)
            dv = _dot(a.astype(BF16), dob, TN) + _dot(g["ke"].astype(BF16), dst.astype(BF16), NT)
            dq = _dot(do, st_in_ref[0, h], NN, X3) * g["eb"]
            dk = _dot(g["v"], dst, NN, X3) * g["ec"]
            for qs, ks, eq, ek, m in zip(g["qs"], g["ks"], g["eqs"], g["eks"], masks):
                dam = jnp.where(m, da, 0.0)
                dq = dq + _dot(dam, ks, NN, X3) * eq
                dk = dk + _dot(dam, qs, TN, X3) * ek
            dst_ref[h] = dst * g["e_end"] + _dot(do, g["qe"], TN, X3)
            dgd = g["q"] * dq - g["kk"] * dk
            dgl = _dot(upper, dgd, NN, HI) + carry_ref[h]
            carry_ref[h] += jnp.sum(dgd, axis=0, keepdims=True)
            df = dgl / g["f"] - dk
            sg = g["sg"]
            dlb_ref[:, cs] += jnp.sum(df * (1.0 - sg), axis=0, keepdims=True)
            sq, qr = g["sq"], g["qr"]
            dp_ref[:, c0:c0 + HG_DIM] = (dq * (sq * (1.0 + qr * (1.0 - sq)))).astype(BF16)
            dp_ref[:, Dm + c0:Dm + c0 + HG_DIM] = (df * (1.0 - g["lbh"]) * sg * (1.0 - sg)).astype(BF16)
            dp_ref[:, 2 * Dm + c0:2 * Dm + c0 + HG_DIM] = dv.astype(BF16)
            dp_ref[:, 3 * Dm + c0:3 * Dm + c0 + HG_DIM] = dgate.astype(BF16)

    vshape = jax.ShapeDtypeStruct((1, Dm), F32)
    rrow = lambda w: pl.BlockSpec((CHUNK, w), lambda i: (rev(i), 0))
    return pl.pallas_call(
        body, name=name, grid=(nc,),
        out_shape=(jax.ShapeDtypeStruct((T, 4 * Dm), BF16), vshape, vshape),
        in_specs=[rrow(4 * Dm), _vec_spec(Dm), _vec_spec(Dm), rrow(Dm), rrow(Dm),
                  pl.BlockSpec((1, HG_HEADS, HG_DIM, HG_DIM), lambda i: (rev(i), 0, 0, 0))],
        out_specs=(rrow(4 * Dm), _vec_spec(Dm), _vec_spec(Dm)),
        scratch_shapes=[pltpu.VMEM((HG_HEADS, HG_DIM, HG_DIM), F32), pltpu.VMEM((HG_HEADS, 1, HG_DIM), F32)],
        compiler_params=_params(("arbitrary",)),
    )(p, lb, gn, o, dog, states)


def _lb_fwd(hg_lb, name):
    def body(a_ref, o_ref):
        a0, a1 = a_ref[0:1], a_ref[1:2]
        m = jnp.maximum(a0, a1)
        e0, e1 = jnp.exp(a0 - m), jnp.exp(a1 - m)
        p0, p1 = e0 / (e0 + e1), e1 / (e0 + e1)
        o_ref[0:1] = p0 - p0
        o_ref[1:2] = (p0 + p1) - p0

    return pl.pallas_call(body, name=name, out_shape=jax.ShapeDtypeStruct(hg_lb.shape, F32))(hg_lb)


def _lb_bwd(hg_lb, dlb_all, name):
    def body(a_ref, d_ref, o_ref):
        a0, a1 = a_ref[0:1], a_ref[1:2]
        m = jnp.maximum(a0, a1)
        e0, e1 = jnp.exp(a0 - m), jnp.exp(a1 - m)
        p0, p1 = e0 / (e0 + e1), e1 / (e0 + e1)
        d1 = d_ref[1:2]
        o_ref[0:1] = -p0 * p1 * d1
        o_ref[1:2] = p1 * (1.0 - p1) * d1

    return pl.pallas_call(body, name=name, out_shape=jax.ShapeDtypeStruct(hg_lb.shape, F32))(hg_lb, dlb_all)


CONV_COLS = 256


def _conv_fwd(a, w, b, name):
    T = a.shape[0]
    Fh = FFN_HIDDEN
    tr = _pick(T, (ROW_TILE,))
    cw = CONV_COLS
    hb = tr // 8

    def body(a_ref, ap_ref, w_ref, b_ref, m_ref):
        m0 = (pl.program_id(0) > 0).astype(F32)

        def conv(cc):
            x = jnp.concatenate([ap_ref[:, pl.ds(cc, cw)] * m0, a_ref[:, pl.ds(cc, cw)]], axis=0)
            wv = w_ref[:, pl.ds(cc, cw)]
            y = b_ref[:, pl.ds(cc, cw)] + wv[2:3] * x + wv[1:2] * pltpu.roll(x, 1, axis=0) \
                + wv[0:1] * pltpu.roll(x, 2, axis=0)
            return y[8:]

        def step(c, carry):
            c0 = pl.multiple_of(c * cw, cw)
            c1 = pl.multiple_of(Fh + c * cw, cw)
            yg, yv = conv(c0), conv(c1)
            m_ref[:, pl.ds(c0, cw)] = (yg * _cdf(yg) * yv).astype(BF16)
            return carry

        lax.fori_loop(0, Fh // cw, step, 0)

    return pl.pallas_call(
        body, name=name, out_shape=jax.ShapeDtypeStruct((T, Fh), BF16), grid=(T // tr,),
        in_specs=[_row_spec(tr, 2 * Fh), pl.BlockSpec((8, 2 * Fh), lambda i: (jnp.maximum(i * hb - 1, 0), 0)),
                  _vec_spec(2 * Fh, 3), _vec_spec(2 * Fh)],
        out_specs=_row_spec(tr, Fh), compiler_params=_params(("parallel",)),
    )(a, a, w, b)


def _conv_bwd(a, dm, w, b, name):
    T = a.shape[0]
    Fh = FFN_HIDDEN
    tr = _pick(T, (ROW_TILE,))
    cw = CONV_COLS
    hb = tr // 8
    nsteps = T // tr
    n = tr + 8

    def body(a_ref, ap_ref, an_ref, dm_ref, dmn_ref, w_ref, b_ref, da_ref, dw_ref, db_ref):
        i = pl.program_id(0)
        m0 = (i > 0).astype(F32)
        m1 = (i < nsteps - 1).astype(F32)

        @pl.when(i == 0)
        def _():
            dw_ref[...] = jnp.zeros_like(dw_ref)
            db_ref[...] = jnp.zeros_like(db_ref)

        def prep(cc):
            x = jnp.concatenate([ap_ref[:, pl.ds(cc, cw)] * m0, a_ref[:, pl.ds(cc, cw)],
                                 an_ref[:, pl.ds(cc, cw)] * m1], axis=0)
            wv = w_ref[:, pl.ds(cc, cw)]
            s1 = pltpu.roll(x, 1, axis=0)
            s2 = pltpu.roll(x, 2, axis=0)
            y = b_ref[:, pl.ds(cc, cw)] + wv[2:3] * x + wv[1:2] * s1 + wv[0:1] * s2
            return wv, x[8:], s1[8:], s2[8:], y[8:]

        def back(cc, dy, wv, x0, s1, s2):
            da = wv[2:3] * dy + wv[1:2] * pltpu.roll(dy, n - 1, axis=0) + wv[0:1] * pltpu.roll(dy, n - 2, axis=0)
            da_ref[:, pl.ds(cc, cw)] = da[:tr].astype(BF16)
            d = dy[:tr]
            db_ref[:, pl.ds(cc, cw)] += jnp.sum(d, axis=0, keepdims=True)
            dw_ref[2:3, pl.ds(cc, cw)] += jnp.sum(d * x0[:tr], axis=0, keepdims=True)
            dw_ref[1:2, pl.ds(cc, cw)] += jnp.sum(d * s1[:tr], axis=0, keepdims=True)
            dw_ref[0:1, pl.ds(cc, cw)] += jnp.sum(d * s2[:tr], axis=0, keepdims=True)

        def step(c, carry):
            c0 = pl.multiple_of(c * cw, cw)
            c1 = pl.multiple_of(Fh + c * cw, cw)
            dmx = jnp.concatenate([dm_ref[:, pl.ds(c0, cw)], dmn_ref[:, pl.ds(c0, cw)] * m1], axis=0)
            wg, xg, s1g, s2g, yg = prep(c0)
            wv, xv, s1v, s2v, yv = prep(c1)
            cg = _cdf(yg)
            back(c0, dmx * yv * (cg + yg * _pdf(yg)), wg, xg, s1g, s2g)
            back(c1, dmx * (yg * cg), wv, xv, s1v, s2v)
            return carry

        lax.fori_loop(0, Fh // cw, step, 0)

    prev = lambda wd: pl.BlockSpec((8, wd), lambda i: (jnp.maximum(i * hb - 1, 0), 0))
    nxt = lambda wd: pl.BlockSpec((8, wd), lambda i: (jnp.minimum((i + 1) * hb, T // 8 - 1), 0))
    return pl.pallas_call(
        body, name=name, grid=(nsteps,),
        out_shape=(jax.ShapeDtypeStruct((T, 2 * Fh), BF16), jax.ShapeDtypeStruct((3, 2 * Fh), F32),
                   jax.ShapeDtypeStruct((1, 2 * Fh), F32)),
        in_specs=[_row_spec(tr, 2 * Fh), prev(2 * Fh), nxt(2 * Fh), _row_spec(tr, Fh), nxt(Fh),
                  _vec_spec(2 * Fh, 3), _vec_spec(2 * Fh)],
        out_specs=(_row_spec(tr, 2 * Fh), _vec_spec(2 * Fh, 3), _vec_spec(2 * Fh)),
        compiler_params=_params(("arbitrary",)),
    )(a, a, a, dm, dm, w, b)


def _ada_fwd(c_all, ada_w, ada_b, name):
    L, Dm, cols = ada_w.shape
    tn = _pick(cols, (512, 256, 128))

    def body(c_ref, w_ref, b_ref, o_ref):
        cv = c_ref[...]
        cond = (cv * _sig(cv)).astype(BF16)
        o_ref[...] = _dot(cond, w_ref[...].astype(BF16), NN) + b_ref[...]

    return pl.pallas_call(
        body, name=name, out_shape=jax.ShapeDtypeStruct((L, N_DEV, cols), F32), grid=(L, cols // tn),
        in_specs=[pl.BlockSpec((N_DEV, Dm), lambda l, j: (0, 0)), pl.BlockSpec((None, Dm, tn), lambda l, j: (l, 0, j)),
                  pl.BlockSpec((None, 1, tn), lambda l, j: (l, 0, j))],
        out_specs=pl.BlockSpec((None, N_DEV, tn), lambda l, j: (l, 0, j)),
        compiler_params=_params(("parallel", "parallel")),
    )(c_all, ada_w, ada_b.reshape(L, 1, cols))


def _ada_bwd(c_all, dmod, name):
    L, _, cols = dmod.shape
    Dm = c_all.shape[1]
    tn = _pick(cols, (512, 256, 128))

    def body(c_ref, d_ref, o_ref):
        cv = c_ref[...]
        o_ref[...] = _dot(cv * _sig(cv), d_ref[...], TN, HI)

    return pl.pallas_call(
        body, name=name, out_shape=jax.ShapeDtypeStruct((L, Dm, cols), F32), grid=(L, cols // tn),
        in_specs=[pl.BlockSpec((N_DEV, Dm), lambda l, j: (0, 0)), pl.BlockSpec((None, N_DEV, tn), lambda l, j: (l, 0, j))],
        out_specs=pl.BlockSpec((None, Dm, tn), lambda l, j: (l, 0, j)),
        compiler_params=_params(("parallel", "parallel")),
    )(c_all, dmod)


def _add_own_half(g4, rb, core, name):
    S, _, rh, cw = g4.shape
    tr = _pick(rh, (256, 128, 176, 64))

    def body(core_ref, g_ref, r_ref, o_ref):
        o_ref[...] = (g_ref[...] + r_ref[...].astype(F32)).astype(BF16)

    return pl.pallas_call(
        body, name=name, out_shape=jax.ShapeDtypeStruct((S, rh, cw), BF16),
        grid_spec=pltpu.PrefetchScalarGridSpec(
            num_scalar_prefetch=1, grid=(S, rh // tr),
            in_specs=[pl.BlockSpec((None, None, tr, cw), lambda s, i, core_ref: (s, core_ref[0], i, 0)),
                      pl.BlockSpec((None, tr, cw), lambda s, i, core_ref: (s, i, 0))],
            out_specs=pl.BlockSpec((None, tr, cw), lambda s, i, core_ref: (s, i, 0))),
        compiler_params=_params(("parallel", "parallel")),
    )(core, g4, rb)


def _sum_chips(rcs, name):
    L = len(rcs)
    S, rh, cw = rcs[0].shape
    tr = _pick(rh, (256, 128, 176, 64))

    def body(*refs):
        in_refs, o_ref = refs[:L], refs[L]
        for k in range(L):
            @pl.when(pl.program_id(0) == k)
            def _(k=k):
                acc = in_refs[k][0].astype(F32)
                for s in range(1, S):
                    acc = acc + in_refs[k][s].astype(F32)
                o_ref[...] = acc

    return pl.pallas_call(
        body, name=name, out_shape=jax.ShapeDtypeStruct((L, rh, cw), F32), grid=(L, rh // tr),
        in_specs=[pl.BlockSpec((S, tr, cw), lambda l, i, k=k: (0, jnp.where(l == k, i, 0), 0)) for k in range(L)],
        out_specs=pl.BlockSpec((None, tr, cw), lambda l, i: (l, i, 0)),
        compiler_params=_params(("arbitrary", "arbitrary")),
    )(*rcs)


def _sum_devices(gathered, name):
    n, R, _ = gathered.shape
    tr = _pick(R, (512, 256, 128, 64, 32, 16, 8))

    def body(g_ref, o_ref):
        acc = g_ref[0]
        for d in range(1, n):
            acc = acc + g_ref[d]
        o_ref[...] = acc

    return pl.pallas_call(
        body, name=name, out_shape=jax.ShapeDtypeStruct((R, LANES), F32), grid=(R // tr,),
        in_specs=[pl.BlockSpec((n, tr, LANES), lambda i: (0, i, 0))], out_specs=pl.BlockSpec((tr, LANES), lambda i: (i, 0)),
        compiler_params=_params(("parallel",)),
    )(gathered)


def _adamw(w, g, m, v, name):
    R, C = w.shape
    tr = _pick(R, (256, 128, 64, 32, 16, 8))
    c1 = 1.0 / (1.0 - ADAM_B1 ** ADAM_STEP)
    c2 = 1.0 / (1.0 - ADAM_B2 ** ADAM_STEP)

    def body(w_ref, g_ref, m_ref, v_ref, d_ref, mo_ref, vo_ref):
        gv = g_ref[...]
        m2 = ADAM_B1 * m_ref[...] + (1.0 - ADAM_B1) * gv
        v2 = ADAM_B2 * v_ref[...] + (1.0 - ADAM_B2) * (gv * gv)
        mo_ref[...] = m2
        vo_ref[...] = v2
        d_ref[...] = -ADAM_LR * ((m2 * c1) / (jnp.sqrt(v2 * c2) + ADAM_EPS) + ADAM_WD * w_ref[...])

    spec = pl.BlockSpec((tr, C), lambda i: (i, 0))
    shp = jax.ShapeDtypeStruct((R, C), F32)
    return pl.pallas_call(body, name=name, out_shape=(shp, shp, shp), grid=(R // tr,), in_specs=[spec] * 4,
                          out_specs=(spec, spec, spec), compiler_params=_params(("parallel",)))(w, g, m, v)


def _adamw_halves(w, own, recv, m, v, core, name):
    L, rh, cw = own.shape
    tr = _pick(rh, (256, 128, 176, 64))
    c1 = 1.0 / (1.0 - ADAM_B1 ** ADAM_STEP)
    c2 = 1.0 / (1.0 - ADAM_B2 ** ADAM_STEP)

    def body(core_ref, w_ref, own_ref, recv_ref, m_ref, v_ref, g_ref, d_ref, mo_ref, vo_ref):
        gv = jnp.where(pl.program_id(1) == core_ref[0], own_ref[...], recv_ref[...])
        g_ref[...] = gv
        m2 = ADAM_B1 * m_ref[...] + (1.0 - ADAM_B1) * gv
        v2 = ADAM_B2 * v_ref[...] + (1.0 - ADAM_B2) * (gv * gv)
        mo_ref[...] = m2
        vo_ref[...] = v2
        d_ref[...] = -ADAM_LR * ((m2 * c1) / (jnp.sqrt(v2 * c2) + ADAM_EPS) + ADAM_WD * w_ref[...])

    full = pl.BlockSpec((None, None, tr, cw), lambda l, hf, i, core_ref: (l, hf, i, 0))
    mine = pl.BlockSpec((None, tr, cw), lambda l, hf, i, core_ref: (l, jnp.where(hf == core_ref[0], i, 0), 0))
    other = pl.BlockSpec((None, tr, cw), lambda l, hf, i, core_ref: (l, jnp.where(hf == core_ref[0], 0, i), 0))
    shp = jax.ShapeDtypeStruct((L, 2, rh, cw), F32)
    view = lambda a: a.reshape(L, 2, rh, cw)
    outs = pl.pallas_call(
        body, name=name, out_shape=(shp, shp, shp, shp),
        grid_spec=pltpu.PrefetchScalarGridSpec(
            num_scalar_prefetch=1, grid=(L, 2, rh // tr), in_specs=[full, mine, other, full, full],
            out_specs=(full, full, full, full)),
        compiler_params=_params(("arbitrary", "arbitrary", "arbitrary")),
    )(core, view(w), own, recv, view(m), view(v))
    return tuple(o.reshape(L, 2 * rh, cw) for o in outs)


ANY = pl.BlockSpec(memory_space=pl.ANY)


def _position():
    x, y, c = lax.axis_index("x"), lax.axis_index("y"), lax.axis_index("c")
    return x, y, c


def _allgather(ins, out_shapes, src_fns, dst_fns, name, in_vmem):
    n = len(ins)

    def body(*refs):
        in_refs, out_refs = refs[:n], refs[n:2 * n]
        send_sems, recv_sems, local_sems = refs[2 * n:]
        x, y, c = _position()
        me, sibling = (x, y, c), (x, y, 1 - c)
        chips = [(1 - x, y), (x, 1 - y), (1 - x, 1 - y)]

        def copy(k, j, block, to, own=False):
            dst = dst_fns[k](out_refs[k], *block)
            return pltpu.make_async_remote_copy(
                src_ref=src_fns[k](in_refs[k], c) if own else dst, dst_ref=dst,
                send_sem=send_sems.at[k, j], recv_sem=recv_sems.at[k, j], device_id=to, device_id_type=MESH)

        mine = [pltpu.make_async_copy(src_fns[k](in_refs[k], c), dst_fns[k](out_refs[k], *me), local_sems.at[k])
                for k in range(n)]
        for cp in mine:
            cp.start()
        first = []
        for k in range(n):
            first.append(copy(k, 0, me, sibling, own=True))
            first += [copy(k, 1 + j, me, (*chip, c), own=True) for j, chip in enumerate(chips)]
        for cp in first:
            cp.start()
        passed = []
        for j, chip in enumerate(chips):
            for k in range(n):
                copy(k, 1 + j, (*chip, c), me).wait_recv()
                fwd = copy(k, 4 + j, (*chip, c), sibling)
                fwd.start()
                passed.append(fwd)
        for k in range(n):
            copy(k, 0, sibling, me).wait_recv()
        for j, chip in enumerate(chips):
            for k in range(n):
                copy(k, 4 + j, (*chip, 1 - c), me).wait_recv()
        for cp in first + passed:
            cp.wait_send()
        for cp in mine:
            cp.wait()

    spec = pl.BlockSpec(memory_space=pltpu.VMEM) if in_vmem else ANY
    return pl.pallas_call(
        body, name=name, out_shape=tuple(out_shapes), in_specs=[spec] * n, out_specs=tuple([spec] * n),
        scratch_shapes=[pltpu.SemaphoreType.DMA((n, 7)), pltpu.SemaphoreType.DMA((n, 7)),
                        pltpu.SemaphoreType.DMA((n,))],
        compiler_params=pltpu.CompilerParams(vmem_limit_bytes=VMEM_LIMIT_BYTES),
    )(*ins)


def _allgather_small(payload, name):
    R = payload.shape[0]
    (out,) = _allgather(
        [payload], [jax.ShapeDtypeStruct((N_DEV, R, LANES), F32)],
        [lambda ref, c: ref], [lambda ref, px, py, pc: ref.at[4 * px + 2 * py + pc]], name, in_vmem=True)
    return out


def _gather_weights(shards, col_sharded, name):
    src_fns, dst_fns, out_shapes = [], [], []
    for sh, col in zip(shards, col_sharded):
        L, r, cw = sh.shape
        rh = r // 2
        src_fns.append(lambda ref, c, rh=rh: ref.at[:, pl.ds(c * rh, rh), :])
        if col:
            out_shapes.append(jax.ShapeDtypeStruct((L, r, N_CHIPS * cw), sh.dtype))
            dst_fns.append(lambda ref, px, py, pc, rh=rh, cw=cw:
                           ref.at[:, pl.ds(pc * rh, rh), pl.ds((2 * px + py) * cw, cw)])
        else:
            out_shapes.append(jax.ShapeDtypeStruct((L, N_CHIPS * r, cw), sh.dtype))
            dst_fns.append(lambda ref, px, py, pc, rh=rh, r=r:
                           ref.at[:, pl.ds((2 * px + py) * r + pc * rh, rh), :])
    return _allgather(list(shards), out_shapes, src_fns, dst_fns, name, in_vmem=False)


def _swap_sibling(ins, name):
    n = len(ins)

    def body(*refs):
        in_refs, out_refs = refs[:n], refs[n:2 * n]
        send_sems, recv_sems = refs[2 * n:]
        x, y, c = _position()
        copies = [pltpu.make_async_remote_copy(
            src_ref=in_refs[k], dst_ref=out_refs[k], send_sem=send_sems.at[k], recv_sem=recv_sems.at[k],
            device_id=(x, y, 1 - c), device_id_type=MESH) for k in range(n)]
        for cp in copies:
            cp.start()
        for cp in copies:
            cp.wait_recv()
        for cp in copies:
            cp.wait_send()

    return pl.pallas_call(
        body, name=name, out_shape=tuple(jax.ShapeDtypeStruct(a.shape, a.dtype) for a in ins),
        in_specs=[ANY] * n, out_specs=tuple([ANY] * n),
        scratch_shapes=[pltpu.SemaphoreType.DMA((n,)), pltpu.SemaphoreType.DMA((n,))],
        compiler_params=pltpu.CompilerParams(vmem_limit_bytes=VMEM_LIMIT_BYTES),
    )(*ins)


def _exchange_chips(ins, name):
    n = len(ins)

    def body(*refs):
        in_refs, out_refs = refs[:n], refs[n:2 * n]
        send_sems, recv_sems, local_sems = refs[2 * n:]
        x, y, c = _position()
        my_chip = 2 * x + y
        chips = [(1 - x, y), (x, 1 - y), (1 - x, 1 - y)]
        local = [pltpu.make_async_copy(in_refs[k].at[my_chip], out_refs[k].at[my_chip], local_sems.at[k])
                 for k in range(n)]
        for cp in local:
            cp.start()
        sends = []
        for j, (px, py) in enumerate(chips):
            for k in range(n):
                sends.append(pltpu.make_async_remote_copy(
                    src_ref=in_refs[k].at[2 * px + py], dst_ref=out_refs[k].at[my_chip],
                    send_sem=send_sems.at[k, j], recv_sem=recv_sems.at[k, j],
                    device_id=(px, py, c), device_id_type=MESH))
        for cp in sends:
            cp.start()
        for j, (px, py) in enumerate(chips):
            for k in range(n):
                pltpu.make_async_remote_copy(
                    src_ref=in_refs[k].at[2 * px + py], dst_ref=out_refs[k].at[2 * px + py],
                    send_sem=send_sems.at[k, j], recv_sem=recv_sems.at[k, j],
                    device_id=(px, py, c), device_id_type=MESH).wait_recv()
        for cp in sends:
            cp.wait_send()
        for cp in local:
            cp.wait()

    return pl.pallas_call(
        body, name=name, out_shape=tuple(jax.ShapeDtypeStruct(a.shape, a.dtype) for a in ins),
        in_specs=[ANY] * n, out_specs=tuple([ANY] * n),
        scratch_shapes=[pltpu.SemaphoreType.DMA((n, 3)), pltpu.SemaphoreType.DMA((n, 3)),
                        pltpu.SemaphoreType.DMA((n,))],
        compiler_params=pltpu.CompilerParams(vmem_limit_bytes=VMEM_LIMIT_BYTES),
    )(*ins)


def _vec(a):
    return a.reshape(1, -1)


def _local_step(x, tgt, mod, W, P):
    Dm = D_MODEL
    G = {k: [] for k in ("gm_w_in", "gm_w_out", "hg_w_in", "hg_w_out", "ffn_w_up", "ffn_w_down")}
    lb_all = _lb_fwd(P["hg_lb"], "lb_fwd")
    saved = []
    xs = x
    y_prev = gate_prev = None
    for i in range(DEPTH):
        m = [_vec(mod[i, j * Dm:(j + 1) * Dm]) for j in range(6)]
        sh1, sc1, g1, sh2, sc2, g2 = m
        j = i // 2
        xs, h = _norm_fwd(xs, y_prev, gate_prev, _vec(P["norm_g"][i, 0]), sc1, sh1, f"norm_fwd_a{i}")
        rec = dict(x1=xs, h1=h)
        if i % 2 == 0:
            a = _mm(h, W["gm_w_in"], "nn", f"gm_in{i}", b_layer=j)
            gated = _gm_mid_fwd(a, _vec(P["gm_ln_g"][j]), _vec(P["gm_ln_b"][j]), P["gm_w_s"][j],
                                P["gm_b_s"][j].reshape(GM_HEADS, GM_BLOCK, 1), f"gm_mid_fwd{i}")
            y1 = _mm(gated, W["gm_w_out"], "nn", f"gm_out{i}", b_layer=j)
            rec.update(a=a, act=gated)
        else:
            p = _mm(h, W["hg_w_in"], "nn", f"hg_in{i}", b_layer=j)
            o, og, states = _hg_scan_fwd(p, _vec(lb_all[j]), _vec(P["hg_gn_g"][j]), f"hg_scan_fwd{i}")
            y1 = _mm(og, W["hg_w_out"], "nn", f"hg_out{i}", b_layer=j)
            rec.update(a=p, act=og, o=o, states=states)
        rec["y1"] = y1
        xs, h2 = _norm_fwd(xs, y1, g1, _vec(P["norm_g"][i, 1]), sc2, sh2, f"norm_fwd_b{i}")
        a2 = _mm(h2, W["ffn_w_up"], "nn", f"ffn_up{i}", b_layer=i)
        mm_ = _conv_fwd(a2, P["ffn_conv_w"][i], _vec(P["ffn_conv_b"][i]), f"conv_fwd{i}")
        y2 = _mm(mm_, W["ffn_w_down"], "nn", f"ffn_down{i}", b_layer=i)
        rec.update(x2=xs, h2=h2, a2=a2, m=mm_, y2=y2, mods=m)
        saved.append(rec)
        y_prev, gate_prev = y2, g2
    dx, dy, loss, s_fg, s_gate = _loss_head(xs, y_prev, gate_prev, _vec(P["final_g"]), tgt, "loss_head")
    small = dict(final_g=s_fg, norm_g=[None] * DEPTH, dmod=[None] * DEPTH, ffn_conv_w=[None] * DEPTH,
                 ffn_conv_b=[None] * DEPTH, gm_ln_g=[None] * 2, gm_ln_b=[None] * 2, gm_w_s=[None] * 2,
                 gm_b_s=[None] * 2, hg_gn_g=[None] * 2, dlb=[None] * 2)
    for i in reversed(range(DEPTH)):
        rec = saved[i]
        sh1, sc1, g1, sh2, sc2, g2 = rec["mods"]
        j = i // 2
        d_g2 = s_gate
        dm = _mm(dy, W["ffn_w_down"], "nt", f"ffn_down_dx{i}", b_layer=i)
        G["ffn_w_down"].append(_mm(rec["m"], dy, "tn", f"ffn_down_dw{i}"))
        da2, dcw, dcb = _conv_bwd(rec["a2"], dm, P["ffn_conv_w"][i], _vec(P["ffn_conv_b"][i]), f"conv_bwd{i}")
        small["ffn_conv_w"][i], small["ffn_conv_b"][i] = dcw, dcb
        dh2 = _mm(da2, W["ffn_w_up"], "nt", f"ffn_up_dx{i}", b_layer=i)
        G["ffn_w_up"].append(_mm(rec["h2"], da2, "tn", f"ffn_up_dw{i}", exchange_out=True))
        ng2 = _vec(P["norm_g"][i, 1])
        dx, dy, s_sh2, s_x2, d_g1 = _norm_bwd(rec["x2"], dh2, dx, ng2, sc2, rec["y1"], g1, f"norm_bwd_b{i}")
        d_sc2, d_ng2 = s_x2 * ng2, s_x2 * (1.0 + sc2)
        if i % 2 == 0:
            dgated = _mm(dy, W["gm_w_out"], "nt", f"gm_out_dx{i}", b_layer=j)
            G["gm_w_out"].append(_mm(rec["act"], dy, "tn", f"gm_out_dw{i}"))
            da, dws, dbs, dlg, dlbeta = _gm_mid_bwd(
                rec["a"], dgated, _vec(P["gm_ln_g"][j]), _vec(P["gm_ln_b"][j]), P["gm_w_s"][j],
                P["gm_b_s"][j].reshape(GM_HEADS, GM_BLOCK, 1), f"gm_mid_bwd{i}")
            small["gm_w_s"][j], small["gm_b_s"][j] = dws, dbs[:, :GM_HEADS].T
            small["gm_ln_g"][j], small["gm_ln_b"][j] = dlg, dlbeta
            dh1 = _mm(da, W["gm_w_in"], "nt", f"gm_in_dx{i}", b_layer=j)
            G["gm_w_in"].append(_mm(rec["h1"], da, "tn", f"gm_in_dw{i}", exchange_out=True))
        else:
            dog = _mm(dy, W["hg_w_out"], "nt", f"hg_out_dx{i}", b_layer=j)
            G["hg_w_out"].append(_mm(rec["act"], dy, "tn", f"hg_out_dw{i}"))
            dp, dlb, dgn = _hg_scan_bwd(rec["a"], _vec(lb_all[j]), _vec(P["hg_gn_g"][j]), rec["o"], dog,
                                        rec["states"], f"hg_scan_bwd{i}")
            small["dlb"][j], small["hg_gn_g"][j] = dlb, dgn
            dh1 = _mm(dp, W["hg_w_in"], "nt", f"hg_in_dx{i}", b_layer=j)
            G["hg_w_in"].append(_mm(rec["h1"], dp, "tn", f"hg_in_dw{i}", exchange_out=True))
        ng1 = _vec(P["norm_g"][i, 0])
        if i > 0:
            prev = saved[i - 1]
            dx, dy, s_sh1, s_x1, s_gate = _norm_bwd(rec["x1"], dh1, dx, ng1, sc1, prev["y2"], prev["mods"][5],
                                                    f"norm_bwd_a{i}")
        else:
            dx, s_sh1, s_x1 = _norm_bwd(rec["x1"], dh1, dx, ng1, sc1, None, None, f"norm_bwd_a{i}")
        d_sc1, d_ng1 = s_x1 * ng1, s_x1 * (1.0 + sc1)
        small["norm_g"][i] = jnp.concatenate([d_ng1, d_ng2], axis=0)
        small["dmod"][i] = jnp.concatenate([s_sh1, d_sc1, d_g1, s_sh2, d_sc2, d_g2], axis=1)
    for k in G:
        G[k] = G[k][::-1]
    dlb_all = jnp.concatenate(small.pop("dlb"), axis=0)
    small["hg_lb"] = _lb_bwd(P["hg_lb"], dlb_all, "lb_bwd")
    return loss, dx, G, small


BIG = ("gm_w_in", "gm_w_out", "hg_w_in", "hg_w_out", "ffn_w_up", "ffn_w_down")
COL_SHARDED = dict(gm_w_in=True, gm_w_out=False, hg_w_in=True, hg_w_out=False, ffn_w_up=True, ffn_w_down=False)


def _pack(pieces):
    flat = [p.reshape(-1).astype(F32) for p in pieces]
    offs, tot = [], 0
    for f in flat:
        offs.append((tot, f.shape[0]))
        tot += f.shape[0]
    padded = -(-tot // (8 * LANES)) * (8 * LANES)
    if padded > tot:
        flat.append(jnp.zeros((padded - tot,), F32))
    return jnp.concatenate(flat).reshape(-1, LANES), offs


def _unpack(rows, offs, shapes):
    lead = rows.shape[:-2]
    flat = rows.reshape(lead + (-1,))
    return [flat[..., o:o + n].reshape(lead + tuple(s)) for (o, n), s in zip(offs, shapes)]


def _from_chips(per_dev, axis):
    per_chip = per_dev[0::2]
    return jnp.concatenate([per_chip[s] for s in range(N_CHIPS)], axis=axis)


def kernel(x, c, gm_w_in, gm_ln_g, gm_ln_b, gm_w_s, gm_b_s, gm_w_out, hg_w_in, hg_lb, hg_gn_g, hg_w_out, ffn_w_up, ffn_conv_w, ffn_conv_b, ffn_w_down, norm_g, ada_w, ada_b, final_g, loss_target, m_gm_w_in, m_gm_ln_g, m_gm_ln_b, m_gm_w_s, m_gm_b_s, m_gm_w_out, m_hg_w_in, m_hg_lb, m_hg_gn_g, m_hg_w_out, m_ffn_w_up, m_ffn_conv_w, m_ffn_conv_b, m_ffn_w_down, m_norm_g, m_ada_w, m_ada_b, m_final_g, v_gm_w_in, v_gm_ln_g, v_gm_ln_b, v_gm_w_s, v_gm_b_s, v_gm_w_out, v_hg_w_in, v_hg_lb, v_hg_gn_g, v_hg_w_out, v_ffn_w_up, v_ffn_conv_w, v_ffn_conv_b, v_ffn_w_down, v_norm_g, v_ada_w, v_ada_b, v_final_g):
    Dm = D_MODEL
    xi, yi, ci = _position()
    chip = 2 * xi + yi
    dev = 4 * xi + 2 * yi + ci
    weights = dict(gm_w_in=gm_w_in, gm_ln_g=gm_ln_g, gm_ln_b=gm_ln_b, gm_w_s=gm_w_s, gm_b_s=gm_b_s,
                   gm_w_out=gm_w_out, hg_w_in=hg_w_in, hg_lb=hg_lb, hg_gn_g=hg_gn_g, hg_w_out=hg_w_out,
                   ffn_w_up=ffn_w_up, ffn_conv_w=ffn_conv_w, ffn_conv_b=ffn_conv_b, ffn_w_down=ffn_w_down,
                   norm_g=norm_g, ada_w=ada_w, ada_b=ada_b, final_g=final_g)
    mom_m = dict(gm_w_in=m_gm_w_in, gm_ln_g=m_gm_ln_g, gm_ln_b=m_gm_ln_b, gm_w_s=m_gm_w_s, gm_b_s=m_gm_b_s,
                 gm_w_out=m_gm_w_out, hg_w_in=m_hg_w_in, hg_lb=m_hg_lb, hg_gn_g=m_hg_gn_g, hg_w_out=m_hg_w_out,
                 ffn_w_up=m_ffn_w_up, ffn_conv_w=m_ffn_conv_w, ffn_conv_b=m_ffn_conv_b, ffn_w_down=m_ffn_w_down,
                 norm_g=m_norm_g, ada_w=m_ada_w, ada_b=m_ada_b, final_g=m_final_g)
    mom_v = dict(gm_w_in=v_gm_w_in, gm_ln_g=v_gm_ln_g, gm_ln_b=v_gm_ln_b, gm_w_s=v_gm_w_s, gm_b_s=v_gm_b_s,
                 gm_w_out=v_gm_w_out, hg_w_in=v_hg_w_in, hg_lb=v_hg_lb, hg_gn_g=v_hg_gn_g, hg_w_out=v_hg_w_out,
                 ffn_w_up=v_ffn_w_up, ffn_conv_w=v_ffn_conv_w, ffn_conv_b=v_ffn_conv_b, ffn_w_down=v_ffn_w_down,
                 norm_g=v_norm_g, ada_w=v_ada_w, ada_b=v_ada_b, final_g=v_final_g)
    order = list(weights)

    pieces = [c, hg_lb, hg_gn_g, norm_g, ffn_conv_w]
    payload, offs = _pack(pieces)
    got = _allgather_small(payload, "gather_small")
    c_g, lb_g, gn_g, ng_g, cw_g = _unpack(got, offs, [p.shape for p in pieces])
    c_all = c_g.reshape(N_DEV, Dm)
    P = dict(hg_lb=_from_chips(lb_g, 1), hg_gn_g=_from_chips(gn_g, 1), norm_g=_from_chips(ng_g, 2),
             ffn_conv_w=_from_chips(cw_g, 2), gm_ln_g=gm_ln_g, gm_ln_b=gm_ln_b, gm_w_s=gm_w_s, gm_b_s=gm_b_s,
             ffn_conv_b=ffn_conv_b, final_g=final_g)

    cols = ada_w.shape[2]
    ada_b_sh = lax.dynamic_slice_in_dim(ada_b, chip * cols, cols, axis=1)
    mod_sh = _ada_fwd(c_all, ada_w, ada_b_sh, "ada_fwd")
    mod_g = _allgather_small(mod_sh.reshape(-1, LANES), "gather_mod").reshape(N_DEV, DEPTH, N_DEV, cols)
    mod_mine = lax.dynamic_index_in_dim(mod_g[0::2], dev, axis=2, keepdims=False)
    mod = jnp.transpose(mod_mine, (1, 0, 2)).reshape(DEPTH, N_CHIPS * cols)

    shards = [weights[k].astype(BF16) for k in BIG]
    full = _gather_weights(shards, [COL_SHARDED[k] for k in BIG], "gather_weights")
    W = dict(zip(BIG, full))

    loss_part, dx, G, small = _local_step(x[0], loss_target[0], mod, W, P)

    g4, names = [], []
    for k in BIG:
        for l, g in enumerate(G[k]):
            if not COL_SHARDED[k]:
                R, C = g.shape
                g = g.reshape(N_CHIPS, 2, R // (2 * N_CHIPS), C)
            g4.append(g)
            names.append(f"{k}{l}")
    other = 1 - ci
    to_sib = [lax.dynamic_index_in_dim(g, other, axis=1, keepdims=False).astype(BF16) for g in g4]
    from_sib = _swap_sibling(to_sib, "reduce_swap")
    core = jnp.reshape(ci, (1,)).astype(jnp.int32)
    chip_sums = [_add_own_half(g, r, core, f"chip_sum_{nm}") for g, r, nm in zip(g4, from_sib, names)]
    by_src = _exchange_chips(chip_sums, "reduce_exchange")
    own_halves, pos = [], 0
    for k in BIG:
        own_halves.append(_sum_chips(by_src[pos:pos + len(G[k])], f"sum_chips_{k}"))
        pos += len(G[k])
    sib_halves = _swap_sibling(own_halves, "reduce_join")
    grads, deltas, new_m, new_v = {}, {}, {}, {}
    for k, own, recv in zip(BIG, own_halves, sib_halves):
        grads[k], deltas[k], new_m[k], new_v[k] = _adamw_halves(
            weights[k], own, recv, mom_m[k], mom_v[k], core, f"adamw_{k}")

    sum_pieces = [loss_part[:, :1], small["final_g"], jnp.stack(small["gm_ln_g"]), jnp.stack(small["gm_ln_b"]),
                  jnp.stack(small["gm_w_s"]), jnp.stack(small["gm_b_s"]), jnp.stack(small["ffn_conv_b"]),
                  small["hg_lb"], jnp.stack(small["hg_gn_g"]), jnp.stack(small["norm_g"]),
                  jnp.stack(small["ffn_conv_w"])]
    dmod = jnp.concatenate(small["dmod"], axis=0)
    payload2, offs2 = _pack(sum_pieces + [dmod])
    got2 = _allgather_small(payload2, "gather_grads")
    dmod_all = _unpack(got2, offs2[-1:], [dmod.shape])[0]
    summed = _sum_devices(got2, "sum_devices")
    (loss_s, d_final_g, d_ln_g, d_ln_b, d_ws, d_bs, d_cb, d_lb, d_gn, d_ng, d_cw) = _unpack(
        summed, offs2[:-1], [(1,), final_g.shape, gm_ln_g.shape, gm_ln_b.shape, gm_w_s.shape, gm_b_s.shape,
                             ffn_conv_b.shape, (2, Dm), (2, Dm), (DEPTH, 2, Dm), (DEPTH, 3, 2 * FFN_HIDDEN)])
    grads.update(final_g=d_final_g, gm_ln_g=d_ln_g, gm_ln_b=d_ln_b, gm_w_s=d_ws, gm_b_s=d_bs, ffn_conv_b=d_cb)
    grads["hg_lb"] = lax.dynamic_slice_in_dim(d_lb, chip * hg_lb.shape[1], hg_lb.shape[1], axis=1)
    grads["hg_gn_g"] = lax.dynamic_slice_in_dim(d_gn, chip * hg_gn_g.shape[1], hg_gn_g.shape[1], axis=1)
    grads["norm_g"] = lax.dynamic_slice_in_dim(d_ng, chip * norm_g.shape[2], norm_g.shape[2], axis=2)
    grads["ffn_conv_w"] = lax.dynamic_slice_in_dim(d_cw, chip * ffn_conv_w.shape[2], ffn_conv_w.shape[2], axis=2)
    dmod_sh = lax.dynamic_slice_in_dim(dmod_all, chip * cols, cols, axis=2)
    grads["ada_w"] = _ada_bwd(c_all, jnp.transpose(dmod_sh, (1, 0, 2)), "ada_bwd")
    grads["ada_b"] = _sum_devices(dmod_all.reshape(N_DEV, -1, LANES), "sum_ada_b").reshape(ada_b.shape)

    for k in order:
        if k in BIG:
            continue
        w = weights[k]
        shp = w.shape
        view = (-1, shp[-1]) if w.ndim > 1 else (8, -1)
        d, m2, v2 = _adamw(w.reshape(view), grads[k].reshape(view), mom_m[k].reshape(view), mom_v[k].reshape(view),
                           f"adamw_{k}")
        deltas[k], new_m[k], new_v[k] = d.reshape(shp), m2.reshape(shp), v2.reshape(shp)
        grads[k] = grads[k].reshape(shp)

    loss = loss_s.reshape(())
    return (loss, dx[None], *[grads[k] for k in order], *[deltas[k] for k in order],
            *[new_m[k] for k in order], *[new_v[k] for k in order])
```

```python
import functools

import jax
import jax.numpy as jnp
from jax import lax
from jax.experimental import pallas as pl
from jax.experimental.pallas import tpu as pltpu

F32 = jnp.float32
BF16 = jnp.bfloat16
HI = lax.Precision.HIGHEST
X3 = lax.Precision.HIGH
MESH = pl.DeviceIdType.MESH

D_MODEL = 1024
DEPTH = 4
EPS = 1e-6
GM_WIDTH = 2048
GM_HEADS = 8
GM_BLOCK = 128
GM_HEAD_DIM = 256
CHUNK = 64
HG_HEADS = 8
HG_DIM = 128
FFN_HIDDEN = 2816
N_CHIPS = 4
N_DEV = 8

ADAM_LR = 0.001
ADAM_B1 = 0.9
ADAM_B2 = 0.999
ADAM_EPS = 1e-08
ADAM_WD = 0.01
ADAM_STEP = 10

VMEM_LIMIT_BYTES = 56 * 1024 * 1024
ROW_TILE = 256
LANES = 128

_SQRT_HALF = 0.7071067811865476
_INV_SQRT_2PI = 0.3989422804014327


def _pick(dim, prefs):
    for p in prefs:
        if dim % p == 0:
            return p
    return dim


def _params(sem):
    return pltpu.CompilerParams(dimension_semantics=sem, vmem_limit_bytes=VMEM_LIMIT_BYTES)


def _cdf(x):
    return 0.5 * (1.0 + lax.erf(x * _SQRT_HALF))


def _pdf(x):
    return jnp.exp(-0.5 * x * x) * _INV_SQRT_2PI


def _sig(x):
    return jax.nn.sigmoid(x)


def _dot(a, b, dims, prec=None):
    return lax.dot_general(a, b, (dims, ((), ())), precision=prec, preferred_element_type=F32)


NN = ((1,), (0,))
NT = ((1,), (1,))
TN = ((0,), (0,))


def _mm(a, b, mode, name, b_layer=None, out_dtype=F32, exchange_out=False):
    b2 = b.shape[-2:]
    if mode == "nn":
        (M, K), (_, N) = a.shape, b2
    elif mode == "nt":
        (M, K), (N, _) = a.shape, b2
    else:
        (K, M), (_, N) = a.shape, b2
    tm = _pick(M, (1408, 512, 256, 128) if mode == "tn" else (1024, 512, 256, 128))
    tn = _pick(N, (1408, 1024, 512, 256, 128))
    tk = _pick(K, (1408, 1024, 512, 256, 128))
    nk = K // tk
    dims = {"nn": NN, "nt": NT, "tn": TN}[mode]

    def body(a_ref, b_ref, o_ref, *scratch):
        part = _dot(a_ref[...].astype(BF16), b_ref[...].astype(BF16), dims)
        if nk == 1:
            o_ref[...] = part.astype(o_ref.dtype)
            return
        (acc_ref,) = scratch
        k = pl.program_id(2)

        @pl.when(k == 0)
        def _():
            acc_ref[...] = part

        @pl.when(k > 0)
        def _():
            acc_ref[...] += part

        @pl.when(k == nk - 1)
        def _():
            o_ref[...] = acc_ref[...].astype(o_ref.dtype)

    if mode == "tn":
        a_spec = pl.BlockSpec((tk, tm), lambda i, j, k: (k, i))
    else:
        a_spec = pl.BlockSpec((tm, tk), lambda i, j, k: (i, k))
    bblk = (tk, tn) if mode in ("nn", "tn") else (tn, tk)
    bidx = (lambda i, j, k: (k, j)) if mode in ("nn", "tn") else (lambda i, j, k: (j, k))
    if b_layer is None:
        b_spec = pl.BlockSpec(bblk, bidx)
    else:
        b_spec = pl.BlockSpec((None,) + bblk, lambda i, j, k: (b_layer,) + bidx(i, j, k))
    if exchange_out:
        mh, cw = M // 2, N // N_CHIPS
        assert mh % tm == 0 and cw % tn == 0
        out_shape = jax.ShapeDtypeStruct((N_CHIPS, 2, mh, cw), out_dtype)
        o_spec = pl.BlockSpec(
            (None, None, tm, tn),
            lambda i, j, k: (j // (cw // tn), i // (mh // tm), i % (mh // tm), j % (cw // tn)))
    else:
        out_shape = jax.ShapeDtypeStruct((M, N), out_dtype)
        o_spec = pl.BlockSpec((tm, tn), lambda i, j, k: (i, j))
    return pl.pallas_call(
        body, name=name, out_shape=out_shape, grid=(M // tm, N // tn, nk),
        in_specs=[a_spec, b_spec], out_specs=o_spec,
        scratch_shapes=[] if nk == 1 else [pltpu.VMEM((tm, tn), F32)],
        compiler_params=_params(("parallel", "parallel", "arbitrary")),
    )(a, b)


def _row_spec(tr, width):
    return pl.BlockSpec((tr, width), lambda i: (i, 0))


def _vec_spec(width, rows=1):
    return pl.BlockSpec((rows, width), lambda i: (0, 0))


def _norm_fwd(x, y, gate, g, sc, sh, name):
    T, Dm = x.shape
    tr = _pick(T, (ROW_TILE,))
    has_res = y is not None

    def body(*refs):
        if has_res:
            x_ref, y_ref, gate_ref, g_ref, sc_ref, sh_ref, xo_ref, h_ref = refs
            xv = x_ref[...] + gate_ref[...] * y_ref[...]
            xo_ref[...] = xv
        else:
            x_ref, g_ref, sc_ref, sh_ref, h_ref = refs
            xv = x_ref[...]
        rstd = lax.rsqrt(jnp.mean(xv * xv, axis=-1, keepdims=True) + EPS)
        h_ref[...] = ((xv * rstd * g_ref[...]) * (1.0 + sc_ref[...]) + sh_ref[...]).astype(BF16)

    row, vec = _row_spec(tr, Dm), _vec_spec(Dm)
    if has_res:
        ins, in_specs = (x, y, gate, g, sc, sh), [row, row, vec, vec, vec, vec]
        out_shape = (jax.ShapeDtypeStruct((T, Dm), F32), jax.ShapeDtypeStruct((T, Dm), BF16))
        out_specs = (row, row)
    else:
        ins, in_specs = (x, g, sc, sh), [row, vec, vec, vec]
        out_shape = jax.ShapeDtypeStruct((T, Dm), BF16)
        out_specs = row
    out = pl.pallas_call(body, name=name, out_shape=out_shape, grid=(T // tr,), in_specs=in_specs,
                         out_specs=out_specs, compiler_params=_params(("parallel",)))(*ins)
    return out if has_res else (x, out)


def _norm_bwd(x, dh, dxo, g, sc, y_prev, gate_prev, name):
    T, Dm = x.shape
    tr = _pick(T, (ROW_TILE,))
    has_prev = y_prev is not None

    def body(*refs):
        if has_prev:
            x_ref, dh_ref, dxo_ref, g_ref, sc_ref, yp_ref, gp_ref, dx_ref, dyp_ref, s1_ref, s2_ref, s3_ref = refs
        else:
            x_ref, dh_ref, dxo_ref, g_ref, sc_ref, dx_ref, s1_ref, s2_ref = refs

        @pl.when(pl.program_id(0) == 0)
        def _():
            s1_ref[...] = jnp.zeros_like(s1_ref)
            s2_ref[...] = jnp.zeros_like(s2_ref)
            if has_prev:
                s3_ref[...] = jnp.zeros_like(s3_ref)

        xv = x_ref[...]
        rstd = lax.rsqrt(jnp.mean(xv * xv, axis=-1, keepdims=True) + EPS)
        xhat = xv * rstd
        dh = dh_ref[...]
        dxhat = dh * (g_ref[...] * (1.0 + sc_ref[...]))
        dx = dxo_ref[...] + rstd * (dxhat - xhat * jnp.mean(dxhat * xhat, axis=-1, keepdims=True))
        dx_ref[...] = dx
        s1_ref[...] += jnp.sum(dh, axis=0, keepdims=True)
        s2_ref[...] += jnp.sum(dh * xhat, axis=0, keepdims=True)
        if has_prev:
            dyp_ref[...] = (gp_ref[...] * dx).astype(BF16)
            s3_ref[...] += jnp.sum(dx * yp_ref[...], axis=0, keepdims=True)

    row, vec = _row_spec(tr, Dm), _vec_spec(Dm)
    vshape = jax.ShapeDtypeStruct((1, Dm), F32)
    if has_prev:
        ins, in_specs = (x, dh, dxo, g, sc, y_prev, gate_prev), [row, row, row, vec, vec, row, vec]
        out_shape = (jax.ShapeDtypeStruct((T, Dm), F32), jax.ShapeDtypeStruct((T, Dm), BF16), vshape, vshape, vshape)
        out_specs = (row, row, vec, vec, vec)
    else:
        ins, in_specs = (x, dh, dxo, g, sc), [row, row, row, vec, vec]
        out_shape = (jax.ShapeDtypeStruct((T, Dm), F32), vshape, vshape)
        out_specs = (row, vec, vec)
    return pl.pallas_call(body, name=name, out_shape=out_shape, grid=(T // tr,), in_specs=in_specs,
                          out_specs=out_specs, compiler_params=_params(("arbitrary",)))(*ins)


def _loss_head(x, y, gate, fg, tgt, name):
    T, Dm = x.shape
    tr = _pick(T, (ROW_TILE,))
    nsteps = T // tr

    def body(x_ref, y_ref, gate_ref, fg_ref, t_ref, dx_ref, dy_ref, loss_ref, sfg_ref, sg_ref, acc_ref):
        i = pl.program_id(0)

        @pl.when(i == 0)
        def _():
            acc_ref[...] = jnp.zeros_like(acc_ref)
            sfg_ref[...] = jnp.zeros_like(sfg_ref)
            sg_ref[...] = jnp.zeros_like(sg_ref)

        yv = y_ref[...]
        xv = x_ref[...] + gate_ref[...] * yv
        rstd = lax.rsqrt(jnp.mean(xv * xv, axis=-1, keepdims=True) + EPS)
        xhat = xv * rstd
        err = xhat * fg_ref[...] - t_ref[...]
        acc_ref[...] += jnp.sum(err * err, axis=0, keepdims=True)
        dyn = err * (1.0 / Dm)
        sfg_ref[...] += jnp.sum(dyn * xhat, axis=0, keepdims=True)
        dxhat = dyn * fg_ref[...]
        dx = rstd * (dxhat - xhat * jnp.mean(dxhat * xhat, axis=-1, keepdims=True))
        dx_ref[...] = dx
        dy_ref[...] = (gate_ref[...] * dx).astype(BF16)
        sg_ref[...] += jnp.sum(dx * yv, axis=0, keepdims=True)

        @pl.when(i == nsteps - 1)
        def _():
            total = jnp.sum(acc_ref[...], axis=1, keepdims=True) * (0.5 / Dm)
            loss_ref[...] = jnp.broadcast_to(total, loss_ref.shape)

    row, vec = _row_spec(tr, Dm), _vec_spec(Dm)
    vshape = jax.ShapeDtypeStruct((1, Dm), F32)
    return pl.pallas_call(
        body, name=name, grid=(nsteps,),
        out_shape=(jax.ShapeDtypeStruct((T, Dm), F32), jax.ShapeDtypeStruct((T, Dm), BF16),
                   jax.ShapeDtypeStruct((1, LANES), F32), vshape, vshape),
        in_specs=[row, row, vec, vec, row], out_specs=(row, row, _vec_spec(LANES), vec, vec),
        scratch_shapes=[pltpu.VMEM((1, Dm), F32)], compiler_params=_params(("arbitrary",)),
    )(x, y, gate, fg, tgt)


def _spatial_mask():
    r = lax.broadcasted_iota(jnp.int32, (GM_BLOCK, GM_BLOCK), 0) // CHUNK
    c = lax.broadcasted_iota(jnp.int32, (GM_BLOCK, GM_BLOCK), 1) // CHUNK
    return r >= c


def _gm_specs(tr):
    return [_row_spec(tr, 2 * GM_WIDTH), _vec_spec(GM_WIDTH), _vec_spec(GM_WIDTH),
            pl.BlockSpec((GM_HEADS, GM_BLOCK, GM_BLOCK), lambda i: (0, 0, 0)),
            pl.BlockSpec((GM_HEADS, GM_BLOCK, 1), lambda i: (0, 0, 0))]


def _gm_mid_fwd(a, ln_g, ln_b, ws, bs3, name):
    T = a.shape[0]
    tr = _pick(T, (ROW_TILE,))
    W = GM_WIDTH

    def body(a_ref, lg_ref, lb_ref, ws_ref, bs_ref, o_ref, vn_scr):
        av = a_ref[:, W:]
        v = av * _cdf(av)
        vc = v - jnp.mean(v, axis=-1, keepdims=True)
        rstd = lax.rsqrt(jnp.mean(vc * vc, axis=-1, keepdims=True) + EPS)
        vn_scr[...] = (vc * rstd * lg_ref[...] + lb_ref[...]).astype(BF16)
        mask = _spatial_mask()
        for h in range(GM_HEADS):
            w = jnp.where(mask, ws_ref[h], 0.0).astype(BF16)
            cs = slice(h * GM_HEAD_DIM, (h + 1) * GM_HEAD_DIM)
            for blk in range(tr // GM_BLOCK):
                rs = slice(blk * GM_BLOCK, (blk + 1) * GM_BLOCK)
                s = _dot(w, vn_scr[rs, cs], NN) + bs_ref[h]
                au = a_ref[rs, cs]
                o_ref[rs, cs] = (au * _cdf(au) * s).astype(BF16)

    return pl.pallas_call(
        body, name=name, out_shape=jax.ShapeDtypeStruct((T, W), BF16), grid=(T // tr,),
        in_specs=_gm_specs(tr), out_specs=_row_spec(tr, W),
        scratch_shapes=[pltpu.VMEM((tr, W), BF16)], compiler_params=_params(("parallel",)),
    )(a, ln_g, ln_b, ws, bs3)


def _gm_mid_bwd(a, dgated, ln_g, ln_b, ws, bs3, name):
    T = a.shape[0]
    tr = _pick(T, (ROW_TILE,))
    W = GM_WIDTH
    nsteps = T // tr

    def body(a_ref, dg_ref, lg_ref, lb_ref, ws_ref, bs_ref, da_ref, dws_ref, dbs_ref, dlg_ref, dlb_ref,
             vn_scr, vhat_scr, dvn_scr, dsum_scr):
        i = pl.program_id(0)

        @pl.when(i == 0)
        def _():
            dws_ref[...] = jnp.zeros_like(dws_ref)
            dbs_ref[...] = jnp.zeros_like(dbs_ref)
            dlg_ref[...] = jnp.zeros_like(dlg_ref)
            dlb_ref[...] = jnp.zeros_like(dlb_ref)
            dsum_scr[...] = jnp.zeros_like(dsum_scr)

        av = a_ref[:, W:]
        cdf_v = _cdf(av)
        v = av * cdf_v
        vc = v - jnp.mean(v, axis=-1, keepdims=True)
        rstd = lax.rsqrt(jnp.mean(vc * vc, axis=-1, keepdims=True) + EPS)
        vhat_scr[...] = vc * rstd
        vn_scr[...] = (vhat_scr[...] * lg_ref[...] + lb_ref[...]).astype(BF16)
        mask = _spatial_mask()
        for h in range(GM_HEADS):
            w = jnp.where(mask, ws_ref[h], 0.0).astype(BF16)
            cs = slice(h * GM_HEAD_DIM, (h + 1) * GM_HEAD_DIM)
            for blk in range(tr // GM_BLOCK):
                rs = slice(blk * GM_BLOCK, (blk + 1) * GM_BLOCK)
                vnb = vn_scr[rs, cs]
                s = _dot(w, vnb, NN) + bs_ref[h]
                au = a_ref[rs, cs]
                cdf_u = _cdf(au)
                dg = dg_ref[rs, cs]
                ds = dg * (au * cdf_u)
                da_ref[rs, cs] = (dg * s * (cdf_u + au * _pdf(au))).astype(BF16)
                dsb = ds.astype(BF16)
                dvn_scr[rs, cs] = _dot(w, dsb, TN)
                dws_ref[h] += _dot(dsb, vnb, NT)
                dsum_scr[:, cs] += ds
        dvn = dvn_scr[...]
        vhat = vhat_scr[...]
        dlg_ref[...] += jnp.sum(dvn * vhat, axis=0, keepdims=True)
        dlb_ref[...] += jnp.sum(dvn, axis=0, keepdims=True)
        dvh = dvn * lg_ref[...]
        dv = rstd * (dvh - jnp.mean(dvh, axis=-1, keepdims=True)
                     - vhat * jnp.mean(dvh * vhat, axis=-1, keepdims=True))
        da_ref[:, W:] = (dv * (cdf_v + av * _pdf(av))).astype(BF16)

        @pl.when(i == nsteps - 1)
        def _():
            for h in range(GM_HEADS):
                dws_ref[h] = jnp.where(mask, dws_ref[h], 0.0)
            col_head = lax.broadcasted_iota(jnp.int32, (W, GM_BLOCK), 0) // GM_HEAD_DIM
            sel = (col_head == lax.broadcasted_iota(jnp.int32, (W, GM_BLOCK), 1)).astype(F32)
            dbs_ref[...] = _dot(dsum_scr[...], sel, NN, HI)

    vshape = jax.ShapeDtypeStruct((1, W), F32)
    return pl.pallas_call(
        body, name=name, grid=(nsteps,),
        out_shape=(jax.ShapeDtypeStruct((T, 2 * W), BF16), jax.ShapeDtypeStruct((GM_HEADS, GM_BLOCK, GM_BLOCK), F32),
                   jax.ShapeDtypeStruct((GM_BLOCK, GM_BLOCK), F32), vshape, vshape),
        in_specs=[_gm_specs(tr)[0], _row_spec(tr, W)] + _gm_specs(tr)[1:],
        out_specs=(_row_spec(tr, 2 * W), pl.BlockSpec((GM_HEADS, GM_BLOCK, GM_BLOCK), lambda i: (0, 0, 0)),
                   pl.BlockSpec((GM_BLOCK, GM_BLOCK), lambda i: (0, 0)), _vec_spec(W), _vec_spec(W)),
        scratch_shapes=[pltpu.VMEM((tr, W), BF16), pltpu.VMEM((tr, W), F32), pltpu.VMEM((tr, W), F32),
                        pltpu.VMEM((GM_BLOCK, W), F32)],
        compiler_params=_params(("arbitrary",)),
    )(a, dgated, ln_g, ln_b, ws, bs3)


SUB = 16
EXP_CLAMP = 80.0


def _tri(lower):
    r = lax.broadcasted_iota(jnp.int32, (CHUNK, CHUNK), 0)
    c = lax.broadcasted_iota(jnp.int32, (CHUNK, CHUNK), 1)
    return (r >= c) if lower else (c >= r)


def _score_masks():
    i = lax.broadcasted_iota(jnp.int32, (CHUNK, CHUNK), 0)
    j = lax.broadcasted_iota(jnp.int32, (CHUNK, CHUNK), 1)
    bi, bj = i // SUB, j // SUB
    diag = (bi == bj) & (i >= j)
    pair = (bi % 2 == 1) & (bj == bi - 1)
    half = (i >= CHUNK // 2) & (j < CHUNK // 2)
    return diag, pair, half


def _dot01(m, x):
    x1 = x.astype(BF16)
    rest = x - x1.astype(F32)
    x2 = rest.astype(BF16)
    x3 = (rest - x2.astype(F32)).astype(BF16)
    return _dot(m, x1, NN) + (_dot(m, x2, NN) + _dot(m, x3, NN))


def _block_rows(b, offset):
    parts = []
    for blk in range(0, CHUNK, SUB):
        r = blk + offset
        parts.append(jnp.zeros((SUB, b.shape[1]), F32) if r < 0 else jnp.broadcast_to(b[r:r + 1], (SUB, b.shape[1])))
    return jnp.concatenate(parts, axis=0)


def _hg_gates(p_ref, lb_ref, h, lower):
    Dm = D_MODEL
    c0 = h * HG_DIM
    qr = p_ref[:, c0:c0 + HG_DIM]
    fz = p_ref[:, Dm + c0:Dm + c0 + HG_DIM]
    v = p_ref[:, 2 * Dm + c0:2 * Dm + c0 + HG_DIM]
    gt = p_ref[:, 3 * Dm + c0:3 * Dm + c0 + HG_DIM]
    lbh = lb_ref[:, c0:c0 + HG_DIM]
    sg = _sig(fz)
    f = lbh + (1.0 - lbh) * sg
    gl = jnp.log(f)
    kk = 1.0 - f
    sq = _sig(qr)
    q = qr * sq
    b = _dot01(lower, gl)
    r_mid = _block_rows(b, SUB // 2 - 1)
    r_prev = _block_rows(b, -1)
    r_end = _block_rows(b, SUB - 1)
    r_half = jnp.broadcast_to(b[CHUNK // 2 - 1:CHUNK // 2], b.shape)
    bc = b[CHUNK - 1:CHUNK]
    eqs = (jnp.exp(jnp.clip(b - r_mid, -EXP_CLAMP, EXP_CLAMP)), jnp.exp(jnp.minimum(b - r_prev, 0.0)),
           jnp.exp(jnp.minimum(b - r_half, 0.0)))
    eks = (jnp.exp(jnp.clip(r_mid - b, -EXP_CLAMP, EXP_CLAMP)), jnp.exp(jnp.minimum(r_end - b, 0.0)),
           jnp.exp(jnp.minimum(r_half - b, 0.0)))
    eb = jnp.exp(b)
    ec = jnp.exp(bc - b)
    return dict(qr=qr, v=v, gt=gt, lbh=lbh, sg=sg, f=f, kk=kk, sq=sq, q=q, eqs=eqs, eks=eks, eb=eb, ec=ec,
                e_end=jnp.exp(bc), qs=[q * e for e in eqs], ks=[kk * e for e in eks], qe=q * eb, ke=kk * ec)


def _scores(g, masks):
    a = None
    for qs, ks, m in zip(g["qs"], g["ks"], masks):
        part = jnp.where(m, _dot(qs.astype(BF16), ks.astype(BF16), NT), 0.0)
        a = part if a is None else a + part
    return a


def _hg_scan_fwd(p, lb, gn, name):
    T = p.shape[0]
    nc = T // CHUNK
    Dm = D_MODEL

    def body(p_ref, lb_ref, gn_ref, o_ref, og_ref, so_ref, st_ref):
        @pl.when(pl.program_id(0) == 0)
        def _():
            st_ref[...] = jnp.zeros_like(st_ref)

        masks = _score_masks()
        lower = _tri(True).astype(BF16)
        for h in range(HG_HEADS):
            g = _hg_gates(p_ref, lb_ref, h, lower)
            cs = slice(h * HG_DIM, (h + 1) * HG_DIM)
            a = _scores(g, masks)
            st = st_ref[h]
            so_ref[0, h] = st
            o = _dot(a.astype(BF16), g["v"].astype(BF16), NN) + _dot(g["qe"], st, NT, X3)
            st_ref[h] = st * g["e_end"] + _dot(g["v"], g["ke"], TN, X3)
            o_ref[:, cs] = o
            r = lax.rsqrt(jnp.mean(o * o, axis=-1, keepdims=True) + EPS)
            gt = g["gt"]
            og_ref[:, cs] = (((o * r) * gn_ref[:, cs]).astype(F32) * (gt * _sig(gt))).astype(BF16)

    return pl.pallas_call(
        body, name=name, grid=(nc,),
        out_shape=(jax.ShapeDtypeStruct((T, Dm), F32), jax.ShapeDtypeStruct((T, Dm), BF16),
                   jax.ShapeDtypeStruct((nc, HG_HEADS, HG_DIM, HG_DIM), F32)),
        in_specs=[_row_spec(CHUNK, 4 * Dm), _vec_spec(Dm), _vec_spec(Dm)],
        out_specs=(_row_spec(CHUNK, Dm), _row_spec(CHUNK, Dm),
                   pl.BlockSpec((1, HG_HEADS, HG_DIM, HG_DIM), lambda i: (i, 0, 0, 0))),
        scratch_shapes=[pltpu.VMEM((HG_HEADS, HG_DIM, HG_DIM), F32)],
        compiler_params=_params(("arbitrary",)),
    )(p, lb, gn)


def _hg_scan_bwd(p, lb, gn, o, dog, states, name):
    T = p.shape[0]
    nc = T // CHUNK
    Dm = D_MODEL

    def rev(i):
        return nc - 1 - i

    def body(p_ref, lb_ref, gn_ref, o_ref, dog_ref, st_in_ref, dp_ref, dlb_ref, dgn_ref, dst_ref, carry_ref):
        @pl.when(pl.program_id(0) == 0)
        def _():
            dst_ref[...] = jnp.zeros_like(dst_ref)
            carry_ref[...] = jnp.zeros_like(carry_ref)
            dlb_ref[...] = jnp.zeros_like(dlb_ref)
            dgn_ref[...] = jnp.zeros_like(dgn_ref)

        upper = _tri(False).astype(BF16)
        lower = _tri(True).astype(BF16)
        masks = _score_masks()
        for h in range(HG_HEADS):
            g = _hg_gates(p_ref, lb_ref, h, lower)
            c0 = h * HG_DIM
            cs = slice(c0, c0 + HG_DIM)
            oh = o_ref[:, cs]
            r = lax.rsqrt(jnp.mean(oh * oh, axis=-1, keepdims=True) + EPS)
            on = oh * r
            gt = g["gt"]
            sgt = _sig(gt)
            sil = gt * sgt
            dogh = dog_ref[:, cs]
            gnh = gn_ref[:, cs]
            don = dogh * gnh * sil
            dgn_ref[:, cs] += jnp.sum(dogh * on * sil, axis=0, keepdims=True)
            dgate = dogh * on * gnh * (sgt * (1.0 + gt * (1.0 - sgt)))
            do = r * (don - on * jnp.mean(don * on, axis=-1, keepdims=True))
            a = _scores(g, masks)
            dst = dst_ref[h]
            dob = do.astype(BF16)
            da = _dot(dob, g["v"].astype(BF16), NT)
            dv = _dot(a.astype(BF16), dob, TN) + _dot(g["ke"].astype(BF16), dst.astype(BF16), NT)
            dq = _dot(do, st_in_ref[0, h], NN, X3) * g["eb"]
            dk = _dot(g["v"], dst, NN, X3) * g["ec"]
            for qs, ks, eq, ek, m in zip(g["qs"], g["ks"], g["eqs"], g["eks"], masks):
                dam = jnp.where(m, da, 0.0)
                dq = dq + _dot(dam, ks, NN, X3) * eq
                dk = dk + _dot(dam, qs, TN, X3) * ek
            dst_ref[h] = dst * g["e_end"] + _dot(do, g["qe"], TN, X3)
            dgd = g["q"] * dq - g["kk"] * dk
            dgl = _dot01(upper, dgd) + carry_ref[h]
            carry_ref[h] += jnp.sum(dgd, axis=0, keepdims=True)
            df = dgl / g["f"] - dk
            sg = g["sg"]
            dlb_ref[:, cs] += jnp.sum(df * (1.0 - sg), axis=0, keepdims=True)
            sq, qr = g["sq"], g["qr"]
            dp_ref[:, c0:c0 + HG_DIM] = (dq * (sq * (1.0 + qr * (1.0 - sq)))).astype(BF16)
            dp_ref[:, Dm + c0:Dm + c0 + HG_DIM] = (df * (1.0 - g["lbh"]) * sg * (1.0 - sg)).astype(BF16)
            dp_ref[:, 2 * Dm + c0:2 * Dm + c0 + HG_DIM] = dv.astype(BF16)
            dp_ref[:, 3 * Dm + c0:3 * Dm + c0 + HG_DIM] = dgate.astype(BF16)

    vshape = jax.ShapeDtypeStruct((1, Dm), F32)
    rrow = lambda w: pl.BlockSpec((CHUNK, w), lambda i: (rev(i), 0))
    return pl.pallas_call(
        body, name=name, grid=(nc,),
        out_shape=(jax.ShapeDtypeStruct((T, 4 * Dm), BF16), vshape, vshape),
        in_specs=[rrow(4 * Dm), _vec_spec(Dm), _vec_spec(Dm), rrow(Dm), rrow(Dm),
                  pl.BlockSpec((1, HG_HEADS, HG_DIM, HG_DIM), lambda i: (rev(i), 0, 0, 0))],
        out_specs=(rrow(4 * Dm), _vec_spec(Dm), _vec_spec(Dm)),
        scratch_shapes=[pltpu.VMEM((HG_HEADS, HG_DIM, HG_DIM), F32), pltpu.VMEM((HG_HEADS, 1, HG_DIM), F32)],
        compiler_params=_params(("arbitrary",)),
    )(p, lb, gn, o, dog, states)


def _lb_fwd(hg_lb, name):
    def body(a_ref, o_ref):
        a0, a1 = a_ref[0:1], a_ref[1:2]
        m = jnp.maximum(a0, a1)
        e0, e1 = jnp.exp(a0 - m), jnp.exp(a1 - m)
        p0, p1 = e0 / (e0 + e1), e1 / (e0 + e1)
        o_ref[0:1] = p0 - p0
        o_ref[1:2] = (p0 + p1) - p0

    return pl.pallas_call(body, name=name, out_shape=jax.ShapeDtypeStruct(hg_lb.shape, F32))(hg_lb)


def _lb_bwd(hg_lb, dlb_all, name):
    def body(a_ref, d_ref, o_ref):
        a0, a1 = a_ref[0:1], a_ref[1:2]
        m = jnp.maximum(a0, a1)
        e0, e1 = jnp.exp(a0 - m), jnp.exp(a1 - m)
        p0, p1 = e0 / (e0 + e1), e1 / (e0 + e1)
        d1 = d_ref[1:2]
        o_ref[0:1] = -p0 * p1 * d1
        o_ref[1:2] = p1 * (1.0 - p1) * d1

    return pl.pallas_call(body, name=name, out_shape=jax.ShapeDtypeStruct(hg_lb.shape, F32))(hg_lb, dlb_all)


CONV_COLS = 256


def _conv_fwd(a, w, b, name):
    T = a.shape[0]
    Fh = FFN_HIDDEN
    tr = _pick(T, (ROW_TILE,))
    cw = CONV_COLS
    hb = tr // 8

    def body(a_ref, ap_ref, w_ref, b_ref, m_ref):
        m0 = (pl.program_id(0) > 0).astype(F32)

        def conv(cc):
            x = jnp.concatenate([ap_ref[:, pl.ds(cc, cw)] * m0, a_ref[:, pl.ds(cc, cw)]], axis=0)
            wv = w_ref[:, pl.ds(cc, cw)]
            y = b_ref[:, pl.ds(cc, cw)] + wv[2:3] * x + wv[1:2] * pltpu.roll(x, 1, axis=0) \
                + wv[0:1] * pltpu.roll(x, 2, axis=0)
            return y[8:]

        def step(c, carry):
            c0 = pl.multiple_of(c * cw, cw)
            c1 = pl.multiple_of(Fh + c * cw, cw)
            yg, yv = conv(c0), conv(c1)
            m_ref[:, pl.ds(c0, cw)] = (yg * _cdf(yg) * yv).astype(BF16)
            return carry

        lax.fori_loop(0, Fh // cw, step, 0)

    return pl.pallas_call(
        body, name=name, out_shape=jax.ShapeDtypeStruct((T, Fh), BF16), grid=(T // tr,),
        in_specs=[_row_spec(tr, 2 * Fh), pl.BlockSpec((8, 2 * Fh), lambda i: (jnp.maximum(i * hb - 1, 0), 0)),
                  _vec_spec(2 * Fh, 3), _vec_spec(2 * Fh)],
        out_specs=_row_spec(tr, Fh), compiler_params=_params(("parallel",)),
    )(a, a, w, b)


def _conv_bwd(a, dm, w, b, name):
    T = a.shape[0]
    Fh = FFN_HIDDEN
    tr = _pick(T, (ROW_TILE,))
    cw = CONV_COLS
    hb = tr // 8
    nsteps = T // tr
    n = tr + 8

    def body(a_ref, ap_ref, an_ref, dm_ref, dmn_ref, w_ref, b_ref, da_ref, dw_ref, db_ref):
        i = pl.program_id(0)
        m0 = (i > 0).astype(F32)
        m1 = (i < nsteps - 1).astype(F32)

        @pl.when(i == 0)
        def _():
            dw_ref[...] = jnp.zeros_like(dw_ref)
            db_ref[...] = jnp.zeros_like(db_ref)

        def prep(cc):
            x = jnp.concatenate([ap_ref[:, pl.ds(cc, cw)] * m0, a_ref[:, pl.ds(cc, cw)],
                                 an_ref[:, pl.ds(cc, cw)] * m1], axis=0)
            wv = w_ref[:, pl.ds(cc, cw)]
            s1 = pltpu.roll(x, 1, axis=0)
            s2 = pltpu.roll(x, 2, axis=0)
            y = b_ref[:, pl.ds(cc, cw)] + wv[2:3] * x + wv[1:2] * s1 + wv[0:1] * s2
            return wv, x[8:], s1[8:], s2[8:], y[8:]

        def back(cc, dy, wv, x0, s1, s2):
            da = wv[2:3] * dy + wv[1:2] * pltpu.roll(dy, n - 1, axis=0) + wv[0:1] * pltpu.roll(dy, n - 2, axis=0)
            da_ref[:, pl.ds(cc, cw)] = da[:tr].astype(BF16)
            d = dy[:tr]
            db_ref[:, pl.ds(cc, cw)] += jnp.sum(d, axis=0, keepdims=True)
            dw_ref[2:3, pl.ds(cc, cw)] += jnp.sum(d * x0[:tr], axis=0, keepdims=True)
            dw_ref[1:2, pl.ds(cc, cw)] += jnp.sum(d * s1[:tr], axis=0, keepdims=True)
            dw_ref[0:1, pl.ds(cc, cw)] += jnp.sum(d * s2[:tr], axis=0, keepdims=True)

        def step(c, carry):
            c0 = pl.multiple_of(c * cw, cw)
            c1 = pl.multiple_of(Fh + c * cw, cw)
            dmx = jnp.concatenate([dm_ref[:, pl.ds(c0, cw)], dmn_ref[:, pl.ds(c0, cw)] * m1], axis=0)
            wg, xg, s1g, s2g, yg = prep(c0)
            wv, xv, s1v, s2v, yv = prep(c1)
            cg = _cdf(yg)
            back(c0, dmx * yv * (cg + yg * _pdf(yg)), wg, xg, s1g, s2g)
            back(c1, dmx * (yg * cg), wv, xv, s1v, s2v)
            return carry

        lax.fori_loop(0, Fh // cw, step, 0)

    prev = lambda wd: pl.BlockSpec((8, wd), lambda i: (jnp.maximum(i * hb - 1, 0), 0))
    nxt = lambda wd: pl.BlockSpec((8, wd), lambda i: (jnp.minimum((i + 1) * hb, T // 8 - 1), 0))
    return pl.pallas_call(
        body, name=name, grid=(nsteps,),
        out_shape=(jax.ShapeDtypeStruct((T, 2 * Fh), BF16), jax.ShapeDtypeStruct((3, 2 * Fh), F32),
                   jax.ShapeDtypeStruct((1, 2 * Fh), F32)),
        in_specs=[_row_spec(tr, 2 * Fh), prev(2 * Fh), nxt(2 * Fh), _row_spec(tr, Fh), nxt(Fh),
                  _vec_spec(2 * Fh, 3), _vec_spec(2 * Fh)],
        out_specs=(_row_spec(tr, 2 * Fh), _vec_spec(2 * Fh, 3), _vec_spec(2 * Fh)),
        compiler_params=_params(("arbitrary",)),
    )(a, a, a, dm, dm, w, b)


def _ada_fwd(c_all, ada_w, ada_b, name):
    L, Dm, cols = ada_w.shape
    tn = _pick(cols, (512, 256, 128))

    def body(c_ref, w_ref, b_ref, o_ref):
        cv = c_ref[...]
        cond = (cv * _sig(cv)).astype(BF16)
        o_ref[...] = _dot(cond, w_ref[...].astype(BF16), NN) + b_ref[...]

    return pl.pallas_call(
        body, name=name, out_shape=jax.ShapeDtypeStruct((L, N_DEV, cols), F32), grid=(L, cols // tn),
        in_specs=[pl.BlockSpec((N_DEV, Dm), lambda l, j: (0, 0)), pl.BlockSpec((None, Dm, tn), lambda l, j: (l, 0, j)),
                  pl.BlockSpec((None, 1, tn), lambda l, j: (l, 0, j))],
        out_specs=pl.BlockSpec((None, N_DEV, tn), lambda l, j: (l, 0, j)),
        compiler_params=_params(("parallel", "parallel")),
    )(c_all, ada_w, ada_b.reshape(L, 1, cols))


def _ada_bwd(c_all, dmod, name):
    L, _, cols = dmod.shape
    Dm = c_all.shape[1]
    tn = _pick(cols, (512, 256, 128))

    def body(c_ref, d_ref, o_ref):
        cv = c_ref[...]
        o_ref[...] = _dot(cv * _sig(cv), d_ref[...], TN, HI)

    return pl.pallas_call(
        body, name=name, out_shape=jax.ShapeDtypeStruct((L, Dm, cols), F32), grid=(L, cols // tn),
        in_specs=[pl.BlockSpec((N_DEV, Dm), lambda l, j: (0, 0)), pl.BlockSpec((None, N_DEV, tn), lambda l, j: (l, 0, j))],
        out_specs=pl.BlockSpec((None, Dm, tn), lambda l, j: (l, 0, j)),
        compiler_params=_params(("parallel", "parallel")),
    )(c_all, dmod)


def _add_own_half(g4, rb, core, name):
    S, _, rh, cw = g4.shape
    tr = _pick(rh, (256, 128, 176, 64))

    def body(core_ref, g_ref, r_ref, o_ref):
        o_ref[...] = (g_ref[...] + r_ref[...].astype(F32)).astype(BF16)

    return pl.pallas_call(
        body, name=name, out_shape=jax.ShapeDtypeStruct((S, rh, cw), BF16),
        grid_spec=pltpu.PrefetchScalarGridSpec(
            num_scalar_prefetch=1, grid=(S, rh // tr),
            in_specs=[pl.BlockSpec((None, None, tr, cw), lambda s, i, core_ref: (s, core_ref[0], i, 0)),
                      pl.BlockSpec((None, tr, cw), lambda s, i, core_ref: (s, i, 0))],
            out_specs=pl.BlockSpec((None, tr, cw), lambda s, i, core_ref: (s, i, 0))),
        compiler_params=_params(("parallel", "parallel")),
    )(core, g4, rb)


def _sum_chips(rcs, name):
    L = len(rcs)
    S, rh, cw = rcs[0].shape
    tr = _pick(rh, (256, 128, 176, 64))

    def body(*refs):
        in_refs, o_ref = refs[:L], refs[L]
        for k in range(L):
            @pl.when(pl.program_id(0) == k)
            def _(k=k):
                acc = in_refs[k][0].astype(F32)
                for s in range(1, S):
                    acc = acc + in_refs[k][s].astype(F32)
                o_ref[...] = acc

    return pl.pallas_call(
        body, name=name, out_shape=jax.ShapeDtypeStruct((L, rh, cw), F32), grid=(L, rh // tr),
        in_specs=[pl.BlockSpec((S, tr, cw), lambda l, i, k=k: (0, jnp.where(l == k, i, 0), 0)) for k in range(L)],
        out_specs=pl.BlockSpec((None, tr, cw), lambda l, i: (l, i, 0)),
        compiler_params=_params(("arbitrary", "arbitrary")),
    )(*rcs)


def _sum_devices(gathered, name):
    n, R, _ = gathered.shape
    tr = _pick(R, (512, 256, 128, 64, 32, 16, 8))

    def body(g_ref, o_ref):
        acc = g_ref[0]
        for d in range(1, n):
            acc = acc + g_ref[d]
        o_ref[...] = acc

    return pl.pallas_call(
        body, name=name, out_shape=jax.ShapeDtypeStruct((R, LANES), F32), grid=(R // tr,),
        in_specs=[pl.BlockSpec((n, tr, LANES), lambda i: (0, i, 0))], out_specs=pl.BlockSpec((tr, LANES), lambda i: (i, 0)),
        compiler_params=_params(("parallel",)),
    )(gathered)


def _adamw(w, g, m, v, name):
    R, C = w.shape
    tr = _pick(R, (256, 128, 64, 32, 16, 8))
    c1 = 1.0 / (1.0 - ADAM_B1 ** ADAM_STEP)
    c2 = 1.0 / (1.0 - ADAM_B2 ** ADAM_STEP)

    def body(w_ref, g_ref, m_ref, v_ref, d_ref, mo_ref, vo_ref):
        gv = g_ref[...]
        m2 = ADAM_B1 * m_ref[...] + (1.0 - ADAM_B1) * gv
        v2 = ADAM_B2 * v_ref[...] + (1.0 - ADAM_B2) * (gv * gv)
        mo_ref[...] = m2
        vo_ref[...] = v2
        d_ref[...] = -ADAM_LR * ((m2 * c1) / (jnp.sqrt(v2 * c2) + ADAM_EPS) + ADAM_WD * w_ref[...])

    spec = pl.BlockSpec((tr, C), lambda i: (i, 0))
    shp = jax.ShapeDtypeStruct((R, C), F32)
    return pl.pallas_call(body, name=name, out_shape=(shp, shp, shp), grid=(R // tr,), in_specs=[spec] * 4,
                          out_specs=(spec, spec, spec), compiler_params=_params(("parallel",)))(w, g, m, v)


def _adamw_halves(w, own, recv, m, v, core, name):
    L, rh, cw = own.shape
    tr = _pick(rh, (256, 128, 176, 64))
    c1 = 1.0 / (1.0 - ADAM_B1 ** ADAM_STEP)
    c2 = 1.0 / (1.0 - ADAM_B2 ** ADAM_STEP)

    def body(core_ref, w_ref, own_ref, recv_ref, m_ref, v_ref, g_ref, d_ref, mo_ref, vo_ref):
        gv = jnp.where(pl.program_id(1) == core_ref[0], own_ref[...], recv_ref[...])
        g_ref[...] = gv
        m2 = ADAM_B1 * m_ref[...] + (1.0 - ADAM_B1) * gv
        v2 = ADAM_B2 * v_ref[...] + (1.0 - ADAM_B2) * (gv * gv)
        mo_ref[...] = m2
        vo_ref[...] = v2
        d_ref[...] = -ADAM_LR * ((m2 * c1) / (jnp.sqrt(v2 * c2) + ADAM_EPS) + ADAM_WD * w_ref[...])

    full = pl.BlockSpec((None, None, tr, cw), lambda l, hf, i, core_ref: (l, hf, i, 0))
    mine = pl.BlockSpec((None, tr, cw), lambda l, hf, i, core_ref: (l, jnp.where(hf == core_ref[0], i, 0), 0))
    other = pl.BlockSpec((None, tr, cw), lambda l, hf, i, core_ref: (l, jnp.where(hf == core_ref[0], 0, i), 0))
    shp = jax.ShapeDtypeStruct((L, 2, rh, cw), F32)
    view = lambda a: a.reshape(L, 2, rh, cw)
    outs = pl.pallas_call(
        body, name=name, out_shape=(shp, shp, shp, shp),
        grid_spec=pltpu.PrefetchScalarGridSpec(
            num_scalar_prefetch=1, grid=(L, 2, rh // tr), in_specs=[full, mine, other, full, full],
            out_specs=(full, full, full, full)),
        compiler_params=_params(("arbitrary", "arbitrary", "arbitrary")),
    )(core, view(w), own, recv, view(m), view(v))
    return tuple(o.reshape(L, 2 * rh, cw) for o in outs)


ANY = pl.BlockSpec(memory_space=pl.ANY)


def _position():
    x, y, c = lax.axis_index("x"), lax.axis_index("y"), lax.axis_index("c")
    return x, y, c


def _allgather(ins, out_shapes, src_fns, dst_fns, name, in_vmem):
    n = len(ins)

    def body(*refs):
        in_refs, out_refs = refs[:n], refs[n:2 * n]
        send_sems, recv_sems, local_sems = refs[2 * n:]
        x, y, c = _position()
        me, sibling = (x, y, c), (x, y, 1 - c)
        chips = [(1 - x, y), (x, 1 - y), (1 - x, 1 - y)]

        def copy(k, j, block, to, own=False):
            dst = dst_fns[k](out_refs[k], *block)
            return pltpu.make_async_remote_copy(
                src_ref=src_fns[k](in_refs[k], c) if own else dst, dst_ref=dst,
                send_sem=send_sems.at[k, j], recv_sem=recv_sems.at[k, j], device_id=to, device_id_type=MESH)

        mine = [pltpu.make_async_copy(src_fns[k](in_refs[k], c), dst_fns[k](out_refs[k], *me), local_sems.at[k])
                for k in range(n)]
        for cp in mine:
            cp.start()
        first = []
        for k in range(n):
            first.append(copy(k, 0, me, sibling, own=True))
            first += [copy(k, 1 + j, me, (*chip, c), own=True) for j, chip in enumerate(chips)]
        for cp in first:
            cp.start()
        passed = []
        for j, chip in enumerate(chips):
            for k in range(n):
                copy(k, 1 + j, (*chip, c), me).wait_recv()
                fwd = copy(k, 4 + j, (*chip, c), sibling)
                fwd.start()
                passed.append(fwd)
        for k in range(n):
            copy(k, 0, sibling, me).wait_recv()
        for j, chip in enumerate(chips):
            for k in range(n):
                copy(k, 4 + j, (*chip, 1 - c), me).wait_recv()
        for cp in first + passed:
            cp.wait_send()
        for cp in mine:
            cp.wait()

    spec = pl.BlockSpec(memory_space=pltpu.VMEM) if in_vmem else ANY
    return pl.pallas_call(
        body, name=name, out_shape=tuple(out_shapes), in_specs=[spec] * n, out_specs=tuple([spec] * n),
        scratch_shapes=[pltpu.SemaphoreType.DMA((n, 7)), pltpu.SemaphoreType.DMA((n, 7)),
                        pltpu.SemaphoreType.DMA((n,))],
        compiler_params=pltpu.CompilerParams(vmem_limit_bytes=VMEM_LIMIT_BYTES),
    )(*ins)


def _allgather_small(payload, name):
    R = payload.shape[0]
    (out,) = _allgather(
        [payload], [jax.ShapeDtypeStruct((N_DEV, R, LANES), F32)],
        [lambda ref, c: ref], [lambda ref, px, py, pc: ref.at[4 * px + 2 * py + pc]], name, in_vmem=True)
    return out


def _gather_weights(shards, col_sharded, name):
    src_fns, dst_fns, out_shapes = [], [], []
    for sh, col in zip(shards, col_sharded):
        L, r, cw = sh.shape
        rh = r // 2
        src_fns.append(lambda ref, c, rh=rh: ref.at[:, pl.ds(c * rh, rh), :])
        if col:
            out_shapes.append(jax.ShapeDtypeStruct((L, r, N_CHIPS * cw), sh.dtype))
            dst_fns.append(lambda ref, px, py, pc, rh=rh, cw=cw:
                           ref.at[:, pl.ds(pc * rh, rh), pl.ds((2 * px + py) * cw, cw)])
        else:
            out_shapes.append(jax.ShapeDtypeStruct((L, N_CHIPS * r, cw), sh.dtype))
            dst_fns.append(lambda ref, px, py, pc, rh=rh, r=r:
                           ref.at[:, pl.ds((2 * px + py) * r + pc * rh, rh), :])
    return _allgather(list(shards), out_shapes, src_fns, dst_fns, name, in_vmem=False)


def _swap_sibling(ins, name):
    n = len(ins)

    def body(*refs):
        in_refs, out_refs = refs[:n], refs[n:2 * n]
        send_sems, recv_sems = refs[2 * n:]
        x, y, c = _position()
        copies = [pltpu.make_async_remote_copy(
            src_ref=in_refs[k], dst_ref=out_refs[k], send_sem=send_sems.at[k], recv_sem=recv_sems.at[k],
            device_id=(x, y, 1 - c), device_id_type=MESH) for k in range(n)]
        for cp in copies:
            cp.start()
        for cp in copies:
            cp.wait_recv()
        for cp in copies:
            cp.wait_send()

    return pl.pallas_call(
        body, name=name, out_shape=tuple(jax.ShapeDtypeStruct(a.shape, a.dtype) for a in ins),
        in_specs=[ANY] * n, out_specs=tuple([ANY] * n),
        scratch_shapes=[pltpu.SemaphoreType.DMA((n,)), pltpu.SemaphoreType.DMA((n,))],
        compiler_params=pltpu.CompilerParams(vmem_limit_bytes=VMEM_LIMIT_BYTES),
    )(*ins)


def _exchange_chips(ins, name):
    n = len(ins)

    def body(*refs):
        in_refs, out_refs = refs[:n], refs[n:2 * n]
        send_sems, recv_sems, local_sems = refs[2 * n:]
        x, y, c = _position()
        my_chip = 2 * x + y
        chips = [(1 - x, y), (x, 1 - y), (1 - x, 1 - y)]
        local = [pltpu.make_async_copy(in_refs[k].at[my_chip], out_refs[k].at[my_chip], local_sems.at[k])
                 for k in range(n)]
        for cp in local:
            cp.start()
        sends = []
        for j, (px, py) in enumerate(chips):
            for k in range(n):
                sends.append(pltpu.make_async_remote_copy(
                    src_ref=in_refs[k].at[2 * px + py], dst_ref=out_refs[k].at[my_chip],
                    send_sem=send_sems.at[k, j], recv_sem=recv_sems.at[k, j],
                    device_id=(px, py, c), device_id_type=MESH))
        for cp in sends:
            cp.start()
        for j, (px, py) in enumerate(chips):
            for k in range(n):
                pltpu.make_async_remote_copy(
                    src_ref=in_refs[k].at[2 * px + py], dst_ref=out_refs[k].at[2 * px + py],
                    send_sem=send_sems.at[k, j], recv_sem=recv_sems.at[k, j],
                    device_id=(px, py, c), device_id_type=MESH).wait_recv()
        for cp in sends:
            cp.wait_send()
        for cp in local:
            cp.wait()

    return pl.pallas_call(
        body, name=name, out_shape=tuple(jax.ShapeDtypeStruct(a.shape, a.dtype) for a in ins),
        in_specs=[ANY] * n, out_specs=tuple([ANY] * n),
        scratch_shapes=[pltpu.SemaphoreType.DMA((n, 3)), pltpu.SemaphoreType.DMA((n, 3)),
                        pltpu.SemaphoreType.DMA((n,))],
        compiler_params=pltpu.CompilerParams(vmem_limit_bytes=VMEM_LIMIT_BYTES),
    )(*ins)


def _vec(a):
    return a.reshape(1, -1)


def _local_step(x, tgt, mod, W, P):
    Dm = D_MODEL
    G = {k: [] for k in ("gm_w_in", "gm_w_out", "hg_w_in", "hg_w_out", "ffn_w_up", "ffn_w_down")}
    lb_all = _lb_fwd(P["hg_lb"], "lb_fwd")
    saved = []
    xs = x
    y_prev = gate_prev = None
    for i in range(DEPTH):
        m = [_vec(mod[i, j * Dm:(j + 1) * Dm]) for j in range(6)]
        sh1, sc1, g1, sh2, sc2, g2 = m
        j = i // 2
        xs, h = _norm_fwd(xs, y_prev, gate_prev, _vec(P["norm_g"][i, 0]), sc1, sh1, f"norm_fwd_a{i}")
        rec = dict(x1=xs, h1=h)
        if i % 2 == 0:
            a = _mm(h, W["gm_w_in"], "nn", f"gm_in{i}", b_layer=j)
            gated = _gm_mid_fwd(a, _vec(P["gm_ln_g"][j]), _vec(P["gm_ln_b"][j]), P["gm_w_s"][j],
                                P["gm_b_s"][j].reshape(GM_HEADS, GM_BLOCK, 1), f"gm_mid_fwd{i}")
            y1 = _mm(gated, W["gm_w_out"], "nn", f"gm_out{i}", b_layer=j)
            rec.update(a=a, act=gated)
        else:
            p = _mm(h, W["hg_w_in"], "nn", f"hg_in{i}", b_layer=j)
            o, og, states = _hg_scan_fwd(p, _vec(lb_all[j]), _vec(P["hg_gn_g"][j]), f"hg_scan_fwd{i}")
            y1 = _mm(og, W["hg_w_out"], "nn", f"hg_out{i}", b_layer=j)
            rec.update(a=p, act=og, o=o, states=states)
        rec["y1"] = y1
        xs, h2 = _norm_fwd(xs, y1, g1, _vec(P["norm_g"][i, 1]), sc2, sh2, f"norm_fwd_b{i}")
        a2 = _mm(h2, W["ffn_w_up"], "nn", f"ffn_up{i}", b_layer=i)
        mm_ = _conv_fwd(a2, P["ffn_conv_w"][i], _vec(P["ffn_conv_b"][i]), f"conv_fwd{i}")
        y2 = _mm(mm_, W["ffn_w_down"], "nn", f"ffn_down{i}", b_layer=i)
        rec.update(x2=xs, h2=h2, a2=a2, m=mm_, y2=y2, mods=m)
        saved.append(rec)
        y_prev, gate_prev = y2, g2
    dx, dy, loss, s_fg, s_gate = _loss_head(xs, y_prev, gate_prev, _vec(P["final_g"]), tgt, "loss_head")
    small = dict(final_g=s_fg, norm_g=[None] * DEPTH, dmod=[None] * DEPTH, ffn_conv_w=[None] * DEPTH,
                 ffn_conv_b=[None] * DEPTH, gm_ln_g=[None] * 2, gm_ln_b=[None] * 2, gm_w_s=[None] * 2,
                 gm_b_s=[None] * 2, hg_gn_g=[None] * 2, dlb=[None] * 2)
    for i in reversed(range(DEPTH)):
        rec = saved[i]
        sh1, sc1, g1, sh2, sc2, g2 = rec["mods"]
        j = i // 2
        d_g2 = s_gate
        dm = _mm(dy, W["ffn_w_down"], "nt", f"ffn_down_dx{i}", b_layer=i)
        G["ffn_w_down"].append(_mm(rec["m"], dy, "tn", f"ffn_down_dw{i}"))
        da2, dcw, dcb = _conv_bwd(rec["a2"], dm, P["ffn_conv_w"][i], _vec(P["ffn_conv_b"][i]), f"conv_bwd{i}")
        small["ffn_conv_w"][i], small["ffn_conv_b"][i] = dcw, dcb
        dh2 = _mm(da2, W["ffn_w_up"], "nt", f"ffn_up_dx{i}", b_layer=i)
        G["ffn_w_up"].append(_mm(rec["h2"], da2, "tn", f"ffn_up_dw{i}", exchange_out=True))
        ng2 = _vec(P["norm_g"][i, 1])
        dx, dy, s_sh2, s_x2, d_g1 = _norm_bwd(rec["x2"], dh2, dx, ng2, sc2, rec["y1"], g1, f"norm_bwd_b{i}")
        d_sc2, d_ng2 = s_x2 * ng2, s_x2 * (1.0 + sc2)
        if i % 2 == 0:
            dgated = _mm(dy, W["gm_w_out"], "nt", f"gm_out_dx{i}", b_layer=j)
            G["gm_w_out"].append(_mm(rec["act"], dy, "tn", f"gm_out_dw{i}"))
            da, dws, dbs, dlg, dlbeta = _gm_mid_bwd(
                rec["a"], dgated, _vec(P["gm_ln_g"][j]), _vec(P["gm_ln_b"][j]), P["gm_w_s"][j],
                P["gm_b_s"][j].reshape(GM_HEADS, GM_BLOCK, 1), f"gm_mid_bwd{i}")
            small["gm_w_s"][j], small["gm_b_s"][j] = dws, dbs[:, :GM_HEADS].T
            small["gm_ln_g"][j], small["gm_ln_b"][j] = dlg, dlbeta
            dh1 = _mm(da, W["gm_w_in"], "nt", f"gm_in_dx{i}", b_layer=j)
            G["gm_w_in"].append(_mm(rec["h1"], da, "tn", f"gm_in_dw{i}", exchange_out=True))
        else:
            dog = _mm(dy, W["hg_w_out"], "nt", f"hg_out_dx{i}", b_layer=j)
            G["hg_w_out"].append(_mm(rec["act"], dy, "tn", f"hg_out_dw{i}"))
            dp, dlb, dgn = _hg_scan_bwd(rec["a"], _vec(lb_all[j]), _vec(P["hg_gn_g"][j]), rec["o"], dog,
                                        rec["states"], f"hg_scan_bwd{i}")
            small["dlb"][j], small["hg_gn_g"][j] = dlb, dgn
            dh1 = _mm(dp, W["hg_w_in"], "nt", f"hg_in_dx{i}", b_layer=j)
            G["hg_w_in"].append(_mm(rec["h1"], dp, "tn", f"hg_in_dw{i}", exchange_out=True))
        ng1 = _vec(P["norm_g"][i, 0])
        if i > 0:
            prev = saved[i - 1]
            dx, dy, s_sh1, s_x1, s_gate = _norm_bwd(rec["x1"], dh1, dx, ng1, sc1, prev["y2"], prev["mods"][5],
                                                    f"norm_bwd_a{i}")
        else:
            dx, s_sh1, s_x1 = _norm_bwd(rec["x1"], dh1, dx, ng1, sc1, None, None, f"norm_bwd_a{i}")
        d_sc1, d_ng1 = s_x1 * ng1, s_x1 * (1.0 + sc1)
        small["norm_g"][i] = jnp.concatenate([d_ng1, d_ng2], axis=0)
        small["dmod"][i] = jnp.concatenate([s_sh1, d_sc1, d_g1, s_sh2, d_sc2, d_g2], axis=1)
    for k in G:
        G[k] = G[k][::-1]
    dlb_all = jnp.concatenate(small.pop("dlb"), axis=0)
    small["hg_lb"] = _lb_bwd(P["hg_lb"], dlb_all, "lb_bwd")
    return loss, dx, G, small


BIG = ("gm_w_in", "gm_w_out", "hg_w_in", "hg_w_out", "ffn_w_up", "ffn_w_down")
COL_SHARDED = dict(gm_w_in=True, gm_w_out=False, hg_w_in=True, hg_w_out=False, ffn_w_up=True, ffn_w_down=False)


def _pack(pieces):
    flat = [p.reshape(-1).astype(F32) for p in pieces]
    offs, tot = [], 0
    for f in flat:
        offs.append((tot, f.shape[0]))
        tot += f.shape[0]
    padded = -(-tot // (8 * LANES)) * (8 * LANES)
    if padded > tot:
        flat.append(jnp.zeros((padded - tot,), F32))
    return jnp.concatenate(flat).reshape(-1, LANES), offs


def _unpack(rows, offs, shapes):
    lead = rows.shape[:-2]
    flat = rows.reshape(lead + (-1,))
    return [flat[..., o:o + n].reshape(lead + tuple(s)) for (o, n), s in zip(offs, shapes)]


def _from_chips(per_dev, axis):
    per_chip = per_dev[0::2]
    return jnp.concatenate([per_chip[s] for s in range(N_CHIPS)], axis=axis)


def kernel(x, c, gm_w_in, gm_ln_g, gm_ln_b, gm_w_s, gm_b_s, gm_w_out, hg_w_in, hg_lb, hg_gn_g, hg_w_out, ffn_w_up, ffn_conv_w, ffn_conv_b, ffn_w_down, norm_g, ada_w, ada_b, final_g, loss_target, m_gm_w_in, m_gm_ln_g, m_gm_ln_b, m_gm_w_s, m_gm_b_s, m_gm_w_out, m_hg_w_in, m_hg_lb, m_hg_gn_g, m_hg_w_out, m_ffn_w_up, m_ffn_conv_w, m_ffn_conv_b, m_ffn_w_down, m_norm_g, m_ada_w, m_ada_b, m_final_g, v_gm_w_in, v_gm_ln_g, v_gm_ln_b, v_gm_w_s, v_gm_b_s, v_gm_w_out, v_hg_w_in, v_hg_lb, v_hg_gn_g, v_hg_w_out, v_ffn_w_up, v_ffn_conv_w, v_ffn_conv_b, v_ffn_w_down, v_norm_g, v_ada_w, v_ada_b, v_final_g):
    Dm = D_MODEL
    xi, yi, ci = _position()
    chip = 2 * xi + yi
    dev = 4 * xi + 2 * yi + ci
    weights = dict(gm_w_in=gm_w_in, gm_ln_g=gm_ln_g, gm_ln_b=gm_ln_b, gm_w_s=gm_w_s, gm_b_s=gm_b_s,
                   gm_w_out=gm_w_out, hg_w_in=hg_w_in, hg_lb=hg_lb, hg_gn_g=hg_gn_g, hg_w_out=hg_w_out,
                   ffn_w_up=ffn_w_up, ffn_conv_w=ffn_conv_w, ffn_conv_b=ffn_conv_b, ffn_w_down=ffn_w_down,
                   norm_g=norm_g, ada_w=ada_w, ada_b=ada_b, final_g=final_g)
    mom_m = dict(gm_w_in=m_gm_w_in, gm_ln_g=m_gm_ln_g, gm_ln_b=m_gm_ln_b, gm_w_s=m_gm_w_s, gm_b_s=m_gm_b_s,
                 gm_w_out=m_gm_w_out, hg_w_in=m_hg_w_in, hg_lb=m_hg_lb, hg_gn_g=m_hg_gn_g, hg_w_out=m_hg_w_out,
                 ffn_w_up=m_ffn_w_up, ffn_conv_w=m_ffn_conv_w, ffn_conv_b=m_ffn_conv_b, ffn_w_down=m_ffn_w_down,
                 norm_g=m_norm_g, ada_w=m_ada_w, ada_b=m_ada_b, final_g=m_final_g)
    mom_v = dict(gm_w_in=v_gm_w_in, gm_ln_g=v_gm_ln_g, gm_ln_b=v_gm_ln_b, gm_w_s=v_gm_w_s, gm_b_s=v_gm_b_s,
                 gm_w_out=v_gm_w_out, hg_w_in=v_hg_w_in, hg_lb=v_hg_lb, hg_gn_g=v_hg_gn_g, hg_w_out=v_hg_w_out,
                 ffn_w_up=v_ffn_w_up, ffn_conv_w=v_ffn_conv_w, ffn_conv_b=v_ffn_conv_b, ffn_w_down=v_ffn_w_down,
                 norm_g=v_norm_g, ada_w=v_ada_w, ada_b=v_ada_b, final_g=v_final_g)
    order = list(weights)

    pieces = [c, hg_lb, hg_gn_g, norm_g, ffn_conv_w]
    payload, offs = _pack(pieces)
    got = _allgather_small(payload, "gather_small")
    c_g, lb_g, gn_g, ng_g, cw_g = _unpack(got, offs, [p.shape for p in pieces])
    c_all = c_g.reshape(N_DEV, Dm)
    P = dict(hg_lb=_from_chips(lb_g, 1), hg_gn_g=_from_chips(gn_g, 1), norm_g=_from_chips(ng_g, 2),
             ffn_conv_w=_from_chips(cw_g, 2), gm_ln_g=gm_ln_g, gm_ln_b=gm_ln_b, gm_w_s=gm_w_s, gm_b_s=gm_b_s,
             ffn_conv_b=ffn_conv_b, final_g=final_g)

    cols = ada_w.shape[2]
    ada_b_sh = lax.dynamic_slice_in_dim(ada_b, chip * cols, cols, axis=1)
    mod_sh = _ada_fwd(c_all, ada_w, ada_b_sh, "ada_fwd")
    mod_g = _allgather_small(mod_sh.reshape(-1, LANES), "gather_mod").reshape(N_DEV, DEPTH, N_DEV, cols)
    mod_mine = lax.dynamic_index_in_dim(mod_g[0::2], dev, axis=2, keepdims=False)
    mod = jnp.transpose(mod_mine, (1, 0, 2)).reshape(DEPTH, N_CHIPS * cols)

    shards = [weights[k].astype(BF16) for k in BIG]
    full = _gather_weights(shards, [COL_SHARDED[k] for k in BIG], "gather_weights")
    W = dict(zip(BIG, full))

    loss_part, dx, G, small = _local_step(x[0], loss_target[0], mod, W, P)

    g4, names = [], []
    for k in BIG:
        for l, g in enumerate(G[k]):
            if not COL_SHARDED[k]:
                R, C = g.shape
                g = g.reshape(N_CHIPS, 2, R // (2 * N_CHIPS), C)
            g4.append(g)
            names.append(f"{k}{l}")
    other = 1 - ci
    to_sib = [lax.dynamic_index_in_dim(g, other, axis=1, keepdims=False).astype(BF16) for g in g4]
    from_sib = _swap_sibling(to_sib, "reduce_swap")
    core = jnp.reshape(ci, (1,)).astype(jnp.int32)
    chip_sums = [_add_own_half(g, r, core, f"chip_sum_{nm}") for g, r, nm in zip(g4, from_sib, names)]
    by_src = _exchange_chips(chip_sums, "reduce_exchange")
    own_halves, pos = [], 0
    for k in BIG:
        own_halves.append(_sum_chips(by_src[pos:pos + len(G[k])], f"sum_chips_{k}"))
        pos += len(G[k])
    sib_halves = _swap_sibling(own_halves, "reduce_join")
    grads, deltas, new_m, new_v = {}, {}, {}, {}
    for k, own, recv in zip(BIG, own_halves, sib_halves):
        grads[k], deltas[k], new_m[k], new_v[k] = _adamw_halves(
            weights[k], own, recv, mom_m[k], mom_v[k], core, f"adamw_{k}")

    sum_pieces = [loss_part[:, :1], small["final_g"], jnp.stack(small["gm_ln_g"]), jnp.stack(small["gm_ln_b"]),
                  jnp.stack(small["gm_w_s"]), jnp.stack(small["gm_b_s"]), jnp.stack(small["ffn_conv_b"]),
                  small["hg_lb"], jnp.stack(small["hg_gn_g"]), jnp.stack(small["norm_g"]),
                  jnp.stack(small["ffn_conv_w"])]
    dmod = jnp.concatenate(small["dmod"], axis=0)
    payload2, offs2 = _pack(sum_pieces + [dmod])
    got2 = _allgather_small(payload2, "gather_grads")
    dmod_all = _unpack(got2, offs2[-1:], [dmod.shape])[0]
    summed = _sum_devices(got2, "sum_devices")
    (loss_s, d_final_g, d_ln_g, d_ln_b, d_ws, d_bs, d_cb, d_lb, d_gn, d_ng, d_cw) = _unpack(
        summed, offs2[:-1], [(1,), final_g.shape, gm_ln_g.shape, gm_ln_b.shape, gm_w_s.shape, gm_b_s.shape,
                             ffn_conv_b.shape, (2, Dm), (2, Dm), (DEPTH, 2, Dm), (DEPTH, 3, 2 * FFN_HIDDEN)])
    grads.update(final_g=d_final_g, gm_ln_g=d_ln_g, gm_ln_b=d_ln_b, gm_w_s=d_ws, gm_b_s=d_bs, ffn_conv_b=d_cb)
    grads["hg_lb"] = lax.dynamic_slice_in_dim(d_lb, chip * hg_lb.shape[1], hg_lb.shape[1], axis=1)
    grads["hg_gn_g"] = lax.dynamic_slice_in_dim(d_gn, chip * hg_gn_g.shape[1], hg_gn_g.shape[1], axis=1)
    grads["norm_g"] = lax.dynamic_slice_in_dim(d_ng, chip * norm_g.shape[2], norm_g.shape[2], axis=2)
    grads["ffn_conv_w"] = lax.dynamic_slice_in_dim(d_cw, chip * ffn_conv_w.shape[2], ffn_conv_w.shape[2], axis=2)
    dmod_sh = lax.dynamic_slice_in_dim(dmod_all, chip * cols, cols, axis=2)
    grads["ada_w"] = _ada_bwd(c_all, jnp.transpose(dmod_sh, (1, 0, 2)), "ada_bwd")
    grads["ada_b"] = _sum_devices(dmod_all.reshape(N_DEV, -1, LANES), "sum_ada_b").reshape(ada_b.shape)

    for k in order:
        if k in BIG:
            continue
        w = weights[k]
        shp = w.shape
        view = (-1, shp[-1]) if w.ndim > 1 else (8, -1)
        d, m2, v2 = _adamw(w.reshape(view), grads[k].reshape(view), mom_m[k].reshape(view), mom_v[k].reshape(view),
                           f"adamw_{k}")
        deltas[k], new_m[k], new_v[k] = d.reshape(shp), m2.reshape(shp), v2.reshape(shp)
        grads[k] = grads[k].reshape(shp)

    loss = loss_s.reshape(())
    return (loss, dx[None], *[grads[k] for k in order], *[deltas[k] for k in order],
            *[new_m[k] for k in order], *[new_v[k] for k in order])
```

```python
import functools

import jax
import jax.numpy as jnp
from jax import lax
from jax.experimental import pallas as pl
from jax.experimental.pallas import tpu as pltpu

F32 = jnp.float32
BF16 = jnp.bfloat16
HI = lax.Precision.HIGHEST
X3 = lax.Precision.HIGH
MESH = pl.DeviceIdType.MESH

D_MODEL = 1024
DEPTH = 4
EPS = 1e-6
GM_WIDTH = 2048
GM_HEADS = 8
GM_BLOCK = 128
GM_HEAD_DIM = 256
CHUNK = 64
HG_HEADS = 8
HG_DIM = 128
FFN_HIDDEN = 2816
N_CHIPS = 4
N_DEV = 8

ADAM_LR = 0.001
ADAM_B1 = 0.9
ADAM_B2 = 0.999
ADAM_EPS = 1e-08
ADAM_WD = 0.01
ADAM_STEP = 10

VMEM_LIMIT_BYTES = 56 * 1024 * 1024
ROW_TILE = 256
LANES = 128

_SQRT_HALF = 0.7071067811865476
_INV_SQRT_2PI = 0.3989422804014327


def _pick(dim, prefs):
    for p in prefs:
        if dim % p == 0:
            return p
    return dim


def _params(sem):
    return pltpu.CompilerParams(dimension_semantics=sem, vmem_limit_bytes=VMEM_LIMIT_BYTES)


def _cdf(x):
    return 0.5 * (1.0 + lax.erf(x * _SQRT_HALF))


def _pdf(x):
    return jnp.exp(-0.5 * x * x) * _INV_SQRT_2PI


def _sig(x):
    return jax.nn.sigmoid(x)


def _dot(a, b, dims, prec=None):
    return lax.dot_general(a, b, (dims, ((), ())), precision=prec, preferred_element_type=F32)


NN = ((1,), (0,))
NT = ((1,), (1,))
TN = ((0,), (0,))


def _mm(a, b, mode, name, b_layer=None, out_dtype=F32, exchange_out=False):
    b2 = b.shape[-2:]
    if mode == "nn":
        (M, K), (_, N) = a.shape, b2
    elif mode == "nt":
        (M, K), (N, _) = a.shape, b2
    else:
        (K, M), (_, N) = a.shape, b2
    tm = _pick(M, (1408, 512, 256, 128) if mode == "tn" else (1024, 512, 256, 128))
    tn = _pick(N, (1408, 1024, 512, 256, 128))
    tk = _pick(K, (1408, 1024, 512, 256, 128))
    nk = K // tk
    dims = {"nn": NN, "nt": NT, "tn": TN}[mode]

    def body(a_ref, b_ref, o_ref, *scratch):
        part = _dot(a_ref[...].astype(BF16), b_ref[...].astype(BF16), dims)
        if nk == 1:
            o_ref[...] = part.astype(o_ref.dtype)
            return
        (acc_ref,) = scratch
        k = pl.program_id(2)

        @pl.when(k == 0)
        def _():
            acc_ref[...] = part

        @pl.when(k > 0)
        def _():
            acc_ref[...] += part

        @pl.when(k == nk - 1)
        def _():
            o_ref[...] = acc_ref[...].astype(o_ref.dtype)

    if mode == "tn":
        a_spec = pl.BlockSpec((tk, tm), lambda i, j, k: (k, i))
    else:
        a_spec = pl.BlockSpec((tm, tk), lambda i, j, k: (i, k))
    bblk = (tk, tn) if mode in ("nn", "tn") else (tn, tk)
    bidx = (lambda i, j, k: (k, j)) if mode in ("nn", "tn") else (lambda i, j, k: (j, k))
    if b_layer is None:
        b_spec = pl.BlockSpec(bblk, bidx)
    else:
        b_spec = pl.BlockSpec((None,) + bblk, lambda i, j, k: (b_layer,) + bidx(i, j, k))
    if exchange_out:
        mh, cw = M // 2, N // N_CHIPS
        assert mh % tm == 0 and cw % tn == 0
        out_shape = jax.ShapeDtypeStruct((N_CHIPS, 2, mh, cw), out_dtype)
        o_spec = pl.BlockSpec(
            (None, None, tm, tn),
            lambda i, j, k: (j // (cw // tn), i // (mh // tm), i % (mh // tm), j % (cw // tn)))
    else:
        out_shape = jax.ShapeDtypeStruct((M, N), out_dtype)
        o_spec = pl.BlockSpec((tm, tn), lambda i, j, k: (i, j))
    return pl.pallas_call(
        body, name=name, out_shape=out_shape, grid=(M // tm, N // tn, nk),
        in_specs=[a_spec, b_spec], out_specs=o_spec,
        scratch_shapes=[] if nk == 1 else [pltpu.VMEM((tm, tn), F32)],
        compiler_params=_params(("parallel", "parallel", "arbitrary")),
    )(a, b)


def _row_spec(tr, width):
    return pl.BlockSpec((tr, width), lambda i: (i, 0))


def _vec_spec(width, rows=1):
    return pl.BlockSpec((rows, width), lambda i: (0, 0))


def _norm_fwd(x, y, gate, g, sc, sh, name):
    T, Dm = x.shape
    tr = _pick(T, (ROW_TILE,))
    has_res = y is not None

    def body(*refs):
        if has_res:
            x_ref, y_ref, gate_ref, g_ref, sc_ref, sh_ref, xo_ref, h_ref = refs
            xv = x_ref[...] + gate_ref[...] * y_ref[...]
            xo_ref[...] = xv
        else:
            x_ref, g_ref, sc_ref, sh_ref, h_ref = refs
            xv = x_ref[...]
        rstd = lax.rsqrt(jnp.mean(xv * xv, axis=-1, keepdims=True) + EPS)
        h_ref[...] = ((xv * rstd * g_ref[...]) * (1.0 + sc_ref[...]) + sh_ref[...]).astype(BF16)

    row, vec = _row_spec(tr, Dm), _vec_spec(Dm)
    if has_res:
        ins, in_specs = (x, y, gate, g, sc, sh), [row, row, vec, vec, vec, vec]
        out_shape = (jax.ShapeDtypeStruct((T, Dm), F32), jax.ShapeDtypeStruct((T, Dm), BF16))
        out_specs = (row, row)
    else:
        ins, in_specs = (x, g, sc, sh), [row, vec, vec, vec]
        out_shape = jax.ShapeDtypeStruct((T, Dm), BF16)
        out_specs = row
    out = pl.pallas_call(body, name=name, out_shape=out_shape, grid=(T // tr,), in_specs=in_specs,
                         out_specs=out_specs, compiler_params=_params(("parallel",)))(*ins)
    return out if has_res else (x, out)


def _norm_bwd(x, dh, dxo, g, sc, y_prev, gate_prev, name):
    T, Dm = x.shape
    tr = _pick(T, (ROW_TILE,))
    has_prev = y_prev is not None

    def body(*refs):
        if has_prev:
            x_ref, dh_ref, dxo_ref, g_ref, sc_ref, yp_ref, gp_ref, dx_ref, dyp_ref, s1_ref, s2_ref, s3_ref = refs
        else:
            x_ref, dh_ref, dxo_ref, g_ref, sc_ref, dx_ref, s1_ref, s2_ref = refs

        @pl.when(pl.program_id(0) == 0)
        def _():
            s1_ref[...] = jnp.zeros_like(s1_ref)
            s2_ref[...] = jnp.zeros_like(s2_ref)
            if has_prev:
                s3_ref[...] = jnp.zeros_like(s3_ref)

        xv = x_ref[...]
        rstd = lax.rsqrt(jnp.mean(xv * xv, axis=-1, keepdims=True) + EPS)
        xhat = xv * rstd
        dh = dh_ref[...]
        dxhat = dh * (g_ref[...] * (1.0 + sc_ref[...]))
        dx = dxo_ref[...] + rstd * (dxhat - xhat * jnp.mean(dxhat * xhat, axis=-1, keepdims=True))
        dx_ref[...] = dx
        s1_ref[...] += jnp.sum(dh, axis=0, keepdims=True)
        s2_ref[...] += jnp.sum(dh * xhat, axis=0, keepdims=True)
        if has_prev:
            dyp_ref[...] = (gp_ref[...] * dx).astype(BF16)
            s3_ref[...] += jnp.sum(dx * yp_ref[...], axis=0, keepdims=True)

    row, vec = _row_spec(tr, Dm), _vec_spec(Dm)
    vshape = jax.ShapeDtypeStruct((1, Dm), F32)
    if has_prev:
        ins, in_specs = (x, dh, dxo, g, sc, y_prev, gate_prev), [row, row, row, vec, vec, row, vec]
        out_shape = (jax.ShapeDtypeStruct((T, Dm), F32), jax.ShapeDtypeStruct((T, Dm), BF16), vshape, vshape, vshape)
        out_specs = (row, row, vec, vec, vec)
    else:
        ins, in_specs = (x, dh, dxo, g, sc), [row, row, row, vec, vec]
        out_shape = (jax.ShapeDtypeStruct((T, Dm), F32), vshape, vshape)
        out_specs = (row, vec, vec)
    return pl.pallas_call(body, name=name, out_shape=out_shape, grid=(T // tr,), in_specs=in_specs,
                          out_specs=out_specs, compiler_params=_params(("arbitrary",)))(*ins)


def _loss_head(x, y, gate, fg, tgt, name):
    T, Dm = x.shape
    tr = _pick(T, (ROW_TILE,))
    nsteps = T // tr

    def body(x_ref, y_ref, gate_ref, fg_ref, t_ref, dx_ref, dy_ref, loss_ref, sfg_ref, sg_ref, acc_ref):
        i = pl.program_id(0)

        @pl.when(i == 0)
        def _():
            acc_ref[...] = jnp.zeros_like(acc_ref)
            sfg_ref[...] = jnp.zeros_like(sfg_ref)
            sg_ref[...] = jnp.zeros_like(sg_ref)

        yv = y_ref[...]
        xv = x_ref[...] + gate_ref[...] * yv
        rstd = lax.rsqrt(jnp.mean(xv * xv, axis=-1, keepdims=True) + EPS)
        xhat = xv * rstd
        err = xhat * fg_ref[...] - t_ref[...]
        acc_ref[...] += jnp.sum(err * err, axis=0, keepdims=True)
        dyn = err * (1.0 / Dm)
        sfg_ref[...] += jnp.sum(dyn * xhat, axis=0, keepdims=True)
        dxhat = dyn * fg_ref[...]
        dx = rstd * (dxhat - xhat * jnp.mean(dxhat * xhat, axis=-1, keepdims=True))
        dx_ref[...] = dx
        dy_ref[...] = (gate_ref[...] * dx).astype(BF16)
        sg_ref[...] += jnp.sum(dx * yv, axis=0, keepdims=True)

        @pl.when(i == nsteps - 1)
        def _():
            total = jnp.sum(acc_ref[...], axis=1, keepdims=True) * (0.5 / Dm)
            loss_ref[...] = jnp.broadcast_to(total, loss_ref.shape)

    row, vec = _row_spec(tr, Dm), _vec_spec(Dm)
    vshape = jax.ShapeDtypeStruct((1, Dm), F32)
    return pl.pallas_call(
        body, name=name, grid=(nsteps,),
        out_shape=(jax.ShapeDtypeStruct((T, Dm), F32), jax.ShapeDtypeStruct((T, Dm), BF16),
                   jax.ShapeDtypeStruct((1, LANES), F32), vshape, vshape),
        in_specs=[row, row, vec, vec, row], out_specs=(row, row, _vec_spec(LANES), vec, vec),
        scratch_shapes=[pltpu.VMEM((1, Dm), F32)], compiler_params=_params(("arbitrary",)),
    )(x, y, gate, fg, tgt)


def _spatial_mask():
    r = lax.broadcasted_iota(jnp.int32, (GM_BLOCK, GM_BLOCK), 0) // CHUNK
    c = lax.broadcasted_iota(jnp.int32, (GM_BLOCK, GM_BLOCK), 1) // CHUNK
    return r >= c


def _gm_specs(tr):
    return [_row_spec(tr, 2 * GM_WIDTH), _vec_spec(GM_WIDTH), _vec_spec(GM_WIDTH),
            pl.BlockSpec((GM_HEADS, GM_BLOCK, GM_BLOCK), lambda i: (0, 0, 0)),
            pl.BlockSpec((GM_HEADS, GM_BLOCK, 1), lambda i: (0, 0, 0))]


def _gm_mid_fwd(a, ln_g, ln_b, ws, bs3, name):
    T = a.shape[0]
    tr = _pick(T, (ROW_TILE,))
    W = GM_WIDTH

    def body(a_ref, lg_ref, lb_ref, ws_ref, bs_ref, o_ref, vn_scr):
        av = a_ref[:, W:]
        v = av * _cdf(av)
        vc = v - jnp.mean(v, axis=-1, keepdims=True)
        rstd = lax.rsqrt(jnp.mean(vc * vc, axis=-1, keepdims=True) + EPS)
        vn_scr[...] = (vc * rstd * lg_ref[...] + lb_ref[...]).astype(BF16)
        mask = _spatial_mask()
        for h in range(GM_HEADS):
            w = jnp.where(mask, ws_ref[h], 0.0).astype(BF16)
            cs = slice(h * GM_HEAD_DIM, (h + 1) * GM_HEAD_DIM)
            for blk in range(tr // GM_BLOCK):
                rs = slice(blk * GM_BLOCK, (blk + 1) * GM_BLOCK)
                s = _dot(w, vn_scr[rs, cs], NN) + bs_ref[h]
                au = a_ref[rs, cs]
                o_ref[rs, cs] = (au * _cdf(au) * s).astype(BF16)

    return pl.pallas_call(
        body, name=name, out_shape=jax.ShapeDtypeStruct((T, W), BF16), grid=(T // tr,),
        in_specs=_gm_specs(tr), out_specs=_row_spec(tr, W),
        scratch_shapes=[pltpu.VMEM((tr, W), BF16)], compiler_params=_params(("parallel",)),
    )(a, ln_g, ln_b, ws, bs3)


def _gm_mid_bwd(a, dgated, ln_g, ln_b, ws, bs3, name):
    T = a.shape[0]
    tr = _pick(T, (ROW_TILE,))
    W = GM_WIDTH
    nsteps = T // tr

    def body(a_ref, dg_ref, lg_ref, lb_ref, ws_ref, bs_ref, da_ref, dws_ref, dbs_ref, dlg_ref, dlb_ref,
             vn_scr, vhat_scr, dvn_scr, dsum_scr):
        i = pl.program_id(0)

        @pl.when(i == 0)
        def _():
            dws_ref[...] = jnp.zeros_like(dws_ref)
            dbs_ref[...] = jnp.zeros_like(dbs_ref)
            dlg_ref[...] = jnp.zeros_like(dlg_ref)
            dlb_ref[...] = jnp.zeros_like(dlb_ref)
            dsum_scr[...] = jnp.zeros_like(dsum_scr)

        av = a_ref[:, W:]
        cdf_v = _cdf(av)
        v = av * cdf_v
        vc = v - jnp.mean(v, axis=-1, keepdims=True)
        rstd = lax.rsqrt(jnp.mean(vc * vc, axis=-1, keepdims=True) + EPS)
        vhat_scr[...] = vc * rstd
        vn_scr[...] = (vhat_scr[...] * lg_ref[...] + lb_ref[...]).astype(BF16)
        mask = _spatial_mask()
        for h in range(GM_HEADS):
            w = jnp.where(mask, ws_ref[h], 0.0).astype(BF16)
            cs = slice(h * GM_HEAD_DIM, (h + 1) * GM_HEAD_DIM)
            for blk in range(tr // GM_BLOCK):
                rs = slice(blk * GM_BLOCK, (blk + 1) * GM_BLOCK)
                vnb = vn_scr[rs, cs]
                s = _dot(w, vnb, NN) + bs_ref[h]
                au = a_ref[rs, cs]
                cdf_u = _cdf(au)
                dg = dg_ref[rs, cs]
                ds = dg * (au * cdf_u)
                da_ref[rs, cs] = (dg * s * (cdf_u + au * _pdf(au))).astype(BF16)
                dsb = ds.astype(BF16)
                dvn_scr[rs, cs] = _dot(w, dsb, TN)
                dws_ref[h] += _dot(dsb, vnb, NT)
                dsum_scr[:, cs] += ds
        dvn = dvn_scr[...]
        vhat = vhat_scr[...]
        dlg_ref[...] += jnp.sum(dvn * vhat, axis=0, keepdims=True)
        dlb_ref[...] += jnp.sum(dvn, axis=0, keepdims=True)
        dvh = dvn * lg_ref[...]
        dv = rstd * (dvh - jnp.mean(dvh, axis=-1, keepdims=True)
                     - vhat * jnp.mean(dvh * vhat, axis=-1, keepdims=True))
        da_ref[:, W:] = (dv * (cdf_v + av * _pdf(av))).astype(BF16)

        @pl.when(i == nsteps - 1)
        def _():
            for h in range(GM_HEADS):
                dws_ref[h] = jnp.where(mask, dws_ref[h], 0.0)
            col_head = lax.broadcasted_iota(jnp.int32, (W, GM_BLOCK), 0) // GM_HEAD_DIM
            sel = (col_head == lax.broadcasted_iota(jnp.int32, (W, GM_BLOCK), 1)).astype(F32)
            dbs_ref[...] = _dot(dsum_scr[...], sel, NN, HI)

    vshape = jax.ShapeDtypeStruct((1, W), F32)
    return pl.pallas_call(
        body, name=name, grid=(nsteps,),
        out_shape=(jax.ShapeDtypeStruct((T, 2 * W), BF16), jax.ShapeDtypeStruct((GM_HEADS, GM_BLOCK, GM_BLOCK), F32),
                   jax.ShapeDtypeStruct((GM_BLOCK, GM_BLOCK), F32), vshape, vshape),
        in_specs=[_gm_specs(tr)[0], _row_spec(tr, W)] + _gm_specs(tr)[1:],
        out_specs=(_row_spec(tr, 2 * W), pl.BlockSpec((GM_HEADS, GM_BLOCK, GM_BLOCK), lambda i: (0, 0, 0)),
                   pl.BlockSpec((GM_BLOCK, GM_BLOCK), lambda i: (0, 0)), _vec_spec(W), _vec_spec(W)),
        scratch_shapes=[pltpu.VMEM((tr, W), BF16), pltpu.VMEM((tr, W), F32), pltpu.VMEM((tr, W), F32),
                        pltpu.VMEM((GM_BLOCK, W), F32)],
        compiler_params=_params(("arbitrary",)),
    )(a, dgated, ln_g, ln_b, ws, bs3)


SUB = 16
EXP_CLAMP = 80.0


def _tri(lower):
    r = lax.broadcasted_iota(jnp.int32, (CHUNK, CHUNK), 0)
    c = lax.broadcasted_iota(jnp.int32, (CHUNK, CHUNK), 1)
    return (r >= c) if lower else (c >= r)


def _score_masks():
    i = lax.broadcasted_iota(jnp.int32, (CHUNK, CHUNK), 0)
    j = lax.broadcasted_iota(jnp.int32, (CHUNK, CHUNK), 1)
    bi, bj = i // SUB, j // SUB
    diag = (bi == bj) & (i >= j)
    pair = (bi % 2 == 1) & (bj == bi - 1)
    half = (i >= CHUNK // 2) & (j < CHUNK // 2)
    return diag, pair, half


def _dot01(m, x):
    x1 = x.astype(BF16)
    rest = x - x1.astype(F32)
    x2 = rest.astype(BF16)
    x3 = (rest - x2.astype(F32)).astype(BF16)
    return _dot(m, x1, NN) + (_dot(m, x2, NN) + _dot(m, x3, NN))


def _block_rows(b, offset):
    parts = []
    for blk in range(0, CHUNK, SUB):
        r = blk + offset
        parts.append(jnp.zeros((SUB, b.shape[1]), F32) if r < 0 else jnp.broadcast_to(b[r:r + 1], (SUB, b.shape[1])))
    return jnp.concatenate(parts, axis=0)


def _hg_gates(p_ref, lb_ref, h, lower):
    Dm = D_MODEL
    c0 = h * HG_DIM
    qr = p_ref[:, c0:c0 + HG_DIM]
    fz = p_ref[:, Dm + c0:Dm + c0 + HG_DIM]
    v = p_ref[:, 2 * Dm + c0:2 * Dm + c0 + HG_DIM]
    gt = p_ref[:, 3 * Dm + c0:3 * Dm + c0 + HG_DIM]
    lbh = lb_ref[:, c0:c0 + HG_DIM]
    sg = _sig(fz)
    f = lbh + (1.0 - lbh) * sg
    gl = jnp.log(f)
    kk = 1.0 - f
    sq = _sig(qr)
    q = qr * sq
    b = _dot01(lower, gl)
    r_mid = _block_rows(b, SUB // 2 - 1)
    r_prev = _block_rows(b, -1)
    r_end = _block_rows(b, SUB - 1)
    r_half = jnp.broadcast_to(b[CHUNK // 2 - 1:CHUNK // 2], b.shape)
    bc = b[CHUNK - 1:CHUNK]
    eqs = (jnp.exp(jnp.clip(b - r_mid, -EXP_CLAMP, EXP_CLAMP)), jnp.exp(jnp.minimum(b - r_prev, 0.0)),
           jnp.exp(jnp.minimum(b - r_half, 0.0)))
    eks = (jnp.exp(jnp.clip(r_mid - b, -EXP_CLAMP, EXP_CLAMP)), jnp.exp(jnp.minimum(r_end - b, 0.0)),
           jnp.exp(jnp.minimum(r_half - b, 0.0)))
    eb = jnp.exp(b)
    ec = jnp.exp(bc - b)
    return dict(qr=qr, v=v, gt=gt, lbh=lbh, sg=sg, f=f, kk=kk, sq=sq, q=q, eqs=eqs, eks=eks, eb=eb, ec=ec,
                e_end=jnp.exp(bc), qs=[q * e for e in eqs], ks=[kk * e for e in eks], qe=q * eb, ke=kk * ec)


def _scores(g, masks):
    a = None
    for qs, ks, m in zip(g["qs"], g["ks"], masks):
        part = jnp.where(m, _dot(qs.astype(BF16), ks.astype(BF16), NT), 0.0)
        a = part if a is None else a + part
    return a


def _hg_scan_fwd(p, lb, gn, name):
    T = p.shape[0]
    nc = T // CHUNK
    Dm = D_MODEL

    def body(p_ref, lb_ref, gn_ref, o_ref, og_ref, so_ref, st_ref):
        @pl.when(pl.program_id(0) == 0)
        def _():
            st_ref[...] = jnp.zeros_like(st_ref)

        masks = _score_masks()
        lower = _tri(True).astype(BF16)
        for h in range(HG_HEADS):
            g = _hg_gates(p_ref, lb_ref, h, lower)
            cs = slice(h * HG_DIM, (h + 1) * HG_DIM)
            a = _scores(g, masks)
            st = st_ref[h]
            so_ref[0, h] = st
            o = _dot(a.astype(BF16), g["v"].astype(BF16), NN) + _dot(g["qe"], st, NT, X3)
            st_ref[h] = st * g["e_end"] + _dot(g["v"], g["ke"], TN, X3)
            o_ref[:, cs] = o
            r = lax.rsqrt(jnp.mean(o * o, axis=-1, keepdims=True) + EPS)
            gt = g["gt"]
            og_ref[:, cs] = (((o * r) * gn_ref[:, cs]).astype(F32) * (gt * _sig(gt))).astype(BF16)

    return pl.pallas_call(
        body, name=name, grid=(nc,),
        out_shape=(jax.ShapeDtypeStruct((T, Dm), F32), jax.ShapeDtypeStruct((T, Dm), BF16),
                   jax.ShapeDtypeStruct((nc, HG_HEADS, HG_DIM, HG_DIM), F32)),
        in_specs=[_row_spec(CHUNK, 4 * Dm), _vec_spec(Dm), _vec_spec(Dm)],
        out_specs=(_row_spec(CHUNK, Dm), _row_spec(CHUNK, Dm),
                   pl.BlockSpec((1, HG_HEADS, HG_DIM, HG_DIM), lambda i: (i, 0, 0, 0))),
        scratch_shapes=[pltpu.VMEM((HG_HEADS, HG_DIM, HG_DIM), F32)],
        compiler_params=_params(("arbitrary",)),
    )(p, lb, gn)


def _hg_scan_bwd(p, lb, gn, o, dog, states, name):
    T = p.shape[0]
    nc = T // CHUNK
    Dm = D_MODEL

    def rev(i):
        return nc - 1 - i

    def body(p_ref, lb_ref, gn_ref, o_ref, dog_ref, st_in_ref, dp_ref, dlb_ref, dgn_ref, dst_ref, carry_ref):
        @pl.when(pl.program_id(0) == 0)
        def _():
            dst_ref[...] = jnp.zeros_like(dst_ref)
            carry_ref[...] = jnp.zeros_like(carry_ref)
            dlb_ref[...] = jnp.zeros_like(dlb_ref)
            dgn_ref[...] = jnp.zeros_like(dgn_ref)

        upper = _tri(False).astype(BF16)
        lower = _tri(True).astype(BF16)
        masks = _score_masks()
        for h in range(HG_HEADS):
            g = _hg_gates(p_ref, lb_ref, h, lower)
            c0 = h * HG_DIM
            cs = slice(c0, c0 + HG_DIM)
            oh = o_ref[:, cs]
            r = lax.rsqrt(jnp.mean(oh * oh, axis=-1, keepdims=True) + EPS)
            on = oh * r
            gt = g["gt"]
            sgt = _sig(gt)
            sil = gt * sgt
            dogh = dog_ref[:, cs]
            gnh = gn_ref[:, cs]
            don = dogh * gnh * sil
            dgn_ref[:, cs] += jnp.sum(dogh * on * sil, axis=0, keepdims=True)
            dgate = dogh * on * gnh * (sgt * (1.0 + gt * (1.0 - sgt)))
            do = r * (don - on * jnp.mean(don * on, axis=-1, keepdims=True))
            a = _scores(g, masks)
            dst = dst_ref[h]
            dob = do.astype(BF16)
            da = _dot(dob, g["v"].astype(BF16), NT)
            dv = _dot(a.astype(BF16), dob, TN) + _dot(g["ke"].astype(BF16), dst.astype(BF16), NT)
            dq = _dot(do, st_in_ref[0, h], NN, X3) * g["eb"]
            dk = _dot(g["v"], dst, NN, X3) * g["ec"]
            for qs, ks, eq, ek, m in zip(g["qs"], g["ks"], g["eqs"], g["eks"], masks):
                dam = jnp.where(m, da, 0.0)
                dq = dq + _dot(dam, ks, NN, X3) * eq
                dk = dk + _dot(dam, qs, TN, X3) * ek
            dst_ref[h] = dst * g["e_end"] + _dot(do, g["qe"], TN, X3)
            dgd = g["q"] * dq - g["kk"] * dk
            dgl = _dot01(upper, dgd) + carry_ref[h]
            carry_ref[h] += jnp.sum(dgd, axis=0, keepdims=True)
            df = dgl / g["f"] - dk
            sg = g["sg"]
            dlb_ref[:, cs] += jnp.sum(df * (1.0 - sg), axis=0, keepdims=True)
            sq, qr = g["sq"], g["qr"]
            dp_ref[:, c0:c0 + HG_DIM] = (dq * (sq * (1.0 + qr * (1.0 - sq)))).astype(BF16)
            dp_ref[:, Dm + c0:Dm + c0 + HG_DIM] = (df * (1.0 - g["lbh"]) * sg * (1.0 - sg)).astype(BF16)
            dp_ref[:, 2 * Dm + c0:2 * Dm + c0 + HG_DIM] = dv.astype(BF16)
            dp_ref[:, 3 * Dm + c0:3 * Dm + c0 + HG_DIM] = dgate.astype(BF16)

    vshape = jax.ShapeDtypeStruct((1, Dm), F32)
    rrow = lambda w: pl.BlockSpec((CHUNK, w), lambda i: (rev(i), 0))
    return pl.pallas_call(
        body, name=name, grid=(nc,),
        out_shape=(jax.ShapeDtypeStruct((T, 4 * Dm), BF16), vshape, vshape),
        in_specs=[rrow(4 * Dm), _vec_spec(Dm), _vec_spec(Dm), rrow(Dm), rrow(Dm),
                  pl.BlockSpec((1, HG_HEADS, HG_DIM, HG_DIM), lambda i: (rev(i), 0, 0, 0))],
        out_specs=(rrow(4 * Dm), _vec_spec(Dm), _vec_spec(Dm)),
        scratch_shapes=[pltpu.VMEM((HG_HEADS, HG_DIM, HG_DIM), F32), pltpu.VMEM((HG_HEADS, 1, HG_DIM), F32)],
        compiler_params=_params(("arbitrary",)),
    )(p, lb, gn, o, dog, states)


def _lb_fwd(hg_lb, name):
    def body(a_ref, o_ref):
        a0, a1 = a_ref[0:1], a_ref[1:2]
        m = jnp.maximum(a0, a1)
        e0, e1 = jnp.exp(a0 - m), jnp.exp(a1 - m)
        p0, p1 = e0 / (e0 + e1), e1 / (e0 + e1)
        o_ref[0:1] = p0 - p0
        o_ref[1:2] = (p0 + p1) - p0

    return pl.pallas_call(body, name=name, out_shape=jax.ShapeDtypeStruct(hg_lb.shape, F32))(hg_lb)


def _lb_bwd(hg_lb, dlb_all, name):
    def body(a_ref, d_ref, o_ref):
        a0, a1 = a_ref[0:1], a_ref[1:2]
        m = jnp.maximum(a0, a1)
        e0, e1 = jnp.exp(a0 - m), jnp.exp(a1 - m)
        p0, p1 = e0 / (e0 + e1), e1 / (e0 + e1)
        d1 = d_ref[1:2]
        o_ref[0:1] = -p0 * p1 * d1
        o_ref[1:2] = p1 * (1.0 - p1) * d1

    return pl.pallas_call(body, name=name, out_shape=jax.ShapeDtypeStruct(hg_lb.shape, F32))(hg_lb, dlb_all)


CONV_COLS = 256


def _conv_fwd(a, w, b, name):
    T = a.shape[0]
    Fh = FFN_HIDDEN
    tr = _pick(T, (ROW_TILE,))
    cw = CONV_COLS
    hb = tr // 8

    def body(a_ref, ap_ref, w_ref, b_ref, m_ref):
        m0 = (pl.program_id(0) > 0).astype(F32)

        def conv(cc):
            x = jnp.concatenate([ap_ref[:, pl.ds(cc, cw)] * m0, a_ref[:, pl.ds(cc, cw)]], axis=0)
            wv = w_ref[:, pl.ds(cc, cw)]
            y = b_ref[:, pl.ds(cc, cw)] + wv[2:3] * x + wv[1:2] * pltpu.roll(x, 1, axis=0) \
                + wv[0:1] * pltpu.roll(x, 2, axis=0)
            return y[8:]

        def step(c, carry):
            c0 = pl.multiple_of(c * cw, cw)
            c1 = pl.multiple_of(Fh + c * cw, cw)
            yg, yv = conv(c0), conv(c1)
            m_ref[:, pl.ds(c0, cw)] = (yg * _cdf(yg) * yv).astype(BF16)
            return carry

        lax.fori_loop(0, Fh // cw, step, 0)

    return pl.pallas_call(
        body, name=name, out_shape=jax.ShapeDtypeStruct((T, Fh), BF16), grid=(T // tr,),
        in_specs=[_row_spec(tr, 2 * Fh), pl.BlockSpec((8, 2 * Fh), lambda i: (jnp.maximum(i * hb - 1, 0), 0)),
                  _vec_spec(2 * Fh, 3), _vec_spec(2 * Fh)],
        out_specs=_row_spec(tr, Fh), compiler_params=_params(("parallel",)),
    )(a, a, w, b)


def _conv_bwd(a, dm, w, b, name):
    T = a.shape[0]
    Fh = FFN_HIDDEN
    tr = _pick(T, (ROW_TILE,))
    cw = CONV_COLS
    hb = tr // 8
    nsteps = T // tr
    n = tr + 8

    def body(a_ref, ap_ref, an_ref, dm_ref, dmn_ref, w_ref, b_ref, da_ref, dw_ref, db_ref):
        i = pl.program_id(0)
        m0 = (i > 0).astype(F32)
        m1 = (i < nsteps - 1).astype(F32)

        @pl.when(i == 0)
        def _():
            dw_ref[...] = jnp.zeros_like(dw_ref)
            db_ref[...] = jnp.zeros_like(db_ref)

        def prep(cc):
            x = jnp.concatenate([ap_ref[:, pl.ds(cc, cw)] * m0, a_ref[:, pl.ds(cc, cw)],
                                 an_ref[:, pl.ds(cc, cw)] * m1], axis=0)
            wv = w_ref[:, pl.ds(cc, cw)]
            s1 = pltpu.roll(x, 1, axis=0)
            s2 = pltpu.roll(x, 2, axis=0)
            y = b_ref[:, pl.ds(cc, cw)] + wv[2:3] * x + wv[1:2] * s1 + wv[0:1] * s2
            return wv, x[8:], s1[8:], s2[8:], y[8:]

        def back(cc, dy, wv, x0, s1, s2):
            da = wv[2:3] * dy + wv[1:2] * pltpu.roll(dy, n - 1, axis=0) + wv[0:1] * pltpu.roll(dy, n - 2, axis=0)
            da_ref[:, pl.ds(cc, cw)] = da[:tr].astype(BF16)
            d = dy[:tr]
            db_ref[:, pl.ds(cc, cw)] += jnp.sum(d, axis=0, keepdims=True)
            dw_ref[2:3, pl.ds(cc, cw)] += jnp.sum(d * x0[:tr], axis=0, keepdims=True)
            dw_ref[1:2, pl.ds(cc, cw)] += jnp.sum(d * s1[:tr], axis=0, keepdims=True)
            dw_ref[0:1, pl.ds(cc, cw)] += jnp.sum(d * s2[:tr], axis=0, keepdims=True)

        def step(c, carry):
            c0 = pl.multiple_of(c * cw, cw)
            c1 = pl.multiple_of(Fh + c * cw, cw)
            dmx = jnp.concatenate([dm_ref[:, pl.ds(c0, cw)], dmn_ref[:, pl.ds(c0, cw)] * m1], axis=0)
            wg, xg, s1g, s2g, yg = prep(c0)
            wv, xv, s1v, s2v, yv = prep(c1)
            cg = _cdf(yg)
            back(c0, dmx * yv * (cg + yg * _pdf(yg)), wg, xg, s1g, s2g)
            back(c1, dmx * (yg * cg), wv, xv, s1v, s2v)
            return carry

        lax.fori_loop(0, Fh // cw, step, 0)

    prev = lambda wd: pl.BlockSpec((8, wd), lambda i: (jnp.maximum(i * hb - 1, 0), 0))
    nxt = lambda wd: pl.BlockSpec((8, wd), lambda i: (jnp.minimum((i + 1) * hb, T // 8 - 1), 0))
    return pl.pallas_call(
        body, name=name, grid=(nsteps,),
        out_shape=(jax.ShapeDtypeStruct((T, 2 * Fh), BF16), jax.ShapeDtypeStruct((3, 2 * Fh), F32),
                   jax.ShapeDtypeStruct((1, 2 * Fh), F32)),
        in_specs=[_row_spec(tr, 2 * Fh), prev(2 * Fh), nxt(2 * Fh), _row_spec(tr, Fh), nxt(Fh),
                  _vec_spec(2 * Fh, 3), _vec_spec(2 * Fh)],
        out_specs=(_row_spec(tr, 2 * Fh), _vec_spec(2 * Fh, 3), _vec_spec(2 * Fh)),
        compiler_params=_params(("arbitrary",)),
    )(a, a, a, dm, dm, w, b)


def _ada_fwd(c_all, ada_w, ada_b, name):
    L, Dm, cols = ada_w.shape
    tn = _pick(cols, (512, 256, 128))

    def body(c_ref, w_ref, b_ref, o_ref):
        cv = c_ref[...]
        cond = (cv * _sig(cv)).astype(BF16)
        o_ref[...] = _dot(cond, w_ref[...].astype(BF16), NN) + b_ref[...]

    return pl.pallas_call(
        body, name=name, out_shape=jax.ShapeDtypeStruct((L, N_DEV, cols), F32), grid=(L, cols // tn),
        in_specs=[pl.BlockSpec((N_DEV, Dm), lambda l, j: (0, 0)), pl.BlockSpec((None, Dm, tn), lambda l, j: (l, 0, j)),
                  pl.BlockSpec((None, 1, tn), lambda l, j: (l, 0, j))],
        out_specs=pl.BlockSpec((None, N_DEV, tn), lambda l, j: (l, 0, j)),
        compiler_params=_params(("parallel", "parallel")),
    )(c_all, ada_w, ada_b.reshape(L, 1, cols))


def _ada_bwd(c_all, dmod, name):
    L, _, cols = dmod.shape
    Dm = c_all.shape[1]
    tn = _pick(cols, (512, 256, 128))

    def body(c_ref, d_ref, o_ref):
        cv = c_ref[...]
        o_ref[...] = _dot(cv * _sig(cv), d_ref[...], TN, HI)

    return pl.pallas_call(
        body, name=name, out_shape=jax.ShapeDtypeStruct((L, Dm, cols), F32), grid=(L, cols // tn),
        in_specs=[pl.BlockSpec((N_DEV, Dm), lambda l, j: (0, 0)), pl.BlockSpec((None, N_DEV, tn), lambda l, j: (l, 0, j))],
        out_specs=pl.BlockSpec((None, Dm, tn), lambda l, j: (l, 0, j)),
        compiler_params=_params(("parallel", "parallel")),
    )(c_all, dmod)


def _add_own_half(g4, rb, core, name):
    S, _, rh, cw = g4.shape
    tr = _pick(rh, (256, 128, 176, 64))

    def body(core_ref, g_ref, r_ref, o_ref):
        o_ref[...] = (g_ref[...] + r_ref[...].astype(F32)).astype(BF16)

    return pl.pallas_call(
        body, name=name, out_shape=jax.ShapeDtypeStruct((S, rh, cw), BF16),
        grid_spec=pltpu.PrefetchScalarGridSpec(
            num_scalar_prefetch=1, grid=(S, rh // tr),
            in_specs=[pl.BlockSpec((None, None, tr, cw), lambda s, i, core_ref: (s, core_ref[0], i, 0)),
                      pl.BlockSpec((None, tr, cw), lambda s, i, core_ref: (s, i, 0))],
            out_specs=pl.BlockSpec((None, tr, cw), lambda s, i, core_ref: (s, i, 0))),
        compiler_params=_params(("parallel", "parallel")),
    )(core, g4, rb)


def _sum_chips(lands, sums, chip, name):
    L = len(lands)
    _, rh, cw = lands[0].shape
    tr = _pick(rh, (256, 128, 176, 64))

    def body(chip_ref, *refs):
        ld, cs, o_ref = refs[:L], refs[L:2 * L], refs[2 * L]
        me = chip_ref[0]
        for k in range(L):
            @pl.when(pl.program_id(0) == k)
            def _(k=k):
                own = cs[k][...].astype(F32)
                got = [ld[k][j].astype(F32) for j in range(3)]
                acc = None
                for t in range(N_CHIPS):
                    d = jnp.bitwise_xor(jnp.int32(t), me)
                    term = jnp.where(d == 0, own, jnp.where(d == 2, got[0], jnp.where(d == 1, got[1], got[2])))
                    acc = term if acc is None else acc + term
                o_ref[...] = acc

    frozen = lambda l, i, k: jnp.where(l == k, i, 0)
    in_specs = [pl.BlockSpec((3, tr, cw), lambda l, i, chip_ref, k=k: (0, frozen(l, i, k), 0)) for k in range(L)]
    in_specs += [pl.BlockSpec((None, tr, cw), lambda l, i, chip_ref, k=k: (chip_ref[0], frozen(l, i, k), 0))
                 for k in range(L)]
    return pl.pallas_call(
        body, name=name, out_shape=jax.ShapeDtypeStruct((L, rh, cw), F32),
        grid_spec=pltpu.PrefetchScalarGridSpec(
            num_scalar_prefetch=1, grid=(L, rh // tr), in_specs=in_specs,
            out_specs=pl.BlockSpec((None, tr, cw), lambda l, i, chip_ref: (l, i, 0))),
        compiler_params=_params(("arbitrary", "arbitrary")),
    )(chip, *lands, *sums)


def _sum_devices(gathered, name):
    n, R, _ = gathered.shape
    tr = _pick(R, (512, 256, 128, 64, 32, 16, 8))

    def body(g_ref, o_ref):
        acc = g_ref[0]
        for d in range(1, n):
            acc = acc + g_ref[d]
        o_ref[...] = acc

    return pl.pallas_call(
        body, name=name, out_shape=jax.ShapeDtypeStruct((R, LANES), F32), grid=(R // tr,),
        in_specs=[pl.BlockSpec((n, tr, LANES), lambda i: (0, i, 0))], out_specs=pl.BlockSpec((tr, LANES), lambda i: (i, 0)),
        compiler_params=_params(("parallel",)),
    )(gathered)


def _adamw(w, g, m, v, name):
    R, C = w.shape
    tr = _pick(R, (256, 128, 64, 32, 16, 8))
    c1 = 1.0 / (1.0 - ADAM_B1 ** ADAM_STEP)
    c2 = 1.0 / (1.0 - ADAM_B2 ** ADAM_STEP)

    def body(w_ref, g_ref, m_ref, v_ref, d_ref, mo_ref, vo_ref):
        gv = g_ref[...]
        m2 = ADAM_B1 * m_ref[...] + (1.0 - ADAM_B1) * gv
        v2 = ADAM_B2 * v_ref[...] + (1.0 - ADAM_B2) * (gv * gv)
        mo_ref[...] = m2
        vo_ref[...] = v2
        d_ref[...] = -ADAM_LR * ((m2 * c1) / (jnp.sqrt(v2 * c2) + ADAM_EPS) + ADAM_WD * w_ref[...])

    spec = pl.BlockSpec((tr, C), lambda i: (i, 0))
    shp = jax.ShapeDtypeStruct((R, C), F32)
    return pl.pallas_call(body, name=name, out_shape=(shp, shp, shp), grid=(R // tr,), in_specs=[spec] * 4,
                          out_specs=(spec, spec, spec), compiler_params=_params(("parallel",)))(w, g, m, v)


def _adamw_halves(w, own, recv, m, v, core, name):
    L, rh, cw = own.shape
    tr = _pick(rh, (256, 128, 176, 64))
    c1 = 1.0 / (1.0 - ADAM_B1 ** ADAM_STEP)
    c2 = 1.0 / (1.0 - ADAM_B2 ** ADAM_STEP)

    def body(core_ref, w_ref, own_ref, recv_ref, m_ref, v_ref, g_ref, d_ref, mo_ref, vo_ref):
        gv = jnp.where(pl.program_id(1) == core_ref[0], own_ref[...], recv_ref[...])
        g_ref[...] = gv
        m2 = ADAM_B1 * m_ref[...] + (1.0 - ADAM_B1) * gv
        v2 = ADAM_B2 * v_ref[...] + (1.0 - ADAM_B2) * (gv * gv)
        mo_ref[...] = m2
        vo_ref[...] = v2
        d_ref[...] = -ADAM_LR * ((m2 * c1) / (jnp.sqrt(v2 * c2) + ADAM_EPS) + ADAM_WD * w_ref[...])

    full = pl.BlockSpec((None, None, tr, cw), lambda l, hf, i, core_ref: (l, hf, i, 0))
    mine = pl.BlockSpec((None, tr, cw), lambda l, hf, i, core_ref: (l, jnp.where(hf == core_ref[0], i, 0), 0))
    other = pl.BlockSpec((None, tr, cw), lambda l, hf, i, core_ref: (l, jnp.where(hf == core_ref[0], 0, i), 0))
    shp = jax.ShapeDtypeStruct((L, 2, rh, cw), F32)
    view = lambda a: a.reshape(L, 2, rh, cw)
    outs = pl.pallas_call(
        body, name=name, out_shape=(shp, shp, shp, shp),
        grid_spec=pltpu.PrefetchScalarGridSpec(
            num_scalar_prefetch=1, grid=(L, 2, rh // tr), in_specs=[full, mine, other, full, full],
            out_specs=(full, full, full, full)),
        compiler_params=_params(("arbitrary", "arbitrary", "arbitrary")),
    )(core, view(w), own, recv, view(m), view(v))
    return tuple(o.reshape(L, 2 * rh, cw) for o in outs)


ANY = pl.BlockSpec(memory_space=pl.ANY)


def _position():
    x, y, c = lax.axis_index("x"), lax.axis_index("y"), lax.axis_index("c")
    return x, y, c


def _allgather(ins, out_shapes, src_fns, dst_fns, name, in_vmem):
    n = len(ins)

    def body(*refs):
        in_refs, out_refs = refs[:n], refs[n:2 * n]
        send_sems, recv_sems, local_sems = refs[2 * n:]
        x, y, c = _position()
        me, sibling = (x, y, c), (x, y, 1 - c)
        chips = [(1 - x, y), (x, 1 - y), (1 - x, 1 - y)]

        def copy(k, j, block, to, own=False):
            dst = dst_fns[k](out_refs[k], *block)
            return pltpu.make_async_remote_copy(
                src_ref=src_fns[k](in_refs[k], c) if own else dst, dst_ref=dst,
                send_sem=send_sems.at[k, j], recv_sem=recv_sems.at[k, j], device_id=to, device_id_type=MESH)

        mine = [pltpu.make_async_copy(src_fns[k](in_refs[k], c), dst_fns[k](out_refs[k], *me), local_sems.at[k])
                for k in range(n)]
        for cp in mine:
            cp.start()
        first = []
        for k in range(n):
            first.append(copy(k, 0, me, sibling, own=True))
            first += [copy(k, 1 + j, me, (*chip, c), own=True) for j, chip in enumerate(chips)]
        for cp in first:
            cp.start()
        passed = []
        for j, chip in enumerate(chips):
            for k in range(n):
                copy(k, 1 + j, (*chip, c), me).wait_recv()
                fwd = copy(k, 4 + j, (*chip, c), sibling)
                fwd.start()
                passed.append(fwd)
        for k in range(n):
            copy(k, 0, sibling, me).wait_recv()
        for j, chip in enumerate(chips):
            for k in range(n):
                copy(k, 4 + j, (*chip, 1 - c), me).wait_recv()
        for cp in first + passed:
            cp.wait_send()
        for cp in mine:
            cp.wait()

    spec = pl.BlockSpec(memory_space=pltpu.VMEM) if in_vmem else ANY
    return pl.pallas_call(
        body, name=name, out_shape=tuple(out_shapes), in_specs=[spec] * n, out_specs=tuple([spec] * n),
        scratch_shapes=[pltpu.SemaphoreType.DMA((n, 7)), pltpu.SemaphoreType.DMA((n, 7)),
                        pltpu.SemaphoreType.DMA((n,))],
        compiler_params=pltpu.CompilerParams(vmem_limit_bytes=VMEM_LIMIT_BYTES),
    )(*ins)


def _allgather_small(payload, name):
    R = payload.shape[0]
    (out,) = _allgather(
        [payload], [jax.ShapeDtypeStruct((N_DEV, R, LANES), F32)],
        [lambda ref, c: ref], [lambda ref, px, py, pc: ref.at[4 * px + 2 * py + pc]], name, in_vmem=True)
    return out


def _gather_weights(shards, col_sharded, name):
    src_fns, dst_fns, out_shapes = [], [], []
    for sh, col in zip(shards, col_sharded):
        L, r, cw = sh.shape
        rh = r // 2
        src_fns.append(lambda ref, c, rh=rh: ref.at[:, pl.ds(c * rh, rh), :])
        if col:
            out_shapes.append(jax.ShapeDtypeStruct((L, r, N_CHIPS * cw), sh.dtype))
            dst_fns.append(lambda ref, px, py, pc, rh=rh, cw=cw:
                           ref.at[:, pl.ds(pc * rh, rh), pl.ds((2 * px + py) * cw, cw)])
        else:
            out_shapes.append(jax.ShapeDtypeStruct((L, N_CHIPS * r, cw), sh.dtype))
            dst_fns.append(lambda ref, px, py, pc, rh=rh, r=r:
                           ref.at[:, pl.ds((2 * px + py) * r + pc * rh, rh), :])
    return _allgather(list(shards), out_shapes, src_fns, dst_fns, name, in_vmem=False)


def _swap_sibling(ins, name):
    n = len(ins)

    def body(*refs):
        in_refs, out_refs = refs[:n], refs[n:2 * n]
        send_sems, recv_sems = refs[2 * n:]
        x, y, c = _position()
        copies = [pltpu.make_async_remote_copy(
            src_ref=in_refs[k], dst_ref=out_refs[k], send_sem=send_sems.at[k], recv_sem=recv_sems.at[k],
            device_id=(x, y, 1 - c), device_id_type=MESH) for k in range(n)]
        for cp in copies:
            cp.start()
        for cp in copies:
            cp.wait_recv()
        for cp in copies:
            cp.wait_send()

    return pl.pallas_call(
        body, name=name, out_shape=tuple(jax.ShapeDtypeStruct(a.shape, a.dtype) for a in ins),
        in_specs=[ANY] * n, out_specs=tuple([ANY] * n),
        scratch_shapes=[pltpu.SemaphoreType.DMA((n,)), pltpu.SemaphoreType.DMA((n,))],
        compiler_params=pltpu.CompilerParams(vmem_limit_bytes=VMEM_LIMIT_BYTES),
    )(*ins)


def _exchange_chips(ins, name):
    n = len(ins)

    def body(*refs):
        in_refs, out_refs = refs[:n], refs[n:2 * n]
        send_sems, recv_sems, local_sems = refs[2 * n:]
        x, y, c = _position()
        my_chip = 2 * x + y
        chips = [(1 - x, y), (x, 1 - y), (1 - x, 1 - y)]
        local = [pltpu.make_async_copy(in_refs[k].at[my_chip], out_refs[k].at[my_chip], local_sems.at[k])
                 for k in range(n)]
        for cp in local:
            cp.start()
        sends = []
        for j, (px, py) in enumerate(chips):
            for k in range(n):
                sends.append(pltpu.make_async_remote_copy(
                    src_ref=in_refs[k].at[2 * px + py], dst_ref=out_refs[k].at[my_chip],
                    send_sem=send_sems.at[k, j], recv_sem=recv_sems.at[k, j],
                    device_id=(px, py, c), device_id_type=MESH))
        for cp in sends:
            cp.start()
        for j, (px, py) in enumerate(chips):
            for k in range(n):
                pltpu.make_async_remote_copy(
                    src_ref=in_refs[k].at[2 * px + py], dst_ref=out_refs[k].at[2 * px + py],
                    send_sem=send_sems.at[k, j], recv_sem=recv_sems.at[k, j],
                    device_id=(px, py, c), device_id_type=MESH).wait_recv()
        for cp in sends:
            cp.wait_send()
        for cp in local:
            cp.wait()

    return pl.pallas_call(
        body, name=name, out_shape=tuple(jax.ShapeDtypeStruct(a.shape, a.dtype) for a in ins),
        in_specs=[ANY] * n, out_specs=tuple([ANY] * n),
        scratch_shapes=[pltpu.SemaphoreType.DMA((n, 3)), pltpu.SemaphoreType.DMA((n, 3)),
                        pltpu.SemaphoreType.DMA((n,))],
        compiler_params=pltpu.CompilerParams(vmem_limit_bytes=VMEM_LIMIT_BYTES),
    )(*ins)


HBM_SPEC = pl.BlockSpec(memory_space=pltpu.HBM)
SEM_SPEC = pl.BlockSpec(memory_space=pltpu.SEMAPHORE)
SPLIT_PARAMS = pltpu.CompilerParams(has_side_effects=pltpu.SideEffectType.DATAFLOW_SIDE_EFFECTING)
TOKEN = jax.ShapeDtypeStruct((8, LANES), F32)


def _hbm(a):
    return pltpu.with_memory_space_constraint(a, pltpu.HBM)


def _weight_window(ref, col, r, cw, px, py, pc):
    rh = r // 2
    if col:
        return ref.at[pl.ds(pc * rh, rh), pl.ds((2 * px + py) * cw, cw)]
    return ref.at[pl.ds((2 * px + py) * r + pc * rh, rh), :]


def _peers(x, y, c):
    return [(x, y, 1 - c), (1 - x, y, c), (x, 1 - y, c), (1 - x, 1 - y, c)]


def _place_own(shard, col, pos, name):
    r, cw = shard.shape
    rh = r // 2
    tr = _pick(rh, (256, 128, 176, 64))
    nb = rh // tr
    shape = (r, N_CHIPS * cw) if col else (N_CHIPS * r, cw)

    def body(pos_ref, x_ref, o_ref):
        o_ref[...] = x_ref[...]

    if col:
        out_idx = lambda i, pos_ref: (pos_ref[1] * nb + i, pos_ref[0])
    else:
        out_idx = lambda i, pos_ref: (pos_ref[0] * (2 * nb) + pos_ref[1] * nb + i, 0)
    return pl.pallas_call(
        body, name=name, out_shape=jax.ShapeDtypeStruct(shape, shard.dtype),
        grid_spec=pltpu.PrefetchScalarGridSpec(
            num_scalar_prefetch=1, grid=(nb,),
            in_specs=[pl.BlockSpec((tr, cw), lambda i, pos_ref: (pos_ref[1] * nb + i, 0))],
            out_specs=pl.BlockSpec((tr, cw), out_idx)),
        compiler_params=_params(("arbitrary",)),
    )(pos, shard)


def _gather_start(shards, lands, cols, per_layer, name):
    n = len(shards)
    nl = n // per_layer

    def body(*refs):
        sh, ld = refs[:n], refs[n:2 * n]
        sems, token = refs[2 * n:2 * n + 2 * nl], refs[-1]
        x, y, c = _position()
        for k in range(n):
            l, a = divmod(k, per_layer)
            r, cw = shards[k].shape
            src = sh[k].at[pl.ds(c * (r // 2), r // 2), :]
            dst = _weight_window(ld[k], cols[k], r, cw, x, y, c)
            for j, peer in enumerate(_peers(x, y, c)):
                pltpu.make_async_remote_copy(src_ref=src, dst_ref=dst, send_sem=sems[2 * l].at[4 * a + j],
                                             recv_sem=sems[2 * l + 1].at[4 * a + j], device_id=peer,
                                             device_id_type=MESH).start()
        token[...] = jnp.zeros_like(token)

    arrs = list(shards) + list(lands)
    out = pl.pallas_call(
        body, name=name,
        out_shape=tuple(pltpu.SemaphoreType.DMA((per_layer * 4,)) for _ in range(2 * nl))
        + tuple(pltpu.HBM(a.shape, a.dtype) for a in arrs) + (TOKEN,),
        in_specs=[HBM_SPEC] * (2 * n),
        out_specs=(SEM_SPEC,) * (2 * nl) + (HBM_SPEC,) * (2 * n) + (pl.BlockSpec(memory_space=pltpu.VMEM),),
        input_output_aliases={i: 2 * nl + i for i in range(2 * n)}, compiler_params=SPLIT_PARAMS,
    )(*[_hbm(a) for a in arrs])
    return out[:2 * nl], out[2 * nl:2 * nl + n], out[2 * nl + n:2 * nl + 2 * n], out[-1]


def _gather_wait(shards, lands, send, recv, after, cols, name):
    m = len(shards)

    def body(*refs):
        sh, ld = refs[:m], refs[m:2 * m]
        send_ref, recv_ref = refs[2 * m], refs[2 * m + 1]
        x, y, c = _position()
        for a in range(m):
            r, cw = shards[a].shape
            src = sh[a].at[pl.ds(c * (r // 2), r // 2), :]
            for j, (px, py, pc) in enumerate(_peers(x, y, c)):
                cp = pltpu.make_async_remote_copy(
                    src_ref=src, dst_ref=_weight_window(ld[a], cols[a], r, cw, px, py, pc),
                    send_sem=send_ref.at[4 * a + j], recv_sem=recv_ref.at[4 * a + j], device_id=(px, py, pc),
                    device_id_type=MESH)
                cp.wait_send()
                cp.wait_recv()

    arrs = list(shards) + list(lands)
    out = pl.pallas_call(
        body, name=name, out_shape=tuple(pltpu.HBM(a.shape, a.dtype) for a in arrs),
        in_specs=[HBM_SPEC] * (2 * m) + [SEM_SPEC, SEM_SPEC, ANY], out_specs=(HBM_SPEC,) * (2 * m),
        input_output_aliases={i: i for i in range(2 * m)}, compiler_params=SPLIT_PARAMS,
    )(*arrs, send, recv, after)
    return out[m:]


def _forward_sibling(lands, cols, shard_shapes, name):
    m = len(lands)

    def body(*refs):
        ins, outs = refs[:m], refs[m:2 * m]
        send_sems, recv_sems = refs[2 * m:]
        x, y, c = _position()
        chips = [(1 - x, y), (x, 1 - y), (1 - x, 1 - y)]
        sends = []
        for a in range(m):
            r, cw = shard_shapes[a]
            for j, (px, py) in enumerate(chips):
                cp = pltpu.make_async_remote_copy(
                    src_ref=_weight_window(ins[a], cols[a], r, cw, px, py, c),
                    dst_ref=_weight_window(outs[a], cols[a], r, cw, px, py, c),
                    send_sem=send_sems.at[a, j], recv_sem=recv_sems.at[a, j], device_id=(x, y, 1 - c),
                    device_id_type=MESH)
                cp.start()
                sends.append(cp)
        for a in range(m):
            r, cw = shard_shapes[a]
            for j, (px, py) in enumerate(chips):
                pltpu.make_async_remote_copy(
                    src_ref=_weight_window(ins[a], cols[a], r, cw, px, py, c),
                    dst_ref=_weight_window(outs[a], cols[a], r, cw, px, py, 1 - c),
                    send_sem=send_sems.at[a, j], recv_sem=recv_sems.at[a, j], device_id=(x, y, 1 - c),
                    device_id_type=MESH).wait_recv()
        for cp in sends:
            cp.wait_send()

    return pl.pallas_call(
        body, name=name, out_shape=tuple(jax.ShapeDtypeStruct(a.shape, a.dtype) for a in lands),
        in_specs=[ANY] * m, out_specs=tuple([ANY] * m), input_output_aliases={i: i for i in range(m)},
        scratch_shapes=[pltpu.SemaphoreType.DMA((m, 3)), pltpu.SemaphoreType.DMA((m, 3))],
        compiler_params=pltpu.CompilerParams(vmem_limit_bytes=VMEM_LIMIT_BYTES),
    )(*lands)


def _exchange_start(sums, name):
    m = len(sums)
    lands = [lax.empty((3,) + s.shape[1:], s.dtype) for s in sums]

    def body(*refs):
        cs, ld = refs[:m], refs[m:2 * m]
        send_ref, recv_ref, token = refs[2 * m], refs[2 * m + 1], refs[-1]
        x, y, c = _position()
        for a in range(m):
            for j, (px, py) in enumerate([(1 - x, y), (x, 1 - y), (1 - x, 1 - y)]):
                pltpu.make_async_remote_copy(
                    src_ref=cs[a].at[2 * px + py], dst_ref=ld[a].at[j], send_sem=send_ref.at[3 * a + j],
                    recv_sem=recv_ref.at[3 * a + j], device_id=(px, py, c), device_id_type=MESH).start()
        token[...] = jnp.zeros_like(token)

    arrs = list(sums) + lands
    out = pl.pallas_call(
        body, name=name,
        out_shape=(pltpu.SemaphoreType.DMA((m * 3,)), pltpu.SemaphoreType.DMA((m * 3,)))
        + tuple(pltpu.HBM(a.shape, a.dtype) for a in arrs) + (TOKEN,),
        in_specs=[HBM_SPEC] * (2 * m),
        out_specs=(SEM_SPEC, SEM_SPEC) + (HBM_SPEC,) * (2 * m) + (pl.BlockSpec(memory_space=pltpu.VMEM),),
        input_output_aliases={i: 2 + i for i in range(2 * m)}, compiler_params=SPLIT_PARAMS,
    )(*[_hbm(a) for a in arrs])
    return out[0], out[1], out[2:2 + m], out[2 + m:2 + 2 * m], out[-1]


def _exchange_wait(sums, lands, send, recv, after, name):
    m = len(sums)

    def body(*refs):
        cs, ld = refs[:m], refs[m:2 * m]
        send_ref, recv_ref = refs[2 * m], refs[2 * m + 1]
        x, y, c = _position()
        for a in range(m):
            for j, (px, py) in enumerate([(1 - x, y), (x, 1 - y), (1 - x, 1 - y)]):
                cp = pltpu.make_async_remote_copy(
                    src_ref=cs[a].at[2 * px + py], dst_ref=ld[a].at[j], send_sem=send_ref.at[3 * a + j],
                    recv_sem=recv_ref.at[3 * a + j], device_id=(px, py, c), device_id_type=MESH)
                cp.wait_send()
                cp.wait_recv()

    arrs = list(sums) + list(lands)
    out = pl.pallas_call(
        body, name=name, out_shape=tuple(pltpu.HBM(a.shape, a.dtype) for a in arrs),
        in_specs=[HBM_SPEC] * (2 * m) + [SEM_SPEC, SEM_SPEC, ANY], out_specs=(HBM_SPEC,) * (2 * m),
        input_output_aliases={i: i for i in range(2 * m)}, compiler_params=SPLIT_PARAMS,
    )(*arrs, send, recv, after)
    return out[:m], out[m:]


def _vec(a):
    return a.reshape(1, -1)


def _local_step(x, tgt, mod, W, P, get_w=None, on_grads=None):
    Dm = D_MODEL
    G = {k: [] for k in ("gm_w_in", "gm_w_out", "hg_w_in", "hg_w_out", "ffn_w_up", "ffn_w_down")}
    lb_all = _lb_fwd(P["hg_lb"], "lb_fwd")
    saved = []
    xs = x
    y_prev = gate_prev = None
    layer_w = [None] * DEPTH

    def wmm(xa, kind, i, mode, name):
        if layer_w[i] is not None:
            return _mm(xa, layer_w[i][kind], mode, name)
        return _mm(xa, W[kind], mode, name, b_layer=i if kind.startswith("ffn") else i // 2)

    for i in range(DEPTH):
        m = [_vec(mod[i, j * Dm:(j + 1) * Dm]) for j in range(6)]
        sh1, sc1, g1, sh2, sc2, g2 = m
        j = i // 2
        if get_w is not None:
            layer_w[i] = get_w(i, xs if y_prev is None else y_prev)
        xs, h = _norm_fwd(xs, y_prev, gate_prev, _vec(P["norm_g"][i, 0]), sc1, sh1, f"norm_fwd_a{i}")
        rec = dict(x1=xs, h1=h)
        if i % 2 == 0:
            a = wmm(h, "gm_w_in", i, "nn", f"gm_in{i}")
            gated = _gm_mid_fwd(a, _vec(P["gm_ln_g"][j]), _vec(P["gm_ln_b"][j]), P["gm_w_s"][j],
                                P["gm_b_s"][j].reshape(GM_HEADS, GM_BLOCK, 1), f"gm_mid_fwd{i}")
            y1 = wmm(gated, "gm_w_out", i, "nn", f"gm_out{i}")
            rec.update(a=a, act=gated)
        else:
            p = wmm(h, "hg_w_in", i, "nn", f"hg_in{i}")
            o, og, states = _hg_scan_fwd(p, _vec(lb_all[j]), _vec(P["hg_gn_g"][j]), f"hg_scan_fwd{i}")
            y1 = wmm(og, "hg_w_out", i, "nn", f"hg_out{i}")
            rec.update(a=p, act=og, o=o, states=states)
        rec["y1"] = y1
        xs, h2 = _norm_fwd(xs, y1, g1, _vec(P["norm_g"][i, 1]), sc2, sh2, f"norm_fwd_b{i}")
        a2 = wmm(h2, "ffn_w_up", i, "nn", f"ffn_up{i}")
        mm_ = _conv_fwd(a2, P["ffn_conv_w"][i], _vec(P["ffn_conv_b"][i]), f"conv_fwd{i}")
        y2 = wmm(mm_, "ffn_w_down", i, "nn", f"ffn_down{i}")
        rec.update(x2=xs, h2=h2, a2=a2, m=mm_, y2=y2, mods=m)
        saved.append(rec)
        y_prev, gate_prev = y2, g2
    dx, dy, loss, s_fg, s_gate = _loss_head(xs, y_prev, gate_prev, _vec(P["final_g"]), tgt, "loss_head")
    small = dict(final_g=s_fg, norm_g=[None] * DEPTH, dmod=[None] * DEPTH, ffn_conv_w=[None] * DEPTH,
                 ffn_conv_b=[None] * DEPTH, gm_ln_g=[None] * 2, gm_ln_b=[None] * 2, gm_w_s=[None] * 2,
                 gm_b_s=[None] * 2, hg_gn_g=[None] * 2, dlb=[None] * 2)
    for i in reversed(range(DEPTH)):
        rec = saved[i]
        sh1, sc1, g1, sh2, sc2, g2 = rec["mods"]
        j = i // 2
        d_g2 = s_gate
        dm = wmm(dy, "ffn_w_down", i, "nt", f"ffn_down_dx{i}")
        G["ffn_w_down"].append(_mm(rec["m"], dy, "tn", f"ffn_down_dw{i}"))
        da2, dcw, dcb = _conv_bwd(rec["a2"], dm, P["ffn_conv_w"][i], _vec(P["ffn_conv_b"][i]), f"conv_bwd{i}")
        small["ffn_conv_w"][i], small["ffn_conv_b"][i] = dcw, dcb
        dh2 = wmm(da2, "ffn_w_up", i, "nt", f"ffn_up_dx{i}")
        G["ffn_w_up"].append(_mm(rec["h2"], da2, "tn", f"ffn_up_dw{i}", exchange_out=True))
        ng2 = _vec(P["norm_g"][i, 1])
        dx, dy, s_sh2, s_x2, d_g1 = _norm_bwd(rec["x2"], dh2, dx, ng2, sc2, rec["y1"], g1, f"norm_bwd_b{i}")
        d_sc2, d_ng2 = s_x2 * ng2, s_x2 * (1.0 + sc2)
        if i % 2 == 0:
            dgated = wmm(dy, "gm_w_out", i, "nt", f"gm_out_dx{i}")
            G["gm_w_out"].append(_mm(rec["act"], dy, "tn", f"gm_out_dw{i}"))
            da, dws, dbs, dlg, dlbeta = _gm_mid_bwd(
                rec["a"], dgated, _vec(P["gm_ln_g"][j]), _vec(P["gm_ln_b"][j]), P["gm_w_s"][j],
                P["gm_b_s"][j].reshape(GM_HEADS, GM_BLOCK, 1), f"gm_mid_bwd{i}")
            small["gm_w_s"][j], small["gm_b_s"][j] = dws, dbs[:, :GM_HEADS].T
            small["gm_ln_g"][j], small["gm_ln_b"][j] = dlg, dlbeta
            dh1 = wmm(da, "gm_w_in", i, "nt", f"gm_in_dx{i}")
            G["gm_w_in"].append(_mm(rec["h1"], da, "tn", f"gm_in_dw{i}", exchange_out=True))
        else:
            dog = wmm(dy, "hg_w_out", i, "nt", f"hg_out_dx{i}")
            G["hg_w_out"].append(_mm(rec["act"], dy, "tn", f"hg_out_dw{i}"))
            dp, dlb, dgn = _hg_scan_bwd(rec["a"], _vec(lb_all[j]), _vec(P["hg_gn_g"][j]), rec["o"], dog,
                                        rec["states"], f"hg_scan_bwd{i}")
            small["dlb"][j], small["hg_gn_g"][j] = dlb, dgn
            dh1 = wmm(dp, "hg_w_in", i, "nt", f"hg_in_dx{i}")
            G["hg_w_in"].append(_mm(rec["h1"], dp, "tn", f"hg_in_dw{i}", exchange_out=True))
        ng1 = _vec(P["norm_g"][i, 0])
        if on_grads is not None:
            mixer = ("gm_w_in", "gm_w_out") if i % 2 == 0 else ("hg_w_in", "hg_w_out")
            ng1 = ng1 + on_grads(i, {k: G[k][-1] for k in mixer + ("ffn_w_up", "ffn_w_down")})
        if i > 0:
            prev = saved[i - 1]
            dx, dy, s_sh1, s_x1, s_gate = _norm_bwd(rec["x1"], dh1, dx, ng1, sc1, prev["y2"], prev["mods"][5],
                                                    f"norm_bwd_a{i}")
        else:
            dx, s_sh1, s_x1 = _norm_bwd(rec["x1"], dh1, dx, ng1, sc1, None, None, f"norm_bwd_a{i}")
        d_sc1, d_ng1 = s_x1 * ng1, s_x1 * (1.0 + sc1)
        small["norm_g"][i] = jnp.concatenate([d_ng1, d_ng2], axis=0)
        small["dmod"][i] = jnp.concatenate([s_sh1, d_sc1, d_g1, s_sh2, d_sc2, d_g2], axis=1)
    for k in G:
        G[k] = G[k][::-1]
    dlb_all = jnp.concatenate(small.pop("dlb"), axis=0)
    small["hg_lb"] = _lb_bwd(P["hg_lb"], dlb_all, "lb_bwd")
    return loss, dx, G, small


BIG = ("gm_w_in", "gm_w_out", "hg_w_in", "hg_w_out", "ffn_w_up", "ffn_w_down")
COL_SHARDED = dict(gm_w_in=True, gm_w_out=False, hg_w_in=True, hg_w_out=False, ffn_w_up=True, ffn_w_down=False)
LAYER_WEIGHTS = 4


def _layer_kinds(i):
    return (("gm_w_in", "gm_w_out") if i % 2 == 0 else ("hg_w_in", "hg_w_out")) + ("ffn_w_up", "ffn_w_down")


def _pack(pieces):
    flat = [p.reshape(-1).astype(F32) for p in pieces]
    offs, tot = [], 0
    for f in flat:
        offs.append((tot, f.shape[0]))
        tot += f.shape[0]
    padded = -(-tot // (8 * LANES)) * (8 * LANES)
    if padded > tot:
        flat.append(jnp.zeros((padded - tot,), F32))
    return jnp.concatenate(flat).reshape(-1, LANES), offs


def _unpack(rows, offs, shapes):
    lead = rows.shape[:-2]
    flat = rows.reshape(lead + (-1,))
    return [flat[..., o:o + n].reshape(lead + tuple(s)) for (o, n), s in zip(offs, shapes)]


def _from_chips(per_dev, axis):
    per_chip = per_dev[0::2]
    return jnp.concatenate([per_chip[s] for s in range(N_CHIPS)], axis=axis)


def kernel(x, c, gm_w_in, gm_ln_g, gm_ln_b, gm_w_s, gm_b_s, gm_w_out, hg_w_in, hg_lb, hg_gn_g, hg_w_out, ffn_w_up, ffn_conv_w, ffn_conv_b, ffn_w_down, norm_g, ada_w, ada_b, final_g, loss_target, m_gm_w_in, m_gm_ln_g, m_gm_ln_b, m_gm_w_s, m_gm_b_s, m_gm_w_out, m_hg_w_in, m_hg_lb, m_hg_gn_g, m_hg_w_out, m_ffn_w_up, m_ffn_conv_w, m_ffn_conv_b, m_ffn_w_down, m_norm_g, m_ada_w, m_ada_b, m_final_g, v_gm_w_in, v_gm_ln_g, v_gm_ln_b, v_gm_w_s, v_gm_b_s, v_gm_w_out, v_hg_w_in, v_hg_lb, v_hg_gn_g, v_hg_w_out, v_ffn_w_up, v_ffn_conv_w, v_ffn_conv_b, v_ffn_w_down, v_norm_g, v_ada_w, v_ada_b, v_final_g):
    Dm = D_MODEL
    xi, yi, ci = _position()
    chip = 2 * xi + yi
    dev = 4 * xi + 2 * yi + ci
    weights = dict(gm_w_in=gm_w_in, gm_ln_g=gm_ln_g, gm_ln_b=gm_ln_b, gm_w_s=gm_w_s, gm_b_s=gm_b_s,
                   gm_w_out=gm_w_out, hg_w_in=hg_w_in, hg_lb=hg_lb, hg_gn_g=hg_gn_g, hg_w_out=hg_w_out,
                   ffn_w_up=ffn_w_up, ffn_conv_w=ffn_conv_w, ffn_conv_b=ffn_conv_b, ffn_w_down=ffn_w_down,
                   norm_g=norm_g, ada_w=ada_w, ada_b=ada_b, final_g=final_g)
    mom_m = dict(gm_w_in=m_gm_w_in, gm_ln_g=m_gm_ln_g, gm_ln_b=m_gm_ln_b, gm_w_s=m_gm_w_s, gm_b_s=m_gm_b_s,
                 gm_w_out=m_gm_w_out, hg_w_in=m_hg_w_in, hg_lb=m_hg_lb, hg_gn_g=m_hg_gn_g, hg_w_out=m_hg_w_out,
                 ffn_w_up=m_ffn_w_up, ffn_conv_w=m_ffn_conv_w, ffn_conv_b=m_ffn_conv_b, ffn_w_down=m_ffn_w_down,
                 norm_g=m_norm_g, ada_w=m_ada_w, ada_b=m_ada_b, final_g=m_final_g)
    mom_v = dict(gm_w_in=v_gm_w_in, gm_ln_g=v_gm_ln_g, gm_ln_b=v_gm_ln_b, gm_w_s=v_gm_w_s, gm_b_s=v_gm_b_s,
                 gm_w_out=v_gm_w_out, hg_w_in=v_hg_w_in, hg_lb=v_hg_lb, hg_gn_g=v_hg_gn_g, hg_w_out=v_hg_w_out,
                 ffn_w_up=v_ffn_w_up, ffn_conv_w=v_ffn_conv_w, ffn_conv_b=v_ffn_conv_b, ffn_w_down=v_ffn_w_down,
                 norm_g=v_norm_g, ada_w=v_ada_w, ada_b=v_ada_b, final_g=v_final_g)
    order = list(weights)

    pos = jnp.stack([chip, ci]).astype(jnp.int32)
    shards, by_col = [], []
    for i in range(DEPTH):
        for k in _layer_kinds(i):
            shards.append(weights[k][i if k.startswith("ffn") else i // 2].astype(BF16))
            by_col.append(COL_SHARDED[k])
    placed = [_place_own(sh, col, pos, f"place_own{n}") for n, (sh, col) in enumerate(zip(shards, by_col))]
    gsems, sh_thru, ld_thru, _ = _gather_start(shards, placed, by_col, LAYER_WEIGHTS, "gather_start")

    pieces = [c, hg_lb, hg_gn_g, norm_g, ffn_conv_w]
    payload, offs = _pack(pieces)
    got = _allgather_small(payload, "gather_small")
    c_g, lb_g, gn_g, ng_g, cw_g = _unpack(got, offs, [p.shape for p in pieces])
    c_all = c_g.reshape(N_DEV, Dm)
    P = dict(hg_lb=_from_chips(lb_g, 1), hg_gn_g=_from_chips(gn_g, 1), norm_g=_from_chips(ng_g, 2),
             ffn_conv_w=_from_chips(cw_g, 2), gm_ln_g=gm_ln_g, gm_ln_b=gm_ln_b, gm_w_s=gm_w_s, gm_b_s=gm_b_s,
             ffn_conv_b=ffn_conv_b, final_g=final_g)

    cols = ada_w.shape[2]
    ada_b_sh = lax.dynamic_slice_in_dim(ada_b, chip * cols, cols, axis=1)
    mod_sh = _ada_fwd(c_all, ada_w, ada_b_sh, "ada_fwd")
    mod_g = _allgather_small(mod_sh.reshape(-1, LANES), "gather_mod").reshape(N_DEV, DEPTH, N_DEV, cols)
    mod_mine = lax.dynamic_index_in_dim(mod_g[0::2], dev, axis=2, keepdims=False)
    mod = jnp.transpose(mod_mine, (1, 0, 2)).reshape(DEPTH, N_CHIPS * cols)

    core = jnp.reshape(ci, (1,)).astype(jnp.int32)
    chip_arr = jnp.reshape(chip, (1,)).astype(jnp.int32)
    pending = [None] * DEPTH

    def get_w(i, after):
        s = slice(LAYER_WEIGHTS * i, LAYER_WEIGHTS * (i + 1))
        landed = _gather_wait(sh_thru[s], ld_thru[s], gsems[2 * i], gsems[2 * i + 1], after, by_col[s],
                              f"gather_wait{i}")
        full = _forward_sibling(landed, by_col[s], [a.shape for a in shards[s]], f"gather_forward{i}")
        return dict(zip(_layer_kinds(i), full))

    def on_grads(i, gdict):
        kinds = _layer_kinds(i)
        g4 = []
        for k in kinds:
            g = gdict[k]
            if not COL_SHARDED[k]:
                R, C = g.shape
                g = g.reshape(N_CHIPS, 2, R // (2 * N_CHIPS), C)
            g4.append(g)
        to_sib = [lax.dynamic_index_in_dim(g, 1 - ci, axis=1, keepdims=False).astype(BF16) for g in g4]
        from_sib = _swap_sibling(to_sib, f"reduce_swap{i}")
        sums = [_add_own_half(g, r, core, f"chip_sum_{k}{i}") for g, r, k in zip(g4, from_sib, kinds)]
        send, recv, sums_thru, lands, token = _exchange_start(sums, f"reduce_start{i}")
        pending[i] = (send, recv, sums_thru, lands)
        return token[0, 0]

    loss_part, dx, G, small = _local_step(x[0], loss_target[0], mod, None, P, get_w, on_grads)

    by_kind = {k: ([], []) for k in BIG}
    for i in range(DEPTH):
        send, recv, sums_thru, lands = pending[i]
        sums_i, lands_i = _exchange_wait(sums_thru, lands, send, recv, dx, f"reduce_wait{i}")
        for k, s_, l_ in zip(_layer_kinds(i), sums_i, lands_i):
            by_kind[k][0].append(l_)
            by_kind[k][1].append(s_)
    own_halves = [_sum_chips(by_kind[k][0], by_kind[k][1], chip_arr, f"sum_chips_{k}") for k in BIG]
    sib_halves = _swap_sibling(own_halves, "reduce_join")
    grads, deltas, new_m, new_v = {}, {}, {}, {}
    for k, own, recv in zip(BIG, own_halves, sib_halves):
        grads[k], deltas[k], new_m[k], new_v[k] = _adamw_halves(
            weights[k], own, recv, mom_m[k], mom_v[k], core, f"adamw_{k}")

    sum_pieces = [loss_part[:, :1], small["final_g"], jnp.stack(small["gm_ln_g"]), jnp.stack(small["gm_ln_b"]),
                  jnp.stack(small["gm_w_s"]), jnp.stack(small["gm_b_s"]), jnp.stack(small["ffn_conv_b"]),
                  small["hg_lb"], jnp.stack(small["hg_gn_g"]), jnp.stack(small["norm_g"]),
                  jnp.stack(small["ffn_conv_w"])]
    dmod = jnp.concatenate(small["dmod"], axis=0)
    payload2, offs2 = _pack(sum_pieces + [dmod])
    got2 = _allgather_small(payload2, "gather_grads")
    dmod_all = _unpack(got2, offs2[-1:], [dmod.shape])[0]
    summed = _sum_devices(got2, "sum_devices")
    (loss_s, d_final_g, d_ln_g, d_ln_b, d_ws, d_bs, d_cb, d_lb, d_gn, d_ng, d_cw) = _unpack(
        summed, offs2[:-1], [(1,), final_g.shape, gm_ln_g.shape, gm_ln_b.shape, gm_w_s.shape, gm_b_s.shape,
                             ffn_conv_b.shape, (2, Dm), (2, Dm), (DEPTH, 2, Dm), (DEPTH, 3, 2 * FFN_HIDDEN)])
    grads.update(final_g=d_final_g, gm_ln_g=d_ln_g, gm_ln_b=d_ln_b, gm_w_s=d_ws, gm_b_s=d_bs, ffn_conv_b=d_cb)
    grads["hg_lb"] = lax.dynamic_slice_in_dim(d_lb, chip * hg_lb.shape[1], hg_lb.shape[1], axis=1)
    grads["hg_gn_g"] = lax.dynamic_slice_in_dim(d_gn, chip * hg_gn_g.shape[1], hg_gn_g.shape[1], axis=1)
    grads["norm_g"] = lax.dynamic_slice_in_dim(d_ng, chip * norm_g.shape[2], norm_g.shape[2], axis=2)
    grads["ffn_conv_w"] = lax.dynamic_slice_in_dim(d_cw, chip * ffn_conv_w.shape[2], ffn_conv_w.shape[2], axis=2)
    dmod_sh = lax.dynamic_slice_in_dim(dmod_all, chip * cols, cols, axis=2)
    grads["ada_w"] = _ada_bwd(c_all, jnp.transpose(dmod_sh, (1, 0, 2)), "ada_bwd")
    grads["ada_b"] = _sum_devices(dmod_all.reshape(N_DEV, -1, LANES), "sum_ada_b").reshape(ada_b.shape)

    for k in order:
        if k in BIG:
            continue
        w = weights[k]
        shp = w.shape
        view = (-1, shp[-1]) if w.ndim > 1 else (8, -1)
        d, m2, v2 = _adamw(w.reshape(view), grads[k].reshape(view), mom_m[k].reshape(view), mom_v[k].reshape(view),
                           f"adamw_{k}")
        deltas[k], new_m[k], new_v[k] = d.reshape(shp), m2.reshape(shp), v2.reshape(shp)
        grads[k] = grads[k].reshape(shp)

    loss = loss_s.reshape(())
    return (loss, dx[None], *[grads[k] for k in order], *[deltas[k] for k in order],
            *[new_m[k] for k in order], *[new_v[k] for k in order])
```

```python
import functools

import jax
import jax.numpy as jnp
from jax import lax
from jax.experimental import pallas as pl
from jax.experimental.pallas import tpu as pltpu

F32 = jnp.float32
BF16 = jnp.bfloat16
HI = lax.Precision.HIGHEST
X3 = lax.Precision.HIGH
MESH = pl.DeviceIdType.MESH

D_MODEL = 1024
DEPTH = 4
EPS = 1e-6
GM_WIDTH = 2048
GM_HEADS = 8
GM_BLOCK = 128
GM_HEAD_DIM = 256
CHUNK = 64
HG_HEADS = 8
HG_DIM = 128
FFN_HIDDEN = 2816
N_CHIPS = 4
N_DEV = 8

ADAM_LR = 0.001
ADAM_B1 = 0.9
ADAM_B2 = 0.999
ADAM_EPS = 1e-08
ADAM_WD = 0.01
ADAM_STEP = 10

VMEM_LIMIT_BYTES = 56 * 1024 * 1024
ROW_TILE = 256
LANES = 128

_SQRT_HALF = 0.7071067811865476
_INV_SQRT_2PI = 0.3989422804014327


def _pick(dim, prefs):
    for p in prefs:
        if dim % p == 0:
            return p
    return dim


def _params(sem):
    return pltpu.CompilerParams(dimension_semantics=sem, vmem_limit_bytes=VMEM_LIMIT_BYTES)


def _cdf(x):
    return 0.5 * (1.0 + lax.erf(x * _SQRT_HALF))


def _pdf(x):
    return jnp.exp(-0.5 * x * x) * _INV_SQRT_2PI


def _sig(x):
    return jax.nn.sigmoid(x)


def _dot(a, b, dims, prec=None):
    return lax.dot_general(a, b, (dims, ((), ())), precision=prec, preferred_element_type=F32)


NN = ((1,), (0,))
NT = ((1,), (1,))
TN = ((0,), (0,))


MM_VMEM_BUDGET = 40 * 1024 * 1024


def _mm_tiles(mode, M, N, K, a_bytes, b_bytes, exchange_out):
    tn = _pick(N, (1408, 1024, 512, 256, 128))
    tms = [t for t in (1408, 1024, 512, 256, 128) if M % t == 0 and not (exchange_out and (M // 2) % t)] or [M]
    tks = [K] + [t for t in (2816, 2048, 1408, 1024, 512, 256, 128) if t < K and K % t == 0]

    def fits(tm, tk):
        acc = tm * tn * 4 if tk < K else 0
        return 2 * tm * tk * a_bytes + 2 * tk * tn * b_bytes + 2 * tm * tn * 4 + acc <= MM_VMEM_BUDGET

    for min_tm in (min(512, tms[0]), 0):
        for tk in tks:
            for tm in tms:
                if tm >= min_tm and fits(tm, tk):
                    return tm, tn, tk
    return tms[-1], tn, tks[-1]


def _mm(a, b, mode, name, b_layer=None, out_dtype=F32, exchange_out=False):
    b2 = b.shape[-2:]
    if mode == "nn":
        (M, K), (_, N) = a.shape, b2
    elif mode == "nt":
        (M, K), (N, _) = a.shape, b2
    else:
        (K, M), (_, N) = a.shape, b2
    tm, tn, tk = _mm_tiles(mode, M, N, K, a.dtype.itemsize, b.dtype.itemsize, exchange_out)
    nk = K // tk
    dims = {"nn": NN, "nt": NT, "tn": TN}[mode]

    def body(a_ref, b_ref, o_ref, *scratch):
        part = _dot(a_ref[...].astype(BF16), b_ref[...].astype(BF16), dims)
        if nk == 1:
            o_ref[...] = part.astype(o_ref.dtype)
            return
        (acc_ref,) = scratch
        k = pl.program_id(2)

        @pl.when(k == 0)
        def _():
            acc_ref[...] = part

        @pl.when(k > 0)
        def _():
            acc_ref[...] += part

        @pl.when(k == nk - 1)
        def _():
            o_ref[...] = acc_ref[...].astype(o_ref.dtype)

    if mode == "tn":
        a_spec = pl.BlockSpec((tk, tm), lambda i, j, k: (k, i))
    else:
        a_spec = pl.BlockSpec((tm, tk), lambda i, j, k: (i, k))
    bblk = (tk, tn) if mode in ("nn", "tn") else (tn, tk)
    bidx = (lambda i, j, k: (k, j)) if mode in ("nn", "tn") else (lambda i, j, k: (j, k))
    if b_layer is None:
        b_spec = pl.BlockSpec(bblk, bidx)
    else:
        b_spec = pl.BlockSpec((None,) + bblk, lambda i, j, k: (b_layer,) + bidx(i, j, k))
    if exchange_out:
        mh, cw = M // 2, N // N_CHIPS
        assert mh % tm == 0 and cw % tn == 0
        out_shape = jax.ShapeDtypeStruct((N_CHIPS, 2, mh, cw), out_dtype)
        o_spec = pl.BlockSpec(
            (None, None, tm, tn),
            lambda i, j, k: (j // (cw // tn), i // (mh // tm), i % (mh // tm), j % (cw // tn)))
    else:
        out_shape = jax.ShapeDtypeStruct((M, N), out_dtype)
        o_spec = pl.BlockSpec((tm, tn), lambda i, j, k: (i, j))
    return pl.pallas_call(
        body, name=name, out_shape=out_shape, grid=(M // tm, N // tn, nk),
        in_specs=[a_spec, b_spec], out_specs=o_spec,
        scratch_shapes=[] if nk == 1 else [pltpu.VMEM((tm, tn), F32)],
        compiler_params=_params(("parallel", "parallel", "arbitrary")),
    )(a, b)


def _row_spec(tr, width):
    return pl.BlockSpec((tr, width), lambda i: (i, 0))


def _vec_spec(width, rows=1):
    return pl.BlockSpec((rows, width), lambda i: (0, 0))


def _norm_fwd(x, y, gate, g, sc, sh, name):
    T, Dm = x.shape
    tr = _pick(T, (ROW_TILE,))
    has_res = y is not None

    def body(*refs):
        if has_res:
            x_ref, y_ref, gate_ref, g_ref, sc_ref, sh_ref, xo_ref, h_ref = refs
            xv = x_ref[...] + gate_ref[...] * y_ref[...]
            xo_ref[...] = xv
        else:
            x_ref, g_ref, sc_ref, sh_ref, h_ref = refs
            xv = x_ref[...]
        rstd = lax.rsqrt(jnp.mean(xv * xv, axis=-1, keepdims=True) + EPS)
        h_ref[...] = ((xv * rstd * g_ref[...]) * (1.0 + sc_ref[...]) + sh_ref[...]).astype(BF16)

    row, vec = _row_spec(tr, Dm), _vec_spec(Dm)
    if has_res:
        ins, in_specs = (x, y, gate, g, sc, sh), [row, row, vec, vec, vec, vec]
        out_shape = (jax.ShapeDtypeStruct((T, Dm), F32), jax.ShapeDtypeStruct((T, Dm), BF16))
        out_specs = (row, row)
    else:
        ins, in_specs = (x, g, sc, sh), [row, vec, vec, vec]
        out_shape = jax.ShapeDtypeStruct((T, Dm), BF16)
        out_specs = row
    out = pl.pallas_call(body, name=name, out_shape=out_shape, grid=(T // tr,), in_specs=in_specs,
                         out_specs=out_specs, compiler_params=_params(("parallel",)))(*ins)
    return out if has_res else (x, out)


def _norm_bwd(x, dh, dxo, g, sc, y_prev, gate_prev, name):
    T, Dm = x.shape
    tr = _pick(T, (ROW_TILE,))
    has_prev = y_prev is not None

    def body(*refs):
        if has_prev:
            x_ref, dh_ref, dxo_ref, g_ref, sc_ref, yp_ref, gp_ref, dx_ref, dyp_ref, s1_ref, s2_ref, s3_ref = refs
        else:
            x_ref, dh_ref, dxo_ref, g_ref, sc_ref, dx_ref, s1_ref, s2_ref = refs

        @pl.when(pl.program_id(0) == 0)
        def _():
            s1_ref[...] = jnp.zeros_like(s1_ref)
            s2_ref[...] = jnp.zeros_like(s2_ref)
            if has_prev:
                s3_ref[...] = jnp.zeros_like(s3_ref)

        xv = x_ref[...]
        rstd = lax.rsqrt(jnp.mean(xv * xv, axis=-1, keepdims=True) + EPS)
        xhat = xv * rstd
        dh = dh_ref[...]
        dxhat = dh * (g_ref[...] * (1.0 + sc_ref[...]))
        dx = dxo_ref[...] + rstd * (dxhat - xhat * jnp.mean(dxhat * xhat, axis=-1, keepdims=True))
        dx_ref[...] = dx
        s1_ref[...] += jnp.sum(dh, axis=0, keepdims=True)
        s2_ref[...] += jnp.sum(dh * xhat, axis=0, keepdims=True)
        if has_prev:
            dyp_ref[...] = (gp_ref[...] * dx).astype(BF16)
            s3_ref[...] += jnp.sum(dx * yp_ref[...], axis=0, keepdims=True)

    row, vec = _row_spec(tr, Dm), _vec_spec(Dm)
    vshape = jax.ShapeDtypeStruct((1, Dm), F32)
    if has_prev:
        ins, in_specs = (x, dh, dxo, g, sc, y_prev, gate_prev), [row, row, row, vec, vec, row, vec]
        out_shape = (jax.ShapeDtypeStruct((T, Dm), F32), jax.ShapeDtypeStruct((T, Dm), BF16), vshape, vshape, vshape)
        out_specs = (row, row, vec, vec, vec)
    else:
        ins, in_specs = (x, dh, dxo, g, sc), [row, row, row, vec, vec]
        out_shape = (jax.ShapeDtypeStruct((T, Dm), F32), vshape, vshape)
        out_specs = (row, vec, vec)
    return pl.pallas_call(body, name=name, out_shape=out_shape, grid=(T // tr,), in_specs=in_specs,
                          out_specs=out_specs, compiler_params=_params(("arbitrary",)))(*ins)


def _loss_head(x, y, gate, fg, tgt, name):
    T, Dm = x.shape
    tr = _pick(T, (ROW_TILE,))
    nsteps = T // tr

    def body(x_ref, y_ref, gate_ref, fg_ref, t_ref, dx_ref, dy_ref, loss_ref, sfg_ref, sg_ref, acc_ref):
        i = pl.program_id(0)

        @pl.when(i == 0)
        def _():
            acc_ref[...] = jnp.zeros_like(acc_ref)
            sfg_ref[...] = jnp.zeros_like(sfg_ref)
            sg_ref[...] = jnp.zeros_like(sg_ref)

        yv = y_ref[...]
        xv = x_ref[...] + gate_ref[...] * yv
        rstd = lax.rsqrt(jnp.mean(xv * xv, axis=-1, keepdims=True) + EPS)
        xhat = xv * rstd
        err = xhat * fg_ref[...] - t_ref[...]
        acc_ref[...] += jnp.sum(err * err, axis=0, keepdims=True)
        dyn = err * (1.0 / Dm)
        sfg_ref[...] += jnp.sum(dyn * xhat, axis=0, keepdims=True)
        dxhat = dyn * fg_ref[...]
        dx = rstd * (dxhat - xhat * jnp.mean(dxhat * xhat, axis=-1, keepdims=True))
        dx_ref[...] = dx
        dy_ref[...] = (gate_ref[...] * dx).astype(BF16)
        sg_ref[...] += jnp.sum(dx * yv, axis=0, keepdims=True)

        @pl.when(i == nsteps - 1)
        def _():
            total = jnp.sum(acc_ref[...], axis=1, keepdims=True) * (0.5 / Dm)
            loss_ref[...] = jnp.broadcast_to(total, loss_ref.shape)

    row, vec = _row_spec(tr, Dm), _vec_spec(Dm)
    vshape = jax.ShapeDtypeStruct((1, Dm), F32)
    return pl.pallas_call(
        body, name=name, grid=(nsteps,),
        out_shape=(jax.ShapeDtypeStruct((T, Dm), F32), jax.ShapeDtypeStruct((T, Dm), BF16),
                   jax.ShapeDtypeStruct((1, LANES), F32), vshape, vshape),
        in_specs=[row, row, vec, vec, row], out_specs=(row, row, _vec_spec(LANES), vec, vec),
        scratch_shapes=[pltpu.VMEM((1, Dm), F32)], compiler_params=_params(("arbitrary",)),
    )(x, y, gate, fg, tgt)


def _spatial_mask():
    r = lax.broadcasted_iota(jnp.int32, (GM_BLOCK, GM_BLOCK), 0) // CHUNK
    c = lax.broadcasted_iota(jnp.int32, (GM_BLOCK, GM_BLOCK), 1) // CHUNK
    return r >= c


def _gm_specs(tr):
    return [_row_spec(tr, 2 * GM_WIDTH), _vec_spec(GM_WIDTH), _vec_spec(GM_WIDTH),
            pl.BlockSpec((GM_HEADS, GM_BLOCK, GM_BLOCK), lambda i: (0, 0, 0)),
            pl.BlockSpec((GM_HEADS, GM_BLOCK, 1), lambda i: (0, 0, 0))]


def _gm_mid_fwd(a, ln_g, ln_b, ws, bs3, name):
    T = a.shape[0]
    tr = _pick(T, (ROW_TILE,))
    W = GM_WIDTH

    def body(a_ref, lg_ref, lb_ref, ws_ref, bs_ref, o_ref, vn_scr):
        av = a_ref[:, W:]
        v = av * _cdf(av)
        vc = v - jnp.mean(v, axis=-1, keepdims=True)
        rstd = lax.rsqrt(jnp.mean(vc * vc, axis=-1, keepdims=True) + EPS)
        vn_scr[...] = (vc * rstd * lg_ref[...] + lb_ref[...]).astype(BF16)
        mask = _spatial_mask()
        for h in range(GM_HEADS):
            w = jnp.where(mask, ws_ref[h], 0.0).astype(BF16)
            cs = slice(h * GM_HEAD_DIM, (h + 1) * GM_HEAD_DIM)
            for blk in range(tr // GM_BLOCK):
                rs = slice(blk * GM_BLOCK, (blk + 1) * GM_BLOCK)
                s = _dot(w, vn_scr[rs, cs], NN) + bs_ref[h]
                au = a_ref[rs, cs]
                o_ref[rs, cs] = (au * _cdf(au) * s).astype(BF16)

    return pl.pallas_call(
        body, name=name, out_shape=jax.ShapeDtypeStruct((T, W), BF16), grid=(T // tr,),
        in_specs=_gm_specs(tr), out_specs=_row_spec(tr, W),
        scratch_shapes=[pltpu.VMEM((tr, W), BF16)], compiler_params=_params(("parallel",)),
    )(a, ln_g, ln_b, ws, bs3)


def _gm_mid_bwd(a, dgated, ln_g, ln_b, ws, bs3, name):
    T = a.shape[0]
    tr = _pick(T, (ROW_TILE,))
    W = GM_WIDTH
    nsteps = T // tr

    def body(a_ref, dg_ref, lg_ref, lb_ref, ws_ref, bs_ref, da_ref, dws_ref, dbs_ref, dlg_ref, dlb_ref,
             vn_scr, vhat_scr, dvn_scr, dsum_scr):
        i = pl.program_id(0)

        @pl.when(i == 0)
        def _():
            dws_ref[...] = jnp.zeros_like(dws_ref)
            dbs_ref[...] = jnp.zeros_like(dbs_ref)
            dlg_ref[...] = jnp.zeros_like(dlg_ref)
            dlb_ref[...] = jnp.zeros_like(dlb_ref)
            dsum_scr[...] = jnp.zeros_like(dsum_scr)

        av = a_ref[:, W:]
        cdf_v = _cdf(av)
        v = av * cdf_v
        vc = v - jnp.mean(v, axis=-1, keepdims=True)
        rstd = lax.rsqrt(jnp.mean(vc * vc, axis=-1, keepdims=True) + EPS)
        vhat_scr[...] = vc * rstd
        vn_scr[...] = (vhat_scr[...] * lg_ref[...] + lb_ref[...]).astype(BF16)
        mask = _spatial_mask()
        for h in range(GM_HEADS):
            w = jnp.where(mask, ws_ref[h], 0.0).astype(BF16)
            cs = slice(h * GM_HEAD_DIM, (h + 1) * GM_HEAD_DIM)
            for blk in range(tr // GM_BLOCK):
                rs = slice(blk * GM_BLOCK, (blk + 1) * GM_BLOCK)
                vnb = vn_scr[rs, cs]
                s = _dot(w, vnb, NN) + bs_ref[h]
                au = a_ref[rs, cs]
                cdf_u = _cdf(au)
                dg = dg_ref[rs, cs]
                ds = dg * (au * cdf_u)
                da_ref[rs, cs] = (dg * s * (cdf_u + au * _pdf(au))).astype(BF16)
                dsb = ds.astype(BF16)
                dvn_scr[rs, cs] = _dot(w, dsb, TN)
                dws_ref[h] += _dot(dsb, vnb, NT)
                dsum_scr[:, cs] += ds
        dvn = dvn_scr[...]
        vhat = vhat_scr[...]
        dlg_ref[...] += jnp.sum(dvn * vhat, axis=0, keepdims=True)
        dlb_ref[...] += jnp.sum(dvn, axis=0, keepdims=True)
        dvh = dvn * lg_ref[...]
        dv = rstd * (dvh - jnp.mean(dvh, axis=-1, keepdims=True)
                     - vhat * jnp.mean(dvh * vhat, axis=-1, keepdims=True))
        da_ref[:, W:] = (dv * (cdf_v + av * _pdf(av))).astype(BF16)

        @pl.when(i == nsteps - 1)
        def _():
            for h in range(GM_HEADS):
                dws_ref[h] = jnp.where(mask, dws_ref[h], 0.0)
            col_head = lax.broadcasted_iota(jnp.int32, (W, GM_BLOCK), 0) // GM_HEAD_DIM
            sel = (col_head == lax.broadcasted_iota(jnp.int32, (W, GM_BLOCK), 1)).astype(F32)
            dbs_ref[...] = _dot(dsum_scr[...], sel, NN, HI)

    vshape = jax.ShapeDtypeStruct((1, W), F32)
    return pl.pallas_call(
        body, name=name, grid=(nsteps,),
        out_shape=(jax.ShapeDtypeStruct((T, 2 * W), BF16), jax.ShapeDtypeStruct((GM_HEADS, GM_BLOCK, GM_BLOCK), F32),
                   jax.ShapeDtypeStruct((GM_BLOCK, GM_BLOCK), F32), vshape, vshape),
        in_specs=[_gm_specs(tr)[0], _row_spec(tr, W)] + _gm_specs(tr)[1:],
        out_specs=(_row_spec(tr, 2 * W), pl.BlockSpec((GM_HEADS, GM_BLOCK, GM_BLOCK), lambda i: (0, 0, 0)),
                   pl.BlockSpec((GM_BLOCK, GM_BLOCK), lambda i: (0, 0)), _vec_spec(W), _vec_spec(W)),
        scratch_shapes=[pltpu.VMEM((tr, W), BF16), pltpu.VMEM((tr, W), F32), pltpu.VMEM((tr, W), F32),
                        pltpu.VMEM((GM_BLOCK, W), F32)],
        compiler_params=_params(("arbitrary",)),
    )(a, dgated, ln_g, ln_b, ws, bs3)


SUB = 16
EXP_CLAMP = 80.0


def _tri(lower):
    r = lax.broadcasted_iota(jnp.int32, (CHUNK, CHUNK), 0)
    c = lax.broadcasted_iota(jnp.int32, (CHUNK, CHUNK), 1)
    return (r >= c) if lower else (c >= r)


def _score_masks():
    i = lax.broadcasted_iota(jnp.int32, (CHUNK, CHUNK), 0)
    j = lax.broadcasted_iota(jnp.int32, (CHUNK, CHUNK), 1)
    bi, bj = i // SUB, j // SUB
    diag = (bi == bj) & (i >= j)
    pair = (bi % 2 == 1) & (bj == bi - 1)
    half = (i >= CHUNK // 2) & (j < CHUNK // 2)
    return diag, pair, half


def _dot01(m, x):
    x1 = x.astype(BF16)
    rest = x - x1.astype(F32)
    x2 = rest.astype(BF16)
    x3 = (rest - x2.astype(F32)).astype(BF16)
    return _dot(m, x1, NN) + (_dot(m, x2, NN) + _dot(m, x3, NN))


def _block_rows(b, offset):
    parts = []
    for blk in range(0, CHUNK, SUB):
        r = blk + offset
        parts.append(jnp.zeros((SUB, b.shape[1]), F32) if r < 0 else jnp.broadcast_to(b[r:r + 1], (SUB, b.shape[1])))
    return jnp.concatenate(parts, axis=0)


def _hg_gates(p_ref, lb_ref, h, lower):
    Dm = D_MODEL
    c0 = h * HG_DIM
    qr = p_ref[:, c0:c0 + HG_DIM]
    fz = p_ref[:, Dm + c0:Dm + c0 + HG_DIM]
    v = p_ref[:, 2 * Dm + c0:2 * Dm + c0 + HG_DIM]
    gt = p_ref[:, 3 * Dm + c0:3 * Dm + c0 + HG_DIM]
    lbh = lb_ref[:, c0:c0 + HG_DIM]
    sg = _sig(fz)
    f = lbh + (1.0 - lbh) * sg
    gl = jnp.log(f)
    kk = 1.0 - f
    sq = _sig(qr)
    q = qr * sq
    b = _dot01(lower, gl)
    r_mid = _block_rows(b, SUB // 2 - 1)
    r_prev = _block_rows(b, -1)
    r_end = _block_rows(b, SUB - 1)
    r_half = jnp.broadcast_to(b[CHUNK // 2 - 1:CHUNK // 2], b.shape)
    bc = b[CHUNK - 1:CHUNK]
    eqs = (jnp.exp(jnp.clip(b - r_mid, -EXP_CLAMP, EXP_CLAMP)), jnp.exp(jnp.minimum(b - r_prev, 0.0)),
           jnp.exp(jnp.minimum(b - r_half, 0.0)))
    eks = (jnp.exp(jnp.clip(r_mid - b, -EXP_CLAMP, EXP_CLAMP)), jnp.exp(jnp.minimum(r_end - b, 0.0)),
           jnp.exp(jnp.minimum(r_half - b, 0.0)))
    eb = jnp.exp(b)
    ec = jnp.exp(bc - b)
    return dict(qr=qr, v=v, gt=gt, lbh=lbh, sg=sg, f=f, kk=kk, sq=sq, q=q, eqs=eqs, eks=eks, eb=eb, ec=ec,
                e_end=jnp.exp(bc), qs=[q * e for e in eqs], ks=[kk * e for e in eks], qe=q * eb, ke=kk * ec)


def _scores(g, masks):
    a = None
    for qs, ks, m in zip(g["qs"], g["ks"], masks):
        part = jnp.where(m, _dot(qs.astype(BF16), ks.astype(BF16), NT), 0.0)
        a = part if a is None else a + part
    return a


def _hg_scan_fwd(p, lb, gn, name):
    T = p.shape[0]
    nc = T // CHUNK
    Dm = D_MODEL

    def body(p_ref, lb_ref, gn_ref, o_ref, og_ref, so_ref, st_ref):
        @pl.when(pl.program_id(0) == 0)
        def _():
            st_ref[...] = jnp.zeros_like(st_ref)

        masks = _score_masks()
        lower = _tri(True).astype(BF16)
        for h in range(HG_HEADS):
            g = _hg_gates(p_ref, lb_ref, h, lower)
            cs = slice(h * HG_DIM, (h + 1) * HG_DIM)
            a = _scores(g, masks)
            st = st_ref[h]
            so_ref[0, h] = st
            o = _dot(a.astype(BF16), g["v"].astype(BF16), NN) + _dot(g["qe"], st, NT, X3)
            st_ref[h] = st * g["e_end"] + _dot(g["v"], g["ke"], TN, X3)
            o_ref[:, cs] = o
            r = lax.rsqrt(jnp.mean(o * o, axis=-1, keepdims=True) + EPS)
            gt = g["gt"]
            og_ref[:, cs] = (((o * r) * gn_ref[:, cs]).astype(F32) * (gt * _sig(gt))).astype(BF16)

    return pl.pallas_call(
        body, name=name, grid=(nc,),
        out_shape=(jax.ShapeDtypeStruct((T, Dm), F32), jax.ShapeDtypeStruct((T, Dm), BF16),
                   jax.ShapeDtypeStruct((nc, HG_HEADS, HG_DIM, HG_DIM), F32)),
        in_specs=[_row_spec(CHUNK, 4 * Dm), _vec_spec(Dm), _vec_spec(Dm)],
        out_specs=(_row_spec(CHUNK, Dm), _row_spec(CHUNK, Dm),
                   pl.BlockSpec((1, HG_HEADS, HG_DIM, HG_DIM), lambda i: (i, 0, 0, 0))),
        scratch_shapes=[pltpu.VMEM((HG_HEADS, HG_DIM, HG_DIM), F32)],
        compiler_params=_params(("arbitrary",)),
    )(p, lb, gn)


def _hg_scan_bwd(p, lb, gn, o, dog, states, name):
    T = p.shape[0]
    nc = T // CHUNK
    Dm = D_MODEL

    def rev(i):
        return nc - 1 - i

    def body(p_ref, lb_ref, gn_ref, o_ref, dog_ref, st_in_ref, dp_ref, dlb_ref, dgn_ref, dst_ref, carry_ref):
        @pl.when(pl.program_id(0) == 0)
        def _():
            dst_ref[...] = jnp.zeros_like(dst_ref)
            carry_ref[...] = jnp.zeros_like(carry_ref)
            dlb_ref[...] = jnp.zeros_like(dlb_ref)
            dgn_ref[...] = jnp.zeros_like(dgn_ref)

        upper = _tri(False).astype(BF16)
        lower = _tri(True).astype(BF16)
        masks = _score_masks()
        for h in range(HG_HEADS):
            g = _hg_gates(p_ref, lb_ref, h, lower)
            c0 = h * HG_DIM
            cs = slice(c0, c0 + HG_DIM)
            oh = o_ref[:, cs]
            r = lax.rsqrt(jnp.mean(oh * oh, axis=-1, keepdims=True) + EPS)
            on = oh * r
            gt = g["gt"]
            sgt = _sig(gt)
            sil = gt * sgt
            dogh = dog_ref[:, cs]
            gnh = gn_ref[:, cs]
            don = dogh * gnh * sil
            dgn_ref[:, cs] += jnp.sum(dogh * on * sil, axis=0, keepdims=True)
            dgate = dogh * on * gnh * (sgt * (1.0 + gt * (1.0 - sgt)))
            do = r * (don - on * jnp.mean(don * on, axis=-1, keepdims=True))
            a = _scores(g, masks)
            dst = dst_ref[h]
            dob = do.astype(BF16)
            da = _dot(dob, g["v"].astype(BF16), NT)
            dv = _dot(a.astype(BF16), dob, TN) + _dot(g["ke"].astype(BF16), dst.astype(BF16), NT)
            dq = _dot(do, st_in_ref[0, h], NN, X3) * g["eb"]
            dk = _dot(g["v"], dst, NN, X3) * g["ec"]
            for qs, ks, eq, ek, m in zip(g["qs"], g["ks"], g["eqs"], g["eks"], masks):
                dam = jnp.where(m, da, 0.0)
                dq = dq + _dot(dam, ks, NN, X3) * eq
                dk = dk + _dot(dam, qs, TN, X3) * ek
            dst_ref[h] = dst * g["e_end"] + _dot(do, g["qe"], TN, X3)
            dgd = g["q"] * dq - g["kk"] * dk
            dgl = _dot01(upper, dgd) + carry_ref[h]
            carry_ref[h] += jnp.sum(dgd, axis=0, keepdims=True)
            df = dgl / g["f"] - dk
            sg = g["sg"]
            dlb_ref[:, cs] += jnp.sum(df * (1.0 - sg), axis=0, keepdims=True)
            sq, qr = g["sq"], g["qr"]
            dp_ref[:, c0:c0 + HG_DIM] = (dq * (sq * (1.0 + qr * (1.0 - sq)))).astype(BF16)
            dp_ref[:, Dm + c0:Dm + c0 + HG_DIM] = (df * (1.0 - g["lbh"]) * sg * (1.0 - sg)).astype(BF16)
            dp_ref[:, 2 * Dm + c0:2 * Dm + c0 + HG_DIM] = dv.astype(BF16)
            dp_ref[:, 3 * Dm + c0:3 * Dm + c0 + HG_DIM] = dgate.astype(BF16)

    vshape = jax.ShapeDtypeStruct((1, Dm), F32)
    rrow = lambda w: pl.BlockSpec((CHUNK, w), lambda i: (rev(i), 0))
    return pl.pallas_call(
        body, name=name, grid=(nc,),
        out_shape=(jax.ShapeDtypeStruct((T, 4 * Dm), BF16), vshape, vshape),
        in_specs=[rrow(4 * Dm), _vec_spec(Dm), _vec_spec(Dm), rrow(Dm), rrow(Dm),
                  pl.BlockSpec((1, HG_HEADS, HG_DIM, HG_DIM), lambda i: (rev(i), 0, 0, 0))],
        out_specs=(rrow(4 * Dm), _vec_spec(Dm), _vec_spec(Dm)),
        scratch_shapes=[pltpu.VMEM((HG_HEADS, HG_DIM, HG_DIM), F32), pltpu.VMEM((HG_HEADS, 1, HG_DIM), F32)],
        compiler_params=_params(("arbitrary",)),
    )(p, lb, gn, o, dog, states)


def _lb_fwd(hg_lb, name):
    def body(a_ref, o_ref):
        a0, a1 = a_ref[0:1], a_ref[1:2]
        m = jnp.maximum(a0, a1)
        e0, e1 = jnp.exp(a0 - m), jnp.exp(a1 - m)
        p0, p1 = e0 / (e0 + e1), e1 / (e0 + e1)
        o_ref[0:1] = p0 - p0
        o_ref[1:2] = (p0 + p1) - p0

    return pl.pallas_call(body, name=name, out_shape=jax.ShapeDtypeStruct(hg_lb.shape, F32))(hg_lb)


def _lb_bwd(hg_lb, dlb_all, name):
    def body(a_ref, d_ref, o_ref):
        a0, a1 = a_ref[0:1], a_ref[1:2]
        m = jnp.maximum(a0, a1)
        e0, e1 = jnp.exp(a0 - m), jnp.exp(a1 - m)
        p0, p1 = e0 / (e0 + e1), e1 / (e0 + e1)
        d1 = d_ref[1:2]
        o_ref[0:1] = -p0 * p1 * d1
        o_ref[1:2] = p1 * (1.0 - p1) * d1

    return pl.pallas_call(body, name=name, out_shape=jax.ShapeDtypeStruct(hg_lb.shape, F32))(hg_lb, dlb_all)


CONV_COLS = 256


def _conv_fwd(a, w, b, name):
    T = a.shape[0]
    Fh = FFN_HIDDEN
    tr = _pick(T, (ROW_TILE,))
    cw = CONV_COLS
    hb = tr // 8

    def body(a_ref, ap_ref, w_ref, b_ref, m_ref):
        m0 = (pl.program_id(0) > 0).astype(F32)

        def conv(cc):
            x = jnp.concatenate([ap_ref[:, pl.ds(cc, cw)] * m0, a_ref[:, pl.ds(cc, cw)]], axis=0)
            wv = w_ref[:, pl.ds(cc, cw)]
            y = b_ref[:, pl.ds(cc, cw)] + wv[2:3] * x + wv[1:2] * pltpu.roll(x, 1, axis=0) \
                + wv[0:1] * pltpu.roll(x, 2, axis=0)
            return y[8:]

        def step(c, carry):
            c0 = pl.multiple_of(c * cw, cw)
            c1 = pl.multiple_of(Fh + c * cw, cw)
            yg, yv = conv(c0), conv(c1)
            m_ref[:, pl.ds(c0, cw)] = (yg * _cdf(yg) * yv).astype(BF16)
            return carry

        lax.fori_loop(0, Fh // cw, step, 0)

    return pl.pallas_call(
        body, name=name, out_shape=jax.ShapeDtypeStruct((T, Fh), BF16), grid=(T // tr,),
        in_specs=[_row_spec(tr, 2 * Fh), pl.BlockSpec((8, 2 * Fh), lambda i: (jnp.maximum(i * hb - 1, 0), 0)),
                  _vec_spec(2 * Fh, 3), _vec_spec(2 * Fh)],
        out_specs=_row_spec(tr, Fh), compiler_params=_params(("parallel",)),
    )(a, a, w, b)


def _conv_bwd(a, dm, w, b, name):
    T = a.shape[0]
    Fh = FFN_HIDDEN
    tr = _pick(T, (ROW_TILE,))
    cw = CONV_COLS
    hb = tr // 8
    nsteps = T // tr
    n = tr + 8

    def body(a_ref, ap_ref, an_ref, dm_ref, dmn_ref, w_ref, b_ref, da_ref, dw_ref, db_ref):
        i = pl.program_id(0)
        m0 = (i > 0).astype(F32)
        m1 = (i < nsteps - 1).astype(F32)

        @pl.when(i == 0)
        def _():
            dw_ref[...] = jnp.zeros_like(dw_ref)
            db_ref[...] = jnp.zeros_like(db_ref)

        def prep(cc):
            x = jnp.concatenate([ap_ref[:, pl.ds(cc, cw)] * m0, a_ref[:, pl.ds(cc, cw)],
                                 an_ref[:, pl.ds(cc, cw)] * m1], axis=0)
            wv = w_ref[:, pl.ds(cc, cw)]
            s1 = pltpu.roll(x, 1, axis=0)
            s2 = pltpu.roll(x, 2, axis=0)
            y = b_ref[:, pl.ds(cc, cw)] + wv[2:3] * x + wv[1:2] * s1 + wv[0:1] * s2
            return wv, x[8:], s1[8:], s2[8:], y[8:]

        def back(cc, dy, wv, x0, s1, s2):
            da = wv[2:3] * dy + wv[1:2] * pltpu.roll(dy, n - 1, axis=0) + wv[0:1] * pltpu.roll(dy, n - 2, axis=0)
            da_ref[:, pl.ds(cc, cw)] = da[:tr].astype(BF16)
            d = dy[:tr]
            db_ref[:, pl.ds(cc, cw)] += jnp.sum(d, axis=0, keepdims=True)
            dw_ref[2:3, pl.ds(cc, cw)] += jnp.sum(d * x0[:tr], axis=0, keepdims=True)
            dw_ref[1:2, pl.ds(cc, cw)] += jnp.sum(d * s1[:tr], axis=0, keepdims=True)
            dw_ref[0:1, pl.ds(cc, cw)] += jnp.sum(d * s2[:tr], axis=0, keepdims=True)

        def step(c, carry):
            c0 = pl.multiple_of(c * cw, cw)
            c1 = pl.multiple_of(Fh + c * cw, cw)
            dmx = jnp.concatenate([dm_ref[:, pl.ds(c0, cw)], dmn_ref[:, pl.ds(c0, cw)] * m1], axis=0)
            wg, xg, s1g, s2g, yg = prep(c0)
            wv, xv, s1v, s2v, yv = prep(c1)
            cg = _cdf(yg)
            back(c0, dmx * yv * (cg + yg * _pdf(yg)), wg, xg, s1g, s2g)
            back(c1, dmx * (yg * cg), wv, xv, s1v, s2v)
            return carry

        lax.fori_loop(0, Fh // cw, step, 0)

    prev = lambda wd: pl.BlockSpec((8, wd), lambda i: (jnp.maximum(i * hb - 1, 0), 0))
    nxt = lambda wd: pl.BlockSpec((8, wd), lambda i: (jnp.minimum((i + 1) * hb, T // 8 - 1), 0))
    return pl.pallas_call(
        body, name=name, grid=(nsteps,),
        out_shape=(jax.ShapeDtypeStruct((T, 2 * Fh), BF16), jax.ShapeDtypeStruct((3, 2 * Fh), F32),
                   jax.ShapeDtypeStruct((1, 2 * Fh), F32)),
        in_specs=[_row_spec(tr, 2 * Fh), prev(2 * Fh), nxt(2 * Fh), _row_spec(tr, Fh), nxt(Fh),
                  _vec_spec(2 * Fh, 3), _vec_spec(2 * Fh)],
        out_specs=(_row_spec(tr, 2 * Fh), _vec_spec(2 * Fh, 3), _vec_spec(2 * Fh)),
        compiler_params=_params(("arbitrary",)),
    )(a, a, a, dm, dm, w, b)


def _ada_fwd(c_all, ada_w, ada_b, name):
    L, Dm, cols = ada_w.shape
    tn = _pick(cols, (512, 256, 128))

    def body(c_ref, w_ref, b_ref, o_ref):
        cv = c_ref[...]
        cond = (cv * _sig(cv)).astype(BF16)
        o_ref[...] = _dot(cond, w_ref[...].astype(BF16), NN) + b_ref[...]

    return pl.pallas_call(
        body, name=name, out_shape=jax.ShapeDtypeStruct((L, N_DEV, cols), F32), grid=(L, cols // tn),
        in_specs=[pl.BlockSpec((N_DEV, Dm), lambda l, j: (0, 0)), pl.BlockSpec((None, Dm, tn), lambda l, j: (l, 0, j)),
                  pl.BlockSpec((None, 1, tn), lambda l, j: (l, 0, j))],
        out_specs=pl.BlockSpec((None, N_DEV, tn), lambda l, j: (l, 0, j)),
        compiler_params=_params(("parallel", "parallel")),
    )(c_all, ada_w, ada_b.reshape(L, 1, cols))


def _ada_bwd(c_all, dmod, name):
    L, _, cols = dmod.shape
    Dm = c_all.shape[1]
    tn = _pick(cols, (512, 256, 128))

    def body(c_ref, d_ref, o_ref):
        cv = c_ref[...]
        o_ref[...] = _dot(cv * _sig(cv), d_ref[...], TN, HI)

    return pl.pallas_call(
        body, name=name, out_shape=jax.ShapeDtypeStruct((L, Dm, cols), F32), grid=(L, cols // tn),
        in_specs=[pl.BlockSpec((N_DEV, Dm), lambda l, j: (0, 0)), pl.BlockSpec((None, N_DEV, tn), lambda l, j: (l, 0, j))],
        out_specs=pl.BlockSpec((None, Dm, tn), lambda l, j: (l, 0, j)),
        compiler_params=_params(("parallel", "parallel")),
    )(c_all, dmod)


def _add_own_half(g4, rb, core, name):
    S, _, rh, cw = g4.shape
    tr = _pick(rh, (256, 128, 176, 64))

    def body(core_ref, g_ref, r_ref, o_ref):
        o_ref[...] = (g_ref[...] + r_ref[...].astype(F32)).astype(BF16)

    return pl.pallas_call(
        body, name=name, out_shape=jax.ShapeDtypeStruct((S, rh, cw), BF16),
        grid_spec=pltpu.PrefetchScalarGridSpec(
            num_scalar_prefetch=1, grid=(S, rh // tr),
            in_specs=[pl.BlockSpec((None, None, tr, cw), lambda s, i, core_ref: (s, core_ref[0], i, 0)),
                      pl.BlockSpec((None, tr, cw), lambda s, i, core_ref: (s, i, 0))],
            out_specs=pl.BlockSpec((None, tr, cw), lambda s, i, core_ref: (s, i, 0))),
        compiler_params=_params(("parallel", "parallel")),
    )(core, g4, rb)


def _sum_chips(lands, sums, chip, name):
    L = len(lands)
    _, rh, cw = lands[0].shape
    tr = _pick(rh, (256, 128, 176, 64))

    def body(chip_ref, *refs):
        ld, cs, o_ref = refs[:L], refs[L:2 * L], refs[2 * L]
        me = chip_ref[0]
        for k in range(L):
            @pl.when(pl.program_id(0) == k)
            def _(k=k):
                own = cs[k][...].astype(F32)
                got = [ld[k][j].astype(F32) for j in range(3)]
                acc = None
                for t in range(N_CHIPS):
                    d = jnp.bitwise_xor(jnp.int32(t), me)
                    term = jnp.where(d == 0, own, jnp.where(d == 2, got[0], jnp.where(d == 1, got[1], got[2])))
                    acc = term if acc is None else acc + term
                o_ref[...] = acc

    frozen = lambda l, i, k: jnp.where(l == k, i, 0)
    in_specs = [pl.BlockSpec((3, tr, cw), lambda l, i, chip_ref, k=k: (0, frozen(l, i, k), 0)) for k in range(L)]
    in_specs += [pl.BlockSpec((None, tr, cw), lambda l, i, chip_ref, k=k: (chip_ref[0], frozen(l, i, k), 0))
                 for k in range(L)]
    return pl.pallas_call(
        body, name=name, out_shape=jax.ShapeDtypeStruct((L, rh, cw), F32),
        grid_spec=pltpu.PrefetchScalarGridSpec(
            num_scalar_prefetch=1, grid=(L, rh // tr), in_specs=in_specs,
            out_specs=pl.BlockSpec((None, tr, cw), lambda l, i, chip_ref: (l, i, 0))),
        compiler_params=_params(("arbitrary", "arbitrary")),
    )(chip, *lands, *sums)


def _sum_devices(gathered, name):
    n, R, _ = gathered.shape
    tr = _pick(R, (512, 256, 128, 64, 32, 16, 8))

    def body(g_ref, o_ref):
        acc = g_ref[0]
        for d in range(1, n):
            acc = acc + g_ref[d]
        o_ref[...] = acc

    return pl.pallas_call(
        body, name=name, out_shape=jax.ShapeDtypeStruct((R, LANES), F32), grid=(R // tr,),
        in_specs=[pl.BlockSpec((n, tr, LANES), lambda i: (0, i, 0))], out_specs=pl.BlockSpec((tr, LANES), lambda i: (i, 0)),
        compiler_params=_params(("parallel",)),
    )(gathered)


def _adamw(w, g, m, v, name):
    R, C = w.shape
    tr = _pick(R, (256, 128, 64, 32, 16, 8))
    c1 = 1.0 / (1.0 - ADAM_B1 ** ADAM_STEP)
    c2 = 1.0 / (1.0 - ADAM_B2 ** ADAM_STEP)

    def body(w_ref, g_ref, m_ref, v_ref, d_ref, mo_ref, vo_ref):
        gv = g_ref[...]
        m2 = ADAM_B1 * m_ref[...] + (1.0 - ADAM_B1) * gv
        v2 = ADAM_B2 * v_ref[...] + (1.0 - ADAM_B2) * (gv * gv)
        mo_ref[...] = m2
        vo_ref[...] = v2
        d_ref[...] = -ADAM_LR * ((m2 * c1) / (jnp.sqrt(v2 * c2) + ADAM_EPS) + ADAM_WD * w_ref[...])

    spec = pl.BlockSpec((tr, C), lambda i: (i, 0))
    shp = jax.ShapeDtypeStruct((R, C), F32)
    return pl.pallas_call(body, name=name, out_shape=(shp, shp, shp), grid=(R // tr,), in_specs=[spec] * 4,
                          out_specs=(spec, spec, spec), compiler_params=_params(("parallel",)))(w, g, m, v)


def _adamw_halves(w, own, recv, m, v, core, name):
    L, rh, cw = own.shape
    tr = _pick(rh, (256, 128, 176, 64))
    c1 = 1.0 / (1.0 - ADAM_B1 ** ADAM_STEP)
    c2 = 1.0 / (1.0 - ADAM_B2 ** ADAM_STEP)

    def body(core_ref, w_ref, own_ref, recv_ref, m_ref, v_ref, g_ref, d_ref, mo_ref, vo_ref):
        gv = jnp.where(pl.program_id(1) == core_ref[0], own_ref[...], recv_ref[...])
        g_ref[...] = gv
        m2 = ADAM_B1 * m_ref[...] + (1.0 - ADAM_B1) * gv
        v2 = ADAM_B2 * v_ref[...] + (1.0 - ADAM_B2) * (gv * gv)
        mo_ref[...] = m2
        vo_ref[...] = v2
        d_ref[...] = -ADAM_LR * ((m2 * c1) / (jnp.sqrt(v2 * c2) + ADAM_EPS) + ADAM_WD * w_ref[...])

    full = pl.BlockSpec((None, None, tr, cw), lambda l, hf, i, core_ref: (l, hf, i, 0))
    mine = pl.BlockSpec((None, tr, cw), lambda l, hf, i, core_ref: (l, jnp.where(hf == core_ref[0], i, 0), 0))
    other = pl.BlockSpec((None, tr, cw), lambda l, hf, i, core_ref: (l, jnp.where(hf == core_ref[0], 0, i), 0))
    shp = jax.ShapeDtypeStruct((L, 2, rh, cw), F32)
    view = lambda a: a.reshape(L, 2, rh, cw)
    outs = pl.pallas_call(
        body, name=name, out_shape=(shp, shp, shp, shp),
        grid_spec=pltpu.PrefetchScalarGridSpec(
            num_scalar_prefetch=1, grid=(L, 2, rh // tr), in_specs=[full, mine, other, full, full],
            out_specs=(full, full, full, full)),
        compiler_params=_params(("arbitrary", "arbitrary", "arbitrary")),
    )(core, view(w), own, recv, view(m), view(v))
    return tuple(o.reshape(L, 2 * rh, cw) for o in outs)


ANY = pl.BlockSpec(memory_space=pl.ANY)


def _position():
    x, y, c = lax.axis_index("x"), lax.axis_index("y"), lax.axis_index("c")
    return x, y, c


def _allgather(ins, out_shapes, src_fns, dst_fns, name, in_vmem):
    n = len(ins)

    def body(*refs):
        in_refs, out_refs = refs[:n], refs[n:2 * n]
        send_sems, recv_sems, local_sems = refs[2 * n:]
        x, y, c = _position()
        me, sibling = (x, y, c), (x, y, 1 - c)
        chips = [(1 - x, y), (x, 1 - y), (1 - x, 1 - y)]

        def copy(k, j, block, to, own=False):
            dst = dst_fns[k](out_refs[k], *block)
            return pltpu.make_async_remote_copy(
                src_ref=src_fns[k](in_refs[k], c) if own else dst, dst_ref=dst,
                send_sem=send_sems.at[k, j], recv_sem=recv_sems.at[k, j], device_id=to, device_id_type=MESH)

        mine = [pltpu.make_async_copy(src_fns[k](in_refs[k], c), dst_fns[k](out_refs[k], *me), local_sems.at[k])
                for k in range(n)]
        for cp in mine:
            cp.start()
        first = []
        for k in range(n):
            first.append(copy(k, 0, me, sibling, own=True))
            first += [copy(k, 1 + j, me, (*chip, c), own=True) for j, chip in enumerate(chips)]
        for cp in first:
            cp.start()
        passed = []
        for j, chip in enumerate(chips):
            for k in range(n):
                copy(k, 1 + j, (*chip, c), me).wait_recv()
                fwd = copy(k, 4 + j, (*chip, c), sibling)
                fwd.start()
                passed.append(fwd)
        for k in range(n):
            copy(k, 0, sibling, me).wait_recv()
        for j, chip in enumerate(chips):
            for k in range(n):
                copy(k, 4 + j, (*chip, 1 - c), me).wait_recv()
        for cp in first + passed:
            cp.wait_send()
        for cp in mine:
            cp.wait()

    spec = pl.BlockSpec(memory_space=pltpu.VMEM) if in_vmem else ANY
    return pl.pallas_call(
        body, name=name, out_shape=tuple(out_shapes), in_specs=[spec] * n, out_specs=tuple([spec] * n),
        scratch_shapes=[pltpu.SemaphoreType.DMA((n, 7)), pltpu.SemaphoreType.DMA((n, 7)),
                        pltpu.SemaphoreType.DMA((n,))],
        compiler_params=pltpu.CompilerParams(vmem_limit_bytes=VMEM_LIMIT_BYTES),
    )(*ins)


def _allgather_small(payload, name):
    R = payload.shape[0]
    (out,) = _allgather(
        [payload], [jax.ShapeDtypeStruct((N_DEV, R, LANES), F32)],
        [lambda ref, c: ref], [lambda ref, px, py, pc: ref.at[4 * px + 2 * py + pc]], name, in_vmem=True)
    return out


def _swap_sibling(ins, name):
    n = len(ins)

    def body(*refs):
        in_refs, out_refs = refs[:n], refs[n:2 * n]
        send_sems, recv_sems = refs[2 * n:]
        x, y, c = _position()
        copies = [pltpu.make_async_remote_copy(
            src_ref=in_refs[k], dst_ref=out_refs[k], send_sem=send_sems.at[k], recv_sem=recv_sems.at[k],
            device_id=(x, y, 1 - c), device_id_type=MESH) for k in range(n)]
        for cp in copies:
            cp.start()
        for cp in copies:
            cp.wait_recv()
        for cp in copies:
            cp.wait_send()

    return pl.pallas_call(
        body, name=name, out_shape=tuple(jax.ShapeDtypeStruct(a.shape, a.dtype) for a in ins),
        in_specs=[ANY] * n, out_specs=tuple([ANY] * n),
        scratch_shapes=[pltpu.SemaphoreType.DMA((n,)), pltpu.SemaphoreType.DMA((n,))],
        compiler_params=pltpu.CompilerParams(vmem_limit_bytes=VMEM_LIMIT_BYTES),
    )(*ins)


HBM_SPEC = pl.BlockSpec(memory_space=pltpu.HBM)
SEM_SPEC = pl.BlockSpec(memory_space=pltpu.SEMAPHORE)
SPLIT_PARAMS = pltpu.CompilerParams(has_side_effects=pltpu.SideEffectType.DATAFLOW_SIDE_EFFECTING)
TOKEN = jax.ShapeDtypeStruct((8, LANES), F32)


def _hbm(a):
    return pltpu.with_memory_space_constraint(a, pltpu.HBM)


def _weight_window(ref, col, r, cw, px, py, pc):
    rh = r // 2
    if col:
        return ref.at[pl.ds(pc * rh, rh), pl.ds((2 * px + py) * cw, cw)]
    return ref.at[pl.ds((2 * px + py) * r + pc * rh, rh), :]


def _peers(x, y, c):
    return [(x, y, 1 - c), (1 - x, y, c), (x, 1 - y, c), (1 - x, 1 - y, c)]


def _place_own(shard, col, pos, name):
    r, cw = shard.shape
    rh = r // 2
    tr = _pick(rh, (256, 128, 176, 64))
    nb = rh // tr
    shape = (r, N_CHIPS * cw) if col else (N_CHIPS * r, cw)

    def body(pos_ref, x_ref, o_ref):
        o_ref[...] = x_ref[...]

    if col:
        out_idx = lambda i, pos_ref: (pos_ref[1] * nb + i, pos_ref[0])
    else:
        out_idx = lambda i, pos_ref: (pos_ref[0] * (2 * nb) + pos_ref[1] * nb + i, 0)
    return pl.pallas_call(
        body, name=name, out_shape=jax.ShapeDtypeStruct(shape, shard.dtype),
        grid_spec=pltpu.PrefetchScalarGridSpec(
            num_scalar_prefetch=1, grid=(nb,),
            in_specs=[pl.BlockSpec((tr, cw), lambda i, pos_ref: (pos_ref[1] * nb + i, 0))],
            out_specs=pl.BlockSpec((tr, cw), out_idx)),
        compiler_params=_params(("arbitrary",)),
    )(pos, shard)


def _gather_start(shards, lands, cols, per_layer, name):
    n = len(shards)
    nl = n // per_layer

    def body(*refs):
        sh, ld = refs[:n], refs[n:2 * n]
        sems, token = refs[2 * n:2 * n + 2 * nl], refs[-1]
        x, y, c = _position()
        for k in range(n):
            l, a = divmod(k, per_layer)
            r, cw = shards[k].shape
            src = sh[k].at[pl.ds(c * (r // 2), r // 2), :]
            dst = _weight_window(ld[k], cols[k], r, cw, x, y, c)
            for j, peer in enumerate(_peers(x, y, c)):
                pltpu.make_async_remote_copy(src_ref=src, dst_ref=dst, send_sem=sems[2 * l].at[4 * a + j],
                                             recv_sem=sems[2 * l + 1].at[4 * a + j], device_id=peer,
                                             device_id_type=MESH).start()
        token[...] = jnp.zeros_like(token)

    arrs = list(shards) + list(lands)
    out = pl.pallas_call(
        body, name=name,
        out_shape=tuple(pltpu.SemaphoreType.DMA((per_layer * 4,)) for _ in range(2 * nl))
        + tuple(pltpu.HBM(a.shape, a.dtype) for a in arrs) + (TOKEN,),
        in_specs=[HBM_SPEC] * (2 * n),
        out_specs=(SEM_SPEC,) * (2 * nl) + (HBM_SPEC,) * (2 * n) + (pl.BlockSpec(memory_space=pltpu.VMEM),),
        input_output_aliases={i: 2 * nl + i for i in range(2 * n)}, compiler_params=SPLIT_PARAMS,
    )(*[_hbm(a) for a in arrs])
    return out[:2 * nl], out[2 * nl:2 * nl + n], out[2 * nl + n:2 * nl + 2 * n], out[-1]


def _gather_wait(shards, lands, send, recv, after, cols, name):
    m = len(shards)

    def body(*refs):
        sh, ld = refs[:m], refs[m:2 * m]
        send_ref, recv_ref = refs[2 * m], refs[2 * m + 1]
        x, y, c = _position()
        for a in range(m):
            r, cw = shards[a].shape
            src = sh[a].at[pl.ds(c * (r // 2), r // 2), :]
            for j, (px, py, pc) in enumerate(_peers(x, y, c)):
                cp = pltpu.make_async_remote_copy(
                    src_ref=src, dst_ref=_weight_window(ld[a], cols[a], r, cw, px, py, pc),
                    send_sem=send_ref.at[4 * a + j], recv_sem=recv_ref.at[4 * a + j], device_id=(px, py, pc),
                    device_id_type=MESH)
                cp.wait_send()
                cp.wait_recv()

    arrs = list(shards) + list(lands)
    out = pl.pallas_call(
        body, name=name, out_shape=tuple(pltpu.HBM(a.shape, a.dtype) for a in arrs),
        in_specs=[HBM_SPEC] * (2 * m) + [SEM_SPEC, SEM_SPEC, ANY], out_specs=(HBM_SPEC,) * (2 * m),
        input_output_aliases={i: i for i in range(2 * m)}, compiler_params=SPLIT_PARAMS,
    )(*arrs, send, recv, after)
    return out[m:]


def _forward_sibling(lands, cols, shard_shapes, name):
    m = len(lands)

    def body(*refs):
        ins, outs = refs[:m], refs[m:2 * m]
        send_sems, recv_sems = refs[2 * m:]
        x, y, c = _position()
        chips = [(1 - x, y), (x, 1 - y), (1 - x, 1 - y)]
        sends = []
        for a in range(m):
            r, cw = shard_shapes[a]
            for j, (px, py) in enumerate(chips):
                cp = pltpu.make_async_remote_copy(
                    src_ref=_weight_window(ins[a], cols[a], r, cw, px, py, c),
                    dst_ref=_weight_window(outs[a], cols[a], r, cw, px, py, c),
                    send_sem=send_sems.at[a, j], recv_sem=recv_sems.at[a, j], device_id=(x, y, 1 - c),
                    device_id_type=MESH)
                cp.start()
                sends.append(cp)
        for a in range(m):
            r, cw = shard_shapes[a]
            for j, (px, py) in enumerate(chips):
                pltpu.make_async_remote_copy(
                    src_ref=_weight_window(ins[a], cols[a], r, cw, px, py, c),
                    dst_ref=_weight_window(outs[a], cols[a], r, cw, px, py, 1 - c),
                    send_sem=send_sems.at[a, j], recv_sem=recv_sems.at[a, j], device_id=(x, y, 1 - c),
                    device_id_type=MESH).wait_recv()
        for cp in sends:
            cp.wait_send()

    return pl.pallas_call(
        body, name=name, out_shape=tuple(jax.ShapeDtypeStruct(a.shape, a.dtype) for a in lands),
        in_specs=[ANY] * m, out_specs=tuple([ANY] * m), input_output_aliases={i: i for i in range(m)},
        scratch_shapes=[pltpu.SemaphoreType.DMA((m, 3)), pltpu.SemaphoreType.DMA((m, 3))],
        compiler_params=pltpu.CompilerParams(vmem_limit_bytes=VMEM_LIMIT_BYTES),
    )(*lands)


def _exchange_start(sums, name):
    m = len(sums)
    lands = [lax.empty((3,) + s.shape[1:], s.dtype) for s in sums]

    def body(*refs):
        cs, ld = refs[:m], refs[m:2 * m]
        send_ref, recv_ref, token = refs[2 * m], refs[2 * m + 1], refs[-1]
        x, y, c = _position()
        for a in range(m):
            for j, (px, py) in enumerate([(1 - x, y), (x, 1 - y), (1 - x, 1 - y)]):
                pltpu.make_async_remote_copy(
                    src_ref=cs[a].at[2 * px + py], dst_ref=ld[a].at[j], send_sem=send_ref.at[3 * a + j],
                    recv_sem=recv_ref.at[3 * a + j], device_id=(px, py, c), device_id_type=MESH).start()
        token[...] = jnp.zeros_like(token)

    arrs = list(sums) + lands
    out = pl.pallas_call(
        body, name=name,
        out_shape=(pltpu.SemaphoreType.DMA((m * 3,)), pltpu.SemaphoreType.DMA((m * 3,)))
        + tuple(pltpu.HBM(a.shape, a.dtype) for a in arrs) + (TOKEN,),
        in_specs=[HBM_SPEC] * (2 * m),
        out_specs=(SEM_SPEC, SEM_SPEC) + (HBM_SPEC,) * (2 * m) + (pl.BlockSpec(memory_space=pltpu.VMEM),),
        input_output_aliases={i: 2 + i for i in range(2 * m)}, compiler_params=SPLIT_PARAMS,
    )(*[_hbm(a) for a in arrs])
    return out[0], out[1], out[2:2 + m], out[2 + m:2 + 2 * m], out[-1]


def _exchange_wait(sums, lands, send, recv, after, name):
    m = len(sums)

    def body(*refs):
        cs, ld = refs[:m], refs[m:2 * m]
        send_ref, recv_ref = refs[2 * m], refs[2 * m + 1]
        x, y, c = _position()
        for a in range(m):
            for j, (px, py) in enumerate([(1 - x, y), (x, 1 - y), (1 - x, 1 - y)]):
                cp = pltpu.make_async_remote_copy(
                    src_ref=cs[a].at[2 * px + py], dst_ref=ld[a].at[j], send_sem=send_ref.at[3 * a + j],
                    recv_sem=recv_ref.at[3 * a + j], device_id=(px, py, c), device_id_type=MESH)
                cp.wait_send()
                cp.wait_recv()

    arrs = list(sums) + list(lands)
    out = pl.pallas_call(
        body, name=name, out_shape=tuple(pltpu.HBM(a.shape, a.dtype) for a in arrs),
        in_specs=[HBM_SPEC] * (2 * m) + [SEM_SPEC, SEM_SPEC, ANY], out_specs=(HBM_SPEC,) * (2 * m),
        input_output_aliases={i: i for i in range(2 * m)}, compiler_params=SPLIT_PARAMS,
    )(*arrs, send, recv, after)
    return out[:m], out[m:]


def _vec(a):
    return a.reshape(1, -1)


def _local_step(x, tgt, mod, W, P, get_w=None, on_grads=None):
    Dm = D_MODEL
    G = {k: [] for k in ("gm_w_in", "gm_w_out", "hg_w_in", "hg_w_out", "ffn_w_up", "ffn_w_down")}
    lb_all = _lb_fwd(P["hg_lb"], "lb_fwd")
    saved = []
    xs = x
    y_prev = gate_prev = None
    layer_w = [None] * DEPTH

    def wmm(xa, kind, i, mode, name):
        if layer_w[i] is not None:
            return _mm(xa, layer_w[i][kind], mode, name)
        return _mm(xa, W[kind], mode, name, b_layer=i if kind.startswith("ffn") else i // 2)

    for i in range(DEPTH):
        m = [_vec(mod[i, j * Dm:(j + 1) * Dm]) for j in range(6)]
        sh1, sc1, g1, sh2, sc2, g2 = m
        j = i // 2
        if get_w is not None:
            layer_w[i] = get_w(i, xs if y_prev is None else y_prev)
        xs, h = _norm_fwd(xs, y_prev, gate_prev, _vec(P["norm_g"][i, 0]), sc1, sh1, f"norm_fwd_a{i}")
        rec = dict(x1=xs, h1=h)
        if i % 2 == 0:
            a = wmm(h, "gm_w_in", i, "nn", f"gm_in{i}")
            gated = _gm_mid_fwd(a, _vec(P["gm_ln_g"][j]), _vec(P["gm_ln_b"][j]), P["gm_w_s"][j],
                                P["gm_b_s"][j].reshape(GM_HEADS, GM_BLOCK, 1), f"gm_mid_fwd{i}")
            y1 = wmm(gated, "gm_w_out", i, "nn", f"gm_out{i}")
            rec.update(a=a, act=gated)
        else:
            p = wmm(h, "hg_w_in", i, "nn", f"hg_in{i}")
            o, og, states = _hg_scan_fwd(p, _vec(lb_all[j]), _vec(P["hg_gn_g"][j]), f"hg_scan_fwd{i}")
            y1 = wmm(og, "hg_w_out", i, "nn", f"hg_out{i}")
            rec.update(a=p, act=og, o=o, states=states)
        rec["y1"] = y1
        xs, h2 = _norm_fwd(xs, y1, g1, _vec(P["norm_g"][i, 1]), sc2, sh2, f"norm_fwd_b{i}")
        a2 = wmm(h2, "ffn_w_up", i, "nn", f"ffn_up{i}")
        mm_ = _conv_fwd(a2, P["ffn_conv_w"][i], _vec(P["ffn_conv_b"][i]), f"conv_fwd{i}")
        y2 = wmm(mm_, "ffn_w_down", i, "nn", f"ffn_down{i}")
        rec.update(x2=xs, h2=h2, a2=a2, m=mm_, y2=y2, mods=m)
        saved.append(rec)
        y_prev, gate_prev = y2, g2
    dx, dy, loss, s_fg, s_gate = _loss_head(xs, y_prev, gate_prev, _vec(P["final_g"]), tgt, "loss_head")
    small = dict(final_g=s_fg, norm_g=[None] * DEPTH, dmod=[None] * DEPTH, ffn_conv_w=[None] * DEPTH,
                 ffn_conv_b=[None] * DEPTH, gm_ln_g=[None] * 2, gm_ln_b=[None] * 2, gm_w_s=[None] * 2,
                 gm_b_s=[None] * 2, hg_gn_g=[None] * 2, dlb=[None] * 2)
    for i in reversed(range(DEPTH)):
        rec = saved[i]
        sh1, sc1, g1, sh2, sc2, g2 = rec["mods"]
        j = i // 2
        d_g2 = s_gate
        dm = wmm(dy, "ffn_w_down", i, "nt", f"ffn_down_dx{i}")
        G["ffn_w_down"].append(_mm(rec["m"], dy, "tn", f"ffn_down_dw{i}"))
        da2, dcw, dcb = _conv_bwd(rec["a2"], dm, P["ffn_conv_w"][i], _vec(P["ffn_conv_b"][i]), f"conv_bwd{i}")
        small["ffn_conv_w"][i], small["ffn_conv_b"][i] = dcw, dcb
        dh2 = wmm(da2, "ffn_w_up", i, "nt", f"ffn_up_dx{i}")
        G["ffn_w_up"].append(_mm(rec["h2"], da2, "tn", f"ffn_up_dw{i}", exchange_out=True))
        ng2 = _vec(P["norm_g"][i, 1])
        dx, dy, s_sh2, s_x2, d_g1 = _norm_bwd(rec["x2"], dh2, dx, ng2, sc2, rec["y1"], g1, f"norm_bwd_b{i}")
        d_sc2, d_ng2 = s_x2 * ng2, s_x2 * (1.0 + sc2)
        if i % 2 == 0:
            dgated = wmm(dy, "gm_w_out", i, "nt", f"gm_out_dx{i}")
            G["gm_w_out"].append(_mm(rec["act"], dy, "tn", f"gm_out_dw{i}"))
            da, dws, dbs, dlg, dlbeta = _gm_mid_bwd(
                rec["a"], dgated, _vec(P["gm_ln_g"][j]), _vec(P["gm_ln_b"][j]), P["gm_w_s"][j],
                P["gm_b_s"][j].reshape(GM_HEADS, GM_BLOCK, 1), f"gm_mid_bwd{i}")
            small["gm_w_s"][j], small["gm_b_s"][j] = dws, dbs[:, :GM_HEADS].T
            small["gm_ln_g"][j], small["gm_ln_b"][j] = dlg, dlbeta
            dh1 = wmm(da, "gm_w_in", i, "nt", f"gm_in_dx{i}")
            G["gm_w_in"].append(_mm(rec["h1"], da, "tn", f"gm_in_dw{i}", exchange_out=True))
        else:
            dog = wmm(dy, "hg_w_out", i, "nt", f"hg_out_dx{i}")
            G["hg_w_out"].append(_mm(rec["act"], dy, "tn", f"hg_out_dw{i}"))
            dp, dlb, dgn = _hg_scan_bwd(rec["a"], _vec(lb_all[j]), _vec(P["hg_gn_g"][j]), rec["o"], dog,
                                        rec["states"], f"hg_scan_bwd{i}")
            small["dlb"][j], small["hg_gn_g"][j] = dlb, dgn
            dh1 = wmm(dp, "hg_w_in", i, "nt", f"hg_in_dx{i}")
            G["hg_w_in"].append(_mm(rec["h1"], dp, "tn", f"hg_in_dw{i}", exchange_out=True))
        ng1 = _vec(P["norm_g"][i, 0])
        if on_grads is not None:
            mixer = ("gm_w_in", "gm_w_out") if i % 2 == 0 else ("hg_w_in", "hg_w_out")
            ng1 = ng1 + on_grads(i, {k: G[k][-1] for k in mixer + ("ffn_w_up", "ffn_w_down")})
        if i > 0:
            prev = saved[i - 1]
            dx, dy, s_sh1, s_x1, s_gate = _norm_bwd(rec["x1"], dh1, dx, ng1, sc1, prev["y2"], prev["mods"][5],
                                                    f"norm_bwd_a{i}")
        else:
            dx, s_sh1, s_x1 = _norm_bwd(rec["x1"], dh1, dx, ng1, sc1, None, None, f"norm_bwd_a{i}")
        d_sc1, d_ng1 = s_x1 * ng1, s_x1 * (1.0 + sc1)
        small["norm_g"][i] = jnp.concatenate([d_ng1, d_ng2], axis=0)
        small["dmod"][i] = jnp.concatenate([s_sh1, d_sc1, d_g1, s_sh2, d_sc2, d_g2], axis=1)
    for k in G:
        G[k] = G[k][::-1]
    dlb_all = jnp.concatenate(small.pop("dlb"), axis=0)
    small["hg_lb"] = _lb_bwd(P["hg_lb"], dlb_all, "lb_bwd")
    return loss, dx, G, small


BIG = ("gm_w_in", "gm_w_out", "hg_w_in", "hg_w_out", "ffn_w_up", "ffn_w_down")
COL_SHARDED = dict(gm_w_in=True, gm_w_out=False, hg_w_in=True, hg_w_out=False, ffn_w_up=True, ffn_w_down=False)
LAYER_WEIGHTS = 4


def _layer_kinds(i):
    return (("gm_w_in", "gm_w_out") if i % 2 == 0 else ("hg_w_in", "hg_w_out")) + ("ffn_w_up", "ffn_w_down")


def _pack(pieces):
    flat = [p.reshape(-1).astype(F32) for p in pieces]
    offs, tot = [], 0
    for f in flat:
        offs.append((tot, f.shape[0]))
        tot += f.shape[0]
    padded = -(-tot // (8 * LANES)) * (8 * LANES)
    if padded > tot:
        flat.append(jnp.zeros((padded - tot,), F32))
    return jnp.concatenate(flat).reshape(-1, LANES), offs


def _unpack(rows, offs, shapes):
    lead = rows.shape[:-2]
    flat = rows.reshape(lead + (-1,))
    return [flat[..., o:o + n].reshape(lead + tuple(s)) for (o, n), s in zip(offs, shapes)]


def _from_chips(per_dev, axis):
    per_chip = per_dev[0::2]
    return jnp.concatenate([per_chip[s] for s in range(N_CHIPS)], axis=axis)


def kernel(x, c, gm_w_in, gm_ln_g, gm_ln_b, gm_w_s, gm_b_s, gm_w_out, hg_w_in, hg_lb, hg_gn_g, hg_w_out, ffn_w_up, ffn_conv_w, ffn_conv_b, ffn_w_down, norm_g, ada_w, ada_b, final_g, loss_target, m_gm_w_in, m_gm_ln_g, m_gm_ln_b, m_gm_w_s, m_gm_b_s, m_gm_w_out, m_hg_w_in, m_hg_lb, m_hg_gn_g, m_hg_w_out, m_ffn_w_up, m_ffn_conv_w, m_ffn_conv_b, m_ffn_w_down, m_norm_g, m_ada_w, m_ada_b, m_final_g, v_gm_w_in, v_gm_ln_g, v_gm_ln_b, v_gm_w_s, v_gm_b_s, v_gm_w_out, v_hg_w_in, v_hg_lb, v_hg_gn_g, v_hg_w_out, v_ffn_w_up, v_ffn_conv_w, v_ffn_conv_b, v_ffn_w_down, v_norm_g, v_ada_w, v_ada_b, v_final_g):
    Dm = D_MODEL
    xi, yi, ci = _position()
    chip = 2 * xi + yi
    dev = 4 * xi + 2 * yi + ci
    weights = dict(gm_w_in=gm_w_in, gm_ln_g=gm_ln_g, gm_ln_b=gm_ln_b, gm_w_s=gm_w_s, gm_b_s=gm_b_s,
                   gm_w_out=gm_w_out, hg_w_in=hg_w_in, hg_lb=hg_lb, hg_gn_g=hg_gn_g, hg_w_out=hg_w_out,
                   ffn_w_up=ffn_w_up, ffn_conv_w=ffn_conv_w, ffn_conv_b=ffn_conv_b, ffn_w_down=ffn_w_down,
                   norm_g=norm_g, ada_w=ada_w, ada_b=ada_b, final_g=final_g)
    mom_m = dict(gm_w_in=m_gm_w_in, gm_ln_g=m_gm_ln_g, gm_ln_b=m_gm_ln_b, gm_w_s=m_gm_w_s, gm_b_s=m_gm_b_s,
                 gm_w_out=m_gm_w_out, hg_w_in=m_hg_w_in, hg_lb=m_hg_lb, hg_gn_g=m_hg_gn_g, hg_w_out=m_hg_w_out,
                 ffn_w_up=m_ffn_w_up, ffn_conv_w=m_ffn_conv_w, ffn_conv_b=m_ffn_conv_b, ffn_w_down=m_ffn_w_down,
                 norm_g=m_norm_g, ada_w=m_ada_w, ada_b=m_ada_b, final_g=m_final_g)
    mom_v = dict(gm_w_in=v_gm_w_in, gm_ln_g=v_gm_ln_g, gm_ln_b=v_gm_ln_b, gm_w_s=v_gm_w_s, gm_b_s=v_gm_b_s,
                 gm_w_out=v_gm_w_out, hg_w_in=v_hg_w_in, hg_lb=v_hg_lb, hg_gn_g=v_hg_gn_g, hg_w_out=v_hg_w_out,
                 ffn_w_up=v_ffn_w_up, ffn_conv_w=v_ffn_conv_w, ffn_conv_b=v_ffn_conv_b, ffn_w_down=v_ffn_w_down,
                 norm_g=v_norm_g, ada_w=v_ada_w, ada_b=v_ada_b, final_g=v_final_g)
    order = list(weights)

    pos = jnp.stack([chip, ci]).astype(jnp.int32)
    shards, by_col = [], []
    for i in range(DEPTH):
        for k in _layer_kinds(i):
            shards.append(weights[k][i if k.startswith("ffn") else i // 2].astype(BF16))
            by_col.append(COL_SHARDED[k])
    placed = [_place_own(sh, col, pos, f"place_own{n}") for n, (sh, col) in enumerate(zip(shards, by_col))]
    gsems, sh_thru, ld_thru, _ = _gather_start(shards, placed, by_col, LAYER_WEIGHTS, "gather_start")

    pieces = [c, hg_lb, hg_gn_g, norm_g, ffn_conv_w]
    payload, offs = _pack(pieces)
    got = _allgather_small(payload, "gather_small")
    c_g, lb_g, gn_g, ng_g, cw_g = _unpack(got, offs, [p.shape for p in pieces])
    c_all = c_g.reshape(N_DEV, Dm)
    P = dict(hg_lb=_from_chips(lb_g, 1), hg_gn_g=_from_chips(gn_g, 1), norm_g=_from_chips(ng_g, 2),
             ffn_conv_w=_from_chips(cw_g, 2), gm_ln_g=gm_ln_g, gm_ln_b=gm_ln_b, gm_w_s=gm_w_s, gm_b_s=gm_b_s,
             ffn_conv_b=ffn_conv_b, final_g=final_g)

    cols = ada_w.shape[2]
    ada_b_sh = lax.dynamic_slice_in_dim(ada_b, chip * cols, cols, axis=1)
    mod_sh = _ada_fwd(c_all, ada_w, ada_b_sh, "ada_fwd")
    mod_g = _allgather_small(mod_sh.reshape(-1, LANES), "gather_mod").reshape(N_DEV, DEPTH, N_DEV, cols)
    mod_mine = lax.dynamic_index_in_dim(mod_g[0::2], dev, axis=2, keepdims=False)
    mod = jnp.transpose(mod_mine, (1, 0, 2)).reshape(DEPTH, N_CHIPS * cols)

    core = jnp.reshape(ci, (1,)).astype(jnp.int32)
    chip_arr = jnp.reshape(chip, (1,)).astype(jnp.int32)
    pending = [None] * DEPTH

    def get_w(i, after):
        s = slice(LAYER_WEIGHTS * i, LAYER_WEIGHTS * (i + 1))
        landed = _gather_wait(sh_thru[s], ld_thru[s], gsems[2 * i], gsems[2 * i + 1], after, by_col[s],
                              f"gather_wait{i}")
        full = _forward_sibling(landed, by_col[s], [a.shape for a in shards[s]], f"gather_forward{i}")
        return dict(zip(_layer_kinds(i), full))

    def on_grads(i, gdict):
        kinds = _layer_kinds(i)
        g4 = []
        for k in kinds:
            g = gdict[k]
            if not COL_SHARDED[k]:
                R, C = g.shape
                g = g.reshape(N_CHIPS, 2, R // (2 * N_CHIPS), C)
            g4.append(g)
        to_sib = [lax.dynamic_index_in_dim(g, 1 - ci, axis=1, keepdims=False).astype(BF16) for g in g4]
        from_sib = _swap_sibling(to_sib, f"reduce_swap{i}")
        sums = [_add_own_half(g, r, core, f"chip_sum_{k}{i}") for g, r, k in zip(g4, from_sib, kinds)]
        send, recv, sums_thru, lands, token = _exchange_start(sums, f"reduce_start{i}")
        pending[i] = (send, recv, sums_thru, lands)
        return token[0, 0]

    loss_part, dx, G, small = _local_step(x[0], loss_target[0], mod, None, P, get_w, on_grads)

    by_kind = {k: ([], []) for k in BIG}
    for i in range(DEPTH):
        send, recv, sums_thru, lands = pending[i]
        sums_i, lands_i = _exchange_wait(sums_thru, lands, send, recv, dx, f"reduce_wait{i}")
        for k, s_, l_ in zip(_layer_kinds(i), sums_i, lands_i):
            by_kind[k][0].append(l_)
            by_kind[k][1].append(s_)
    own_halves = [_sum_chips(by_kind[k][0], by_kind[k][1], chip_arr, f"sum_chips_{k}") for k in BIG]
    sib_halves = _swap_sibling(own_halves, "reduce_join")
    grads, deltas, new_m, new_v = {}, {}, {}, {}
    for k, own, recv in zip(BIG, own_halves, sib_halves):
        grads[k], deltas[k], new_m[k], new_v[k] = _adamw_halves(
            weights[k], own, recv, mom_m[k], mom_v[k], core, f"adamw_{k}")

    sum_pieces = [loss_part[:, :1], small["final_g"], jnp.stack(small["gm_ln_g"]), jnp.stack(small["gm_ln_b"]),
                  jnp.stack(small["gm_w_s"]), jnp.stack(small["gm_b_s"]), jnp.stack(small["ffn_conv_b"]),
                  small["hg_lb"], jnp.stack(small["hg_gn_g"]), jnp.stack(small["norm_g"]),
                  jnp.stack(small["ffn_conv_w"])]
    dmod = jnp.concatenate(small["dmod"], axis=0)
    payload2, offs2 = _pack(sum_pieces + [dmod])
    got2 = _allgather_small(payload2, "gather_grads")
    dmod_all = _unpack(got2, offs2[-1:], [dmod.shape])[0]
    summed = _sum_devices(got2, "sum_devices")
    (loss_s, d_final_g, d_ln_g, d_ln_b, d_ws, d_bs, d_cb, d_lb, d_gn, d_ng, d_cw) = _unpack(
        summed, offs2[:-1], [(1,), final_g.shape, gm_ln_g.shape, gm_ln_b.shape, gm_w_s.shape, gm_b_s.shape,
                             ffn_conv_b.shape, (2, Dm), (2, Dm), (DEPTH, 2, Dm), (DEPTH, 3, 2 * FFN_HIDDEN)])
    grads.update(final_g=d_final_g, gm_ln_g=d_ln_g, gm_ln_b=d_ln_b, gm_w_s=d_ws, gm_b_s=d_bs, ffn_conv_b=d_cb)
    grads["hg_lb"] = lax.dynamic_slice_in_dim(d_lb, chip * hg_lb.shape[1], hg_lb.shape[1], axis=1)
    grads["hg_gn_g"] = lax.dynamic_slice_in_dim(d_gn, chip * hg_gn_g.shape[1], hg_gn_g.shape[1], axis=1)
    grads["norm_g"] = lax.dynamic_slice_in_dim(d_ng, chip * norm_g.shape[2], norm_g.shape[2], axis=2)
    grads["ffn_conv_w"] = lax.dynamic_slice_in_dim(d_cw, chip * ffn_conv_w.shape[2], ffn_conv_w.shape[2], axis=2)
    dmod_sh = lax.dynamic_slice_in_dim(dmod_all, chip * cols, cols, axis=2)
    grads["ada_w"] = _ada_bwd(c_all, jnp.transpose(dmod_sh, (1, 0, 2)), "ada_bwd")
    grads["ada_b"] = _sum_devices(dmod_all.reshape(N_DEV, -1, LANES), "sum_ada_b").reshape(ada_b.shape)

    for k in order:
        if k in BIG:
            continue
        w = weights[k]
        shp = w.shape
        view = (-1, shp[-1]) if w.ndim > 1 else (8, -1)
        d, m2, v2 = _adamw(w.reshape(view), grads[k].reshape(view), mom_m[k].reshape(view), mom_v[k].reshape(view),
                           f"adamw_{k}")
        deltas[k], new_m[k], new_v[k] = d.reshape(shp), m2.reshape(shp), v2.reshape(shp)
        grads[k] = grads[k].reshape(shp)

    loss = loss_s.reshape(())
    return (loss, dx[None], *[grads[k] for k in order], *[deltas[k] for k in order],
            *[new_m[k] for k in order], *[new_v[k] for k in order])
```

```python
import functools

import jax
import jax.numpy as jnp
from jax import lax
from jax.experimental import pallas as pl
from jax.experimental.pallas import tpu as pltpu

F32 = jnp.float32
BF16 = jnp.bfloat16
HI = lax.Precision.HIGHEST
X3 = lax.Precision.HIGH
MESH = pl.DeviceIdType.MESH

D_MODEL = 1024
DEPTH = 4
EPS = 1e-6
GM_WIDTH = 2048
GM_HEADS = 8
GM_BLOCK = 128
GM_HEAD_DIM = 256
CHUNK = 64
HG_HEADS = 8
HG_DIM = 128
FFN_HIDDEN = 2816
N_CHIPS = 4
N_DEV = 8

ADAM_LR = 0.001
ADAM_B1 = 0.9
ADAM_B2 = 0.999
ADAM_EPS = 1e-08
ADAM_WD = 0.01
ADAM_STEP = 10

VMEM_LIMIT_BYTES = 56 * 1024 * 1024
ROW_TILE = 256
LANES = 128

_SQRT_HALF = 0.7071067811865476
_INV_SQRT_2PI = 0.3989422804014327


def _pick(dim, prefs):
    for p in prefs:
        if dim % p == 0:
            return p
    return dim


def _params(sem):
    return pltpu.CompilerParams(dimension_semantics=sem, vmem_limit_bytes=VMEM_LIMIT_BYTES)


def _cdf(x):
    return 0.5 * (1.0 + lax.erf(x * _SQRT_HALF))


def _pdf(x):
    return jnp.exp(-0.5 * x * x) * _INV_SQRT_2PI


def _sig(x):
    return jax.nn.sigmoid(x)


def _dot(a, b, dims, prec=None):
    return lax.dot_general(a, b, (dims, ((), ())), precision=prec, preferred_element_type=F32)


NN = ((1,), (0,))
NT = ((1,), (1,))
TN = ((0,), (0,))


MM_VMEM_BUDGET = 40 * 1024 * 1024


def _mm_tiles(mode, M, N, K, a_bytes, b_bytes, exchange_out):
    tn = _pick(N, (1408, 1024, 512, 256, 128))
    tms = [t for t in (1408, 1024, 512, 256, 128) if M % t == 0 and not (exchange_out and (M // 2) % t)] or [M]
    tks = [K] + [t for t in (2816, 2048, 1408, 1024, 512, 256, 128) if t < K and K % t == 0]

    def fits(tm, tk):
        acc = tm * tn * 4 if tk < K else 0
        return 2 * tm * tk * a_bytes + 2 * tk * tn * b_bytes + 2 * tm * tn * 4 + acc <= MM_VMEM_BUDGET

    for min_tm in (min(512, tms[0]), 0):
        for tk in tks:
            for tm in tms:
                if tm >= min_tm and fits(tm, tk):
                    return tm, tn, tk
    return tms[-1], tn, tks[-1]


def _mm(a, b, mode, name, b_layer=None, out_dtype=F32, exchange_out=False):
    b2 = b.shape[-2:]
    if mode == "nn":
        (M, K), (_, N) = a.shape, b2
    elif mode == "nt":
        (M, K), (N, _) = a.shape, b2
    else:
        (K, M), (_, N) = a.shape, b2
    tm, tn, tk = _mm_tiles(mode, M, N, K, a.dtype.itemsize, b.dtype.itemsize, exchange_out)
    nk = K // tk
    dims = {"nn": NN, "nt": NT, "tn": TN}[mode]

    def body(a_ref, b_ref, o_ref, *scratch):
        part = _dot(a_ref[...].astype(BF16), b_ref[...].astype(BF16), dims)
        if nk == 1:
            o_ref[...] = part.astype(o_ref.dtype)
            return
        (acc_ref,) = scratch
        k = pl.program_id(2)

        @pl.when(k == 0)
        def _():
            acc_ref[...] = part

        @pl.when(k > 0)
        def _():
            acc_ref[...] += part

        @pl.when(k == nk - 1)
        def _():
            o_ref[...] = acc_ref[...].astype(o_ref.dtype)

    if mode == "tn":
        a_spec = pl.BlockSpec((tk, tm), lambda i, j, k: (k, i))
    else:
        a_spec = pl.BlockSpec((tm, tk), lambda i, j, k: (i, k))
    bblk = (tk, tn) if mode in ("nn", "tn") else (tn, tk)
    bidx = (lambda i, j, k: (k, j)) if mode in ("nn", "tn") else (lambda i, j, k: (j, k))
    if b_layer is None:
        b_spec = pl.BlockSpec(bblk, bidx)
    else:
        b_spec = pl.BlockSpec((None,) + bblk, lambda i, j, k: (b_layer,) + bidx(i, j, k))
    if exchange_out:
        mh, cw = M // 2, N // N_CHIPS
        assert mh % tm == 0 and cw % tn == 0
        out_shape = jax.ShapeDtypeStruct((N_CHIPS, 2, mh, cw), out_dtype)
        o_spec = pl.BlockSpec(
            (None, None, tm, tn),
            lambda i, j, k: (j // (cw // tn), i // (mh // tm), i % (mh // tm), j % (cw // tn)))
    else:
        out_shape = jax.ShapeDtypeStruct((M, N), out_dtype)
        o_spec = pl.BlockSpec((tm, tn), lambda i, j, k: (i, j))
    return pl.pallas_call(
        body, name=name, out_shape=out_shape, grid=(M // tm, N // tn, nk),
        in_specs=[a_spec, b_spec], out_specs=o_spec,
        scratch_shapes=[] if nk == 1 else [pltpu.VMEM((tm, tn), F32)],
        compiler_params=_params(("parallel", "parallel", "arbitrary")),
    )(a, b)


def _row_spec(tr, width):
    return pl.BlockSpec((tr, width), lambda i: (i, 0))


def _vec_spec(width, rows=1):
    return pl.BlockSpec((rows, width), lambda i: (0, 0))


def _norm_fwd(x, y, gate, g, sc, sh, name):
    T, Dm = x.shape
    tr = _pick(T, (ROW_TILE,))
    has_res = y is not None

    def body(*refs):
        if has_res:
            x_ref, y_ref, gate_ref, g_ref, sc_ref, sh_ref, xo_ref, h_ref = refs
            xv = x_ref[...] + gate_ref[...] * y_ref[...]
            xo_ref[...] = xv
        else:
            x_ref, g_ref, sc_ref, sh_ref, h_ref = refs
            xv = x_ref[...]
        rstd = lax.rsqrt(jnp.mean(xv * xv, axis=-1, keepdims=True) + EPS)
        h_ref[...] = ((xv * rstd * g_ref[...]) * (1.0 + sc_ref[...]) + sh_ref[...]).astype(BF16)

    row, vec = _row_spec(tr, Dm), _vec_spec(Dm)
    if has_res:
        ins, in_specs = (x, y, gate, g, sc, sh), [row, row, vec, vec, vec, vec]
        out_shape = (jax.ShapeDtypeStruct((T, Dm), F32), jax.ShapeDtypeStruct((T, Dm), BF16))
        out_specs = (row, row)
    else:
        ins, in_specs = (x, g, sc, sh), [row, vec, vec, vec]
        out_shape = jax.ShapeDtypeStruct((T, Dm), BF16)
        out_specs = row
    out = pl.pallas_call(body, name=name, out_shape=out_shape, grid=(T // tr,), in_specs=in_specs,
                         out_specs=out_specs, compiler_params=_params(("parallel",)))(*ins)
    return out if has_res else (x, out)


def _norm_bwd(x, dh, dxo, g, sc, y_prev, gate_prev, name):
    T, Dm = x.shape
    tr = _pick(T, (ROW_TILE,))
    has_prev = y_prev is not None

    def body(*refs):
        if has_prev:
            x_ref, dh_ref, dxo_ref, g_ref, sc_ref, yp_ref, gp_ref, dx_ref, dyp_ref, s1_ref, s2_ref, s3_ref = refs
        else:
            x_ref, dh_ref, dxo_ref, g_ref, sc_ref, dx_ref, s1_ref, s2_ref = refs

        @pl.when(pl.program_id(0) == 0)
        def _():
            s1_ref[...] = jnp.zeros_like(s1_ref)
            s2_ref[...] = jnp.zeros_like(s2_ref)
            if has_prev:
                s3_ref[...] = jnp.zeros_like(s3_ref)

        xv = x_ref[...]
        rstd = lax.rsqrt(jnp.mean(xv * xv, axis=-1, keepdims=True) + EPS)
        xhat = xv * rstd
        dh = dh_ref[...]
        dxhat = dh * (g_ref[...] * (1.0 + sc_ref[...]))
        dx = dxo_ref[...] + rstd * (dxhat - xhat * jnp.mean(dxhat * xhat, axis=-1, keepdims=True))
        dx_ref[...] = dx
        s1_ref[...] += jnp.sum(dh, axis=0, keepdims=True)
        s2_ref[...] += jnp.sum(dh * xhat, axis=0, keepdims=True)
        if has_prev:
            dyp_ref[...] = (gp_ref[...] * dx).astype(BF16)
            s3_ref[...] += jnp.sum(dx * yp_ref[...], axis=0, keepdims=True)

    row, vec = _row_spec(tr, Dm), _vec_spec(Dm)
    vshape = jax.ShapeDtypeStruct((1, Dm), F32)
    if has_prev:
        ins, in_specs = (x, dh, dxo, g, sc, y_prev, gate_prev), [row, row, row, vec, vec, row, vec]
        out_shape = (jax.ShapeDtypeStruct((T, Dm), F32), jax.ShapeDtypeStruct((T, Dm), BF16), vshape, vshape, vshape)
        out_specs = (row, row, vec, vec, vec)
    else:
        ins, in_specs = (x, dh, dxo, g, sc), [row, row, row, vec, vec]
        out_shape = (jax.ShapeDtypeStruct((T, Dm), F32), vshape, vshape)
        out_specs = (row, vec, vec)
    return pl.pallas_call(body, name=name, out_shape=out_shape, grid=(T // tr,), in_specs=in_specs,
                          out_specs=out_specs, compiler_params=_params(("arbitrary",)))(*ins)


def _loss_head(x, y, gate, fg, tgt, name):
    T, Dm = x.shape
    tr = _pick(T, (ROW_TILE,))
    nsteps = T // tr

    def body(x_ref, y_ref, gate_ref, fg_ref, t_ref, dx_ref, dy_ref, loss_ref, sfg_ref, sg_ref, acc_ref):
        i = pl.program_id(0)

        @pl.when(i == 0)
        def _():
            acc_ref[...] = jnp.zeros_like(acc_ref)
            sfg_ref[...] = jnp.zeros_like(sfg_ref)
            sg_ref[...] = jnp.zeros_like(sg_ref)

        yv = y_ref[...]
        xv = x_ref[...] + gate_ref[...] * yv
        rstd = lax.rsqrt(jnp.mean(xv * xv, axis=-1, keepdims=True) + EPS)
        xhat = xv * rstd
        err = xhat * fg_ref[...] - t_ref[...]
        acc_ref[...] += jnp.sum(err * err, axis=0, keepdims=True)
        dyn = err * (1.0 / Dm)
        sfg_ref[...] += jnp.sum(dyn * xhat, axis=0, keepdims=True)
        dxhat = dyn * fg_ref[...]
        dx = rstd * (dxhat - xhat * jnp.mean(dxhat * xhat, axis=-1, keepdims=True))
        dx_ref[...] = dx
        dy_ref[...] = (gate_ref[...] * dx).astype(BF16)
        sg_ref[...] += jnp.sum(dx * yv, axis=0, keepdims=True)

        @pl.when(i == nsteps - 1)
        def _():
            total = jnp.sum(acc_ref[...], axis=1, keepdims=True) * (0.5 / Dm)
            loss_ref[...] = jnp.broadcast_to(total, loss_ref.shape)

    row, vec = _row_spec(tr, Dm), _vec_spec(Dm)
    vshape = jax.ShapeDtypeStruct((1, Dm), F32)
    return pl.pallas_call(
        body, name=name, grid=(nsteps,),
        out_shape=(jax.ShapeDtypeStruct((T, Dm), F32), jax.ShapeDtypeStruct((T, Dm), BF16),
                   jax.ShapeDtypeStruct((1, LANES), F32), vshape, vshape),
        in_specs=[row, row, vec, vec, row], out_specs=(row, row, _vec_spec(LANES), vec, vec),
        scratch_shapes=[pltpu.VMEM((1, Dm), F32)], compiler_params=_params(("arbitrary",)),
    )(x, y, gate, fg, tgt)


def _spatial_mask():
    r = lax.broadcasted_iota(jnp.int32, (GM_BLOCK, GM_BLOCK), 0) // CHUNK
    c = lax.broadcasted_iota(jnp.int32, (GM_BLOCK, GM_BLOCK), 1) // CHUNK
    return r >= c


def _gm_specs(tr):
    return [_row_spec(tr, 2 * GM_WIDTH), _vec_spec(GM_WIDTH), _vec_spec(GM_WIDTH),
            pl.BlockSpec((GM_HEADS, GM_BLOCK, GM_BLOCK), lambda i: (0, 0, 0)),
            pl.BlockSpec((GM_HEADS, GM_BLOCK, 1), lambda i: (0, 0, 0))]


def _gm_mid_fwd(a, ln_g, ln_b, ws, bs3, name):
    T = a.shape[0]
    tr = _pick(T, (ROW_TILE,))
    W = GM_WIDTH

    def body(a_ref, lg_ref, lb_ref, ws_ref, bs_ref, o_ref, vn_scr):
        av = a_ref[:, W:]
        v = av * _cdf(av)
        vc = v - jnp.mean(v, axis=-1, keepdims=True)
        rstd = lax.rsqrt(jnp.mean(vc * vc, axis=-1, keepdims=True) + EPS)
        vn_scr[...] = (vc * rstd * lg_ref[...] + lb_ref[...]).astype(BF16)
        mask = _spatial_mask()
        for h in range(GM_HEADS):
            w = jnp.where(mask, ws_ref[h], 0.0).astype(BF16)
            cs = slice(h * GM_HEAD_DIM, (h + 1) * GM_HEAD_DIM)
            for blk in range(tr // GM_BLOCK):
                rs = slice(blk * GM_BLOCK, (blk + 1) * GM_BLOCK)
                s = _dot(w, vn_scr[rs, cs], NN) + bs_ref[h]
                au = a_ref[rs, cs]
                o_ref[rs, cs] = (au * _cdf(au) * s).astype(BF16)

    return pl.pallas_call(
        body, name=name, out_shape=jax.ShapeDtypeStruct((T, W), BF16), grid=(T // tr,),
        in_specs=_gm_specs(tr), out_specs=_row_spec(tr, W),
        scratch_shapes=[pltpu.VMEM((tr, W), BF16)], compiler_params=_params(("parallel",)),
    )(a, ln_g, ln_b, ws, bs3)


def _gm_mid_bwd(a, dgated, ln_g, ln_b, ws, bs3, name):
    T = a.shape[0]
    tr = _pick(T, (ROW_TILE,))
    W = GM_WIDTH
    nsteps = T // tr

    def body(a_ref, dg_ref, lg_ref, lb_ref, ws_ref, bs_ref, da_ref, dws_ref, dbs_ref, dlg_ref, dlb_ref,
             vn_scr, vhat_scr, dvn_scr, dsum_scr):
        i = pl.program_id(0)

        @pl.when(i == 0)
        def _():
            dws_ref[...] = jnp.zeros_like(dws_ref)
            dbs_ref[...] = jnp.zeros_like(dbs_ref)
            dlg_ref[...] = jnp.zeros_like(dlg_ref)
            dlb_ref[...] = jnp.zeros_like(dlb_ref)
            dsum_scr[...] = jnp.zeros_like(dsum_scr)

        av = a_ref[:, W:]
        cdf_v = _cdf(av)
        v = av * cdf_v
        vc = v - jnp.mean(v, axis=-1, keepdims=True)
        rstd = lax.rsqrt(jnp.mean(vc * vc, axis=-1, keepdims=True) + EPS)
        vhat_scr[...] = vc * rstd
        vn_scr[...] = (vhat_scr[...] * lg_ref[...] + lb_ref[...]).astype(BF16)
        mask = _spatial_mask()
        for h in range(GM_HEADS):
            w = jnp.where(mask, ws_ref[h], 0.0).astype(BF16)
            cs = slice(h * GM_HEAD_DIM, (h + 1) * GM_HEAD_DIM)
            for blk in range(tr // GM_BLOCK):
                rs = slice(blk * GM_BLOCK, (blk + 1) * GM_BLOCK)
                vnb = vn_scr[rs, cs]
                s = _dot(w, vnb, NN) + bs_ref[h]
                au = a_ref[rs, cs]
                cdf_u = _cdf(au)
                dg = dg_ref[rs, cs]
                ds = dg * (au * cdf_u)
                da_ref[rs, cs] = (dg * s * (cdf_u + au * _pdf(au))).astype(BF16)
                dsb = ds.astype(BF16)
                dvn_scr[rs, cs] = _dot(w, dsb, TN)
                dws_ref[h] += _dot(dsb, vnb, NT)
                dsum_scr[:, cs] += ds
        dvn = dvn_scr[...]
        vhat = vhat_scr[...]
        dlg_ref[...] += jnp.sum(dvn * vhat, axis=0, keepdims=True)
        dlb_ref[...] += jnp.sum(dvn, axis=0, keepdims=True)
        dvh = dvn * lg_ref[...]
        dv = rstd * (dvh - jnp.mean(dvh, axis=-1, keepdims=True)
                     - vhat * jnp.mean(dvh * vhat, axis=-1, keepdims=True))
        da_ref[:, W:] = (dv * (cdf_v + av * _pdf(av))).astype(BF16)

        @pl.when(i == nsteps - 1)
        def _():
            for h in range(GM_HEADS):
                dws_ref[h] = jnp.where(mask, dws_ref[h], 0.0)
            col_head = lax.broadcasted_iota(jnp.int32, (W, GM_BLOCK), 0) // GM_HEAD_DIM
            sel = (col_head == lax.broadcasted_iota(jnp.int32, (W, GM_BLOCK), 1)).astype(F32)
            dbs_ref[...] = _dot(dsum_scr[...], sel, NN, HI)

    vshape = jax.ShapeDtypeStruct((1, W), F32)
    return pl.pallas_call(
        body, name=name, grid=(nsteps,),
        out_shape=(jax.ShapeDtypeStruct((T, 2 * W), BF16), jax.ShapeDtypeStruct((GM_HEADS, GM_BLOCK, GM_BLOCK), F32),
                   jax.ShapeDtypeStruct((GM_BLOCK, GM_BLOCK), F32), vshape, vshape),
        in_specs=[_gm_specs(tr)[0], _row_spec(tr, W)] + _gm_specs(tr)[1:],
        out_specs=(_row_spec(tr, 2 * W), pl.BlockSpec((GM_HEADS, GM_BLOCK, GM_BLOCK), lambda i: (0, 0, 0)),
                   pl.BlockSpec((GM_BLOCK, GM_BLOCK), lambda i: (0, 0)), _vec_spec(W), _vec_spec(W)),
        scratch_shapes=[pltpu.VMEM((tr, W), BF16), pltpu.VMEM((tr, W), F32), pltpu.VMEM((tr, W), F32),
                        pltpu.VMEM((GM_BLOCK, W), F32)],
        compiler_params=_params(("arbitrary",)),
    )(a, dgated, ln_g, ln_b, ws, bs3)


SUB = 16
EXP_CLAMP = 80.0


def _tri(lower):
    r = lax.broadcasted_iota(jnp.int32, (CHUNK, CHUNK), 0)
    c = lax.broadcasted_iota(jnp.int32, (CHUNK, CHUNK), 1)
    return (r >= c) if lower else (c >= r)


def _score_masks():
    i = lax.broadcasted_iota(jnp.int32, (CHUNK, CHUNK), 0)
    j = lax.broadcasted_iota(jnp.int32, (CHUNK, CHUNK), 1)
    bi, bj = i // SUB, j // SUB
    diag = (bi == bj) & (i >= j)
    pair = (bi % 2 == 1) & (bj == bi - 1)
    half = (i >= CHUNK // 2) & (j < CHUNK // 2)
    return diag, pair, half


def _dot01(m, x):
    x1 = x.astype(BF16)
    rest = x - x1.astype(F32)
    x2 = rest.astype(BF16)
    x3 = (rest - x2.astype(F32)).astype(BF16)
    return _dot(m, x1, NN) + (_dot(m, x2, NN) + _dot(m, x3, NN))


def _block_rows(b, offset):
    parts = []
    for blk in range(0, CHUNK, SUB):
        r = blk + offset
        parts.append(jnp.zeros((SUB, b.shape[1]), F32) if r < 0 else jnp.broadcast_to(b[r:r + 1], (SUB, b.shape[1])))
    return jnp.concatenate(parts, axis=0)


def _hg_gates(p_ref, lb_ref, lower):
    Dm = D_MODEL
    heads = []
    for h in range(HG_HEADS):
        c0 = h * HG_DIM
        qr = p_ref[:, c0:c0 + HG_DIM]
        fz = p_ref[:, Dm + c0:Dm + c0 + HG_DIM]
        lbh = lb_ref[:, c0:c0 + HG_DIM]
        sg = _sig(fz)
        f = lbh + (1.0 - lbh) * sg
        sq = _sig(qr)
        heads.append(dict(qr=qr, v=p_ref[:, 2 * Dm + c0:2 * Dm + c0 + HG_DIM],
                          gt=p_ref[:, 3 * Dm + c0:3 * Dm + c0 + HG_DIM], lbh=lbh, sg=sg, f=f, gl=jnp.log(f),
                          kk=1.0 - f, sq=sq, q=qr * sq))
    for g in heads:
        g["b"] = _dot01(lower, g.pop("gl"))
    for g in heads:
        g.update(_hg_scalings(g["q"], g["kk"], g.pop("b")))
    return heads


def _hg_scalings(q, kk, b):
    r_mid = _block_rows(b, SUB // 2 - 1)
    r_prev = _block_rows(b, -1)
    r_end = _block_rows(b, SUB - 1)
    r_half = jnp.broadcast_to(b[CHUNK // 2 - 1:CHUNK // 2], b.shape)
    bc = b[CHUNK - 1:CHUNK]
    eqs = (jnp.exp(jnp.clip(b - r_mid, -EXP_CLAMP, EXP_CLAMP)), jnp.exp(jnp.minimum(b - r_prev, 0.0)),
           jnp.exp(jnp.minimum(b - r_half, 0.0)))
    eks = (jnp.exp(jnp.clip(r_mid - b, -EXP_CLAMP, EXP_CLAMP)), jnp.exp(jnp.minimum(r_end - b, 0.0)),
           jnp.exp(jnp.minimum(r_half - b, 0.0)))
    eb = jnp.exp(b)
    ec = jnp.exp(bc - b)
    return dict(eqs=eqs, eks=eks, eb=eb, ec=ec, e_end=jnp.exp(bc), qs=[q * e for e in eqs],
                ks=[kk * e for e in eks], qe=q * eb, ke=kk * ec)


def _scores(g, masks):
    a = None
    for qs, ks, m in zip(g["qs"], g["ks"], masks):
        part = jnp.where(m, _dot(qs.astype(BF16), ks.astype(BF16), NT), 0.0)
        a = part if a is None else a + part
    return a


def _hg_scan_fwd(p, lb, gn, name):
    T = p.shape[0]
    nc = T // CHUNK
    Dm = D_MODEL

    def body(p_ref, lb_ref, gn_ref, o_ref, og_ref, so_ref, st_ref):
        @pl.when(pl.program_id(0) == 0)
        def _():
            st_ref[...] = jnp.zeros_like(st_ref)

        masks = _score_masks()
        heads = _hg_gates(p_ref, lb_ref, _tri(True).astype(BF16))
        states = [st_ref[h] for h in range(HG_HEADS)]
        scores = [_scores(g, masks) for g in heads]
        outs = [_dot(a.astype(BF16), g["v"].astype(BF16), NN) + _dot(g["qe"], st, NT, X3)
                for g, a, st in zip(heads, scores, states)]
        new_states = [st * g["e_end"] + _dot(g["v"], g["ke"], TN, X3) for g, st in zip(heads, states)]
        for h, (g, o, st, st2) in enumerate(zip(heads, outs, states, new_states)):
            cs = slice(h * HG_DIM, (h + 1) * HG_DIM)
            so_ref[0, h] = st
            st_ref[h] = st2
            o_ref[:, cs] = o
            r = lax.rsqrt(jnp.mean(o * o, axis=-1, keepdims=True) + EPS)
            gt = g["gt"]
            og_ref[:, cs] = (((o * r) * gn_ref[:, cs]).astype(F32) * (gt * _sig(gt))).astype(BF16)

    return pl.pallas_call(
        body, name=name, grid=(nc,),
        out_shape=(jax.ShapeDtypeStruct((T, Dm), F32), jax.ShapeDtypeStruct((T, Dm), BF16),
                   jax.ShapeDtypeStruct((nc, HG_HEADS, HG_DIM, HG_DIM), F32)),
        in_specs=[_row_spec(CHUNK, 4 * Dm), _vec_spec(Dm), _vec_spec(Dm)],
        out_specs=(_row_spec(CHUNK, Dm), _row_spec(CHUNK, Dm),
                   pl.BlockSpec((1, HG_HEADS, HG_DIM, HG_DIM), lambda i: (i, 0, 0, 0))),
        scratch_shapes=[pltpu.VMEM((HG_HEADS, HG_DIM, HG_DIM), F32)],
        compiler_params=_params(("arbitrary",)),
    )(p, lb, gn)


def _hg_scan_bwd(p, lb, gn, o, dog, states, name):
    T = p.shape[0]
    nc = T // CHUNK
    Dm = D_MODEL

    def rev(i):
        return nc - 1 - i

    def body(p_ref, lb_ref, gn_ref, o_ref, dog_ref, st_in_ref, dp_ref, dlb_ref, dgn_ref, dst_ref, carry_ref):
        @pl.when(pl.program_id(0) == 0)
        def _():
            dst_ref[...] = jnp.zeros_like(dst_ref)
            carry_ref[...] = jnp.zeros_like(carry_ref)
            dlb_ref[...] = jnp.zeros_like(dlb_ref)
            dgn_ref[...] = jnp.zeros_like(dgn_ref)

        upper = _tri(False).astype(BF16)
        masks = _score_masks()
        heads = _hg_gates(p_ref, lb_ref, _tri(True).astype(BF16))
        for h, g in enumerate(heads):
            cs = slice(h * HG_DIM, (h + 1) * HG_DIM)
            oh = o_ref[:, cs]
            r = lax.rsqrt(jnp.mean(oh * oh, axis=-1, keepdims=True) + EPS)
            on = oh * r
            gt = g["gt"]
            sgt = _sig(gt)
            sil = gt * sgt
            dogh = dog_ref[:, cs]
            gnh = gn_ref[:, cs]
            don = dogh * gnh * sil
            g["dgn"] = jnp.sum(dogh * on * sil, axis=0, keepdims=True)
            g["dgate"] = dogh * on * gnh * (sgt * (1.0 + gt * (1.0 - sgt)))
            g["do"] = r * (don - on * jnp.mean(don * on, axis=-1, keepdims=True))
            g["dst"] = dst_ref[h]
            g["st"] = st_in_ref[0, h]
            g["carry"] = carry_ref[h]
        for g in heads:
            g["a"] = _scores(g, masks)
            g["dob"] = g["do"].astype(BF16)
            g["da"] = _dot(g["dob"], g["v"].astype(BF16), NT)
        for g in heads:
            g["dv"] = _dot(g["a"].astype(BF16), g["dob"], TN) + _dot(g["ke"].astype(BF16), g["dst"].astype(BF16), NT)
            g["dq"] = _dot(g["do"], g["st"], NN, X3) * g["eb"]
            g["dk"] = _dot(g["v"], g["dst"], NN, X3) * g["ec"]
            g["dst2"] = g["dst"] * g["e_end"] + _dot(g["do"], g["qe"], TN, X3)
        for lvl in range(3):
            for g in heads:
                dam = jnp.where(masks[lvl], g["da"], 0.0)
                g["dq"] = g["dq"] + _dot(dam, g["ks"][lvl], NN, X3) * g["eqs"][lvl]
                g["dk"] = g["dk"] + _dot(dam, g["qs"][lvl], TN, X3) * g["eks"][lvl]
        for g in heads:
            g["dgd"] = g["q"] * g["dq"] - g["kk"] * g["dk"]
            g["dgl"] = _dot01(upper, g["dgd"]) + g["carry"]
        for h, g in enumerate(heads):
            c0 = h * HG_DIM
            cs = slice(c0, c0 + HG_DIM)
            df = g["dgl"] / g["f"] - g["dk"]
            sg, sq, qr = g["sg"], g["sq"], g["qr"]
            dst_ref[h] = g["dst2"]
            carry_ref[h] = g["carry"] + jnp.sum(g["dgd"], axis=0, keepdims=True)
            dgn_ref[:, cs] += g["dgn"]
            dlb_ref[:, cs] += jnp.sum(df * (1.0 - sg), axis=0, keepdims=True)
            dp_ref[:, c0:c0 + HG_DIM] = (g["dq"] * (sq * (1.0 + qr * (1.0 - sq)))).astype(BF16)
            dp_ref[:, Dm + c0:Dm + c0 + HG_DIM] = (df * (1.0 - g["lbh"]) * sg * (1.0 - sg)).astype(BF16)
            dp_ref[:, 2 * Dm + c0:2 * Dm + c0 + HG_DIM] = g["dv"].astype(BF16)
            dp_ref[:, 3 * Dm + c0:3 * Dm + c0 + HG_DIM] = g["dgate"].astype(BF16)

    vshape = jax.ShapeDtypeStruct((1, Dm), F32)
    rrow = lambda w: pl.BlockSpec((CHUNK, w), lambda i: (rev(i), 0))
    return pl.pallas_call(
        body, name=name, grid=(nc,),
        out_shape=(jax.ShapeDtypeStruct((T, 4 * Dm), BF16), vshape, vshape),
        in_specs=[rrow(4 * Dm), _vec_spec(Dm), _vec_spec(Dm), rrow(Dm), rrow(Dm),
                  pl.BlockSpec((1, HG_HEADS, HG_DIM, HG_DIM), lambda i: (rev(i), 0, 0, 0))],
        out_specs=(rrow(4 * Dm), _vec_spec(Dm), _vec_spec(Dm)),
        scratch_shapes=[pltpu.VMEM((HG_HEADS, HG_DIM, HG_DIM), F32), pltpu.VMEM((HG_HEADS, 1, HG_DIM), F32)],
        compiler_params=_params(("arbitrary",)),
    )(p, lb, gn, o, dog, states)


def _lb_fwd(hg_lb, name):
    def body(a_ref, o_ref):
        a0, a1 = a_ref[0:1], a_ref[1:2]
        m = jnp.maximum(a0, a1)
        e0, e1 = jnp.exp(a0 - m), jnp.exp(a1 - m)
        p0, p1 = e0 / (e0 + e1), e1 / (e0 + e1)
        o_ref[0:1] = p0 - p0
        o_ref[1:2] = (p0 + p1) - p0

    return pl.pallas_call(body, name=name, out_shape=jax.ShapeDtypeStruct(hg_lb.shape, F32))(hg_lb)


def _lb_bwd(hg_lb, dlb_all, name):
    def body(a_ref, d_ref, o_ref):
        a0, a1 = a_ref[0:1], a_ref[1:2]
        m = jnp.maximum(a0, a1)
        e0, e1 = jnp.exp(a0 - m), jnp.exp(a1 - m)
        p0, p1 = e0 / (e0 + e1), e1 / (e0 + e1)
        d1 = d_ref[1:2]
        o_ref[0:1] = -p0 * p1 * d1
        o_ref[1:2] = p1 * (1.0 - p1) * d1

    return pl.pallas_call(body, name=name, out_shape=jax.ShapeDtypeStruct(hg_lb.shape, F32))(hg_lb, dlb_all)


CONV_COLS = 256


def _conv_fwd(a, w, b, name):
    T = a.shape[0]
    Fh = FFN_HIDDEN
    tr = _pick(T, (ROW_TILE,))
    cw = CONV_COLS
    hb = tr // 8

    def body(a_ref, ap_ref, w_ref, b_ref, m_ref):
        m0 = (pl.program_id(0) > 0).astype(F32)

        def conv(cc):
            x = jnp.concatenate([ap_ref[:, pl.ds(cc, cw)] * m0, a_ref[:, pl.ds(cc, cw)]], axis=0)
            wv = w_ref[:, pl.ds(cc, cw)]
            y = b_ref[:, pl.ds(cc, cw)] + wv[2:3] * x + wv[1:2] * pltpu.roll(x, 1, axis=0) \
                + wv[0:1] * pltpu.roll(x, 2, axis=0)
            return y[8:]

        def step(c, carry):
            c0 = pl.multiple_of(c * cw, cw)
            c1 = pl.multiple_of(Fh + c * cw, cw)
            yg, yv = conv(c0), conv(c1)
            m_ref[:, pl.ds(c0, cw)] = (yg * _cdf(yg) * yv).astype(BF16)
            return carry

        lax.fori_loop(0, Fh // cw, step, 0)

    return pl.pallas_call(
        body, name=name, out_shape=jax.ShapeDtypeStruct((T, Fh), BF16), grid=(T // tr,),
        in_specs=[_row_spec(tr, 2 * Fh), pl.BlockSpec((8, 2 * Fh), lambda i: (jnp.maximum(i * hb - 1, 0), 0)),
                  _vec_spec(2 * Fh, 3), _vec_spec(2 * Fh)],
        out_specs=_row_spec(tr, Fh), compiler_params=_params(("parallel",)),
    )(a, a, w, b)


def _conv_bwd(a, dm, w, b, name):
    T = a.shape[0]
    Fh = FFN_HIDDEN
    tr = _pick(T, (ROW_TILE,))
    cw = CONV_COLS
    hb = tr // 8
    nsteps = T // tr
    n = tr + 8

    def body(a_ref, ap_ref, an_ref, dm_ref, dmn_ref, w_ref, b_ref, da_ref, dw_ref, db_ref):
        i = pl.program_id(0)
        m0 = (i > 0).astype(F32)
        m1 = (i < nsteps - 1).astype(F32)

        @pl.when(i == 0)
        def _():
            dw_ref[...] = jnp.zeros_like(dw_ref)
            db_ref[...] = jnp.zeros_like(db_ref)

        def prep(cc):
            x = jnp.concatenate([ap_ref[:, pl.ds(cc, cw)] * m0, a_ref[:, pl.ds(cc, cw)],
                                 an_ref[:, pl.ds(cc, cw)] * m1], axis=0)
            wv = w_ref[:, pl.ds(cc, cw)]
            s1 = pltpu.roll(x, 1, axis=0)
            s2 = pltpu.roll(x, 2, axis=0)
            y = b_ref[:, pl.ds(cc, cw)] + wv[2:3] * x + wv[1:2] * s1 + wv[0:1] * s2
            return wv, x[8:], s1[8:], s2[8:], y[8:]

        def back(cc, dy, wv, x0, s1, s2):
            da = wv[2:3] * dy + wv[1:2] * pltpu.roll(dy, n - 1, axis=0) + wv[0:1] * pltpu.roll(dy, n - 2, axis=0)
            da_ref[:, pl.ds(cc, cw)] = da[:tr].astype(BF16)
            d = dy[:tr]
            db_ref[:, pl.ds(cc, cw)] += jnp.sum(d, axis=0, keepdims=True)
            dw_ref[2:3, pl.ds(cc, cw)] += jnp.sum(d * x0[:tr], axis=0, keepdims=True)
            dw_ref[1:2, pl.ds(cc, cw)] += jnp.sum(d * s1[:tr], axis=0, keepdims=True)
            dw_ref[0:1, pl.ds(cc, cw)] += jnp.sum(d * s2[:tr], axis=0, keepdims=True)

        def step(c, carry):
            c0 = pl.multiple_of(c * cw, cw)
            c1 = pl.multiple_of(Fh + c * cw, cw)
            dmx = jnp.concatenate([dm_ref[:, pl.ds(c0, cw)], dmn_ref[:, pl.ds(c0, cw)] * m1], axis=0)
            wg, xg, s1g, s2g, yg = prep(c0)
            wv, xv, s1v, s2v, yv = prep(c1)
            cg = _cdf(yg)
            back(c0, dmx * yv * (cg + yg * _pdf(yg)), wg, xg, s1g, s2g)
            back(c1, dmx * (yg * cg), wv, xv, s1v, s2v)
            return carry

        lax.fori_loop(0, Fh // cw, step, 0)

    prev = lambda wd: pl.BlockSpec((8, wd), lambda i: (jnp.maximum(i * hb - 1, 0), 0))
    nxt = lambda wd: pl.BlockSpec((8, wd), lambda i: (jnp.minimum((i + 1) * hb, T // 8 - 1), 0))
    return pl.pallas_call(
        body, name=name, grid=(nsteps,),
        out_shape=(jax.ShapeDtypeStruct((T, 2 * Fh), BF16), jax.ShapeDtypeStruct((3, 2 * Fh), F32),
                   jax.ShapeDtypeStruct((1, 2 * Fh), F32)),
        in_specs=[_row_spec(tr, 2 * Fh), prev(2 * Fh), nxt(2 * Fh), _row_spec(tr, Fh), nxt(Fh),
                  _vec_spec(2 * Fh, 3), _vec_spec(2 * Fh)],
        out_specs=(_row_spec(tr, 2 * Fh), _vec_spec(2 * Fh, 3), _vec_spec(2 * Fh)),
        compiler_params=_params(("arbitrary",)),
    )(a, a, a, dm, dm, w, b)


def _ada_fwd(c_all, ada_w, ada_b, name):
    L, Dm, cols = ada_w.shape
    tn = _pick(cols, (512, 256, 128))

    def body(c_ref, w_ref, b_ref, o_ref):
        cv = c_ref[...]
        cond = (cv * _sig(cv)).astype(BF16)
        o_ref[...] = _dot(cond, w_ref[...].astype(BF16), NN) + b_ref[...]

    return pl.pallas_call(
        body, name=name, out_shape=jax.ShapeDtypeStruct((L, N_DEV, cols), F32), grid=(L, cols // tn),
        in_specs=[pl.BlockSpec((N_DEV, Dm), lambda l, j: (0, 0)), pl.BlockSpec((None, Dm, tn), lambda l, j: (l, 0, j)),
                  pl.BlockSpec((None, 1, tn), lambda l, j: (l, 0, j))],
        out_specs=pl.BlockSpec((None, N_DEV, tn), lambda l, j: (l, 0, j)),
        compiler_params=_params(("parallel", "parallel")),
    )(c_all, ada_w, ada_b.reshape(L, 1, cols))


def _ada_bwd(c_all, dmod, name):
    L, _, cols = dmod.shape
    Dm = c_all.shape[1]
    tn = _pick(cols, (512, 256, 128))

    def body(c_ref, d_ref, o_ref):
        cv = c_ref[...]
        o_ref[...] = _dot(cv * _sig(cv), d_ref[...], TN, HI)

    return pl.pallas_call(
        body, name=name, out_shape=jax.ShapeDtypeStruct((L, Dm, cols), F32), grid=(L, cols // tn),
        in_specs=[pl.BlockSpec((N_DEV, Dm), lambda l, j: (0, 0)), pl.BlockSpec((None, N_DEV, tn), lambda l, j: (l, 0, j))],
        out_specs=pl.BlockSpec((None, Dm, tn), lambda l, j: (l, 0, j)),
        compiler_params=_params(("parallel", "parallel")),
    )(c_all, dmod)


def _add_own_half(g4, rb, core, name):
    S, _, rh, cw = g4.shape
    tr = _pick(rh, (256, 128, 176, 64))

    def body(core_ref, g_ref, r_ref, o_ref):
        o_ref[...] = (g_ref[...] + r_ref[...].astype(F32)).astype(BF16)

    return pl.pallas_call(
        body, name=name, out_shape=jax.ShapeDtypeStruct((S, rh, cw), BF16),
        grid_spec=pltpu.PrefetchScalarGridSpec(
            num_scalar_prefetch=1, grid=(S, rh // tr),
            in_specs=[pl.BlockSpec((None, None, tr, cw), lambda s, i, core_ref: (s, core_ref[0], i, 0)),
                      pl.BlockSpec((None, tr, cw), lambda s, i, core_ref: (s, i, 0))],
            out_specs=pl.BlockSpec((None, tr, cw), lambda s, i, core_ref: (s, i, 0))),
        compiler_params=_params(("parallel", "parallel")),
    )(core, g4, rb)


def _sum_chips(lands, sums, chip, name):
    L = len(lands)
    _, rh, cw = lands[0].shape
    tr = _pick(rh, (256, 128, 176, 64))

    def body(chip_ref, *refs):
        ld, cs, o_ref = refs[:L], refs[L:2 * L], refs[2 * L]
        me = chip_ref[0]
        for k in range(L):
            @pl.when(pl.program_id(0) == k)
            def _(k=k):
                own = cs[k][...].astype(F32)
                got = [ld[k][j].astype(F32) for j in range(3)]
                acc = None
                for t in range(N_CHIPS):
                    d = jnp.bitwise_xor(jnp.int32(t), me)
                    term = jnp.where(d == 0, own, jnp.where(d == 2, got[0], jnp.where(d == 1, got[1], got[2])))
                    acc = term if acc is None else acc + term
                o_ref[...] = acc

    frozen = lambda l, i, k: jnp.where(l == k, i, 0)
    in_specs = [pl.BlockSpec((3, tr, cw), lambda l, i, chip_ref, k=k: (0, frozen(l, i, k), 0)) for k in range(L)]
    in_specs += [pl.BlockSpec((None, tr, cw), lambda l, i, chip_ref, k=k: (chip_ref[0], frozen(l, i, k), 0))
                 for k in range(L)]
    return pl.pallas_call(
        body, name=name, out_shape=jax.ShapeDtypeStruct((L, rh, cw), F32),
        grid_spec=pltpu.PrefetchScalarGridSpec(
            num_scalar_prefetch=1, grid=(L, rh // tr), in_specs=in_specs,
            out_specs=pl.BlockSpec((None, tr, cw), lambda l, i, chip_ref: (l, i, 0))),
        compiler_params=_params(("arbitrary", "arbitrary")),
    )(chip, *lands, *sums)


def _sum_devices(gathered, name):
    n, R, _ = gathered.shape
    tr = _pick(R, (512, 256, 128, 64, 32, 16, 8))

    def body(g_ref, o_ref):
        acc = g_ref[0]
        for d in range(1, n):
            acc = acc + g_ref[d]
        o_ref[...] = acc

    return pl.pallas_call(
        body, name=name, out_shape=jax.ShapeDtypeStruct((R, LANES), F32), grid=(R // tr,),
        in_specs=[pl.BlockSpec((n, tr, LANES), lambda i: (0, i, 0))], out_specs=pl.BlockSpec((tr, LANES), lambda i: (i, 0)),
        compiler_params=_params(("parallel",)),
    )(gathered)


def _adamw(w, g, m, v, name):
    R, C = w.shape
    tr = _pick(R, (256, 128, 64, 32, 16, 8))
    c1 = 1.0 / (1.0 - ADAM_B1 ** ADAM_STEP)
    c2 = 1.0 / (1.0 - ADAM_B2 ** ADAM_STEP)

    def body(w_ref, g_ref, m_ref, v_ref, d_ref, mo_ref, vo_ref):
        gv = g_ref[...]
        m2 = ADAM_B1 * m_ref[...] + (1.0 - ADAM_B1) * gv
        v2 = ADAM_B2 * v_ref[...] + (1.0 - ADAM_B2) * (gv * gv)
        mo_ref[...] = m2
        vo_ref[...] = v2
        d_ref[...] = -ADAM_LR * ((m2 * c1) / (jnp.sqrt(v2 * c2) + ADAM_EPS) + ADAM_WD * w_ref[...])

    spec = pl.BlockSpec((tr, C), lambda i: (i, 0))
    shp = jax.ShapeDtypeStruct((R, C), F32)
    return pl.pallas_call(body, name=name, out_shape=(shp, shp, shp), grid=(R // tr,), in_specs=[spec] * 4,
                          out_specs=(spec, spec, spec), compiler_params=_params(("parallel",)))(w, g, m, v)


def _adamw_halves(w, own, recv, m, v, core, name):
    L, rh, cw = own.shape
    tr = _pick(rh, (256, 128, 176, 64))
    c1 = 1.0 / (1.0 - ADAM_B1 ** ADAM_STEP)
    c2 = 1.0 / (1.0 - ADAM_B2 ** ADAM_STEP)

    def body(core_ref, w_ref, own_ref, recv_ref, m_ref, v_ref, g_ref, d_ref, mo_ref, vo_ref):
        gv = jnp.where(pl.program_id(1) == core_ref[0], own_ref[...], recv_ref[...])
        g_ref[...] = gv
        m2 = ADAM_B1 * m_ref[...] + (1.0 - ADAM_B1) * gv
        v2 = ADAM_B2 * v_ref[...] + (1.0 - ADAM_B2) * (gv * gv)
        mo_ref[...] = m2
        vo_ref[...] = v2
        d_ref[...] = -ADAM_LR * ((m2 * c1) / (jnp.sqrt(v2 * c2) + ADAM_EPS) + ADAM_WD * w_ref[...])

    full = pl.BlockSpec((None, None, tr, cw), lambda l, hf, i, core_ref: (l, hf, i, 0))
    mine = pl.BlockSpec((None, tr, cw), lambda l, hf, i, core_ref: (l, jnp.where(hf == core_ref[0], i, 0), 0))
    other = pl.BlockSpec((None, tr, cw), lambda l, hf, i, core_ref: (l, jnp.where(hf == core_ref[0], 0, i), 0))
    shp = jax.ShapeDtypeStruct((L, 2, rh, cw), F32)
    view = lambda a: a.reshape(L, 2, rh, cw)
    outs = pl.pallas_call(
        body, name=name, out_shape=(shp, shp, shp, shp),
        grid_spec=pltpu.PrefetchScalarGridSpec(
            num_scalar_prefetch=1, grid=(L, 2, rh // tr), in_specs=[full, mine, other, full, full],
            out_specs=(full, full, full, full)),
        compiler_params=_params(("arbitrary", "arbitrary", "arbitrary")),
    )(core, view(w), own, recv, view(m), view(v))
    return tuple(o.reshape(L, 2 * rh, cw) for o in outs)


ANY = pl.BlockSpec(memory_space=pl.ANY)


def _position():
    x, y, c = lax.axis_index("x"), lax.axis_index("y"), lax.axis_index("c")
    return x, y, c


def _allgather(ins, out_shapes, src_fns, dst_fns, name, in_vmem):
    n = len(ins)

    def body(*refs):
        in_refs, out_refs = refs[:n], refs[n:2 * n]
        send_sems, recv_sems, local_sems = refs[2 * n:]
        x, y, c = _position()
        me, sibling = (x, y, c), (x, y, 1 - c)
        chips = [(1 - x, y), (x, 1 - y), (1 - x, 1 - y)]

        def copy(k, j, block, to, own=False):
            dst = dst_fns[k](out_refs[k], *block)
            return pltpu.make_async_remote_copy(
                src_ref=src_fns[k](in_refs[k], c) if own else dst, dst_ref=dst,
                send_sem=send_sems.at[k, j], recv_sem=recv_sems.at[k, j], device_id=to, device_id_type=MESH)

        mine = [pltpu.make_async_copy(src_fns[k](in_refs[k], c), dst_fns[k](out_refs[k], *me), local_sems.at[k])
                for k in range(n)]
        for cp in mine:
            cp.start()
        first = []
        for k in range(n):
            first.append(copy(k, 0, me, sibling, own=True))
            first += [copy(k, 1 + j, me, (*chip, c), own=True) for j, chip in enumerate(chips)]
        for cp in first:
            cp.start()
        passed = []
        for j, chip in enumerate(chips):
            for k in range(n):
                copy(k, 1 + j, (*chip, c), me).wait_recv()
                fwd = copy(k, 4 + j, (*chip, c), sibling)
                fwd.start()
                passed.append(fwd)
        for k in range(n):
            copy(k, 0, sibling, me).wait_recv()
        for j, chip in enumerate(chips):
            for k in range(n):
                copy(k, 4 + j, (*chip, 1 - c), me).wait_recv()
        for cp in first + passed:
            cp.wait_send()
        for cp in mine:
            cp.wait()

    spec = pl.BlockSpec(memory_space=pltpu.VMEM) if in_vmem else ANY
    return pl.pallas_call(
        body, name=name, out_shape=tuple(out_shapes), in_specs=[spec] * n, out_specs=tuple([spec] * n),
        scratch_shapes=[pltpu.SemaphoreType.DMA((n, 7)), pltpu.SemaphoreType.DMA((n, 7)),
                        pltpu.SemaphoreType.DMA((n,))],
        compiler_params=pltpu.CompilerParams(vmem_limit_bytes=VMEM_LIMIT_BYTES),
    )(*ins)


def _allgather_small(payload, name):
    R = payload.shape[0]
    (out,) = _allgather(
        [payload], [jax.ShapeDtypeStruct((N_DEV, R, LANES), F32)],
        [lambda ref, c: ref], [lambda ref, px, py, pc: ref.at[4 * px + 2 * py + pc]], name, in_vmem=True)
    return out


def _swap_sibling(ins, name):
    n = len(ins)

    def body(*refs):
        in_refs, out_refs = refs[:n], refs[n:2 * n]
        send_sems, recv_sems = refs[2 * n:]
        x, y, c = _position()
        copies = [pltpu.make_async_remote_copy(
            src_ref=in_refs[k], dst_ref=out_refs[k], send_sem=send_sems.at[k], recv_sem=recv_sems.at[k],
            device_id=(x, y, 1 - c), device_id_type=MESH) for k in range(n)]
        for cp in copies:
            cp.start()
        for cp in copies:
            cp.wait_recv()
        for cp in copies:
            cp.wait_send()

    return pl.pallas_call(
        body, name=name, out_shape=tuple(jax.ShapeDtypeStruct(a.shape, a.dtype) for a in ins),
        in_specs=[ANY] * n, out_specs=tuple([ANY] * n),
        scratch_shapes=[pltpu.SemaphoreType.DMA((n,)), pltpu.SemaphoreType.DMA((n,))],
        compiler_params=pltpu.CompilerParams(vmem_limit_bytes=VMEM_LIMIT_BYTES),
    )(*ins)


HBM_SPEC = pl.BlockSpec(memory_space=pltpu.HBM)
SEM_SPEC = pl.BlockSpec(memory_space=pltpu.SEMAPHORE)
SPLIT_PARAMS = pltpu.CompilerParams(has_side_effects=pltpu.SideEffectType.DATAFLOW_SIDE_EFFECTING)
TOKEN = jax.ShapeDtypeStruct((8, LANES), F32)


def _hbm(a):
    return pltpu.with_memory_space_constraint(a, pltpu.HBM)


def _weight_window(ref, col, r, cw, px, py, pc):
    rh = r // 2
    if col:
        return ref.at[pl.ds(pc * rh, rh), pl.ds((2 * px + py) * cw, cw)]
    return ref.at[pl.ds((2 * px + py) * r + pc * rh, rh), :]


def _peers(x, y, c):
    return [(x, y, 1 - c), (1 - x, y, c), (x, 1 - y, c), (1 - x, 1 - y, c)]


def _place_own(shard, col, pos, name):
    r, cw = shard.shape
    rh = r // 2
    tr = _pick(rh, (256, 128, 176, 64))
    nb = rh // tr
    shape = (r, N_CHIPS * cw) if col else (N_CHIPS * r, cw)

    def body(pos_ref, x_ref, o_ref):
        o_ref[...] = x_ref[...]

    if col:
        out_idx = lambda i, pos_ref: (pos_ref[1] * nb + i, pos_ref[0])
    else:
        out_idx = lambda i, pos_ref: (pos_ref[0] * (2 * nb) + pos_ref[1] * nb + i, 0)
    return pl.pallas_call(
        body, name=name, out_shape=jax.ShapeDtypeStruct(shape, shard.dtype),
        grid_spec=pltpu.PrefetchScalarGridSpec(
            num_scalar_prefetch=1, grid=(nb,),
            in_specs=[pl.BlockSpec((tr, cw), lambda i, pos_ref: (pos_ref[1] * nb + i, 0))],
            out_specs=pl.BlockSpec((tr, cw), out_idx)),
        compiler_params=_params(("arbitrary",)),
    )(pos, shard)


def _gather_start(shards, lands, cols, per_layer, name):
    n = len(shards)
    nl = n // per_layer

    def body(*refs):
        sh, ld = refs[:n], refs[n:2 * n]
        sems, token = refs[2 * n:2 * n + 2 * nl], refs[-1]
        x, y, c = _position()
        for k in range(n):
            l, a = divmod(k, per_layer)
            r, cw = shards[k].shape
            src = sh[k].at[pl.ds(c * (r // 2), r // 2), :]
            dst = _weight_window(ld[k], cols[k], r, cw, x, y, c)
            for j, peer in enumerate(_peers(x, y, c)):
                pltpu.make_async_remote_copy(src_ref=src, dst_ref=dst, send_sem=sems[2 * l].at[4 * a + j],
                                             recv_sem=sems[2 * l + 1].at[4 * a + j], device_id=peer,
                                             device_id_type=MESH).start()
        token[...] = jnp.zeros_like(token)

    arrs = list(shards) + list(lands)
    out = pl.pallas_call(
        body, name=name,
        out_shape=tuple(pltpu.SemaphoreType.DMA((per_layer * 4,)) for _ in range(2 * nl))
        + tuple(pltpu.HBM(a.shape, a.dtype) for a in arrs) + (TOKEN,),
        in_specs=[HBM_SPEC] * (2 * n),
        out_specs=(SEM_SPEC,) * (2 * nl) + (HBM_SPEC,) * (2 * n) + (pl.BlockSpec(memory_space=pltpu.VMEM),),
        input_output_aliases={i: 2 * nl + i for i in range(2 * n)}, compiler_params=SPLIT_PARAMS,
    )(*[_hbm(a) for a in arrs])
    return out[:2 * nl], out[2 * nl:2 * nl + n], out[2 * nl + n:2 * nl + 2 * n], out[-1]


def _gather_wait(shards, lands, send, recv, after, cols, name):
    m = len(shards)

    def body(*refs):
        sh, ld = refs[:m], refs[m:2 * m]
        send_ref, recv_ref = refs[2 * m], refs[2 * m + 1]
        x, y, c = _position()
        for a in range(m):
            r, cw = shards[a].shape
            src = sh[a].at[pl.ds(c * (r // 2), r // 2), :]
            for j, (px, py, pc) in enumerate(_peers(x, y, c)):
                cp = pltpu.make_async_remote_copy(
                    src_ref=src, dst_ref=_weight_window(ld[a], cols[a], r, cw, px, py, pc),
                    send_sem=send_ref.at[4 * a + j], recv_sem=recv_ref.at[4 * a + j], device_id=(px, py, pc),
                    device_id_type=MESH)
                cp.wait_send()
                cp.wait_recv()

    arrs = list(shards) + list(lands)
    out = pl.pallas_call(
        body, name=name, out_shape=tuple(pltpu.HBM(a.shape, a.dtype) for a in arrs),
        in_specs=[HBM_SPEC] * (2 * m) + [SEM_SPEC, SEM_SPEC, ANY], out_specs=(HBM_SPEC,) * (2 * m),
        input_output_aliases={i: i for i in range(2 * m)}, compiler_params=SPLIT_PARAMS,
    )(*arrs, send, recv, after)
    return out[m:]


def _forward_sibling(lands, cols, shard_shapes, name):
    m = len(lands)

    def body(*refs):
        ins, outs = refs[:m], refs[m:2 * m]
        send_sems, recv_sems = refs[2 * m:]
        x, y, c = _position()
        chips = [(1 - x, y), (x, 1 - y), (1 - x, 1 - y)]
        sends = []
        for a in range(m):
            r, cw = shard_shapes[a]
            for j, (px, py) in enumerate(chips):
                cp = pltpu.make_async_remote_copy(
                    src_ref=_weight_window(ins[a], cols[a], r, cw, px, py, c),
                    dst_ref=_weight_window(outs[a], cols[a], r, cw, px, py, c),
                    send_sem=send_sems.at[a, j], recv_sem=recv_sems.at[a, j], device_id=(x, y, 1 - c),
                    device_id_type=MESH)
                cp.start()
                sends.append(cp)
        for a in range(m):
            r, cw = shard_shapes[a]
            for j, (px, py) in enumerate(chips):
                pltpu.make_async_remote_copy(
                    src_ref=_weight_window(ins[a], cols[a], r, cw, px, py, c),
                    dst_ref=_weight_window(outs[a], cols[a], r, cw, px, py, 1 - c),
                    send_sem=send_sems.at[a, j], recv_sem=recv_sems.at[a, j], device_id=(x, y, 1 - c),
                    device_id_type=MESH).wait_recv()
        for cp in sends:
            cp.wait_send()

    return pl.pallas_call(
        body, name=name, out_shape=tuple(jax.ShapeDtypeStruct(a.shape, a.dtype) for a in lands),
        in_specs=[ANY] * m, out_specs=tuple([ANY] * m), input_output_aliases={i: i for i in range(m)},
        scratch_shapes=[pltpu.SemaphoreType.DMA((m, 3)), pltpu.SemaphoreType.DMA((m, 3))],
        compiler_params=pltpu.CompilerParams(vmem_limit_bytes=VMEM_LIMIT_BYTES),
    )(*lands)


def _exchange_start(sums, name):
    m = len(sums)
    lands = [lax.empty((3,) + s.shape[1:], s.dtype) for s in sums]

    def body(*refs):
        cs, ld = refs[:m], refs[m:2 * m]
        send_ref, recv_ref, token = refs[2 * m], refs[2 * m + 1], refs[-1]
        x, y, c = _position()
        for a in range(m):
            for j, (px, py) in enumerate([(1 - x, y), (x, 1 - y), (1 - x, 1 - y)]):
                pltpu.make_async_remote_copy(
                    src_ref=cs[a].at[2 * px + py], dst_ref=ld[a].at[j], send_sem=send_ref.at[3 * a + j],
                    recv_sem=recv_ref.at[3 * a + j], device_id=(px, py, c), device_id_type=MESH).start()
        token[...] = jnp.zeros_like(token)

    arrs = list(sums) + lands
    out = pl.pallas_call(
        body, name=name,
        out_shape=(pltpu.SemaphoreType.DMA((m * 3,)), pltpu.SemaphoreType.DMA((m * 3,)))
        + tuple(pltpu.HBM(a.shape, a.dtype) for a in arrs) + (TOKEN,),
        in_specs=[HBM_SPEC] * (2 * m),
        out_specs=(SEM_SPEC, SEM_SPEC) + (HBM_SPEC,) * (2 * m) + (pl.BlockSpec(memory_space=pltpu.VMEM),),
        input_output_aliases={i: 2 + i for i in range(2 * m)}, compiler_params=SPLIT_PARAMS,
    )(*[_hbm(a) for a in arrs])
    return out[0], out[1], out[2:2 + m], out[2 + m:2 + 2 * m], out[-1]


def _exchange_wait(sums, lands, send, recv, after, name):
    m = len(sums)

    def body(*refs):
        cs, ld = refs[:m], refs[m:2 * m]
        send_ref, recv_ref = refs[2 * m], refs[2 * m + 1]
        x, y, c = _position()
        for a in range(m):
            for j, (px, py) in enumerate([(1 - x, y), (x, 1 - y), (1 - x, 1 - y)]):
                cp = pltpu.make_async_remote_copy(
                    src_ref=cs[a].at[2 * px + py], dst_ref=ld[a].at[j], send_sem=send_ref.at[3 * a + j],
                    recv_sem=recv_ref.at[3 * a + j], device_id=(px, py, c), device_id_type=MESH)
                cp.wait_send()
                cp.wait_recv()

    arrs = list(sums) + list(lands)
    out = pl.pallas_call(
        body, name=name, out_shape=tuple(pltpu.HBM(a.shape, a.dtype) for a in arrs),
        in_specs=[HBM_SPEC] * (2 * m) + [SEM_SPEC, SEM_SPEC, ANY], out_specs=(HBM_SPEC,) * (2 * m),
        input_output_aliases={i: i for i in range(2 * m)}, compiler_params=SPLIT_PARAMS,
    )(*arrs, send, recv, after)
    return out[:m], out[m:]


def _vec(a):
    return a.reshape(1, -1)


def _local_step(x, tgt, mod, W, P, get_w=None, on_grads=None):
    Dm = D_MODEL
    G = {k: [] for k in ("gm_w_in", "gm_w_out", "hg_w_in", "hg_w_out", "ffn_w_up", "ffn_w_down")}
    lb_all = _lb_fwd(P["hg_lb"], "lb_fwd")
    saved = []
    xs = x
    y_prev = gate_prev = None
    layer_w = [None] * DEPTH

    def wmm(xa, kind, i, mode, name):
        if layer_w[i] is not None:
            return _mm(xa, layer_w[i][kind], mode, name)
        return _mm(xa, W[kind], mode, name, b_layer=i if kind.startswith("ffn") else i // 2)

    for i in range(DEPTH):
        m = [_vec(mod[i, j * Dm:(j + 1) * Dm]) for j in range(6)]
        sh1, sc1, g1, sh2, sc2, g2 = m
        j = i // 2
        if get_w is not None:
            layer_w[i] = get_w(i, xs if y_prev is None else y_prev)
        xs, h = _norm_fwd(xs, y_prev, gate_prev, _vec(P["norm_g"][i, 0]), sc1, sh1, f"norm_fwd_a{i}")
        rec = dict(x1=xs, h1=h)
        if i % 2 == 0:
            a = wmm(h, "gm_w_in", i, "nn", f"gm_in{i}")
            gated = _gm_mid_fwd(a, _vec(P["gm_ln_g"][j]), _vec(P["gm_ln_b"][j]), P["gm_w_s"][j],
                                P["gm_b_s"][j].reshape(GM_HEADS, GM_BLOCK, 1), f"gm_mid_fwd{i}")
            y1 = wmm(gated, "gm_w_out", i, "nn", f"gm_out{i}")
            rec.update(a=a, act=gated)
        else:
            p = wmm(h, "hg_w_in", i, "nn", f"hg_in{i}")
            o, og, states = _hg_scan_fwd(p, _vec(lb_all[j]), _vec(P["hg_gn_g"][j]), f"hg_scan_fwd{i}")
            y1 = wmm(og, "hg_w_out", i, "nn", f"hg_out{i}")
            rec.update(a=p, act=og, o=o, states=states)
        rec["y1"] = y1
        xs, h2 = _norm_fwd(xs, y1, g1, _vec(P["norm_g"][i, 1]), sc2, sh2, f"norm_fwd_b{i}")
        a2 = wmm(h2, "ffn_w_up", i, "nn", f"ffn_up{i}")
        mm_ = _conv_fwd(a2, P["ffn_conv_w"][i], _vec(P["ffn_conv_b"][i]), f"conv_fwd{i}")
        y2 = wmm(mm_, "ffn_w_down", i, "nn", f"ffn_down{i}")
        rec.update(x2=xs, h2=h2, a2=a2, m=mm_, y2=y2, mods=m)
        saved.append(rec)
        y_prev, gate_prev = y2, g2
    dx, dy, loss, s_fg, s_gate = _loss_head(xs, y_prev, gate_prev, _vec(P["final_g"]), tgt, "loss_head")
    small = dict(final_g=s_fg, norm_g=[None] * DEPTH, dmod=[None] * DEPTH, ffn_conv_w=[None] * DEPTH,
                 ffn_conv_b=[None] * DEPTH, gm_ln_g=[None] * 2, gm_ln_b=[None] * 2, gm_w_s=[None] * 2,
                 gm_b_s=[None] * 2, hg_gn_g=[None] * 2, dlb=[None] * 2)
    for i in reversed(range(DEPTH)):
        rec = saved[i]
        sh1, sc1, g1, sh2, sc2, g2 = rec["mods"]
        j = i // 2
        d_g2 = s_gate
        dm = wmm(dy, "ffn_w_down", i, "nt", f"ffn_down_dx{i}")
        G["ffn_w_down"].append(_mm(rec["m"], dy, "tn", f"ffn_down_dw{i}"))
        da2, dcw, dcb = _conv_bwd(rec["a2"], dm, P["ffn_conv_w"][i], _vec(P["ffn_conv_b"][i]), f"conv_bwd{i}")
        small["ffn_conv_w"][i], small["ffn_conv_b"][i] = dcw, dcb
        dh2 = wmm(da2, "ffn_w_up", i, "nt", f"ffn_up_dx{i}")
        G["ffn_w_up"].append(_mm(rec["h2"], da2, "tn", f"ffn_up_dw{i}", exchange_out=True))
        ng2 = _vec(P["norm_g"][i, 1])
        dx, dy, s_sh2, s_x2, d_g1 = _norm_bwd(rec["x2"], dh2, dx, ng2, sc2, rec["y1"], g1, f"norm_bwd_b{i}")
        d_sc2, d_ng2 = s_x2 * ng2, s_x2 * (1.0 + sc2)
        if i % 2 == 0:
            dgated = wmm(dy, "gm_w_out", i, "nt", f"gm_out_dx{i}")
            G["gm_w_out"].append(_mm(rec["act"], dy, "tn", f"gm_out_dw{i}"))
            da, dws, dbs, dlg, dlbeta = _gm_mid_bwd(
                rec["a"], dgated, _vec(P["gm_ln_g"][j]), _vec(P["gm_ln_b"][j]), P["gm_w_s"][j],
                P["gm_b_s"][j].reshape(GM_HEADS, GM_BLOCK, 1), f"gm_mid_bwd{i}")
            small["gm_w_s"][j], small["gm_b_s"][j] = dws, dbs[:, :GM_HEADS].T
            small["gm_ln_g"][j], small["gm_ln_b"][j] = dlg, dlbeta
            dh1 = wmm(da, "gm_w_in", i, "nt", f"gm_in_dx{i}")
            G["gm_w_in"].append(_mm(rec["h1"], da, "tn", f"gm_in_dw{i}", exchange_out=True))
        else:
            dog = wmm(dy, "hg_w_out", i, "nt", f"hg_out_dx{i}")
            G["hg_w_out"].append(_mm(rec["act"], dy, "tn", f"hg_out_dw{i}"))
            dp, dlb, dgn = _hg_scan_bwd(rec["a"], _vec(lb_all[j]), _vec(P["hg_gn_g"][j]), rec["o"], dog,
                                        rec["states"], f"hg_scan_bwd{i}")
            small["dlb"][j], small["hg_gn_g"][j] = dlb, dgn
            dh1 = wmm(dp, "hg_w_in", i, "nt", f"hg_in_dx{i}")
            G["hg_w_in"].append(_mm(rec["h1"], dp, "tn", f"hg_in_dw{i}", exchange_out=True))
        ng1 = _vec(P["norm_g"][i, 0])
        if on_grads is not None:
            mixer = ("gm_w_in", "gm_w_out") if i % 2 == 0 else ("hg_w_in", "hg_w_out")
            ng1 = ng1 + on_grads(i, {k: G[k][-1] for k in mixer + ("ffn_w_up", "ffn_w_down")})
        if i > 0:
            prev = saved[i - 1]
            dx, dy, s_sh1, s_x1, s_gate = _norm_bwd(rec["x1"], dh1, dx, ng1, sc1, prev["y2"], prev["mods"][5],
                                                    f"norm_bwd_a{i}")
        else:
            dx, s_sh1, s_x1 = _norm_bwd(rec["x1"], dh1, dx, ng1, sc1, None, None, f"norm_bwd_a{i}")
        d_sc1, d_ng1 = s_x1 * ng1, s_x1 * (1.0 + sc1)
        small["norm_g"][i] = jnp.concatenate([d_ng1, d_ng2], axis=0)
        small["dmod"][i] = jnp.concatenate([s_sh1, d_sc1, d_g1, s_sh2, d_sc2, d_g2], axis=1)
    for k in G:
        G[k] = G[k][::-1]
    dlb_all = jnp.concatenate(small.pop("dlb"), axis=0)
    small["hg_lb"] = _lb_bwd(P["hg_lb"], dlb_all, "lb_bwd")
    return loss, dx, G, small


BIG = ("gm_w_in", "gm_w_out", "hg_w_in", "hg_w_out", "ffn_w_up", "ffn_w_down")
COL_SHARDED = dict(gm_w_in=True, gm_w_out=False, hg_w_in=True, hg_w_out=False, ffn_w_up=True, ffn_w_down=False)
LAYER_WEIGHTS = 4


def _layer_kinds(i):
    return (("gm_w_in", "gm_w_out") if i % 2 == 0 else ("hg_w_in", "hg_w_out")) + ("ffn_w_up", "ffn_w_down")


def _pack(pieces):
    flat = [p.reshape(-1).astype(F32) for p in pieces]
    offs, tot = [], 0
    for f in flat:
        offs.append((tot, f.shape[0]))
        tot += f.shape[0]
    padded = -(-tot // (8 * LANES)) * (8 * LANES)
    if padded > tot:
        flat.append(jnp.zeros((padded - tot,), F32))
    return jnp.concatenate(flat).reshape(-1, LANES), offs


def _unpack(rows, offs, shapes):
    lead = rows.shape[:-2]
    flat = rows.reshape(lead + (-1,))
    return [flat[..., o:o + n].reshape(lead + tuple(s)) for (o, n), s in zip(offs, shapes)]


def _from_chips(per_dev, axis):
    per_chip = per_dev[0::2]
    return jnp.concatenate([per_chip[s] for s in range(N_CHIPS)], axis=axis)


def kernel(x, c, gm_w_in, gm_ln_g, gm_ln_b, gm_w_s, gm_b_s, gm_w_out, hg_w_in, hg_lb, hg_gn_g, hg_w_out, ffn_w_up, ffn_conv_w, ffn_conv_b, ffn_w_down, norm_g, ada_w, ada_b, final_g, loss_target, m_gm_w_in, m_gm_ln_g, m_gm_ln_b, m_gm_w_s, m_gm_b_s, m_gm_w_out, m_hg_w_in, m_hg_lb, m_hg_gn_g, m_hg_w_out, m_ffn_w_up, m_ffn_conv_w, m_ffn_conv_b, m_ffn_w_down, m_norm_g, m_ada_w, m_ada_b, m_final_g, v_gm_w_in, v_gm_ln_g, v_gm_ln_b, v_gm_w_s, v_gm_b_s, v_gm_w_out, v_hg_w_in, v_hg_lb, v_hg_gn_g, v_hg_w_out, v_ffn_w_up, v_ffn_conv_w, v_ffn_conv_b, v_ffn_w_down, v_norm_g, v_ada_w, v_ada_b, v_final_g):
    Dm = D_MODEL
    xi, yi, ci = _position()
    chip = 2 * xi + yi
    dev = 4 * xi + 2 * yi + ci
    weights = dict(gm_w_in=gm_w_in, gm_ln_g=gm_ln_g, gm_ln_b=gm_ln_b, gm_w_s=gm_w_s, gm_b_s=gm_b_s,
                   gm_w_out=gm_w_out, hg_w_in=hg_w_in, hg_lb=hg_lb, hg_gn_g=hg_gn_g, hg_w_out=hg_w_out,
                   ffn_w_up=ffn_w_up, ffn_conv_w=ffn_conv_w, ffn_conv_b=ffn_conv_b, ffn_w_down=ffn_w_down,
                   norm_g=norm_g, ada_w=ada_w, ada_b=ada_b, final_g=final_g)
    mom_m = dict(gm_w_in=m_gm_w_in, gm_ln_g=m_gm_ln_g, gm_ln_b=m_gm_ln_b, gm_w_s=m_gm_w_s, gm_b_s=m_gm_b_s,
                 gm_w_out=m_gm_w_out, hg_w_in=m_hg_w_in, hg_lb=m_hg_lb, hg_gn_g=m_hg_gn_g, hg_w_out=m_hg_w_out,
                 ffn_w_up=m_ffn_w_up, ffn_conv_w=m_ffn_conv_w, ffn_conv_b=m_ffn_conv_b, ffn_w_down=m_ffn_w_down,
                 norm_g=m_norm_g, ada_w=m_ada_w, ada_b=m_ada_b, final_g=m_final_g)
    mom_v = dict(gm_w_in=v_gm_w_in, gm_ln_g=v_gm_ln_g, gm_ln_b=v_gm_ln_b, gm_w_s=v_gm_w_s, gm_b_s=v_gm_b_s,
                 gm_w_out=v_gm_w_out, hg_w_in=v_hg_w_in, hg_lb=v_hg_lb, hg_gn_g=v_hg_gn_g, hg_w_out=v_hg_w_out,
                 ffn_w_up=v_ffn_w_up, ffn_conv_w=v_ffn_conv_w, ffn_conv_b=v_ffn_conv_b, ffn_w_down=v_ffn_w_down,
                 norm_g=v_norm_g, ada_w=v_ada_w, ada_b=v_ada_b, final_g=v_final_g)
    order = list(weights)

    pos = jnp.stack([chip, ci]).astype(jnp.int32)
    shards, by_col = [], []
    for i in range(DEPTH):
        for k in _layer_kinds(i):
            shards.append(weights[k][i if k.startswith("ffn") else i // 2].astype(BF16))
            by_col.append(COL_SHARDED[k])
    placed = [_place_own(sh, col, pos, f"place_own{n}") for n, (sh, col) in enumerate(zip(shards, by_col))]
    gsems, sh_thru, ld_thru, _ = _gather_start(shards, placed, by_col, LAYER_WEIGHTS, "gather_start")

    pieces = [c, hg_lb, hg_gn_g, norm_g, ffn_conv_w]
    payload, offs = _pack(pieces)
    got = _allgather_small(payload, "gather_small")
    c_g, lb_g, gn_g, ng_g, cw_g = _unpack(got, offs, [p.shape for p in pieces])
    c_all = c_g.reshape(N_DEV, Dm)
    P = dict(hg_lb=_from_chips(lb_g, 1), hg_gn_g=_from_chips(gn_g, 1), norm_g=_from_chips(ng_g, 2),
             ffn_conv_w=_from_chips(cw_g, 2), gm_ln_g=gm_ln_g, gm_ln_b=gm_ln_b, gm_w_s=gm_w_s, gm_b_s=gm_b_s,
             ffn_conv_b=ffn_conv_b, final_g=final_g)

    cols = ada_w.shape[2]
    ada_b_sh = lax.dynamic_slice_in_dim(ada_b, chip * cols, cols, axis=1)
    mod_sh = _ada_fwd(c_all, ada_w, ada_b_sh, "ada_fwd")
    mod_g = _allgather_small(mod_sh.reshape(-1, LANES), "gather_mod").reshape(N_DEV, DEPTH, N_DEV, cols)
    mod_mine = lax.dynamic_index_in_dim(mod_g[0::2], dev, axis=2, keepdims=False)
    mod = jnp.transpose(mod_mine, (1, 0, 2)).reshape(DEPTH, N_CHIPS * cols)

    core = jnp.reshape(ci, (1,)).astype(jnp.int32)
    chip_arr = jnp.reshape(chip, (1,)).astype(jnp.int32)
    pending = [None] * DEPTH

    def get_w(i, after):
        s = slice(LAYER_WEIGHTS * i, LAYER_WEIGHTS * (i + 1))
        landed = _gather_wait(sh_thru[s], ld_thru[s], gsems[2 * i], gsems[2 * i + 1], after, by_col[s],
                              f"gather_wait{i}")
        full = _forward_sibling(landed, by_col[s], [a.shape for a in shards[s]], f"gather_forward{i}")
        return dict(zip(_layer_kinds(i), full))

    def on_grads(i, gdict):
        kinds = _layer_kinds(i)
        g4 = []
        for k in kinds:
            g = gdict[k]
            if not COL_SHARDED[k]:
                R, C = g.shape
                g = g.reshape(N_CHIPS, 2, R // (2 * N_CHIPS), C)
            g4.append(g)
        to_sib = [lax.dynamic_index_in_dim(g, 1 - ci, axis=1, keepdims=False).astype(BF16) for g in g4]
        from_sib = _swap_sibling(to_sib, f"reduce_swap{i}")
        sums = [_add_own_half(g, r, core, f"chip_sum_{k}{i}") for g, r, k in zip(g4, from_sib, kinds)]
        send, recv, sums_thru, lands, token = _exchange_start(sums, f"reduce_start{i}")
        pending[i] = (send, recv, sums_thru, lands)
        return token[0, 0]

    loss_part, dx, G, small = _local_step(x[0], loss_target[0], mod, None, P, get_w, on_grads)

    by_kind = {k: ([], []) for k in BIG}
    for i in range(DEPTH):
        send, recv, sums_thru, lands = pending[i]
        sums_i, lands_i = _exchange_wait(sums_thru, lands, send, recv, dx, f"reduce_wait{i}")
        for k, s_, l_ in zip(_layer_kinds(i), sums_i, lands_i):
            by_kind[k][0].append(l_)
            by_kind[k][1].append(s_)
    own_halves = [_sum_chips(by_kind[k][0], by_kind[k][1], chip_arr, f"sum_chips_{k}") for k in BIG]
    sib_halves = _swap_sibling(own_halves, "reduce_join")
    grads, deltas, new_m, new_v = {}, {}, {}, {}
    for k, own, recv in zip(BIG, own_halves, sib_halves):
        grads[k], deltas[k], new_m[k], new_v[k] = _adamw_halves(
            weights[k], own, recv, mom_m[k], mom_v[k], core, f"adamw_{k}")

    sum_pieces = [loss_part[:, :1], small["final_g"], jnp.stack(small["gm_ln_g"]), jnp.stack(small["gm_ln_b"]),
                  jnp.stack(small["gm_w_s"]), jnp.stack(small["gm_b_s"]), jnp.stack(small["ffn_conv_b"]),
                  small["hg_lb"], jnp.stack(small["hg_gn_g"]), jnp.stack(small["norm_g"]),
                  jnp.stack(small["ffn_conv_w"])]
    dmod = jnp.concatenate(small["dmod"], axis=0)
    payload2, offs2 = _pack(sum_pieces + [dmod])
    got2 = _allgather_small(payload2, "gather_grads")
    dmod_all = _unpack(got2, offs2[-1:], [dmod.shape])[0]
    summed = _sum_devices(got2, "sum_devices")
    (loss_s, d_final_g, d_ln_g, d_ln_b, d_ws, d_bs, d_cb, d_lb, d_gn, d_ng, d_cw) = _unpack(
        summed, offs2[:-1], [(1,), final_g.shape, gm_ln_g.shape, gm_ln_b.shape, gm_w_s.shape, gm_b_s.shape,
                             ffn_conv_b.shape, (2, Dm), (2, Dm), (DEPTH, 2, Dm), (DEPTH, 3, 2 * FFN_HIDDEN)])
    grads.update(final_g=d_final_g, gm_ln_g=d_ln_g, gm_ln_b=d_ln_b, gm_w_s=d_ws, gm_b_s=d_bs, ffn_conv_b=d_cb)
    grads["hg_lb"] = lax.dynamic_slice_in_dim(d_lb, chip * hg_lb.shape[1], hg_lb.shape[1], axis=1)
    grads["hg_gn_g"] = lax.dynamic_slice_in_dim(d_gn, chip * hg_gn_g.shape[1], hg_gn_g.shape[1], axis=1)
    grads["norm_g"] = lax.dynamic_slice_in_dim(d_ng, chip * norm_g.shape[2], norm_g.shape[2], axis=2)
    grads["ffn_conv_w"] = lax.dynamic_slice_in_dim(d_cw, chip * ffn_conv_w.shape[2], ffn_conv_w.shape[2], axis=2)
    dmod_sh = lax.dynamic_slice_in_dim(dmod_all, chip * cols, cols, axis=2)
    grads["ada_w"] = _ada_bwd(c_all, jnp.transpose(dmod_sh, (1, 0, 2)), "ada_bwd")
    grads["ada_b"] = _sum_devices(dmod_all.reshape(N_DEV, -1, LANES), "sum_ada_b").reshape(ada_b.shape)

    for k in order:
        if k in BIG:
            continue
        w = weights[k]
        shp = w.shape
        view = (-1, shp[-1]) if w.ndim > 1 else (8, -1)
        d, m2, v2 = _adamw(w.reshape(view), grads[k].reshape(view), mom_m[k].reshape(view), mom_v[k].reshape(view),
                           f"adamw_{k}")
        deltas[k], new_m[k], new_v[k] = d.reshape(shp), m2.reshape(shp), v2.reshape(shp)
        grads[k] = grads[k].reshape(shp)

    loss = loss_s.reshape(())
    return (loss, dx[None], *[grads[k] for k in order], *[deltas[k] for k in order],
            *[new_m[k] for k in order], *[new_v[k] for k in order])
```

```python
import functools

import jax
import jax.numpy as jnp
from jax import lax
from jax.experimental import pallas as pl
from jax.experimental.pallas import tpu as pltpu

F32 = jnp.float32
BF16 = jnp.bfloat16
HI = lax.Precision.HIGHEST
X3 = lax.Precision.HIGH
MESH = pl.DeviceIdType.MESH

D_MODEL = 1024
DEPTH = 4
EPS = 1e-6
GM_WIDTH = 2048
GM_HEADS = 8
GM_BLOCK = 128
GM_HEAD_DIM = 256
CHUNK = 64
HG_HEADS = 8
HG_DIM = 128
FFN_HIDDEN = 2816
N_CHIPS = 4
N_DEV = 8

ADAM_LR = 0.001
ADAM_B1 = 0.9
ADAM_B2 = 0.999
ADAM_EPS = 1e-08
ADAM_WD = 0.01
ADAM_STEP = 10

VMEM_LIMIT_BYTES = 56 * 1024 * 1024
ROW_TILE = 256
LANES = 128

_SQRT_HALF = 0.7071067811865476
_INV_SQRT_2PI = 0.3989422804014327


def _pick(dim, prefs):
    for p in prefs:
        if dim % p == 0:
            return p
    return dim


def _params(sem):
    return pltpu.CompilerParams(dimension_semantics=sem, vmem_limit_bytes=VMEM_LIMIT_BYTES)


def _cdf(x):
    return 0.5 * (1.0 + lax.erf(x * _SQRT_HALF))


def _pdf(x):
    return jnp.exp(-0.5 * x * x) * _INV_SQRT_2PI


def _sig(x):
    return jax.nn.sigmoid(x)


def _dot(a, b, dims, prec=None):
    return lax.dot_general(a, b, (dims, ((), ())), precision=prec, preferred_element_type=F32)


NN = ((1,), (0,))
NT = ((1,), (1,))
TN = ((0,), (0,))


MM_VMEM_BUDGET = 40 * 1024 * 1024


def _mm_tiles(mode, M, N, K, a_bytes, b_bytes, exchange_out):
    tn = _pick(N, (1408, 1024, 512, 256, 128))
    tms = [t for t in (1408, 1024, 512, 256, 128) if M % t == 0 and not (exchange_out and (M // 2) % t)] or [M]
    tks = [K] + [t for t in (2816, 2048, 1408, 1024, 512, 256, 128) if t < K and K % t == 0]

    def fits(tm, tk):
        acc = tm * tn * 4 if tk < K else 0
        return 2 * tm * tk * a_bytes + 2 * tk * tn * b_bytes + 2 * tm * tn * 4 + acc <= MM_VMEM_BUDGET

    for min_tm in (min(512, tms[0]), 0):
        for tk in tks:
            for tm in tms:
                if tm >= min_tm and fits(tm, tk):
                    return tm, tn, tk
    return tms[-1], tn, tks[-1]


def _mm(a, b, mode, name, b_layer=None, out_dtype=F32, exchange_out=False):
    b2 = b.shape[-2:]
    if mode == "nn":
        (M, K), (_, N) = a.shape, b2
    elif mode == "nt":
        (M, K), (N, _) = a.shape, b2
    else:
        (K, M), (_, N) = a.shape, b2
    tm, tn, tk = _mm_tiles(mode, M, N, K, a.dtype.itemsize, b.dtype.itemsize, exchange_out)
    nk = K // tk
    dims = {"nn": NN, "nt": NT, "tn": TN}[mode]

    def body(a_ref, b_ref, o_ref, *scratch):
        part = _dot(a_ref[...].astype(BF16), b_ref[...].astype(BF16), dims)
        if nk == 1:
            o_ref[...] = part.astype(o_ref.dtype)
            return
        (acc_ref,) = scratch
        k = pl.program_id(2)

        @pl.when(k == 0)
        def _():
            acc_ref[...] = part

        @pl.when(k > 0)
        def _():
            acc_ref[...] += part

        @pl.when(k == nk - 1)
        def _():
            o_ref[...] = acc_ref[...].astype(o_ref.dtype)

    if mode == "tn":
        a_spec = pl.BlockSpec((tk, tm), lambda i, j, k: (k, i))
    else:
        a_spec = pl.BlockSpec((tm, tk), lambda i, j, k: (i, k))
    bblk = (tk, tn) if mode in ("nn", "tn") else (tn, tk)
    bidx = (lambda i, j, k: (k, j)) if mode in ("nn", "tn") else (lambda i, j, k: (j, k))
    if b_layer is None:
        b_spec = pl.BlockSpec(bblk, bidx)
    else:
        b_spec = pl.BlockSpec((None,) + bblk, lambda i, j, k: (b_layer,) + bidx(i, j, k))
    if exchange_out:
        mh, cw = M // 2, N // N_CHIPS
        assert mh % tm == 0 and cw % tn == 0
        out_shape = jax.ShapeDtypeStruct((N_CHIPS, 2, mh, cw), out_dtype)
        o_spec = pl.BlockSpec(
            (None, None, tm, tn),
            lambda i, j, k: (j // (cw // tn), i // (mh // tm), i % (mh // tm), j % (cw // tn)))
    else:
        out_shape = jax.ShapeDtypeStruct((M, N), out_dtype)
        o_spec = pl.BlockSpec((tm, tn), lambda i, j, k: (i, j))
    return pl.pallas_call(
        body, name=name, out_shape=out_shape, grid=(M // tm, N // tn, nk),
        in_specs=[a_spec, b_spec], out_specs=o_spec,
        scratch_shapes=[] if nk == 1 else [pltpu.VMEM((tm, tn), F32)],
        compiler_params=_params(("parallel", "parallel", "arbitrary")),
    )(a, b)


def _row_spec(tr, width):
    return pl.BlockSpec((tr, width), lambda i: (i, 0))


def _vec_spec(width, rows=1):
    return pl.BlockSpec((rows, width), lambda i: (0, 0))


def _norm_fwd(x, y, gate, g, sc, sh, name):
    T, Dm = x.shape
    tr = _pick(T, (ROW_TILE,))
    has_res = y is not None

    def body(*refs):
        if has_res:
            x_ref, y_ref, gate_ref, g_ref, sc_ref, sh_ref, xo_ref, h_ref = refs
            xv = x_ref[...] + gate_ref[...] * y_ref[...]
            xo_ref[...] = xv
        else:
            x_ref, g_ref, sc_ref, sh_ref, h_ref = refs
            xv = x_ref[...]
        rstd = lax.rsqrt(jnp.mean(xv * xv, axis=-1, keepdims=True) + EPS)
        h_ref[...] = ((xv * rstd * g_ref[...]) * (1.0 + sc_ref[...]) + sh_ref[...]).astype(BF16)

    row, vec = _row_spec(tr, Dm), _vec_spec(Dm)
    if has_res:
        ins, in_specs = (x, y, gate, g, sc, sh), [row, row, vec, vec, vec, vec]
        out_shape = (jax.ShapeDtypeStruct((T, Dm), F32), jax.ShapeDtypeStruct((T, Dm), BF16))
        out_specs = (row, row)
    else:
        ins, in_specs = (x, g, sc, sh), [row, vec, vec, vec]
        out_shape = jax.ShapeDtypeStruct((T, Dm), BF16)
        out_specs = row
    out = pl.pallas_call(body, name=name, out_shape=out_shape, grid=(T // tr,), in_specs=in_specs,
                         out_specs=out_specs, compiler_params=_params(("parallel",)))(*ins)
    return out if has_res else (x, out)


def _norm_bwd(x, dh, dxo, g, sc, y_prev, gate_prev, name):
    T, Dm = x.shape
    tr = _pick(T, (ROW_TILE,))
    has_prev = y_prev is not None

    def body(*refs):
        if has_prev:
            x_ref, dh_ref, dxo_ref, g_ref, sc_ref, yp_ref, gp_ref, dx_ref, dyp_ref, s1_ref, s2_ref, s3_ref = refs
        else:
            x_ref, dh_ref, dxo_ref, g_ref, sc_ref, dx_ref, s1_ref, s2_ref = refs

        @pl.when(pl.program_id(0) == 0)
        def _():
            s1_ref[...] = jnp.zeros_like(s1_ref)
            s2_ref[...] = jnp.zeros_like(s2_ref)
            if has_prev:
                s3_ref[...] = jnp.zeros_like(s3_ref)

        xv = x_ref[...]
        rstd = lax.rsqrt(jnp.mean(xv * xv, axis=-1, keepdims=True) + EPS)
        xhat = xv * rstd
        dh = dh_ref[...]
        dxhat = dh * (g_ref[...] * (1.0 + sc_ref[...]))
        dx = dxo_ref[...] + rstd * (dxhat - xhat * jnp.mean(dxhat * xhat, axis=-1, keepdims=True))
        dx_ref[...] = dx
        s1_ref[...] += jnp.sum(dh, axis=0, keepdims=True)
        s2_ref[...] += jnp.sum(dh * xhat, axis=0, keepdims=True)
        if has_prev:
            dyp_ref[...] = (gp_ref[...] * dx).astype(BF16)
            s3_ref[...] += jnp.sum(dx * yp_ref[...], axis=0, keepdims=True)

    row, vec = _row_spec(tr, Dm), _vec_spec(Dm)
    vshape = jax.ShapeDtypeStruct((1, Dm), F32)
    if has_prev:
        ins, in_specs = (x, dh, dxo, g, sc, y_prev, gate_prev), [row, row, row, vec, vec, row, vec]
        out_shape = (jax.ShapeDtypeStruct((T, Dm), F32), jax.ShapeDtypeStruct((T, Dm), BF16), vshape, vshape, vshape)
        out_specs = (row, row, vec, vec, vec)
    else:
        ins, in_specs = (x, dh, dxo, g, sc), [row, row, row, vec, vec]
        out_shape = (jax.ShapeDtypeStruct((T, Dm), F32), vshape, vshape)
        out_specs = (row, vec, vec)
    return pl.pallas_call(body, name=name, out_shape=out_shape, grid=(T // tr,), in_specs=in_specs,
                          out_specs=out_specs, compiler_params=_params(("arbitrary",)))(*ins)


def _loss_head(x, y, gate, fg, tgt, name):
    T, Dm = x.shape
    tr = _pick(T, (ROW_TILE,))
    nsteps = T // tr

    def body(x_ref, y_ref, gate_ref, fg_ref, t_ref, dx_ref, dy_ref, loss_ref, sfg_ref, sg_ref, acc_ref):
        i = pl.program_id(0)

        @pl.when(i == 0)
        def _():
            acc_ref[...] = jnp.zeros_like(acc_ref)
            sfg_ref[...] = jnp.zeros_like(sfg_ref)
            sg_ref[...] = jnp.zeros_like(sg_ref)

        yv = y_ref[...]
        xv = x_ref[...] + gate_ref[...] * yv
        rstd = lax.rsqrt(jnp.mean(xv * xv, axis=-1, keepdims=True) + EPS)
        xhat = xv * rstd
        err = xhat * fg_ref[...] - t_ref[...]
        acc_ref[...] += jnp.sum(err * err, axis=0, keepdims=True)
        dyn = err * (1.0 / Dm)
        sfg_ref[...] += jnp.sum(dyn * xhat, axis=0, keepdims=True)
        dxhat = dyn * fg_ref[...]
        dx = rstd * (dxhat - xhat * jnp.mean(dxhat * xhat, axis=-1, keepdims=True))
        dx_ref[...] = dx
        dy_ref[...] = (gate_ref[...] * dx).astype(BF16)
        sg_ref[...] += jnp.sum(dx * yv, axis=0, keepdims=True)

        @pl.when(i == nsteps - 1)
        def _():
            total = jnp.sum(acc_ref[...], axis=1, keepdims=True) * (0.5 / Dm)
            loss_ref[...] = jnp.broadcast_to(total, loss_ref.shape)

    row, vec = _row_spec(tr, Dm), _vec_spec(Dm)
    vshape = jax.ShapeDtypeStruct((1, Dm), F32)
    return pl.pallas_call(
        body, name=name, grid=(nsteps,),
        out_shape=(jax.ShapeDtypeStruct((T, Dm), F32), jax.ShapeDtypeStruct((T, Dm), BF16),
                   jax.ShapeDtypeStruct((1, LANES), F32), vshape, vshape),
        in_specs=[row, row, vec, vec, row], out_specs=(row, row, _vec_spec(LANES), vec, vec),
        scratch_shapes=[pltpu.VMEM((1, Dm), F32)], compiler_params=_params(("arbitrary",)),
    )(x, y, gate, fg, tgt)


def _spatial_mask():
    r = lax.broadcasted_iota(jnp.int32, (GM_BLOCK, GM_BLOCK), 0) // CHUNK
    c = lax.broadcasted_iota(jnp.int32, (GM_BLOCK, GM_BLOCK), 1) // CHUNK
    return r >= c


def _gm_specs(tr):
    return [_row_spec(tr, 2 * GM_WIDTH), _vec_spec(GM_WIDTH), _vec_spec(GM_WIDTH),
            pl.BlockSpec((GM_HEADS, GM_BLOCK, GM_BLOCK), lambda i: (0, 0, 0)),
            pl.BlockSpec((GM_HEADS, GM_BLOCK, 1), lambda i: (0, 0, 0))]


def _gm_mid_fwd(a, ln_g, ln_b, ws, bs3, name):
    T = a.shape[0]
    tr = _pick(T, (ROW_TILE,))
    W = GM_WIDTH

    def body(a_ref, lg_ref, lb_ref, ws_ref, bs_ref, o_ref, vn_scr):
        av = a_ref[:, W:]
        v = av * _cdf(av)
        vc = v - jnp.mean(v, axis=-1, keepdims=True)
        rstd = lax.rsqrt(jnp.mean(vc * vc, axis=-1, keepdims=True) + EPS)
        vn_scr[...] = (vc * rstd * lg_ref[...] + lb_ref[...]).astype(BF16)
        mask = _spatial_mask()
        for h in range(GM_HEADS):
            w = jnp.where(mask, ws_ref[h], 0.0).astype(BF16)
            cs = slice(h * GM_HEAD_DIM, (h + 1) * GM_HEAD_DIM)
            for blk in range(tr // GM_BLOCK):
                rs = slice(blk * GM_BLOCK, (blk + 1) * GM_BLOCK)
                s = _dot(w, vn_scr[rs, cs], NN) + bs_ref[h]
                au = a_ref[rs, cs]
                o_ref[rs, cs] = (au * _cdf(au) * s).astype(BF16)

    return pl.pallas_call(
        body, name=name, out_shape=jax.ShapeDtypeStruct((T, W), BF16), grid=(T // tr,),
        in_specs=_gm_specs(tr), out_specs=_row_spec(tr, W),
        scratch_shapes=[pltpu.VMEM((tr, W), BF16)], compiler_params=_params(("parallel",)),
    )(a, ln_g, ln_b, ws, bs3)


def _gm_mid_bwd(a, dgated, ln_g, ln_b, ws, bs3, name):
    T = a.shape[0]
    tr = _pick(T, (ROW_TILE,))
    W = GM_WIDTH
    nsteps = T // tr

    def body(a_ref, dg_ref, lg_ref, lb_ref, ws_ref, bs_ref, da_ref, dws_ref, dbs_ref, dlg_ref, dlb_ref,
             vn_scr, vhat_scr, dvn_scr, dsum_scr):
        i = pl.program_id(0)

        @pl.when(i == 0)
        def _():
            dws_ref[...] = jnp.zeros_like(dws_ref)
            dbs_ref[...] = jnp.zeros_like(dbs_ref)
            dlg_ref[...] = jnp.zeros_like(dlg_ref)
            dlb_ref[...] = jnp.zeros_like(dlb_ref)
            dsum_scr[...] = jnp.zeros_like(dsum_scr)

        av = a_ref[:, W:]
        cdf_v = _cdf(av)
        v = av * cdf_v
        vc = v - jnp.mean(v, axis=-1, keepdims=True)
        rstd = lax.rsqrt(jnp.mean(vc * vc, axis=-1, keepdims=True) + EPS)
        vhat_scr[...] = vc * rstd
        vn_scr[...] = (vhat_scr[...] * lg_ref[...] + lb_ref[...]).astype(BF16)
        mask = _spatial_mask()
        for h in range(GM_HEADS):
            w = jnp.where(mask, ws_ref[h], 0.0).astype(BF16)
            cs = slice(h * GM_HEAD_DIM, (h + 1) * GM_HEAD_DIM)
            for blk in range(tr // GM_BLOCK):
                rs = slice(blk * GM_BLOCK, (blk + 1) * GM_BLOCK)
                vnb = vn_scr[rs, cs]
                s = _dot(w, vnb, NN) + bs_ref[h]
                au = a_ref[rs, cs]
                cdf_u = _cdf(au)
                dg = dg_ref[rs, cs]
                ds = dg * (au * cdf_u)
                da_ref[rs, cs] = (dg * s * (cdf_u + au * _pdf(au))).astype(BF16)
                dsb = ds.astype(BF16)
                dvn_scr[rs, cs] = _dot(w, dsb, TN)
                dws_ref[h] += _dot(dsb, vnb, NT)
                dsum_scr[:, cs] += ds
        dvn = dvn_scr[...]
        vhat = vhat_scr[...]
        dlg_ref[...] += jnp.sum(dvn * vhat, axis=0, keepdims=True)
        dlb_ref[...] += jnp.sum(dvn, axis=0, keepdims=True)
        dvh = dvn * lg_ref[...]
        dv = rstd * (dvh - jnp.mean(dvh, axis=-1, keepdims=True)
                     - vhat * jnp.mean(dvh * vhat, axis=-1, keepdims=True))
        da_ref[:, W:] = (dv * (cdf_v + av * _pdf(av))).astype(BF16)

        @pl.when(i == nsteps - 1)
        def _():
            for h in range(GM_HEADS):
                dws_ref[h] = jnp.where(mask, dws_ref[h], 0.0)
            col_head = lax.broadcasted_iota(jnp.int32, (W, GM_BLOCK), 0) // GM_HEAD_DIM
            sel = (col_head == lax.broadcasted_iota(jnp.int32, (W, GM_BLOCK), 1)).astype(F32)
            dbs_ref[...] = _dot(dsum_scr[...], sel, NN, HI)

    vshape = jax.ShapeDtypeStruct((1, W), F32)
    return pl.pallas_call(
        body, name=name, grid=(nsteps,),
        out_shape=(jax.ShapeDtypeStruct((T, 2 * W), BF16), jax.ShapeDtypeStruct((GM_HEADS, GM_BLOCK, GM_BLOCK), F32),
                   jax.ShapeDtypeStruct((GM_BLOCK, GM_BLOCK), F32), vshape, vshape),
        in_specs=[_gm_specs(tr)[0], _row_spec(tr, W)] + _gm_specs(tr)[1:],
        out_specs=(_row_spec(tr, 2 * W), pl.BlockSpec((GM_HEADS, GM_BLOCK, GM_BLOCK), lambda i: (0, 0, 0)),
                   pl.BlockSpec((GM_BLOCK, GM_BLOCK), lambda i: (0, 0)), _vec_spec(W), _vec_spec(W)),
        scratch_shapes=[pltpu.VMEM((tr, W), BF16), pltpu.VMEM((tr, W), F32), pltpu.VMEM((tr, W), F32),
                        pltpu.VMEM((GM_BLOCK, W), F32)],
        compiler_params=_params(("arbitrary",)),
    )(a, dgated, ln_g, ln_b, ws, bs3)


SUB = 16
EXP_CLAMP = 80.0


def _tri(lower):
    r = lax.broadcasted_iota(jnp.int32, (CHUNK, CHUNK), 0)
    c = lax.broadcasted_iota(jnp.int32, (CHUNK, CHUNK), 1)
    return (r >= c) if lower else (c >= r)


def _score_masks():
    i = lax.broadcasted_iota(jnp.int32, (CHUNK, CHUNK), 0)
    j = lax.broadcasted_iota(jnp.int32, (CHUNK, CHUNK), 1)
    bi, bj = i // SUB, j // SUB
    diag = (bi == bj) & (i >= j)
    pair = (bi % 2 == 1) & (bj == bi - 1)
    half = (i >= CHUNK // 2) & (j < CHUNK // 2)
    return diag, pair, half


def _dot01(m, x):
    x1 = x.astype(BF16)
    rest = x - x1.astype(F32)
    x2 = rest.astype(BF16)
    x3 = (rest - x2.astype(F32)).astype(BF16)
    return _dot(m, x1, NN) + (_dot(m, x2, NN) + _dot(m, x3, NN))


def _block_rows(b, offset):
    parts = []
    for blk in range(0, CHUNK, SUB):
        r = blk + offset
        parts.append(jnp.zeros((SUB, b.shape[1]), F32) if r < 0 else jnp.broadcast_to(b[r:r + 1], (SUB, b.shape[1])))
    return jnp.concatenate(parts, axis=0)


def _hg_gates(p_ref, lb_ref, lower):
    Dm = D_MODEL
    heads = []
    for h in range(HG_HEADS):
        c0 = h * HG_DIM
        qr = p_ref[:, c0:c0 + HG_DIM]
        fz = p_ref[:, Dm + c0:Dm + c0 + HG_DIM]
        lbh = lb_ref[:, c0:c0 + HG_DIM]
        sg = _sig(fz)
        f = lbh + (1.0 - lbh) * sg
        sq = _sig(qr)
        heads.append(dict(qr=qr, v=p_ref[:, 2 * Dm + c0:2 * Dm + c0 + HG_DIM],
                          gt=p_ref[:, 3 * Dm + c0:3 * Dm + c0 + HG_DIM], lbh=lbh, sg=sg, f=f, gl=jnp.log(f),
                          kk=1.0 - f, sq=sq, q=qr * sq))
    for g in heads:
        g["b"] = _dot01(lower, g.pop("gl"))
    for g in heads:
        g.update(_hg_scalings(g["q"], g["kk"], g.pop("b")))
    return heads


def _hg_scalings(q, kk, b):
    r_mid = _block_rows(b, SUB // 2 - 1)
    r_prev = _block_rows(b, -1)
    r_end = _block_rows(b, SUB - 1)
    r_half = jnp.broadcast_to(b[CHUNK // 2 - 1:CHUNK // 2], b.shape)
    bc = b[CHUNK - 1:CHUNK]
    eqs = (jnp.exp(jnp.clip(b - r_mid, -EXP_CLAMP, EXP_CLAMP)), jnp.exp(jnp.minimum(b - r_prev, 0.0)),
           jnp.exp(jnp.minimum(b - r_half, 0.0)))
    eks = (jnp.exp(jnp.clip(r_mid - b, -EXP_CLAMP, EXP_CLAMP)), jnp.exp(jnp.minimum(r_end - b, 0.0)),
           jnp.exp(jnp.minimum(r_half - b, 0.0)))
    eb = jnp.exp(b)
    ec = jnp.exp(bc - b)
    return dict(eqs=eqs, eks=eks, eb=eb, ec=ec, e_end=jnp.exp(bc), qs=[q * e for e in eqs],
                ks=[kk * e for e in eks], qe=q * eb, ke=kk * ec)


def _scores(g, masks):
    a = None
    for qs, ks, m in zip(g["qs"], g["ks"], masks):
        part = jnp.where(m, _dot(qs.astype(BF16), ks.astype(BF16), NT), 0.0)
        a = part if a is None else a + part
    return a


def _hg_scan_fwd(p, lb, gn, name):
    T = p.shape[0]
    nc = T // CHUNK
    Dm = D_MODEL

    def body(p_ref, lb_ref, gn_ref, o_ref, og_ref, so_ref, st_ref):
        @pl.when(pl.program_id(0) == 0)
        def _():
            st_ref[...] = jnp.zeros_like(st_ref)

        masks = _score_masks()
        heads = _hg_gates(p_ref, lb_ref, _tri(True).astype(BF16))
        states = [st_ref[h] for h in range(HG_HEADS)]
        scores = [_scores(g, masks) for g in heads]
        outs = [_dot(a.astype(BF16), g["v"].astype(BF16), NN) + _dot(g["qe"], st, NT, X3)
                for g, a, st in zip(heads, scores, states)]
        new_states = [st * g["e_end"] + _dot(g["v"], g["ke"], TN, X3) for g, st in zip(heads, states)]
        for h, (g, o, st, st2) in enumerate(zip(heads, outs, states, new_states)):
            cs = slice(h * HG_DIM, (h + 1) * HG_DIM)
            so_ref[0, h] = st
            st_ref[h] = st2
            o_ref[:, cs] = o
            r = lax.rsqrt(jnp.mean(o * o, axis=-1, keepdims=True) + EPS)
            gt = g["gt"]
            og_ref[:, cs] = (((o * r) * gn_ref[:, cs]).astype(F32) * (gt * _sig(gt))).astype(BF16)

    return pl.pallas_call(
        body, name=name, grid=(nc,),
        out_shape=(jax.ShapeDtypeStruct((T, Dm), F32), jax.ShapeDtypeStruct((T, Dm), BF16),
                   jax.ShapeDtypeStruct((nc, HG_HEADS, HG_DIM, HG_DIM), F32)),
        in_specs=[_row_spec(CHUNK, 4 * Dm), _vec_spec(Dm), _vec_spec(Dm)],
        out_specs=(_row_spec(CHUNK, Dm), _row_spec(CHUNK, Dm),
                   pl.BlockSpec((1, HG_HEADS, HG_DIM, HG_DIM), lambda i: (i, 0, 0, 0))),
        scratch_shapes=[pltpu.VMEM((HG_HEADS, HG_DIM, HG_DIM), F32)],
        compiler_params=_params(("arbitrary",)),
    )(p, lb, gn)


def _hg_scan_bwd(p, lb, gn, o, dog, states, name):
    T = p.shape[0]
    nc = T // CHUNK
    Dm = D_MODEL

    def rev(i):
        return nc - 1 - i

    def body(p_ref, lb_ref, gn_ref, o_ref, dog_ref, st_in_ref, dp_ref, dlb_ref, dgn_ref, dst_ref, carry_ref):
        @pl.when(pl.program_id(0) == 0)
        def _():
            dst_ref[...] = jnp.zeros_like(dst_ref)
            carry_ref[...] = jnp.zeros_like(carry_ref)
            dlb_ref[...] = jnp.zeros_like(dlb_ref)
            dgn_ref[...] = jnp.zeros_like(dgn_ref)

        upper = _tri(False).astype(BF16)
        masks = _score_masks()
        heads = _hg_gates(p_ref, lb_ref, _tri(True).astype(BF16))
        for h, g in enumerate(heads):
            cs = slice(h * HG_DIM, (h + 1) * HG_DIM)
            oh = o_ref[:, cs]
            r = lax.rsqrt(jnp.mean(oh * oh, axis=-1, keepdims=True) + EPS)
            on = oh * r
            gt = g["gt"]
            sgt = _sig(gt)
            sil = gt * sgt
            dogh = dog_ref[:, cs]
            gnh = gn_ref[:, cs]
            don = dogh * gnh * sil
            g["dgn"] = jnp.sum(dogh * on * sil, axis=0, keepdims=True)
            g["dgate"] = dogh * on * gnh * (sgt * (1.0 + gt * (1.0 - sgt)))
            g["do"] = r * (don - on * jnp.mean(don * on, axis=-1, keepdims=True))
            g["dst"] = dst_ref[h]
            g["st"] = st_in_ref[0, h]
            g["carry"] = carry_ref[h]
        for g in heads:
            g["a"] = _scores(g, masks)
            g["dob"] = g["do"].astype(BF16)
            g["da"] = _dot(g["dob"], g["v"].astype(BF16), NT)
        for g in heads:
            g["dv"] = _dot(g["a"].astype(BF16), g["dob"], TN) + _dot(g["ke"].astype(BF16), g["dst"].astype(BF16), NT)
            g["dq"] = _dot(g["do"], g["st"], NN, X3) * g["eb"]
            g["dk"] = _dot(g["v"], g["dst"], NN, X3) * g["ec"]
            g["dst2"] = g["dst"] * g["e_end"] + _dot(g["do"], g["qe"], TN, X3)
        for lvl in range(3):
            for g in heads:
                dam = jnp.where(masks[lvl], g["da"], 0.0)
                g["dq"] = g["dq"] + _dot(dam, g["ks"][lvl], NN, X3) * g["eqs"][lvl]
                g["dk"] = g["dk"] + _dot(dam, g["qs"][lvl], TN, X3) * g["eks"][lvl]
        for g in heads:
            g["dgd"] = g["q"] * g["dq"] - g["kk"] * g["dk"]
            g["dgl"] = _dot01(upper, g["dgd"]) + g["carry"]
        for h, g in enumerate(heads):
            c0 = h * HG_DIM
            cs = slice(c0, c0 + HG_DIM)
            df = g["dgl"] / g["f"] - g["dk"]
            sg, sq, qr = g["sg"], g["sq"], g["qr"]
            dst_ref[h] = g["dst2"]
            carry_ref[h] = g["carry"] + jnp.sum(g["dgd"], axis=0, keepdims=True)
            dgn_ref[:, cs] += g["dgn"]
            dlb_ref[:, cs] += jnp.sum(df * (1.0 - sg), axis=0, keepdims=True)
            dp_ref[:, c0:c0 + HG_DIM] = (g["dq"] * (sq * (1.0 + qr * (1.0 - sq)))).astype(BF16)
            dp_ref[:, Dm + c0:Dm + c0 + HG_DIM] = (df * (1.0 - g["lbh"]) * sg * (1.0 - sg)).astype(BF16)
            dp_ref[:, 2 * Dm + c0:2 * Dm + c0 + HG_DIM] = g["dv"].astype(BF16)
            dp_ref[:, 3 * Dm + c0:3 * Dm + c0 + HG_DIM] = g["dgate"].astype(BF16)

    vshape = jax.ShapeDtypeStruct((1, Dm), F32)
    rrow = lambda w: pl.BlockSpec((CHUNK, w), lambda i: (rev(i), 0))
    return pl.pallas_call(
        body, name=name, grid=(nc,),
        out_shape=(jax.ShapeDtypeStruct((T, 4 * Dm), BF16), vshape, vshape),
        in_specs=[rrow(4 * Dm), _vec_spec(Dm), _vec_spec(Dm), rrow(Dm), rrow(Dm),
                  pl.BlockSpec((1, HG_HEADS, HG_DIM, HG_DIM), lambda i: (rev(i), 0, 0, 0))],
        out_specs=(rrow(4 * Dm), _vec_spec(Dm), _vec_spec(Dm)),
        scratch_shapes=[pltpu.VMEM((HG_HEADS, HG_DIM, HG_DIM), F32), pltpu.VMEM((HG_HEADS, 1, HG_DIM), F32)],
        compiler_params=_params(("arbitrary",)),
    )(p, lb, gn, o, dog, states)


def _lb_fwd(hg_lb, name):
    def body(a_ref, o_ref):
        a0, a1 = a_ref[0:1], a_ref[1:2]
        m = jnp.maximum(a0, a1)
        e0, e1 = jnp.exp(a0 - m), jnp.exp(a1 - m)
        p0, p1 = e0 / (e0 + e1), e1 / (e0 + e1)
        o_ref[0:1] = p0 - p0
        o_ref[1:2] = (p0 + p1) - p0

    return pl.pallas_call(body, name=name, out_shape=jax.ShapeDtypeStruct(hg_lb.shape, F32))(hg_lb)


def _lb_bwd(hg_lb, dlb_all, name):
    def body(a_ref, d_ref, o_ref):
        a0, a1 = a_ref[0:1], a_ref[1:2]
        m = jnp.maximum(a0, a1)
        e0, e1 = jnp.exp(a0 - m), jnp.exp(a1 - m)
        p0, p1 = e0 / (e0 + e1), e1 / (e0 + e1)
        d1 = d_ref[1:2]
        o_ref[0:1] = -p0 * p1 * d1
        o_ref[1:2] = p1 * (1.0 - p1) * d1

    return pl.pallas_call(body, name=name, out_shape=jax.ShapeDtypeStruct(hg_lb.shape, F32))(hg_lb, dlb_all)


CONV_COLS = 256


def _conv_fwd(a, w, b, name):
    T = a.shape[0]
    Fh = FFN_HIDDEN
    tr = _pick(T, (ROW_TILE,))
    cw = CONV_COLS
    hb = tr // 8

    def body(a_ref, ap_ref, w_ref, b_ref, m_ref):
        m0 = (pl.program_id(0) > 0).astype(F32)

        def conv(cc):
            x = jnp.concatenate([ap_ref[:, pl.ds(cc, cw)] * m0, a_ref[:, pl.ds(cc, cw)]], axis=0)
            wv = w_ref[:, pl.ds(cc, cw)]
            y = b_ref[:, pl.ds(cc, cw)] + wv[2:3] * x + wv[1:2] * pltpu.roll(x, 1, axis=0) \
                + wv[0:1] * pltpu.roll(x, 2, axis=0)
            return y[8:]

        def step(c, carry):
            c0 = pl.multiple_of(c * cw, cw)
            c1 = pl.multiple_of(Fh + c * cw, cw)
            yg, yv = conv(c0), conv(c1)
            m_ref[:, pl.ds(c0, cw)] = (yg * _cdf(yg) * yv).astype(BF16)
            return carry

        lax.fori_loop(0, Fh // cw, step, 0)

    return pl.pallas_call(
        body, name=name, out_shape=jax.ShapeDtypeStruct((T, Fh), BF16), grid=(T // tr,),
        in_specs=[_row_spec(tr, 2 * Fh), pl.BlockSpec((8, 2 * Fh), lambda i: (jnp.maximum(i * hb - 1, 0), 0)),
                  _vec_spec(2 * Fh, 3), _vec_spec(2 * Fh)],
        out_specs=_row_spec(tr, Fh), compiler_params=_params(("parallel",)),
    )(a, a, w, b)


def _conv_bwd(a, dm, w, b, name):
    T = a.shape[0]
    Fh = FFN_HIDDEN
    tr = _pick(T, (ROW_TILE,))
    cw = CONV_COLS
    hb = tr // 8
    nsteps = T // tr
    n = tr + 8

    def body(a_ref, ap_ref, an_ref, dm_ref, dmn_ref, w_ref, b_ref, da_ref, dw_ref, db_ref):
        i = pl.program_id(0)
        m0 = (i > 0).astype(F32)
        m1 = (i < nsteps - 1).astype(F32)

        @pl.when(i == 0)
        def _():
            dw_ref[...] = jnp.zeros_like(dw_ref)
            db_ref[...] = jnp.zeros_like(db_ref)

        def prep(cc):
            x = jnp.concatenate([ap_ref[:, pl.ds(cc, cw)] * m0, a_ref[:, pl.ds(cc, cw)],
                                 an_ref[:, pl.ds(cc, cw)] * m1], axis=0)
            wv = w_ref[:, pl.ds(cc, cw)]
            s1 = pltpu.roll(x, 1, axis=0)
            s2 = pltpu.roll(x, 2, axis=0)
            y = b_ref[:, pl.ds(cc, cw)] + wv[2:3] * x + wv[1:2] * s1 + wv[0:1] * s2
            return wv, x[8:], s1[8:], s2[8:], y[8:]

        def back(cc, dy, wv, x0, s1, s2):
            da = wv[2:3] * dy + wv[1:2] * pltpu.roll(dy, n - 1, axis=0) + wv[0:1] * pltpu.roll(dy, n - 2, axis=0)
            da_ref[:, pl.ds(cc, cw)] = da[:tr].astype(BF16)
            d = dy[:tr]
            db_ref[:, pl.ds(cc, cw)] += jnp.sum(d, axis=0, keepdims=True)
            dw_ref[2:3, pl.ds(cc, cw)] += jnp.sum(d * x0[:tr], axis=0, keepdims=True)
            dw_ref[1:2, pl.ds(cc, cw)] += jnp.sum(d * s1[:tr], axis=0, keepdims=True)
            dw_ref[0:1, pl.ds(cc, cw)] += jnp.sum(d * s2[:tr], axis=0, keepdims=True)

        def step(c, carry):
            c0 = pl.multiple_of(c * cw, cw)
            c1 = pl.multiple_of(Fh + c * cw, cw)
            dmx = jnp.concatenate([dm_ref[:, pl.ds(c0, cw)], dmn_ref[:, pl.ds(c0, cw)] * m1], axis=0)
            wg, xg, s1g, s2g, yg = prep(c0)
            wv, xv, s1v, s2v, yv = prep(c1)
            cg = _cdf(yg)
            back(c0, dmx * yv * (cg + yg * _pdf(yg)), wg, xg, s1g, s2g)
            back(c1, dmx * (yg * cg), wv, xv, s1v, s2v)
            return carry

        lax.fori_loop(0, Fh // cw, step, 0)

    prev = lambda wd: pl.BlockSpec((8, wd), lambda i: (jnp.maximum(i * hb - 1, 0), 0))
    nxt = lambda wd: pl.BlockSpec((8, wd), lambda i: (jnp.minimum((i + 1) * hb, T // 8 - 1), 0))
    return pl.pallas_call(
        body, name=name, grid=(nsteps,),
        out_shape=(jax.ShapeDtypeStruct((T, 2 * Fh), BF16), jax.ShapeDtypeStruct((3, 2 * Fh), F32),
                   jax.ShapeDtypeStruct((1, 2 * Fh), F32)),
        in_specs=[_row_spec(tr, 2 * Fh), prev(2 * Fh), nxt(2 * Fh), _row_spec(tr, Fh), nxt(Fh),
                  _vec_spec(2 * Fh, 3), _vec_spec(2 * Fh)],
        out_specs=(_row_spec(tr, 2 * Fh), _vec_spec(2 * Fh, 3), _vec_spec(2 * Fh)),
        compiler_params=_params(("arbitrary",)),
    )(a, a, a, dm, dm, w, b)


def _ada_fwd(c_all, ada_w, ada_b, name):
    L, Dm, cols = ada_w.shape
    tn = _pick(cols, (512, 256, 128))

    def body(c_ref, w_ref, b_ref, o_ref):
        cv = c_ref[...]
        cond = (cv * _sig(cv)).astype(BF16)
        o_ref[...] = _dot(cond, w_ref[...].astype(BF16), NN) + b_ref[...]

    return pl.pallas_call(
        body, name=name, out_shape=jax.ShapeDtypeStruct((L, N_DEV, cols), F32), grid=(L, cols // tn),
        in_specs=[pl.BlockSpec((N_DEV, Dm), lambda l, j: (0, 0)), pl.BlockSpec((None, Dm, tn), lambda l, j: (l, 0, j)),
                  pl.BlockSpec((None, 1, tn), lambda l, j: (l, 0, j))],
        out_specs=pl.BlockSpec((None, N_DEV, tn), lambda l, j: (l, 0, j)),
        compiler_params=_params(("parallel", "parallel")),
    )(c_all, ada_w, ada_b.reshape(L, 1, cols))


def _ada_bwd(c_all, dmod, name):
    L, _, cols = dmod.shape
    Dm = c_all.shape[1]
    tn = _pick(cols, (512, 256, 128))

    def body(c_ref, d_ref, o_ref):
        cv = c_ref[...]
        o_ref[...] = _dot(cv * _sig(cv), d_ref[...], TN, HI)

    return pl.pallas_call(
        body, name=name, out_shape=jax.ShapeDtypeStruct((L, Dm, cols), F32), grid=(L, cols // tn),
        in_specs=[pl.BlockSpec((N_DEV, Dm), lambda l, j: (0, 0)), pl.BlockSpec((None, N_DEV, tn), lambda l, j: (l, 0, j))],
        out_specs=pl.BlockSpec((None, Dm, tn), lambda l, j: (l, 0, j)),
        compiler_params=_params(("parallel", "parallel")),
    )(c_all, dmod)


def _add_own_half(g4, rb, core, name):
    S, _, rh, cw = g4.shape
    tr = _pick(rh, (256, 128, 176, 64))

    def body(core_ref, g_ref, r_ref, o_ref):
        o_ref[...] = (g_ref[...] + r_ref[...].astype(F32)).astype(BF16)

    return pl.pallas_call(
        body, name=name, out_shape=jax.ShapeDtypeStruct((S, rh, cw), BF16),
        grid_spec=pltpu.PrefetchScalarGridSpec(
            num_scalar_prefetch=1, grid=(S, rh // tr),
            in_specs=[pl.BlockSpec((None, None, tr, cw), lambda s, i, core_ref: (s, core_ref[0], i, 0)),
                      pl.BlockSpec((None, tr, cw), lambda s, i, core_ref: (s, i, 0))],
            out_specs=pl.BlockSpec((None, tr, cw), lambda s, i, core_ref: (s, i, 0))),
        compiler_params=_params(("parallel", "parallel")),
    )(core, g4, rb)


def _sum_chips(lands, sums, chip, name):
    L = len(lands)
    _, rh, cw = lands[0].shape
    tr = _pick(rh, (256, 128, 176, 64))

    def body(chip_ref, *refs):
        ld, cs, o_ref = refs[:L], refs[L:2 * L], refs[2 * L]
        me = chip_ref[0]
        for k in range(L):
            @pl.when(pl.program_id(0) == k)
            def _(k=k):
                own = cs[k][...].astype(F32)
                got = [ld[k][j].astype(F32) for j in range(3)]
                acc = None
                for t in range(N_CHIPS):
                    d = jnp.bitwise_xor(jnp.int32(t), me)
                    term = jnp.where(d == 0, own, jnp.where(d == 2, got[0], jnp.where(d == 1, got[1], got[2])))
                    acc = term if acc is None else acc + term
                o_ref[...] = acc

    frozen = lambda l, i, k: jnp.where(l == k, i, 0)
    in_specs = [pl.BlockSpec((3, tr, cw), lambda l, i, chip_ref, k=k: (0, frozen(l, i, k), 0)) for k in range(L)]
    in_specs += [pl.BlockSpec((None, tr, cw), lambda l, i, chip_ref, k=k: (chip_ref[0], frozen(l, i, k), 0))
                 for k in range(L)]
    return pl.pallas_call(
        body, name=name, out_shape=jax.ShapeDtypeStruct((L, rh, cw), F32),
        grid_spec=pltpu.PrefetchScalarGridSpec(
            num_scalar_prefetch=1, grid=(L, rh // tr), in_specs=in_specs,
            out_specs=pl.BlockSpec((None, tr, cw), lambda l, i, chip_ref: (l, i, 0))),
        compiler_params=_params(("arbitrary", "arbitrary")),
    )(chip, *lands, *sums)


def _sum_devices(gathered, name):
    n, R, _ = gathered.shape
    tr = _pick(R, (512, 256, 128, 64, 32, 16, 8))

    def body(g_ref, o_ref):
        acc = g_ref[0]
        for d in range(1, n):
            acc = acc + g_ref[d]
        o_ref[...] = acc

    return pl.pallas_call(
        body, name=name, out_shape=jax.ShapeDtypeStruct((R, LANES), F32), grid=(R // tr,),
        in_specs=[pl.BlockSpec((n, tr, LANES), lambda i: (0, i, 0))], out_specs=pl.BlockSpec((tr, LANES), lambda i: (i, 0)),
        compiler_params=_params(("parallel",)),
    )(gathered)


def _adamw(w, g, m, v, name):
    R, C = w.shape
    tr = _pick(R, (256, 128, 64, 32, 16, 8))
    c1 = 1.0 / (1.0 - ADAM_B1 ** ADAM_STEP)
    c2 = 1.0 / (1.0 - ADAM_B2 ** ADAM_STEP)

    def body(w_ref, g_ref, m_ref, v_ref, d_ref, mo_ref, vo_ref):
        gv = g_ref[...]
        m2 = ADAM_B1 * m_ref[...] + (1.0 - ADAM_B1) * gv
        v2 = ADAM_B2 * v_ref[...] + (1.0 - ADAM_B2) * (gv * gv)
        mo_ref[...] = m2
        vo_ref[...] = v2
        d_ref[...] = -ADAM_LR * ((m2 * c1) / (jnp.sqrt(v2 * c2) + ADAM_EPS) + ADAM_WD * w_ref[...])

    spec = pl.BlockSpec((tr, C), lambda i: (i, 0))
    shp = jax.ShapeDtypeStruct((R, C), F32)
    return pl.pallas_call(body, name=name, out_shape=(shp, shp, shp), grid=(R // tr,), in_specs=[spec] * 4,
                          out_specs=(spec, spec, spec), compiler_params=_params(("parallel",)))(w, g, m, v)


def _adamw_halves(w, own, recv, m, v, core, name):
    L, rh, cw = own.shape
    tr = _pick(rh, (256, 128, 176, 64))
    c1 = 1.0 / (1.0 - ADAM_B1 ** ADAM_STEP)
    c2 = 1.0 / (1.0 - ADAM_B2 ** ADAM_STEP)

    def body(core_ref, w_ref, own_ref, recv_ref, m_ref, v_ref, g_ref, d_ref, mo_ref, vo_ref):
        gv = jnp.where(pl.program_id(1) == core_ref[0], own_ref[...], recv_ref[...])
        g_ref[...] = gv
        m2 = ADAM_B1 * m_ref[...] + (1.0 - ADAM_B1) * gv
        v2 = ADAM_B2 * v_ref[...] + (1.0 - ADAM_B2) * (gv * gv)
        mo_ref[...] = m2
        vo_ref[...] = v2
        d_ref[...] = -ADAM_LR * ((m2 * c1) / (jnp.sqrt(v2 * c2) + ADAM_EPS) + ADAM_WD * w_ref[...])

    full = pl.BlockSpec((None, None, tr, cw), lambda l, hf, i, core_ref: (l, hf, i, 0))
    mine = pl.BlockSpec((None, tr, cw), lambda l, hf, i, core_ref: (l, jnp.where(hf == core_ref[0], i, 0), 0))
    other = pl.BlockSpec((None, tr, cw), lambda l, hf, i, core_ref: (l, jnp.where(hf == core_ref[0], 0, i), 0))
    shp = jax.ShapeDtypeStruct((L, 2, rh, cw), F32)
    view = lambda a: a.reshape(L, 2, rh, cw)
    outs = pl.pallas_call(
        body, name=name, out_shape=(shp, shp, shp, shp),
        grid_spec=pltpu.PrefetchScalarGridSpec(
            num_scalar_prefetch=1, grid=(L, 2, rh // tr), in_specs=[full, mine, other, full, full],
            out_specs=(full, full, full, full)),
        compiler_params=_params(("arbitrary", "arbitrary", "arbitrary")),
    )(core, view(w), own, recv, view(m), view(v))
    return tuple(o.reshape(L, 2 * rh, cw) for o in outs)


ANY = pl.BlockSpec(memory_space=pl.ANY)


def _position():
    x, y, c = lax.axis_index("x"), lax.axis_index("y"), lax.axis_index("c")
    return x, y, c


def _allgather(ins, out_shapes, src_fns, dst_fns, name, in_vmem):
    n = len(ins)

    def body(*refs):
        in_refs, out_refs = refs[:n], refs[n:2 * n]
        send_sems, recv_sems, local_sems = refs[2 * n:]
        x, y, c = _position()
        me, sibling = (x, y, c), (x, y, 1 - c)
        chips = [(1 - x, y), (x, 1 - y), (1 - x, 1 - y)]

        def copy(k, j, block, to, own=False):
            dst = dst_fns[k](out_refs[k], *block)
            return pltpu.make_async_remote_copy(
                src_ref=src_fns[k](in_refs[k], c) if own else dst, dst_ref=dst,
                send_sem=send_sems.at[k, j], recv_sem=recv_sems.at[k, j], device_id=to, device_id_type=MESH)

        mine = [pltpu.make_async_copy(src_fns[k](in_refs[k], c), dst_fns[k](out_refs[k], *me), local_sems.at[k])
                for k in range(n)]
        for cp in mine:
            cp.start()
        first = []
        for k in range(n):
            first.append(copy(k, 0, me, sibling, own=True))
            first += [copy(k, 1 + j, me, (*chip, c), own=True) for j, chip in enumerate(chips)]
        for cp in first:
            cp.start()
        passed = []
        for j, chip in enumerate(chips):
            for k in range(n):
                copy(k, 1 + j, (*chip, c), me).wait_recv()
                fwd = copy(k, 4 + j, (*chip, c), sibling)
                fwd.start()
                passed.append(fwd)
        for k in range(n):
            copy(k, 0, sibling, me).wait_recv()
        for j, chip in enumerate(chips):
            for k in range(n):
                copy(k, 4 + j, (*chip, 1 - c), me).wait_recv()
        for cp in first + passed:
            cp.wait_send()
        for cp in mine:
            cp.wait()

    spec = pl.BlockSpec(memory_space=pltpu.VMEM) if in_vmem else ANY
    return pl.pallas_call(
        body, name=name, out_shape=tuple(out_shapes), in_specs=[spec] * n, out_specs=tuple([spec] * n),
        scratch_shapes=[pltpu.SemaphoreType.DMA((n, 7)), pltpu.SemaphoreType.DMA((n, 7)),
                        pltpu.SemaphoreType.DMA((n,))],
        compiler_params=pltpu.CompilerParams(vmem_limit_bytes=VMEM_LIMIT_BYTES),
    )(*ins)


def _allgather_small(payload, name):
    R = payload.shape[0]
    (out,) = _allgather(
        [payload], [jax.ShapeDtypeStruct((N_DEV, R, LANES), F32)],
        [lambda ref, c: ref], [lambda ref, px, py, pc: ref.at[4 * px + 2 * py + pc]], name, in_vmem=True)
    return out


def _swap_sibling(ins, name):
    n = len(ins)

    def body(*refs):
        in_refs, out_refs = refs[:n], refs[n:2 * n]
        send_sems, recv_sems = refs[2 * n:]
        x, y, c = _position()
        copies = [pltpu.make_async_remote_copy(
            src_ref=in_refs[k], dst_ref=out_refs[k], send_sem=send_sems.at[k], recv_sem=recv_sems.at[k],
            device_id=(x, y, 1 - c), device_id_type=MESH) for k in range(n)]
        for cp in copies:
            cp.start()
        for cp in copies:
            cp.wait_recv()
        for cp in copies:
            cp.wait_send()

    return pl.pallas_call(
        body, name=name, out_shape=tuple(jax.ShapeDtypeStruct(a.shape, a.dtype) for a in ins),
        in_specs=[ANY] * n, out_specs=tuple([ANY] * n),
        scratch_shapes=[pltpu.SemaphoreType.DMA((n,)), pltpu.SemaphoreType.DMA((n,))],
        compiler_params=pltpu.CompilerParams(vmem_limit_bytes=VMEM_LIMIT_BYTES),
    )(*ins)


HBM_SPEC = pl.BlockSpec(memory_space=pltpu.HBM)
SEM_SPEC = pl.BlockSpec(memory_space=pltpu.SEMAPHORE)
SPLIT_PARAMS = pltpu.CompilerParams(has_side_effects=pltpu.SideEffectType.DATAFLOW_SIDE_EFFECTING)
TOKEN = jax.ShapeDtypeStruct((8, LANES), F32)


def _hbm(a):
    return pltpu.with_memory_space_constraint(a, pltpu.HBM)


def _weight_window(ref, col, r, cw, px, py, pc):
    rh = r // 2
    if col:
        return ref.at[pl.ds(pc * rh, rh), pl.ds((2 * px + py) * cw, cw)]
    return ref.at[pl.ds((2 * px + py) * r + pc * rh, rh), :]


def _peers(x, y, c):
    return [(x, y, 1 - c), (1 - x, y, c), (x, 1 - y, c), (1 - x, 1 - y, c)]


def _place_own(shard, col, pos, name):
    r, cw = shard.shape
    rh = r // 2
    tr = _pick(rh, (256, 128, 176, 64))
    nb = rh // tr
    shape = (r, N_CHIPS * cw) if col else (N_CHIPS * r, cw)

    def body(pos_ref, x_ref, o_ref):
        o_ref[...] = x_ref[...]

    if col:
        out_idx = lambda i, pos_ref: (pos_ref[1] * nb + i, pos_ref[0])
    else:
        out_idx = lambda i, pos_ref: (pos_ref[0] * (2 * nb) + pos_ref[1] * nb + i, 0)
    return pl.pallas_call(
        body, name=name, out_shape=jax.ShapeDtypeStruct(shape, shard.dtype),
        grid_spec=pltpu.PrefetchScalarGridSpec(
            num_scalar_prefetch=1, grid=(nb,),
            in_specs=[pl.BlockSpec((tr, cw), lambda i, pos_ref: (pos_ref[1] * nb + i, 0))],
            out_specs=pl.BlockSpec((tr, cw), out_idx)),
        compiler_params=_params(("arbitrary",)),
    )(pos, shard)


def _gather_start(shards, lands, cols, per_layer, name):
    n = len(shards)
    nl = n // per_layer

    def body(*refs):
        sh, ld = refs[:n], refs[n:2 * n]
        sems, token = refs[2 * n:2 * n + 2 * nl], refs[-1]
        x, y, c = _position()
        for k in range(n):
            l, a = divmod(k, per_layer)
            r, cw = shards[k].shape
            src = sh[k].at[pl.ds(c * (r // 2), r // 2), :]
            dst = _weight_window(ld[k], cols[k], r, cw, x, y, c)
            for j, peer in enumerate(_peers(x, y, c)):
                pltpu.make_async_remote_copy(src_ref=src, dst_ref=dst, send_sem=sems[2 * l].at[4 * a + j],
                                             recv_sem=sems[2 * l + 1].at[4 * a + j], device_id=peer,
                                             device_id_type=MESH).start()
        token[...] = jnp.zeros_like(token)

    arrs = list(shards) + list(lands)
    out = pl.pallas_call(
        body, name=name,
        out_shape=tuple(pltpu.SemaphoreType.DMA((per_layer * 4,)) for _ in range(2 * nl))
        + tuple(pltpu.HBM(a.shape, a.dtype) for a in arrs) + (TOKEN,),
        in_specs=[HBM_SPEC] * (2 * n),
        out_specs=(SEM_SPEC,) * (2 * nl) + (HBM_SPEC,) * (2 * n) + (pl.BlockSpec(memory_space=pltpu.VMEM),),
        input_output_aliases={i: 2 * nl + i for i in range(2 * n)}, compiler_params=SPLIT_PARAMS,
    )(*[_hbm(a) for a in arrs])
    return out[:2 * nl], out[2 * nl:2 * nl + n], out[2 * nl + n:2 * nl + 2 * n], out[-1]


def _gather_wait(shards, lands, send, recv, after, cols, name):
    m = len(shards)

    def body(*refs):
        sh, ld = refs[:m], refs[m:2 * m]
        send_ref, recv_ref = refs[2 * m], refs[2 * m + 1]
        x, y, c = _position()
        for a in range(m):
            r, cw = shards[a].shape
            src = sh[a].at[pl.ds(c * (r // 2), r // 2), :]
            for j, (px, py, pc) in enumerate(_peers(x, y, c)):
                cp = pltpu.make_async_remote_copy(
                    src_ref=src, dst_ref=_weight_window(ld[a], cols[a], r, cw, px, py, pc),
                    send_sem=send_ref.at[4 * a + j], recv_sem=recv_ref.at[4 * a + j], device_id=(px, py, pc),
                    device_id_type=MESH)
                cp.wait_send()
                cp.wait_recv()

    arrs = list(shards) + list(lands)
    out = pl.pallas_call(
        body, name=name, out_shape=tuple(pltpu.HBM(a.shape, a.dtype) for a in arrs),
        in_specs=[HBM_SPEC] * (2 * m) + [SEM_SPEC, SEM_SPEC, ANY], out_specs=(HBM_SPEC,) * (2 * m),
        input_output_aliases={i: i for i in range(2 * m)}, compiler_params=SPLIT_PARAMS,
    )(*arrs, send, recv, after)
    return out[m:]


def _forward_sibling(lands, cols, shard_shapes, name):
    m = len(lands)

    def body(*refs):
        ins, outs = refs[:m], refs[m:2 * m]
        send_sems, recv_sems = refs[2 * m:]
        x, y, c = _position()
        chips = [(1 - x, y), (x, 1 - y), (1 - x, 1 - y)]
        sends = []
        for a in range(m):
            r, cw = shard_shapes[a]
            for j, (px, py) in enumerate(chips):
                cp = pltpu.make_async_remote_copy(
                    src_ref=_weight_window(ins[a], cols[a], r, cw, px, py, c),
                    dst_ref=_weight_window(outs[a], cols[a], r, cw, px, py, c),
                    send_sem=send_sems.at[a, j], recv_sem=recv_sems.at[a, j], device_id=(x, y, 1 - c),
                    device_id_type=MESH)
                cp.start()
                sends.append(cp)
        for a in range(m):
            r, cw = shard_shapes[a]
            for j, (px, py) in enumerate(chips):
                pltpu.make_async_remote_copy(
                    src_ref=_weight_window(ins[a], cols[a], r, cw, px, py, c),
                    dst_ref=_weight_window(outs[a], cols[a], r, cw, px, py, 1 - c),
                    send_sem=send_sems.at[a, j], recv_sem=recv_sems.at[a, j], device_id=(x, y, 1 - c),
                    device_id_type=MESH).wait_recv()
        for cp in sends:
            cp.wait_send()

    return pl.pallas_call(
        body, name=name, out_shape=tuple(jax.ShapeDtypeStruct(a.shape, a.dtype) for a in lands),
        in_specs=[ANY] * m, out_specs=tuple([ANY] * m), input_output_aliases={i: i for i in range(m)},
        scratch_shapes=[pltpu.SemaphoreType.DMA((m, 3)), pltpu.SemaphoreType.DMA((m, 3))],
        compiler_params=pltpu.CompilerParams(vmem_limit_bytes=VMEM_LIMIT_BYTES),
    )(*lands)


def _exchange_start(sums, name):
    m = len(sums)
    lands = [lax.empty((3,) + s.shape[1:], s.dtype) for s in sums]

    def body(*refs):
        cs, ld = refs[:m], refs[m:2 * m]
        send_ref, recv_ref, token = refs[2 * m], refs[2 * m + 1], refs[-1]
        x, y, c = _position()
        for a in range(m):
            for j, (px, py) in enumerate([(1 - x, y), (x, 1 - y), (1 - x, 1 - y)]):
                pltpu.make_async_remote_copy(
                    src_ref=cs[a].at[2 * px + py], dst_ref=ld[a].at[j], send_sem=send_ref.at[3 * a + j],
                    recv_sem=recv_ref.at[3 * a + j], device_id=(px, py, c), device_id_type=MESH).start()
        token[...] = jnp.zeros_like(token)

    arrs = list(sums) + lands
    out = pl.pallas_call(
        body, name=name,
        out_shape=(pltpu.SemaphoreType.DMA((m * 3,)), pltpu.SemaphoreType.DMA((m * 3,)))
        + tuple(pltpu.HBM(a.shape, a.dtype) for a in arrs) + (TOKEN,),
        in_specs=[HBM_SPEC] * (2 * m),
        out_specs=(SEM_SPEC, SEM_SPEC) + (HBM_SPEC,) * (2 * m) + (pl.BlockSpec(memory_space=pltpu.VMEM),),
        input_output_aliases={i: 2 + i for i in range(2 * m)}, compiler_params=SPLIT_PARAMS,
    )(*[_hbm(a) for a in arrs])
    return out[0], out[1], out[2:2 + m], out[2 + m:2 + 2 * m], out[-1]


def _exchange_wait(sums, lands, send, recv, after, name):
    m = len(sums)

    def body(*refs):
        cs, ld = refs[:m], refs[m:2 * m]
        send_ref, recv_ref = refs[2 * m], refs[2 * m + 1]
        x, y, c = _position()
        for a in range(m):
            for j, (px, py) in enumerate([(1 - x, y), (x, 1 - y), (1 - x, 1 - y)]):
                cp = pltpu.make_async_remote_copy(
                    src_ref=cs[a].at[2 * px + py], dst_ref=ld[a].at[j], send_sem=send_ref.at[3 * a + j],
                    recv_sem=recv_ref.at[3 * a + j], device_id=(px, py, c), device_id_type=MESH)
                cp.wait_send()
                cp.wait_recv()

    arrs = list(sums) + list(lands)
    out = pl.pallas_call(
        body, name=name, out_shape=tuple(pltpu.HBM(a.shape, a.dtype) for a in arrs),
        in_specs=[HBM_SPEC] * (2 * m) + [SEM_SPEC, SEM_SPEC, ANY], out_specs=(HBM_SPEC,) * (2 * m),
        input_output_aliases={i: i for i in range(2 * m)}, compiler_params=SPLIT_PARAMS,
    )(*arrs, send, recv, after)
    return out[:m], out[m:]


def _place_row(payload, dev, name):
    R = payload.shape[0]
    tr = _pick(R, (512, 256, 128, 64, 32, 16, 8))

    def body(dev_ref, x_ref, o_ref):
        o_ref[...] = x_ref[...]

    return pl.pallas_call(
        body, name=name, out_shape=jax.ShapeDtypeStruct((N_DEV, R, LANES), payload.dtype),
        grid_spec=pltpu.PrefetchScalarGridSpec(
            num_scalar_prefetch=1, grid=(R // tr,),
            in_specs=[pl.BlockSpec((tr, LANES), lambda i, dev_ref: (i, 0))],
            out_specs=pl.BlockSpec((None, tr, LANES), lambda i, dev_ref: (dev_ref[0], i, 0))),
        compiler_params=_params(("arbitrary",)),
    )(dev, payload)


def _others(x, y, c):
    return [(1 - x if fx else x, 1 - y if fy else y, 1 - c if fc else c)
            for fx in (0, 1) for fy in (0, 1) for fc in (0, 1) if fx or fy or fc]


def _broadcast_start(payload, land, name):
    def body(p_ref, l_ref, send_ref, recv_ref, p_thru, l_thru, token):
        x, y, c = _position()
        for j, peer in enumerate(_others(x, y, c)):
            pltpu.make_async_remote_copy(src_ref=p_ref, dst_ref=l_ref.at[4 * x + 2 * y + c], send_sem=send_ref.at[j],
                                         recv_sem=recv_ref.at[j], device_id=peer, device_id_type=MESH).start()
        token[...] = jnp.zeros_like(token)

    n = N_DEV - 1
    return pl.pallas_call(
        body, name=name,
        out_shape=(pltpu.SemaphoreType.DMA((n,)), pltpu.SemaphoreType.DMA((n,)), pltpu.HBM(payload.shape, payload.dtype),
                   pltpu.HBM(land.shape, land.dtype), TOKEN),
        in_specs=[HBM_SPEC, HBM_SPEC],
        out_specs=(SEM_SPEC, SEM_SPEC, HBM_SPEC, HBM_SPEC, pl.BlockSpec(memory_space=pltpu.VMEM)),
        input_output_aliases={0: 2, 1: 3}, compiler_params=SPLIT_PARAMS,
    )(_hbm(payload), _hbm(land))


def _broadcast_wait(payload, land, send, recv, after, name):
    def body(p_ref, l_ref, send_ref, recv_ref, after_ref, p_thru, l_thru):
        x, y, c = _position()
        for j, (px, py, pc) in enumerate(_others(x, y, c)):
            cp = pltpu.make_async_remote_copy(src_ref=p_ref, dst_ref=l_ref.at[4 * px + 2 * py + pc],
                                              send_sem=send_ref.at[j], recv_sem=recv_ref.at[j],
                                              device_id=(px, py, pc), device_id_type=MESH)
            cp.wait_send()
            cp.wait_recv()

    out = pl.pallas_call(
        body, name=name, out_shape=(pltpu.HBM(payload.shape, payload.dtype), pltpu.HBM(land.shape, land.dtype)),
        in_specs=[HBM_SPEC, HBM_SPEC, SEM_SPEC, SEM_SPEC, ANY], out_specs=(HBM_SPEC, HBM_SPEC),
        input_output_aliases={0: 0, 1: 1}, compiler_params=SPLIT_PARAMS,
    )(payload, land, send, recv, after)
    return out[1]


def _vec(a):
    return a.reshape(1, -1)


def _local_step(x, tgt, mod, W, P, get_w=None, on_grads=None):
    Dm = D_MODEL
    G = {k: [] for k in ("gm_w_in", "gm_w_out", "hg_w_in", "hg_w_out", "ffn_w_up", "ffn_w_down")}
    lb_all = _lb_fwd(P["hg_lb"], "lb_fwd")
    saved = []
    xs = x
    y_prev = gate_prev = None
    layer_w = [None] * DEPTH

    def wmm(xa, kind, i, mode, name):
        if layer_w[i] is not None:
            return _mm(xa, layer_w[i][kind], mode, name)
        return _mm(xa, W[kind], mode, name, b_layer=i if kind.startswith("ffn") else i // 2)

    for i in range(DEPTH):
        m = [_vec(mod[i, j * Dm:(j + 1) * Dm]) for j in range(6)]
        sh1, sc1, g1, sh2, sc2, g2 = m
        j = i // 2
        if get_w is not None:
            layer_w[i] = get_w(i, xs if y_prev is None else y_prev)
        xs, h = _norm_fwd(xs, y_prev, gate_prev, _vec(P["norm_g"][i, 0]), sc1, sh1, f"norm_fwd_a{i}")
        rec = dict(x1=xs, h1=h)
        if i % 2 == 0:
            a = wmm(h, "gm_w_in", i, "nn", f"gm_in{i}")
            gated = _gm_mid_fwd(a, _vec(P["gm_ln_g"][j]), _vec(P["gm_ln_b"][j]), P["gm_w_s"][j],
                                P["gm_b_s"][j].reshape(GM_HEADS, GM_BLOCK, 1), f"gm_mid_fwd{i}")
            y1 = wmm(gated, "gm_w_out", i, "nn", f"gm_out{i}")
            rec.update(a=a, act=gated)
        else:
            p = wmm(h, "hg_w_in", i, "nn", f"hg_in{i}")
            o, og, states = _hg_scan_fwd(p, _vec(lb_all[j]), _vec(P["hg_gn_g"][j]), f"hg_scan_fwd{i}")
            y1 = wmm(og, "hg_w_out", i, "nn", f"hg_out{i}")
            rec.update(a=p, act=og, o=o, states=states)
        rec["y1"] = y1
        xs, h2 = _norm_fwd(xs, y1, g1, _vec(P["norm_g"][i, 1]), sc2, sh2, f"norm_fwd_b{i}")
        a2 = wmm(h2, "ffn_w_up", i, "nn", f"ffn_up{i}")
        mm_ = _conv_fwd(a2, P["ffn_conv_w"][i], _vec(P["ffn_conv_b"][i]), f"conv_fwd{i}")
        y2 = wmm(mm_, "ffn_w_down", i, "nn", f"ffn_down{i}")
        rec.update(x2=xs, h2=h2, a2=a2, m=mm_, y2=y2, mods=m)
        saved.append(rec)
        y_prev, gate_prev = y2, g2
    dx, dy, loss, s_fg, s_gate = _loss_head(xs, y_prev, gate_prev, _vec(P["final_g"]), tgt, "loss_head")
    small = dict(final_g=s_fg, norm_g=[None] * DEPTH, dmod=[None] * DEPTH, ffn_conv_w=[None] * DEPTH,
                 ffn_conv_b=[None] * DEPTH, gm_ln_g=[None] * 2, gm_ln_b=[None] * 2, gm_w_s=[None] * 2,
                 gm_b_s=[None] * 2, hg_gn_g=[None] * 2, dlb=[None] * 2)
    for i in reversed(range(DEPTH)):
        rec = saved[i]
        sh1, sc1, g1, sh2, sc2, g2 = rec["mods"]
        j = i // 2
        d_g2 = s_gate
        dm = wmm(dy, "ffn_w_down", i, "nt", f"ffn_down_dx{i}")
        G["ffn_w_down"].append(_mm(rec["m"], dy, "tn", f"ffn_down_dw{i}"))
        da2, dcw, dcb = _conv_bwd(rec["a2"], dm, P["ffn_conv_w"][i], _vec(P["ffn_conv_b"][i]), f"conv_bwd{i}")
        small["ffn_conv_w"][i], small["ffn_conv_b"][i] = dcw, dcb
        dh2 = wmm(da2, "ffn_w_up", i, "nt", f"ffn_up_dx{i}")
        G["ffn_w_up"].append(_mm(rec["h2"], da2, "tn", f"ffn_up_dw{i}", exchange_out=True))
        ng2 = _vec(P["norm_g"][i, 1])
        if on_grads is not None:
            ng2 = ng2 + on_grads(i, {k: G[k][-1] for k in ("ffn_w_up", "ffn_w_down")})
        dx, dy, s_sh2, s_x2, d_g1 = _norm_bwd(rec["x2"], dh2, dx, ng2, sc2, rec["y1"], g1, f"norm_bwd_b{i}")
        d_sc2, d_ng2 = s_x2 * ng2, s_x2 * (1.0 + sc2)
        if i % 2 == 0:
            dgated = wmm(dy, "gm_w_out", i, "nt", f"gm_out_dx{i}")
            G["gm_w_out"].append(_mm(rec["act"], dy, "tn", f"gm_out_dw{i}"))
            da, dws, dbs, dlg, dlbeta = _gm_mid_bwd(
                rec["a"], dgated, _vec(P["gm_ln_g"][j]), _vec(P["gm_ln_b"][j]), P["gm_w_s"][j],
                P["gm_b_s"][j].reshape(GM_HEADS, GM_BLOCK, 1), f"gm_mid_bwd{i}")
            small["gm_w_s"][j], small["gm_b_s"][j] = dws, dbs[:, :GM_HEADS].T
            small["gm_ln_g"][j], small["gm_ln_b"][j] = dlg, dlbeta
            dh1 = wmm(da, "gm_w_in", i, "nt", f"gm_in_dx{i}")
            G["gm_w_in"].append(_mm(rec["h1"], da, "tn", f"gm_in_dw{i}", exchange_out=True))
        else:
            dog = wmm(dy, "hg_w_out", i, "nt", f"hg_out_dx{i}")
            G["hg_w_out"].append(_mm(rec["act"], dy, "tn", f"hg_out_dw{i}"))
            dp, dlb, dgn = _hg_scan_bwd(rec["a"], _vec(lb_all[j]), _vec(P["hg_gn_g"][j]), rec["o"], dog,
                                        rec["states"], f"hg_scan_bwd{i}")
            small["dlb"][j], small["hg_gn_g"][j] = dlb, dgn
            dh1 = wmm(dp, "hg_w_in", i, "nt", f"hg_in_dx{i}")
            G["hg_w_in"].append(_mm(rec["h1"], dp, "tn", f"hg_in_dw{i}", exchange_out=True))
        ng1 = _vec(P["norm_g"][i, 0])
        if on_grads is not None:
            mixer = ("gm_w_in", "gm_w_out") if i % 2 == 0 else ("hg_w_in", "hg_w_out")
            ng1 = ng1 + on_grads(i, {k: G[k][-1] for k in mixer})
        if i > 0:
            prev = saved[i - 1]
            dx, dy, s_sh1, s_x1, s_gate = _norm_bwd(rec["x1"], dh1, dx, ng1, sc1, prev["y2"], prev["mods"][5],
                                                    f"norm_bwd_a{i}")
        else:
            dx, s_sh1, s_x1 = _norm_bwd(rec["x1"], dh1, dx, ng1, sc1, None, None, f"norm_bwd_a{i}")
        d_sc1, d_ng1 = s_x1 * ng1, s_x1 * (1.0 + sc1)
        small["norm_g"][i] = jnp.concatenate([d_ng1, d_ng2], axis=0)
        small["dmod"][i] = jnp.concatenate([s_sh1, d_sc1, d_g1, s_sh2, d_sc2, d_g2], axis=1)
    for k in G:
        G[k] = G[k][::-1]
    dlb_all = jnp.concatenate(small.pop("dlb"), axis=0)
    small["hg_lb"] = _lb_bwd(P["hg_lb"], dlb_all, "lb_bwd")
    return loss, dx, G, small


BIG = ("gm_w_in", "gm_w_out", "hg_w_in", "hg_w_out", "ffn_w_up", "ffn_w_down")
COL_SHARDED = dict(gm_w_in=True, gm_w_out=False, hg_w_in=True, hg_w_out=False, ffn_w_up=True, ffn_w_down=False)
LAYER_WEIGHTS = 4


def _layer_kinds(i):
    return (("gm_w_in", "gm_w_out") if i % 2 == 0 else ("hg_w_in", "hg_w_out")) + ("ffn_w_up", "ffn_w_down")


def _pack(pieces):
    flat = [p.reshape(-1).astype(F32) for p in pieces]
    offs, tot = [], 0
    for f in flat:
        offs.append((tot, f.shape[0]))
        tot += f.shape[0]
    padded = -(-tot // (8 * LANES)) * (8 * LANES)
    if padded > tot:
        flat.append(jnp.zeros((padded - tot,), F32))
    return jnp.concatenate(flat).reshape(-1, LANES), offs


def _unpack(rows, offs, shapes):
    lead = rows.shape[:-2]
    flat = rows.reshape(lead + (-1,))
    return [flat[..., o:o + n].reshape(lead + tuple(s)) for (o, n), s in zip(offs, shapes)]


def _from_chips(per_dev, axis):
    per_chip = per_dev[0::2]
    return jnp.concatenate([per_chip[s] for s in range(N_CHIPS)], axis=axis)


def kernel(x, c, gm_w_in, gm_ln_g, gm_ln_b, gm_w_s, gm_b_s, gm_w_out, hg_w_in, hg_lb, hg_gn_g, hg_w_out, ffn_w_up, ffn_conv_w, ffn_conv_b, ffn_w_down, norm_g, ada_w, ada_b, final_g, loss_target, m_gm_w_in, m_gm_ln_g, m_gm_ln_b, m_gm_w_s, m_gm_b_s, m_gm_w_out, m_hg_w_in, m_hg_lb, m_hg_gn_g, m_hg_w_out, m_ffn_w_up, m_ffn_conv_w, m_ffn_conv_b, m_ffn_w_down, m_norm_g, m_ada_w, m_ada_b, m_final_g, v_gm_w_in, v_gm_ln_g, v_gm_ln_b, v_gm_w_s, v_gm_b_s, v_gm_w_out, v_hg_w_in, v_hg_lb, v_hg_gn_g, v_hg_w_out, v_ffn_w_up, v_ffn_conv_w, v_ffn_conv_b, v_ffn_w_down, v_norm_g, v_ada_w, v_ada_b, v_final_g):
    Dm = D_MODEL
    xi, yi, ci = _position()
    chip = 2 * xi + yi
    dev = 4 * xi + 2 * yi + ci
    weights = dict(gm_w_in=gm_w_in, gm_ln_g=gm_ln_g, gm_ln_b=gm_ln_b, gm_w_s=gm_w_s, gm_b_s=gm_b_s,
                   gm_w_out=gm_w_out, hg_w_in=hg_w_in, hg_lb=hg_lb, hg_gn_g=hg_gn_g, hg_w_out=hg_w_out,
                   ffn_w_up=ffn_w_up, ffn_conv_w=ffn_conv_w, ffn_conv_b=ffn_conv_b, ffn_w_down=ffn_w_down,
                   norm_g=norm_g, ada_w=ada_w, ada_b=ada_b, final_g=final_g)
    mom_m = dict(gm_w_in=m_gm_w_in, gm_ln_g=m_gm_ln_g, gm_ln_b=m_gm_ln_b, gm_w_s=m_gm_w_s, gm_b_s=m_gm_b_s,
                 gm_w_out=m_gm_w_out, hg_w_in=m_hg_w_in, hg_lb=m_hg_lb, hg_gn_g=m_hg_gn_g, hg_w_out=m_hg_w_out,
                 ffn_w_up=m_ffn_w_up, ffn_conv_w=m_ffn_conv_w, ffn_conv_b=m_ffn_conv_b, ffn_w_down=m_ffn_w_down,
                 norm_g=m_norm_g, ada_w=m_ada_w, ada_b=m_ada_b, final_g=m_final_g)
    mom_v = dict(gm_w_in=v_gm_w_in, gm_ln_g=v_gm_ln_g, gm_ln_b=v_gm_ln_b, gm_w_s=v_gm_w_s, gm_b_s=v_gm_b_s,
                 gm_w_out=v_gm_w_out, hg_w_in=v_hg_w_in, hg_lb=v_hg_lb, hg_gn_g=v_hg_gn_g, hg_w_out=v_hg_w_out,
                 ffn_w_up=v_ffn_w_up, ffn_conv_w=v_ffn_conv_w, ffn_conv_b=v_ffn_conv_b, ffn_w_down=v_ffn_w_down,
                 norm_g=v_norm_g, ada_w=v_ada_w, ada_b=v_ada_b, final_g=v_final_g)
    order = list(weights)

    pos = jnp.stack([chip, ci]).astype(jnp.int32)
    shards, by_col = [], []
    for i in range(DEPTH):
        for k in _layer_kinds(i):
            shards.append(weights[k][i if k.startswith("ffn") else i // 2].astype(BF16))
            by_col.append(COL_SHARDED[k])
    placed = [_place_own(sh, col, pos, f"place_own{n}") for n, (sh, col) in enumerate(zip(shards, by_col))]
    gsems, sh_thru, ld_thru, _ = _gather_start(shards, placed, by_col, LAYER_WEIGHTS, "gather_start")

    pieces = [c, hg_lb, hg_gn_g, norm_g, ffn_conv_w]
    payload, offs = _pack(pieces)
    got = _allgather_small(payload, "gather_small")
    c_g, lb_g, gn_g, ng_g, cw_g = _unpack(got, offs, [p.shape for p in pieces])
    c_all = c_g.reshape(N_DEV, Dm)
    P = dict(hg_lb=_from_chips(lb_g, 1), hg_gn_g=_from_chips(gn_g, 1), norm_g=_from_chips(ng_g, 2),
             ffn_conv_w=_from_chips(cw_g, 2), gm_ln_g=gm_ln_g, gm_ln_b=gm_ln_b, gm_w_s=gm_w_s, gm_b_s=gm_b_s,
             ffn_conv_b=ffn_conv_b, final_g=final_g)

    cols = ada_w.shape[2]
    ada_b_sh = lax.dynamic_slice_in_dim(ada_b, chip * cols, cols, axis=1)
    mod_sh = _ada_fwd(c_all, ada_w, ada_b_sh, "ada_fwd")
    mod_g = _allgather_small(mod_sh.reshape(-1, LANES), "gather_mod").reshape(N_DEV, DEPTH, N_DEV, cols)
    mod_mine = lax.dynamic_index_in_dim(mod_g[0::2], dev, axis=2, keepdims=False)
    mod = jnp.transpose(mod_mine, (1, 0, 2)).reshape(DEPTH, N_CHIPS * cols)

    core = jnp.reshape(ci, (1,)).astype(jnp.int32)
    chip_arr = jnp.reshape(chip, (1,)).astype(jnp.int32)
    pending, held = [], {}

    def get_w(i, after):
        s = slice(LAYER_WEIGHTS * i, LAYER_WEIGHTS * (i + 1))
        landed = _gather_wait(sh_thru[s], ld_thru[s], gsems[2 * i], gsems[2 * i + 1], after, by_col[s],
                              f"gather_wait{i}")
        full = _forward_sibling(landed, by_col[s], [a.shape for a in shards[s]], f"gather_forward{i}")
        return dict(zip(_layer_kinds(i), full))

    def on_grads(i, gdict):
        if i > 0 and "ffn_w_up" in gdict:
            held[i] = gdict
            return 0.0
        gdict = {**held.pop(i, {}), **gdict}
        kinds = [k for k in _layer_kinds(i) if k in gdict]
        tag = f"{i}_ffn" if kinds[0] == "ffn_w_up" else f"{i}"
        g4 = []
        for k in kinds:
            g = gdict[k]
            if not COL_SHARDED[k]:
                R, C = g.shape
                g = g.reshape(N_CHIPS, 2, R // (2 * N_CHIPS), C)
            g4.append(g)
        to_sib = [lax.dynamic_index_in_dim(g, 1 - ci, axis=1, keepdims=False).astype(BF16) for g in g4]
        from_sib = _swap_sibling(to_sib, f"reduce_swap{tag}")
        sums = [_add_own_half(g, r, core, f"chip_sum_{k}{i}") for g, r, k in zip(g4, from_sib, kinds)]
        send, recv, sums_thru, lands, token = _exchange_start(sums, f"reduce_start{tag}")
        pending.append((tag, i, kinds, send, recv, sums_thru, lands))
        return token[0, 0]

    loss_part, dx, G, small = _local_step(x[0], loss_target[0], mod, None, P, get_w, on_grads)

    sum_pieces = [loss_part[:, :1], small["final_g"], jnp.stack(small["gm_ln_g"]), jnp.stack(small["gm_ln_b"]),
                  jnp.stack(small["gm_w_s"]), jnp.stack(small["gm_b_s"]), jnp.stack(small["ffn_conv_b"]),
                  small["hg_lb"], jnp.stack(small["hg_gn_g"]), jnp.stack(small["norm_g"]),
                  jnp.stack(small["ffn_conv_w"])]
    dmod = jnp.concatenate(small["dmod"], axis=0)
    payload2, offs2 = _pack(sum_pieces + [dmod])
    placed2 = _place_row(payload2, jnp.reshape(dev, (1,)).astype(jnp.int32), "place_grads")
    bsend, brecv, p2_thru, l2_thru, small_token = _broadcast_start(payload2, placed2, "gather_grads_start")

    landed = {}
    for tag, i, kinds, send, recv, sums_thru, lands in pending:
        sums_i, lands_i = _exchange_wait(sums_thru, lands, send, recv, small_token, f"reduce_wait{tag}")
        for k, s_, l_ in zip(kinds, sums_i, lands_i):
            landed[(k, i)] = (l_, s_)
    own_halves = []
    for k in BIG:
        layers = [landed[(k, i)] for i in range(DEPTH) if (k, i) in landed]
        own_halves.append(_sum_chips([l_ for l_, _ in layers], [s_ for _, s_ in layers], chip_arr, f"sum_chips_{k}"))
    sib_halves = _swap_sibling(own_halves, "reduce_join")
    grads, deltas, new_m, new_v = {}, {}, {}, {}
    for k, own, recv in zip(BIG, own_halves, sib_halves):
        grads[k], deltas[k], new_m[k], new_v[k] = _adamw_halves(
            weights[k], own, recv, mom_m[k], mom_v[k], core, f"adamw_{k}")

    got2 = _broadcast_wait(p2_thru, l2_thru, bsend, brecv, new_v[BIG[-1]], "gather_grads_wait")
    dmod_all = _unpack(got2, offs2[-1:], [dmod.shape])[0]
    summed = _sum_devices(got2, "sum_devices")
    (loss_s, d_final_g, d_ln_g, d_ln_b, d_ws, d_bs, d_cb, d_lb, d_gn, d_ng, d_cw) = _unpack(
        summed, offs2[:-1], [(1,), final_g.shape, gm_ln_g.shape, gm_ln_b.shape, gm_w_s.shape, gm_b_s.shape,
                             ffn_conv_b.shape, (2, Dm), (2, Dm), (DEPTH, 2, Dm), (DEPTH, 3, 2 * FFN_HIDDEN)])
    grads.update(final_g=d_final_g, gm_ln_g=d_ln_g, gm_ln_b=d_ln_b, gm_w_s=d_ws, gm_b_s=d_bs, ffn_conv_b=d_cb)
    grads["hg_lb"] = lax.dynamic_slice_in_dim(d_lb, chip * hg_lb.shape[1], hg_lb.shape[1], axis=1)
    grads["hg_gn_g"] = lax.dynamic_slice_in_dim(d_gn, chip * hg_gn_g.shape[1], hg_gn_g.shape[1], axis=1)
    grads["norm_g"] = lax.dynamic_slice_in_dim(d_ng, chip * norm_g.shape[2], norm_g.shape[2], axis=2)
    grads["ffn_conv_w"] = lax.dynamic_slice_in_dim(d_cw, chip * ffn_conv_w.shape[2], ffn_conv_w.shape[2], axis=2)
    dmod_sh = lax.dynamic_slice_in_dim(dmod_all, chip * cols, cols, axis=2)
    grads["ada_w"] = _ada_bwd(c_all, jnp.transpose(dmod_sh, (1, 0, 2)), "ada_bwd")
    grads["ada_b"] = _sum_devices(dmod_all.reshape(N_DEV, -1, LANES), "sum_ada_b").reshape(ada_b.shape)

    for k in order:
        if k in BIG:
            continue
        w = weights[k]
        shp = w.shape
        view = (-1, shp[-1]) if w.ndim > 1 else (8, -1)
        d, m2, v2 = _adamw(w.reshape(view), grads[k].reshape(view), mom_m[k].reshape(view), mom_v[k].reshape(view),
                           f"adamw_{k}")
        deltas[k], new_m[k], new_v[k] = d.reshape(shp), m2.reshape(shp), v2.reshape(shp)
        grads[k] = grads[k].reshape(shp)

    loss = loss_s.reshape(())
    return (loss, dx[None], *[grads[k] for k in order], *[deltas[k] for k in order],
            *[new_m[k] for k in order], *[new_v[k] for k in order])
```

```python
import functools

import jax
import jax.numpy as jnp
from jax import lax
from jax.experimental import pallas as pl
from jax.experimental.pallas import tpu as pltpu

F32 = jnp.float32
BF16 = jnp.bfloat16
HI = lax.Precision.HIGHEST
X3 = lax.Precision.HIGH
MESH = pl.DeviceIdType.MESH

D_MODEL = 1024
DEPTH = 4
EPS = 1e-6
GM_WIDTH = 2048
GM_HEADS = 8
GM_BLOCK = 128
GM_HEAD_DIM = 256
CHUNK = 64
HG_HEADS = 8
HG_DIM = 128
FFN_HIDDEN = 2816
N_CHIPS = 4
N_DEV = 8

ADAM_LR = 0.001
ADAM_B1 = 0.9
ADAM_B2 = 0.999
ADAM_EPS = 1e-08
ADAM_WD = 0.01
ADAM_STEP = 10

VMEM_LIMIT_BYTES = 56 * 1024 * 1024
ROW_TILE = 256
LANES = 128

_SQRT_HALF = 0.7071067811865476
_INV_SQRT_2PI = 0.3989422804014327


def _pick(dim, prefs):
    for p in prefs:
        if dim % p == 0:
            return p
    return dim


def _params(sem):
    return pltpu.CompilerParams(dimension_semantics=sem, vmem_limit_bytes=VMEM_LIMIT_BYTES)


def _cdf(x):
    return 0.5 * (1.0 + lax.erf(x * _SQRT_HALF))


def _pdf(x):
    return jnp.exp(-0.5 * x * x) * _INV_SQRT_2PI


def _sig(x):
    return jax.nn.sigmoid(x)


def _dot(a, b, dims, prec=None):
    return lax.dot_general(a, b, (dims, ((), ())), precision=prec, preferred_element_type=F32)


NN = ((1,), (0,))
NT = ((1,), (1,))
TN = ((0,), (0,))


MM_VMEM_BUDGET = 40 * 1024 * 1024


def _mm_tiles(mode, M, N, K, a_bytes, b_bytes, exchange_out):
    tn = _pick(N, (1408, 1024, 512, 256, 128))
    tms = [t for t in (1408, 1024, 512, 256, 128) if M % t == 0 and not (exchange_out and (M // 2) % t)] or [M]
    tks = [K] + [t for t in (2816, 2048, 1408, 1024, 512, 256, 128) if t < K and K % t == 0]

    def fits(tm, tk):
        acc = tm * tn * 4 if tk < K else 0
        return 2 * tm * tk * a_bytes + 2 * tk * tn * b_bytes + 2 * tm * tn * 4 + acc <= MM_VMEM_BUDGET

    for min_tm in (min(512, tms[0]), 0):
        for tk in tks:
            for tm in tms:
                if tm >= min_tm and fits(tm, tk):
                    return tm, tn, tk
    return tms[-1], tn, tks[-1]


def _mm(a, b, mode, name, b_layer=None, out_dtype=F32, exchange_out=False):
    b2 = b.shape[-2:]
    if mode == "nn":
        (M, K), (_, N) = a.shape, b2
    elif mode == "nt":
        (M, K), (N, _) = a.shape, b2
    else:
        (K, M), (_, N) = a.shape, b2
    tm, tn, tk = _mm_tiles(mode, M, N, K, a.dtype.itemsize, b.dtype.itemsize, exchange_out)
    nk = K // tk
    dims = {"nn": NN, "nt": NT, "tn": TN}[mode]

    def body(a_ref, b_ref, o_ref, *scratch):
        part = _dot(a_ref[...].astype(BF16), b_ref[...].astype(BF16), dims)
        if nk == 1:
            o_ref[...] = part.astype(o_ref.dtype)
            return
        (acc_ref,) = scratch
        k = pl.program_id(2)

        @pl.when(k == 0)
        def _():
            acc_ref[...] = part

        @pl.when(k > 0)
        def _():
            acc_ref[...] += part

        @pl.when(k == nk - 1)
        def _():
            o_ref[...] = acc_ref[...].astype(o_ref.dtype)

    if mode == "tn":
        a_spec = pl.BlockSpec((tk, tm), lambda i, j, k: (k, i))
    else:
        a_spec = pl.BlockSpec((tm, tk), lambda i, j, k: (i, k))
    bblk = (tk, tn) if mode in ("nn", "tn") else (tn, tk)
    bidx = (lambda i, j, k: (k, j)) if mode in ("nn", "tn") else (lambda i, j, k: (j, k))
    if b_layer is None:
        b_spec = pl.BlockSpec(bblk, bidx)
    else:
        b_spec = pl.BlockSpec((None,) + bblk, lambda i, j, k: (b_layer,) + bidx(i, j, k))
    if exchange_out:
        mh, cw = M // 2, N // N_CHIPS
        assert mh % tm == 0 and cw % tn == 0
        out_shape = jax.ShapeDtypeStruct((N_CHIPS, 2, mh, cw), out_dtype)
        o_spec = pl.BlockSpec(
            (None, None, tm, tn),
            lambda i, j, k: (j // (cw // tn), i // (mh // tm), i % (mh // tm), j % (cw // tn)))
    else:
        out_shape = jax.ShapeDtypeStruct((M, N), out_dtype)
        o_spec = pl.BlockSpec((tm, tn), lambda i, j, k: (i, j))
    return pl.pallas_call(
        body, name=name, out_shape=out_shape, grid=(M // tm, N // tn, nk),
        in_specs=[a_spec, b_spec], out_specs=o_spec,
        scratch_shapes=[] if nk == 1 else [pltpu.VMEM((tm, tn), F32)],
        compiler_params=_params(("parallel", "parallel", "arbitrary")),
    )(a, b)


def _row_spec(tr, width):
    return pl.BlockSpec((tr, width), lambda i: (i, 0))


def _vec_spec(width, rows=1):
    return pl.BlockSpec((rows, width), lambda i: (0, 0))


def _norm_fwd(x, y, gate, g, sc, sh, name):
    T, Dm = x.shape
    tr = _pick(T, (ROW_TILE,))
    has_res = y is not None

    def body(*refs):
        if has_res:
            x_ref, y_ref, gate_ref, g_ref, sc_ref, sh_ref, xo_ref, h_ref = refs
            xv = x_ref[...] + gate_ref[...] * y_ref[...]
            xo_ref[...] = xv
        else:
            x_ref, g_ref, sc_ref, sh_ref, h_ref = refs
            xv = x_ref[...]
        rstd = lax.rsqrt(jnp.mean(xv * xv, axis=-1, keepdims=True) + EPS)
        h_ref[...] = ((xv * rstd * g_ref[...]) * (1.0 + sc_ref[...]) + sh_ref[...]).astype(BF16)

    row, vec = _row_spec(tr, Dm), _vec_spec(Dm)
    if has_res:
        ins, in_specs = (x, y, gate, g, sc, sh), [row, row, vec, vec, vec, vec]
        out_shape = (jax.ShapeDtypeStruct((T, Dm), F32), jax.ShapeDtypeStruct((T, Dm), BF16))
        out_specs = (row, row)
    else:
        ins, in_specs = (x, g, sc, sh), [row, vec, vec, vec]
        out_shape = jax.ShapeDtypeStruct((T, Dm), BF16)
        out_specs = row
    out = pl.pallas_call(body, name=name, out_shape=out_shape, grid=(T // tr,), in_specs=in_specs,
                         out_specs=out_specs, compiler_params=_params(("parallel",)))(*ins)
    return out if has_res else (x, out)


def _norm_bwd(x, dh, dxo, g, sc, y_prev, gate_prev, name):
    T, Dm = x.shape
    tr = _pick(T, (ROW_TILE,))
    has_prev = y_prev is not None

    def body(*refs):
        if has_prev:
            x_ref, dh_ref, dxo_ref, g_ref, sc_ref, yp_ref, gp_ref, dx_ref, dyp_ref, s1_ref, s2_ref, s3_ref = refs
        else:
            x_ref, dh_ref, dxo_ref, g_ref, sc_ref, dx_ref, s1_ref, s2_ref = refs

        @pl.when(pl.program_id(0) == 0)
        def _():
            s1_ref[...] = jnp.zeros_like(s1_ref)
            s2_ref[...] = jnp.zeros_like(s2_ref)
            if has_prev:
                s3_ref[...] = jnp.zeros_like(s3_ref)

        xv = x_ref[...]
        rstd = lax.rsqrt(jnp.mean(xv * xv, axis=-1, keepdims=True) + EPS)
        xhat = xv * rstd
        dh = dh_ref[...]
        dxhat = dh * (g_ref[...] * (1.0 + sc_ref[...]))
        dx = dxo_ref[...] + rstd * (dxhat - xhat * jnp.mean(dxhat * xhat, axis=-1, keepdims=True))
        dx_ref[...] = dx
        s1_ref[...] += jnp.sum(dh, axis=0, keepdims=True)
        s2_ref[...] += jnp.sum(dh * xhat, axis=0, keepdims=True)
        if has_prev:
            dyp_ref[...] = (gp_ref[...] * dx).astype(BF16)
            s3_ref[...] += jnp.sum(dx * yp_ref[...], axis=0, keepdims=True)

    row, vec = _row_spec(tr, Dm), _vec_spec(Dm)
    vshape = jax.ShapeDtypeStruct((1, Dm), F32)
    if has_prev:
        ins, in_specs = (x, dh, dxo, g, sc, y_prev, gate_prev), [row, row, row, vec, vec, row, vec]
        out_shape = (jax.ShapeDtypeStruct((T, Dm), F32), jax.ShapeDtypeStruct((T, Dm), BF16), vshape, vshape, vshape)
        out_specs = (row, row, vec, vec, vec)
    else:
        ins, in_specs = (x, dh, dxo, g, sc), [row, row, row, vec, vec]
        out_shape = (jax.ShapeDtypeStruct((T, Dm), F32), vshape, vshape)
        out_specs = (row, vec, vec)
    return pl.pallas_call(body, name=name, out_shape=out_shape, grid=(T // tr,), in_specs=in_specs,
                          out_specs=out_specs, compiler_params=_params(("arbitrary",)))(*ins)


def _loss_head(x, y, gate, fg, tgt, name):
    T, Dm = x.shape
    tr = _pick(T, (ROW_TILE,))
    nsteps = T // tr

    def body(x_ref, y_ref, gate_ref, fg_ref, t_ref, dx_ref, dy_ref, loss_ref, sfg_ref, sg_ref, acc_ref):
        i = pl.program_id(0)

        @pl.when(i == 0)
        def _():
            acc_ref[...] = jnp.zeros_like(acc_ref)
            sfg_ref[...] = jnp.zeros_like(sfg_ref)
            sg_ref[...] = jnp.zeros_like(sg_ref)

        yv = y_ref[...]
        xv = x_ref[...] + gate_ref[...] * yv
        rstd = lax.rsqrt(jnp.mean(xv * xv, axis=-1, keepdims=True) + EPS)
        xhat = xv * rstd
        err = xhat * fg_ref[...] - t_ref[...]
        acc_ref[...] += jnp.sum(err * err, axis=0, keepdims=True)
        dyn = err * (1.0 / Dm)
        sfg_ref[...] += jnp.sum(dyn * xhat, axis=0, keepdims=True)
        dxhat = dyn * fg_ref[...]
        dx = rstd * (dxhat - xhat * jnp.mean(dxhat * xhat, axis=-1, keepdims=True))
        dx_ref[...] = dx
        dy_ref[...] = (gate_ref[...] * dx).astype(BF16)
        sg_ref[...] += jnp.sum(dx * yv, axis=0, keepdims=True)

        @pl.when(i == nsteps - 1)
        def _():
            total = jnp.sum(acc_ref[...], axis=1, keepdims=True) * (0.5 / Dm)
            loss_ref[...] = jnp.broadcast_to(total, loss_ref.shape)

    row, vec = _row_spec(tr, Dm), _vec_spec(Dm)
    vshape = jax.ShapeDtypeStruct((1, Dm), F32)
    return pl.pallas_call(
        body, name=name, grid=(nsteps,),
        out_shape=(jax.ShapeDtypeStruct((T, Dm), F32), jax.ShapeDtypeStruct((T, Dm), BF16),
                   jax.ShapeDtypeStruct((1, LANES), F32), vshape, vshape),
        in_specs=[row, row, vec, vec, row], out_specs=(row, row, _vec_spec(LANES), vec, vec),
        scratch_shapes=[pltpu.VMEM((1, Dm), F32)], compiler_params=_params(("arbitrary",)),
    )(x, y, gate, fg, tgt)


def _spatial_mask():
    r = lax.broadcasted_iota(jnp.int32, (GM_BLOCK, GM_BLOCK), 0) // CHUNK
    c = lax.broadcasted_iota(jnp.int32, (GM_BLOCK, GM_BLOCK), 1) // CHUNK
    return r >= c


def _gm_specs(tr):
    return [_row_spec(tr, 2 * GM_WIDTH), _vec_spec(GM_WIDTH), _vec_spec(GM_WIDTH),
            pl.BlockSpec((GM_HEADS, GM_BLOCK, GM_BLOCK), lambda i: (0, 0, 0)),
            pl.BlockSpec((GM_HEADS, GM_BLOCK, 1), lambda i: (0, 0, 0))]


def _gm_mid_fwd(a, ln_g, ln_b, ws, bs3, name):
    T = a.shape[0]
    tr = _pick(T, (ROW_TILE,))
    W = GM_WIDTH

    def body(a_ref, lg_ref, lb_ref, ws_ref, bs_ref, o_ref, vn_scr):
        av = a_ref[:, W:]
        v = av * _cdf(av)
        vc = v - jnp.mean(v, axis=-1, keepdims=True)
        rstd = lax.rsqrt(jnp.mean(vc * vc, axis=-1, keepdims=True) + EPS)
        vn_scr[...] = (vc * rstd * lg_ref[...] + lb_ref[...]).astype(BF16)
        mask = _spatial_mask()
        for h in range(GM_HEADS):
            w = jnp.where(mask, ws_ref[h], 0.0).astype(BF16)
            cs = slice(h * GM_HEAD_DIM, (h + 1) * GM_HEAD_DIM)
            for blk in range(tr // GM_BLOCK):
                rs = slice(blk * GM_BLOCK, (blk + 1) * GM_BLOCK)
                s = _dot(w, vn_scr[rs, cs], NN) + bs_ref[h]
                au = a_ref[rs, cs]
                o_ref[rs, cs] = (au * _cdf(au) * s).astype(BF16)

    return pl.pallas_call(
        body, name=name, out_shape=jax.ShapeDtypeStruct((T, W), BF16), grid=(T // tr,),
        in_specs=_gm_specs(tr), out_specs=_row_spec(tr, W),
        scratch_shapes=[pltpu.VMEM((tr, W), BF16)], compiler_params=_params(("parallel",)),
    )(a, ln_g, ln_b, ws, bs3)


def _gm_mid_bwd(a, dgated, ln_g, ln_b, ws, bs3, name):
    T = a.shape[0]
    tr = _pick(T, (ROW_TILE,))
    W = GM_WIDTH
    nsteps = T // tr

    def body(a_ref, dg_ref, lg_ref, lb_ref, ws_ref, bs_ref, da_ref, dws_ref, dbs_ref, dlg_ref, dlb_ref,
             vn_scr, vhat_scr, dvn_scr, dsum_scr):
        i = pl.program_id(0)

        @pl.when(i == 0)
        def _():
            dws_ref[...] = jnp.zeros_like(dws_ref)
            dbs_ref[...] = jnp.zeros_like(dbs_ref)
            dlg_ref[...] = jnp.zeros_like(dlg_ref)
            dlb_ref[...] = jnp.zeros_like(dlb_ref)
            dsum_scr[...] = jnp.zeros_like(dsum_scr)

        av = a_ref[:, W:]
        cdf_v = _cdf(av)
        v = av * cdf_v
        vc = v - jnp.mean(v, axis=-1, keepdims=True)
        rstd = lax.rsqrt(jnp.mean(vc * vc, axis=-1, keepdims=True) + EPS)
        vhat_scr[...] = vc * rstd
        vn_scr[...] = (vhat_scr[...] * lg_ref[...] + lb_ref[...]).astype(BF16)
        mask = _spatial_mask()
        for h in range(GM_HEADS):
            w = jnp.where(mask, ws_ref[h], 0.0).astype(BF16)
            cs = slice(h * GM_HEAD_DIM, (h + 1) * GM_HEAD_DIM)
            for blk in range(tr // GM_BLOCK):
                rs = slice(blk * GM_BLOCK, (blk + 1) * GM_BLOCK)
                vnb = vn_scr[rs, cs]
                s = _dot(w, vnb, NN) + bs_ref[h]
                au = a_ref[rs, cs]
                cdf_u = _cdf(au)
                dg = dg_ref[rs, cs]
                ds = dg * (au * cdf_u)
                da_ref[rs, cs] = (dg * s * (cdf_u + au * _pdf(au))).astype(BF16)
                dsb = ds.astype(BF16)
                dvn_scr[rs, cs] = _dot(w, dsb, TN)
                dws_ref[h] += _dot(dsb, vnb, NT)
                dsum_scr[:, cs] += ds
        dvn = dvn_scr[...]
        vhat = vhat_scr[...]
        dlg_ref[...] += jnp.sum(dvn * vhat, axis=0, keepdims=True)
        dlb_ref[...] += jnp.sum(dvn, axis=0, keepdims=True)
        dvh = dvn * lg_ref[...]
        dv = rstd * (dvh - jnp.mean(dvh, axis=-1, keepdims=True)
                     - vhat * jnp.mean(dvh * vhat, axis=-1, keepdims=True))
        da_ref[:, W:] = (dv * (cdf_v + av * _pdf(av))).astype(BF16)

        @pl.when(i == nsteps - 1)
        def _():
            for h in range(GM_HEADS):
                dws_ref[h] = jnp.where(mask, dws_ref[h], 0.0)
            col_head = lax.broadcasted_iota(jnp.int32, (W, GM_BLOCK), 0) // GM_HEAD_DIM
            sel = (col_head == lax.broadcasted_iota(jnp.int32, (W, GM_BLOCK), 1)).astype(F32)
            dbs_ref[...] = _dot(dsum_scr[...], sel, NN, HI)

    vshape = jax.ShapeDtypeStruct((1, W), F32)
    return pl.pallas_call(
        body, name=name, grid=(nsteps,),
        out_shape=(jax.ShapeDtypeStruct((T, 2 * W), BF16), jax.ShapeDtypeStruct((GM_HEADS, GM_BLOCK, GM_BLOCK), F32),
                   jax.ShapeDtypeStruct((GM_BLOCK, GM_BLOCK), F32), vshape, vshape),
        in_specs=[_gm_specs(tr)[0], _row_spec(tr, W)] + _gm_specs(tr)[1:],
        out_specs=(_row_spec(tr, 2 * W), pl.BlockSpec((GM_HEADS, GM_BLOCK, GM_BLOCK), lambda i: (0, 0, 0)),
                   pl.BlockSpec((GM_BLOCK, GM_BLOCK), lambda i: (0, 0)), _vec_spec(W), _vec_spec(W)),
        scratch_shapes=[pltpu.VMEM((tr, W), BF16), pltpu.VMEM((tr, W), F32), pltpu.VMEM((tr, W), F32),
                        pltpu.VMEM((GM_BLOCK, W), F32)],
        compiler_params=_params(("arbitrary",)),
    )(a, dgated, ln_g, ln_b, ws, bs3)


SUB = 16
EXP_CLAMP = 80.0


def _tri(lower):
    r = lax.broadcasted_iota(jnp.int32, (CHUNK, CHUNK), 0)
    c = lax.broadcasted_iota(jnp.int32, (CHUNK, CHUNK), 1)
    return (r >= c) if lower else (c >= r)


def _score_masks():
    i = lax.broadcasted_iota(jnp.int32, (CHUNK, CHUNK), 0)
    j = lax.broadcasted_iota(jnp.int32, (CHUNK, CHUNK), 1)
    bi, bj = i // SUB, j // SUB
    diag = (bi == bj) & (i >= j)
    pair = (bi % 2 == 1) & (bj == bi - 1)
    half = (i >= CHUNK // 2) & (j < CHUNK // 2)
    return diag, pair, half


def _dot01(m, x):
    x1 = x.astype(BF16)
    rest = x - x1.astype(F32)
    x2 = rest.astype(BF16)
    x3 = (rest - x2.astype(F32)).astype(BF16)
    return _dot(m, x1, NN) + (_dot(m, x2, NN) + _dot(m, x3, NN))


def _block_rows(b, offset):
    parts = []
    for blk in range(0, CHUNK, SUB):
        r = blk + offset
        parts.append(jnp.zeros((SUB, b.shape[1]), F32) if r < 0 else jnp.broadcast_to(b[r:r + 1], (SUB, b.shape[1])))
    return jnp.concatenate(parts, axis=0)


def _hg_gates(p_ref, lb_ref, lower):
    Dm = D_MODEL
    heads = []
    for h in range(HG_HEADS):
        c0 = h * HG_DIM
        qr = p_ref[:, c0:c0 + HG_DIM]
        fz = p_ref[:, Dm + c0:Dm + c0 + HG_DIM]
        lbh = lb_ref[:, c0:c0 + HG_DIM]
        sg = _sig(fz)
        f = lbh + (1.0 - lbh) * sg
        sq = _sig(qr)
        heads.append(dict(qr=qr, v=p_ref[:, 2 * Dm + c0:2 * Dm + c0 + HG_DIM],
                          gt=p_ref[:, 3 * Dm + c0:3 * Dm + c0 + HG_DIM], lbh=lbh, sg=sg, f=f, gl=jnp.log(f),
                          kk=1.0 - f, sq=sq, q=qr * sq))
    for g in heads:
        g["b"] = _dot01(lower, g.pop("gl"))
    for g in heads:
        g.update(_hg_scalings(g["q"], g["kk"], g.pop("b")))
    return heads


def _hg_scalings(q, kk, b):
    r_mid = _block_rows(b, SUB // 2 - 1)
    r_prev = _block_rows(b, -1)
    r_end = _block_rows(b, SUB - 1)
    r_half = jnp.broadcast_to(b[CHUNK // 2 - 1:CHUNK // 2], b.shape)
    bc = b[CHUNK - 1:CHUNK]
    eqs = (jnp.exp(jnp.clip(b - r_mid, -EXP_CLAMP, EXP_CLAMP)), jnp.exp(jnp.minimum(b - r_prev, 0.0)),
           jnp.exp(jnp.minimum(b - r_half, 0.0)))
    eks = (jnp.exp(jnp.clip(r_mid - b, -EXP_CLAMP, EXP_CLAMP)), jnp.exp(jnp.minimum(r_end - b, 0.0)),
           jnp.exp(jnp.minimum(r_half - b, 0.0)))
    eb = jnp.exp(b)
    ec = jnp.exp(bc - b)
    return dict(eqs=eqs, eks=eks, eb=eb, ec=ec, e_end=jnp.exp(bc), qs=[q * e for e in eqs],
                ks=[kk * e for e in eks], qe=q * eb, ke=kk * ec)


def _scores(g, masks):
    a = None
    for qs, ks, m in zip(g["qs"], g["ks"], masks):
        part = jnp.where(m, _dot(qs.astype(BF16), ks.astype(BF16), NT), 0.0)
        a = part if a is None else a + part
    return a


def _hg_scan_fwd(p, lb, gn, name):
    T = p.shape[0]
    nc = T // CHUNK
    Dm = D_MODEL

    def body(p_ref, lb_ref, gn_ref, o_ref, og_ref, so_ref, st_ref):
        @pl.when(pl.program_id(0) == 0)
        def _():
            st_ref[...] = jnp.zeros_like(st_ref)

        masks = _score_masks()
        heads = _hg_gates(p_ref, lb_ref, _tri(True).astype(BF16))
        states = [st_ref[h] for h in range(HG_HEADS)]
        scores = [_scores(g, masks) for g in heads]
        outs = [_dot(a.astype(BF16), g["v"].astype(BF16), NN) + _dot(g["qe"], st, NT, X3)
                for g, a, st in zip(heads, scores, states)]
        new_states = [st * g["e_end"] + _dot(g["v"], g["ke"], TN, X3) for g, st in zip(heads, states)]
        for h, (g, o, st, st2) in enumerate(zip(heads, outs, states, new_states)):
            cs = slice(h * HG_DIM, (h + 1) * HG_DIM)
            so_ref[0, h] = st
            st_ref[h] = st2
            o_ref[:, cs] = o
            r = lax.rsqrt(jnp.mean(o * o, axis=-1, keepdims=True) + EPS)
            gt = g["gt"]
            og_ref[:, cs] = (((o * r) * gn_ref[:, cs]).astype(F32) * (gt * _sig(gt))).astype(BF16)

    return pl.pallas_call(
        body, name=name, grid=(nc,),
        out_shape=(jax.ShapeDtypeStruct((T, Dm), F32), jax.ShapeDtypeStruct((T, Dm), BF16),
                   jax.ShapeDtypeStruct((nc, HG_HEADS, HG_DIM, HG_DIM), F32)),
        in_specs=[_row_spec(CHUNK, 4 * Dm), _vec_spec(Dm), _vec_spec(Dm)],
        out_specs=(_row_spec(CHUNK, Dm), _row_spec(CHUNK, Dm),
                   pl.BlockSpec((1, HG_HEADS, HG_DIM, HG_DIM), lambda i: (i, 0, 0, 0))),
        scratch_shapes=[pltpu.VMEM((HG_HEADS, HG_DIM, HG_DIM), F32)],
        compiler_params=_params(("arbitrary",)),
    )(p, lb, gn)


def _hg_scan_bwd(p, lb, gn, o, dog, states, name):
    T = p.shape[0]
    nc = T // CHUNK
    Dm = D_MODEL

    def rev(i):
        return nc - 1 - i

    def body(p_ref, lb_ref, gn_ref, o_ref, dog_ref, st_in_ref, dp_ref, dlb_ref, dgn_ref, dst_ref, carry_ref):
        @pl.when(pl.program_id(0) == 0)
        def _():
            dst_ref[...] = jnp.zeros_like(dst_ref)
            carry_ref[...] = jnp.zeros_like(carry_ref)
            dlb_ref[...] = jnp.zeros_like(dlb_ref)
            dgn_ref[...] = jnp.zeros_like(dgn_ref)

        upper = _tri(False).astype(BF16)
        masks = _score_masks()
        heads = _hg_gates(p_ref, lb_ref, _tri(True).astype(BF16))
        for h, g in enumerate(heads):
            cs = slice(h * HG_DIM, (h + 1) * HG_DIM)
            oh = o_ref[:, cs]
            r = lax.rsqrt(jnp.mean(oh * oh, axis=-1, keepdims=True) + EPS)
            on = oh * r
            gt = g["gt"]
            sgt = _sig(gt)
            sil = gt * sgt
            dogh = dog_ref[:, cs]
            gnh = gn_ref[:, cs]
            don = dogh * gnh * sil
            g["dgn"] = jnp.sum(dogh * on * sil, axis=0, keepdims=True)
            g["dgate"] = dogh * on * gnh * (sgt * (1.0 + gt * (1.0 - sgt)))
            g["do"] = r * (don - on * jnp.mean(don * on, axis=-1, keepdims=True))
            g["dst"] = dst_ref[h]
            g["st"] = st_in_ref[0, h]
            g["carry"] = carry_ref[h]
        for g in heads:
            g["a"] = _scores(g, masks)
            g["dob"] = g["do"].astype(BF16)
            g["da"] = _dot(g["dob"], g["v"].astype(BF16), NT)
        for g in heads:
            g["dv"] = _dot(g["a"].astype(BF16), g["dob"], TN) + _dot(g["ke"].astype(BF16), g["dst"].astype(BF16), NT)
            g["dq"] = _dot(g["do"], g["st"], NN, X3) * g["eb"]
            g["dk"] = _dot(g["v"], g["dst"], NN, X3) * g["ec"]
            g["dst2"] = g["dst"] * g["e_end"] + _dot(g["do"], g["qe"], TN, X3)
        for lvl in range(3):
            for g in heads:
                dam = jnp.where(masks[lvl], g["da"], 0.0)
                g["dq"] = g["dq"] + _dot(dam, g["ks"][lvl], NN, X3) * g["eqs"][lvl]
                g["dk"] = g["dk"] + _dot(dam, g["qs"][lvl], TN, X3) * g["eks"][lvl]
        for g in heads:
            g["dgd"] = g["q"] * g["dq"] - g["kk"] * g["dk"]
            g["dgl"] = _dot01(upper, g["dgd"]) + g["carry"]
        for h, g in enumerate(heads):
            c0 = h * HG_DIM
            cs = slice(c0, c0 + HG_DIM)
            df = g["dgl"] / g["f"] - g["dk"]
            sg, sq, qr = g["sg"], g["sq"], g["qr"]
            dst_ref[h] = g["dst2"]
            carry_ref[h] = g["carry"] + jnp.sum(g["dgd"], axis=0, keepdims=True)
            dgn_ref[:, cs] += g["dgn"]
            dlb_ref[:, cs] += jnp.sum(df * (1.0 - sg), axis=0, keepdims=True)
            dp_ref[:, c0:c0 + HG_DIM] = (g["dq"] * (sq * (1.0 + qr * (1.0 - sq)))).astype(BF16)
            dp_ref[:, Dm + c0:Dm + c0 + HG_DIM] = (df * (1.0 - g["lbh"]) * sg * (1.0 - sg)).astype(BF16)
            dp_ref[:, 2 * Dm + c0:2 * Dm + c0 + HG_DIM] = g["dv"].astype(BF16)
            dp_ref[:, 3 * Dm + c0:3 * Dm + c0 + HG_DIM] = g["dgate"].astype(BF16)

    vshape = jax.ShapeDtypeStruct((1, Dm), F32)
    rrow = lambda w: pl.BlockSpec((CHUNK, w), lambda i: (rev(i), 0))
    return pl.pallas_call(
        body, name=name, grid=(nc,),
        out_shape=(jax.ShapeDtypeStruct((T, 4 * Dm), BF16), vshape, vshape),
        in_specs=[rrow(4 * Dm), _vec_spec(Dm), _vec_spec(Dm), rrow(Dm), rrow(Dm),
                  pl.BlockSpec((1, HG_HEADS, HG_DIM, HG_DIM), lambda i: (rev(i), 0, 0, 0))],
        out_specs=(rrow(4 * Dm), _vec_spec(Dm), _vec_spec(Dm)),
        scratch_shapes=[pltpu.VMEM((HG_HEADS, HG_DIM, HG_DIM), F32), pltpu.VMEM((HG_HEADS, 1, HG_DIM), F32)],
        compiler_params=_params(("arbitrary",)),
    )(p, lb, gn, o, dog, states)


def _lb_fwd(hg_lb, name):
    def body(a_ref, o_ref):
        a0, a1 = a_ref[0:1], a_ref[1:2]
        m = jnp.maximum(a0, a1)
        e0, e1 = jnp.exp(a0 - m), jnp.exp(a1 - m)
        p0, p1 = e0 / (e0 + e1), e1 / (e0 + e1)
        o_ref[0:1] = p0 - p0
        o_ref[1:2] = (p0 + p1) - p0

    return pl.pallas_call(body, name=name, out_shape=jax.ShapeDtypeStruct(hg_lb.shape, F32))(hg_lb)


def _lb_bwd(hg_lb, dlb_all, name):
    def body(a_ref, d_ref, o_ref):
        a0, a1 = a_ref[0:1], a_ref[1:2]
        m = jnp.maximum(a0, a1)
        e0, e1 = jnp.exp(a0 - m), jnp.exp(a1 - m)
        p0, p1 = e0 / (e0 + e1), e1 / (e0 + e1)
        d1 = d_ref[1:2]
        o_ref[0:1] = -p0 * p1 * d1
        o_ref[1:2] = p1 * (1.0 - p1) * d1

    return pl.pallas_call(body, name=name, out_shape=jax.ShapeDtypeStruct(hg_lb.shape, F32))(hg_lb, dlb_all)


CONV_COLS = 256


def _conv_fwd(a, w, b, name):
    T = a.shape[0]
    Fh = FFN_HIDDEN
    tr = _pick(T, (ROW_TILE,))
    cw = CONV_COLS
    hb = tr // 8

    def body(a_ref, ap_ref, w_ref, b_ref, m_ref):
        m0 = (pl.program_id(0) > 0).astype(F32)

        def conv(cc):
            x = jnp.concatenate([ap_ref[:, pl.ds(cc, cw)] * m0, a_ref[:, pl.ds(cc, cw)]], axis=0)
            wv = w_ref[:, pl.ds(cc, cw)]
            y = b_ref[:, pl.ds(cc, cw)] + wv[2:3] * x + wv[1:2] * pltpu.roll(x, 1, axis=0) \
                + wv[0:1] * pltpu.roll(x, 2, axis=0)
            return y[8:]

        def step(c, carry):
            c0 = pl.multiple_of(c * cw, cw)
            c1 = pl.multiple_of(Fh + c * cw, cw)
            yg, yv = conv(c0), conv(c1)
            m_ref[:, pl.ds(c0, cw)] = (yg * _cdf(yg) * yv).astype(BF16)
            return carry

        lax.fori_loop(0, Fh // cw, step, 0)

    return pl.pallas_call(
        body, name=name, out_shape=jax.ShapeDtypeStruct((T, Fh), BF16), grid=(T // tr,),
        in_specs=[_row_spec(tr, 2 * Fh), pl.BlockSpec((8, 2 * Fh), lambda i: (jnp.maximum(i * hb - 1, 0), 0)),
                  _vec_spec(2 * Fh, 3), _vec_spec(2 * Fh)],
        out_specs=_row_spec(tr, Fh), compiler_params=_params(("parallel",)),
    )(a, a, w, b)


def _conv_bwd(a, dm, w, b, name):
    T = a.shape[0]
    Fh = FFN_HIDDEN
    tr = _pick(T, (ROW_TILE,))
    cw = CONV_COLS
    hb = tr // 8
    nsteps = T // tr
    n = tr + 8

    def body(a_ref, ap_ref, an_ref, dm_ref, dmn_ref, w_ref, b_ref, da_ref, dw_ref, db_ref):
        i = pl.program_id(0)
        m0 = (i > 0).astype(F32)
        m1 = (i < nsteps - 1).astype(F32)

        @pl.when(i == 0)
        def _():
            dw_ref[...] = jnp.zeros_like(dw_ref)
            db_ref[...] = jnp.zeros_like(db_ref)

        def prep(cc):
            x = jnp.concatenate([ap_ref[:, pl.ds(cc, cw)] * m0, a_ref[:, pl.ds(cc, cw)],
                                 an_ref[:, pl.ds(cc, cw)] * m1], axis=0)
            wv = w_ref[:, pl.ds(cc, cw)]
            s1 = pltpu.roll(x, 1, axis=0)
            s2 = pltpu.roll(x, 2, axis=0)
            y = b_ref[:, pl.ds(cc, cw)] + wv[2:3] * x + wv[1:2] * s1 + wv[0:1] * s2
            return wv, x[8:], s1[8:], s2[8:], y[8:]

        def back(cc, dy, wv, x0, s1, s2):
            da = wv[2:3] * dy + wv[1:2] * pltpu.roll(dy, n - 1, axis=0) + wv[0:1] * pltpu.roll(dy, n - 2, axis=0)
            da_ref[:, pl.ds(cc, cw)] = da[:tr].astype(BF16)
            d = dy[:tr]
            db_ref[:, pl.ds(cc, cw)] += jnp.sum(d, axis=0, keepdims=True)
            dw_ref[2:3, pl.ds(cc, cw)] += jnp.sum(d * x0[:tr], axis=0, keepdims=True)
            dw_ref[1:2, pl.ds(cc, cw)] += jnp.sum(d * s1[:tr], axis=0, keepdims=True)
            dw_ref[0:1, pl.ds(cc, cw)] += jnp.sum(d * s2[:tr], axis=0, keepdims=True)

        def step(c, carry):
            c0 = pl.multiple_of(c * cw, cw)
            c1 = pl.multiple_of(Fh + c * cw, cw)
            dmx = jnp.concatenate([dm_ref[:, pl.ds(c0, cw)], dmn_ref[:, pl.ds(c0, cw)] * m1], axis=0)
            wg, xg, s1g, s2g, yg = prep(c0)
            wv, xv, s1v, s2v, yv = prep(c1)
            cg = _cdf(yg)
            back(c0, dmx * yv * (cg + yg * _pdf(yg)), wg, xg, s1g, s2g)
            back(c1, dmx * (yg * cg), wv, xv, s1v, s2v)
            return carry

        lax.fori_loop(0, Fh // cw, step, 0)

    prev = lambda wd: pl.BlockSpec((8, wd), lambda i: (jnp.maximum(i * hb - 1, 0), 0))
    nxt = lambda wd: pl.BlockSpec((8, wd), lambda i: (jnp.minimum((i + 1) * hb, T // 8 - 1), 0))
    return pl.pallas_call(
        body, name=name, grid=(nsteps,),
        out_shape=(jax.ShapeDtypeStruct((T, 2 * Fh), BF16), jax.ShapeDtypeStruct((3, 2 * Fh), F32),
                   jax.ShapeDtypeStruct((1, 2 * Fh), F32)),
        in_specs=[_row_spec(tr, 2 * Fh), prev(2 * Fh), nxt(2 * Fh), _row_spec(tr, Fh), nxt(Fh),
                  _vec_spec(2 * Fh, 3), _vec_spec(2 * Fh)],
        out_specs=(_row_spec(tr, 2 * Fh), _vec_spec(2 * Fh, 3), _vec_spec(2 * Fh)),
        compiler_params=_params(("arbitrary",)),
    )(a, a, a, dm, dm, w, b)


def _ada_fwd(c_all, ada_w, ada_b, name):
    L, Dm, cols = ada_w.shape
    tn = _pick(cols, (512, 256, 128))

    def body(c_ref, w_ref, b_ref, o_ref):
        cv = c_ref[...]
        cond = (cv * _sig(cv)).astype(BF16)
        o_ref[...] = _dot(cond, w_ref[...].astype(BF16), NN) + b_ref[...]

    return pl.pallas_call(
        body, name=name, out_shape=jax.ShapeDtypeStruct((L, N_DEV, cols), F32), grid=(L, cols // tn),
        in_specs=[pl.BlockSpec((N_DEV, Dm), lambda l, j: (0, 0)), pl.BlockSpec((None, Dm, tn), lambda l, j: (l, 0, j)),
                  pl.BlockSpec((None, 1, tn), lambda l, j: (l, 0, j))],
        out_specs=pl.BlockSpec((None, N_DEV, tn), lambda l, j: (l, 0, j)),
        compiler_params=_params(("parallel", "parallel")),
    )(c_all, ada_w, ada_b.reshape(L, 1, cols))


def _ada_bwd(c_all, dmod, name):
    L, _, cols = dmod.shape
    Dm = c_all.shape[1]
    tn = _pick(cols, (512, 256, 128))

    def body(c_ref, d_ref, o_ref):
        cv = c_ref[...]
        o_ref[...] = _dot(cv * _sig(cv), d_ref[...], TN, HI)

    return pl.pallas_call(
        body, name=name, out_shape=jax.ShapeDtypeStruct((L, Dm, cols), F32), grid=(L, cols // tn),
        in_specs=[pl.BlockSpec((N_DEV, Dm), lambda l, j: (0, 0)), pl.BlockSpec((None, N_DEV, tn), lambda l, j: (l, 0, j))],
        out_specs=pl.BlockSpec((None, Dm, tn), lambda l, j: (l, 0, j)),
        compiler_params=_params(("parallel", "parallel")),
    )(c_all, dmod)


def _add_own_half(g4, rb, core, name):
    S, _, rh, cw = g4.shape
    tr = _pick(rh, (256, 128, 176, 64))

    def body(core_ref, g_ref, r_ref, o_ref):
        o_ref[...] = (g_ref[...] + r_ref[...].astype(F32)).astype(BF16)

    return pl.pallas_call(
        body, name=name, out_shape=jax.ShapeDtypeStruct((S, rh, cw), BF16),
        grid_spec=pltpu.PrefetchScalarGridSpec(
            num_scalar_prefetch=1, grid=(S, rh // tr),
            in_specs=[pl.BlockSpec((None, None, tr, cw), lambda s, i, core_ref: (s, core_ref[0], i, 0)),
                      pl.BlockSpec((None, tr, cw), lambda s, i, core_ref: (s, i, 0))],
            out_specs=pl.BlockSpec((None, tr, cw), lambda s, i, core_ref: (s, i, 0))),
        compiler_params=_params(("parallel", "parallel")),
    )(core, g4, rb)


def _sum_chips(lands, sums, chip, name):
    L = len(lands)
    _, rh, cw = lands[0].shape
    tr = _pick(rh, (256, 128, 176, 64))

    def body(chip_ref, *refs):
        ld, cs, o_ref = refs[:L], refs[L:2 * L], refs[2 * L]
        me = chip_ref[0]
        for k in range(L):
            @pl.when(pl.program_id(0) == k)
            def _(k=k):
                own = cs[k][...].astype(F32)
                got = [ld[k][j].astype(F32) for j in range(3)]
                acc = None
                for t in range(N_CHIPS):
                    d = jnp.bitwise_xor(jnp.int32(t), me)
                    term = jnp.where(d == 0, own, jnp.where(d == 2, got[0], jnp.where(d == 1, got[1], got[2])))
                    acc = term if acc is None else acc + term
                o_ref[...] = acc

    frozen = lambda l, i, k: jnp.where(l == k, i, 0)
    in_specs = [pl.BlockSpec((3, tr, cw), lambda l, i, chip_ref, k=k: (0, frozen(l, i, k), 0)) for k in range(L)]
    in_specs += [pl.BlockSpec((None, tr, cw), lambda l, i, chip_ref, k=k: (chip_ref[0], frozen(l, i, k), 0))
                 for k in range(L)]
    return pl.pallas_call(
        body, name=name, out_shape=jax.ShapeDtypeStruct((L, rh, cw), F32),
        grid_spec=pltpu.PrefetchScalarGridSpec(
            num_scalar_prefetch=1, grid=(L, rh // tr), in_specs=in_specs,
            out_specs=pl.BlockSpec((None, tr, cw), lambda l, i, chip_ref: (l, i, 0))),
        compiler_params=_params(("arbitrary", "arbitrary")),
    )(chip, *lands, *sums)


def _sum_devices(gathered, name):
    n, R, _ = gathered.shape
    tr = _pick(R, (512, 448, 384, 256, 192, 128, 64, 32, 16, 8))

    def body(g_ref, o_ref):
        acc = g_ref[0]
        for d in range(1, n):
            acc = acc + g_ref[d]
        o_ref[...] = acc

    return pl.pallas_call(
        body, name=name, out_shape=jax.ShapeDtypeStruct((R, LANES), F32), grid=(R // tr,),
        in_specs=[pl.BlockSpec((n, tr, LANES), lambda i: (0, i, 0))], out_specs=pl.BlockSpec((tr, LANES), lambda i: (i, 0)),
        compiler_params=_params(("parallel",)),
    )(gathered)


def _adamw(w, g, m, v, name):
    R, C = w.shape
    tr = _pick(R, (256, 128, 64, 32, 16, 8))
    c1 = 1.0 / (1.0 - ADAM_B1 ** ADAM_STEP)
    c2 = 1.0 / (1.0 - ADAM_B2 ** ADAM_STEP)

    def body(w_ref, g_ref, m_ref, v_ref, d_ref, mo_ref, vo_ref):
        gv = g_ref[...]
        m2 = ADAM_B1 * m_ref[...] + (1.0 - ADAM_B1) * gv
        v2 = ADAM_B2 * v_ref[...] + (1.0 - ADAM_B2) * (gv * gv)
        mo_ref[...] = m2
        vo_ref[...] = v2
        d_ref[...] = -ADAM_LR * ((m2 * c1) / (jnp.sqrt(v2 * c2) + ADAM_EPS) + ADAM_WD * w_ref[...])

    spec = pl.BlockSpec((tr, C), lambda i: (i, 0))
    shp = jax.ShapeDtypeStruct((R, C), F32)
    return pl.pallas_call(body, name=name, out_shape=(shp, shp, shp), grid=(R // tr,), in_specs=[spec] * 4,
                          out_specs=(spec, spec, spec), compiler_params=_params(("parallel",)))(w, g, m, v)


def _adamw_halves(w, own, recv, m, v, core, name):
    L, rh, cw = own.shape
    tr = _pick(rh, (256, 128, 176, 64))
    c1 = 1.0 / (1.0 - ADAM_B1 ** ADAM_STEP)
    c2 = 1.0 / (1.0 - ADAM_B2 ** ADAM_STEP)

    def body(core_ref, w_ref, own_ref, recv_ref, m_ref, v_ref, g_ref, d_ref, mo_ref, vo_ref):
        gv = jnp.where(pl.program_id(1) == core_ref[0], own_ref[...], recv_ref[...])
        g_ref[...] = gv
        m2 = ADAM_B1 * m_ref[...] + (1.0 - ADAM_B1) * gv
        v2 = ADAM_B2 * v_ref[...] + (1.0 - ADAM_B2) * (gv * gv)
        mo_ref[...] = m2
        vo_ref[...] = v2
        d_ref[...] = -ADAM_LR * ((m2 * c1) / (jnp.sqrt(v2 * c2) + ADAM_EPS) + ADAM_WD * w_ref[...])

    full = pl.BlockSpec((None, None, tr, cw), lambda l, hf, i, core_ref: (l, hf, i, 0))
    mine = pl.BlockSpec((None, tr, cw), lambda l, hf, i, core_ref: (l, jnp.where(hf == core_ref[0], i, 0), 0))
    other = pl.BlockSpec((None, tr, cw), lambda l, hf, i, core_ref: (l, jnp.where(hf == core_ref[0], 0, i), 0))
    shp = jax.ShapeDtypeStruct((L, 2, rh, cw), F32)
    view = lambda a: a.reshape(L, 2, rh, cw)
    outs = pl.pallas_call(
        body, name=name, out_shape=(shp, shp, shp, shp),
        grid_spec=pltpu.PrefetchScalarGridSpec(
            num_scalar_prefetch=1, grid=(L, 2, rh // tr), in_specs=[full, mine, other, full, full],
            out_specs=(full, full, full, full)),
        compiler_params=_params(("arbitrary", "arbitrary", "arbitrary")),
    )(core, view(w), own, recv, view(m), view(v))
    return tuple(o.reshape(L, 2 * rh, cw) for o in outs)


ANY = pl.BlockSpec(memory_space=pl.ANY)


def _position():
    x, y, c = lax.axis_index("x"), lax.axis_index("y"), lax.axis_index("c")
    return x, y, c


def _allgather(ins, out_shapes, src_fns, dst_fns, name, in_vmem):
    n = len(ins)

    def body(*refs):
        in_refs, out_refs = refs[:n], refs[n:2 * n]
        send_sems, recv_sems, local_sems = refs[2 * n:]
        x, y, c = _position()
        me, sibling = (x, y, c), (x, y, 1 - c)
        chips = [(1 - x, y), (x, 1 - y), (1 - x, 1 - y)]

        def copy(k, j, block, to, own=False):
            dst = dst_fns[k](out_refs[k], *block)
            return pltpu.make_async_remote_copy(
                src_ref=src_fns[k](in_refs[k], c) if own else dst, dst_ref=dst,
                send_sem=send_sems.at[k, j], recv_sem=recv_sems.at[k, j], device_id=to, device_id_type=MESH)

        mine = [pltpu.make_async_copy(src_fns[k](in_refs[k], c), dst_fns[k](out_refs[k], *me), local_sems.at[k])
                for k in range(n)]
        for cp in mine:
            cp.start()
        first = []
        for k in range(n):
            first.append(copy(k, 0, me, sibling, own=True))
            first += [copy(k, 1 + j, me, (*chip, c), own=True) for j, chip in enumerate(chips)]
        for cp in first:
            cp.start()
        passed = []
        for j, chip in enumerate(chips):
            for k in range(n):
                copy(k, 1 + j, (*chip, c), me).wait_recv()
                fwd = copy(k, 4 + j, (*chip, c), sibling)
                fwd.start()
                passed.append(fwd)
        for k in range(n):
            copy(k, 0, sibling, me).wait_recv()
        for j, chip in enumerate(chips):
            for k in range(n):
                copy(k, 4 + j, (*chip, 1 - c), me).wait_recv()
        for cp in first + passed:
            cp.wait_send()
        for cp in mine:
            cp.wait()

    spec = pl.BlockSpec(memory_space=pltpu.VMEM) if in_vmem else ANY
    return pl.pallas_call(
        body, name=name, out_shape=tuple(out_shapes), in_specs=[spec] * n, out_specs=tuple([spec] * n),
        scratch_shapes=[pltpu.SemaphoreType.DMA((n, 7)), pltpu.SemaphoreType.DMA((n, 7)),
                        pltpu.SemaphoreType.DMA((n,))],
        compiler_params=pltpu.CompilerParams(vmem_limit_bytes=VMEM_LIMIT_BYTES),
    )(*ins)


def _allgather_small(payload, name):
    R = payload.shape[0]
    (out,) = _allgather(
        [payload], [jax.ShapeDtypeStruct((N_DEV, R, LANES), F32)],
        [lambda ref, c: ref], [lambda ref, px, py, pc: ref.at[4 * px + 2 * py + pc]], name, in_vmem=True)
    return out


def _swap_sibling(ins, name):
    n = len(ins)

    def body(*refs):
        in_refs, out_refs = refs[:n], refs[n:2 * n]
        send_sems, recv_sems = refs[2 * n:]
        x, y, c = _position()
        copies = [pltpu.make_async_remote_copy(
            src_ref=in_refs[k], dst_ref=out_refs[k], send_sem=send_sems.at[k], recv_sem=recv_sems.at[k],
            device_id=(x, y, 1 - c), device_id_type=MESH) for k in range(n)]
        for cp in copies:
            cp.start()
        for cp in copies:
            cp.wait_recv()
        for cp in copies:
            cp.wait_send()

    return pl.pallas_call(
        body, name=name, out_shape=tuple(jax.ShapeDtypeStruct(a.shape, a.dtype) for a in ins),
        in_specs=[ANY] * n, out_specs=tuple([ANY] * n),
        scratch_shapes=[pltpu.SemaphoreType.DMA((n,)), pltpu.SemaphoreType.DMA((n,))],
        compiler_params=pltpu.CompilerParams(vmem_limit_bytes=VMEM_LIMIT_BYTES),
    )(*ins)


HBM_SPEC = pl.BlockSpec(memory_space=pltpu.HBM)
SEM_SPEC = pl.BlockSpec(memory_space=pltpu.SEMAPHORE)
SPLIT_PARAMS = pltpu.CompilerParams(has_side_effects=pltpu.SideEffectType.DATAFLOW_SIDE_EFFECTING)
TOKEN = jax.ShapeDtypeStruct((8, LANES), F32)


def _hbm(a):
    return pltpu.with_memory_space_constraint(a, pltpu.HBM)


def _weight_window(ref, col, r, cw, px, py, pc):
    rh = r // 2
    if col:
        return ref.at[pl.ds(pc * rh, rh), pl.ds((2 * px + py) * cw, cw)]
    return ref.at[pl.ds((2 * px + py) * r + pc * rh, rh), :]


def _peers(x, y, c):
    return [(x, y, 1 - c), (1 - x, y, c), (x, 1 - y, c), (1 - x, 1 - y, c)]


def _place_own(shard, col, pos, name):
    r, cw = shard.shape
    rh = r // 2
    tr = _pick(rh, (256, 128, 176, 64))
    nb = rh // tr
    shape = (r, N_CHIPS * cw) if col else (N_CHIPS * r, cw)

    def body(pos_ref, x_ref, o_ref):
        o_ref[...] = x_ref[...]

    if col:
        out_idx = lambda i, pos_ref: (pos_ref[1] * nb + i, pos_ref[0])
    else:
        out_idx = lambda i, pos_ref: (pos_ref[0] * (2 * nb) + pos_ref[1] * nb + i, 0)
    return pl.pallas_call(
        body, name=name, out_shape=jax.ShapeDtypeStruct(shape, shard.dtype),
        grid_spec=pltpu.PrefetchScalarGridSpec(
            num_scalar_prefetch=1, grid=(nb,),
            in_specs=[pl.BlockSpec((tr, cw), lambda i, pos_ref: (pos_ref[1] * nb + i, 0))],
            out_specs=pl.BlockSpec((tr, cw), out_idx)),
        compiler_params=_params(("arbitrary",)),
    )(pos, shard)


def _gather_start(shards, lands, cols, per_layer, name):
    n = len(shards)
    nl = n // per_layer

    def body(*refs):
        sh, ld = refs[:n], refs[n:2 * n]
        sems, token = refs[2 * n:2 * n + 2 * nl], refs[-1]
        x, y, c = _position()
        for k in range(n):
            l, a = divmod(k, per_layer)
            r, cw = shards[k].shape
            src = sh[k].at[pl.ds(c * (r // 2), r // 2), :]
            dst = _weight_window(ld[k], cols[k], r, cw, x, y, c)
            for j, peer in enumerate(_peers(x, y, c)):
                pltpu.make_async_remote_copy(src_ref=src, dst_ref=dst, send_sem=sems[2 * l].at[4 * a + j],
                                             recv_sem=sems[2 * l + 1].at[4 * a + j], device_id=peer,
                                             device_id_type=MESH).start()
        token[...] = jnp.zeros_like(token)

    arrs = list(shards) + list(lands)
    out = pl.pallas_call(
        body, name=name,
        out_shape=tuple(pltpu.SemaphoreType.DMA((per_layer * 4,)) for _ in range(2 * nl))
        + tuple(pltpu.HBM(a.shape, a.dtype) for a in arrs) + (TOKEN,),
        in_specs=[HBM_SPEC] * (2 * n),
        out_specs=(SEM_SPEC,) * (2 * nl) + (HBM_SPEC,) * (2 * n) + (pl.BlockSpec(memory_space=pltpu.VMEM),),
        input_output_aliases={i: 2 * nl + i for i in range(2 * n)}, compiler_params=SPLIT_PARAMS,
    )(*[_hbm(a) for a in arrs])
    return out[:2 * nl], out[2 * nl:2 * nl + n], out[2 * nl + n:2 * nl + 2 * n], out[-1]


def _gather_wait(shards, lands, send, recv, after, cols, first, name):
    m = len(shards)

    def body(*refs):
        sh, ld = refs[:m], refs[m:2 * m]
        send_ref, recv_ref = refs[2 * m], refs[2 * m + 1]
        x, y, c = _position()
        for a in range(m):
            r, cw = shards[a].shape
            src = sh[a].at[pl.ds(c * (r // 2), r // 2), :]
            for j, (px, py, pc) in enumerate(_peers(x, y, c)):
                cp = pltpu.make_async_remote_copy(
                    src_ref=src, dst_ref=_weight_window(ld[a], cols[a], r, cw, px, py, pc),
                    send_sem=send_ref.at[4 * (first + a) + j], recv_sem=recv_ref.at[4 * (first + a) + j],
                    device_id=(px, py, pc),
                    device_id_type=MESH)
                cp.wait_send()
                cp.wait_recv()

    arrs = list(shards) + list(lands)
    out = pl.pallas_call(
        body, name=name, out_shape=tuple(pltpu.HBM(a.shape, a.dtype) for a in arrs),
        in_specs=[HBM_SPEC] * (2 * m) + [SEM_SPEC, SEM_SPEC, ANY], out_specs=(HBM_SPEC,) * (2 * m),
        input_output_aliases={i: i for i in range(2 * m)}, compiler_params=SPLIT_PARAMS,
    )(*arrs, send, recv, after)
    return out[m:]


def _forward_sibling(lands, cols, shard_shapes, name):
    m = len(lands)

    def body(*refs):
        ins, outs = refs[:m], refs[m:2 * m]
        send_sems, recv_sems = refs[2 * m:]
        x, y, c = _position()
        chips = [(1 - x, y), (x, 1 - y), (1 - x, 1 - y)]
        sends = []
        for a in range(m):
            r, cw = shard_shapes[a]
            for j, (px, py) in enumerate(chips):
                cp = pltpu.make_async_remote_copy(
                    src_ref=_weight_window(ins[a], cols[a], r, cw, px, py, c),
                    dst_ref=_weight_window(outs[a], cols[a], r, cw, px, py, c),
                    send_sem=send_sems.at[a, j], recv_sem=recv_sems.at[a, j], device_id=(x, y, 1 - c),
                    device_id_type=MESH)
                cp.start()
                sends.append(cp)
        for a in range(m):
            r, cw = shard_shapes[a]
            for j, (px, py) in enumerate(chips):
                pltpu.make_async_remote_copy(
                    src_ref=_weight_window(ins[a], cols[a], r, cw, px, py, c),
                    dst_ref=_weight_window(outs[a], cols[a], r, cw, px, py, 1 - c),
                    send_sem=send_sems.at[a, j], recv_sem=recv_sems.at[a, j], device_id=(x, y, 1 - c),
                    device_id_type=MESH).wait_recv()
        for cp in sends:
            cp.wait_send()

    return pl.pallas_call(
        body, name=name, out_shape=tuple(jax.ShapeDtypeStruct(a.shape, a.dtype) for a in lands),
        in_specs=[ANY] * m, out_specs=tuple([ANY] * m), input_output_aliases={i: i for i in range(m)},
        scratch_shapes=[pltpu.SemaphoreType.DMA((m, 3)), pltpu.SemaphoreType.DMA((m, 3))],
        compiler_params=pltpu.CompilerParams(vmem_limit_bytes=VMEM_LIMIT_BYTES),
    )(*lands)


def _exchange_start(sums, name):
    m = len(sums)
    lands = [lax.empty((3,) + s.shape[1:], s.dtype) for s in sums]

    def body(*refs):
        cs, ld = refs[:m], refs[m:2 * m]
        send_ref, recv_ref, token = refs[2 * m], refs[2 * m + 1], refs[-1]
        x, y, c = _position()
        for a in range(m):
            for j, (px, py) in enumerate([(1 - x, y), (x, 1 - y), (1 - x, 1 - y)]):
                pltpu.make_async_remote_copy(
                    src_ref=cs[a].at[2 * px + py], dst_ref=ld[a].at[j], send_sem=send_ref.at[3 * a + j],
                    recv_sem=recv_ref.at[3 * a + j], device_id=(px, py, c), device_id_type=MESH).start()
        token[...] = jnp.zeros_like(token)

    arrs = list(sums) + lands
    out = pl.pallas_call(
        body, name=name,
        out_shape=(pltpu.SemaphoreType.DMA((m * 3,)), pltpu.SemaphoreType.DMA((m * 3,)))
        + tuple(pltpu.HBM(a.shape, a.dtype) for a in arrs) + (TOKEN,),
        in_specs=[HBM_SPEC] * (2 * m),
        out_specs=(SEM_SPEC, SEM_SPEC) + (HBM_SPEC,) * (2 * m) + (pl.BlockSpec(memory_space=pltpu.VMEM),),
        input_output_aliases={i: 2 + i for i in range(2 * m)}, compiler_params=SPLIT_PARAMS,
    )(*[_hbm(a) for a in arrs])
    return out[0], out[1], out[2:2 + m], out[2 + m:2 + 2 * m], out[-1]


def _exchange_wait(sums, lands, send, recv, after, name):
    m = len(sums)

    def body(*refs):
        cs, ld = refs[:m], refs[m:2 * m]
        send_ref, recv_ref = refs[2 * m], refs[2 * m + 1]
        x, y, c = _position()
        for a in range(m):
            for j, (px, py) in enumerate([(1 - x, y), (x, 1 - y), (1 - x, 1 - y)]):
                cp = pltpu.make_async_remote_copy(
                    src_ref=cs[a].at[2 * px + py], dst_ref=ld[a].at[j], send_sem=send_ref.at[3 * a + j],
                    recv_sem=recv_ref.at[3 * a + j], device_id=(px, py, c), device_id_type=MESH)
                cp.wait_send()
                cp.wait_recv()

    arrs = list(sums) + list(lands)
    out = pl.pallas_call(
        body, name=name, out_shape=tuple(pltpu.HBM(a.shape, a.dtype) for a in arrs),
        in_specs=[HBM_SPEC] * (2 * m) + [SEM_SPEC, SEM_SPEC, ANY], out_specs=(HBM_SPEC,) * (2 * m),
        input_output_aliases={i: i for i in range(2 * m)}, compiler_params=SPLIT_PARAMS,
    )(*arrs, send, recv, after)
    return out[:m], out[m:]


def _place_row(payload, dev, name):
    R = payload.shape[0]
    tr = _pick(R, (512, 448, 384, 256, 192, 128, 64, 32, 16, 8))

    def body(dev_ref, x_ref, o_ref):
        o_ref[...] = x_ref[...]

    return pl.pallas_call(
        body, name=name, out_shape=jax.ShapeDtypeStruct((N_DEV, R, LANES), payload.dtype),
        grid_spec=pltpu.PrefetchScalarGridSpec(
            num_scalar_prefetch=1, grid=(R // tr,),
            in_specs=[pl.BlockSpec((tr, LANES), lambda i, dev_ref: (i, 0))],
            out_specs=pl.BlockSpec((None, tr, LANES), lambda i, dev_ref: (dev_ref[0], i, 0))),
        compiler_params=_params(("arbitrary",)),
    )(dev, payload)


def _others(x, y, c):
    return [(1 - x if fx else x, 1 - y if fy else y, 1 - c if fc else c)
            for fx in (0, 1) for fy in (0, 1) for fc in (0, 1) if fx or fy or fc]


def _broadcast_start(payload, land, name):
    def body(p_ref, l_ref, send_ref, recv_ref, p_thru, l_thru, token):
        x, y, c = _position()
        for j, peer in enumerate(_others(x, y, c)):
            pltpu.make_async_remote_copy(src_ref=p_ref, dst_ref=l_ref.at[4 * x + 2 * y + c], send_sem=send_ref.at[j],
                                         recv_sem=recv_ref.at[j], device_id=peer, device_id_type=MESH).start()
        token[...] = jnp.zeros_like(token)

    n = N_DEV - 1
    return pl.pallas_call(
        body, name=name,
        out_shape=(pltpu.SemaphoreType.DMA((n,)), pltpu.SemaphoreType.DMA((n,)), pltpu.HBM(payload.shape, payload.dtype),
                   pltpu.HBM(land.shape, land.dtype), TOKEN),
        in_specs=[HBM_SPEC, HBM_SPEC],
        out_specs=(SEM_SPEC, SEM_SPEC, HBM_SPEC, HBM_SPEC, pl.BlockSpec(memory_space=pltpu.VMEM)),
        input_output_aliases={0: 2, 1: 3}, compiler_params=SPLIT_PARAMS,
    )(_hbm(payload), _hbm(land))


def _broadcast_wait(payload, land, send, recv, after, name):
    def body(p_ref, l_ref, send_ref, recv_ref, after_ref, p_thru, l_thru):
        x, y, c = _position()
        for j, (px, py, pc) in enumerate(_others(x, y, c)):
            cp = pltpu.make_async_remote_copy(src_ref=p_ref, dst_ref=l_ref.at[4 * px + 2 * py + pc],
                                              send_sem=send_ref.at[j], recv_sem=recv_ref.at[j],
                                              device_id=(px, py, pc), device_id_type=MESH)
            cp.wait_send()
            cp.wait_recv()

    out = pl.pallas_call(
        body, name=name, out_shape=(pltpu.HBM(payload.shape, payload.dtype), pltpu.HBM(land.shape, land.dtype)),
        in_specs=[HBM_SPEC, HBM_SPEC, SEM_SPEC, SEM_SPEC, ANY], out_specs=(HBM_SPEC, HBM_SPEC),
        input_output_aliases={0: 0, 1: 1}, compiler_params=SPLIT_PARAMS,
    )(payload, land, send, recv, after)
    return out[1]


def _vec(a):
    return a.reshape(1, -1)


def _local_step(x, tgt, mod, W, P, get_w=None, on_grads=None):
    Dm = D_MODEL
    G = {k: [] for k in ("gm_w_in", "gm_w_out", "hg_w_in", "hg_w_out", "ffn_w_up", "ffn_w_down")}
    lb_all = _lb_fwd(P["hg_lb"], "lb_fwd")
    saved = []
    xs = x
    y_prev = gate_prev = None
    layer_w = [None] * DEPTH

    def wmm(xa, kind, i, mode, name):
        if layer_w[i] is not None:
            return _mm(xa, layer_w[i][kind], mode, name)
        return _mm(xa, W[kind], mode, name, b_layer=i if kind.startswith("ffn") else i // 2)

    for i in range(DEPTH):
        m = [_vec(mod[i, j * Dm:(j + 1) * Dm]) for j in range(6)]
        sh1, sc1, g1, sh2, sc2, g2 = m
        j = i // 2
        if get_w is not None:
            layer_w[i] = get_w(i, 0, xs if y_prev is None else y_prev)
        xs, h = _norm_fwd(xs, y_prev, gate_prev, _vec(P["norm_g"][i, 0]), sc1, sh1, f"norm_fwd_a{i}")
        rec = dict(x1=xs, h1=h)
        if i % 2 == 0:
            a = wmm(h, "gm_w_in", i, "nn", f"gm_in{i}")
            gated = _gm_mid_fwd(a, _vec(P["gm_ln_g"][j]), _vec(P["gm_ln_b"][j]), P["gm_w_s"][j],
                                P["gm_b_s"][j].reshape(GM_HEADS, GM_BLOCK, 1), f"gm_mid_fwd{i}")
            if get_w is not None:
                layer_w[i].update(get_w(i, 1, gated))
            y1 = wmm(gated, "gm_w_out", i, "nn", f"gm_out{i}")
            rec.update(a=a, act=gated)
        else:
            p = wmm(h, "hg_w_in", i, "nn", f"hg_in{i}")
            o, og, states = _hg_scan_fwd(p, _vec(lb_all[j]), _vec(P["hg_gn_g"][j]), f"hg_scan_fwd{i}")
            if get_w is not None:
                layer_w[i].update(get_w(i, 1, og))
            y1 = wmm(og, "hg_w_out", i, "nn", f"hg_out{i}")
            rec.update(a=p, act=og, o=o, states=states)
        rec["y1"] = y1
        xs, h2 = _norm_fwd(xs, y1, g1, _vec(P["norm_g"][i, 1]), sc2, sh2, f"norm_fwd_b{i}")
        a2 = wmm(h2, "ffn_w_up", i, "nn", f"ffn_up{i}")
        mm_ = _conv_fwd(a2, P["ffn_conv_w"][i], _vec(P["ffn_conv_b"][i]), f"conv_fwd{i}")
        y2 = wmm(mm_, "ffn_w_down", i, "nn", f"ffn_down{i}")
        rec.update(x2=xs, h2=h2, a2=a2, m=mm_, y2=y2, mods=m)
        saved.append(rec)
        y_prev, gate_prev = y2, g2
    dx, dy, loss, s_fg, s_gate = _loss_head(xs, y_prev, gate_prev, _vec(P["final_g"]), tgt, "loss_head")
    small = dict(final_g=s_fg, norm_g=[None] * DEPTH, dmod=[None] * DEPTH, ffn_conv_w=[None] * DEPTH,
                 ffn_conv_b=[None] * DEPTH, gm_ln_g=[None] * 2, gm_ln_b=[None] * 2, gm_w_s=[None] * 2,
                 gm_b_s=[None] * 2, hg_gn_g=[None] * 2, dlb=[None] * 2)
    for i in reversed(range(DEPTH)):
        rec = saved[i]
        sh1, sc1, g1, sh2, sc2, g2 = rec["mods"]
        j = i // 2
        d_g2 = s_gate
        dm = wmm(dy, "ffn_w_down", i, "nt", f"ffn_down_dx{i}")
        G["ffn_w_down"].append(_mm(rec["m"], dy, "tn", f"ffn_down_dw{i}"))
        da2, dcw, dcb = _conv_bwd(rec["a2"], dm, P["ffn_conv_w"][i], _vec(P["ffn_conv_b"][i]), f"conv_bwd{i}")
        small["ffn_conv_w"][i], small["ffn_conv_b"][i] = dcw, dcb
        dh2 = wmm(da2, "ffn_w_up", i, "nt", f"ffn_up_dx{i}")
        G["ffn_w_up"].append(_mm(rec["h2"], da2, "tn", f"ffn_up_dw{i}", exchange_out=True))
        ng2 = _vec(P["norm_g"][i, 1])
        if on_grads is not None:
            ng2 = ng2 + on_grads(i, {k: G[k][-1] for k in ("ffn_w_up", "ffn_w_down")})
        dx, dy, s_sh2, s_x2, d_g1 = _norm_bwd(rec["x2"], dh2, dx, ng2, sc2, rec["y1"], g1, f"norm_bwd_b{i}")
        d_sc2, d_ng2 = s_x2 * ng2, s_x2 * (1.0 + sc2)
        if i % 2 == 0:
            dgated = wmm(dy, "gm_w_out", i, "nt", f"gm_out_dx{i}")
            G["gm_w_out"].append(_mm(rec["act"], dy, "tn", f"gm_out_dw{i}"))
            da, dws, dbs, dlg, dlbeta = _gm_mid_bwd(
                rec["a"], dgated, _vec(P["gm_ln_g"][j]), _vec(P["gm_ln_b"][j]), P["gm_w_s"][j],
                P["gm_b_s"][j].reshape(GM_HEADS, GM_BLOCK, 1), f"gm_mid_bwd{i}")
            small["gm_w_s"][j], small["gm_b_s"][j] = dws, dbs[:, :GM_HEADS].T
            small["gm_ln_g"][j], small["gm_ln_b"][j] = dlg, dlbeta
            dh1 = wmm(da, "gm_w_in", i, "nt", f"gm_in_dx{i}")
            G["gm_w_in"].append(_mm(rec["h1"], da, "tn", f"gm_in_dw{i}", exchange_out=True))
        else:
            dog = wmm(dy, "hg_w_out", i, "nt", f"hg_out_dx{i}")
            G["hg_w_out"].append(_mm(rec["act"], dy, "tn", f"hg_out_dw{i}"))
            dp, dlb, dgn = _hg_scan_bwd(rec["a"], _vec(lb_all[j]), _vec(P["hg_gn_g"][j]), rec["o"], dog,
                                        rec["states"], f"hg_scan_bwd{i}")
            small["dlb"][j], small["hg_gn_g"][j] = dlb, dgn
            dh1 = wmm(dp, "hg_w_in", i, "nt", f"hg_in_dx{i}")
            G["hg_w_in"].append(_mm(rec["h1"], dp, "tn", f"hg_in_dw{i}", exchange_out=True))
        ng1 = _vec(P["norm_g"][i, 0])
        if on_grads is not None:
            mixer = ("gm_w_in", "gm_w_out") if i % 2 == 0 else ("hg_w_in", "hg_w_out")
            ng1 = ng1 + on_grads(i, {k: G[k][-1] for k in mixer})
        if i > 0:
            prev = saved[i - 1]
            dx, dy, s_sh1, s_x1, s_gate = _norm_bwd(rec["x1"], dh1, dx, ng1, sc1, prev["y2"], prev["mods"][5],
                                                    f"norm_bwd_a{i}")
        else:
            dx, s_sh1, s_x1 = _norm_bwd(rec["x1"], dh1, dx, ng1, sc1, None, None, f"norm_bwd_a{i}")
        d_sc1, d_ng1 = s_x1 * ng1, s_x1 * (1.0 + sc1)
        small["norm_g"][i] = jnp.concatenate([d_ng1, d_ng2], axis=0)
        small["dmod"][i] = jnp.concatenate([s_sh1, d_sc1, d_g1, s_sh2, d_sc2, d_g2], axis=1)
    for k in G:
        G[k] = G[k][::-1]
    dlb_all = jnp.concatenate(small.pop("dlb"), axis=0)
    small["hg_lb"] = _lb_bwd(P["hg_lb"], dlb_all, "lb_bwd")
    return loss, dx, G, small


BIG = ("gm_w_in", "gm_w_out", "hg_w_in", "hg_w_out", "ffn_w_up", "ffn_w_down")
COL_SHARDED = dict(gm_w_in=True, gm_w_out=False, hg_w_in=True, hg_w_out=False, ffn_w_up=True, ffn_w_down=False)
LAYER_WEIGHTS = 4


def _layer_kinds(i):
    return (("gm_w_in", "gm_w_out") if i % 2 == 0 else ("hg_w_in", "hg_w_out")) + ("ffn_w_up", "ffn_w_down")


def _pack(pieces):
    flat = [p.reshape(-1).astype(F32) for p in pieces]
    offs, tot = [], 0
    for f in flat:
        offs.append((tot, f.shape[0]))
        tot += f.shape[0]
    padded = -(-tot // (8 * LANES)) * (8 * LANES)
    if padded > tot:
        flat.append(jnp.zeros((padded - tot,), F32))
    return jnp.concatenate(flat).reshape(-1, LANES), offs


def _unpack(rows, offs, shapes):
    lead = rows.shape[:-2]
    flat = rows.reshape(lead + (-1,))
    return [flat[..., o:o + n].reshape(lead + tuple(s)) for (o, n), s in zip(offs, shapes)]


def _from_chips(per_dev, axis):
    per_chip = per_dev[0::2]
    return jnp.concatenate([per_chip[s] for s in range(N_CHIPS)], axis=axis)


def kernel(x, c, gm_w_in, gm_ln_g, gm_ln_b, gm_w_s, gm_b_s, gm_w_out, hg_w_in, hg_lb, hg_gn_g, hg_w_out, ffn_w_up, ffn_conv_w, ffn_conv_b, ffn_w_down, norm_g, ada_w, ada_b, final_g, loss_target, m_gm_w_in, m_gm_ln_g, m_gm_ln_b, m_gm_w_s, m_gm_b_s, m_gm_w_out, m_hg_w_in, m_hg_lb, m_hg_gn_g, m_hg_w_out, m_ffn_w_up, m_ffn_conv_w, m_ffn_conv_b, m_ffn_w_down, m_norm_g, m_ada_w, m_ada_b, m_final_g, v_gm_w_in, v_gm_ln_g, v_gm_ln_b, v_gm_w_s, v_gm_b_s, v_gm_w_out, v_hg_w_in, v_hg_lb, v_hg_gn_g, v_hg_w_out, v_ffn_w_up, v_ffn_conv_w, v_ffn_conv_b, v_ffn_w_down, v_norm_g, v_ada_w, v_ada_b, v_final_g):
    Dm = D_MODEL
    xi, yi, ci = _position()
    chip = 2 * xi + yi
    dev = 4 * xi + 2 * yi + ci
    weights = dict(gm_w_in=gm_w_in, gm_ln_g=gm_ln_g, gm_ln_b=gm_ln_b, gm_w_s=gm_w_s, gm_b_s=gm_b_s,
                   gm_w_out=gm_w_out, hg_w_in=hg_w_in, hg_lb=hg_lb, hg_gn_g=hg_gn_g, hg_w_out=hg_w_out,
                   ffn_w_up=ffn_w_up, ffn_conv_w=ffn_conv_w, ffn_conv_b=ffn_conv_b, ffn_w_down=ffn_w_down,
                   norm_g=norm_g, ada_w=ada_w, ada_b=ada_b, final_g=final_g)
    mom_m = dict(gm_w_in=m_gm_w_in, gm_ln_g=m_gm_ln_g, gm_ln_b=m_gm_ln_b, gm_w_s=m_gm_w_s, gm_b_s=m_gm_b_s,
                 gm_w_out=m_gm_w_out, hg_w_in=m_hg_w_in, hg_lb=m_hg_lb, hg_gn_g=m_hg_gn_g, hg_w_out=m_hg_w_out,
                 ffn_w_up=m_ffn_w_up, ffn_conv_w=m_ffn_conv_w, ffn_conv_b=m_ffn_conv_b, ffn_w_down=m_ffn_w_down,
                 norm_g=m_norm_g, ada_w=m_ada_w, ada_b=m_ada_b, final_g=m_final_g)
    mom_v = dict(gm_w_in=v_gm_w_in, gm_ln_g=v_gm_ln_g, gm_ln_b=v_gm_ln_b, gm_w_s=v_gm_w_s, gm_b_s=v_gm_b_s,
                 gm_w_out=v_gm_w_out, hg_w_in=v_hg_w_in, hg_lb=v_hg_lb, hg_gn_g=v_hg_gn_g, hg_w_out=v_hg_w_out,
                 ffn_w_up=v_ffn_w_up, ffn_conv_w=v_ffn_conv_w, ffn_conv_b=v_ffn_conv_b, ffn_w_down=v_ffn_w_down,
                 norm_g=v_norm_g, ada_w=v_ada_w, ada_b=v_ada_b, final_g=v_final_g)
    order = list(weights)

    pos = jnp.stack([chip, ci]).astype(jnp.int32)
    shards, by_col = [], []
    for i in range(DEPTH):
        for k in _layer_kinds(i):
            shards.append(weights[k][i if k.startswith("ffn") else i // 2].astype(BF16))
            by_col.append(COL_SHARDED[k])
    placed = [_place_own(sh, col, pos, f"place_own{n}") for n, (sh, col) in enumerate(zip(shards, by_col))]
    gsems, sh_thru, ld_thru, _ = _gather_start(shards, placed, by_col, LAYER_WEIGHTS, "gather_start")

    pieces = [c, hg_lb, hg_gn_g, norm_g, ffn_conv_w]
    payload, offs = _pack(pieces)
    got = _allgather_small(payload, "gather_small")
    c_g, lb_g, gn_g, ng_g, cw_g = _unpack(got, offs, [p.shape for p in pieces])
    c_all = c_g.reshape(N_DEV, Dm)
    P = dict(hg_lb=_from_chips(lb_g, 1), hg_gn_g=_from_chips(gn_g, 1), norm_g=_from_chips(ng_g, 2),
             ffn_conv_w=_from_chips(cw_g, 2), gm_ln_g=gm_ln_g, gm_ln_b=gm_ln_b, gm_w_s=gm_w_s, gm_b_s=gm_b_s,
             ffn_conv_b=ffn_conv_b, final_g=final_g)

    cols = ada_w.shape[2]
    ada_b_sh = lax.dynamic_slice_in_dim(ada_b, chip * cols, cols, axis=1)
    mod_sh = _ada_fwd(c_all, ada_w, ada_b_sh, "ada_fwd")
    mod_g = _allgather_small(mod_sh.reshape(-1, LANES), "gather_mod").reshape(N_DEV, DEPTH, N_DEV, cols)
    mod_mine = lax.dynamic_index_in_dim(mod_g[0::2], dev, axis=2, keepdims=False)
    mod = jnp.transpose(mod_mine, (1, 0, 2)).reshape(DEPTH, N_CHIPS * cols)

    core = jnp.reshape(ci, (1,)).astype(jnp.int32)
    chip_arr = jnp.reshape(chip, (1,)).astype(jnp.int32)
    pending, held = [], {}

    def get_w(i, group, after):
        lo, hi = LAYER_WEIGHTS * i, LAYER_WEIGHTS * (i + 1)
        if i == 0:
            s = slice(lo, lo + 1) if group == 0 else slice(lo + 1, hi)
        elif group == 0:
            s = slice(lo, hi)
        else:
            return {}
        landed = _gather_wait(sh_thru[s], ld_thru[s], gsems[2 * i], gsems[2 * i + 1], after, by_col[s],
                              s.start - lo, f"gather_wait{i}_{group}")
        full = _forward_sibling(landed, by_col[s], [a.shape for a in shards[s]], f"gather_forward{i}_{group}")
        return dict(zip(_layer_kinds(i)[s.start - lo:s.stop - lo], full))

    def on_grads(i, gdict):
        if i > 0 and "ffn_w_up" in gdict:
            held[i] = gdict
            return 0.0
        gdict = {**held.pop(i, {}), **gdict}
        kinds = [k for k in _layer_kinds(i) if k in gdict]
        tag = f"{i}_ffn" if kinds[0] == "ffn_w_up" else f"{i}"
        g4 = []
        for k in kinds:
            g = gdict[k]
            if not COL_SHARDED[k]:
                R, C = g.shape
                g = g.reshape(N_CHIPS, 2, R // (2 * N_CHIPS), C)
            g4.append(g)
        to_sib = [lax.dynamic_index_in_dim(g, 1 - ci, axis=1, keepdims=False).astype(BF16) for g in g4]
        from_sib = _swap_sibling(to_sib, f"reduce_swap{tag}")
        sums = [_add_own_half(g, r, core, f"chip_sum_{k}{i}") for g, r, k in zip(g4, from_sib, kinds)]
        send, recv, sums_thru, lands, token = _exchange_start(sums, f"reduce_start{tag}")
        pending.append((tag, i, kinds, send, recv, sums_thru, lands))
        return token[0, 0]

    loss_part, dx, G, small = _local_step(x[0], loss_target[0], mod, None, P, get_w, on_grads)

    sum_pieces = [loss_part[:, :1], small["final_g"], jnp.stack(small["gm_ln_g"]), jnp.stack(small["gm_ln_b"]),
                  jnp.stack(small["gm_w_s"]), jnp.stack(small["gm_b_s"]), jnp.stack(small["ffn_conv_b"]),
                  small["hg_lb"], jnp.stack(small["hg_gn_g"]), jnp.stack(small["norm_g"]),
                  jnp.stack(small["ffn_conv_w"])]
    dmod = jnp.concatenate(small["dmod"], axis=0)
    payload2, offs2 = _pack(sum_pieces + [dmod])
    placed2 = _place_row(payload2, jnp.reshape(dev, (1,)).astype(jnp.int32), "place_grads")
    bsend, brecv, p2_thru, l2_thru, small_token = _broadcast_start(payload2, placed2, "gather_grads_start")

    landed = {}
    for tag, i, kinds, send, recv, sums_thru, lands in pending:
        sums_i, lands_i = _exchange_wait(sums_thru, lands, send, recv, small_token, f"reduce_wait{tag}")
        for k, s_, l_ in zip(kinds, sums_i, lands_i):
            landed[(k, i)] = (l_, s_)
    own_halves = []
    for k in BIG:
        layers = [landed[(k, i)] for i in range(DEPTH) if (k, i) in landed]
        own_halves.append(_sum_chips([l_ for l_, _ in layers], [s_ for _, s_ in layers], chip_arr, f"sum_chips_{k}"))
    sib_halves = _swap_sibling(own_halves, "reduce_join")
    grads, deltas, new_m, new_v = {}, {}, {}, {}
    for k, own, recv in zip(BIG, own_halves, sib_halves):
        grads[k], deltas[k], new_m[k], new_v[k] = _adamw_halves(
            weights[k], own, recv, mom_m[k], mom_v[k], core, f"adamw_{k}")

    got2 = _broadcast_wait(p2_thru, l2_thru, bsend, brecv, new_v[BIG[-1]], "gather_grads_wait")
    dmod_all = _unpack(got2, offs2[-1:], [dmod.shape])[0]
    summed = _sum_devices(got2, "sum_devices")
    (loss_s, d_final_g, d_ln_g, d_ln_b, d_ws, d_bs, d_cb, d_lb, d_gn, d_ng, d_cw) = _unpack(
        summed, offs2[:-1], [(1,), final_g.shape, gm_ln_g.shape, gm_ln_b.shape, gm_w_s.shape, gm_b_s.shape,
                             ffn_conv_b.shape, (2, Dm), (2, Dm), (DEPTH, 2, Dm), (DEPTH, 3, 2 * FFN_HIDDEN)])
    grads.update(final_g=d_final_g, gm_ln_g=d_ln_g, gm_ln_b=d_ln_b, gm_w_s=d_ws, gm_b_s=d_bs, ffn_conv_b=d_cb)
    grads["hg_lb"] = lax.dynamic_slice_in_dim(d_lb, chip * hg_lb.shape[1], hg_lb.shape[1], axis=1)
    grads["hg_gn_g"] = lax.dynamic_slice_in_dim(d_gn, chip * hg_gn_g.shape[1], hg_gn_g.shape[1], axis=1)
    grads["norm_g"] = lax.dynamic_slice_in_dim(d_ng, chip * norm_g.shape[2], norm_g.shape[2], axis=2)
    grads["ffn_conv_w"] = lax.dynamic_slice_in_dim(d_cw, chip * ffn_conv_w.shape[2], ffn_conv_w.shape[2], axis=2)
    dmod_sh = lax.dynamic_slice_in_dim(dmod_all, chip * cols, cols, axis=2)
    grads["ada_w"] = _ada_bwd(c_all, jnp.transpose(dmod_sh, (1, 0, 2)), "ada_bwd")
    grads["ada_b"] = _sum_devices(dmod_all.reshape(N_DEV, -1, LANES), "sum_ada_b").reshape(ada_b.shape)

    for k in order:
        if k in BIG:
            continue
        w = weights[k]
        shp = w.shape
        view = (-1, shp[-1]) if w.ndim > 1 else (8, -1)
        d, m2, v2 = _adamw(w.reshape(view), grads[k].reshape(view), mom_m[k].reshape(view), mom_v[k].reshape(view),
                           f"adamw_{k}")
        deltas[k], new_m[k], new_v[k] = d.reshape(shp), m2.reshape(shp), v2.reshape(shp)
        grads[k] = grads[k].reshape(shp)

    loss = loss_s.reshape(())
    return (loss, dx[None], *[grads[k] for k in order], *[deltas[k] for k in order],
            *[new_m[k] for k in order], *[new_v[k] for k in order])
```

```python
import functools

import jax
import jax.numpy as jnp
from jax import lax
from jax.experimental import pallas as pl
from jax.experimental.pallas import tpu as pltpu

F32 = jnp.float32
BF16 = jnp.bfloat16
HI = lax.Precision.HIGHEST
X3 = lax.Precision.HIGH
GRAD_WIRE = BF16
MESH = pl.DeviceIdType.MESH

D_MODEL = 1024
DEPTH = 4
EPS = 1e-6
GM_WIDTH = 2048
GM_HEADS = 8
GM_BLOCK = 128
GM_HEAD_DIM = 256
CHUNK = 64
HG_HEADS = 8
HG_DIM = 128
FFN_HIDDEN = 2816
N_CHIPS = 4
N_DEV = 8

ADAM_LR = 0.001
ADAM_B1 = 0.9
ADAM_B2 = 0.999
ADAM_EPS = 1e-08
ADAM_WD = 0.01
ADAM_STEP = 10

VMEM_LIMIT_BYTES = 56 * 1024 * 1024
ROW_TILE = 256
LANES = 128

_SQRT_HALF = 0.7071067811865476
_INV_SQRT_2PI = 0.3989422804014327


def _pick(dim, prefs):
    for p in prefs:
        if dim % p == 0:
            return p
    return dim


def _params(sem):
    return pltpu.CompilerParams(dimension_semantics=sem, vmem_limit_bytes=VMEM_LIMIT_BYTES)


def _cdf(x):
    return 0.5 * (1.0 + lax.erf(x * _SQRT_HALF))


def _pdf(x):
    return jnp.exp(-0.5 * x * x) * _INV_SQRT_2PI


def _sig(x):
    return jax.nn.sigmoid(x)


def _dot(a, b, dims, prec=None):
    return lax.dot_general(a, b, (dims, ((), ())), precision=prec, preferred_element_type=F32)


NN = ((1,), (0,))
NT = ((1,), (1,))
TN = ((0,), (0,))


MM_VMEM_BUDGET = 40 * 1024 * 1024


def _mm_tiles(mode, M, N, K, a_bytes, b_bytes, exchange_out):
    tn = _pick(N, (1408, 1024, 512, 256, 128))
    tms = [t for t in (1408, 1024, 512, 256, 128) if M % t == 0 and not (exchange_out and (M // 2) % t)] or [M]
    tks = [K] + [t for t in (2816, 2048, 1408, 1024, 512, 256, 128) if t < K and K % t == 0]

    def fits(tm, tk):
        acc = tm * tn * 4 if tk < K else 0
        return 2 * tm * tk * a_bytes + 2 * tk * tn * b_bytes + 2 * tm * tn * 4 + acc <= MM_VMEM_BUDGET

    for min_tm in (min(512, tms[0]), 0):
        for tk in tks:
            for tm in tms:
                if tm >= min_tm and fits(tm, tk):
                    return tm, tn, tk
    return tms[-1], tn, tks[-1]


def _mm(a, b, mode, name, b_layer=None, out_dtype=F32, exchange_out=False):
    b2 = b.shape[-2:]
    if mode == "nn":
        (M, K), (_, N) = a.shape, b2
    elif mode == "nt":
        (M, K), (N, _) = a.shape, b2
    else:
        (K, M), (_, N) = a.shape, b2
    tm, tn, tk = _mm_tiles(mode, M, N, K, a.dtype.itemsize, b.dtype.itemsize, exchange_out)
    nk = K // tk
    dims = {"nn": NN, "nt": NT, "tn": TN}[mode]

    def body(a_ref, b_ref, o_ref, *scratch):
        part = _dot(a_ref[...].astype(BF16), b_ref[...].astype(BF16), dims)
        if nk == 1:
            o_ref[...] = part.astype(o_ref.dtype)
            return
        (acc_ref,) = scratch
        k = pl.program_id(2)

        @pl.when(k == 0)
        def _():
            acc_ref[...] = part

        @pl.when(k > 0)
        def _():
            acc_ref[...] += part

        @pl.when(k == nk - 1)
        def _():
            o_ref[...] = acc_ref[...].astype(o_ref.dtype)

    if mode == "tn":
        a_spec = pl.BlockSpec((tk, tm), lambda i, j, k: (k, i))
    else:
        a_spec = pl.BlockSpec((tm, tk), lambda i, j, k: (i, k))
    bblk = (tk, tn) if mode in ("nn", "tn") else (tn, tk)
    bidx = (lambda i, j, k: (k, j)) if mode in ("nn", "tn") else (lambda i, j, k: (j, k))
    if b_layer is None:
        b_spec = pl.BlockSpec(bblk, bidx)
    else:
        b_spec = pl.BlockSpec((None,) + bblk, lambda i, j, k: (b_layer,) + bidx(i, j, k))
    if exchange_out:
        mh, cw = M // 2, N // N_CHIPS
        assert mh % tm == 0 and cw % tn == 0
        out_shape = jax.ShapeDtypeStruct((N_CHIPS, 2, mh, cw), out_dtype)
        o_spec = pl.BlockSpec(
            (None, None, tm, tn),
            lambda i, j, k: (j // (cw // tn), i // (mh // tm), i % (mh // tm), j % (cw // tn)))
    else:
        out_shape = jax.ShapeDtypeStruct((M, N), out_dtype)
        o_spec = pl.BlockSpec((tm, tn), lambda i, j, k: (i, j))
    return pl.pallas_call(
        body, name=name, out_shape=out_shape, grid=(M // tm, N // tn, nk),
        in_specs=[a_spec, b_spec], out_specs=o_spec,
        scratch_shapes=[] if nk == 1 else [pltpu.VMEM((tm, tn), F32)],
        compiler_params=_params(("parallel", "parallel", "arbitrary")),
    )(a, b)


def _row_spec(tr, width):
    return pl.BlockSpec((tr, width), lambda i: (i, 0))


def _vec_spec(width, rows=1):
    return pl.BlockSpec((rows, width), lambda i: (0, 0))


def _norm_fwd(x, y, gate, g, sc, sh, name):
    T, Dm = x.shape
    tr = _pick(T, (ROW_TILE,))
    has_res = y is not None

    def body(*refs):
        if has_res:
            x_ref, y_ref, gate_ref, g_ref, sc_ref, sh_ref, xo_ref, h_ref = refs
            xv = x_ref[...] + gate_ref[...] * y_ref[...]
            xo_ref[...] = xv
        else:
            x_ref, g_ref, sc_ref, sh_ref, h_ref = refs
            xv = x_ref[...]
        rstd = lax.rsqrt(jnp.mean(xv * xv, axis=-1, keepdims=True) + EPS)
        h_ref[...] = ((xv * rstd * g_ref[...]) * (1.0 + sc_ref[...]) + sh_ref[...]).astype(BF16)

    row, vec = _row_spec(tr, Dm), _vec_spec(Dm)
    if has_res:
        ins, in_specs = (x, y, gate, g, sc, sh), [row, row, vec, vec, vec, vec]
        out_shape = (jax.ShapeDtypeStruct((T, Dm), F32), jax.ShapeDtypeStruct((T, Dm), BF16))
        out_specs = (row, row)
    else:
        ins, in_specs = (x, g, sc, sh), [row, vec, vec, vec]
        out_shape = jax.ShapeDtypeStruct((T, Dm), BF16)
        out_specs = row
    out = pl.pallas_call(body, name=name, out_shape=out_shape, grid=(T // tr,), in_specs=in_specs,
                         out_specs=out_specs, compiler_params=_params(("parallel",)))(*ins)
    return out if has_res else (x, out)


def _norm_bwd(x, dh, dxo, g, sc, y_prev, gate_prev, name):
    T, Dm = x.shape
    tr = _pick(T, (ROW_TILE,))
    has_prev = y_prev is not None

    def body(*refs):
        if has_prev:
            x_ref, dh_ref, dxo_ref, g_ref, sc_ref, yp_ref, gp_ref, dx_ref, dyp_ref, s1_ref, s2_ref, s3_ref = refs
        else:
            x_ref, dh_ref, dxo_ref, g_ref, sc_ref, dx_ref, s1_ref, s2_ref = refs

        @pl.when(pl.program_id(0) == 0)
        def _():
            s1_ref[...] = jnp.zeros_like(s1_ref)
            s2_ref[...] = jnp.zeros_like(s2_ref)
            if has_prev:
                s3_ref[...] = jnp.zeros_like(s3_ref)

        xv = x_ref[...]
        rstd = lax.rsqrt(jnp.mean(xv * xv, axis=-1, keepdims=True) + EPS)
        xhat = xv * rstd
        dh = dh_ref[...]
        dxhat = dh * (g_ref[...] * (1.0 + sc_ref[...]))
        dx = dxo_ref[...] + rstd * (dxhat - xhat * jnp.mean(dxhat * xhat, axis=-1, keepdims=True))
        dx_ref[...] = dx
        s1_ref[...] += jnp.sum(dh, axis=0, keepdims=True)
        s2_ref[...] += jnp.sum(dh * xhat, axis=0, keepdims=True)
        if has_prev:
            dyp_ref[...] = (gp_ref[...] * dx).astype(BF16)
            s3_ref[...] += jnp.sum(dx * yp_ref[...], axis=0, keepdims=True)

    row, vec = _row_spec(tr, Dm), _vec_spec(Dm)
    vshape = jax.ShapeDtypeStruct((1, Dm), F32)
    if has_prev:
        ins, in_specs = (x, dh, dxo, g, sc, y_prev, gate_prev), [row, row, row, vec, vec, row, vec]
        out_shape = (jax.ShapeDtypeStruct((T, Dm), F32), jax.ShapeDtypeStruct((T, Dm), BF16), vshape, vshape, vshape)
        out_specs = (row, row, vec, vec, vec)
    else:
        ins, in_specs = (x, dh, dxo, g, sc), [row, row, row, vec, vec]
        out_shape = (jax.ShapeDtypeStruct((T, Dm), F32), vshape, vshape)
        out_specs = (row, vec, vec)
    return pl.pallas_call(body, name=name, out_shape=out_shape, grid=(T // tr,), in_specs=in_specs,
                          out_specs=out_specs, compiler_params=_params(("arbitrary",)))(*ins)


def _loss_head(x, y, gate, fg, tgt, name):
    T, Dm = x.shape
    tr = _pick(T, (ROW_TILE,))
    nsteps = T // tr

    def body(x_ref, y_ref, gate_ref, fg_ref, t_ref, dx_ref, dy_ref, loss_ref, sfg_ref, sg_ref, acc_ref):
        i = pl.program_id(0)

        @pl.when(i == 0)
        def _():
            acc_ref[...] = jnp.zeros_like(acc_ref)
            sfg_ref[...] = jnp.zeros_like(sfg_ref)
            sg_ref[...] = jnp.zeros_like(sg_ref)

        yv = y_ref[...]
        xv = x_ref[...] + gate_ref[...] * yv
        rstd = lax.rsqrt(jnp.mean(xv * xv, axis=-1, keepdims=True) + EPS)
        xhat = xv * rstd
        err = xhat * fg_ref[...] - t_ref[...]
        acc_ref[...] += jnp.sum(err * err, axis=0, keepdims=True)
        dyn = err * (1.0 / Dm)
        sfg_ref[...] += jnp.sum(dyn * xhat, axis=0, keepdims=True)
        dxhat = dyn * fg_ref[...]
        dx = rstd * (dxhat - xhat * jnp.mean(dxhat * xhat, axis=-1, keepdims=True))
        dx_ref[...] = dx
        dy_ref[...] = (gate_ref[...] * dx).astype(BF16)
        sg_ref[...] += jnp.sum(dx * yv, axis=0, keepdims=True)

        @pl.when(i == nsteps - 1)
        def _():
            total = jnp.sum(acc_ref[...], axis=1, keepdims=True) * (0.5 / Dm)
            loss_ref[...] = jnp.broadcast_to(total, loss_ref.shape)

    row, vec = _row_spec(tr, Dm), _vec_spec(Dm)
    vshape = jax.ShapeDtypeStruct((1, Dm), F32)
    return pl.pallas_call(
        body, name=name, grid=(nsteps,),
        out_shape=(jax.ShapeDtypeStruct((T, Dm), F32), jax.ShapeDtypeStruct((T, Dm), BF16),
                   jax.ShapeDtypeStruct((1, LANES), F32), vshape, vshape),
        in_specs=[row, row, vec, vec, row], out_specs=(row, row, _vec_spec(LANES), vec, vec),
        scratch_shapes=[pltpu.VMEM((1, Dm), F32)], compiler_params=_params(("arbitrary",)),
    )(x, y, gate, fg, tgt)


def _spatial_mask():
    r = lax.broadcasted_iota(jnp.int32, (GM_BLOCK, GM_BLOCK), 0) // CHUNK
    c = lax.broadcasted_iota(jnp.int32, (GM_BLOCK, GM_BLOCK), 1) // CHUNK
    return r >= c


def _gm_specs(tr):
    return [_row_spec(tr, 2 * GM_WIDTH), _vec_spec(GM_WIDTH), _vec_spec(GM_WIDTH),
            pl.BlockSpec((GM_HEADS, GM_BLOCK, GM_BLOCK), lambda i: (0, 0, 0)),
            pl.BlockSpec((GM_HEADS, GM_BLOCK, 1), lambda i: (0, 0, 0))]


def _gm_mid_fwd(a, ln_g, ln_b, ws, bs3, name):
    T = a.shape[0]
    tr = _pick(T, (ROW_TILE,))
    W = GM_WIDTH

    def body(a_ref, lg_ref, lb_ref, ws_ref, bs_ref, o_ref, vn_scr):
        av = a_ref[:, W:]
        v = av * _cdf(av)
        vc = v - jnp.mean(v, axis=-1, keepdims=True)
        rstd = lax.rsqrt(jnp.mean(vc * vc, axis=-1, keepdims=True) + EPS)
        vn_scr[...] = (vc * rstd * lg_ref[...] + lb_ref[...]).astype(BF16)
        mask = _spatial_mask()
        for h in range(GM_HEADS):
            w = jnp.where(mask, ws_ref[h], 0.0).astype(BF16)
            cs = slice(h * GM_HEAD_DIM, (h + 1) * GM_HEAD_DIM)
            for blk in range(tr // GM_BLOCK):
                rs = slice(blk * GM_BLOCK, (blk + 1) * GM_BLOCK)
                s = _dot(w, vn_scr[rs, cs], NN) + bs_ref[h]
                au = a_ref[rs, cs]
                o_ref[rs, cs] = (au * _cdf(au) * s).astype(BF16)

    return pl.pallas_call(
        body, name=name, out_shape=jax.ShapeDtypeStruct((T, W), BF16), grid=(T // tr,),
        in_specs=_gm_specs(tr), out_specs=_row_spec(tr, W),
        scratch_shapes=[pltpu.VMEM((tr, W), BF16)], compiler_params=_params(("parallel",)),
    )(a, ln_g, ln_b, ws, bs3)


def _gm_mid_bwd(a, dgated, ln_g, ln_b, ws, bs3, name):
    T = a.shape[0]
    tr = _pick(T, (ROW_TILE,))
    W = GM_WIDTH
    nsteps = T // tr

    def body(a_ref, dg_ref, lg_ref, lb_ref, ws_ref, bs_ref, da_ref, dws_ref, dbs_ref, dlg_ref, dlb_ref,
             vn_scr, vhat_scr, dvn_scr, dsum_scr):
        i = pl.program_id(0)

        @pl.when(i == 0)
        def _():
            dws_ref[...] = jnp.zeros_like(dws_ref)
            dbs_ref[...] = jnp.zeros_like(dbs_ref)
            dlg_ref[...] = jnp.zeros_like(dlg_ref)
            dlb_ref[...] = jnp.zeros_like(dlb_ref)
            dsum_scr[...] = jnp.zeros_like(dsum_scr)

        av = a_ref[:, W:]
        cdf_v = _cdf(av)
        v = av * cdf_v
        vc = v - jnp.mean(v, axis=-1, keepdims=True)
        rstd = lax.rsqrt(jnp.mean(vc * vc, axis=-1, keepdims=True) + EPS)
        vhat_scr[...] = vc * rstd
        vn_scr[...] = (vhat_scr[...] * lg_ref[...] + lb_ref[...]).astype(BF16)
        mask = _spatial_mask()
        for h in range(GM_HEADS):
            w = jnp.where(mask, ws_ref[h], 0.0).astype(BF16)
            cs = slice(h * GM_HEAD_DIM, (h + 1) * GM_HEAD_DIM)
            for blk in range(tr // GM_BLOCK):
                rs = slice(blk * GM_BLOCK, (blk + 1) * GM_BLOCK)
                vnb = vn_scr[rs, cs]
                s = _dot(w, vnb, NN) + bs_ref[h]
                au = a_ref[rs, cs]
                cdf_u = _cdf(au)
                dg = dg_ref[rs, cs]
                ds = dg * (au * cdf_u)
                da_ref[rs, cs] = (dg * s * (cdf_u + au * _pdf(au))).astype(BF16)
                dsb = ds.astype(BF16)
                dvn_scr[rs, cs] = _dot(w, dsb, TN)
                dws_ref[h] += _dot(dsb, vnb, NT)
                dsum_scr[:, cs] += ds
        dvn = dvn_scr[...]
        vhat = vhat_scr[...]
        dlg_ref[...] += jnp.sum(dvn * vhat, axis=0, keepdims=True)
        dlb_ref[...] += jnp.sum(dvn, axis=0, keepdims=True)
        dvh = dvn * lg_ref[...]
        dv = rstd * (dvh - jnp.mean(dvh, axis=-1, keepdims=True)
                     - vhat * jnp.mean(dvh * vhat, axis=-1, keepdims=True))
        da_ref[:, W:] = (dv * (cdf_v + av * _pdf(av))).astype(BF16)

        @pl.when(i == nsteps - 1)
        def _():
            for h in range(GM_HEADS):
                dws_ref[h] = jnp.where(mask, dws_ref[h], 0.0)
            col_head = lax.broadcasted_iota(jnp.int32, (W, GM_BLOCK), 0) // GM_HEAD_DIM
            sel = (col_head == lax.broadcasted_iota(jnp.int32, (W, GM_BLOCK), 1)).astype(F32)
            dbs_ref[...] = _dot(dsum_scr[...], sel, NN, HI)

    vshape = jax.ShapeDtypeStruct((1, W), F32)
    return pl.pallas_call(
        body, name=name, grid=(nsteps,),
        out_shape=(jax.ShapeDtypeStruct((T, 2 * W), BF16), jax.ShapeDtypeStruct((GM_HEADS, GM_BLOCK, GM_BLOCK), F32),
                   jax.ShapeDtypeStruct((GM_BLOCK, GM_BLOCK), F32), vshape, vshape),
        in_specs=[_gm_specs(tr)[0], _row_spec(tr, W)] + _gm_specs(tr)[1:],
        out_specs=(_row_spec(tr, 2 * W), pl.BlockSpec((GM_HEADS, GM_BLOCK, GM_BLOCK), lambda i: (0, 0, 0)),
                   pl.BlockSpec((GM_BLOCK, GM_BLOCK), lambda i: (0, 0)), _vec_spec(W), _vec_spec(W)),
        scratch_shapes=[pltpu.VMEM((tr, W), BF16), pltpu.VMEM((tr, W), F32), pltpu.VMEM((tr, W), F32),
                        pltpu.VMEM((GM_BLOCK, W), F32)],
        compiler_params=_params(("arbitrary",)),
    )(a, dgated, ln_g, ln_b, ws, bs3)


SUB = 16
EXP_CLAMP = 80.0


def _tri(lower):
    r = lax.broadcasted_iota(jnp.int32, (CHUNK, CHUNK), 0)
    c = lax.broadcasted_iota(jnp.int32, (CHUNK, CHUNK), 1)
    return (r >= c) if lower else (c >= r)


def _score_masks():
    i = lax.broadcasted_iota(jnp.int32, (CHUNK, CHUNK), 0)
    j = lax.broadcasted_iota(jnp.int32, (CHUNK, CHUNK), 1)
    bi, bj = i // SUB, j // SUB
    diag = (bi == bj) & (i >= j)
    pair = (bi % 2 == 1) & (bj == bi - 1)
    half = (i >= CHUNK // 2) & (j < CHUNK // 2)
    return diag, pair, half


def _dot01(m, x):
    x1 = x.astype(BF16)
    rest = x - x1.astype(F32)
    x2 = rest.astype(BF16)
    x3 = (rest - x2.astype(F32)).astype(BF16)
    return _dot(m, x1, NN) + (_dot(m, x2, NN) + _dot(m, x3, NN))


def _block_rows(b, offset):
    parts = []
    for blk in range(0, CHUNK, SUB):
        r = blk + offset
        parts.append(jnp.zeros((SUB, b.shape[1]), F32) if r < 0 else jnp.broadcast_to(b[r:r + 1], (SUB, b.shape[1])))
    return jnp.concatenate(parts, axis=0)


def _hg_gates(p_ref, lb_ref, lower):
    Dm = D_MODEL
    heads = []
    for h in range(HG_HEADS):
        c0 = h * HG_DIM
        qr = p_ref[:, c0:c0 + HG_DIM]
        fz = p_ref[:, Dm + c0:Dm + c0 + HG_DIM]
        lbh = lb_ref[:, c0:c0 + HG_DIM]
        sg = _sig(fz)
        f = lbh + (1.0 - lbh) * sg
        sq = _sig(qr)
        heads.append(dict(qr=qr, v=p_ref[:, 2 * Dm + c0:2 * Dm + c0 + HG_DIM],
                          gt=p_ref[:, 3 * Dm + c0:3 * Dm + c0 + HG_DIM], lbh=lbh, sg=sg, f=f, gl=jnp.log(f),
                          kk=1.0 - f, sq=sq, q=qr * sq))
    for g in heads:
        g["b"] = _dot01(lower, g.pop("gl"))
    for g in heads:
        g.update(_hg_scalings(g["q"], g["kk"], g.pop("b")))
    return heads


def _hg_scalings(q, kk, b):
    r_mid = _block_rows(b, SUB // 2 - 1)
    r_prev = _block_rows(b, -1)
    r_end = _block_rows(b, SUB - 1)
    r_half = jnp.broadcast_to(b[CHUNK // 2 - 1:CHUNK // 2], b.shape)
    bc = b[CHUNK - 1:CHUNK]
    eqs = (jnp.exp(jnp.clip(b - r_mid, -EXP_CLAMP, EXP_CLAMP)), jnp.exp(jnp.minimum(b - r_prev, 0.0)),
           jnp.exp(jnp.minimum(b - r_half, 0.0)))
    eks = (jnp.exp(jnp.clip(r_mid - b, -EXP_CLAMP, EXP_CLAMP)), jnp.exp(jnp.minimum(r_end - b, 0.0)),
           jnp.exp(jnp.minimum(r_half - b, 0.0)))
    eb = jnp.exp(b)
    ec = jnp.exp(bc - b)
    return dict(eqs=eqs, eks=eks, eb=eb, ec=ec, e_end=jnp.exp(bc), qs=[q * e for e in eqs],
                ks=[kk * e for e in eks], qe=q * eb, ke=kk * ec)


def _scores(g, masks):
    a = None
    for qs, ks, m in zip(g["qs"], g["ks"], masks):
        part = jnp.where(m, _dot(qs.astype(BF16), ks.astype(BF16), NT), 0.0)
        a = part if a is None else a + part
    return a


def _hg_scan_fwd(p, lb, gn, name):
    T = p.shape[0]
    nc = T // CHUNK
    Dm = D_MODEL

    def body(p_ref, lb_ref, gn_ref, o_ref, og_ref, so_ref, st_ref):
        @pl.when(pl.program_id(0) == 0)
        def _():
            st_ref[...] = jnp.zeros_like(st_ref)

        masks = _score_masks()
        heads = _hg_gates(p_ref, lb_ref, _tri(True).astype(BF16))
        states = [st_ref[h] for h in range(HG_HEADS)]
        scores = [_scores(g, masks) for g in heads]
        outs = [_dot(a.astype(BF16), g["v"].astype(BF16), NN) + _dot(g["qe"], st, NT, X3)
                for g, a, st in zip(heads, scores, states)]
        new_states = [st * g["e_end"] + _dot(g["v"], g["ke"], TN, X3) for g, st in zip(heads, states)]
        for h, (g, o, st, st2) in enumerate(zip(heads, outs, states, new_states)):
            cs = slice(h * HG_DIM, (h + 1) * HG_DIM)
            so_ref[0, h] = st
            st_ref[h] = st2
            o_ref[:, cs] = o
            r = lax.rsqrt(jnp.mean(o * o, axis=-1, keepdims=True) + EPS)
            gt = g["gt"]
            og_ref[:, cs] = (((o * r) * gn_ref[:, cs]).astype(F32) * (gt * _sig(gt))).astype(BF16)

    return pl.pallas_call(
        body, name=name, grid=(nc,),
        out_shape=(jax.ShapeDtypeStruct((T, Dm), F32), jax.ShapeDtypeStruct((T, Dm), BF16),
                   jax.ShapeDtypeStruct((nc, HG_HEADS, HG_DIM, HG_DIM), F32)),
        in_specs=[_row_spec(CHUNK, 4 * Dm), _vec_spec(Dm), _vec_spec(Dm)],
        out_specs=(_row_spec(CHUNK, Dm), _row_spec(CHUNK, Dm),
                   pl.BlockSpec((1, HG_HEADS, HG_DIM, HG_DIM), lambda i: (i, 0, 0, 0))),
        scratch_shapes=[pltpu.VMEM((HG_HEADS, HG_DIM, HG_DIM), F32)],
        compiler_params=_params(("arbitrary",)),
    )(p, lb, gn)


def _hg_scan_bwd(p, lb, gn, o, dog, states, name):
    T = p.shape[0]
    nc = T // CHUNK
    Dm = D_MODEL

    def rev(i):
        return nc - 1 - i

    def body(p_ref, lb_ref, gn_ref, o_ref, dog_ref, st_in_ref, dp_ref, dlb_ref, dgn_ref, dst_ref, carry_ref):
        @pl.when(pl.program_id(0) == 0)
        def _():
            dst_ref[...] = jnp.zeros_like(dst_ref)
            carry_ref[...] = jnp.zeros_like(carry_ref)
            dlb_ref[...] = jnp.zeros_like(dlb_ref)
            dgn_ref[...] = jnp.zeros_like(dgn_ref)

        upper = _tri(False).astype(BF16)
        masks = _score_masks()
        heads = _hg_gates(p_ref, lb_ref, _tri(True).astype(BF16))
        for h, g in enumerate(heads):
            cs = slice(h * HG_DIM, (h + 1) * HG_DIM)
            oh = o_ref[:, cs]
            r = lax.rsqrt(jnp.mean(oh * oh, axis=-1, keepdims=True) + EPS)
            on = oh * r
            gt = g["gt"]
            sgt = _sig(gt)
            sil = gt * sgt
            dogh = dog_ref[:, cs]
            gnh = gn_ref[:, cs]
            don = dogh * gnh * sil
            g["dgn"] = jnp.sum(dogh * on * sil, axis=0, keepdims=True)
            g["dgate"] = dogh * on * gnh * (sgt * (1.0 + gt * (1.0 - sgt)))
            g["do"] = r * (don - on * jnp.mean(don * on, axis=-1, keepdims=True))
            g["dst"] = dst_ref[h]
            g["st"] = st_in_ref[0, h]
            g["carry"] = carry_ref[h]
        for g in heads:
            g["a"] = _scores(g, masks)
            g["dob"] = g["do"].astype(BF16)
            g["da"] = _dot(g["dob"], g["v"].astype(BF16), NT)
        for g in heads:
            g["dv"] = _dot(g["a"].astype(BF16), g["dob"], TN) + _dot(g["ke"].astype(BF16), g["dst"].astype(BF16), NT)
            g["dq"] = _dot(g["do"], g["st"], NN, X3) * g["eb"]
            g["dk"] = _dot(g["v"], g["dst"], NN, X3) * g["ec"]
            g["dst2"] = g["dst"] * g["e_end"] + _dot(g["do"], g["qe"], TN, X3)
        for lvl in range(3):
            for g in heads:
                dam = jnp.where(masks[lvl], g["da"], 0.0)
                g["dq"] = g["dq"] + _dot(dam, g["ks"][lvl], NN, X3) * g["eqs"][lvl]
                g["dk"] = g["dk"] + _dot(dam, g["qs"][lvl], TN, X3) * g["eks"][lvl]
        for g in heads:
            g["dgd"] = g["q"] * g["dq"] - g["kk"] * g["dk"]
            g["dgl"] = _dot01(upper, g["dgd"]) + g["carry"]
        for h, g in enumerate(heads):
            c0 = h * HG_DIM
            cs = slice(c0, c0 + HG_DIM)
            df = g["dgl"] / g["f"] - g["dk"]
            sg, sq, qr = g["sg"], g["sq"], g["qr"]
            dst_ref[h] = g["dst2"]
            carry_ref[h] = g["carry"] + jnp.sum(g["dgd"], axis=0, keepdims=True)
            dgn_ref[:, cs] += g["dgn"]
            dlb_ref[:, cs] += jnp.sum(df * (1.0 - sg), axis=0, keepdims=True)
            dp_ref[:, c0:c0 + HG_DIM] = (g["dq"] * (sq * (1.0 + qr * (1.0 - sq)))).astype(BF16)
            dp_ref[:, Dm + c0:Dm + c0 + HG_DIM] = (df * (1.0 - g["lbh"]) * sg * (1.0 - sg)).astype(BF16)
            dp_ref[:, 2 * Dm + c0:2 * Dm + c0 + HG_DIM] = g["dv"].astype(BF16)
            dp_ref[:, 3 * Dm + c0:3 * Dm + c0 + HG_DIM] = g["dgate"].astype(BF16)

    vshape = jax.ShapeDtypeStruct((1, Dm), F32)
    rrow = lambda w: pl.BlockSpec((CHUNK, w), lambda i: (rev(i), 0))
    return pl.pallas_call(
        body, name=name, grid=(nc,),
        out_shape=(jax.ShapeDtypeStruct((T, 4 * Dm), BF16), vshape, vshape),
        in_specs=[rrow(4 * Dm), _vec_spec(Dm), _vec_spec(Dm), rrow(Dm), rrow(Dm),
                  pl.BlockSpec((1, HG_HEADS, HG_DIM, HG_DIM), lambda i: (rev(i), 0, 0, 0))],
        out_specs=(rrow(4 * Dm), _vec_spec(Dm), _vec_spec(Dm)),
        scratch_shapes=[pltpu.VMEM((HG_HEADS, HG_DIM, HG_DIM), F32), pltpu.VMEM((HG_HEADS, 1, HG_DIM), F32)],
        compiler_params=_params(("arbitrary",)),
    )(p, lb, gn, o, dog, states)


def _lb_fwd(hg_lb, name):
    def body(a_ref, o_ref):
        a0, a1 = a_ref[0:1], a_ref[1:2]
        m = jnp.maximum(a0, a1)
        e0, e1 = jnp.exp(a0 - m), jnp.exp(a1 - m)
        p0, p1 = e0 / (e0 + e1), e1 / (e0 + e1)
        o_ref[0:1] = p0 - p0
        o_ref[1:2] = (p0 + p1) - p0

    return pl.pallas_call(body, name=name, out_shape=jax.ShapeDtypeStruct(hg_lb.shape, F32))(hg_lb)


def _lb_bwd(hg_lb, dlb_all, name):
    def body(a_ref, d_ref, o_ref):
        a0, a1 = a_ref[0:1], a_ref[1:2]
        m = jnp.maximum(a0, a1)
        e0, e1 = jnp.exp(a0 - m), jnp.exp(a1 - m)
        p0, p1 = e0 / (e0 + e1), e1 / (e0 + e1)
        d1 = d_ref[1:2]
        o_ref[0:1] = -p0 * p1 * d1
        o_ref[1:2] = p1 * (1.0 - p1) * d1

    return pl.pallas_call(body, name=name, out_shape=jax.ShapeDtypeStruct(hg_lb.shape, F32))(hg_lb, dlb_all)


CONV_COLS_FWD = 256
CONV_COLS_BWD = 128


def _conv_fwd(a, w, b, name):
    T = a.shape[0]
    Fh = FFN_HIDDEN
    tr = _pick(T, (ROW_TILE,))
    cw = CONV_COLS_FWD
    hb = tr // 8

    def body(a_ref, ap_ref, w_ref, b_ref, m_ref):
        m0 = (pl.program_id(0) > 0).astype(F32)

        def conv(cc):
            x = jnp.concatenate([ap_ref[:, pl.ds(cc, cw)] * m0, a_ref[:, pl.ds(cc, cw)]], axis=0)
            wv = w_ref[:, pl.ds(cc, cw)]
            y = b_ref[:, pl.ds(cc, cw)] + wv[2:3] * x + wv[1:2] * pltpu.roll(x, 1, axis=0) \
                + wv[0:1] * pltpu.roll(x, 2, axis=0)
            return y[8:]

        def step(c, carry):
            c0 = pl.multiple_of(c * cw, cw)
            c1 = pl.multiple_of(Fh + c * cw, cw)
            yg, yv = conv(c0), conv(c1)
            m_ref[:, pl.ds(c0, cw)] = (yg * _cdf(yg) * yv).astype(BF16)
            return carry

        lax.fori_loop(0, Fh // cw, step, 0)

    return pl.pallas_call(
        body, name=name, out_shape=jax.ShapeDtypeStruct((T, Fh), BF16), grid=(T // tr,),
        in_specs=[_row_spec(tr, 2 * Fh), pl.BlockSpec((8, 2 * Fh), lambda i: (jnp.maximum(i * hb - 1, 0), 0)),
                  _vec_spec(2 * Fh, 3), _vec_spec(2 * Fh)],
        out_specs=_row_spec(tr, Fh), compiler_params=_params(("parallel",)),
    )(a, a, w, b)


def _conv_bwd(a, dm, w, b, name):
    T = a.shape[0]
    Fh = FFN_HIDDEN
    tr = _pick(T, (ROW_TILE,))
    cw = CONV_COLS_BWD
    hb = tr // 8
    nsteps = T // tr
    n = tr + 8

    def body(a_ref, ap_ref, an_ref, dm_ref, dmn_ref, w_ref, b_ref, da_ref, dw_ref, db_ref):
        i = pl.program_id(0)
        m0 = (i > 0).astype(F32)
        m1 = (i < nsteps - 1).astype(F32)

        @pl.when(i == 0)
        def _():
            dw_ref[...] = jnp.zeros_like(dw_ref)
            db_ref[...] = jnp.zeros_like(db_ref)

        def prep(cc):
            x = jnp.concatenate([ap_ref[:, pl.ds(cc, cw)] * m0, a_ref[:, pl.ds(cc, cw)],
                                 an_ref[:, pl.ds(cc, cw)] * m1], axis=0)
            wv = w_ref[:, pl.ds(cc, cw)]
            s1 = pltpu.roll(x, 1, axis=0)
            s2 = pltpu.roll(x, 2, axis=0)
            y = b_ref[:, pl.ds(cc, cw)] + wv[2:3] * x + wv[1:2] * s1 + wv[0:1] * s2
            return wv, x[8:], s1[8:], s2[8:], y[8:]

        def back(cc, dy, wv, x0, s1, s2):
            da = wv[2:3] * dy + wv[1:2] * pltpu.roll(dy, n - 1, axis=0) + wv[0:1] * pltpu.roll(dy, n - 2, axis=0)
            da_ref[:, pl.ds(cc, cw)] = da[:tr].astype(BF16)
            d = dy[:tr]
            db_ref[:, pl.ds(cc, cw)] += jnp.sum(d, axis=0, keepdims=True)
            dw_ref[2:3, pl.ds(cc, cw)] += jnp.sum(d * x0[:tr], axis=0, keepdims=True)
            dw_ref[1:2, pl.ds(cc, cw)] += jnp.sum(d * s1[:tr], axis=0, keepdims=True)
            dw_ref[0:1, pl.ds(cc, cw)] += jnp.sum(d * s2[:tr], axis=0, keepdims=True)

        def step(c, carry):
            c0 = pl.multiple_of(c * cw, cw)
            c1 = pl.multiple_of(Fh + c * cw, cw)
            dmx = jnp.concatenate([dm_ref[:, pl.ds(c0, cw)], dmn_ref[:, pl.ds(c0, cw)] * m1], axis=0)
            wg, xg, s1g, s2g, yg = prep(c0)
            wv, xv, s1v, s2v, yv = prep(c1)
            cg = _cdf(yg)
            back(c0, dmx * yv * (cg + yg * _pdf(yg)), wg, xg, s1g, s2g)
            back(c1, dmx * (yg * cg), wv, xv, s1v, s2v)
            return carry

        lax.fori_loop(0, Fh // cw, step, 0)

    prev = lambda wd: pl.BlockSpec((8, wd), lambda i: (jnp.maximum(i * hb - 1, 0), 0))
    nxt = lambda wd: pl.BlockSpec((8, wd), lambda i: (jnp.minimum((i + 1) * hb, T // 8 - 1), 0))
    return pl.pallas_call(
        body, name=name, grid=(nsteps,),
        out_shape=(jax.ShapeDtypeStruct((T, 2 * Fh), BF16), jax.ShapeDtypeStruct((3, 2 * Fh), F32),
                   jax.ShapeDtypeStruct((1, 2 * Fh), F32)),
        in_specs=[_row_spec(tr, 2 * Fh), prev(2 * Fh), nxt(2 * Fh), _row_spec(tr, Fh), nxt(Fh),
                  _vec_spec(2 * Fh, 3), _vec_spec(2 * Fh)],
        out_specs=(_row_spec(tr, 2 * Fh), _vec_spec(2 * Fh, 3), _vec_spec(2 * Fh)),
        compiler_params=_params(("arbitrary",)),
    )(a, a, a, dm, dm, w, b)


def _ada_fwd(c_all, ada_w, ada_b, name):
    L, Dm, cols = ada_w.shape
    tn = _pick(cols, (512, 256, 128))

    def body(c_ref, w_ref, b_ref, o_ref):
        cv = c_ref[...]
        cond = (cv * _sig(cv)).astype(BF16)
        o_ref[...] = _dot(cond, w_ref[...].astype(BF16), NN) + b_ref[...]

    return pl.pallas_call(
        body, name=name, out_shape=jax.ShapeDtypeStruct((L, N_DEV, cols), F32), grid=(L, cols // tn),
        in_specs=[pl.BlockSpec((N_DEV, Dm), lambda l, j: (0, 0)), pl.BlockSpec((None, Dm, tn), lambda l, j: (l, 0, j)),
                  pl.BlockSpec((None, 1, tn), lambda l, j: (l, 0, j))],
        out_specs=pl.BlockSpec((None, N_DEV, tn), lambda l, j: (l, 0, j)),
        compiler_params=_params(("parallel", "parallel")),
    )(c_all, ada_w, ada_b.reshape(L, 1, cols))


def _ada_bwd(c_all, dmod, name):
    L, _, cols = dmod.shape
    Dm = c_all.shape[1]
    tn = _pick(cols, (512, 256, 128))

    def body(c_ref, d_ref, o_ref):
        cv = c_ref[...]
        o_ref[...] = _dot(cv * _sig(cv), d_ref[...], TN, HI)

    return pl.pallas_call(
        body, name=name, out_shape=jax.ShapeDtypeStruct((L, Dm, cols), F32), grid=(L, cols // tn),
        in_specs=[pl.BlockSpec((N_DEV, Dm), lambda l, j: (0, 0)), pl.BlockSpec((None, N_DEV, tn), lambda l, j: (l, 0, j))],
        out_specs=pl.BlockSpec((None, Dm, tn), lambda l, j: (l, 0, j)),
        compiler_params=_params(("parallel", "parallel")),
    )(c_all, dmod)


def _add_own_half(g4, rb, core, name):
    S, _, rh, cw = g4.shape
    tr = _pick(rh, (256, 128, 176, 64))

    def body(core_ref, g_ref, r_ref, o_ref):
        o_ref[...] = (g_ref[...].astype(F32) + r_ref[...].astype(F32)).astype(GRAD_WIRE)

    return pl.pallas_call(
        body, name=name, out_shape=jax.ShapeDtypeStruct((S, rh, cw), BF16),
        grid_spec=pltpu.PrefetchScalarGridSpec(
            num_scalar_prefetch=1, grid=(S, rh // tr),
            in_specs=[pl.BlockSpec((None, None, tr, cw), lambda s, i, core_ref: (s, core_ref[0], i, 0)),
                      pl.BlockSpec((None, tr, cw), lambda s, i, core_ref: (s, i, 0))],
            out_specs=pl.BlockSpec((None, tr, cw), lambda s, i, core_ref: (s, i, 0))),
        compiler_params=_params(("parallel", "parallel")),
    )(core, g4, rb)


def _sum_chips(lands, sums, chip, name):
    L = len(lands)
    _, rh, cw = lands[0].shape
    tr = _pick(rh, (256, 128, 176, 64))

    def body(chip_ref, *refs):
        ld, cs, o_ref = refs[:L], refs[L:2 * L], refs[2 * L]
        me = chip_ref[0]
        for k in range(L):
            @pl.when(pl.program_id(0) == k)
            def _(k=k):
                own = cs[k][...].astype(F32)
                got = [ld[k][j].astype(F32) for j in range(3)]
                acc = None
                for t in range(N_CHIPS):
                    d = jnp.bitwise_xor(jnp.int32(t), me)
                    term = jnp.where(d == 0, own, jnp.where(d == 2, got[0], jnp.where(d == 1, got[1], got[2])))
                    acc = term if acc is None else acc + term
                o_ref[...] = acc

    frozen = lambda l, i, k: jnp.where(l == k, i, 0)
    in_specs = [pl.BlockSpec((3, tr, cw), lambda l, i, chip_ref, k=k: (0, frozen(l, i, k), 0)) for k in range(L)]
    in_specs += [pl.BlockSpec((None, tr, cw), lambda l, i, chip_ref, k=k: (chip_ref[0], frozen(l, i, k), 0))
                 for k in range(L)]
    return pl.pallas_call(
        body, name=name, out_shape=jax.ShapeDtypeStruct((L, rh, cw), F32),
        grid_spec=pltpu.PrefetchScalarGridSpec(
            num_scalar_prefetch=1, grid=(L, rh // tr), in_specs=in_specs,
            out_specs=pl.BlockSpec((None, tr, cw), lambda l, i, chip_ref: (l, i, 0))),
        compiler_params=_params(("arbitrary", "arbitrary")),
    )(chip, *lands, *sums)


def _sum_devices(gathered, name):
    n, R, _ = gathered.shape
    tr = _pick(R, (512, 448, 384, 256, 192, 128, 64, 32, 16, 8))

    def body(g_ref, o_ref):
        acc = g_ref[0]
        for d in range(1, n):
            acc = acc + g_ref[d]
        o_ref[...] = acc

    return pl.pallas_call(
        body, name=name, out_shape=jax.ShapeDtypeStruct((R, LANES), F32), grid=(R // tr,),
        in_specs=[pl.BlockSpec((n, tr, LANES), lambda i: (0, i, 0))], out_specs=pl.BlockSpec((tr, LANES), lambda i: (i, 0)),
        compiler_params=_params(("parallel",)),
    )(gathered)


def _adamw(w, g, m, v, name):
    R, C = w.shape
    tr = _pick(R, (256, 128, 64, 32, 16, 8))
    c1 = 1.0 / (1.0 - ADAM_B1 ** ADAM_STEP)
    c2 = 1.0 / (1.0 - ADAM_B2 ** ADAM_STEP)

    def body(w_ref, g_ref, m_ref, v_ref, d_ref, mo_ref, vo_ref):
        gv = g_ref[...]
        m2 = ADAM_B1 * m_ref[...] + (1.0 - ADAM_B1) * gv
        v2 = ADAM_B2 * v_ref[...] + (1.0 - ADAM_B2) * (gv * gv)
        mo_ref[...] = m2
        vo_ref[...] = v2
        d_ref[...] = -ADAM_LR * ((m2 * c1) / (jnp.sqrt(v2 * c2) + ADAM_EPS) + ADAM_WD * w_ref[...])

    spec = pl.BlockSpec((tr, C), lambda i: (i, 0))
    shp = jax.ShapeDtypeStruct((R, C), F32)
    return pl.pallas_call(body, name=name, out_shape=(shp, shp, shp), grid=(R // tr,), in_specs=[spec] * 4,
                          out_specs=(spec, spec, spec), compiler_params=_params(("parallel",)))(w, g, m, v)


def _adamw_halves(w, own, recv, m, v, core, name):
    L, rh, cw = own.shape
    tr = _pick(rh, (256, 128, 176, 64))
    c1 = 1.0 / (1.0 - ADAM_B1 ** ADAM_STEP)
    c2 = 1.0 / (1.0 - ADAM_B2 ** ADAM_STEP)

    def body(core_ref, w_ref, own_ref, recv_ref, m_ref, v_ref, g_ref, d_ref, mo_ref, vo_ref):
        gv = jnp.where(pl.program_id(1) == core_ref[0], own_ref[...], recv_ref[...])
        g_ref[...] = gv
        m2 = ADAM_B1 * m_ref[...] + (1.0 - ADAM_B1) * gv
        v2 = ADAM_B2 * v_ref[...] + (1.0 - ADAM_B2) * (gv * gv)
        mo_ref[...] = m2
        vo_ref[...] = v2
        d_ref[...] = -ADAM_LR * ((m2 * c1) / (jnp.sqrt(v2 * c2) + ADAM_EPS) + ADAM_WD * w_ref[...])

    full = pl.BlockSpec((None, None, tr, cw), lambda l, hf, i, core_ref: (l, hf, i, 0))
    mine = pl.BlockSpec((None, tr, cw), lambda l, hf, i, core_ref: (l, jnp.where(hf == core_ref[0], i, 0), 0))
    other = pl.BlockSpec((None, tr, cw), lambda l, hf, i, core_ref: (l, jnp.where(hf == core_ref[0], 0, i), 0))
    shp = jax.ShapeDtypeStruct((L, 2, rh, cw), F32)
    view = lambda a: a.reshape(L, 2, rh, cw)
    outs = pl.pallas_call(
        body, name=name, out_shape=(shp, shp, shp, shp),
        grid_spec=pltpu.PrefetchScalarGridSpec(
            num_scalar_prefetch=1, grid=(L, 2, rh // tr), in_specs=[full, mine, other, full, full],
            out_specs=(full, full, full, full)),
        compiler_params=_params(("arbitrary", "arbitrary", "arbitrary")),
    )(core, view(w), own, recv, view(m), view(v))
    return tuple(o.reshape(L, 2 * rh, cw) for o in outs)


ANY = pl.BlockSpec(memory_space=pl.ANY)


def _position():
    x, y, c = lax.axis_index("x"), lax.axis_index("y"), lax.axis_index("c")
    return x, y, c


def _allgather(ins, out_shapes, src_fns, dst_fns, name, in_vmem):
    n = len(ins)

    def body(*refs):
        in_refs, out_refs = refs[:n], refs[n:2 * n]
        send_sems, recv_sems, local_sems = refs[2 * n:]
        x, y, c = _position()
        me, sibling = (x, y, c), (x, y, 1 - c)
        chips = [(1 - x, y), (x, 1 - y), (1 - x, 1 - y)]

        def copy(k, j, block, to, own=False):
            dst = dst_fns[k](out_refs[k], *block)
            return pltpu.make_async_remote_copy(
                src_ref=src_fns[k](in_refs[k], c) if own else dst, dst_ref=dst,
                send_sem=send_sems.at[k, j], recv_sem=recv_sems.at[k, j], device_id=to, device_id_type=MESH)

        mine = [pltpu.make_async_copy(src_fns[k](in_refs[k], c), dst_fns[k](out_refs[k], *me), local_sems.at[k])
                for k in range(n)]
        for cp in mine:
            cp.start()
        first = []
        for k in range(n):
            first.append(copy(k, 0, me, sibling, own=True))
            first += [copy(k, 1 + j, me, (*chip, c), own=True) for j, chip in enumerate(chips)]
        for cp in first:
            cp.start()
        passed = []
        for j, chip in enumerate(chips):
            for k in range(n):
                copy(k, 1 + j, (*chip, c), me).wait_recv()
                fwd = copy(k, 4 + j, (*chip, c), sibling)
                fwd.start()
                passed.append(fwd)
        for k in range(n):
            copy(k, 0, sibling, me).wait_recv()
        for j, chip in enumerate(chips):
            for k in range(n):
                copy(k, 4 + j, (*chip, 1 - c), me).wait_recv()
        for cp in first + passed:
            cp.wait_send()
        for cp in mine:
            cp.wait()

    spec = pl.BlockSpec(memory_space=pltpu.VMEM) if in_vmem else ANY
    return pl.pallas_call(
        body, name=name, out_shape=tuple(out_shapes), in_specs=[spec] * n, out_specs=tuple([spec] * n),
        scratch_shapes=[pltpu.SemaphoreType.DMA((n, 7)), pltpu.SemaphoreType.DMA((n, 7)),
                        pltpu.SemaphoreType.DMA((n,))],
        compiler_params=pltpu.CompilerParams(vmem_limit_bytes=VMEM_LIMIT_BYTES),
    )(*ins)


def _allgather_small(payload, name):
    R = payload.shape[0]
    (out,) = _allgather(
        [payload], [jax.ShapeDtypeStruct((N_DEV, R, LANES), F32)],
        [lambda ref, c: ref], [lambda ref, px, py, pc: ref.at[4 * px + 2 * py + pc]], name, in_vmem=True)
    return out


def _swap_sibling(ins, name, other_half=False):
    n = len(ins)

    def body(*refs):
        in_refs, out_refs = refs[:n], refs[n:2 * n]
        send_sems, recv_sems = refs[2 * n:]
        x, y, c = _position()
        copies = [pltpu.make_async_remote_copy(
            src_ref=in_refs[k].at[:, 1 - c] if other_half else in_refs[k], dst_ref=out_refs[k],
            send_sem=send_sems.at[k], recv_sem=recv_sems.at[k],
            device_id=(x, y, 1 - c), device_id_type=MESH) for k in range(n)]
        for cp in copies:
            cp.start()
        for cp in copies:
            cp.wait_recv()
        for cp in copies:
            cp.wait_send()

    shape = lambda a: (a.shape[0],) + a.shape[2:] if other_half else a.shape
    return pl.pallas_call(
        body, name=name, out_shape=tuple(jax.ShapeDtypeStruct(shape(a), a.dtype) for a in ins),
        in_specs=[ANY] * n, out_specs=tuple([ANY] * n),
        scratch_shapes=[pltpu.SemaphoreType.DMA((n,)), pltpu.SemaphoreType.DMA((n,))],
        compiler_params=pltpu.CompilerParams(vmem_limit_bytes=VMEM_LIMIT_BYTES),
    )(*ins)


HBM_SPEC = pl.BlockSpec(memory_space=pltpu.HBM)
SEM_SPEC = pl.BlockSpec(memory_space=pltpu.SEMAPHORE)
SPLIT_PARAMS = pltpu.CompilerParams(has_side_effects=pltpu.SideEffectType.DATAFLOW_SIDE_EFFECTING)
TOKEN = jax.ShapeDtypeStruct((8, LANES), F32)


def _hbm(a):
    return pltpu.with_memory_space_constraint(a, pltpu.HBM)


def _weight_window(ref, col, r, cw, px, py, pc):
    rh = r // 2
    if col:
        return ref.at[pl.ds(pc * rh, rh), pl.ds((2 * px + py) * cw, cw)]
    return ref.at[pl.ds((2 * px + py) * r + pc * rh, rh), :]


def _peers(x, y, c):
    return [(x, y, 1 - c), (1 - x, y, c), (x, 1 - y, c), (1 - x, 1 - y, c)]


def _place_own(shard, col, pos, name):
    r, cw = shard.shape
    rh = r // 2
    tr = _pick(rh, (256, 128, 176, 64))
    nb = rh // tr
    shape = (r, N_CHIPS * cw) if col else (N_CHIPS * r, cw)

    def body(pos_ref, x_ref, o_ref):
        o_ref[...] = x_ref[...]

    if col:
        out_idx = lambda i, pos_ref: (pos_ref[1] * nb + i, pos_ref[0])
    else:
        out_idx = lambda i, pos_ref: (pos_ref[0] * (2 * nb) + pos_ref[1] * nb + i, 0)
    return pl.pallas_call(
        body, name=name, out_shape=jax.ShapeDtypeStruct(shape, shard.dtype),
        grid_spec=pltpu.PrefetchScalarGridSpec(
            num_scalar_prefetch=1, grid=(nb,),
            in_specs=[pl.BlockSpec((tr, cw), lambda i, pos_ref: (pos_ref[1] * nb + i, 0))],
            out_specs=pl.BlockSpec((tr, cw), out_idx)),
        compiler_params=_params(("arbitrary",)),
    )(pos, shard)


def _gather_start(shards, lands, cols, per_layer, name):
    n = len(shards)
    nl = n // per_layer

    def body(*refs):
        sh, ld = refs[:n], refs[n:2 * n]
        sems, token = refs[2 * n:2 * n + 2 * nl], refs[-1]
        x, y, c = _position()
        for k in range(n):
            l, a = divmod(k, per_layer)
            r, cw = shards[k].shape
            src = sh[k].at[pl.ds(c * (r // 2), r // 2), :]
            dst = _weight_window(ld[k], cols[k], r, cw, x, y, c)
            for j, peer in enumerate(_peers(x, y, c)):
                pltpu.make_async_remote_copy(src_ref=src, dst_ref=dst, send_sem=sems[2 * l].at[4 * a + j],
                                             recv_sem=sems[2 * l + 1].at[4 * a + j], device_id=peer,
                                             device_id_type=MESH).start()
        token[...] = jnp.zeros_like(token)

    arrs = list(shards) + list(lands)
    out = pl.pallas_call(
        body, name=name,
        out_shape=tuple(pltpu.SemaphoreType.DMA((per_layer * 4,)) for _ in range(2 * nl))
        + tuple(pltpu.HBM(a.shape, a.dtype) for a in arrs) + (TOKEN,),
        in_specs=[HBM_SPEC] * (2 * n),
        out_specs=(SEM_SPEC,) * (2 * nl) + (HBM_SPEC,) * (2 * n) + (pl.BlockSpec(memory_space=pltpu.VMEM),),
        input_output_aliases={i: 2 * nl + i for i in range(2 * n)}, compiler_params=SPLIT_PARAMS,
    )(*[_hbm(a) for a in arrs])
    return out[:2 * nl], out[2 * nl:2 * nl + n], out[2 * nl + n:2 * nl + 2 * n], out[-1]


def _gather_wait(shards, lands, send, recv, after, cols, first, name):
    m = len(shards)

    def body(*refs):
        sh, ld = refs[:m], refs[m:2 * m]
        send_ref, recv_ref = refs[2 * m], refs[2 * m + 1]
        x, y, c = _position()
        for a in range(m):
            r, cw = shards[a].shape
            src = sh[a].at[pl.ds(c * (r // 2), r // 2), :]
            for j, (px, py, pc) in enumerate(_peers(x, y, c)):
                cp = pltpu.make_async_remote_copy(
                    src_ref=src, dst_ref=_weight_window(ld[a], cols[a], r, cw, px, py, pc),
                    send_sem=send_ref.at[4 * (first + a) + j], recv_sem=recv_ref.at[4 * (first + a) + j],
                    device_id=(px, py, pc),
                    device_id_type=MESH)
                cp.wait_send()
                cp.wait_recv()

    arrs = list(shards) + list(lands)
    out = pl.pallas_call(
        body, name=name, out_shape=tuple(pltpu.HBM(a.shape, a.dtype) for a in arrs),
        in_specs=[HBM_SPEC] * (2 * m) + [SEM_SPEC, SEM_SPEC, ANY], out_specs=(HBM_SPEC,) * (2 * m),
        input_output_aliases={i: i for i in range(2 * m)}, compiler_params=SPLIT_PARAMS,
    )(*arrs, send, recv, after)
    return out[m:]


def _forward_sibling(lands, cols, shard_shapes, name):
    m = len(lands)

    def body(*refs):
        ins, outs = refs[:m], refs[m:2 * m]
        send_sems, recv_sems = refs[2 * m:]
        x, y, c = _position()
        chips = [(1 - x, y), (x, 1 - y), (1 - x, 1 - y)]
        sends = []
        for a in range(m):
            r, cw = shard_shapes[a]
            for j, (px, py) in enumerate(chips):
                cp = pltpu.make_async_remote_copy(
                    src_ref=_weight_window(ins[a], cols[a], r, cw, px, py, c),
                    dst_ref=_weight_window(outs[a], cols[a], r, cw, px, py, c),
                    send_sem=send_sems.at[a, j], recv_sem=recv_sems.at[a, j], device_id=(x, y, 1 - c),
                    device_id_type=MESH)
                cp.start()
                sends.append(cp)
        for a in range(m):
            r, cw = shard_shapes[a]
            for j, (px, py) in enumerate(chips):
                pltpu.make_async_remote_copy(
                    src_ref=_weight_window(ins[a], cols[a], r, cw, px, py, c),
                    dst_ref=_weight_window(outs[a], cols[a], r, cw, px, py, 1 - c),
                    send_sem=send_sems.at[a, j], recv_sem=recv_sems.at[a, j], device_id=(x, y, 1 - c),
                    device_id_type=MESH).wait_recv()
        for cp in sends:
            cp.wait_send()

    return pl.pallas_call(
        body, name=name, out_shape=tuple(jax.ShapeDtypeStruct(a.shape, a.dtype) for a in lands),
        in_specs=[ANY] * m, out_specs=tuple([ANY] * m), input_output_aliases={i: i for i in range(m)},
        scratch_shapes=[pltpu.SemaphoreType.DMA((m, 3)), pltpu.SemaphoreType.DMA((m, 3))],
        compiler_params=pltpu.CompilerParams(vmem_limit_bytes=VMEM_LIMIT_BYTES),
    )(*lands)


def _exchange_start(sums, name):
    m = len(sums)
    lands = [lax.empty((3,) + s.shape[1:], s.dtype) for s in sums]

    def body(*refs):
        cs, ld = refs[:m], refs[m:2 * m]
        send_ref, recv_ref, token = refs[2 * m], refs[2 * m + 1], refs[-1]
        x, y, c = _position()
        for a in range(m):
            for j, (px, py) in enumerate([(1 - x, y), (x, 1 - y), (1 - x, 1 - y)]):
                pltpu.make_async_remote_copy(
                    src_ref=cs[a].at[2 * px + py], dst_ref=ld[a].at[j], send_sem=send_ref.at[3 * a + j],
                    recv_sem=recv_ref.at[3 * a + j], device_id=(px, py, c), device_id_type=MESH).start()
        token[...] = jnp.zeros_like(token)

    arrs = list(sums) + lands
    out = pl.pallas_call(
        body, name=name,
        out_shape=(pltpu.SemaphoreType.DMA((m * 3,)), pltpu.SemaphoreType.DMA((m * 3,)))
        + tuple(pltpu.HBM(a.shape, a.dtype) for a in arrs) + (TOKEN,),
        in_specs=[HBM_SPEC] * (2 * m),
        out_specs=(SEM_SPEC, SEM_SPEC) + (HBM_SPEC,) * (2 * m) + (pl.BlockSpec(memory_space=pltpu.VMEM),),
        input_output_aliases={i: 2 + i for i in range(2 * m)}, compiler_params=SPLIT_PARAMS,
    )(*[_hbm(a) for a in arrs])
    return out[0], out[1], out[2:2 + m], out[2 + m:2 + 2 * m], out[-1]


def _exchange_wait(sums, lands, send, recv, after, name):
    m = len(sums)

    def body(*refs):
        cs, ld = refs[:m], refs[m:2 * m]
        send_ref, recv_ref = refs[2 * m], refs[2 * m + 1]
        x, y, c = _position()
        for a in range(m):
            for j, (px, py) in enumerate([(1 - x, y), (x, 1 - y), (1 - x, 1 - y)]):
                cp = pltpu.make_async_remote_copy(
                    src_ref=cs[a].at[2 * px + py], dst_ref=ld[a].at[j], send_sem=send_ref.at[3 * a + j],
                    recv_sem=recv_ref.at[3 * a + j], device_id=(px, py, c), device_id_type=MESH)
                cp.wait_send()
                cp.wait_recv()

    arrs = list(sums) + list(lands)
    out = pl.pallas_call(
        body, name=name, out_shape=tuple(pltpu.HBM(a.shape, a.dtype) for a in arrs),
        in_specs=[HBM_SPEC] * (2 * m) + [SEM_SPEC, SEM_SPEC, ANY], out_specs=(HBM_SPEC,) * (2 * m),
        input_output_aliases={i: i for i in range(2 * m)}, compiler_params=SPLIT_PARAMS,
    )(*arrs, send, recv, after)
    return out[:m], out[m:]


def _place_row(payload, dev, name):
    R = payload.shape[0]
    tr = _pick(R, (512, 448, 384, 256, 192, 128, 64, 32, 16, 8))

    def body(dev_ref, x_ref, o_ref):
        o_ref[...] = x_ref[...]

    return pl.pallas_call(
        body, name=name, out_shape=jax.ShapeDtypeStruct((N_DEV, R, LANES), payload.dtype),
        grid_spec=pltpu.PrefetchScalarGridSpec(
            num_scalar_prefetch=1, grid=(R // tr,),
            in_specs=[pl.BlockSpec((tr, LANES), lambda i, dev_ref: (i, 0))],
            out_specs=pl.BlockSpec((None, tr, LANES), lambda i, dev_ref: (dev_ref[0], i, 0))),
        compiler_params=_params(("arbitrary",)),
    )(dev, payload)


def _others(x, y, c):
    return [(1 - x if fx else x, 1 - y if fy else y, 1 - c if fc else c)
            for fx in (0, 1) for fy in (0, 1) for fc in (0, 1) if fx or fy or fc]


def _broadcast_start(payload, land, name):
    def body(p_ref, l_ref, send_ref, recv_ref, p_thru, l_thru, token):
        x, y, c = _position()
        for j, peer in enumerate(_others(x, y, c)):
            pltpu.make_async_remote_copy(src_ref=p_ref, dst_ref=l_ref.at[4 * x + 2 * y + c], send_sem=send_ref.at[j],
                                         recv_sem=recv_ref.at[j], device_id=peer, device_id_type=MESH).start()
        token[...] = jnp.zeros_like(token)

    n = N_DEV - 1
    return pl.pallas_call(
        body, name=name,
        out_shape=(pltpu.SemaphoreType.DMA((n,)), pltpu.SemaphoreType.DMA((n,)), pltpu.HBM(payload.shape, payload.dtype),
                   pltpu.HBM(land.shape, land.dtype), TOKEN),
        in_specs=[HBM_SPEC, HBM_SPEC],
        out_specs=(SEM_SPEC, SEM_SPEC, HBM_SPEC, HBM_SPEC, pl.BlockSpec(memory_space=pltpu.VMEM)),
        input_output_aliases={0: 2, 1: 3}, compiler_params=SPLIT_PARAMS,
    )(_hbm(payload), _hbm(land))


def _broadcast_wait(payload, land, send, recv, after, name):
    def body(p_ref, l_ref, send_ref, recv_ref, after_ref, p_thru, l_thru):
        x, y, c = _position()
        for j, (px, py, pc) in enumerate(_others(x, y, c)):
            cp = pltpu.make_async_remote_copy(src_ref=p_ref, dst_ref=l_ref.at[4 * px + 2 * py + pc],
                                              send_sem=send_ref.at[j], recv_sem=recv_ref.at[j],
                                              device_id=(px, py, pc), device_id_type=MESH)
            cp.wait_send()
            cp.wait_recv()

    out = pl.pallas_call(
        body, name=name, out_shape=(pltpu.HBM(payload.shape, payload.dtype), pltpu.HBM(land.shape, land.dtype)),
        in_specs=[HBM_SPEC, HBM_SPEC, SEM_SPEC, SEM_SPEC, ANY], out_specs=(HBM_SPEC, HBM_SPEC),
        input_output_aliases={0: 0, 1: 1}, compiler_params=SPLIT_PARAMS,
    )(payload, land, send, recv, after)
    return out[1]


def _vec(a):
    return a.reshape(1, -1)


def _local_step(x, tgt, mod, W, P, get_w=None, on_grads=None):
    Dm = D_MODEL
    G = {k: [] for k in ("gm_w_in", "gm_w_out", "hg_w_in", "hg_w_out", "ffn_w_up", "ffn_w_down")}
    lb_all = _lb_fwd(P["hg_lb"], "lb_fwd")
    saved = []
    xs = x
    y_prev = gate_prev = None
    layer_w = [None] * DEPTH

    def wmm(xa, kind, i, mode, name):
        if layer_w[i] is not None:
            return _mm(xa, layer_w[i][kind], mode, name)
        return _mm(xa, W[kind], mode, name, b_layer=i if kind.startswith("ffn") else i // 2)

    for i in range(DEPTH):
        m = [_vec(mod[i, j * Dm:(j + 1) * Dm]) for j in range(6)]
        sh1, sc1, g1, sh2, sc2, g2 = m
        j = i // 2
        if get_w is not None:
            layer_w[i] = get_w(i, 0, xs if y_prev is None else y_prev)
        xs, h = _norm_fwd(xs, y_prev, gate_prev, _vec(P["norm_g"][i, 0]), sc1, sh1, f"norm_fwd_a{i}")
        rec = dict(x1=xs, h1=h)
        if i % 2 == 0:
            a = wmm(h, "gm_w_in", i, "nn", f"gm_in{i}")
            gated = _gm_mid_fwd(a, _vec(P["gm_ln_g"][j]), _vec(P["gm_ln_b"][j]), P["gm_w_s"][j],
                                P["gm_b_s"][j].reshape(GM_HEADS, GM_BLOCK, 1), f"gm_mid_fwd{i}")
            if get_w is not None:
                layer_w[i].update(get_w(i, 1, gated))
            y1 = wmm(gated, "gm_w_out", i, "nn", f"gm_out{i}")
            rec.update(a=a, act=gated)
        else:
            p = wmm(h, "hg_w_in", i, "nn", f"hg_in{i}")
            o, og, states = _hg_scan_fwd(p, _vec(lb_all[j]), _vec(P["hg_gn_g"][j]), f"hg_scan_fwd{i}")
            if get_w is not None:
                layer_w[i].update(get_w(i, 1, og))
            y1 = wmm(og, "hg_w_out", i, "nn", f"hg_out{i}")
            rec.update(a=p, act=og, o=o, states=states)
        rec["y1"] = y1
        xs, h2 = _norm_fwd(xs, y1, g1, _vec(P["norm_g"][i, 1]), sc2, sh2, f"norm_fwd_b{i}")
        a2 = wmm(h2, "ffn_w_up", i, "nn", f"ffn_up{i}")
        mm_ = _conv_fwd(a2, P["ffn_conv_w"][i], _vec(P["ffn_conv_b"][i]), f"conv_fwd{i}")
        y2 = wmm(mm_, "ffn_w_down", i, "nn", f"ffn_down{i}")
        rec.update(x2=xs, h2=h2, a2=a2, m=mm_, y2=y2, mods=m)
        saved.append(rec)
        y_prev, gate_prev = y2, g2
    dx, dy, loss, s_fg, s_gate = _loss_head(xs, y_prev, gate_prev, _vec(P["final_g"]), tgt, "loss_head")
    small = dict(final_g=s_fg, norm_g=[None] * DEPTH, dmod=[None] * DEPTH, ffn_conv_w=[None] * DEPTH,
                 ffn_conv_b=[None] * DEPTH, gm_ln_g=[None] * 2, gm_ln_b=[None] * 2, gm_w_s=[None] * 2,
                 gm_b_s=[None] * 2, hg_gn_g=[None] * 2, dlb=[None] * 2)
    for i in reversed(range(DEPTH)):
        rec = saved[i]
        sh1, sc1, g1, sh2, sc2, g2 = rec["mods"]
        j = i // 2
        d_g2 = s_gate
        dm = wmm(dy, "ffn_w_down", i, "nt", f"ffn_down_dx{i}")
        G["ffn_w_down"].append(_mm(rec["m"], dy, "tn", f"ffn_down_dw{i}", out_dtype=GRAD_WIRE))
        da2, dcw, dcb = _conv_bwd(rec["a2"], dm, P["ffn_conv_w"][i], _vec(P["ffn_conv_b"][i]), f"conv_bwd{i}")
        small["ffn_conv_w"][i], small["ffn_conv_b"][i] = dcw, dcb
        dh2 = wmm(da2, "ffn_w_up", i, "nt", f"ffn_up_dx{i}")
        G["ffn_w_up"].append(_mm(rec["h2"], da2, "tn", f"ffn_up_dw{i}", out_dtype=GRAD_WIRE, exchange_out=True))
        ng2 = _vec(P["norm_g"][i, 1])
        if on_grads is not None:
            ng2 = ng2 + on_grads(i, {k: G[k][-1] for k in ("ffn_w_up", "ffn_w_down")})
        dx, dy, s_sh2, s_x2, d_g1 = _norm_bwd(rec["x2"], dh2, dx, ng2, sc2, rec["y1"], g1, f"norm_bwd_b{i}")
        d_sc2, d_ng2 = s_x2 * ng2, s_x2 * (1.0 + sc2)
        if i % 2 == 0:
            dgated = wmm(dy, "gm_w_out", i, "nt", f"gm_out_dx{i}")
            G["gm_w_out"].append(_mm(rec["act"], dy, "tn", f"gm_out_dw{i}", out_dtype=GRAD_WIRE))
            da, dws, dbs, dlg, dlbeta = _gm_mid_bwd(
                rec["a"], dgated, _vec(P["gm_ln_g"][j]), _vec(P["gm_ln_b"][j]), P["gm_w_s"][j],
                P["gm_b_s"][j].reshape(GM_HEADS, GM_BLOCK, 1), f"gm_mid_bwd{i}")
            small["gm_w_s"][j], small["gm_b_s"][j] = dws, dbs[:, :GM_HEADS].T
            small["gm_ln_g"][j], small["gm_ln_b"][j] = dlg, dlbeta
            dh1 = wmm(da, "gm_w_in", i, "nt", f"gm_in_dx{i}")
            G["gm_w_in"].append(_mm(rec["h1"], da, "tn", f"gm_in_dw{i}", out_dtype=GRAD_WIRE, exchange_out=True))
        else:
            dog = wmm(dy, "hg_w_out", i, "nt", f"hg_out_dx{i}")
            G["hg_w_out"].append(_mm(rec["act"], dy, "tn", f"hg_out_dw{i}", out_dtype=GRAD_WIRE))
            dp, dlb, dgn = _hg_scan_bwd(rec["a"], _vec(lb_all[j]), _vec(P["hg_gn_g"][j]), rec["o"], dog,
                                        rec["states"], f"hg_scan_bwd{i}")
            small["dlb"][j], small["hg_gn_g"][j] = dlb, dgn
            dh1 = wmm(dp, "hg_w_in", i, "nt", f"hg_in_dx{i}")
            G["hg_w_in"].append(_mm(rec["h1"], dp, "tn", f"hg_in_dw{i}", out_dtype=GRAD_WIRE, exchange_out=True))
        ng1 = _vec(P["norm_g"][i, 0])
        if on_grads is not None:
            mixer = ("gm_w_in", "gm_w_out") if i % 2 == 0 else ("hg_w_in", "hg_w_out")
            ng1 = ng1 + on_grads(i, {k: G[k][-1] for k in mixer})
        if i > 0:
            prev = saved[i - 1]
            dx, dy, s_sh1, s_x1, s_gate = _norm_bwd(rec["x1"], dh1, dx, ng1, sc1, prev["y2"], prev["mods"][5],
                                                    f"norm_bwd_a{i}")
        else:
            dx, s_sh1, s_x1 = _norm_bwd(rec["x1"], dh1, dx, ng1, sc1, None, None, f"norm_bwd_a{i}")
        d_sc1, d_ng1 = s_x1 * ng1, s_x1 * (1.0 + sc1)
        small["norm_g"][i] = jnp.concatenate([d_ng1, d_ng2], axis=0)
        small["dmod"][i] = jnp.concatenate([s_sh1, d_sc1, d_g1, s_sh2, d_sc2, d_g2], axis=1)
    for k in G:
        G[k] = G[k][::-1]
    dlb_all = jnp.concatenate(small.pop("dlb"), axis=0)
    small["hg_lb"] = _lb_bwd(P["hg_lb"], dlb_all, "lb_bwd")
    return loss, dx, G, small


BIG = ("gm_w_in", "gm_w_out", "hg_w_in", "hg_w_out", "ffn_w_up", "ffn_w_down")
COL_SHARDED = dict(gm_w_in=True, gm_w_out=False, hg_w_in=True, hg_w_out=False, ffn_w_up=True, ffn_w_down=False)
LAYER_WEIGHTS = 4


def _layer_kinds(i):
    return (("gm_w_in", "gm_w_out") if i % 2 == 0 else ("hg_w_in", "hg_w_out")) + ("ffn_w_up", "ffn_w_down")


def _pack(pieces):
    flat = [p.reshape(-1).astype(F32) for p in pieces]
    offs, tot = [], 0
    for f in flat:
        offs.append((tot, f.shape[0]))
        tot += f.shape[0]
    padded = -(-tot // (8 * LANES)) * (8 * LANES)
    if padded > tot:
        flat.append(jnp.zeros((padded - tot,), F32))
    return jnp.concatenate(flat).reshape(-1, LANES), offs


def _unpack(rows, offs, shapes):
    lead = rows.shape[:-2]
    flat = rows.reshape(lead + (-1,))
    return [flat[..., o:o + n].reshape(lead + tuple(s)) for (o, n), s in zip(offs, shapes)]


def _from_chips(per_dev, axis):
    per_chip = per_dev[0::2]
    return jnp.concatenate([per_chip[s] for s in range(N_CHIPS)], axis=axis)


def kernel(x, c, gm_w_in, gm_ln_g, gm_ln_b, gm_w_s, gm_b_s, gm_w_out, hg_w_in, hg_lb, hg_gn_g, hg_w_out, ffn_w_up, ffn_conv_w, ffn_conv_b, ffn_w_down, norm_g, ada_w, ada_b, final_g, loss_target, m_gm_w_in, m_gm_ln_g, m_gm_ln_b, m_gm_w_s, m_gm_b_s, m_gm_w_out, m_hg_w_in, m_hg_lb, m_hg_gn_g, m_hg_w_out, m_ffn_w_up, m_ffn_conv_w, m_ffn_conv_b, m_ffn_w_down, m_norm_g, m_ada_w, m_ada_b, m_final_g, v_gm_w_in, v_gm_ln_g, v_gm_ln_b, v_gm_w_s, v_gm_b_s, v_gm_w_out, v_hg_w_in, v_hg_lb, v_hg_gn_g, v_hg_w_out, v_ffn_w_up, v_ffn_conv_w, v_ffn_conv_b, v_ffn_w_down, v_norm_g, v_ada_w, v_ada_b, v_final_g):
    Dm = D_MODEL
    xi, yi, ci = _position()
    chip = 2 * xi + yi
    dev = 4 * xi + 2 * yi + ci
    weights = dict(gm_w_in=gm_w_in, gm_ln_g=gm_ln_g, gm_ln_b=gm_ln_b, gm_w_s=gm_w_s, gm_b_s=gm_b_s,
                   gm_w_out=gm_w_out, hg_w_in=hg_w_in, hg_lb=hg_lb, hg_gn_g=hg_gn_g, hg_w_out=hg_w_out,
                   ffn_w_up=ffn_w_up, ffn_conv_w=ffn_conv_w, ffn_conv_b=ffn_conv_b, ffn_w_down=ffn_w_down,
                   norm_g=norm_g, ada_w=ada_w, ada_b=ada_b, final_g=final_g)
    mom_m = dict(gm_w_in=m_gm_w_in, gm_ln_g=m_gm_ln_g, gm_ln_b=m_gm_ln_b, gm_w_s=m_gm_w_s, gm_b_s=m_gm_b_s,
                 gm_w_out=m_gm_w_out, hg_w_in=m_hg_w_in, hg_lb=m_hg_lb, hg_gn_g=m_hg_gn_g, hg_w_out=m_hg_w_out,
                 ffn_w_up=m_ffn_w_up, ffn_conv_w=m_ffn_conv_w, ffn_conv_b=m_ffn_conv_b, ffn_w_down=m_ffn_w_down,
                 norm_g=m_norm_g, ada_w=m_ada_w, ada_b=m_ada_b, final_g=m_final_g)
    mom_v = dict(gm_w_in=v_gm_w_in, gm_ln_g=v_gm_ln_g, gm_ln_b=v_gm_ln_b, gm_w_s=v_gm_w_s, gm_b_s=v_gm_b_s,
                 gm_w_out=v_gm_w_out, hg_w_in=v_hg_w_in, hg_lb=v_hg_lb, hg_gn_g=v_hg_gn_g, hg_w_out=v_hg_w_out,
                 ffn_w_up=v_ffn_w_up, ffn_conv_w=v_ffn_conv_w, ffn_conv_b=v_ffn_conv_b, ffn_w_down=v_ffn_w_down,
                 norm_g=v_norm_g, ada_w=v_ada_w, ada_b=v_ada_b, final_g=v_final_g)
    order = list(weights)

    pos = jnp.stack([chip, ci]).astype(jnp.int32)
    shards, by_col = [], []
    for i in range(DEPTH):
        for k in _layer_kinds(i):
            shards.append(weights[k][i if k.startswith("ffn") else i // 2].astype(BF16))
            by_col.append(COL_SHARDED[k])
    placed = [_place_own(sh, col, pos, f"place_own{n}") for n, (sh, col) in enumerate(zip(shards, by_col))]
    gsems, sh_thru, ld_thru, _ = _gather_start(shards, placed, by_col, LAYER_WEIGHTS, "gather_start")

    pieces = [c, hg_lb, hg_gn_g, norm_g, ffn_conv_w]
    payload, offs = _pack(pieces)
    got = _allgather_small(payload, "gather_small")
    c_g, lb_g, gn_g, ng_g, cw_g = _unpack(got, offs, [p.shape for p in pieces])
    c_all = c_g.reshape(N_DEV, Dm)
    P = dict(hg_lb=_from_chips(lb_g, 1), hg_gn_g=_from_chips(gn_g, 1), norm_g=_from_chips(ng_g, 2),
             ffn_conv_w=_from_chips(cw_g, 2), gm_ln_g=gm_ln_g, gm_ln_b=gm_ln_b, gm_w_s=gm_w_s, gm_b_s=gm_b_s,
             ffn_conv_b=ffn_conv_b, final_g=final_g)

    cols = ada_w.shape[2]
    ada_b_sh = lax.dynamic_slice_in_dim(ada_b, chip * cols, cols, axis=1)
    mod_sh = _ada_fwd(c_all, ada_w, ada_b_sh, "ada_fwd")
    mod_g = _allgather_small(mod_sh.reshape(-1, LANES), "gather_mod").reshape(N_DEV, DEPTH, N_DEV, cols)
    mod_mine = lax.dynamic_index_in_dim(mod_g[0::2], dev, axis=2, keepdims=False)
    mod = jnp.transpose(mod_mine, (1, 0, 2)).reshape(DEPTH, N_CHIPS * cols)

    core = jnp.reshape(ci, (1,)).astype(jnp.int32)
    chip_arr = jnp.reshape(chip, (1,)).astype(jnp.int32)
    pending, held = [], {}

    def get_w(i, group, after):
        lo, hi = LAYER_WEIGHTS * i, LAYER_WEIGHTS * (i + 1)
        if i == 0:
            s = slice(lo, lo + 1) if group == 0 else slice(lo + 1, hi)
        elif group == 0:
            s = slice(lo, hi)
        else:
            return {}
        landed = _gather_wait(sh_thru[s], ld_thru[s], gsems[2 * i], gsems[2 * i + 1], after, by_col[s],
                              s.start - lo, f"gather_wait{i}_{group}")
        full = _forward_sibling(landed, by_col[s], [a.shape for a in shards[s]], f"gather_forward{i}_{group}")
        return dict(zip(_layer_kinds(i)[s.start - lo:s.stop - lo], full))

    def on_grads(i, gdict):
        if i > 0 and "ffn_w_up" in gdict:
            held[i] = gdict
            return 0.0
        gdict = {**held.pop(i, {}), **gdict}
        kinds = [k for k in _layer_kinds(i) if k in gdict]
        tag = f"{i}_ffn" if kinds[0] == "ffn_w_up" else f"{i}"
        g4 = []
        for k in kinds:
            g = gdict[k]
            if not COL_SHARDED[k]:
                R, C = g.shape
                g = g.reshape(N_CHIPS, 2, R // (2 * N_CHIPS), C)
            g4.append(g)
        from_sib = _swap_sibling(g4, f"reduce_swap{tag}", other_half=True)
        sums = [_add_own_half(g, r, core, f"chip_sum_{k}{i}") for g, r, k in zip(g4, from_sib, kinds)]
        send, recv, sums_thru, lands, token = _exchange_start(sums, f"reduce_start{tag}")
        pending.append((tag, i, kinds, send, recv, sums_thru, lands))
        return token[0, 0]

    loss_part, dx, G, small = _local_step(x[0], loss_target[0], mod, None, P, get_w, on_grads)

    sum_pieces = [loss_part[:, :1], small["final_g"], jnp.stack(small["gm_ln_g"]), jnp.stack(small["gm_ln_b"]),
                  jnp.stack(small["gm_w_s"]), jnp.stack(small["gm_b_s"]), jnp.stack(small["ffn_conv_b"]),
                  small["hg_lb"], jnp.stack(small["hg_gn_g"]), jnp.stack(small["norm_g"]),
                  jnp.stack(small["ffn_conv_w"])]
    dmod = jnp.concatenate(small["dmod"], axis=0)
    payload2, offs2 = _pack(sum_pieces + [dmod])
    placed2 = _place_row(payload2, jnp.reshape(dev, (1,)).astype(jnp.int32), "place_grads")
    bsend, brecv, p2_thru, l2_thru, small_token = _broadcast_start(payload2, placed2, "gather_grads_start")

    landed = {}
    for tag, i, kinds, send, recv, sums_thru, lands in pending:
        sums_i, lands_i = _exchange_wait(sums_thru, lands, send, recv, small_token, f"reduce_wait{tag}")
        for k, s_, l_ in zip(kinds, sums_i, lands_i):
            landed[(k, i)] = (l_, s_)
    own_halves = []
    for k in BIG:
        layers = [landed[(k, i)] for i in range(DEPTH) if (k, i) in landed]
        own_halves.append(_sum_chips([l_ for l_, _ in layers], [s_ for _, s_ in layers], chip_arr, f"sum_chips_{k}"))
    sib_halves = _swap_sibling(own_halves, "reduce_join")
    grads, deltas, new_m, new_v = {}, {}, {}, {}
    for k, own, recv in zip(BIG, own_halves, sib_halves):
        grads[k], deltas[k], new_m[k], new_v[k] = _adamw_halves(
            weights[k], own, recv, mom_m[k], mom_v[k], core, f"adamw_{k}")

    got2 = _broadcast_wait(p2_thru, l2_thru, bsend, brecv, new_v[BIG[-1]], "gather_grads_wait")
    dmod_all = _unpack(got2, offs2[-1:], [dmod.shape])[0]
    summed = _sum_devices(got2, "sum_devices")
    (loss_s, d_final_g, d_ln_g, d_ln_b, d_ws, d_bs, d_cb, d_lb, d_gn, d_ng, d_cw) = _unpack(
        summed, offs2[:-1], [(1,), final_g.shape, gm_ln_g.shape, gm_ln_b.shape, gm_w_s.shape, gm_b_s.shape,
                             ffn_conv_b.shape, (2, Dm), (2, Dm), (DEPTH, 2, Dm), (DEPTH, 3, 2 * FFN_HIDDEN)])
    grads.update(final_g=d_final_g, gm_ln_g=d_ln_g, gm_ln_b=d_ln_b, gm_w_s=d_ws, gm_b_s=d_bs, ffn_conv_b=d_cb)
    grads["hg_lb"] = lax.dynamic_slice_in_dim(d_lb, chip * hg_lb.shape[1], hg_lb.shape[1], axis=1)
    grads["hg_gn_g"] = lax.dynamic_slice_in_dim(d_gn, chip * hg_gn_g.shape[1], hg_gn_g.shape[1], axis=1)
    grads["norm_g"] = lax.dynamic_slice_in_dim(d_ng, chip * norm_g.shape[2], norm_g.shape[2], axis=2)
    grads["ffn_conv_w"] = lax.dynamic_slice_in_dim(d_cw, chip * ffn_conv_w.shape[2], ffn_conv_w.shape[2], axis=2)
    dmod_sh = lax.dynamic_slice_in_dim(dmod_all, chip * cols, cols, axis=2)
    grads["ada_w"] = _ada_bwd(c_all, jnp.transpose(dmod_sh, (1, 0, 2)), "ada_bwd")
    grads["ada_b"] = _sum_devices(dmod_all.reshape(N_DEV, -1, LANES), "sum_ada_b").reshape(ada_b.shape)

    for k in order:
        if k in BIG:
            continue
        w = weights[k]
        shp = w.shape
        view = (-1, shp[-1]) if w.ndim > 1 else (8, -1)
        d, m2, v2 = _adamw(w.reshape(view), grads[k].reshape(view), mom_m[k].reshape(view), mom_v[k].reshape(view),
                           f"adamw_{k}")
        deltas[k], new_m[k], new_v[k] = d.reshape(shp), m2.reshape(shp), v2.reshape(shp)
        grads[k] = grads[k].reshape(shp)

    loss = loss_s.reshape(())
    return (loss, dx[None], *[grads[k] for k in order], *[deltas[k] for k in order],
            *[new_m[k] for k in order], *[new_v[k] for k in order])
```

```python
import functools

import jax
import jax.numpy as jnp
from jax import lax
from jax.experimental import pallas as pl
from jax.experimental.pallas import tpu as pltpu

F32 = jnp.float32
BF16 = jnp.bfloat16
HI = lax.Precision.HIGHEST
X3 = lax.Precision.HIGH
GRAD_WIRE = BF16
MESH = pl.DeviceIdType.MESH

D_MODEL = 1024
DEPTH = 4
EPS = 1e-6
GM_WIDTH = 2048
GM_HEADS = 8
GM_BLOCK = 128
GM_HEAD_DIM = 256
CHUNK = 64
HG_HEADS = 8
HG_DIM = 128
FFN_HIDDEN = 2816
N_CHIPS = 4
N_DEV = 8

ADAM_LR = 0.001
ADAM_B1 = 0.9
ADAM_B2 = 0.999
ADAM_EPS = 1e-08
ADAM_WD = 0.01
ADAM_STEP = 10

VMEM_LIMIT_BYTES = 56 * 1024 * 1024
ROW_TILE = 256
LANES = 128

_SQRT_HALF = 0.7071067811865476
_INV_SQRT_2PI = 0.3989422804014327


def _pick(dim, prefs):
    for p in prefs:
        if dim % p == 0:
            return p
    return dim


def _params(sem):
    return pltpu.CompilerParams(dimension_semantics=sem, vmem_limit_bytes=VMEM_LIMIT_BYTES)


def _cdf(x):
    return 0.5 * (1.0 + lax.erf(x * _SQRT_HALF))


def _pdf(x):
    return jnp.exp(-0.5 * x * x) * _INV_SQRT_2PI


def _sig(x):
    return jax.nn.sigmoid(x)


def _dot(a, b, dims, prec=None):
    return lax.dot_general(a, b, (dims, ((), ())), precision=prec, preferred_element_type=F32)


NN = ((1,), (0,))
NT = ((1,), (1,))
TN = ((0,), (0,))


MM_VMEM_BUDGET = 40 * 1024 * 1024


def _mm_tiles(mode, M, N, K, a_bytes, b_bytes, exchange_out):
    tn = _pick(N, (1408, 1024, 512, 256, 128))
    tms = [t for t in (1408, 1024, 512, 256, 128) if M % t == 0 and not (exchange_out and (M // 2) % t)] or [M]
    tks = [K] + [t for t in (2816, 2048, 1408, 1024, 512, 256, 128) if t < K and K % t == 0]

    def fits(tm, tk):
        acc = tm * tn * 4 if tk < K else 0
        return 2 * tm * tk * a_bytes + 2 * tk * tn * b_bytes + 2 * tm * tn * 4 + acc <= MM_VMEM_BUDGET

    for min_tm in (min(512, tms[0]), 0):
        for tk in tks:
            for tm in tms:
                if tm >= min_tm and fits(tm, tk):
                    return tm, tn, tk
    return tms[-1], tn, tks[-1]


def _mm(a, b, mode, name, b_layer=None, out_dtype=F32, exchange_out=False):
    b2 = b.shape[-2:]
    if mode == "nn":
        (M, K), (_, N) = a.shape, b2
    elif mode == "nt":
        (M, K), (N, _) = a.shape, b2
    else:
        (K, M), (_, N) = a.shape, b2
    tm, tn, tk = _mm_tiles(mode, M, N, K, a.dtype.itemsize, b.dtype.itemsize, exchange_out)
    nk = K // tk
    dims = {"nn": NN, "nt": NT, "tn": TN}[mode]

    def body(a_ref, b_ref, o_ref, *scratch):
        part = _dot(a_ref[...].astype(BF16), b_ref[...].astype(BF16), dims)
        if nk == 1:
            o_ref[...] = part.astype(o_ref.dtype)
            return
        (acc_ref,) = scratch
        k = pl.program_id(2)

        @pl.when(k == 0)
        def _():
            acc_ref[...] = part

        @pl.when(k > 0)
        def _():
            acc_ref[...] += part

        @pl.when(k == nk - 1)
        def _():
            o_ref[...] = acc_ref[...].astype(o_ref.dtype)

    if mode == "tn":
        a_spec = pl.BlockSpec((tk, tm), lambda i, j, k: (k, i))
    else:
        a_spec = pl.BlockSpec((tm, tk), lambda i, j, k: (i, k))
    bblk = (tk, tn) if mode in ("nn", "tn") else (tn, tk)
    bidx = (lambda i, j, k: (k, j)) if mode in ("nn", "tn") else (lambda i, j, k: (j, k))
    if b_layer is None:
        b_spec = pl.BlockSpec(bblk, bidx)
    else:
        b_spec = pl.BlockSpec((None,) + bblk, lambda i, j, k: (b_layer,) + bidx(i, j, k))
    if exchange_out:
        mh, cw = M // 2, N // N_CHIPS
        assert mh % tm == 0 and cw % tn == 0
        out_shape = jax.ShapeDtypeStruct((N_CHIPS, 2, mh, cw), out_dtype)
        o_spec = pl.BlockSpec(
            (None, None, tm, tn),
            lambda i, j, k: (j // (cw // tn), i // (mh // tm), i % (mh // tm), j % (cw // tn)))
    else:
        out_shape = jax.ShapeDtypeStruct((M, N), out_dtype)
        o_spec = pl.BlockSpec((tm, tn), lambda i, j, k: (i, j))
    return pl.pallas_call(
        body, name=name, out_shape=out_shape, grid=(M // tm, N // tn, nk),
        in_specs=[a_spec, b_spec], out_specs=o_spec,
        scratch_shapes=[] if nk == 1 else [pltpu.VMEM((tm, tn), F32)],
        compiler_params=_params(("parallel", "parallel", "arbitrary")),
    )(a, b)


def _row_spec(tr, width):
    return pl.BlockSpec((tr, width), lambda i: (i, 0))


def _vec_spec(width, rows=1):
    return pl.BlockSpec((rows, width), lambda i: (0, 0))


def _norm_fwd(x, y, gate, g, sc, sh, name):
    T, Dm = x.shape
    tr = _pick(T, (ROW_TILE,))
    has_res = y is not None

    def body(*refs):
        if has_res:
            x_ref, y_ref, gate_ref, g_ref, sc_ref, sh_ref, xo_ref, h_ref = refs
            xv = x_ref[...] + gate_ref[...] * y_ref[...]
            xo_ref[...] = xv
        else:
            x_ref, g_ref, sc_ref, sh_ref, h_ref = refs
            xv = x_ref[...]
        rstd = lax.rsqrt(jnp.mean(xv * xv, axis=-1, keepdims=True) + EPS)
        h_ref[...] = ((xv * rstd * g_ref[...]) * (1.0 + sc_ref[...]) + sh_ref[...]).astype(BF16)

    row, vec = _row_spec(tr, Dm), _vec_spec(Dm)
    if has_res:
        ins, in_specs = (x, y, gate, g, sc, sh), [row, row, vec, vec, vec, vec]
        out_shape = (jax.ShapeDtypeStruct((T, Dm), F32), jax.ShapeDtypeStruct((T, Dm), BF16))
        out_specs = (row, row)
    else:
        ins, in_specs = (x, g, sc, sh), [row, vec, vec, vec]
        out_shape = jax.ShapeDtypeStruct((T, Dm), BF16)
        out_specs = row
    out = pl.pallas_call(body, name=name, out_shape=out_shape, grid=(T // tr,), in_specs=in_specs,
                         out_specs=out_specs, compiler_params=_params(("parallel",)))(*ins)
    return out if has_res else (x, out)


def _norm_bwd(x, dh, dxo, g, sc, y_prev, gate_prev, name):
    T, Dm = x.shape
    tr = _pick(T, (ROW_TILE,))
    has_prev = y_prev is not None

    def body(*refs):
        if has_prev:
            x_ref, dh_ref, dxo_ref, g_ref, sc_ref, yp_ref, gp_ref, dx_ref, dyp_ref, s1_ref, s2_ref, s3_ref = refs
        else:
            x_ref, dh_ref, dxo_ref, g_ref, sc_ref, dx_ref, s1_ref, s2_ref = refs

        @pl.when(pl.program_id(0) == 0)
        def _():
            s1_ref[...] = jnp.zeros_like(s1_ref)
            s2_ref[...] = jnp.zeros_like(s2_ref)
            if has_prev:
                s3_ref[...] = jnp.zeros_like(s3_ref)

        xv = x_ref[...]
        rstd = lax.rsqrt(jnp.mean(xv * xv, axis=-1, keepdims=True) + EPS)
        xhat = xv * rstd
        dh = dh_ref[...]
        dxhat = dh * (g_ref[...] * (1.0 + sc_ref[...]))
        dx = dxo_ref[...] + rstd * (dxhat - xhat * jnp.mean(dxhat * xhat, axis=-1, keepdims=True))
        dx_ref[...] = dx
        s1_ref[...] += jnp.sum(dh, axis=0, keepdims=True)
        s2_ref[...] += jnp.sum(dh * xhat, axis=0, keepdims=True)
        if has_prev:
            dyp_ref[...] = (gp_ref[...] * dx).astype(BF16)
            s3_ref[...] += jnp.sum(dx * yp_ref[...], axis=0, keepdims=True)

    row, vec = _row_spec(tr, Dm), _vec_spec(Dm)
    vshape = jax.ShapeDtypeStruct((1, Dm), F32)
    if has_prev:
        ins, in_specs = (x, dh, dxo, g, sc, y_prev, gate_prev), [row, row, row, vec, vec, row, vec]
        out_shape = (jax.ShapeDtypeStruct((T, Dm), F32), jax.ShapeDtypeStruct((T, Dm), BF16), vshape, vshape, vshape)
        out_specs = (row, row, vec, vec, vec)
    else:
        ins, in_specs = (x, dh, dxo, g, sc), [row, row, row, vec, vec]
        out_shape = (jax.ShapeDtypeStruct((T, Dm), F32), vshape, vshape)
        out_specs = (row, vec, vec)
    return pl.pallas_call(body, name=name, out_shape=out_shape, grid=(T // tr,), in_specs=in_specs,
                          out_specs=out_specs, compiler_params=_params(("arbitrary",)))(*ins)


def _loss_head(x, y, gate, fg, tgt, name):
    T, Dm = x.shape
    tr = _pick(T, (ROW_TILE,))
    nsteps = T // tr

    def body(x_ref, y_ref, gate_ref, fg_ref, t_ref, dx_ref, dy_ref, loss_ref, sfg_ref, sg_ref, acc_ref):
        i = pl.program_id(0)

        @pl.when(i == 0)
        def _():
            acc_ref[...] = jnp.zeros_like(acc_ref)
            sfg_ref[...] = jnp.zeros_like(sfg_ref)
            sg_ref[...] = jnp.zeros_like(sg_ref)

        yv = y_ref[...]
        xv = x_ref[...] + gate_ref[...] * yv
        rstd = lax.rsqrt(jnp.mean(xv * xv, axis=-1, keepdims=True) + EPS)
        xhat = xv * rstd
        err = xhat * fg_ref[...] - t_ref[...]
        acc_ref[...] += jnp.sum(err * err, axis=0, keepdims=True)
        dyn = err * (1.0 / Dm)
        sfg_ref[...] += jnp.sum(dyn * xhat, axis=0, keepdims=True)
        dxhat = dyn * fg_ref[...]
        dx = rstd * (dxhat - xhat * jnp.mean(dxhat * xhat, axis=-1, keepdims=True))
        dx_ref[...] = dx
        dy_ref[...] = (gate_ref[...] * dx).astype(BF16)
        sg_ref[...] += jnp.sum(dx * yv, axis=0, keepdims=True)

        @pl.when(i == nsteps - 1)
        def _():
            total = jnp.sum(acc_ref[...], axis=1, keepdims=True) * (0.5 / Dm)
            loss_ref[...] = jnp.broadcast_to(total, loss_ref.shape)

    row, vec = _row_spec(tr, Dm), _vec_spec(Dm)
    vshape = jax.ShapeDtypeStruct((1, Dm), F32)
    return pl.pallas_call(
        body, name=name, grid=(nsteps,),
        out_shape=(jax.ShapeDtypeStruct((T, Dm), F32), jax.ShapeDtypeStruct((T, Dm), BF16),
                   jax.ShapeDtypeStruct((1, LANES), F32), vshape, vshape),
        in_specs=[row, row, vec, vec, row], out_specs=(row, row, _vec_spec(LANES), vec, vec),
        scratch_shapes=[pltpu.VMEM((1, Dm), F32)], compiler_params=_params(("arbitrary",)),
    )(x, y, gate, fg, tgt)


def _spatial_mask():
    r = lax.broadcasted_iota(jnp.int32, (GM_BLOCK, GM_BLOCK), 0) // CHUNK
    c = lax.broadcasted_iota(jnp.int32, (GM_BLOCK, GM_BLOCK), 1) // CHUNK
    return r >= c


def _gm_specs(tr):
    return [_row_spec(tr, 2 * GM_WIDTH), _vec_spec(GM_WIDTH), _vec_spec(GM_WIDTH),
            pl.BlockSpec((GM_HEADS, GM_BLOCK, GM_BLOCK), lambda i: (0, 0, 0)),
            pl.BlockSpec((GM_HEADS, GM_BLOCK, 1), lambda i: (0, 0, 0))]


def _gm_mid_fwd(a, ln_g, ln_b, ws, bs3, name):
    T = a.shape[0]
    tr = _pick(T, (ROW_TILE,))
    W = GM_WIDTH

    def body(a_ref, lg_ref, lb_ref, ws_ref, bs_ref, o_ref, vn_scr):
        av = a_ref[:, W:]
        v = av * _cdf(av)
        vc = v - jnp.mean(v, axis=-1, keepdims=True)
        rstd = lax.rsqrt(jnp.mean(vc * vc, axis=-1, keepdims=True) + EPS)
        vn_scr[...] = (vc * rstd * lg_ref[...] + lb_ref[...]).astype(BF16)
        mask = _spatial_mask()
        for h in range(GM_HEADS):
            w = jnp.where(mask, ws_ref[h], 0.0).astype(BF16)
            cs = slice(h * GM_HEAD_DIM, (h + 1) * GM_HEAD_DIM)
            for blk in range(tr // GM_BLOCK):
                rs = slice(blk * GM_BLOCK, (blk + 1) * GM_BLOCK)
                s = _dot(w, vn_scr[rs, cs], NN) + bs_ref[h]
                au = a_ref[rs, cs]
                o_ref[rs, cs] = (au * _cdf(au) * s).astype(BF16)

    return pl.pallas_call(
        body, name=name, out_shape=jax.ShapeDtypeStruct((T, W), BF16), grid=(T // tr,),
        in_specs=_gm_specs(tr), out_specs=_row_spec(tr, W),
        scratch_shapes=[pltpu.VMEM((tr, W), BF16)], compiler_params=_params(("parallel",)),
    )(a, ln_g, ln_b, ws, bs3)


def _gm_mid_bwd(a, dgated, ln_g, ln_b, ws, bs3, name):
    T = a.shape[0]
    tr = _pick(T, (ROW_TILE,))
    W = GM_WIDTH
    nsteps = T // tr

    def body(a_ref, dg_ref, lg_ref, lb_ref, ws_ref, bs_ref, da_ref, dws_ref, dbs_ref, dlg_ref, dlb_ref,
             vn_scr, vhat_scr, dvn_scr, dsum_scr):
        i = pl.program_id(0)

        @pl.when(i == 0)
        def _():
            dws_ref[...] = jnp.zeros_like(dws_ref)
            dbs_ref[...] = jnp.zeros_like(dbs_ref)
            dlg_ref[...] = jnp.zeros_like(dlg_ref)
            dlb_ref[...] = jnp.zeros_like(dlb_ref)
            dsum_scr[...] = jnp.zeros_like(dsum_scr)

        av = a_ref[:, W:]
        cdf_v = _cdf(av)
        v = av * cdf_v
        vc = v - jnp.mean(v, axis=-1, keepdims=True)
        rstd = lax.rsqrt(jnp.mean(vc * vc, axis=-1, keepdims=True) + EPS)
        vhat_scr[...] = vc * rstd
        vn_scr[...] = (vhat_scr[...] * lg_ref[...] + lb_ref[...]).astype(BF16)
        mask = _spatial_mask()
        for h in range(GM_HEADS):
            w = jnp.where(mask, ws_ref[h], 0.0).astype(BF16)
            cs = slice(h * GM_HEAD_DIM, (h + 1) * GM_HEAD_DIM)
            for blk in range(tr // GM_BLOCK):
                rs = slice(blk * GM_BLOCK, (blk + 1) * GM_BLOCK)
                vnb = vn_scr[rs, cs]
                s = _dot(w, vnb, NN) + bs_ref[h]
                au = a_ref[rs, cs]
                cdf_u = _cdf(au)
                dg = dg_ref[rs, cs]
                ds = dg * (au * cdf_u)
                da_ref[rs, cs] = (dg * s * (cdf_u + au * _pdf(au))).astype(BF16)
                dsb = ds.astype(BF16)
                dvn_scr[rs, cs] = _dot(w, dsb, TN)
                dws_ref[h] += _dot(dsb, vnb, NT)
                dsum_scr[:, cs] += ds
        dvn = dvn_scr[...]
        vhat = vhat_scr[...]
        dlg_ref[...] += jnp.sum(dvn * vhat, axis=0, keepdims=True)
        dlb_ref[...] += jnp.sum(dvn, axis=0, keepdims=True)
        dvh = dvn * lg_ref[...]
        dv = rstd * (dvh - jnp.mean(dvh, axis=-1, keepdims=True)
                     - vhat * jnp.mean(dvh * vhat, axis=-1, keepdims=True))
        da_ref[:, W:] = (dv * (cdf_v + av * _pdf(av))).astype(BF16)

        @pl.when(i == nsteps - 1)
        def _():
            for h in range(GM_HEADS):
                dws_ref[h] = jnp.where(mask, dws_ref[h], 0.0)
            col_head = lax.broadcasted_iota(jnp.int32, (W, GM_BLOCK), 0) // GM_HEAD_DIM
            sel = (col_head == lax.broadcasted_iota(jnp.int32, (W, GM_BLOCK), 1)).astype(F32)
            dbs_ref[...] = _dot(dsum_scr[...], sel, NN, HI)

    vshape = jax.ShapeDtypeStruct((1, W), F32)
    return pl.pallas_call(
        body, name=name, grid=(nsteps,),
        out_shape=(jax.ShapeDtypeStruct((T, 2 * W), BF16), jax.ShapeDtypeStruct((GM_HEADS, GM_BLOCK, GM_BLOCK), F32),
                   jax.ShapeDtypeStruct((GM_BLOCK, GM_BLOCK), F32), vshape, vshape),
        in_specs=[_gm_specs(tr)[0], _row_spec(tr, W)] + _gm_specs(tr)[1:],
        out_specs=(_row_spec(tr, 2 * W), pl.BlockSpec((GM_HEADS, GM_BLOCK, GM_BLOCK), lambda i: (0, 0, 0)),
                   pl.BlockSpec((GM_BLOCK, GM_BLOCK), lambda i: (0, 0)), _vec_spec(W), _vec_spec(W)),
        scratch_shapes=[pltpu.VMEM((tr, W), BF16), pltpu.VMEM((tr, W), F32), pltpu.VMEM((tr, W), F32),
                        pltpu.VMEM((GM_BLOCK, W), F32)],
        compiler_params=_params(("arbitrary",)),
    )(a, dgated, ln_g, ln_b, ws, bs3)


SUB = 16
EXP_CLAMP = 80.0


def _tri(lower):
    r = lax.broadcasted_iota(jnp.int32, (CHUNK, CHUNK), 0)
    c = lax.broadcasted_iota(jnp.int32, (CHUNK, CHUNK), 1)
    return (r >= c) if lower else (c >= r)


def _score_masks():
    i = lax.broadcasted_iota(jnp.int32, (CHUNK, CHUNK), 0)
    j = lax.broadcasted_iota(jnp.int32, (CHUNK, CHUNK), 1)
    bi, bj = i // SUB, j // SUB
    diag = (bi == bj) & (i >= j)
    pair = (bi % 2 == 1) & (bj == bi - 1)
    half = (i >= CHUNK // 2) & (j < CHUNK // 2)
    return diag, pair, half


def _dot01(m, x):
    x1 = x.astype(BF16)
    rest = x - x1.astype(F32)
    x2 = rest.astype(BF16)
    x3 = (rest - x2.astype(F32)).astype(BF16)
    return _dot(m, x1, NN) + (_dot(m, x2, NN) + _dot(m, x3, NN))


def _block_rows(b, offset):
    parts = []
    for blk in range(0, CHUNK, SUB):
        r = blk + offset
        parts.append(jnp.zeros((SUB, b.shape[1]), F32) if r < 0 else jnp.broadcast_to(b[r:r + 1], (SUB, b.shape[1])))
    return jnp.concatenate(parts, axis=0)


def _hg_gates(p_ref, lb_ref, lower):
    Dm = D_MODEL
    heads = []
    for h in range(HG_HEADS):
        c0 = h * HG_DIM
        qr = p_ref[:, c0:c0 + HG_DIM]
        fz = p_ref[:, Dm + c0:Dm + c0 + HG_DIM]
        lbh = lb_ref[:, c0:c0 + HG_DIM]
        sg = _sig(fz)
        f = lbh + (1.0 - lbh) * sg
        sq = _sig(qr)
        heads.append(dict(qr=qr, v=p_ref[:, 2 * Dm + c0:2 * Dm + c0 + HG_DIM],
                          gt=p_ref[:, 3 * Dm + c0:3 * Dm + c0 + HG_DIM], lbh=lbh, sg=sg, f=f, gl=jnp.log(f),
                          kk=1.0 - f, sq=sq, q=qr * sq))
    for g in heads:
        g["b"] = _dot01(lower, g.pop("gl"))
    for g in heads:
        g.update(_hg_scalings(g["q"], g["kk"], g.pop("b")))
    return heads


def _hg_scalings(q, kk, b):
    r_mid = _block_rows(b, SUB // 2 - 1)
    r_prev = _block_rows(b, -1)
    r_end = _block_rows(b, SUB - 1)
    r_half = jnp.broadcast_to(b[CHUNK // 2 - 1:CHUNK // 2], b.shape)
    bc = b[CHUNK - 1:CHUNK]
    eqs = (jnp.exp(jnp.clip(b - r_mid, -EXP_CLAMP, EXP_CLAMP)), jnp.exp(jnp.minimum(b - r_prev, 0.0)),
           jnp.exp(jnp.minimum(b - r_half, 0.0)))
    eks = (jnp.exp(jnp.clip(r_mid - b, -EXP_CLAMP, EXP_CLAMP)), jnp.exp(jnp.minimum(r_end - b, 0.0)),
           jnp.exp(jnp.minimum(r_half - b, 0.0)))
    eb = jnp.exp(b)
    ec = jnp.exp(bc - b)
    return dict(eqs=eqs, eks=eks, eb=eb, ec=ec, e_end=jnp.exp(bc), qs=[q * e for e in eqs],
                ks=[kk * e for e in eks], qe=q * eb, ke=kk * ec)


def _scores(g, masks):
    a = None
    for qs, ks, m in zip(g["qs"], g["ks"], masks):
        part = jnp.where(m, _dot(qs.astype(BF16), ks.astype(BF16), NT), 0.0)
        a = part if a is None else a + part
    return a


def _hg_scan_fwd(p, lb, gn, name):
    T = p.shape[0]
    nc = T // CHUNK
    Dm = D_MODEL

    def body(p_ref, lb_ref, gn_ref, o_ref, og_ref, so_ref, st_ref):
        @pl.when(pl.program_id(0) == 0)
        def _():
            st_ref[...] = jnp.zeros_like(st_ref)

        masks = _score_masks()
        heads = _hg_gates(p_ref, lb_ref, _tri(True).astype(BF16))
        states = [st_ref[h] for h in range(HG_HEADS)]
        scores = [_scores(g, masks) for g in heads]
        outs = [_dot(a.astype(BF16), g["v"].astype(BF16), NN) + _dot(g["qe"], st, NT, X3)
                for g, a, st in zip(heads, scores, states)]
        new_states = [st * g["e_end"] + _dot(g["v"], g["ke"], TN, X3) for g, st in zip(heads, states)]
        for h, (g, o, st, st2) in enumerate(zip(heads, outs, states, new_states)):
            cs = slice(h * HG_DIM, (h + 1) * HG_DIM)
            so_ref[0, h] = st
            st_ref[h] = st2
            o_ref[:, cs] = o
            r = lax.rsqrt(jnp.mean(o * o, axis=-1, keepdims=True) + EPS)
            gt = g["gt"]
            og_ref[:, cs] = (((o * r) * gn_ref[:, cs]).astype(F32) * (gt * _sig(gt))).astype(BF16)

    return pl.pallas_call(
        body, name=name, grid=(nc,),
        out_shape=(jax.ShapeDtypeStruct((T, Dm), F32), jax.ShapeDtypeStruct((T, Dm), BF16),
                   jax.ShapeDtypeStruct((nc, HG_HEADS, HG_DIM, HG_DIM), F32)),
        in_specs=[_row_spec(CHUNK, 4 * Dm), _vec_spec(Dm), _vec_spec(Dm)],
        out_specs=(_row_spec(CHUNK, Dm), _row_spec(CHUNK, Dm),
                   pl.BlockSpec((1, HG_HEADS, HG_DIM, HG_DIM), lambda i: (i, 0, 0, 0))),
        scratch_shapes=[pltpu.VMEM((HG_HEADS, HG_DIM, HG_DIM), F32)],
        compiler_params=_params(("arbitrary",)),
    )(p, lb, gn)


def _hg_scan_bwd(p, lb, gn, o, dog, states, name):
    T = p.shape[0]
    nc = T // CHUNK
    Dm = D_MODEL

    def rev(i):
        return nc - 1 - i

    def body(p_ref, lb_ref, gn_ref, o_ref, dog_ref, st_in_ref, dp_ref, dlb_ref, dgn_ref, dst_ref, carry_ref):
        @pl.when(pl.program_id(0) == 0)
        def _():
            dst_ref[...] = jnp.zeros_like(dst_ref)
            carry_ref[...] = jnp.zeros_like(carry_ref)
            dlb_ref[...] = jnp.zeros_like(dlb_ref)
            dgn_ref[...] = jnp.zeros_like(dgn_ref)

        upper = _tri(False).astype(BF16)
        masks = _score_masks()
        heads = _hg_gates(p_ref, lb_ref, _tri(True).astype(BF16))
        for h, g in enumerate(heads):
            cs = slice(h * HG_DIM, (h + 1) * HG_DIM)
            oh = o_ref[:, cs]
            r = lax.rsqrt(jnp.mean(oh * oh, axis=-1, keepdims=True) + EPS)
            on = oh * r
            gt = g["gt"]
            sgt = _sig(gt)
            sil = gt * sgt
            dogh = dog_ref[:, cs]
            gnh = gn_ref[:, cs]
            don = dogh * gnh * sil
            g["dgn"] = jnp.sum(dogh * on * sil, axis=0, keepdims=True)
            g["dgate"] = dogh * on * gnh * (sgt * (1.0 + gt * (1.0 - sgt)))
            g["do"] = r * (don - on * jnp.mean(don * on, axis=-1, keepdims=True))
            g["dst"] = dst_ref[h]
            g["st"] = st_in_ref[0, h]
            g["carry"] = carry_ref[h]
        for g in heads:
            g["a"] = _scores(g, masks)
            g["dob"] = g["do"].astype(BF16)
            g["da"] = _dot(g["dob"], g["v"].astype(BF16), NT)
        for g in heads:
            g["dv"] = _dot(g["a"].astype(BF16), g["dob"], TN) + _dot(g["ke"].astype(BF16), g["dst"].astype(BF16), NT)
            g["dq"] = _dot(g["do"], g["st"], NN, X3) * g["eb"]
            g["dk"] = _dot(g["v"], g["dst"], NN, X3) * g["ec"]
            g["dst2"] = g["dst"] * g["e_end"] + _dot(g["do"], g["qe"], TN, X3)
        for lvl in range(3):
            for g in heads:
                dam = jnp.where(masks[lvl], g["da"], 0.0)
                g["dq"] = g["dq"] + _dot(dam, g["ks"][lvl], NN, X3) * g["eqs"][lvl]
                g["dk"] = g["dk"] + _dot(dam, g["qs"][lvl], TN, X3) * g["eks"][lvl]
        for g in heads:
            g["dgd"] = g["q"] * g["dq"] - g["kk"] * g["dk"]
            g["dgl"] = _dot01(upper, g["dgd"]) + g["carry"]
        for h, g in enumerate(heads):
            c0 = h * HG_DIM
            cs = slice(c0, c0 + HG_DIM)
            df = g["dgl"] / g["f"] - g["dk"]
            sg, sq, qr = g["sg"], g["sq"], g["qr"]
            dst_ref[h] = g["dst2"]
            carry_ref[h] = g["carry"] + jnp.sum(g["dgd"], axis=0, keepdims=True)
            dgn_ref[:, cs] += g["dgn"]
            dlb_ref[:, cs] += jnp.sum(df * (1.0 - sg), axis=0, keepdims=True)
            dp_ref[:, c0:c0 + HG_DIM] = (g["dq"] * (sq * (1.0 + qr * (1.0 - sq)))).astype(BF16)
            dp_ref[:, Dm + c0:Dm + c0 + HG_DIM] = (df * (1.0 - g["lbh"]) * sg * (1.0 - sg)).astype(BF16)
            dp_ref[:, 2 * Dm + c0:2 * Dm + c0 + HG_DIM] = g["dv"].astype(BF16)
            dp_ref[:, 3 * Dm + c0:3 * Dm + c0 + HG_DIM] = g["dgate"].astype(BF16)

    vshape = jax.ShapeDtypeStruct((1, Dm), F32)
    rrow = lambda w: pl.BlockSpec((CHUNK, w), lambda i: (rev(i), 0))
    return pl.pallas_call(
        body, name=name, grid=(nc,),
        out_shape=(jax.ShapeDtypeStruct((T, 4 * Dm), BF16), vshape, vshape),
        in_specs=[rrow(4 * Dm), _vec_spec(Dm), _vec_spec(Dm), rrow(Dm), rrow(Dm),
                  pl.BlockSpec((1, HG_HEADS, HG_DIM, HG_DIM), lambda i: (rev(i), 0, 0, 0))],
        out_specs=(rrow(4 * Dm), _vec_spec(Dm), _vec_spec(Dm)),
        scratch_shapes=[pltpu.VMEM((HG_HEADS, HG_DIM, HG_DIM), F32), pltpu.VMEM((HG_HEADS, 1, HG_DIM), F32)],
        compiler_params=_params(("arbitrary",)),
    )(p, lb, gn, o, dog, states)


def _lb_fwd(hg_lb, name):
    def body(a_ref, o_ref):
        a0, a1 = a_ref[0:1], a_ref[1:2]
        m = jnp.maximum(a0, a1)
        e0, e1 = jnp.exp(a0 - m), jnp.exp(a1 - m)
        p0, p1 = e0 / (e0 + e1), e1 / (e0 + e1)
        o_ref[0:1] = p0 - p0
        o_ref[1:2] = (p0 + p1) - p0

    return pl.pallas_call(body, name=name, out_shape=jax.ShapeDtypeStruct(hg_lb.shape, F32))(hg_lb)


def _lb_bwd(hg_lb, dlb_all, name):
    def body(a_ref, d_ref, o_ref):
        a0, a1 = a_ref[0:1], a_ref[1:2]
        m = jnp.maximum(a0, a1)
        e0, e1 = jnp.exp(a0 - m), jnp.exp(a1 - m)
        p0, p1 = e0 / (e0 + e1), e1 / (e0 + e1)
        d1 = d_ref[1:2]
        o_ref[0:1] = -p0 * p1 * d1
        o_ref[1:2] = p1 * (1.0 - p1) * d1

    return pl.pallas_call(body, name=name, out_shape=jax.ShapeDtypeStruct(hg_lb.shape, F32))(hg_lb, dlb_all)


CONV_COLS_FWD = 256
CONV_COLS_BWD = 128


def _conv_fwd(a, w, b, name):
    T = a.shape[0]
    Fh = FFN_HIDDEN
    tr = _pick(T, (ROW_TILE,))
    cw = CONV_COLS_FWD
    hb = tr // 8

    def body(a_ref, ap_ref, w_ref, b_ref, m_ref):
        m0 = (pl.program_id(0) > 0).astype(F32)

        def conv(cc):
            x = jnp.concatenate([ap_ref[:, pl.ds(cc, cw)] * m0, a_ref[:, pl.ds(cc, cw)]], axis=0)
            wv = w_ref[:, pl.ds(cc, cw)]
            y = b_ref[:, pl.ds(cc, cw)] + wv[2:3] * x + wv[1:2] * pltpu.roll(x, 1, axis=0) \
                + wv[0:1] * pltpu.roll(x, 2, axis=0)
            return y[8:]

        def step(c, carry):
            c0 = pl.multiple_of(c * cw, cw)
            c1 = pl.multiple_of(Fh + c * cw, cw)
            yg, yv = conv(c0), conv(c1)
            m_ref[:, pl.ds(c0, cw)] = (yg * _cdf(yg) * yv).astype(BF16)
            return carry

        lax.fori_loop(0, Fh // cw, step, 0)

    return pl.pallas_call(
        body, name=name, out_shape=jax.ShapeDtypeStruct((T, Fh), BF16), grid=(T // tr,),
        in_specs=[_row_spec(tr, 2 * Fh), pl.BlockSpec((8, 2 * Fh), lambda i: (jnp.maximum(i * hb - 1, 0), 0)),
                  _vec_spec(2 * Fh, 3), _vec_spec(2 * Fh)],
        out_specs=_row_spec(tr, Fh), compiler_params=_params(("parallel",)),
    )(a, a, w, b)


def _conv_bwd(a, dm, w, b, name):
    T = a.shape[0]
    Fh = FFN_HIDDEN
    tr = _pick(T, (ROW_TILE,))
    cw = CONV_COLS_BWD
    hb = tr // 8
    nsteps = T // tr
    n = tr + 8

    def body(a_ref, ap_ref, an_ref, dm_ref, dmn_ref, w_ref, b_ref, da_ref, dw_ref, db_ref):
        i = pl.program_id(0)
        m0 = (i > 0).astype(F32)
        m1 = (i < nsteps - 1).astype(F32)

        @pl.when(i == 0)
        def _():
            dw_ref[...] = jnp.zeros_like(dw_ref)
            db_ref[...] = jnp.zeros_like(db_ref)

        def prep(cc):
            x = jnp.concatenate([ap_ref[:, pl.ds(cc, cw)] * m0, a_ref[:, pl.ds(cc, cw)],
                                 an_ref[:, pl.ds(cc, cw)] * m1], axis=0)
            wv = w_ref[:, pl.ds(cc, cw)]
            s1 = pltpu.roll(x, 1, axis=0)
            s2 = pltpu.roll(x, 2, axis=0)
            y = b_ref[:, pl.ds(cc, cw)] + wv[2:3] * x + wv[1:2] * s1 + wv[0:1] * s2
            return wv, x[8:], s1[8:], s2[8:], y[8:]

        def back(cc, dy, wv, x0, s1, s2):
            da = wv[2:3] * dy + wv[1:2] * pltpu.roll(dy, n - 1, axis=0) + wv[0:1] * pltpu.roll(dy, n - 2, axis=0)
            da_ref[:, pl.ds(cc, cw)] = da[:tr].astype(BF16)
            d = dy[:tr]
            db_ref[:, pl.ds(cc, cw)] += jnp.sum(d, axis=0, keepdims=True)
            dw_ref[2:3, pl.ds(cc, cw)] += jnp.sum(d * x0[:tr], axis=0, keepdims=True)
            dw_ref[1:2, pl.ds(cc, cw)] += jnp.sum(d * s1[:tr], axis=0, keepdims=True)
            dw_ref[0:1, pl.ds(cc, cw)] += jnp.sum(d * s2[:tr], axis=0, keepdims=True)

        def step(c, carry):
            c0 = pl.multiple_of(c * cw, cw)
            c1 = pl.multiple_of(Fh + c * cw, cw)
            dmx = jnp.concatenate([dm_ref[:, pl.ds(c0, cw)], dmn_ref[:, pl.ds(c0, cw)] * m1], axis=0)
            wg, xg, s1g, s2g, yg = prep(c0)
            wv, xv, s1v, s2v, yv = prep(c1)
            cg = _cdf(yg)
            back(c0, dmx * yv * (cg + yg * _pdf(yg)), wg, xg, s1g, s2g)
            back(c1, dmx * (yg * cg), wv, xv, s1v, s2v)
            return carry

        lax.fori_loop(0, Fh // cw, step, 0)

    prev = lambda wd: pl.BlockSpec((8, wd), lambda i: (jnp.maximum(i * hb - 1, 0), 0))
    nxt = lambda wd: pl.BlockSpec((8, wd), lambda i: (jnp.minimum((i + 1) * hb, T // 8 - 1), 0))
    return pl.pallas_call(
        body, name=name, grid=(nsteps,),
        out_shape=(jax.ShapeDtypeStruct((T, 2 * Fh), BF16), jax.ShapeDtypeStruct((3, 2 * Fh), F32),
                   jax.ShapeDtypeStruct((1, 2 * Fh), F32)),
        in_specs=[_row_spec(tr, 2 * Fh), prev(2 * Fh), nxt(2 * Fh), _row_spec(tr, Fh), nxt(Fh),
                  _vec_spec(2 * Fh, 3), _vec_spec(2 * Fh)],
        out_specs=(_row_spec(tr, 2 * Fh), _vec_spec(2 * Fh, 3), _vec_spec(2 * Fh)),
        compiler_params=_params(("arbitrary",)),
    )(a, a, a, dm, dm, w, b)


def _ada_fwd(c_all, ada_w, ada_b, name):
    L, Dm, cols = ada_w.shape
    tn = _pick(cols, (512, 256, 128))

    def body(c_ref, w_ref, b_ref, o_ref):
        cv = c_ref[...]
        cond = (cv * _sig(cv)).astype(BF16)
        o_ref[...] = _dot(cond, w_ref[...].astype(BF16), NN) + b_ref[...]

    return pl.pallas_call(
        body, name=name, out_shape=jax.ShapeDtypeStruct((L, N_DEV, cols), F32), grid=(L, cols // tn),
        in_specs=[pl.BlockSpec((N_DEV, Dm), lambda l, j: (0, 0)), pl.BlockSpec((None, Dm, tn), lambda l, j: (l, 0, j)),
                  pl.BlockSpec((None, 1, tn), lambda l, j: (l, 0, j))],
        out_specs=pl.BlockSpec((None, N_DEV, tn), lambda l, j: (l, 0, j)),
        compiler_params=_params(("parallel", "parallel")),
    )(c_all, ada_w, ada_b.reshape(L, 1, cols))


def _ada_bwd(c_all, dmod, name):
    L, _, cols = dmod.shape
    Dm = c_all.shape[1]
    tn = _pick(cols, (512, 256, 128))

    def body(c_ref, d_ref, o_ref):
        cv = c_ref[...]
        o_ref[...] = _dot(cv * _sig(cv), d_ref[...], TN, HI)

    return pl.pallas_call(
        body, name=name, out_shape=jax.ShapeDtypeStruct((L, Dm, cols), F32), grid=(L, cols // tn),
        in_specs=[pl.BlockSpec((N_DEV, Dm), lambda l, j: (0, 0)), pl.BlockSpec((None, N_DEV, tn), lambda l, j: (l, 0, j))],
        out_specs=pl.BlockSpec((None, Dm, tn), lambda l, j: (l, 0, j)),
        compiler_params=_params(("parallel", "parallel")),
    )(c_all, dmod)


def _add_own_half(g4, rb, core, name):
    S, _, rh, cw = g4.shape
    tr = _pick(rh, (256, 128, 176, 64))

    def body(core_ref, g_ref, r_ref, o_ref):
        o_ref[...] = (g_ref[...].astype(F32) + r_ref[...].astype(F32)).astype(GRAD_WIRE)

    return pl.pallas_call(
        body, name=name, out_shape=jax.ShapeDtypeStruct((S, rh, cw), BF16),
        grid_spec=pltpu.PrefetchScalarGridSpec(
            num_scalar_prefetch=1, grid=(S, rh // tr),
            in_specs=[pl.BlockSpec((None, None, tr, cw), lambda s, i, core_ref: (s, core_ref[0], i, 0)),
                      pl.BlockSpec((None, tr, cw), lambda s, i, core_ref: (s, i, 0))],
            out_specs=pl.BlockSpec((None, tr, cw), lambda s, i, core_ref: (s, i, 0))),
        compiler_params=_params(("parallel", "parallel")),
    )(core, g4, rb)


def _sum_chips(lands, sums, chip, name):
    L = len(lands)
    _, rh, cw = lands[0].shape
    tr = _pick(rh, (256, 128, 176, 64))

    def body(chip_ref, *refs):
        ld, cs, o_ref = refs[:L], refs[L:2 * L], refs[2 * L]
        me = chip_ref[0]
        for k in range(L):
            @pl.when(pl.program_id(0) == k)
            def _(k=k):
                own = cs[k][...].astype(F32)
                got = [ld[k][j].astype(F32) for j in range(3)]
                acc = None
                for t in range(N_CHIPS):
                    d = jnp.bitwise_xor(jnp.int32(t), me)
                    term = jnp.where(d == 0, own, jnp.where(d == 2, got[0], jnp.where(d == 1, got[1], got[2])))
                    acc = term if acc is None else acc + term
                o_ref[...] = acc

    frozen = lambda l, i, k: jnp.where(l == k, i, 0)
    in_specs = [pl.BlockSpec((3, tr, cw), lambda l, i, chip_ref, k=k: (0, frozen(l, i, k), 0)) for k in range(L)]
    in_specs += [pl.BlockSpec((None, tr, cw), lambda l, i, chip_ref, k=k: (chip_ref[0], frozen(l, i, k), 0))
                 for k in range(L)]
    return pl.pallas_call(
        body, name=name, out_shape=jax.ShapeDtypeStruct((L, rh, cw), F32),
        grid_spec=pltpu.PrefetchScalarGridSpec(
            num_scalar_prefetch=1, grid=(L, rh // tr), in_specs=in_specs,
            out_specs=pl.BlockSpec((None, tr, cw), lambda l, i, chip_ref: (l, i, 0))),
        compiler_params=_params(("arbitrary", "arbitrary")),
    )(chip, *lands, *sums)


def _sum_devices(gathered, name):
    n, R, _ = gathered.shape
    tr = _pick(R, (512, 448, 384, 256, 192, 128, 64, 32, 16, 8))

    def body(g_ref, o_ref):
        acc = g_ref[0]
        for d in range(1, n):
            acc = acc + g_ref[d]
        o_ref[...] = acc

    return pl.pallas_call(
        body, name=name, out_shape=jax.ShapeDtypeStruct((R, LANES), F32), grid=(R // tr,),
        in_specs=[pl.BlockSpec((n, tr, LANES), lambda i: (0, i, 0))], out_specs=pl.BlockSpec((tr, LANES), lambda i: (i, 0)),
        compiler_params=_params(("parallel",)),
    )(gathered)


def _adamw(w, g, m, v, name):
    R, C = w.shape
    tr = _pick(R, (256, 128, 64, 32, 16, 8))
    c1 = 1.0 / (1.0 - ADAM_B1 ** ADAM_STEP)
    c2 = 1.0 / (1.0 - ADAM_B2 ** ADAM_STEP)

    def body(w_ref, g_ref, m_ref, v_ref, d_ref, mo_ref, vo_ref):
        gv = g_ref[...]
        m2 = ADAM_B1 * m_ref[...] + (1.0 - ADAM_B1) * gv
        v2 = ADAM_B2 * v_ref[...] + (1.0 - ADAM_B2) * (gv * gv)
        mo_ref[...] = m2
        vo_ref[...] = v2
        d_ref[...] = -ADAM_LR * ((m2 * c1) / (jnp.sqrt(v2 * c2) + ADAM_EPS) + ADAM_WD * w_ref[...])

    spec = pl.BlockSpec((tr, C), lambda i: (i, 0))
    shp = jax.ShapeDtypeStruct((R, C), F32)
    return pl.pallas_call(body, name=name, out_shape=(shp, shp, shp), grid=(R // tr,), in_specs=[spec] * 4,
                          out_specs=(spec, spec, spec), compiler_params=_params(("parallel",)))(w, g, m, v)


def _adamw_halves(w, own, recv, m, v, core, name):
    L, rh, cw = own.shape
    tr = _pick(rh, (256, 128, 176, 64))
    c1 = 1.0 / (1.0 - ADAM_B1 ** ADAM_STEP)
    c2 = 1.0 / (1.0 - ADAM_B2 ** ADAM_STEP)

    def body(core_ref, w_ref, own_ref, recv_ref, m_ref, v_ref, g_ref, d_ref, mo_ref, vo_ref):
        gv = jnp.where(pl.program_id(1) == core_ref[0], own_ref[...], recv_ref[...])
        g_ref[...] = gv
        m2 = ADAM_B1 * m_ref[...] + (1.0 - ADAM_B1) * gv
        v2 = ADAM_B2 * v_ref[...] + (1.0 - ADAM_B2) * (gv * gv)
        mo_ref[...] = m2
        vo_ref[...] = v2
        d_ref[...] = -ADAM_LR * ((m2 * c1) / (jnp.sqrt(v2 * c2) + ADAM_EPS) + ADAM_WD * w_ref[...])

    full = pl.BlockSpec((None, None, tr, cw), lambda l, hf, i, core_ref: (l, hf, i, 0))
    mine = pl.BlockSpec((None, tr, cw), lambda l, hf, i, core_ref: (l, jnp.where(hf == core_ref[0], i, 0), 0))
    other = pl.BlockSpec((None, tr, cw), lambda l, hf, i, core_ref: (l, jnp.where(hf == core_ref[0], 0, i), 0))
    shp = jax.ShapeDtypeStruct((L, 2, rh, cw), F32)
    view = lambda a: a.reshape(L, 2, rh, cw)
    outs = pl.pallas_call(
        body, name=name, out_shape=(shp, shp, shp, shp),
        grid_spec=pltpu.PrefetchScalarGridSpec(
            num_scalar_prefetch=1, grid=(L, 2, rh // tr), in_specs=[full, mine, other, full, full],
            out_specs=(full, full, full, full)),
        compiler_params=_params(("arbitrary", "arbitrary", "arbitrary")),
    )(core, view(w), own, recv, view(m), view(v))
    return tuple(o.reshape(L, 2 * rh, cw) for o in outs)


ANY = pl.BlockSpec(memory_space=pl.ANY)


def _position():
    x, y, c = lax.axis_index("x"), lax.axis_index("y"), lax.axis_index("c")
    return x, y, c


def _allgather(ins, out_shapes, src_fns, dst_fns, name, in_vmem):
    n = len(ins)

    def body(*refs):
        in_refs, out_refs = refs[:n], refs[n:2 * n]
        send_sems, recv_sems, local_sems = refs[2 * n:]
        x, y, c = _position()
        me, sibling = (x, y, c), (x, y, 1 - c)
        chips = [(1 - x, y), (x, 1 - y), (1 - x, 1 - y)]

        def copy(k, j, block, to, own=False):
            dst = dst_fns[k](out_refs[k], *block)
            return pltpu.make_async_remote_copy(
                src_ref=src_fns[k](in_refs[k], c) if own else dst, dst_ref=dst,
                send_sem=send_sems.at[k, j], recv_sem=recv_sems.at[k, j], device_id=to, device_id_type=MESH)

        mine = [pltpu.make_async_copy(src_fns[k](in_refs[k], c), dst_fns[k](out_refs[k], *me), local_sems.at[k])
                for k in range(n)]
        for cp in mine:
            cp.start()
        first = []
        for k in range(n):
            first.append(copy(k, 0, me, sibling, own=True))
            first += [copy(k, 1 + j, me, (*chip, c), own=True) for j, chip in enumerate(chips)]
        for cp in first:
            cp.start()
        passed = []
        for j, chip in enumerate(chips):
            for k in range(n):
                copy(k, 1 + j, (*chip, c), me).wait_recv()
                fwd = copy(k, 4 + j, (*chip, c), sibling)
                fwd.start()
                passed.append(fwd)
        for k in range(n):
            copy(k, 0, sibling, me).wait_recv()
        for j, chip in enumerate(chips):
            for k in range(n):
                copy(k, 4 + j, (*chip, 1 - c), me).wait_recv()
        for cp in first + passed:
            cp.wait_send()
        for cp in mine:
            cp.wait()

    spec = pl.BlockSpec(memory_space=pltpu.VMEM) if in_vmem else ANY
    return pl.pallas_call(
        body, name=name, out_shape=tuple(out_shapes), in_specs=[spec] * n, out_specs=tuple([spec] * n),
        scratch_shapes=[pltpu.SemaphoreType.DMA((n, 7)), pltpu.SemaphoreType.DMA((n, 7)),
                        pltpu.SemaphoreType.DMA((n,))],
        compiler_params=pltpu.CompilerParams(vmem_limit_bytes=VMEM_LIMIT_BYTES),
    )(*ins)


def _allgather_small(payload, name):
    R = payload.shape[0]
    (out,) = _allgather(
        [payload], [jax.ShapeDtypeStruct((N_DEV, R, LANES), F32)],
        [lambda ref, c: ref], [lambda ref, px, py, pc: ref.at[4 * px + 2 * py + pc]], name, in_vmem=True)
    return out


def _swap_sibling(ins, name, other_half=False):
    n = len(ins)

    def body(*refs):
        in_refs, out_refs = refs[:n], refs[n:2 * n]
        send_sems, recv_sems = refs[2 * n:]
        x, y, c = _position()
        copies = [pltpu.make_async_remote_copy(
            src_ref=in_refs[k].at[:, 1 - c] if other_half else in_refs[k], dst_ref=out_refs[k],
            send_sem=send_sems.at[k], recv_sem=recv_sems.at[k],
            device_id=(x, y, 1 - c), device_id_type=MESH) for k in range(n)]
        for cp in copies:
            cp.start()
        for cp in copies:
            cp.wait_recv()
        for cp in copies:
            cp.wait_send()

    shape = lambda a: (a.shape[0],) + a.shape[2:] if other_half else a.shape
    return pl.pallas_call(
        body, name=name, out_shape=tuple(jax.ShapeDtypeStruct(shape(a), a.dtype) for a in ins),
        in_specs=[ANY] * n, out_specs=tuple([ANY] * n),
        scratch_shapes=[pltpu.SemaphoreType.DMA((n,)), pltpu.SemaphoreType.DMA((n,))],
        compiler_params=pltpu.CompilerParams(vmem_limit_bytes=VMEM_LIMIT_BYTES),
    )(*ins)


HBM_SPEC = pl.BlockSpec(memory_space=pltpu.HBM)
SEM_SPEC = pl.BlockSpec(memory_space=pltpu.SEMAPHORE)
SPLIT_PARAMS = pltpu.CompilerParams(has_side_effects=pltpu.SideEffectType.DATAFLOW_SIDE_EFFECTING)
TOKEN = jax.ShapeDtypeStruct((8, LANES), F32)


def _hbm(a):
    return pltpu.with_memory_space_constraint(a, pltpu.HBM)


def _weight_window(ref, col, r, cw, px, py, pc):
    rh = r // 2
    if col:
        return ref.at[pl.ds(pc * rh, rh), pl.ds((2 * px + py) * cw, cw)]
    return ref.at[pl.ds((2 * px + py) * r + pc * rh, rh), :]


def _peers(x, y, c):
    return [(x, y, 1 - c), (1 - x, y, c), (x, 1 - y, c), (1 - x, 1 - y, c)]


def _place_own(shard, col, pos, name):
    r, cw = shard.shape
    rh = r // 2
    tr = _pick(rh, (256, 128, 176, 64))
    nb = rh // tr
    shape = (r, N_CHIPS * cw) if col else (N_CHIPS * r, cw)

    def body(pos_ref, x_ref, o_ref):
        o_ref[...] = x_ref[...]

    if col:
        out_idx = lambda i, pos_ref: (pos_ref[1] * nb + i, pos_ref[0])
    else:
        out_idx = lambda i, pos_ref: (pos_ref[0] * (2 * nb) + pos_ref[1] * nb + i, 0)
    return pl.pallas_call(
        body, name=name, out_shape=jax.ShapeDtypeStruct(shape, shard.dtype),
        grid_spec=pltpu.PrefetchScalarGridSpec(
            num_scalar_prefetch=1, grid=(nb,),
            in_specs=[pl.BlockSpec((tr, cw), lambda i, pos_ref: (pos_ref[1] * nb + i, 0))],
            out_specs=pl.BlockSpec((tr, cw), out_idx)),
        compiler_params=_params(("arbitrary",)),
    )(pos, shard)


def _gather_start(shards, lands, cols, per_layer, name):
    n = len(shards)
    nl = n // per_layer

    def body(*refs):
        sh, ld = refs[:n], refs[n:2 * n]
        sems, token = refs[2 * n:2 * n + 2 * nl], refs[-1]
        x, y, c = _position()
        for k in range(n):
            l, a = divmod(k, per_layer)
            r, cw = shards[k].shape
            src = sh[k].at[pl.ds(c * (r // 2), r // 2), :]
            dst = _weight_window(ld[k], cols[k], r, cw, x, y, c)
            for j, peer in enumerate(_peers(x, y, c)):
                pltpu.make_async_remote_copy(src_ref=src, dst_ref=dst, send_sem=sems[2 * l].at[4 * a + j],
                                             recv_sem=sems[2 * l + 1].at[4 * a + j], device_id=peer,
                                             device_id_type=MESH).start()
        token[...] = jnp.zeros_like(token)

    arrs = list(shards) + list(lands)
    out = pl.pallas_call(
        body, name=name,
        out_shape=tuple(pltpu.SemaphoreType.DMA((per_layer * 4,)) for _ in range(2 * nl))
        + tuple(pltpu.HBM(a.shape, a.dtype) for a in arrs) + (TOKEN,),
        in_specs=[HBM_SPEC] * (2 * n),
        out_specs=(SEM_SPEC,) * (2 * nl) + (HBM_SPEC,) * (2 * n) + (pl.BlockSpec(memory_space=pltpu.VMEM),),
        input_output_aliases={i: 2 * nl + i for i in range(2 * n)}, compiler_params=SPLIT_PARAMS,
    )(*[_hbm(a) for a in arrs])
    return out[:2 * nl], out[2 * nl:2 * nl + n], out[2 * nl + n:2 * nl + 2 * n], out[-1]


def _gather_wait(shards, lands, send, recv, after, cols, first, name):
    m = len(shards)

    def body(*refs):
        sh, ld = refs[:m], refs[m:2 * m]
        send_ref, recv_ref = refs[2 * m], refs[2 * m + 1]
        x, y, c = _position()
        for a in range(m):
            r, cw = shards[a].shape
            src = sh[a].at[pl.ds(c * (r // 2), r // 2), :]
            for j, (px, py, pc) in enumerate(_peers(x, y, c)):
                cp = pltpu.make_async_remote_copy(
                    src_ref=src, dst_ref=_weight_window(ld[a], cols[a], r, cw, px, py, pc),
                    send_sem=send_ref.at[4 * (first + a) + j], recv_sem=recv_ref.at[4 * (first + a) + j],
                    device_id=(px, py, pc),
                    device_id_type=MESH)
                cp.wait_send()
                cp.wait_recv()

    arrs = list(shards) + list(lands)
    out = pl.pallas_call(
        body, name=name, out_shape=tuple(pltpu.HBM(a.shape, a.dtype) for a in arrs),
        in_specs=[HBM_SPEC] * (2 * m) + [SEM_SPEC, SEM_SPEC, ANY], out_specs=(HBM_SPEC,) * (2 * m),
        input_output_aliases={i: i for i in range(2 * m)}, compiler_params=SPLIT_PARAMS,
    )(*arrs, send, recv, after)
    return out[m:]


def _forward_sibling(lands, cols, shard_shapes, name):
    m = len(lands)

    def body(*refs):
        ins, outs = refs[:m], refs[m:2 * m]
        send_sems, recv_sems = refs[2 * m:]
        x, y, c = _position()
        chips = [(1 - x, y), (x, 1 - y), (1 - x, 1 - y)]
        sends = []
        for a in range(m):
            r, cw = shard_shapes[a]
            for j, (px, py) in enumerate(chips):
                cp = pltpu.make_async_remote_copy(
                    src_ref=_weight_window(ins[a], cols[a], r, cw, px, py, c),
                    dst_ref=_weight_window(outs[a], cols[a], r, cw, px, py, c),
                    send_sem=send_sems.at[a, j], recv_sem=recv_sems.at[a, j], device_id=(x, y, 1 - c),
                    device_id_type=MESH)
                cp.start()
                sends.append(cp)
        for a in range(m):
            r, cw = shard_shapes[a]
            for j, (px, py) in enumerate(chips):
                pltpu.make_async_remote_copy(
                    src_ref=_weight_window(ins[a], cols[a], r, cw, px, py, c),
                    dst_ref=_weight_window(outs[a], cols[a], r, cw, px, py, 1 - c),
                    send_sem=send_sems.at[a, j], recv_sem=recv_sems.at[a, j], device_id=(x, y, 1 - c),
                    device_id_type=MESH).wait_recv()
        for cp in sends:
            cp.wait_send()

    return pl.pallas_call(
        body, name=name, out_shape=tuple(jax.ShapeDtypeStruct(a.shape, a.dtype) for a in lands),
        in_specs=[ANY] * m, out_specs=tuple([ANY] * m), input_output_aliases={i: i for i in range(m)},
        scratch_shapes=[pltpu.SemaphoreType.DMA((m, 3)), pltpu.SemaphoreType.DMA((m, 3))],
        compiler_params=pltpu.CompilerParams(vmem_limit_bytes=VMEM_LIMIT_BYTES),
    )(*lands)


def _forward_start(lands, cols, shard_shapes, name):
    m = len(lands)

    def body(*refs):
        ld = refs[:m]
        send_ref, recv_ref, token = refs[m], refs[m + 1], refs[-1]
        x, y, c = _position()
        for a in range(m):
            r, cw = shard_shapes[a]
            for j, (px, py) in enumerate([(1 - x, y), (x, 1 - y), (1 - x, 1 - y)]):
                win = _weight_window(ld[a], cols[a], r, cw, px, py, c)
                pltpu.make_async_remote_copy(src_ref=win, dst_ref=win, send_sem=send_ref.at[3 * a + j],
                                             recv_sem=recv_ref.at[3 * a + j], device_id=(x, y, 1 - c),
                                             device_id_type=MESH).start()
        token[...] = jnp.zeros_like(token)

    out = pl.pallas_call(
        body, name=name,
        out_shape=(pltpu.SemaphoreType.DMA((m * 3,)), pltpu.SemaphoreType.DMA((m * 3,)))
        + tuple(pltpu.HBM(a.shape, a.dtype) for a in lands) + (TOKEN,),
        in_specs=[HBM_SPEC] * m,
        out_specs=(SEM_SPEC, SEM_SPEC) + (HBM_SPEC,) * m + (pl.BlockSpec(memory_space=pltpu.VMEM),),
        input_output_aliases={i: 2 + i for i in range(m)}, compiler_params=SPLIT_PARAMS,
    )(*[_hbm(a) for a in lands])
    return out[0], out[1], out[2:2 + m], out[-1]


def _forward_wait(lands, send, recv, after, cols, shard_shapes, name):
    m = len(lands)

    def body(*refs):
        ld = refs[:m]
        send_ref, recv_ref = refs[m], refs[m + 1]
        x, y, c = _position()
        for a in range(m):
            r, cw = shard_shapes[a]
            for j, (px, py) in enumerate([(1 - x, y), (x, 1 - y), (1 - x, 1 - y)]):
                cp = pltpu.make_async_remote_copy(
                    src_ref=_weight_window(ld[a], cols[a], r, cw, px, py, c),
                    dst_ref=_weight_window(ld[a], cols[a], r, cw, px, py, 1 - c),
                    send_sem=send_ref.at[3 * a + j], recv_sem=recv_ref.at[3 * a + j], device_id=(x, y, 1 - c),
                    device_id_type=MESH)
                cp.wait_send()
                cp.wait_recv()

    return pl.pallas_call(
        body, name=name, out_shape=tuple(pltpu.HBM(a.shape, a.dtype) for a in lands),
        in_specs=[HBM_SPEC] * m + [SEM_SPEC, SEM_SPEC, ANY], out_specs=(HBM_SPEC,) * m,
        input_output_aliases={i: i for i in range(m)}, compiler_params=SPLIT_PARAMS,
    )(*lands, send, recv, after)


def _exchange_start(sums, name):
    m = len(sums)
    lands = [lax.empty((3,) + s.shape[1:], s.dtype) for s in sums]

    def body(*refs):
        cs, ld = refs[:m], refs[m:2 * m]
        send_ref, recv_ref, token = refs[2 * m], refs[2 * m + 1], refs[-1]
        x, y, c = _position()
        for a in range(m):
            for j, (px, py) in enumerate([(1 - x, y), (x, 1 - y), (1 - x, 1 - y)]):
                pltpu.make_async_remote_copy(
                    src_ref=cs[a].at[2 * px + py], dst_ref=ld[a].at[j], send_sem=send_ref.at[3 * a + j],
                    recv_sem=recv_ref.at[3 * a + j], device_id=(px, py, c), device_id_type=MESH).start()
        token[...] = jnp.zeros_like(token)

    arrs = list(sums) + lands
    out = pl.pallas_call(
        body, name=name,
        out_shape=(pltpu.SemaphoreType.DMA((m * 3,)), pltpu.SemaphoreType.DMA((m * 3,)))
        + tuple(pltpu.HBM(a.shape, a.dtype) for a in arrs) + (TOKEN,),
        in_specs=[HBM_SPEC] * (2 * m),
        out_specs=(SEM_SPEC, SEM_SPEC) + (HBM_SPEC,) * (2 * m) + (pl.BlockSpec(memory_space=pltpu.VMEM),),
        input_output_aliases={i: 2 + i for i in range(2 * m)}, compiler_params=SPLIT_PARAMS,
    )(*[_hbm(a) for a in arrs])
    return out[0], out[1], out[2:2 + m], out[2 + m:2 + 2 * m], out[-1]


def _exchange_wait(sums, lands, send, recv, after, name):
    m = len(sums)

    def body(*refs):
        cs, ld = refs[:m], refs[m:2 * m]
        send_ref, recv_ref = refs[2 * m], refs[2 * m + 1]
        x, y, c = _position()
        for a in range(m):
            for j, (px, py) in enumerate([(1 - x, y), (x, 1 - y), (1 - x, 1 - y)]):
                cp = pltpu.make_async_remote_copy(
                    src_ref=cs[a].at[2 * px + py], dst_ref=ld[a].at[j], send_sem=send_ref.at[3 * a + j],
                    recv_sem=recv_ref.at[3 * a + j], device_id=(px, py, c), device_id_type=MESH)
                cp.wait_send()
                cp.wait_recv()

    arrs = list(sums) + list(lands)
    out = pl.pallas_call(
        body, name=name, out_shape=tuple(pltpu.HBM(a.shape, a.dtype) for a in arrs),
        in_specs=[HBM_SPEC] * (2 * m) + [SEM_SPEC, SEM_SPEC, ANY], out_specs=(HBM_SPEC,) * (2 * m),
        input_output_aliases={i: i for i in range(2 * m)}, compiler_params=SPLIT_PARAMS,
    )(*arrs, send, recv, after)
    return out[:m], out[m:]


def _place_row(payload, dev, name):
    R = payload.shape[0]
    tr = _pick(R, (512, 448, 384, 256, 192, 128, 64, 32, 16, 8))

    def body(dev_ref, x_ref, o_ref):
        o_ref[...] = x_ref[...]

    return pl.pallas_call(
        body, name=name, out_shape=jax.ShapeDtypeStruct((N_DEV, R, LANES), payload.dtype),
        grid_spec=pltpu.PrefetchScalarGridSpec(
            num_scalar_prefetch=1, grid=(R // tr,),
            in_specs=[pl.BlockSpec((tr, LANES), lambda i, dev_ref: (i, 0))],
            out_specs=pl.BlockSpec((None, tr, LANES), lambda i, dev_ref: (dev_ref[0], i, 0))),
        compiler_params=_params(("arbitrary",)),
    )(dev, payload)


def _others(x, y, c):
    return [(1 - x if fx else x, 1 - y if fy else y, 1 - c if fc else c)
            for fx in (0, 1) for fy in (0, 1) for fc in (0, 1) if fx or fy or fc]


def _broadcast_start(payload, land, name):
    def body(p_ref, l_ref, send_ref, recv_ref, p_thru, l_thru, token):
        x, y, c = _position()
        for j, peer in enumerate(_others(x, y, c)):
            pltpu.make_async_remote_copy(src_ref=p_ref, dst_ref=l_ref.at[4 * x + 2 * y + c], send_sem=send_ref.at[j],
                                         recv_sem=recv_ref.at[j], device_id=peer, device_id_type=MESH).start()
        token[...] = jnp.zeros_like(token)

    n = N_DEV - 1
    return pl.pallas_call(
        body, name=name,
        out_shape=(pltpu.SemaphoreType.DMA((n,)), pltpu.SemaphoreType.DMA((n,)), pltpu.HBM(payload.shape, payload.dtype),
                   pltpu.HBM(land.shape, land.dtype), TOKEN),
        in_specs=[HBM_SPEC, HBM_SPEC],
        out_specs=(SEM_SPEC, SEM_SPEC, HBM_SPEC, HBM_SPEC, pl.BlockSpec(memory_space=pltpu.VMEM)),
        input_output_aliases={0: 2, 1: 3}, compiler_params=SPLIT_PARAMS,
    )(_hbm(payload), _hbm(land))


def _broadcast_wait(payload, land, send, recv, after, name):
    def body(p_ref, l_ref, send_ref, recv_ref, after_ref, p_thru, l_thru):
        x, y, c = _position()
        for j, (px, py, pc) in enumerate(_others(x, y, c)):
            cp = pltpu.make_async_remote_copy(src_ref=p_ref, dst_ref=l_ref.at[4 * px + 2 * py + pc],
                                              send_sem=send_ref.at[j], recv_sem=recv_ref.at[j],
                                              device_id=(px, py, pc), device_id_type=MESH)
            cp.wait_send()
            cp.wait_recv()

    out = pl.pallas_call(
        body, name=name, out_shape=(pltpu.HBM(payload.shape, payload.dtype), pltpu.HBM(land.shape, land.dtype)),
        in_specs=[HBM_SPEC, HBM_SPEC, SEM_SPEC, SEM_SPEC, ANY], out_specs=(HBM_SPEC, HBM_SPEC),
        input_output_aliases={0: 0, 1: 1}, compiler_params=SPLIT_PARAMS,
    )(payload, land, send, recv, after)
    return out[1]


def _vec(a):
    return a.reshape(1, -1)


def _local_step(x, tgt, mod, W, P, get_w=None, on_grads=None):
    Dm = D_MODEL
    G = {k: [] for k in ("gm_w_in", "gm_w_out", "hg_w_in", "hg_w_out", "ffn_w_up", "ffn_w_down")}
    lb_all = _lb_fwd(P["hg_lb"], "lb_fwd")
    saved = []
    xs = x
    y_prev = gate_prev = None
    layer_w = [None] * DEPTH

    def wmm(xa, kind, i, mode, name):
        if layer_w[i] is not None:
            return _mm(xa, layer_w[i][kind], mode, name)
        return _mm(xa, W[kind], mode, name, b_layer=i if kind.startswith("ffn") else i // 2)

    for i in range(DEPTH):
        m = [_vec(mod[i, j * Dm:(j + 1) * Dm]) for j in range(6)]
        sh1, sc1, g1, sh2, sc2, g2 = m
        j = i // 2
        if get_w is not None:
            layer_w[i] = get_w(i, 0, xs if y_prev is None else y_prev)
        xs, h = _norm_fwd(xs, y_prev, gate_prev, _vec(P["norm_g"][i, 0]), sc1, sh1, f"norm_fwd_a{i}")
        rec = dict(x1=xs, h1=h)
        if i % 2 == 0:
            a = wmm(h, "gm_w_in", i, "nn", f"gm_in{i}")
            gated = _gm_mid_fwd(a, _vec(P["gm_ln_g"][j]), _vec(P["gm_ln_b"][j]), P["gm_w_s"][j],
                                P["gm_b_s"][j].reshape(GM_HEADS, GM_BLOCK, 1), f"gm_mid_fwd{i}")
            if get_w is not None:
                layer_w[i].update(get_w(i, 1, gated))
            y1 = wmm(gated, "gm_w_out", i, "nn", f"gm_out{i}")
            rec.update(a=a, act=gated)
        else:
            p = wmm(h, "hg_w_in", i, "nn", f"hg_in{i}")
            o, og, states = _hg_scan_fwd(p, _vec(lb_all[j]), _vec(P["hg_gn_g"][j]), f"hg_scan_fwd{i}")
            if get_w is not None:
                layer_w[i].update(get_w(i, 1, og))
            y1 = wmm(og, "hg_w_out", i, "nn", f"hg_out{i}")
            rec.update(a=p, act=og, o=o, states=states)
        rec["y1"] = y1
        xs, h2 = _norm_fwd(xs, y1, g1, _vec(P["norm_g"][i, 1]), sc2, sh2, f"norm_fwd_b{i}")
        a2 = wmm(h2, "ffn_w_up", i, "nn", f"ffn_up{i}")
        mm_ = _conv_fwd(a2, P["ffn_conv_w"][i], _vec(P["ffn_conv_b"][i]), f"conv_fwd{i}")
        y2 = wmm(mm_, "ffn_w_down", i, "nn", f"ffn_down{i}")
        rec.update(x2=xs, h2=h2, a2=a2, m=mm_, y2=y2, mods=m)
        saved.append(rec)
        y_prev, gate_prev = y2, g2
    dx, dy, loss, s_fg, s_gate = _loss_head(xs, y_prev, gate_prev, _vec(P["final_g"]), tgt, "loss_head")
    small = dict(final_g=s_fg, norm_g=[None] * DEPTH, dmod=[None] * DEPTH, ffn_conv_w=[None] * DEPTH,
                 ffn_conv_b=[None] * DEPTH, gm_ln_g=[None] * 2, gm_ln_b=[None] * 2, gm_w_s=[None] * 2,
                 gm_b_s=[None] * 2, hg_gn_g=[None] * 2, dlb=[None] * 2)
    for i in reversed(range(DEPTH)):
        rec = saved[i]
        sh1, sc1, g1, sh2, sc2, g2 = rec["mods"]
        j = i // 2
        d_g2 = s_gate
        dm = wmm(dy, "ffn_w_down", i, "nt", f"ffn_down_dx{i}")
        G["ffn_w_down"].append(_mm(rec["m"], dy, "tn", f"ffn_down_dw{i}", out_dtype=GRAD_WIRE))
        da2, dcw, dcb = _conv_bwd(rec["a2"], dm, P["ffn_conv_w"][i], _vec(P["ffn_conv_b"][i]), f"conv_bwd{i}")
        small["ffn_conv_w"][i], small["ffn_conv_b"][i] = dcw, dcb
        dh2 = wmm(da2, "ffn_w_up", i, "nt", f"ffn_up_dx{i}")
        G["ffn_w_up"].append(_mm(rec["h2"], da2, "tn", f"ffn_up_dw{i}", out_dtype=GRAD_WIRE, exchange_out=True))
        ng2 = _vec(P["norm_g"][i, 1])
        if on_grads is not None:
            ng2 = ng2 + on_grads(i, {k: G[k][-1] for k in ("ffn_w_up", "ffn_w_down")})
        dx, dy, s_sh2, s_x2, d_g1 = _norm_bwd(rec["x2"], dh2, dx, ng2, sc2, rec["y1"], g1, f"norm_bwd_b{i}")
        d_sc2, d_ng2 = s_x2 * ng2, s_x2 * (1.0 + sc2)
        if i % 2 == 0:
            dgated = wmm(dy, "gm_w_out", i, "nt", f"gm_out_dx{i}")
            G["gm_w_out"].append(_mm(rec["act"], dy, "tn", f"gm_out_dw{i}", out_dtype=GRAD_WIRE))
            da, dws, dbs, dlg, dlbeta = _gm_mid_bwd(
                rec["a"], dgated, _vec(P["gm_ln_g"][j]), _vec(P["gm_ln_b"][j]), P["gm_w_s"][j],
                P["gm_b_s"][j].reshape(GM_HEADS, GM_BLOCK, 1), f"gm_mid_bwd{i}")
            small["gm_w_s"][j], small["gm_b_s"][j] = dws, dbs[:, :GM_HEADS].T
            small["gm_ln_g"][j], small["gm_ln_b"][j] = dlg, dlbeta
            dh1 = wmm(da, "gm_w_in", i, "nt", f"gm_in_dx{i}")
            G["gm_w_in"].append(_mm(rec["h1"], da, "tn", f"gm_in_dw{i}", out_dtype=GRAD_WIRE, exchange_out=True))
        else:
            dog = wmm(dy, "hg_w_out", i, "nt", f"hg_out_dx{i}")
            G["hg_w_out"].append(_mm(rec["act"], dy, "tn", f"hg_out_dw{i}", out_dtype=GRAD_WIRE))
            dp, dlb, dgn = _hg_scan_bwd(rec["a"], _vec(lb_all[j]), _vec(P["hg_gn_g"][j]), rec["o"], dog,
                                        rec["states"], f"hg_scan_bwd{i}")
            small["dlb"][j], small["hg_gn_g"][j] = dlb, dgn
            dh1 = wmm(dp, "hg_w_in", i, "nt", f"hg_in_dx{i}")
            G["hg_w_in"].append(_mm(rec["h1"], dp, "tn", f"hg_in_dw{i}", out_dtype=GRAD_WIRE, exchange_out=True))
        ng1 = _vec(P["norm_g"][i, 0])
        if on_grads is not None:
            mixer = ("gm_w_in", "gm_w_out") if i % 2 == 0 else ("hg_w_in", "hg_w_out")
            ng1 = ng1 + on_grads(i, {k: G[k][-1] for k in mixer})
        if i > 0:
            prev = saved[i - 1]
            dx, dy, s_sh1, s_x1, s_gate = _norm_bwd(rec["x1"], dh1, dx, ng1, sc1, prev["y2"], prev["mods"][5],
                                                    f"norm_bwd_a{i}")
        else:
            dx, s_sh1, s_x1 = _norm_bwd(rec["x1"], dh1, dx, ng1, sc1, None, None, f"norm_bwd_a{i}")
        d_sc1, d_ng1 = s_x1 * ng1, s_x1 * (1.0 + sc1)
        small["norm_g"][i] = jnp.concatenate([d_ng1, d_ng2], axis=0)
        small["dmod"][i] = jnp.concatenate([s_sh1, d_sc1, d_g1, s_sh2, d_sc2, d_g2], axis=1)
    for k in G:
        G[k] = G[k][::-1]
    dlb_all = jnp.concatenate(small.pop("dlb"), axis=0)
    small["hg_lb"] = _lb_bwd(P["hg_lb"], dlb_all, "lb_bwd")
    return loss, dx, G, small


BIG = ("gm_w_in", "gm_w_out", "hg_w_in", "hg_w_out", "ffn_w_up", "ffn_w_down")
COL_SHARDED = dict(gm_w_in=True, gm_w_out=False, hg_w_in=True, hg_w_out=False, ffn_w_up=True, ffn_w_down=False)
LAYER_WEIGHTS = 4


def _layer_kinds(i):
    return (("gm_w_in", "gm_w_out") if i % 2 == 0 else ("hg_w_in", "hg_w_out")) + ("ffn_w_up", "ffn_w_down")


def _pack(pieces):
    flat = [p.reshape(-1).astype(F32) for p in pieces]
    offs, tot = [], 0
    for f in flat:
        offs.append((tot, f.shape[0]))
        tot += f.shape[0]
    padded = -(-tot // (8 * LANES)) * (8 * LANES)
    if padded > tot:
        flat.append(jnp.zeros((padded - tot,), F32))
    return jnp.concatenate(flat).reshape(-1, LANES), offs


def _unpack(rows, offs, shapes):
    lead = rows.shape[:-2]
    flat = rows.reshape(lead + (-1,))
    return [flat[..., o:o + n].reshape(lead + tuple(s)) for (o, n), s in zip(offs, shapes)]


def _from_chips(per_dev, axis):
    per_chip = per_dev[0::2]
    return jnp.concatenate([per_chip[s] for s in range(N_CHIPS)], axis=axis)


def kernel(x, c, gm_w_in, gm_ln_g, gm_ln_b, gm_w_s, gm_b_s, gm_w_out, hg_w_in, hg_lb, hg_gn_g, hg_w_out, ffn_w_up, ffn_conv_w, ffn_conv_b, ffn_w_down, norm_g, ada_w, ada_b, final_g, loss_target, m_gm_w_in, m_gm_ln_g, m_gm_ln_b, m_gm_w_s, m_gm_b_s, m_gm_w_out, m_hg_w_in, m_hg_lb, m_hg_gn_g, m_hg_w_out, m_ffn_w_up, m_ffn_conv_w, m_ffn_conv_b, m_ffn_w_down, m_norm_g, m_ada_w, m_ada_b, m_final_g, v_gm_w_in, v_gm_ln_g, v_gm_ln_b, v_gm_w_s, v_gm_b_s, v_gm_w_out, v_hg_w_in, v_hg_lb, v_hg_gn_g, v_hg_w_out, v_ffn_w_up, v_ffn_conv_w, v_ffn_conv_b, v_ffn_w_down, v_norm_g, v_ada_w, v_ada_b, v_final_g):
    Dm = D_MODEL
    xi, yi, ci = _position()
    chip = 2 * xi + yi
    dev = 4 * xi + 2 * yi + ci
    weights = dict(gm_w_in=gm_w_in, gm_ln_g=gm_ln_g, gm_ln_b=gm_ln_b, gm_w_s=gm_w_s, gm_b_s=gm_b_s,
                   gm_w_out=gm_w_out, hg_w_in=hg_w_in, hg_lb=hg_lb, hg_gn_g=hg_gn_g, hg_w_out=hg_w_out,
                   ffn_w_up=ffn_w_up, ffn_conv_w=ffn_conv_w, ffn_conv_b=ffn_conv_b, ffn_w_down=ffn_w_down,
                   norm_g=norm_g, ada_w=ada_w, ada_b=ada_b, final_g=final_g)
    mom_m = dict(gm_w_in=m_gm_w_in, gm_ln_g=m_gm_ln_g, gm_ln_b=m_gm_ln_b, gm_w_s=m_gm_w_s, gm_b_s=m_gm_b_s,
                 gm_w_out=m_gm_w_out, hg_w_in=m_hg_w_in, hg_lb=m_hg_lb, hg_gn_g=m_hg_gn_g, hg_w_out=m_hg_w_out,
                 ffn_w_up=m_ffn_w_up, ffn_conv_w=m_ffn_conv_w, ffn_conv_b=m_ffn_conv_b, ffn_w_down=m_ffn_w_down,
                 norm_g=m_norm_g, ada_w=m_ada_w, ada_b=m_ada_b, final_g=m_final_g)
    mom_v = dict(gm_w_in=v_gm_w_in, gm_ln_g=v_gm_ln_g, gm_ln_b=v_gm_ln_b, gm_w_s=v_gm_w_s, gm_b_s=v_gm_b_s,
                 gm_w_out=v_gm_w_out, hg_w_in=v_hg_w_in, hg_lb=v_hg_lb, hg_gn_g=v_hg_gn_g, hg_w_out=v_hg_w_out,
                 ffn_w_up=v_ffn_w_up, ffn_conv_w=v_ffn_conv_w, ffn_conv_b=v_ffn_conv_b, ffn_w_down=v_ffn_w_down,
                 norm_g=v_norm_g, ada_w=v_ada_w, ada_b=v_ada_b, final_g=v_final_g)
    order = list(weights)

    pos = jnp.stack([chip, ci]).astype(jnp.int32)
    shards, by_col = [], []
    for i in range(DEPTH):
        for k in _layer_kinds(i):
            shards.append(weights[k][i if k.startswith("ffn") else i // 2].astype(BF16))
            by_col.append(COL_SHARDED[k])
    placed = [_place_own(sh, col, pos, f"place_own{n}") for n, (sh, col) in enumerate(zip(shards, by_col))]
    gsems, sh_thru, ld_thru, _ = _gather_start(shards, placed, by_col, LAYER_WEIGHTS, "gather_start")

    pieces = [c, hg_lb, hg_gn_g, norm_g, ffn_conv_w]
    payload, offs = _pack(pieces)
    got = _allgather_small(payload, "gather_small")
    c_g, lb_g, gn_g, ng_g, cw_g = _unpack(got, offs, [p.shape for p in pieces])
    c_all = c_g.reshape(N_DEV, Dm)
    P = dict(hg_lb=_from_chips(lb_g, 1), hg_gn_g=_from_chips(gn_g, 1), norm_g=_from_chips(ng_g, 2),
             ffn_conv_w=_from_chips(cw_g, 2), gm_ln_g=gm_ln_g, gm_ln_b=gm_ln_b, gm_w_s=gm_w_s, gm_b_s=gm_b_s,
             ffn_conv_b=ffn_conv_b, final_g=final_g)

    cols = ada_w.shape[2]
    ada_b_sh = lax.dynamic_slice_in_dim(ada_b, chip * cols, cols, axis=1)
    mod_sh = _ada_fwd(c_all, ada_w, ada_b_sh, "ada_fwd")
    mod_g = _allgather_small(mod_sh.reshape(-1, LANES), "gather_mod").reshape(N_DEV, DEPTH, N_DEV, cols)
    mod_mine = lax.dynamic_index_in_dim(mod_g[0::2], dev, axis=2, keepdims=False)
    mod = jnp.transpose(mod_mine, (1, 0, 2)).reshape(DEPTH, N_CHIPS * cols)

    core = jnp.reshape(ci, (1,)).astype(jnp.int32)
    chip_arr = jnp.reshape(chip, (1,)).astype(jnp.int32)
    pending, held, prefetched = [], {}, {}

    def get_w(i, group, after):
        lo, hi = LAYER_WEIGHTS * i, LAYER_WEIGHTS * (i + 1)
        shapes = lambda s: [a.shape for a in shards[s]]
        out = {}
        if i == 0:
            s = slice(lo, lo + 1) if group == 0 else slice(lo + 1, hi)
            landed = _gather_wait(sh_thru[s], ld_thru[s], gsems[0], gsems[1], after, by_col[s], s.start - lo,
                                  f"gather_wait0_{group}")
            full = _forward_sibling(landed, by_col[s], shapes(s), f"gather_forward0_{group}")
            out = dict(zip(_layer_kinds(0)[s.start - lo:s.stop - lo], full))
        elif group == 0:
            s = slice(lo, hi)
            send, recv, lands = prefetched.pop(i)
            full = _forward_wait(lands, send, recv, after, by_col[s], shapes(s), f"gather_forward_wait{i}")
            out = dict(zip(_layer_kinds(i), full))
        if group == 1 and i + 1 < DEPTH:
            s = slice(hi, hi + LAYER_WEIGHTS)
            landed = _gather_wait(sh_thru[s], ld_thru[s], gsems[2 * i + 2], gsems[2 * i + 3], after, by_col[s], 0,
                                  f"gather_wait{i + 1}")
            send, recv, lands, _ = _forward_start(landed, by_col[s], shapes(s), f"gather_forward_start{i + 1}")
            prefetched[i + 1] = (send, recv, lands)
        return out

    def on_grads(i, gdict):
        if i > 0 and "ffn_w_up" in gdict:
            held[i] = gdict
            return 0.0
        gdict = {**held.pop(i, {}), **gdict}
        kinds = [k for k in _layer_kinds(i) if k in gdict]
        tag = f"{i}_ffn" if kinds[0] == "ffn_w_up" else f"{i}"
        g4 = []
        for k in kinds:
            g = gdict[k]
            if not COL_SHARDED[k]:
                R, C = g.shape
                g = g.reshape(N_CHIPS, 2, R // (2 * N_CHIPS), C)
            g4.append(g)
        from_sib = _swap_sibling(g4, f"reduce_swap{tag}", other_half=True)
        sums = [_add_own_half(g, r, core, f"chip_sum_{k}{i}") for g, r, k in zip(g4, from_sib, kinds)]
        send, recv, sums_thru, lands, token = _exchange_start(sums, f"reduce_start{tag}")
        pending.append((tag, i, kinds, send, recv, sums_thru, lands))
        return token[0, 0]

    loss_part, dx, G, small = _local_step(x[0], loss_target[0], mod, None, P, get_w, on_grads)

    sum_pieces = [loss_part[:, :1], small["final_g"], jnp.stack(small["gm_ln_g"]), jnp.stack(small["gm_ln_b"]),
                  jnp.stack(small["gm_w_s"]), jnp.stack(small["gm_b_s"]), jnp.stack(small["ffn_conv_b"]),
                  small["hg_lb"], jnp.stack(small["hg_gn_g"]), jnp.stack(small["norm_g"]),
                  jnp.stack(small["ffn_conv_w"])]
    dmod = jnp.concatenate(small["dmod"], axis=0)
    payload2, offs2 = _pack(sum_pieces + [dmod])
    placed2 = _place_row(payload2, jnp.reshape(dev, (1,)).astype(jnp.int32), "place_grads")
    bsend, brecv, p2_thru, l2_thru, small_token = _broadcast_start(payload2, placed2, "gather_grads_start")

    landed = {}
    for tag, i, kinds, send, recv, sums_thru, lands in pending:
        sums_i, lands_i = _exchange_wait(sums_thru, lands, send, recv, small_token, f"reduce_wait{tag}")
        for k, s_, l_ in zip(kinds, sums_i, lands_i):
            landed[(k, i)] = (l_, s_)
    own_halves = []
    for k in BIG:
        layers = [landed[(k, i)] for i in range(DEPTH) if (k, i) in landed]
        own_halves.append(_sum_chips([l_ for l_, _ in layers], [s_ for _, s_ in layers], chip_arr, f"sum_chips_{k}"))
    sib_halves = _swap_sibling(own_halves, "reduce_join")
    grads, deltas, new_m, new_v = {}, {}, {}, {}
    for k, own, recv in zip(BIG, own_halves, sib_halves):
        grads[k], deltas[k], new_m[k], new_v[k] = _adamw_halves(
            weights[k], own, recv, mom_m[k], mom_v[k], core, f"adamw_{k}")

    got2 = _broadcast_wait(p2_thru, l2_thru, bsend, brecv, new_v[BIG[-1]], "gather_grads_wait")
    dmod_all = _unpack(got2, offs2[-1:], [dmod.shape])[0]
    summed = _sum_devices(got2, "sum_devices")
    (loss_s, d_final_g, d_ln_g, d_ln_b, d_ws, d_bs, d_cb, d_lb, d_gn, d_ng, d_cw) = _unpack(
        summed, offs2[:-1], [(1,), final_g.shape, gm_ln_g.shape, gm_ln_b.shape, gm_w_s.shape, gm_b_s.shape,
                             ffn_conv_b.shape, (2, Dm), (2, Dm), (DEPTH, 2, Dm), (DEPTH, 3, 2 * FFN_HIDDEN)])
    grads.update(final_g=d_final_g, gm_ln_g=d_ln_g, gm_ln_b=d_ln_b, gm_w_s=d_ws, gm_b_s=d_bs, ffn_conv_b=d_cb)
    grads["hg_lb"] = lax.dynamic_slice_in_dim(d_lb, chip * hg_lb.shape[1], hg_lb.shape[1], axis=1)
    grads["hg_gn_g"] = lax.dynamic_slice_in_dim(d_gn, chip * hg_gn_g.shape[1], hg_gn_g.shape[1], axis=1)
    grads["norm_g"] = lax.dynamic_slice_in_dim(d_ng, chip * norm_g.shape[2], norm_g.shape[2], axis=2)
    grads["ffn_conv_w"] = lax.dynamic_slice_in_dim(d_cw, chip * ffn_conv_w.shape[2], ffn_conv_w.shape[2], axis=2)
    dmod_sh = lax.dynamic_slice_in_dim(dmod_all, chip * cols, cols, axis=2)
    grads["ada_w"] = _ada_bwd(c_all, jnp.transpose(dmod_sh, (1, 0, 2)), "ada_bwd")
    grads["ada_b"] = _sum_devices(dmod_all.reshape(N_DEV, -1, LANES), "sum_ada_b").reshape(ada_b.shape)

    for k in order:
        if k in BIG:
            continue
        w = weights[k]
        shp = w.shape
        view = (-1, shp[-1]) if w.ndim > 1 else (8, -1)
        d, m2, v2 = _adamw(w.reshape(view), grads[k].reshape(view), mom_m[k].reshape(view), mom_v[k].reshape(view),
                           f"adamw_{k}")
        deltas[k], new_m[k], new_v[k] = d.reshape(shp), m2.reshape(shp), v2.reshape(shp)
        grads[k] = grads[k].reshape(shp)

    loss = loss_s.reshape(())
    return (loss, dx[None], *[grads[k] for k in order], *[deltas[k] for k in order],
            *[new_m[k] for k in order], *[new_v[k] for k in order])
```

```python
import functools

import jax
import jax.numpy as jnp
from jax import lax
from jax.experimental import pallas as pl
from jax.experimental.pallas import tpu as pltpu

F32 = jnp.float32
BF16 = jnp.bfloat16
HI = lax.Precision.HIGHEST
X3 = lax.Precision.HIGH
GRAD_WIRE = BF16
MESH = pl.DeviceIdType.MESH

D_MODEL = 1024
DEPTH = 4
EPS = 1e-6
GM_WIDTH = 2048
GM_HEADS = 8
GM_BLOCK = 128
GM_HEAD_DIM = 256
CHUNK = 64
HG_HEADS = 8
HG_DIM = 128
FFN_HIDDEN = 2816
N_CHIPS = 4
N_DEV = 8

ADAM_LR = 0.001
ADAM_B1 = 0.9
ADAM_B2 = 0.999
ADAM_EPS = 1e-08
ADAM_WD = 0.01
ADAM_STEP = 10

VMEM_LIMIT_BYTES = 56 * 1024 * 1024
ROW_TILE = 256
LANES = 128

_SQRT_HALF = 0.7071067811865476
_INV_SQRT_2PI = 0.3989422804014327


def _pick(dim, prefs):
    for p in prefs:
        if dim % p == 0:
            return p
    return dim


def _params(sem):
    return pltpu.CompilerParams(dimension_semantics=sem, vmem_limit_bytes=VMEM_LIMIT_BYTES)


def _cdf(x):
    return 0.5 * (1.0 + lax.erf(x * _SQRT_HALF))


def _pdf(x):
    return jnp.exp(-0.5 * x * x) * _INV_SQRT_2PI


def _sig(x):
    return jax.nn.sigmoid(x)


def _dot(a, b, dims, prec=None):
    return lax.dot_general(a, b, (dims, ((), ())), precision=prec, preferred_element_type=F32)


NN = ((1,), (0,))
NT = ((1,), (1,))
TN = ((0,), (0,))


MM_VMEM_BUDGET = 40 * 1024 * 1024


def _mm_tiles(mode, M, N, K, a_bytes, b_bytes, exchange_out):
    tn = _pick(N, (1408, 1024, 512, 256, 128))
    tms = [t for t in (1408, 1024, 512, 256, 128) if M % t == 0 and not (exchange_out and (M // 2) % t)] or [M]
    tks = [K] + [t for t in (2816, 2048, 1408, 1024, 512, 256, 128) if t < K and K % t == 0]

    def fits(tm, tk):
        acc = tm * tn * 4 if tk < K else 0
        return 2 * tm * tk * a_bytes + 2 * tk * tn * b_bytes + 2 * tm * tn * 4 + acc <= MM_VMEM_BUDGET

    for min_tm in (min(512, tms[0]), 0):
        for tk in tks:
            for tm in tms:
                if tm >= min_tm and fits(tm, tk):
                    return tm, tn, tk
    return tms[-1], tn, tks[-1]


def _mm(a, b, mode, name, b_layer=None, out_dtype=F32, exchange_out=False):
    b2 = b.shape[-2:]
    if mode == "nn":
        (M, K), (_, N) = a.shape, b2
    elif mode == "nt":
        (M, K), (N, _) = a.shape, b2
    else:
        (K, M), (_, N) = a.shape, b2
    tm, tn, tk = _mm_tiles(mode, M, N, K, a.dtype.itemsize, b.dtype.itemsize, exchange_out)
    nk = K // tk
    dims = {"nn": NN, "nt": NT, "tn": TN}[mode]

    def body(a_ref, b_ref, o_ref, *scratch):
        part = _dot(a_ref[...].astype(BF16), b_ref[...].astype(BF16), dims)
        if nk == 1:
            o_ref[...] = part.astype(o_ref.dtype)
            return
        (acc_ref,) = scratch
        k = pl.program_id(2)

        @pl.when(k == 0)
        def _():
            acc_ref[...] = part

        @pl.when(k > 0)
        def _():
            acc_ref[...] += part

        @pl.when(k == nk - 1)
        def _():
            o_ref[...] = acc_ref[...].astype(o_ref.dtype)

    if mode == "tn":
        a_spec = pl.BlockSpec((tk, tm), lambda i, j, k: (k, i))
    else:
        a_spec = pl.BlockSpec((tm, tk), lambda i, j, k: (i, k))
    bblk = (tk, tn) if mode in ("nn", "tn") else (tn, tk)
    bidx = (lambda i, j, k: (k, j)) if mode in ("nn", "tn") else (lambda i, j, k: (j, k))
    if b_layer is None:
        b_spec = pl.BlockSpec(bblk, bidx)
    else:
        b_spec = pl.BlockSpec((None,) + bblk, lambda i, j, k: (b_layer,) + bidx(i, j, k))
    if exchange_out:
        mh, cw = M // 2, N // N_CHIPS
        assert mh % tm == 0 and cw % tn == 0
        out_shape = jax.ShapeDtypeStruct((N_CHIPS, 2, mh, cw), out_dtype)
        o_spec = pl.BlockSpec(
            (None, None, tm, tn),
            lambda i, j, k: (j // (cw // tn), i // (mh // tm), i % (mh // tm), j % (cw // tn)))
    else:
        out_shape = jax.ShapeDtypeStruct((M, N), out_dtype)
        o_spec = pl.BlockSpec((tm, tn), lambda i, j, k: (i, j))
    return pl.pallas_call(
        body, name=name, out_shape=out_shape, grid=(M // tm, N // tn, nk),
        in_specs=[a_spec, b_spec], out_specs=o_spec,
        scratch_shapes=[] if nk == 1 else [pltpu.VMEM((tm, tn), F32)],
        compiler_params=_params(("parallel", "parallel", "arbitrary")),
    )(a, b)


def _row_spec(tr, width):
    return pl.BlockSpec((tr, width), lambda i: (i, 0))


def _vec_spec(width, rows=1):
    return pl.BlockSpec((rows, width), lambda i: (0, 0))


def _norm_fwd(x, y, gate, g, sc, sh, name):
    T, Dm = x.shape
    tr = _pick(T, (ROW_TILE,))
    has_res = y is not None

    def body(*refs):
        if has_res:
            x_ref, y_ref, gate_ref, g_ref, sc_ref, sh_ref, xo_ref, h_ref = refs
            xv = x_ref[...] + gate_ref[...] * y_ref[...]
            xo_ref[...] = xv
        else:
            x_ref, g_ref, sc_ref, sh_ref, h_ref = refs
            xv = x_ref[...]
        rstd = lax.rsqrt(jnp.mean(xv * xv, axis=-1, keepdims=True) + EPS)
        h_ref[...] = ((xv * rstd * g_ref[...]) * (1.0 + sc_ref[...]) + sh_ref[...]).astype(BF16)

    row, vec = _row_spec(tr, Dm), _vec_spec(Dm)
    if has_res:
        ins, in_specs = (x, y, gate, g, sc, sh), [row, row, vec, vec, vec, vec]
        out_shape = (jax.ShapeDtypeStruct((T, Dm), F32), jax.ShapeDtypeStruct((T, Dm), BF16))
        out_specs = (row, row)
    else:
        ins, in_specs = (x, g, sc, sh), [row, vec, vec, vec]
        out_shape = jax.ShapeDtypeStruct((T, Dm), BF16)
        out_specs = row
    out = pl.pallas_call(body, name=name, out_shape=out_shape, grid=(T // tr,), in_specs=in_specs,
                         out_specs=out_specs, compiler_params=_params(("parallel",)))(*ins)
    return out if has_res else (x, out)


def _norm_bwd(x, dh, dxo, g, sc, y_prev, gate_prev, name):
    T, Dm = x.shape
    tr = _pick(T, (ROW_TILE,))
    has_prev = y_prev is not None

    def body(*refs):
        if has_prev:
            x_ref, dh_ref, dxo_ref, g_ref, sc_ref, yp_ref, gp_ref, dx_ref, dyp_ref, s1_ref, s2_ref, s3_ref = refs
        else:
            x_ref, dh_ref, dxo_ref, g_ref, sc_ref, dx_ref, s1_ref, s2_ref = refs

        @pl.when(pl.program_id(0) == 0)
        def _():
            s1_ref[...] = jnp.zeros_like(s1_ref)
            s2_ref[...] = jnp.zeros_like(s2_ref)
            if has_prev:
                s3_ref[...] = jnp.zeros_like(s3_ref)

        xv = x_ref[...]
        rstd = lax.rsqrt(jnp.mean(xv * xv, axis=-1, keepdims=True) + EPS)
        xhat = xv * rstd
        dh = dh_ref[...]
        dxhat = dh * (g_ref[...] * (1.0 + sc_ref[...]))
        dx = dxo_ref[...] + rstd * (dxhat - xhat * jnp.mean(dxhat * xhat, axis=-1, keepdims=True))
        dx_ref[...] = dx
        s1_ref[...] += jnp.sum(dh, axis=0, keepdims=True)
        s2_ref[...] += jnp.sum(dh * xhat, axis=0, keepdims=True)
        if has_prev:
            dyp_ref[...] = (gp_ref[...] * dx).astype(BF16)
            s3_ref[...] += jnp.sum(dx * yp_ref[...], axis=0, keepdims=True)

    row, vec = _row_spec(tr, Dm), _vec_spec(Dm)
    vshape = jax.ShapeDtypeStruct((1, Dm), F32)
    if has_prev:
        ins, in_specs = (x, dh, dxo, g, sc, y_prev, gate_prev), [row, row, row, vec, vec, row, vec]
        out_shape = (jax.ShapeDtypeStruct((T, Dm), F32), jax.ShapeDtypeStruct((T, Dm), BF16), vshape, vshape, vshape)
        out_specs = (row, row, vec, vec, vec)
    else:
        ins, in_specs = (x, dh, dxo, g, sc), [row, row, row, vec, vec]
        out_shape = (jax.ShapeDtypeStruct((T, Dm), F32), vshape, vshape)
        out_specs = (row, vec, vec)
    return pl.pallas_call(body, name=name, out_shape=out_shape, grid=(T // tr,), in_specs=in_specs,
                          out_specs=out_specs, compiler_params=_params(("arbitrary",)))(*ins)


def _loss_head(x, y, gate, fg, tgt, name):
    T, Dm = x.shape
    tr = _pick(T, (ROW_TILE,))
    nsteps = T // tr

    def body(x_ref, y_ref, gate_ref, fg_ref, t_ref, dx_ref, dy_ref, loss_ref, sfg_ref, sg_ref, acc_ref):
        i = pl.program_id(0)

        @pl.when(i == 0)
        def _():
            acc_ref[...] = jnp.zeros_like(acc_ref)
            sfg_ref[...] = jnp.zeros_like(sfg_ref)
            sg_ref[...] = jnp.zeros_like(sg_ref)

        yv = y_ref[...]
        xv = x_ref[...] + gate_ref[...] * yv
        rstd = lax.rsqrt(jnp.mean(xv * xv, axis=-1, keepdims=True) + EPS)
        xhat = xv * rstd
        err = xhat * fg_ref[...] - t_ref[...]
        acc_ref[...] += jnp.sum(err * err, axis=0, keepdims=True)
        dyn = err * (1.0 / Dm)
        sfg_ref[...] += jnp.sum(dyn * xhat, axis=0, keepdims=True)
        dxhat = dyn * fg_ref[...]
        dx = rstd * (dxhat - xhat * jnp.mean(dxhat * xhat, axis=-1, keepdims=True))
        dx_ref[...] = dx
        dy_ref[...] = (gate_ref[...] * dx).astype(BF16)
        sg_ref[...] += jnp.sum(dx * yv, axis=0, keepdims=True)

        @pl.when(i == nsteps - 1)
        def _():
            total = jnp.sum(acc_ref[...], axis=1, keepdims=True) * (0.5 / Dm)
            loss_ref[...] = jnp.broadcast_to(total, loss_ref.shape)

    row, vec = _row_spec(tr, Dm), _vec_spec(Dm)
    vshape = jax.ShapeDtypeStruct((1, Dm), F32)
    return pl.pallas_call(
        body, name=name, grid=(nsteps,),
        out_shape=(jax.ShapeDtypeStruct((T, Dm), F32), jax.ShapeDtypeStruct((T, Dm), BF16),
                   jax.ShapeDtypeStruct((1, LANES), F32), vshape, vshape),
        in_specs=[row, row, vec, vec, row], out_specs=(row, row, _vec_spec(LANES), vec, vec),
        scratch_shapes=[pltpu.VMEM((1, Dm), F32)], compiler_params=_params(("arbitrary",)),
    )(x, y, gate, fg, tgt)


def _spatial_mask():
    r = lax.broadcasted_iota(jnp.int32, (GM_BLOCK, GM_BLOCK), 0) // CHUNK
    c = lax.broadcasted_iota(jnp.int32, (GM_BLOCK, GM_BLOCK), 1) // CHUNK
    return r >= c


def _gm_specs(tr):
    return [_row_spec(tr, 2 * GM_WIDTH), _vec_spec(GM_WIDTH), _vec_spec(GM_WIDTH),
            pl.BlockSpec((GM_HEADS, GM_BLOCK, GM_BLOCK), lambda i: (0, 0, 0)),
            pl.BlockSpec((GM_HEADS, GM_BLOCK, 1), lambda i: (0, 0, 0))]


def _gm_mid_fwd(a, ln_g, ln_b, ws, bs3, name):
    T = a.shape[0]
    tr = _pick(T, (ROW_TILE,))
    W = GM_WIDTH

    def body(a_ref, lg_ref, lb_ref, ws_ref, bs_ref, o_ref, vn_scr):
        av = a_ref[:, W:]
        v = av * _cdf(av)
        vc = v - jnp.mean(v, axis=-1, keepdims=True)
        rstd = lax.rsqrt(jnp.mean(vc * vc, axis=-1, keepdims=True) + EPS)
        vn_scr[...] = (vc * rstd * lg_ref[...] + lb_ref[...]).astype(BF16)
        mask = _spatial_mask()
        for h in range(GM_HEADS):
            w = jnp.where(mask, ws_ref[h], 0.0).astype(BF16)
            cs = slice(h * GM_HEAD_DIM, (h + 1) * GM_HEAD_DIM)
            for blk in range(tr // GM_BLOCK):
                rs = slice(blk * GM_BLOCK, (blk + 1) * GM_BLOCK)
                s = _dot(w, vn_scr[rs, cs], NN) + bs_ref[h]
                au = a_ref[rs, cs]
                o_ref[rs, cs] = (au * _cdf(au) * s).astype(BF16)

    return pl.pallas_call(
        body, name=name, out_shape=jax.ShapeDtypeStruct((T, W), BF16), grid=(T // tr,),
        in_specs=_gm_specs(tr), out_specs=_row_spec(tr, W),
        scratch_shapes=[pltpu.VMEM((tr, W), BF16)], compiler_params=_params(("parallel",)),
    )(a, ln_g, ln_b, ws, bs3)


def _gm_mid_bwd(a, dgated, ln_g, ln_b, ws, bs3, name):
    T = a.shape[0]
    tr = _pick(T, (ROW_TILE,))
    W = GM_WIDTH
    nsteps = T // tr

    def body(a_ref, dg_ref, lg_ref, lb_ref, ws_ref, bs_ref, da_ref, dws_ref, dbs_ref, dlg_ref, dlb_ref,
             vn_scr, vhat_scr, dvn_scr, dsum_scr):
        i = pl.program_id(0)

        @pl.when(i == 0)
        def _():
            dws_ref[...] = jnp.zeros_like(dws_ref)
            dbs_ref[...] = jnp.zeros_like(dbs_ref)
            dlg_ref[...] = jnp.zeros_like(dlg_ref)
            dlb_ref[...] = jnp.zeros_like(dlb_ref)
            dsum_scr[...] = jnp.zeros_like(dsum_scr)

        av = a_ref[:, W:]
        cdf_v = _cdf(av)
        v = av * cdf_v
        vc = v - jnp.mean(v, axis=-1, keepdims=True)
        rstd = lax.rsqrt(jnp.mean(vc * vc, axis=-1, keepdims=True) + EPS)
        vhat_scr[...] = vc * rstd
        vn_scr[...] = (vhat_scr[...] * lg_ref[...] + lb_ref[...]).astype(BF16)
        mask = _spatial_mask()
        for h in range(GM_HEADS):
            w = jnp.where(mask, ws_ref[h], 0.0).astype(BF16)
            cs = slice(h * GM_HEAD_DIM, (h + 1) * GM_HEAD_DIM)
            for blk in range(tr // GM_BLOCK):
                rs = slice(blk * GM_BLOCK, (blk + 1) * GM_BLOCK)
                vnb = vn_scr[rs, cs]
                s = _dot(w, vnb, NN) + bs_ref[h]
                au = a_ref[rs, cs]
                cdf_u = _cdf(au)
                dg = dg_ref[rs, cs]
                ds = dg * (au * cdf_u)
                da_ref[rs, cs] = (dg * s * (cdf_u + au * _pdf(au))).astype(BF16)
                dsb = ds.astype(BF16)
                dvn_scr[rs, cs] = _dot(w, dsb, TN)
                dws_ref[h] += _dot(dsb, vnb, NT)
                dsum_scr[:, cs] += ds
        dvn = dvn_scr[...]
        vhat = vhat_scr[...]
        dlg_ref[...] += jnp.sum(dvn * vhat, axis=0, keepdims=True)
        dlb_ref[...] += jnp.sum(dvn, axis=0, keepdims=True)
        dvh = dvn * lg_ref[...]
        dv = rstd * (dvh - jnp.mean(dvh, axis=-1, keepdims=True)
                     - vhat * jnp.mean(dvh * vhat, axis=-1, keepdims=True))
        da_ref[:, W:] = (dv * (cdf_v + av * _pdf(av))).astype(BF16)

        @pl.when(i == nsteps - 1)
        def _():
            for h in range(GM_HEADS):
                dws_ref[h] = jnp.where(mask, dws_ref[h], 0.0)
            col_head = lax.broadcasted_iota(jnp.int32, (W, GM_BLOCK), 0) // GM_HEAD_DIM
            sel = (col_head == lax.broadcasted_iota(jnp.int32, (W, GM_BLOCK), 1)).astype(F32)
            dbs_ref[...] = _dot(dsum_scr[...], sel, NN, HI)

    vshape = jax.ShapeDtypeStruct((1, W), F32)
    return pl.pallas_call(
        body, name=name, grid=(nsteps,),
        out_shape=(jax.ShapeDtypeStruct((T, 2 * W), BF16), jax.ShapeDtypeStruct((GM_HEADS, GM_BLOCK, GM_BLOCK), F32),
                   jax.ShapeDtypeStruct((GM_BLOCK, GM_BLOCK), F32), vshape, vshape),
        in_specs=[_gm_specs(tr)[0], _row_spec(tr, W)] + _gm_specs(tr)[1:],
        out_specs=(_row_spec(tr, 2 * W), pl.BlockSpec((GM_HEADS, GM_BLOCK, GM_BLOCK), lambda i: (0, 0, 0)),
                   pl.BlockSpec((GM_BLOCK, GM_BLOCK), lambda i: (0, 0)), _vec_spec(W), _vec_spec(W)),
        scratch_shapes=[pltpu.VMEM((tr, W), BF16), pltpu.VMEM((tr, W), F32), pltpu.VMEM((tr, W), F32),
                        pltpu.VMEM((GM_BLOCK, W), F32)],
        compiler_params=_params(("arbitrary",)),
    )(a, dgated, ln_g, ln_b, ws, bs3)


SUB = 16
EXP_CLAMP = 80.0


def _tri(lower):
    r = lax.broadcasted_iota(jnp.int32, (CHUNK, CHUNK), 0)
    c = lax.broadcasted_iota(jnp.int32, (CHUNK, CHUNK), 1)
    return (r >= c) if lower else (c >= r)


def _score_masks():
    i = lax.broadcasted_iota(jnp.int32, (CHUNK, CHUNK), 0)
    j = lax.broadcasted_iota(jnp.int32, (CHUNK, CHUNK), 1)
    bi, bj = i // SUB, j // SUB
    diag = (bi == bj) & (i >= j)
    pair = (bi % 2 == 1) & (bj == bi - 1)
    half = (i >= CHUNK // 2) & (j < CHUNK // 2)
    return diag, pair, half


def _dot01(m, x):
    x1 = x.astype(BF16)
    rest = x - x1.astype(F32)
    x2 = rest.astype(BF16)
    x3 = (rest - x2.astype(F32)).astype(BF16)
    return _dot(m, x1, NN) + (_dot(m, x2, NN) + _dot(m, x3, NN))


def _block_rows(b, offset):
    parts = []
    for blk in range(0, CHUNK, SUB):
        r = blk + offset
        parts.append(jnp.zeros((SUB, b.shape[1]), F32) if r < 0 else jnp.broadcast_to(b[r:r + 1], (SUB, b.shape[1])))
    return jnp.concatenate(parts, axis=0)


def _hg_gates(p_ref, lb_ref, lower):
    Dm = D_MODEL
    heads = []
    for h in range(HG_HEADS):
        c0 = h * HG_DIM
        qr = p_ref[:, c0:c0 + HG_DIM]
        fz = p_ref[:, Dm + c0:Dm + c0 + HG_DIM]
        lbh = lb_ref[:, c0:c0 + HG_DIM]
        sg = _sig(fz)
        f = lbh + (1.0 - lbh) * sg
        sq = _sig(qr)
        heads.append(dict(qr=qr, v=p_ref[:, 2 * Dm + c0:2 * Dm + c0 + HG_DIM],
                          gt=p_ref[:, 3 * Dm + c0:3 * Dm + c0 + HG_DIM], lbh=lbh, sg=sg, f=f, gl=jnp.log(f),
                          kk=1.0 - f, sq=sq, q=qr * sq))
    for g in heads:
        g["b"] = _dot01(lower, g.pop("gl"))
    for g in heads:
        g.update(_hg_scalings(g["q"], g["kk"], g.pop("b")))
    return heads


def _hg_scalings(q, kk, b):
    r_mid = _block_rows(b, SUB // 2 - 1)
    r_prev = _block_rows(b, -1)
    r_end = _block_rows(b, SUB - 1)
    r_half = jnp.broadcast_to(b[CHUNK // 2 - 1:CHUNK // 2], b.shape)
    bc = b[CHUNK - 1:CHUNK]
    eqs = (jnp.exp(jnp.clip(b - r_mid, -EXP_CLAMP, EXP_CLAMP)), jnp.exp(jnp.minimum(b - r_prev, 0.0)),
           jnp.exp(jnp.minimum(b - r_half, 0.0)))
    eks = (jnp.exp(jnp.clip(r_mid - b, -EXP_CLAMP, EXP_CLAMP)), jnp.exp(jnp.minimum(r_end - b, 0.0)),
           jnp.exp(jnp.minimum(r_half - b, 0.0)))
    eb = jnp.exp(b)
    ec = jnp.exp(bc - b)
    return dict(eqs=eqs, eks=eks, eb=eb, ec=ec, e_end=jnp.exp(bc), qs=[q * e for e in eqs],
                ks=[kk * e for e in eks], qe=q * eb, ke=kk * ec)


def _scores(g, masks):
    a = None
    for qs, ks, m in zip(g["qs"], g["ks"], masks):
        part = jnp.where(m, _dot(qs.astype(BF16), ks.astype(BF16), NT), 0.0)
        a = part if a is None else a + part
    return a


def _hg_scan_fwd(p, lb, gn, name):
    T = p.shape[0]
    nc = T // CHUNK
    Dm = D_MODEL

    def body(p_ref, lb_ref, gn_ref, o_ref, og_ref, so_ref, st_ref):
        @pl.when(pl.program_id(0) == 0)
        def _():
            st_ref[...] = jnp.zeros_like(st_ref)

        masks = _score_masks()
        heads = _hg_gates(p_ref, lb_ref, _tri(True).astype(BF16))
        states = [st_ref[h] for h in range(HG_HEADS)]
        scores = [_scores(g, masks) for g in heads]
        outs = [_dot(a.astype(BF16), g["v"].astype(BF16), NN) + _dot(g["qe"], st, NT, X3)
                for g, a, st in zip(heads, scores, states)]
        new_states = [st * g["e_end"] + _dot(g["v"], g["ke"], TN, X3) for g, st in zip(heads, states)]
        for h, (g, o, st, st2) in enumerate(zip(heads, outs, states, new_states)):
            cs = slice(h * HG_DIM, (h + 1) * HG_DIM)
            so_ref[0, h] = st
            st_ref[h] = st2
            o_ref[:, cs] = o
            r = lax.rsqrt(jnp.mean(o * o, axis=-1, keepdims=True) + EPS)
            gt = g["gt"]
            og_ref[:, cs] = (((o * r) * gn_ref[:, cs]).astype(F32) * (gt * _sig(gt))).astype(BF16)

    return pl.pallas_call(
        body, name=name, grid=(nc,),
        out_shape=(jax.ShapeDtypeStruct((T, Dm), F32), jax.ShapeDtypeStruct((T, Dm), BF16),
                   jax.ShapeDtypeStruct((nc, HG_HEADS, HG_DIM, HG_DIM), F32)),
        in_specs=[_row_spec(CHUNK, 4 * Dm), _vec_spec(Dm), _vec_spec(Dm)],
        out_specs=(_row_spec(CHUNK, Dm), _row_spec(CHUNK, Dm),
                   pl.BlockSpec((1, HG_HEADS, HG_DIM, HG_DIM), lambda i: (i, 0, 0, 0))),
        scratch_shapes=[pltpu.VMEM((HG_HEADS, HG_DIM, HG_DIM), F32)],
        compiler_params=_params(("arbitrary",)),
    )(p, lb, gn)


def _hg_scan_bwd(p, lb, gn, o, dog, states, name):
    T = p.shape[0]
    nc = T // CHUNK
    Dm = D_MODEL

    def rev(i):
        return nc - 1 - i

    def body(p_ref, lb_ref, gn_ref, o_ref, dog_ref, st_in_ref, dp_ref, dlb_ref, dgn_ref, dst_ref, carry_ref):
        @pl.when(pl.program_id(0) == 0)
        def _():
            dst_ref[...] = jnp.zeros_like(dst_ref)
            carry_ref[...] = jnp.zeros_like(carry_ref)
            dlb_ref[...] = jnp.zeros_like(dlb_ref)
            dgn_ref[...] = jnp.zeros_like(dgn_ref)

        upper = _tri(False).astype(BF16)
        masks = _score_masks()
        heads = _hg_gates(p_ref, lb_ref, _tri(True).astype(BF16))
        for h, g in enumerate(heads):
            cs = slice(h * HG_DIM, (h + 1) * HG_DIM)
            oh = o_ref[:, cs]
            r = lax.rsqrt(jnp.mean(oh * oh, axis=-1, keepdims=True) + EPS)
            on = oh * r
            gt = g["gt"]
            sgt = _sig(gt)
            sil = gt * sgt
            dogh = dog_ref[:, cs]
            gnh = gn_ref[:, cs]
            don = dogh * gnh * sil
            g["dgn"] = jnp.sum(dogh * on * sil, axis=0, keepdims=True)
            g["dgate"] = dogh * on * gnh * (sgt * (1.0 + gt * (1.0 - sgt)))
            g["do"] = r * (don - on * jnp.mean(don * on, axis=-1, keepdims=True))
            g["dst"] = dst_ref[h]
            g["st"] = st_in_ref[0, h]
            g["carry"] = carry_ref[h]
        for g in heads:
            g["a"] = _scores(g, masks)
            g["dob"] = g["do"].astype(BF16)
            g["da"] = _dot(g["dob"], g["v"].astype(BF16), NT)
        for g in heads:
            g["dv"] = _dot(g["a"].astype(BF16), g["dob"], TN) + _dot(g["ke"].astype(BF16), g["dst"].astype(BF16), NT)
            g["dq"] = _dot(g["do"], g["st"], NN, X3) * g["eb"]
            g["dk"] = _dot(g["v"], g["dst"], NN, X3) * g["ec"]
            g["dst2"] = g["dst"] * g["e_end"] + _dot(g["do"], g["qe"], TN, X3)
        for lvl in range(3):
            for g in heads:
                dam = jnp.where(masks[lvl], g["da"], 0.0)
                g["dq"] = g["dq"] + _dot(dam, g["ks"][lvl], NN, X3) * g["eqs"][lvl]
                g["dk"] = g["dk"] + _dot(dam, g["qs"][lvl], TN, X3) * g["eks"][lvl]
        for g in heads:
            g["dgd"] = g["q"] * g["dq"] - g["kk"] * g["dk"]
            g["dgl"] = _dot01(upper, g["dgd"]) + g["carry"]
        for h, g in enumerate(heads):
            c0 = h * HG_DIM
            cs = slice(c0, c0 + HG_DIM)
            df = g["dgl"] / g["f"] - g["dk"]
            sg, sq, qr = g["sg"], g["sq"], g["qr"]
            dst_ref[h] = g["dst2"]
            carry_ref[h] = g["carry"] + jnp.sum(g["dgd"], axis=0, keepdims=True)
            dgn_ref[:, cs] += g["dgn"]
            dlb_ref[:, cs] += jnp.sum(df * (1.0 - sg), axis=0, keepdims=True)
            dp_ref[:, c0:c0 + HG_DIM] = (g["dq"] * (sq * (1.0 + qr * (1.0 - sq)))).astype(BF16)
            dp_ref[:, Dm + c0:Dm + c0 + HG_DIM] = (df * (1.0 - g["lbh"]) * sg * (1.0 - sg)).astype(BF16)
            dp_ref[:, 2 * Dm + c0:2 * Dm + c0 + HG_DIM] = g["dv"].astype(BF16)
            dp_ref[:, 3 * Dm + c0:3 * Dm + c0 + HG_DIM] = g["dgate"].astype(BF16)

    vshape = jax.ShapeDtypeStruct((1, Dm), F32)
    rrow = lambda w: pl.BlockSpec((CHUNK, w), lambda i: (rev(i), 0))
    return pl.pallas_call(
        body, name=name, grid=(nc,),
        out_shape=(jax.ShapeDtypeStruct((T, 4 * Dm), BF16), vshape, vshape),
        in_specs=[rrow(4 * Dm), _vec_spec(Dm), _vec_spec(Dm), rrow(Dm), rrow(Dm),
                  pl.BlockSpec((1, HG_HEADS, HG_DIM, HG_DIM), lambda i: (rev(i), 0, 0, 0))],
        out_specs=(rrow(4 * Dm), _vec_spec(Dm), _vec_spec(Dm)),
        scratch_shapes=[pltpu.VMEM((HG_HEADS, HG_DIM, HG_DIM), F32), pltpu.VMEM((HG_HEADS, 1, HG_DIM), F32)],
        compiler_params=_params(("arbitrary",)),
    )(p, lb, gn, o, dog, states)


def _lb_fwd(hg_lb, name):
    def body(a_ref, o_ref):
        a0, a1 = a_ref[0:1], a_ref[1:2]
        m = jnp.maximum(a0, a1)
        e0, e1 = jnp.exp(a0 - m), jnp.exp(a1 - m)
        p0, p1 = e0 / (e0 + e1), e1 / (e0 + e1)
        o_ref[0:1] = p0 - p0
        o_ref[1:2] = (p0 + p1) - p0

    return pl.pallas_call(body, name=name, out_shape=jax.ShapeDtypeStruct(hg_lb.shape, F32))(hg_lb)


def _lb_bwd(hg_lb, dlb_all, name):
    def body(a_ref, d_ref, o_ref):
        a0, a1 = a_ref[0:1], a_ref[1:2]
        m = jnp.maximum(a0, a1)
        e0, e1 = jnp.exp(a0 - m), jnp.exp(a1 - m)
        p0, p1 = e0 / (e0 + e1), e1 / (e0 + e1)
        d1 = d_ref[1:2]
        o_ref[0:1] = -p0 * p1 * d1
        o_ref[1:2] = p1 * (1.0 - p1) * d1

    return pl.pallas_call(body, name=name, out_shape=jax.ShapeDtypeStruct(hg_lb.shape, F32))(hg_lb, dlb_all)


CONV_COLS_FWD = 256
CONV_COLS_BWD = 128


def _conv_fwd(a, w, b, name):
    T = a.shape[0]
    Fh = FFN_HIDDEN
    tr = _pick(T, (ROW_TILE,))
    cw = CONV_COLS_FWD
    hb = tr // 8

    def body(a_ref, ap_ref, w_ref, b_ref, m_ref):
        m0 = (pl.program_id(0) > 0).astype(F32)

        def conv(cc):
            x = jnp.concatenate([ap_ref[:, pl.ds(cc, cw)] * m0, a_ref[:, pl.ds(cc, cw)]], axis=0)
            wv = w_ref[:, pl.ds(cc, cw)]
            y = b_ref[:, pl.ds(cc, cw)] + wv[2:3] * x + wv[1:2] * pltpu.roll(x, 1, axis=0) \
                + wv[0:1] * pltpu.roll(x, 2, axis=0)
            return y[8:]

        def step(c, carry):
            c0 = pl.multiple_of(c * cw, cw)
            c1 = pl.multiple_of(Fh + c * cw, cw)
            yg, yv = conv(c0), conv(c1)
            m_ref[:, pl.ds(c0, cw)] = (yg * _cdf(yg) * yv).astype(BF16)
            return carry

        lax.fori_loop(0, Fh // cw, step, 0)

    return pl.pallas_call(
        body, name=name, out_shape=jax.ShapeDtypeStruct((T, Fh), BF16), grid=(T // tr,),
        in_specs=[_row_spec(tr, 2 * Fh), pl.BlockSpec((8, 2 * Fh), lambda i: (jnp.maximum(i * hb - 1, 0), 0)),
                  _vec_spec(2 * Fh, 3), _vec_spec(2 * Fh)],
        out_specs=_row_spec(tr, Fh), compiler_params=_params(("parallel",)),
    )(a, a, w, b)


def _conv_bwd(a, dm, w, b, name):
    T = a.shape[0]
    Fh = FFN_HIDDEN
    tr = _pick(T, (ROW_TILE,))
    cw = CONV_COLS_BWD
    hb = tr // 8
    nsteps = T // tr
    n = tr + 8

    def body(a_ref, ap_ref, an_ref, dm_ref, dmn_ref, w_ref, b_ref, da_ref, dw_ref, db_ref):
        i = pl.program_id(0)
        m0 = (i > 0).astype(F32)
        m1 = (i < nsteps - 1).astype(F32)

        @pl.when(i == 0)
        def _():
            dw_ref[...] = jnp.zeros_like(dw_ref)
            db_ref[...] = jnp.zeros_like(db_ref)

        def prep(cc):
            x = jnp.concatenate([ap_ref[:, pl.ds(cc, cw)] * m0, a_ref[:, pl.ds(cc, cw)],
                                 an_ref[:, pl.ds(cc, cw)] * m1], axis=0)
            wv = w_ref[:, pl.ds(cc, cw)]
            s1 = pltpu.roll(x, 1, axis=0)
            s2 = pltpu.roll(x, 2, axis=0)
            y = b_ref[:, pl.ds(cc, cw)] + wv[2:3] * x + wv[1:2] * s1 + wv[0:1] * s2
            return wv, x[8:], s1[8:], s2[8:], y[8:]

        def back(cc, dy, wv, x0, s1, s2):
            da = wv[2:3] * dy + wv[1:2] * pltpu.roll(dy, n - 1, axis=0) + wv[0:1] * pltpu.roll(dy, n - 2, axis=0)
            da_ref[:, pl.ds(cc, cw)] = da[:tr].astype(BF16)
            d = dy[:tr]
            db_ref[:, pl.ds(cc, cw)] += jnp.sum(d, axis=0, keepdims=True)
            dw_ref[2:3, pl.ds(cc, cw)] += jnp.sum(d * x0[:tr], axis=0, keepdims=True)
            dw_ref[1:2, pl.ds(cc, cw)] += jnp.sum(d * s1[:tr], axis=0, keepdims=True)
            dw_ref[0:1, pl.ds(cc, cw)] += jnp.sum(d * s2[:tr], axis=0, keepdims=True)

        def step(c, carry):
            c0 = pl.multiple_of(c * cw, cw)
            c1 = pl.multiple_of(Fh + c * cw, cw)
            dmx = jnp.concatenate([dm_ref[:, pl.ds(c0, cw)], dmn_ref[:, pl.ds(c0, cw)] * m1], axis=0)
            wg, xg, s1g, s2g, yg = prep(c0)
            wv, xv, s1v, s2v, yv = prep(c1)
            cg = _cdf(yg)
            back(c0, dmx * yv * (cg + yg * _pdf(yg)), wg, xg, s1g, s2g)
            back(c1, dmx * (yg * cg), wv, xv, s1v, s2v)
            return carry

        lax.fori_loop(0, Fh // cw, step, 0)

    prev = lambda wd: pl.BlockSpec((8, wd), lambda i: (jnp.maximum(i * hb - 1, 0), 0))
    nxt = lambda wd: pl.BlockSpec((8, wd), lambda i: (jnp.minimum((i + 1) * hb, T // 8 - 1), 0))
    return pl.pallas_call(
        body, name=name, grid=(nsteps,),
        out_shape=(jax.ShapeDtypeStruct((T, 2 * Fh), BF16), jax.ShapeDtypeStruct((3, 2 * Fh), F32),
                   jax.ShapeDtypeStruct((1, 2 * Fh), F32)),
        in_specs=[_row_spec(tr, 2 * Fh), prev(2 * Fh), nxt(2 * Fh), _row_spec(tr, Fh), nxt(Fh),
                  _vec_spec(2 * Fh, 3), _vec_spec(2 * Fh)],
        out_specs=(_row_spec(tr, 2 * Fh), _vec_spec(2 * Fh, 3), _vec_spec(2 * Fh)),
        compiler_params=_params(("arbitrary",)),
    )(a, a, a, dm, dm, w, b)


def _ada_fwd(c_all, ada_w, ada_b, name):
    L, Dm, cols = ada_w.shape
    tn = _pick(cols, (512, 256, 128))

    def body(c_ref, w_ref, b_ref, o_ref):
        cv = c_ref[...]
        cond = (cv * _sig(cv)).astype(BF16)
        o_ref[...] = _dot(cond, w_ref[...].astype(BF16), NN) + b_ref[...]

    return pl.pallas_call(
        body, name=name, out_shape=jax.ShapeDtypeStruct((L, N_DEV, cols), F32), grid=(L, cols // tn),
        in_specs=[pl.BlockSpec((N_DEV, Dm), lambda l, j: (0, 0)), pl.BlockSpec((None, Dm, tn), lambda l, j: (l, 0, j)),
                  pl.BlockSpec((None, 1, tn), lambda l, j: (l, 0, j))],
        out_specs=pl.BlockSpec((None, N_DEV, tn), lambda l, j: (l, 0, j)),
        compiler_params=_params(("parallel", "parallel")),
    )(c_all, ada_w, ada_b.reshape(L, 1, cols))


def _ada_bwd(c_all, dmod, name):
    L, _, cols = dmod.shape
    Dm = c_all.shape[1]
    tn = _pick(cols, (512, 256, 128))

    def body(c_ref, d_ref, o_ref):
        cv = c_ref[...]
        o_ref[...] = _dot(cv * _sig(cv), d_ref[...], TN, HI)

    return pl.pallas_call(
        body, name=name, out_shape=jax.ShapeDtypeStruct((L, Dm, cols), F32), grid=(L, cols // tn),
        in_specs=[pl.BlockSpec((N_DEV, Dm), lambda l, j: (0, 0)), pl.BlockSpec((None, N_DEV, tn), lambda l, j: (l, 0, j))],
        out_specs=pl.BlockSpec((None, Dm, tn), lambda l, j: (l, 0, j)),
        compiler_params=_params(("parallel", "parallel")),
    )(c_all, dmod)


def _add_own_half(g4, rb, core, name):
    S, _, rh, cw = g4.shape
    tr = _pick(rh, (256, 128, 176, 64))

    def body(core_ref, g_ref, r_ref, o_ref):
        o_ref[...] = (g_ref[...].astype(F32) + r_ref[...].astype(F32)).astype(GRAD_WIRE)

    return pl.pallas_call(
        body, name=name, out_shape=jax.ShapeDtypeStruct((S, rh, cw), BF16),
        grid_spec=pltpu.PrefetchScalarGridSpec(
            num_scalar_prefetch=1, grid=(S, rh // tr),
            in_specs=[pl.BlockSpec((None, None, tr, cw), lambda s, i, core_ref: (s, core_ref[0], i, 0)),
                      pl.BlockSpec((None, tr, cw), lambda s, i, core_ref: (s, i, 0))],
            out_specs=pl.BlockSpec((None, tr, cw), lambda s, i, core_ref: (s, i, 0))),
        compiler_params=_params(("parallel", "parallel")),
    )(core, g4, rb)


def _sum_chips(lands, sums, chip, name):
    L = len(lands)
    _, rh, cw = lands[0].shape
    tr = _pick(rh, (256, 128, 176, 64))

    def body(chip_ref, *refs):
        ld, cs, o_ref = refs[:L], refs[L:2 * L], refs[2 * L]
        me = chip_ref[0]
        for k in range(L):
            @pl.when(pl.program_id(0) == k)
            def _(k=k):
                own = cs[k][...].astype(F32)
                got = [ld[k][j].astype(F32) for j in range(3)]
                acc = None
                for t in range(N_CHIPS):
                    d = jnp.bitwise_xor(jnp.int32(t), me)
                    term = jnp.where(d == 0, own, jnp.where(d == 2, got[0], jnp.where(d == 1, got[1], got[2])))
                    acc = term if acc is None else acc + term
                o_ref[...] = acc

    frozen = lambda l, i, k: jnp.where(l == k, i, 0)
    in_specs = [pl.BlockSpec((3, tr, cw), lambda l, i, chip_ref, k=k: (0, frozen(l, i, k), 0)) for k in range(L)]
    in_specs += [pl.BlockSpec((None, tr, cw), lambda l, i, chip_ref, k=k: (chip_ref[0], frozen(l, i, k), 0))
                 for k in range(L)]
    return pl.pallas_call(
        body, name=name, out_shape=jax.ShapeDtypeStruct((L, rh, cw), F32),
        grid_spec=pltpu.PrefetchScalarGridSpec(
            num_scalar_prefetch=1, grid=(L, rh // tr), in_specs=in_specs,
            out_specs=pl.BlockSpec((None, tr, cw), lambda l, i, chip_ref: (l, i, 0))),
        compiler_params=_params(("arbitrary", "arbitrary")),
    )(chip, *lands, *sums)


def _sum_devices(gathered, name):
    n, R, _ = gathered.shape
    tr = _pick(R, (512, 448, 384, 256, 192, 128, 64, 32, 16, 8))

    def body(g_ref, o_ref):
        acc = g_ref[0]
        for d in range(1, n):
            acc = acc + g_ref[d]
        o_ref[...] = acc

    return pl.pallas_call(
        body, name=name, out_shape=jax.ShapeDtypeStruct((R, LANES), F32), grid=(R // tr,),
        in_specs=[pl.BlockSpec((n, tr, LANES), lambda i: (0, i, 0))], out_specs=pl.BlockSpec((tr, LANES), lambda i: (i, 0)),
        compiler_params=_params(("parallel",)),
    )(gathered)


def _adamw(w, g, m, v, name):
    R, C = w.shape
    tr = _pick(R, (256, 128, 64, 32, 16, 8))
    c1 = 1.0 / (1.0 - ADAM_B1 ** ADAM_STEP)
    c2 = 1.0 / (1.0 - ADAM_B2 ** ADAM_STEP)

    def body(w_ref, g_ref, m_ref, v_ref, d_ref, mo_ref, vo_ref):
        gv = g_ref[...]
        m2 = ADAM_B1 * m_ref[...] + (1.0 - ADAM_B1) * gv
        v2 = ADAM_B2 * v_ref[...] + (1.0 - ADAM_B2) * (gv * gv)
        mo_ref[...] = m2
        vo_ref[...] = v2
        d_ref[...] = -ADAM_LR * ((m2 * c1) / (jnp.sqrt(v2 * c2) + ADAM_EPS) + ADAM_WD * w_ref[...])

    spec = pl.BlockSpec((tr, C), lambda i: (i, 0))
    shp = jax.ShapeDtypeStruct((R, C), F32)
    return pl.pallas_call(body, name=name, out_shape=(shp, shp, shp), grid=(R // tr,), in_specs=[spec] * 4,
                          out_specs=(spec, spec, spec), compiler_params=_params(("parallel",)))(w, g, m, v)


def _adamw_halves(w, own, recv, m, v, core, name):
    L, rh, cw = own.shape
    tr = _pick(rh, (256, 128, 176, 64))
    c1 = 1.0 / (1.0 - ADAM_B1 ** ADAM_STEP)
    c2 = 1.0 / (1.0 - ADAM_B2 ** ADAM_STEP)

    def body(core_ref, w_ref, own_ref, recv_ref, m_ref, v_ref, g_ref, d_ref, mo_ref, vo_ref):
        gv = jnp.where(pl.program_id(1) == core_ref[0], own_ref[...], recv_ref[...])
        g_ref[...] = gv
        m2 = ADAM_B1 * m_ref[...] + (1.0 - ADAM_B1) * gv
        v2 = ADAM_B2 * v_ref[...] + (1.0 - ADAM_B2) * (gv * gv)
        mo_ref[...] = m2
        vo_ref[...] = v2
        d_ref[...] = -ADAM_LR * ((m2 * c1) / (jnp.sqrt(v2 * c2) + ADAM_EPS) + ADAM_WD * w_ref[...])

    full = pl.BlockSpec((None, None, tr, cw), lambda l, hf, i, core_ref: (l, hf, i, 0))
    mine = pl.BlockSpec((None, tr, cw), lambda l, hf, i, core_ref: (l, jnp.where(hf == core_ref[0], i, 0), 0))
    other = pl.BlockSpec((None, tr, cw), lambda l, hf, i, core_ref: (l, jnp.where(hf == core_ref[0], 0, i), 0))
    shp = jax.ShapeDtypeStruct((L, 2, rh, cw), F32)
    view = lambda a: a.reshape(L, 2, rh, cw)
    outs = pl.pallas_call(
        body, name=name, out_shape=(shp, shp, shp, shp),
        grid_spec=pltpu.PrefetchScalarGridSpec(
            num_scalar_prefetch=1, grid=(L, 2, rh // tr), in_specs=[full, mine, other, full, full],
            out_specs=(full, full, full, full)),
        compiler_params=_params(("arbitrary", "arbitrary", "arbitrary")),
    )(core, view(w), own, recv, view(m), view(v))
    return tuple(o.reshape(L, 2 * rh, cw) for o in outs)


ANY = pl.BlockSpec(memory_space=pl.ANY)


def _position():
    x, y, c = lax.axis_index("x"), lax.axis_index("y"), lax.axis_index("c")
    return x, y, c


def _allgather(ins, out_shapes, src_fns, dst_fns, name, in_vmem):
    n = len(ins)

    def body(*refs):
        in_refs, out_refs = refs[:n], refs[n:2 * n]
        send_sems, recv_sems, local_sems = refs[2 * n:]
        x, y, c = _position()
        me, sibling = (x, y, c), (x, y, 1 - c)
        chips = [(1 - x, y), (x, 1 - y), (1 - x, 1 - y)]

        def copy(k, j, block, to, own=False):
            dst = dst_fns[k](out_refs[k], *block)
            return pltpu.make_async_remote_copy(
                src_ref=src_fns[k](in_refs[k], c) if own else dst, dst_ref=dst,
                send_sem=send_sems.at[k, j], recv_sem=recv_sems.at[k, j], device_id=to, device_id_type=MESH)

        mine = [pltpu.make_async_copy(src_fns[k](in_refs[k], c), dst_fns[k](out_refs[k], *me), local_sems.at[k])
                for k in range(n)]
        for cp in mine:
            cp.start()
        first = []
        for k in range(n):
            first.append(copy(k, 0, me, sibling, own=True))
            first += [copy(k, 1 + j, me, (*chip, c), own=True) for j, chip in enumerate(chips)]
        for cp in first:
            cp.start()
        passed = []
        for j, chip in enumerate(chips):
            for k in range(n):
                copy(k, 1 + j, (*chip, c), me).wait_recv()
                fwd = copy(k, 4 + j, (*chip, c), sibling)
                fwd.start()
                passed.append(fwd)
        for k in range(n):
            copy(k, 0, sibling, me).wait_recv()
        for j, chip in enumerate(chips):
            for k in range(n):
                copy(k, 4 + j, (*chip, 1 - c), me).wait_recv()
        for cp in first + passed:
            cp.wait_send()
        for cp in mine:
            cp.wait()

    spec = pl.BlockSpec(memory_space=pltpu.VMEM) if in_vmem else ANY
    return pl.pallas_call(
        body, name=name, out_shape=tuple(out_shapes), in_specs=[spec] * n, out_specs=tuple([spec] * n),
        scratch_shapes=[pltpu.SemaphoreType.DMA((n, 7)), pltpu.SemaphoreType.DMA((n, 7)),
                        pltpu.SemaphoreType.DMA((n,))],
        compiler_params=pltpu.CompilerParams(vmem_limit_bytes=VMEM_LIMIT_BYTES),
    )(*ins)


def _allgather_small(payload, name):
    R = payload.shape[0]
    (out,) = _allgather(
        [payload], [jax.ShapeDtypeStruct((N_DEV, R, LANES), F32)],
        [lambda ref, c: ref], [lambda ref, px, py, pc: ref.at[4 * px + 2 * py + pc]], name, in_vmem=True)
    return out


def _swap_sibling(ins, name, other_half=False):
    n = len(ins)

    def body(*refs):
        in_refs, out_refs = refs[:n], refs[n:2 * n]
        send_sems, recv_sems = refs[2 * n:]
        x, y, c = _position()
        copies = [pltpu.make_async_remote_copy(
            src_ref=in_refs[k].at[:, 1 - c] if other_half else in_refs[k], dst_ref=out_refs[k],
            send_sem=send_sems.at[k], recv_sem=recv_sems.at[k],
            device_id=(x, y, 1 - c), device_id_type=MESH) for k in range(n)]
        for cp in copies:
            cp.start()
        for cp in copies:
            cp.wait_recv()
        for cp in copies:
            cp.wait_send()

    shape = lambda a: (a.shape[0],) + a.shape[2:] if other_half else a.shape
    return pl.pallas_call(
        body, name=name, out_shape=tuple(jax.ShapeDtypeStruct(shape(a), a.dtype) for a in ins),
        in_specs=[ANY] * n, out_specs=tuple([ANY] * n),
        scratch_shapes=[pltpu.SemaphoreType.DMA((n,)), pltpu.SemaphoreType.DMA((n,))],
        compiler_params=pltpu.CompilerParams(vmem_limit_bytes=VMEM_LIMIT_BYTES),
    )(*ins)


HBM_SPEC = pl.BlockSpec(memory_space=pltpu.HBM)
SEM_SPEC = pl.BlockSpec(memory_space=pltpu.SEMAPHORE)
SPLIT_PARAMS = pltpu.CompilerParams(has_side_effects=pltpu.SideEffectType.DATAFLOW_SIDE_EFFECTING)
TOKEN = jax.ShapeDtypeStruct((8, LANES), F32)


def _hbm(a):
    return pltpu.with_memory_space_constraint(a, pltpu.HBM)


def _weight_window(ref, col, r, cw, px, py, pc):
    rh = r // 2
    if col:
        return ref.at[pl.ds(pc * rh, rh), pl.ds((2 * px + py) * cw, cw)]
    return ref.at[pl.ds((2 * px + py) * r + pc * rh, rh), :]


def _peers(x, y, c):
    return [(x, y, 1 - c), (1 - x, y, c), (x, 1 - y, c), (1 - x, 1 - y, c)]


def _place_own(shard, col, pos, name):
    r, cw = shard.shape
    rh = r // 2
    tr = _pick(rh, (256, 128, 176, 64))
    nb = rh // tr
    shape = (r, N_CHIPS * cw) if col else (N_CHIPS * r, cw)

    def body(pos_ref, x_ref, o_ref):
        o_ref[...] = x_ref[...]

    if col:
        out_idx = lambda i, pos_ref: (pos_ref[1] * nb + i, pos_ref[0])
    else:
        out_idx = lambda i, pos_ref: (pos_ref[0] * (2 * nb) + pos_ref[1] * nb + i, 0)
    return pl.pallas_call(
        body, name=name, out_shape=jax.ShapeDtypeStruct(shape, shard.dtype),
        grid_spec=pltpu.PrefetchScalarGridSpec(
            num_scalar_prefetch=1, grid=(nb,),
            in_specs=[pl.BlockSpec((tr, cw), lambda i, pos_ref: (pos_ref[1] * nb + i, 0))],
            out_specs=pl.BlockSpec((tr, cw), out_idx)),
        compiler_params=_params(("arbitrary",)),
    )(pos, shard)


def _gather_start(shards, lands, cols, per_layer, name):
    n = len(shards)
    nl = n // per_layer

    def body(*refs):
        sh, ld = refs[:n], refs[n:2 * n]
        sems, token = refs[2 * n:2 * n + 2 * nl], refs[-1]
        x, y, c = _position()
        for k in range(n):
            l, a = divmod(k, per_layer)
            r, cw = shards[k].shape
            src = sh[k].at[pl.ds(c * (r // 2), r // 2), :]
            dst = _weight_window(ld[k], cols[k], r, cw, x, y, c)
            for j, peer in enumerate(_peers(x, y, c)):
                pltpu.make_async_remote_copy(src_ref=src, dst_ref=dst, send_sem=sems[2 * l].at[4 * a + j],
                                             recv_sem=sems[2 * l + 1].at[4 * a + j], device_id=peer,
                                             device_id_type=MESH).start()
        token[...] = jnp.zeros_like(token)

    arrs = list(shards) + list(lands)
    out = pl.pallas_call(
        body, name=name,
        out_shape=tuple(pltpu.SemaphoreType.DMA((per_layer * 4,)) for _ in range(2 * nl))
        + tuple(pltpu.HBM(a.shape, a.dtype) for a in arrs) + (TOKEN,),
        in_specs=[HBM_SPEC] * (2 * n),
        out_specs=(SEM_SPEC,) * (2 * nl) + (HBM_SPEC,) * (2 * n) + (pl.BlockSpec(memory_space=pltpu.VMEM),),
        input_output_aliases={i: 2 * nl + i for i in range(2 * n)}, compiler_params=SPLIT_PARAMS,
    )(*[_hbm(a) for a in arrs])
    return out[:2 * nl], out[2 * nl:2 * nl + n], out[2 * nl + n:2 * nl + 2 * n], out[-1]


def _gather_wait(shards, lands, send, recv, after, cols, first, name):
    m = len(shards)

    def body(*refs):
        sh, ld = refs[:m], refs[m:2 * m]
        send_ref, recv_ref = refs[2 * m], refs[2 * m + 1]
        x, y, c = _position()
        for a in range(m):
            r, cw = shards[a].shape
            src = sh[a].at[pl.ds(c * (r // 2), r // 2), :]
            for j, (px, py, pc) in enumerate(_peers(x, y, c)):
                cp = pltpu.make_async_remote_copy(
                    src_ref=src, dst_ref=_weight_window(ld[a], cols[a], r, cw, px, py, pc),
                    send_sem=send_ref.at[4 * (first + a) + j], recv_sem=recv_ref.at[4 * (first + a) + j],
                    device_id=(px, py, pc),
                    device_id_type=MESH)
                cp.wait_send()
                cp.wait_recv()

    arrs = list(shards) + list(lands)
    out = pl.pallas_call(
        body, name=name, out_shape=tuple(pltpu.HBM(a.shape, a.dtype) for a in arrs),
        in_specs=[HBM_SPEC] * (2 * m) + [SEM_SPEC, SEM_SPEC, ANY], out_specs=(HBM_SPEC,) * (2 * m),
        input_output_aliases={i: i for i in range(2 * m)}, compiler_params=SPLIT_PARAMS,
    )(*arrs, send, recv, after)
    return out[m:]


def _forward_sibling(lands, cols, shard_shapes, name):
    m = len(lands)

    def body(*refs):
        ins, outs = refs[:m], refs[m:2 * m]
        send_sems, recv_sems = refs[2 * m:]
        x, y, c = _position()
        chips = [(1 - x, y), (x, 1 - y), (1 - x, 1 - y)]
        sends = []
        for a in range(m):
            r, cw = shard_shapes[a]
            for j, (px, py) in enumerate(chips):
                cp = pltpu.make_async_remote_copy(
                    src_ref=_weight_window(ins[a], cols[a], r, cw, px, py, c),
                    dst_ref=_weight_window(outs[a], cols[a], r, cw, px, py, c),
                    send_sem=send_sems.at[a, j], recv_sem=recv_sems.at[a, j], device_id=(x, y, 1 - c),
                    device_id_type=MESH)
                cp.start()
                sends.append(cp)
        for a in range(m):
            r, cw = shard_shapes[a]
            for j, (px, py) in enumerate(chips):
                pltpu.make_async_remote_copy(
                    src_ref=_weight_window(ins[a], cols[a], r, cw, px, py, c),
                    dst_ref=_weight_window(outs[a], cols[a], r, cw, px, py, 1 - c),
                    send_sem=send_sems.at[a, j], recv_sem=recv_sems.at[a, j], device_id=(x, y, 1 - c),
                    device_id_type=MESH).wait_recv()
        for cp in sends:
            cp.wait_send()

    return pl.pallas_call(
        body, name=name, out_shape=tuple(jax.ShapeDtypeStruct(a.shape, a.dtype) for a in lands),
        in_specs=[ANY] * m, out_specs=tuple([ANY] * m), input_output_aliases={i: i for i in range(m)},
        scratch_shapes=[pltpu.SemaphoreType.DMA((m, 3)), pltpu.SemaphoreType.DMA((m, 3))],
        compiler_params=pltpu.CompilerParams(vmem_limit_bytes=VMEM_LIMIT_BYTES),
    )(*lands)


def _swap_start(ins, name, other_half=False):
    n = len(ins)
    shape = lambda a: (a.shape[0],) + a.shape[2:] if other_half else a.shape
    lands = [lax.empty(shape(a), a.dtype) for a in ins]

    def body(*refs):
        src, ld = refs[:n], refs[n:2 * n]
        send_ref, recv_ref, token = refs[2 * n], refs[2 * n + 1], refs[-1]
        x, y, c = _position()
        for k in range(n):
            pltpu.make_async_remote_copy(
                src_ref=src[k].at[:, 1 - c] if other_half else src[k], dst_ref=ld[k], send_sem=send_ref.at[k],
                recv_sem=recv_ref.at[k], device_id=(x, y, 1 - c), device_id_type=MESH).start()
        token[...] = jnp.zeros_like(token)

    arrs = list(ins) + lands
    out = pl.pallas_call(
        body, name=name,
        out_shape=(pltpu.SemaphoreType.DMA((n,)), pltpu.SemaphoreType.DMA((n,)))
        + tuple(pltpu.HBM(a.shape, a.dtype) for a in arrs) + (TOKEN,),
        in_specs=[HBM_SPEC] * (2 * n),
        out_specs=(SEM_SPEC, SEM_SPEC) + (HBM_SPEC,) * (2 * n) + (pl.BlockSpec(memory_space=pltpu.VMEM),),
        input_output_aliases={i: 2 + i for i in range(2 * n)}, compiler_params=SPLIT_PARAMS,
    )(*[_hbm(a) for a in arrs])
    return out[0], out[1], out[2:2 + n], out[2 + n:2 + 2 * n], out[-1]


def _swap_wait(ins, lands, send, recv, after, name, other_half=False, first=0):
    n = len(ins)

    def body(*refs):
        src, ld = refs[:n], refs[n:2 * n]
        send_ref, recv_ref = refs[2 * n], refs[2 * n + 1]
        x, y, c = _position()
        for k in range(n):
            cp = pltpu.make_async_remote_copy(
                src_ref=src[k].at[:, 1 - c] if other_half else src[k], dst_ref=ld[k], send_sem=send_ref.at[first + k],
                recv_sem=recv_ref.at[first + k], device_id=(x, y, 1 - c), device_id_type=MESH)
            cp.wait_send()
            cp.wait_recv()

    arrs = list(ins) + list(lands)
    out = pl.pallas_call(
        body, name=name, out_shape=tuple(pltpu.HBM(a.shape, a.dtype) for a in arrs),
        in_specs=[HBM_SPEC] * (2 * n) + [SEM_SPEC, SEM_SPEC, ANY], out_specs=(HBM_SPEC,) * (2 * n),
        input_output_aliases={i: i for i in range(2 * n)}, compiler_params=SPLIT_PARAMS,
    )(*arrs, send, recv, after)
    return out[:n], out[n:]


def _forward_start(lands, cols, shard_shapes, name):
    m = len(lands)

    def body(*refs):
        ld = refs[:m]
        send_ref, recv_ref, token = refs[m], refs[m + 1], refs[-1]
        x, y, c = _position()
        for a in range(m):
            r, cw = shard_shapes[a]
            for j, (px, py) in enumerate([(1 - x, y), (x, 1 - y), (1 - x, 1 - y)]):
                win = _weight_window(ld[a], cols[a], r, cw, px, py, c)
                pltpu.make_async_remote_copy(src_ref=win, dst_ref=win, send_sem=send_ref.at[3 * a + j],
                                             recv_sem=recv_ref.at[3 * a + j], device_id=(x, y, 1 - c),
                                             device_id_type=MESH).start()
        token[...] = jnp.zeros_like(token)

    out = pl.pallas_call(
        body, name=name,
        out_shape=(pltpu.SemaphoreType.DMA((m * 3,)), pltpu.SemaphoreType.DMA((m * 3,)))
        + tuple(pltpu.HBM(a.shape, a.dtype) for a in lands) + (TOKEN,),
        in_specs=[HBM_SPEC] * m,
        out_specs=(SEM_SPEC, SEM_SPEC) + (HBM_SPEC,) * m + (pl.BlockSpec(memory_space=pltpu.VMEM),),
        input_output_aliases={i: 2 + i for i in range(m)}, compiler_params=SPLIT_PARAMS,
    )(*[_hbm(a) for a in lands])
    return out[0], out[1], out[2:2 + m], out[-1]


def _forward_wait(lands, send, recv, after, cols, shard_shapes, name):
    m = len(lands)

    def body(*refs):
        ld = refs[:m]
        send_ref, recv_ref = refs[m], refs[m + 1]
        x, y, c = _position()
        for a in range(m):
            r, cw = shard_shapes[a]
            for j, (px, py) in enumerate([(1 - x, y), (x, 1 - y), (1 - x, 1 - y)]):
                cp = pltpu.make_async_remote_copy(
                    src_ref=_weight_window(ld[a], cols[a], r, cw, px, py, c),
                    dst_ref=_weight_window(ld[a], cols[a], r, cw, px, py, 1 - c),
                    send_sem=send_ref.at[3 * a + j], recv_sem=recv_ref.at[3 * a + j], device_id=(x, y, 1 - c),
                    device_id_type=MESH)
                cp.wait_send()
                cp.wait_recv()

    return pl.pallas_call(
        body, name=name, out_shape=tuple(pltpu.HBM(a.shape, a.dtype) for a in lands),
        in_specs=[HBM_SPEC] * m + [SEM_SPEC, SEM_SPEC, ANY], out_specs=(HBM_SPEC,) * m,
        input_output_aliases={i: i for i in range(m)}, compiler_params=SPLIT_PARAMS,
    )(*lands, send, recv, after)


def _exchange_start(sums, name):
    m = len(sums)
    lands = [lax.empty((3,) + s.shape[1:], s.dtype) for s in sums]

    def body(*refs):
        cs, ld = refs[:m], refs[m:2 * m]
        send_ref, recv_ref, token = refs[2 * m], refs[2 * m + 1], refs[-1]
        x, y, c = _position()
        for a in range(m):
            for j, (px, py) in enumerate([(1 - x, y), (x, 1 - y), (1 - x, 1 - y)]):
                pltpu.make_async_remote_copy(
                    src_ref=cs[a].at[2 * px + py], dst_ref=ld[a].at[j], send_sem=send_ref.at[3 * a + j],
                    recv_sem=recv_ref.at[3 * a + j], device_id=(px, py, c), device_id_type=MESH).start()
        token[...] = jnp.zeros_like(token)

    arrs = list(sums) + lands
    out = pl.pallas_call(
        body, name=name,
        out_shape=(pltpu.SemaphoreType.DMA((m * 3,)), pltpu.SemaphoreType.DMA((m * 3,)))
        + tuple(pltpu.HBM(a.shape, a.dtype) for a in arrs) + (TOKEN,),
        in_specs=[HBM_SPEC] * (2 * m),
        out_specs=(SEM_SPEC, SEM_SPEC) + (HBM_SPEC,) * (2 * m) + (pl.BlockSpec(memory_space=pltpu.VMEM),),
        input_output_aliases={i: 2 + i for i in range(2 * m)}, compiler_params=SPLIT_PARAMS,
    )(*[_hbm(a) for a in arrs])
    return out[0], out[1], out[2:2 + m], out[2 + m:2 + 2 * m], out[-1]


def _exchange_wait(sums, lands, send, recv, after, name):
    m = len(sums)

    def body(*refs):
        cs, ld = refs[:m], refs[m:2 * m]
        send_ref, recv_ref = refs[2 * m], refs[2 * m + 1]
        x, y, c = _position()
        for a in range(m):
            for j, (px, py) in enumerate([(1 - x, y), (x, 1 - y), (1 - x, 1 - y)]):
                cp = pltpu.make_async_remote_copy(
                    src_ref=cs[a].at[2 * px + py], dst_ref=ld[a].at[j], send_sem=send_ref.at[3 * a + j],
                    recv_sem=recv_ref.at[3 * a + j], device_id=(px, py, c), device_id_type=MESH)
                cp.wait_send()
                cp.wait_recv()

    arrs = list(sums) + list(lands)
    out = pl.pallas_call(
        body, name=name, out_shape=tuple(pltpu.HBM(a.shape, a.dtype) for a in arrs),
        in_specs=[HBM_SPEC] * (2 * m) + [SEM_SPEC, SEM_SPEC, ANY], out_specs=(HBM_SPEC,) * (2 * m),
        input_output_aliases={i: i for i in range(2 * m)}, compiler_params=SPLIT_PARAMS,
    )(*arrs, send, recv, after)
    return out[:m], out[m:]


def _place_row(payload, dev, name):
    R = payload.shape[0]
    tr = _pick(R, (512, 448, 384, 256, 192, 128, 64, 32, 16, 8))

    def body(dev_ref, x_ref, o_ref):
        o_ref[...] = x_ref[...]

    return pl.pallas_call(
        body, name=name, out_shape=jax.ShapeDtypeStruct((N_DEV, R, LANES), payload.dtype),
        grid_spec=pltpu.PrefetchScalarGridSpec(
            num_scalar_prefetch=1, grid=(R // tr,),
            in_specs=[pl.BlockSpec((tr, LANES), lambda i, dev_ref: (i, 0))],
            out_specs=pl.BlockSpec((None, tr, LANES), lambda i, dev_ref: (dev_ref[0], i, 0))),
        compiler_params=_params(("arbitrary",)),
    )(dev, payload)


def _others(x, y, c):
    return [(1 - x if fx else x, 1 - y if fy else y, 1 - c if fc else c)
            for fx in (0, 1) for fy in (0, 1) for fc in (0, 1) if fx or fy or fc]


def _broadcast_start(payload, land, name):
    def body(p_ref, l_ref, send_ref, recv_ref, p_thru, l_thru, token):
        x, y, c = _position()
        for j, peer in enumerate(_others(x, y, c)):
            pltpu.make_async_remote_copy(src_ref=p_ref, dst_ref=l_ref.at[4 * x + 2 * y + c], send_sem=send_ref.at[j],
                                         recv_sem=recv_ref.at[j], device_id=peer, device_id_type=MESH).start()
        token[...] = jnp.zeros_like(token)

    n = N_DEV - 1
    return pl.pallas_call(
        body, name=name,
        out_shape=(pltpu.SemaphoreType.DMA((n,)), pltpu.SemaphoreType.DMA((n,)), pltpu.HBM(payload.shape, payload.dtype),
                   pltpu.HBM(land.shape, land.dtype), TOKEN),
        in_specs=[HBM_SPEC, HBM_SPEC],
        out_specs=(SEM_SPEC, SEM_SPEC, HBM_SPEC, HBM_SPEC, pl.BlockSpec(memory_space=pltpu.VMEM)),
        input_output_aliases={0: 2, 1: 3}, compiler_params=SPLIT_PARAMS,
    )(_hbm(payload), _hbm(land))


def _broadcast_wait(payload, land, send, recv, after, name):
    def body(p_ref, l_ref, send_ref, recv_ref, after_ref, p_thru, l_thru):
        x, y, c = _position()
        for j, (px, py, pc) in enumerate(_others(x, y, c)):
            cp = pltpu.make_async_remote_copy(src_ref=p_ref, dst_ref=l_ref.at[4 * px + 2 * py + pc],
                                              send_sem=send_ref.at[j], recv_sem=recv_ref.at[j],
                                              device_id=(px, py, pc), device_id_type=MESH)
            cp.wait_send()
            cp.wait_recv()

    out = pl.pallas_call(
        body, name=name, out_shape=(pltpu.HBM(payload.shape, payload.dtype), pltpu.HBM(land.shape, land.dtype)),
        in_specs=[HBM_SPEC, HBM_SPEC, SEM_SPEC, SEM_SPEC, ANY], out_specs=(HBM_SPEC, HBM_SPEC),
        input_output_aliases={0: 0, 1: 1}, compiler_params=SPLIT_PARAMS,
    )(payload, land, send, recv, after)
    return out[1]


def _vec(a):
    return a.reshape(1, -1)


def _local_step(x, tgt, mod, W, P, get_w=None, on_grads=None):
    Dm = D_MODEL
    G = {k: [] for k in ("gm_w_in", "gm_w_out", "hg_w_in", "hg_w_out", "ffn_w_up", "ffn_w_down")}
    lb_all = _lb_fwd(P["hg_lb"], "lb_fwd")
    saved = []
    xs = x
    y_prev = gate_prev = None
    layer_w = [None] * DEPTH

    def wmm(xa, kind, i, mode, name):
        if layer_w[i] is not None:
            return _mm(xa, layer_w[i][kind], mode, name)
        return _mm(xa, W[kind], mode, name, b_layer=i if kind.startswith("ffn") else i // 2)

    for i in range(DEPTH):
        m = [_vec(mod[i, j * Dm:(j + 1) * Dm]) for j in range(6)]
        sh1, sc1, g1, sh2, sc2, g2 = m
        j = i // 2
        if get_w is not None:
            layer_w[i] = get_w(i, 0, xs if y_prev is None else y_prev)
        xs, h = _norm_fwd(xs, y_prev, gate_prev, _vec(P["norm_g"][i, 0]), sc1, sh1, f"norm_fwd_a{i}")
        rec = dict(x1=xs, h1=h)
        if i % 2 == 0:
            a = wmm(h, "gm_w_in", i, "nn", f"gm_in{i}")
            gated = _gm_mid_fwd(a, _vec(P["gm_ln_g"][j]), _vec(P["gm_ln_b"][j]), P["gm_w_s"][j],
                                P["gm_b_s"][j].reshape(GM_HEADS, GM_BLOCK, 1), f"gm_mid_fwd{i}")
            if get_w is not None:
                layer_w[i].update(get_w(i, 1, gated))
            y1 = wmm(gated, "gm_w_out", i, "nn", f"gm_out{i}")
            rec.update(a=a, act=gated)
        else:
            p = wmm(h, "hg_w_in", i, "nn", f"hg_in{i}")
            o, og, states = _hg_scan_fwd(p, _vec(lb_all[j]), _vec(P["hg_gn_g"][j]), f"hg_scan_fwd{i}")
            if get_w is not None:
                layer_w[i].update(get_w(i, 1, og))
            y1 = wmm(og, "hg_w_out", i, "nn", f"hg_out{i}")
            rec.update(a=p, act=og, o=o, states=states)
        rec["y1"] = y1
        xs, h2 = _norm_fwd(xs, y1, g1, _vec(P["norm_g"][i, 1]), sc2, sh2, f"norm_fwd_b{i}")
        a2 = wmm(h2, "ffn_w_up", i, "nn", f"ffn_up{i}")
        mm_ = _conv_fwd(a2, P["ffn_conv_w"][i], _vec(P["ffn_conv_b"][i]), f"conv_fwd{i}")
        y2 = wmm(mm_, "ffn_w_down", i, "nn", f"ffn_down{i}")
        rec.update(x2=xs, h2=h2, a2=a2, m=mm_, y2=y2, mods=m)
        saved.append(rec)
        y_prev, gate_prev = y2, g2
    dx, dy, loss, s_fg, s_gate = _loss_head(xs, y_prev, gate_prev, _vec(P["final_g"]), tgt, "loss_head")
    small = dict(final_g=s_fg, norm_g=[None] * DEPTH, dmod=[None] * DEPTH, ffn_conv_w=[None] * DEPTH,
                 ffn_conv_b=[None] * DEPTH, gm_ln_g=[None] * 2, gm_ln_b=[None] * 2, gm_w_s=[None] * 2,
                 gm_b_s=[None] * 2, hg_gn_g=[None] * 2, dlb=[None] * 2)
    for i in reversed(range(DEPTH)):
        rec = saved[i]
        sh1, sc1, g1, sh2, sc2, g2 = rec["mods"]
        j = i // 2
        d_g2 = s_gate
        dm = wmm(dy, "ffn_w_down", i, "nt", f"ffn_down_dx{i}")
        G["ffn_w_down"].append(_mm(rec["m"], dy, "tn", f"ffn_down_dw{i}", out_dtype=GRAD_WIRE))
        da2, dcw, dcb = _conv_bwd(rec["a2"], dm, P["ffn_conv_w"][i], _vec(P["ffn_conv_b"][i]), f"conv_bwd{i}")
        small["ffn_conv_w"][i], small["ffn_conv_b"][i] = dcw, dcb
        dh2 = wmm(da2, "ffn_w_up", i, "nt", f"ffn_up_dx{i}")
        G["ffn_w_up"].append(_mm(rec["h2"], da2, "tn", f"ffn_up_dw{i}", out_dtype=GRAD_WIRE, exchange_out=True))
        ng2 = _vec(P["norm_g"][i, 1])
        if on_grads is not None:
            ng2 = ng2 + on_grads(i, {k: G[k][-1] for k in ("ffn_w_up", "ffn_w_down")})
        dx, dy, s_sh2, s_x2, d_g1 = _norm_bwd(rec["x2"], dh2, dx, ng2, sc2, rec["y1"], g1, f"norm_bwd_b{i}")
        d_sc2, d_ng2 = s_x2 * ng2, s_x2 * (1.0 + sc2)
        if i % 2 == 0:
            dgated = wmm(dy, "gm_w_out", i, "nt", f"gm_out_dx{i}")
            G["gm_w_out"].append(_mm(rec["act"], dy, "tn", f"gm_out_dw{i}", out_dtype=GRAD_WIRE))
            da, dws, dbs, dlg, dlbeta = _gm_mid_bwd(
                rec["a"], dgated, _vec(P["gm_ln_g"][j]), _vec(P["gm_ln_b"][j]), P["gm_w_s"][j],
                P["gm_b_s"][j].reshape(GM_HEADS, GM_BLOCK, 1), f"gm_mid_bwd{i}")
            small["gm_w_s"][j], small["gm_b_s"][j] = dws, dbs[:, :GM_HEADS].T
            small["gm_ln_g"][j], small["gm_ln_b"][j] = dlg, dlbeta
            dh1 = wmm(da, "gm_w_in", i, "nt", f"gm_in_dx{i}")
            G["gm_w_in"].append(_mm(rec["h1"], da, "tn", f"gm_in_dw{i}", out_dtype=GRAD_WIRE, exchange_out=True))
        else:
            dog = wmm(dy, "hg_w_out", i, "nt", f"hg_out_dx{i}")
            G["hg_w_out"].append(_mm(rec["act"], dy, "tn", f"hg_out_dw{i}", out_dtype=GRAD_WIRE))
            dp, dlb, dgn = _hg_scan_bwd(rec["a"], _vec(lb_all[j]), _vec(P["hg_gn_g"][j]), rec["o"], dog,
                                        rec["states"], f"hg_scan_bwd{i}")
            small["dlb"][j], small["hg_gn_g"][j] = dlb, dgn
            dh1 = wmm(dp, "hg_w_in", i, "nt", f"hg_in_dx{i}")
            G["hg_w_in"].append(_mm(rec["h1"], dp, "tn", f"hg_in_dw{i}", out_dtype=GRAD_WIRE, exchange_out=True))
        ng1 = _vec(P["norm_g"][i, 0])
        if on_grads is not None:
            mixer = ("gm_w_in", "gm_w_out") if i % 2 == 0 else ("hg_w_in", "hg_w_out")
            ng1 = ng1 + on_grads(i, {k: G[k][-1] for k in mixer})
        if i > 0:
            prev = saved[i - 1]
            dx, dy, s_sh1, s_x1, s_gate = _norm_bwd(rec["x1"], dh1, dx, ng1, sc1, prev["y2"], prev["mods"][5],
                                                    f"norm_bwd_a{i}")
        else:
            dx, s_sh1, s_x1 = _norm_bwd(rec["x1"], dh1, dx, ng1, sc1, None, None, f"norm_bwd_a{i}")
        d_sc1, d_ng1 = s_x1 * ng1, s_x1 * (1.0 + sc1)
        small["norm_g"][i] = jnp.concatenate([d_ng1, d_ng2], axis=0)
        small["dmod"][i] = jnp.concatenate([s_sh1, d_sc1, d_g1, s_sh2, d_sc2, d_g2], axis=1)
    for k in G:
        G[k] = G[k][::-1]
    dlb_all = jnp.concatenate(small.pop("dlb"), axis=0)
    small["hg_lb"] = _lb_bwd(P["hg_lb"], dlb_all, "lb_bwd")
    return loss, dx, G, small


BIG = ("gm_w_in", "gm_w_out", "hg_w_in", "hg_w_out", "ffn_w_up", "ffn_w_down")
COL_SHARDED = dict(gm_w_in=True, gm_w_out=False, hg_w_in=True, hg_w_out=False, ffn_w_up=True, ffn_w_down=False)
LAYER_WEIGHTS = 4


def _layer_kinds(i):
    return (("gm_w_in", "gm_w_out") if i % 2 == 0 else ("hg_w_in", "hg_w_out")) + ("ffn_w_up", "ffn_w_down")


def _pack(pieces):
    flat = [p.reshape(-1).astype(F32) for p in pieces]
    offs, tot = [], 0
    for f in flat:
        offs.append((tot, f.shape[0]))
        tot += f.shape[0]
    padded = -(-tot // (8 * LANES)) * (8 * LANES)
    if padded > tot:
        flat.append(jnp.zeros((padded - tot,), F32))
    return jnp.concatenate(flat).reshape(-1, LANES), offs


def _unpack(rows, offs, shapes):
    lead = rows.shape[:-2]
    flat = rows.reshape(lead + (-1,))
    return [flat[..., o:o + n].reshape(lead + tuple(s)) for (o, n), s in zip(offs, shapes)]


def _from_chips(per_dev, axis):
    per_chip = per_dev[0::2]
    return jnp.concatenate([per_chip[s] for s in range(N_CHIPS)], axis=axis)


def kernel(x, c, gm_w_in, gm_ln_g, gm_ln_b, gm_w_s, gm_b_s, gm_w_out, hg_w_in, hg_lb, hg_gn_g, hg_w_out, ffn_w_up, ffn_conv_w, ffn_conv_b, ffn_w_down, norm_g, ada_w, ada_b, final_g, loss_target, m_gm_w_in, m_gm_ln_g, m_gm_ln_b, m_gm_w_s, m_gm_b_s, m_gm_w_out, m_hg_w_in, m_hg_lb, m_hg_gn_g, m_hg_w_out, m_ffn_w_up, m_ffn_conv_w, m_ffn_conv_b, m_ffn_w_down, m_norm_g, m_ada_w, m_ada_b, m_final_g, v_gm_w_in, v_gm_ln_g, v_gm_ln_b, v_gm_w_s, v_gm_b_s, v_gm_w_out, v_hg_w_in, v_hg_lb, v_hg_gn_g, v_hg_w_out, v_ffn_w_up, v_ffn_conv_w, v_ffn_conv_b, v_ffn_w_down, v_norm_g, v_ada_w, v_ada_b, v_final_g):
    Dm = D_MODEL
    xi, yi, ci = _position()
    chip = 2 * xi + yi
    dev = 4 * xi + 2 * yi + ci
    weights = dict(gm_w_in=gm_w_in, gm_ln_g=gm_ln_g, gm_ln_b=gm_ln_b, gm_w_s=gm_w_s, gm_b_s=gm_b_s,
                   gm_w_out=gm_w_out, hg_w_in=hg_w_in, hg_lb=hg_lb, hg_gn_g=hg_gn_g, hg_w_out=hg_w_out,
                   ffn_w_up=ffn_w_up, ffn_conv_w=ffn_conv_w, ffn_conv_b=ffn_conv_b, ffn_w_down=ffn_w_down,
                   norm_g=norm_g, ada_w=ada_w, ada_b=ada_b, final_g=final_g)
    mom_m = dict(gm_w_in=m_gm_w_in, gm_ln_g=m_gm_ln_g, gm_ln_b=m_gm_ln_b, gm_w_s=m_gm_w_s, gm_b_s=m_gm_b_s,
                 gm_w_out=m_gm_w_out, hg_w_in=m_hg_w_in, hg_lb=m_hg_lb, hg_gn_g=m_hg_gn_g, hg_w_out=m_hg_w_out,
                 ffn_w_up=m_ffn_w_up, ffn_conv_w=m_ffn_conv_w, ffn_conv_b=m_ffn_conv_b, ffn_w_down=m_ffn_w_down,
                 norm_g=m_norm_g, ada_w=m_ada_w, ada_b=m_ada_b, final_g=m_final_g)
    mom_v = dict(gm_w_in=v_gm_w_in, gm_ln_g=v_gm_ln_g, gm_ln_b=v_gm_ln_b, gm_w_s=v_gm_w_s, gm_b_s=v_gm_b_s,
                 gm_w_out=v_gm_w_out, hg_w_in=v_hg_w_in, hg_lb=v_hg_lb, hg_gn_g=v_hg_gn_g, hg_w_out=v_hg_w_out,
                 ffn_w_up=v_ffn_w_up, ffn_conv_w=v_ffn_conv_w, ffn_conv_b=v_ffn_conv_b, ffn_w_down=v_ffn_w_down,
                 norm_g=v_norm_g, ada_w=v_ada_w, ada_b=v_ada_b, final_g=v_final_g)
    order = list(weights)

    pos = jnp.stack([chip, ci]).astype(jnp.int32)
    shards, by_col = [], []
    for i in range(DEPTH):
        for k in _layer_kinds(i):
            shards.append(weights[k][i if k.startswith("ffn") else i // 2].astype(BF16))
            by_col.append(COL_SHARDED[k])
    placed = [_place_own(sh, col, pos, f"place_own{n}") for n, (sh, col) in enumerate(zip(shards, by_col))]
    gsems, sh_thru, ld_thru, _ = _gather_start(shards, placed, by_col, LAYER_WEIGHTS, "gather_start")

    pieces = [c, hg_lb, hg_gn_g, norm_g, ffn_conv_w]
    payload, offs = _pack(pieces)
    got = _allgather_small(payload, "gather_small")
    c_g, lb_g, gn_g, ng_g, cw_g = _unpack(got, offs, [p.shape for p in pieces])
    c_all = c_g.reshape(N_DEV, Dm)
    P = dict(hg_lb=_from_chips(lb_g, 1), hg_gn_g=_from_chips(gn_g, 1), norm_g=_from_chips(ng_g, 2),
             ffn_conv_w=_from_chips(cw_g, 2), gm_ln_g=gm_ln_g, gm_ln_b=gm_ln_b, gm_w_s=gm_w_s, gm_b_s=gm_b_s,
             ffn_conv_b=ffn_conv_b, final_g=final_g)

    cols = ada_w.shape[2]
    ada_b_sh = lax.dynamic_slice_in_dim(ada_b, chip * cols, cols, axis=1)
    mod_sh = _ada_fwd(c_all, ada_w, ada_b_sh, "ada_fwd")
    mod_g = _allgather_small(mod_sh.reshape(-1, LANES), "gather_mod").reshape(N_DEV, DEPTH, N_DEV, cols)
    mod_mine = lax.dynamic_index_in_dim(mod_g[0::2], dev, axis=2, keepdims=False)
    mod = jnp.transpose(mod_mine, (1, 0, 2)).reshape(DEPTH, N_CHIPS * cols)

    core = jnp.reshape(ci, (1,)).astype(jnp.int32)
    chip_arr = jnp.reshape(chip, (1,)).astype(jnp.int32)
    pending, held, prefetched, swapping = [], {}, {}, []

    def get_w(i, group, after):
        lo, hi = LAYER_WEIGHTS * i, LAYER_WEIGHTS * (i + 1)
        shapes = lambda s: [a.shape for a in shards[s]]
        out = {}
        if i == 0:
            s = slice(lo, lo + 1) if group == 0 else slice(lo + 1, hi)
            landed = _gather_wait(sh_thru[s], ld_thru[s], gsems[0], gsems[1], after, by_col[s], s.start - lo,
                                  f"gather_wait0_{group}")
            full = _forward_sibling(landed, by_col[s], shapes(s), f"gather_forward0_{group}")
            out = dict(zip(_layer_kinds(0)[s.start - lo:s.stop - lo], full))
        elif group == 0:
            s = slice(lo, hi)
            send, recv, lands = prefetched.pop(i)
            full = _forward_wait(lands, send, recv, after, by_col[s], shapes(s), f"gather_forward_wait{i}")
            out = dict(zip(_layer_kinds(i), full))
        if group == 1 and i + 1 < DEPTH:
            s = slice(hi, hi + LAYER_WEIGHTS)
            landed = _gather_wait(sh_thru[s], ld_thru[s], gsems[2 * i + 2], gsems[2 * i + 3], after, by_col[s], 0,
                                  f"gather_wait{i + 1}")
            send, recv, lands, _ = _forward_start(landed, by_col[s], shapes(s), f"gather_forward_start{i + 1}")
            prefetched[i + 1] = (send, recv, lands)
        return out

    def on_grads(i, gdict):
        if i > 0 and "ffn_w_up" in gdict:
            held[i] = gdict
            return 0.0
        gdict = {**held.pop(i, {}), **gdict}
        kinds = [k for k in _layer_kinds(i) if k in gdict]
        tag = f"{i}_ffn" if kinds[0] == "ffn_w_up" else f"{i}"
        g4 = []
        for k in kinds:
            g = gdict[k]
            if not COL_SHARDED[k]:
                R, C = g.shape
                g = g.reshape(N_CHIPS, 2, R // (2 * N_CHIPS), C)
            g4.append(g)
        token = finish_swap(g4[0]) if swapping else 0.0
        if tag == "0":
            from_sib = _swap_sibling(g4, f"reduce_swap{tag}", other_half=True)
            return token + start_exchange(tag, i, kinds, g4, from_sib)
        send, recv, g_thru, lands, tok = _swap_start(g4, f"reduce_swap_start{tag}", other_half=True)
        swapping.append((tag, i, kinds, send, recv, g_thru, lands))
        return token + tok[0, 0]

    def start_exchange(tag, i, kinds, g4, from_sib):
        sums = [_add_own_half(g, r, core, f"chip_sum_{k}{i}") for g, r, k in zip(g4, from_sib, kinds)]
        send, recv, sums_thru, lands, token = _exchange_start(sums, f"reduce_start{tag}")
        pending.append((tag, i, kinds, send, recv, sums_thru, lands))
        return token[0, 0]

    def finish_swap(after):
        tag, i, kinds, send, recv, g_thru, lands = swapping.pop()
        g4, from_sib = _swap_wait(g_thru, lands, send, recv, after, f"reduce_swap_wait{tag}", other_half=True)
        return start_exchange(tag, i, kinds, g4, from_sib)

    loss_part, dx, G, small = _local_step(x[0], loss_target[0], mod, None, P, get_w, on_grads)

    sum_pieces = [loss_part[:, :1], small["final_g"], jnp.stack(small["gm_ln_g"]), jnp.stack(small["gm_ln_b"]),
                  jnp.stack(small["gm_w_s"]), jnp.stack(small["gm_b_s"]), jnp.stack(small["ffn_conv_b"]),
                  small["hg_lb"], jnp.stack(small["hg_gn_g"]), jnp.stack(small["norm_g"]),
                  jnp.stack(small["ffn_conv_w"])]
    dmod = jnp.concatenate(small["dmod"], axis=0)
    payload2, offs2 = _pack(sum_pieces + [dmod])
    placed2 = _place_row(payload2, jnp.reshape(dev, (1,)).astype(jnp.int32), "place_grads")
    bsend, brecv, p2_thru, l2_thru, small_token = _broadcast_start(payload2, placed2, "gather_grads_start")

    landed = {}
    for tag, i, kinds, send, recv, sums_thru, lands in pending:
        sums_i, lands_i = _exchange_wait(sums_thru, lands, send, recv, small_token, f"reduce_wait{tag}")
        for k, s_, l_ in zip(kinds, sums_i, lands_i):
            landed[(k, i)] = (l_, s_)
    own_halves = []
    for k in BIG:
        layers = [landed[(k, i)] for i in range(DEPTH) if (k, i) in landed]
        own_halves.append(_sum_chips([l_ for l_, _ in layers], [s_ for _, s_ in layers], chip_arr, f"sum_chips_{k}"))
    jsend, jrecv, own_thru, jlands, after = _swap_start(own_halves, "reduce_join_start")
    grads, deltas, new_m, new_v = {}, {}, {}, {}
    for n, k in enumerate(BIG):
        (own,), (recv,) = _swap_wait([own_thru[n]], [jlands[n]], jsend, jrecv, after, f"reduce_join_wait_{k}",
                                     first=n)
        grads[k], deltas[k], new_m[k], new_v[k] = _adamw_halves(
            weights[k], own, recv, mom_m[k], mom_v[k], core, f"adamw_{k}")
        after = new_v[k]

    got2 = _broadcast_wait(p2_thru, l2_thru, bsend, brecv, new_v[BIG[-1]], "gather_grads_wait")
    dmod_all = _unpack(got2, offs2[-1:], [dmod.shape])[0]
    summed = _sum_devices(got2, "sum_devices")
    (loss_s, d_final_g, d_ln_g, d_ln_b, d_ws, d_bs, d_cb, d_lb, d_gn, d_ng, d_cw) = _unpack(
        summed, offs2[:-1], [(1,), final_g.shape, gm_ln_g.shape, gm_ln_b.shape, gm_w_s.shape, gm_b_s.shape,
                             ffn_conv_b.shape, (2, Dm), (2, Dm), (DEPTH, 2, Dm), (DEPTH, 3, 2 * FFN_HIDDEN)])
    grads.update(final_g=d_final_g, gm_ln_g=d_ln_g, gm_ln_b=d_ln_b, gm_w_s=d_ws, gm_b_s=d_bs, ffn_conv_b=d_cb)
    grads["hg_lb"] = lax.dynamic_slice_in_dim(d_lb, chip * hg_lb.shape[1], hg_lb.shape[1], axis=1)
    grads["hg_gn_g"] = lax.dynamic_slice_in_dim(d_gn, chip * hg_gn_g.shape[1], hg_gn_g.shape[1], axis=1)
    grads["norm_g"] = lax.dynamic_slice_in_dim(d_ng, chip * norm_g.shape[2], norm_g.shape[2], axis=2)
    grads["ffn_conv_w"] = lax.dynamic_slice_in_dim(d_cw, chip * ffn_conv_w.shape[2], ffn_conv_w.shape[2], axis=2)
    dmod_sh = lax.dynamic_slice_in_dim(dmod_all, chip * cols, cols, axis=2)
    grads["ada_w"] = _ada_bwd(c_all, jnp.transpose(dmod_sh, (1, 0, 2)), "ada_bwd")
    grads["ada_b"] = _sum_devices(dmod_all.reshape(N_DEV, -1, LANES), "sum_ada_b").reshape(ada_b.shape)

    for k in order:
        if k in BIG:
            continue
        w = weights[k]
        shp = w.shape
        view = (-1, shp[-1]) if w.ndim > 1 else (8, -1)
        d, m2, v2 = _adamw(w.reshape(view), grads[k].reshape(view), mom_m[k].reshape(view), mom_v[k].reshape(view),
                           f"adamw_{k}")
        deltas[k], new_m[k], new_v[k] = d.reshape(shp), m2.reshape(shp), v2.reshape(shp)
        grads[k] = grads[k].reshape(shp)

    loss = loss_s.reshape(())
    return (loss, dx[None], *[grads[k] for k in order], *[deltas[k] for k in order],
            *[new_m[k] for k in order], *[new_v[k] for k in order])
```

```python
import functools

import jax
import jax.numpy as jnp
from jax import lax
from jax.experimental import pallas as pl
from jax.experimental.pallas import tpu as pltpu

F32 = jnp.float32
BF16 = jnp.bfloat16
HI = lax.Precision.HIGHEST
X3 = lax.Precision.HIGH
GRAD_WIRE = BF16
MESH = pl.DeviceIdType.MESH

D_MODEL = 1024
DEPTH = 4
EPS = 1e-6
GM_WIDTH = 2048
GM_HEADS = 8
GM_BLOCK = 128
GM_HEAD_DIM = 256
CHUNK = 64
HG_HEADS = 8
HG_DIM = 128
FFN_HIDDEN = 2816
N_CHIPS = 4
N_DEV = 8

ADAM_LR = 0.001
ADAM_B1 = 0.9
ADAM_B2 = 0.999
ADAM_EPS = 1e-08
ADAM_WD = 0.01
ADAM_STEP = 10

VMEM_LIMIT_BYTES = 56 * 1024 * 1024
ROW_TILE = 256
NORM_ROW_TILE = 512
LANES = 128

_SQRT_HALF = 0.7071067811865476
_INV_SQRT_2PI = 0.3989422804014327


def _pick(dim, prefs):
    for p in prefs:
        if dim % p == 0:
            return p
    return dim


def _params(sem):
    return pltpu.CompilerParams(dimension_semantics=sem, vmem_limit_bytes=VMEM_LIMIT_BYTES)


def _cdf(x):
    return 0.5 * (1.0 + lax.erf(x * _SQRT_HALF))


def _pdf(x):
    return jnp.exp(-0.5 * x * x) * _INV_SQRT_2PI


def _sig(x):
    return jax.nn.sigmoid(x)


def _dot(a, b, dims, prec=None):
    return lax.dot_general(a, b, (dims, ((), ())), precision=prec, preferred_element_type=F32)


NN = ((1,), (0,))
NT = ((1,), (1,))
TN = ((0,), (0,))


MM_VMEM_BUDGET = 40 * 1024 * 1024


def _mm_tiles(mode, M, N, K, a_bytes, b_bytes, exchange_out):
    tn = _pick(N, (1408, 1024, 512, 256, 128))
    tms = [t for t in (1408, 1024, 512, 256, 128) if M % t == 0 and not (exchange_out and (M // 2) % t)] or [M]
    tks = [K] + [t for t in (2816, 2048, 1408, 1024, 512, 256, 128) if t < K and K % t == 0]

    def fits(tm, tk):
        acc = tm * tn * 4 if tk < K else 0
        return 2 * tm * tk * a_bytes + 2 * tk * tn * b_bytes + 2 * tm * tn * 4 + acc <= MM_VMEM_BUDGET

    for min_tm in (min(512, tms[0]), 0):
        for tk in tks:
            for tm in tms:
                if tm >= min_tm and fits(tm, tk):
                    return tm, tn, tk
    return tms[-1], tn, tks[-1]


def _mm(a, b, mode, name, b_layer=None, out_dtype=F32, exchange_out=False):
    b2 = b.shape[-2:]
    if mode == "nn":
        (M, K), (_, N) = a.shape, b2
    elif mode == "nt":
        (M, K), (N, _) = a.shape, b2
    else:
        (K, M), (_, N) = a.shape, b2
    tm, tn, tk = _mm_tiles(mode, M, N, K, a.dtype.itemsize, b.dtype.itemsize, exchange_out)
    nk = K // tk
    dims = {"nn": NN, "nt": NT, "tn": TN}[mode]

    def body(a_ref, b_ref, o_ref, *scratch):
        part = _dot(a_ref[...].astype(BF16), b_ref[...].astype(BF16), dims)
        if nk == 1:
            o_ref[...] = part.astype(o_ref.dtype)
            return
        (acc_ref,) = scratch
        k = pl.program_id(2)

        @pl.when(k == 0)
        def _():
            acc_ref[...] = part

        @pl.when(k > 0)
        def _():
            acc_ref[...] += part

        @pl.when(k == nk - 1)
        def _():
            o_ref[...] = acc_ref[...].astype(o_ref.dtype)

    if mode == "tn":
        a_spec = pl.BlockSpec((tk, tm), lambda i, j, k: (k, i))
    else:
        a_spec = pl.BlockSpec((tm, tk), lambda i, j, k: (i, k))
    bblk = (tk, tn) if mode in ("nn", "tn") else (tn, tk)
    bidx = (lambda i, j, k: (k, j)) if mode in ("nn", "tn") else (lambda i, j, k: (j, k))
    if b_layer is None:
        b_spec = pl.BlockSpec(bblk, bidx)
    else:
        b_spec = pl.BlockSpec((None,) + bblk, lambda i, j, k: (b_layer,) + bidx(i, j, k))
    if exchange_out:
        mh, cw = M // 2, N // N_CHIPS
        assert mh % tm == 0 and cw % tn == 0
        out_shape = jax.ShapeDtypeStruct((N_CHIPS, 2, mh, cw), out_dtype)
        o_spec = pl.BlockSpec(
            (None, None, tm, tn),
            lambda i, j, k: (j // (cw // tn), i // (mh // tm), i % (mh // tm), j % (cw // tn)))
    else:
        out_shape = jax.ShapeDtypeStruct((M, N), out_dtype)
        o_spec = pl.BlockSpec((tm, tn), lambda i, j, k: (i, j))
    return pl.pallas_call(
        body, name=name, out_shape=out_shape, grid=(M // tm, N // tn, nk),
        in_specs=[a_spec, b_spec], out_specs=o_spec,
        scratch_shapes=[] if nk == 1 else [pltpu.VMEM((tm, tn), F32)],
        compiler_params=_params(("parallel", "parallel", "arbitrary")),
    )(a, b)


def _row_spec(tr, width):
    return pl.BlockSpec((tr, width), lambda i: (i, 0))


def _vec_spec(width, rows=1):
    return pl.BlockSpec((rows, width), lambda i: (0, 0))


def _norm_fwd(x, y, gate, g, sc, sh, name):
    T, Dm = x.shape
    tr = _pick(T, (NORM_ROW_TILE, ROW_TILE))
    has_res = y is not None

    def body(*refs):
        if has_res:
            x_ref, y_ref, gate_ref, g_ref, sc_ref, sh_ref, xo_ref, h_ref = refs
            xv = x_ref[...] + gate_ref[...] * y_ref[...]
            xo_ref[...] = xv
        else:
            x_ref, g_ref, sc_ref, sh_ref, h_ref = refs
            xv = x_ref[...]
        rstd = lax.rsqrt(jnp.mean(xv * xv, axis=-1, keepdims=True) + EPS)
        h_ref[...] = ((xv * rstd * g_ref[...]) * (1.0 + sc_ref[...]) + sh_ref[...]).astype(BF16)

    row, vec = _row_spec(tr, Dm), _vec_spec(Dm)
    if has_res:
        ins, in_specs = (x, y, gate, g, sc, sh), [row, row, vec, vec, vec, vec]
        out_shape = (jax.ShapeDtypeStruct((T, Dm), F32), jax.ShapeDtypeStruct((T, Dm), BF16))
        out_specs = (row, row)
    else:
        ins, in_specs = (x, g, sc, sh), [row, vec, vec, vec]
        out_shape = jax.ShapeDtypeStruct((T, Dm), BF16)
        out_specs = row
    out = pl.pallas_call(body, name=name, out_shape=out_shape, grid=(T // tr,), in_specs=in_specs,
                         out_specs=out_specs, compiler_params=_params(("parallel",)))(*ins)
    return out if has_res else (x, out)


def _norm_bwd(x, dh, dxo, g, sc, y_prev, gate_prev, name):
    T, Dm = x.shape
    tr = _pick(T, (NORM_ROW_TILE, ROW_TILE))
    has_prev = y_prev is not None

    def body(*refs):
        if has_prev:
            x_ref, dh_ref, dxo_ref, g_ref, sc_ref, yp_ref, gp_ref, dx_ref, dyp_ref, s1_ref, s2_ref, s3_ref = refs
        else:
            x_ref, dh_ref, dxo_ref, g_ref, sc_ref, dx_ref, s1_ref, s2_ref = refs

        @pl.when(pl.program_id(0) == 0)
        def _():
            s1_ref[...] = jnp.zeros_like(s1_ref)
            s2_ref[...] = jnp.zeros_like(s2_ref)
            if has_prev:
                s3_ref[...] = jnp.zeros_like(s3_ref)

        xv = x_ref[...]
        rstd = lax.rsqrt(jnp.mean(xv * xv, axis=-1, keepdims=True) + EPS)
        xhat = xv * rstd
        dh = dh_ref[...]
        dxhat = dh * (g_ref[...] * (1.0 + sc_ref[...]))
        dx = dxo_ref[...] + rstd * (dxhat - xhat * jnp.mean(dxhat * xhat, axis=-1, keepdims=True))
        dx_ref[...] = dx
        s1_ref[...] += jnp.sum(dh, axis=0, keepdims=True)
        s2_ref[...] += jnp.sum(dh * xhat, axis=0, keepdims=True)
        if has_prev:
            dyp_ref[...] = (gp_ref[...] * dx).astype(BF16)
            s3_ref[...] += jnp.sum(dx * yp_ref[...], axis=0, keepdims=True)

    row, vec = _row_spec(tr, Dm), _vec_spec(Dm)
    vshape = jax.ShapeDtypeStruct((1, Dm), F32)
    if has_prev:
        ins, in_specs = (x, dh, dxo, g, sc, y_prev, gate_prev), [row, row, row, vec, vec, row, vec]
        out_shape = (jax.ShapeDtypeStruct((T, Dm), F32), jax.ShapeDtypeStruct((T, Dm), BF16), vshape, vshape, vshape)
        out_specs = (row, row, vec, vec, vec)
    else:
        ins, in_specs = (x, dh, dxo, g, sc), [row, row, row, vec, vec]
        out_shape = (jax.ShapeDtypeStruct((T, Dm), F32), vshape, vshape)
        out_specs = (row, vec, vec)
    return pl.pallas_call(body, name=name, out_shape=out_shape, grid=(T // tr,), in_specs=in_specs,
                          out_specs=out_specs, compiler_params=_params(("arbitrary",)))(*ins)


def _loss_head(x, y, gate, fg, tgt, name):
    T, Dm = x.shape
    tr = _pick(T, (NORM_ROW_TILE, ROW_TILE))
    nsteps = T // tr

    def body(x_ref, y_ref, gate_ref, fg_ref, t_ref, dx_ref, dy_ref, loss_ref, sfg_ref, sg_ref, acc_ref):
        i = pl.program_id(0)

        @pl.when(i == 0)
        def _():
            acc_ref[...] = jnp.zeros_like(acc_ref)
            sfg_ref[...] = jnp.zeros_like(sfg_ref)
            sg_ref[...] = jnp.zeros_like(sg_ref)

        yv = y_ref[...]
        xv = x_ref[...] + gate_ref[...] * yv
        rstd = lax.rsqrt(jnp.mean(xv * xv, axis=-1, keepdims=True) + EPS)
        xhat = xv * rstd
        err = xhat * fg_ref[...] - t_ref[...]
        acc_ref[...] += jnp.sum(err * err, axis=0, keepdims=True)
        dyn = err * (1.0 / Dm)
        sfg_ref[...] += jnp.sum(dyn * xhat, axis=0, keepdims=True)
        dxhat = dyn * fg_ref[...]
        dx = rstd * (dxhat - xhat * jnp.mean(dxhat * xhat, axis=-1, keepdims=True))
        dx_ref[...] = dx
        dy_ref[...] = (gate_ref[...] * dx).astype(BF16)
        sg_ref[...] += jnp.sum(dx * yv, axis=0, keepdims=True)

        @pl.when(i == nsteps - 1)
        def _():
            total = jnp.sum(acc_ref[...], axis=1, keepdims=True) * (0.5 / Dm)
            loss_ref[...] = jnp.broadcast_to(total, loss_ref.shape)

    row, vec = _row_spec(tr, Dm), _vec_spec(Dm)
    vshape = jax.ShapeDtypeStruct((1, Dm), F32)
    return pl.pallas_call(
        body, name=name, grid=(nsteps,),
        out_shape=(jax.ShapeDtypeStruct((T, Dm), F32), jax.ShapeDtypeStruct((T, Dm), BF16),
                   jax.ShapeDtypeStruct((1, LANES), F32), vshape, vshape),
        in_specs=[row, row, vec, vec, row], out_specs=(row, row, _vec_spec(LANES), vec, vec),
        scratch_shapes=[pltpu.VMEM((1, Dm), F32)], compiler_params=_params(("arbitrary",)),
    )(x, y, gate, fg, tgt)


def _spatial_mask():
    r = lax.broadcasted_iota(jnp.int32, (GM_BLOCK, GM_BLOCK), 0) // CHUNK
    c = lax.broadcasted_iota(jnp.int32, (GM_BLOCK, GM_BLOCK), 1) // CHUNK
    return r >= c


def _gm_specs(tr):
    return [_row_spec(tr, 2 * GM_WIDTH), _vec_spec(GM_WIDTH), _vec_spec(GM_WIDTH),
            pl.BlockSpec((GM_HEADS, GM_BLOCK, GM_BLOCK), lambda i: (0, 0, 0)),
            pl.BlockSpec((GM_HEADS, GM_BLOCK, 1), lambda i: (0, 0, 0))]


def _gm_mid_fwd(a, ln_g, ln_b, ws, bs3, name):
    T = a.shape[0]
    tr = _pick(T, (ROW_TILE,))
    W = GM_WIDTH

    def body(a_ref, lg_ref, lb_ref, ws_ref, bs_ref, o_ref, vn_scr):
        av = a_ref[:, W:]
        v = av * _cdf(av)
        vc = v - jnp.mean(v, axis=-1, keepdims=True)
        rstd = lax.rsqrt(jnp.mean(vc * vc, axis=-1, keepdims=True) + EPS)
        vn_scr[...] = (vc * rstd * lg_ref[...] + lb_ref[...]).astype(BF16)
        mask = _spatial_mask()
        for h in range(GM_HEADS):
            w = jnp.where(mask, ws_ref[h], 0.0).astype(BF16)
            cs = slice(h * GM_HEAD_DIM, (h + 1) * GM_HEAD_DIM)
            for blk in range(tr // GM_BLOCK):
                rs = slice(blk * GM_BLOCK, (blk + 1) * GM_BLOCK)
                s = _dot(w, vn_scr[rs, cs], NN) + bs_ref[h]
                au = a_ref[rs, cs]
                o_ref[rs, cs] = (au * _cdf(au) * s).astype(BF16)

    return pl.pallas_call(
        body, name=name, out_shape=jax.ShapeDtypeStruct((T, W), BF16), grid=(T // tr,),
        in_specs=_gm_specs(tr), out_specs=_row_spec(tr, W),
        scratch_shapes=[pltpu.VMEM((tr, W), BF16)], compiler_params=_params(("parallel",)),
    )(a, ln_g, ln_b, ws, bs3)


def _gm_mid_bwd(a, dgated, ln_g, ln_b, ws, bs3, name):
    T = a.shape[0]
    tr = _pick(T, (ROW_TILE,))
    W = GM_WIDTH
    nsteps = T // tr

    def body(a_ref, dg_ref, lg_ref, lb_ref, ws_ref, bs_ref, da_ref, dws_ref, dbs_ref, dlg_ref, dlb_ref,
             vn_scr, vhat_scr, dvn_scr, dsum_scr):
        i = pl.program_id(0)

        @pl.when(i == 0)
        def _():
            dws_ref[...] = jnp.zeros_like(dws_ref)
            dbs_ref[...] = jnp.zeros_like(dbs_ref)
            dlg_ref[...] = jnp.zeros_like(dlg_ref)
            dlb_ref[...] = jnp.zeros_like(dlb_ref)
            dsum_scr[...] = jnp.zeros_like(dsum_scr)

        av = a_ref[:, W:]
        cdf_v = _cdf(av)
        v = av * cdf_v
        vc = v - jnp.mean(v, axis=-1, keepdims=True)
        rstd = lax.rsqrt(jnp.mean(vc * vc, axis=-1, keepdims=True) + EPS)
        vhat_scr[...] = vc * rstd
        vn_scr[...] = (vhat_scr[...] * lg_ref[...] + lb_ref[...]).astype(BF16)
        mask = _spatial_mask()
        for h in range(GM_HEADS):
            w = jnp.where(mask, ws_ref[h], 0.0).astype(BF16)
            cs = slice(h * GM_HEAD_DIM, (h + 1) * GM_HEAD_DIM)
            for blk in range(tr // GM_BLOCK):
                rs = slice(blk * GM_BLOCK, (blk + 1) * GM_BLOCK)
                vnb = vn_scr[rs, cs]
                s = _dot(w, vnb, NN) + bs_ref[h]
                au = a_ref[rs, cs]
                cdf_u = _cdf(au)
                dg = dg_ref[rs, cs]
                ds = dg * (au * cdf_u)
                da_ref[rs, cs] = (dg * s * (cdf_u + au * _pdf(au))).astype(BF16)
                dsb = ds.astype(BF16)
                dvn_scr[rs, cs] = _dot(w, dsb, TN)
                dws_ref[h] += _dot(dsb, vnb, NT)
                dsum_scr[:, cs] += ds
        dvn = dvn_scr[...]
        vhat = vhat_scr[...]
        dlg_ref[...] += jnp.sum(dvn * vhat, axis=0, keepdims=True)
        dlb_ref[...] += jnp.sum(dvn, axis=0, keepdims=True)
        dvh = dvn * lg_ref[...]
        dv = rstd * (dvh - jnp.mean(dvh, axis=-1, keepdims=True)
                     - vhat * jnp.mean(dvh * vhat, axis=-1, keepdims=True))
        da_ref[:, W:] = (dv * (cdf_v + av * _pdf(av))).astype(BF16)

        @pl.when(i == nsteps - 1)
        def _():
            for h in range(GM_HEADS):
                dws_ref[h] = jnp.where(mask, dws_ref[h], 0.0)
            col_head = lax.broadcasted_iota(jnp.int32, (W, GM_BLOCK), 0) // GM_HEAD_DIM
            sel = (col_head == lax.broadcasted_iota(jnp.int32, (W, GM_BLOCK), 1)).astype(F32)
            dbs_ref[...] = _dot(dsum_scr[...], sel, NN, HI)

    vshape = jax.ShapeDtypeStruct((1, W), F32)
    return pl.pallas_call(
        body, name=name, grid=(nsteps,),
        out_shape=(jax.ShapeDtypeStruct((T, 2 * W), BF16), jax.ShapeDtypeStruct((GM_HEADS, GM_BLOCK, GM_BLOCK), F32),
                   jax.ShapeDtypeStruct((GM_BLOCK, GM_BLOCK), F32), vshape, vshape),
        in_specs=[_gm_specs(tr)[0], _row_spec(tr, W)] + _gm_specs(tr)[1:],
        out_specs=(_row_spec(tr, 2 * W), pl.BlockSpec((GM_HEADS, GM_BLOCK, GM_BLOCK), lambda i: (0, 0, 0)),
                   pl.BlockSpec((GM_BLOCK, GM_BLOCK), lambda i: (0, 0)), _vec_spec(W), _vec_spec(W)),
        scratch_shapes=[pltpu.VMEM((tr, W), BF16), pltpu.VMEM((tr, W), F32), pltpu.VMEM((tr, W), F32),
                        pltpu.VMEM((GM_BLOCK, W), F32)],
        compiler_params=_params(("arbitrary",)),
    )(a, dgated, ln_g, ln_b, ws, bs3)


SUB = 16
EXP_CLAMP = 80.0


def _tri(lower):
    r = lax.broadcasted_iota(jnp.int32, (CHUNK, CHUNK), 0)
    c = lax.broadcasted_iota(jnp.int32, (CHUNK, CHUNK), 1)
    return (r >= c) if lower else (c >= r)


def _score_masks():
    i = lax.broadcasted_iota(jnp.int32, (CHUNK, CHUNK), 0)
    j = lax.broadcasted_iota(jnp.int32, (CHUNK, CHUNK), 1)
    bi, bj = i // SUB, j // SUB
    diag = (bi == bj) & (i >= j)
    pair = (bi % 2 == 1) & (bj == bi - 1)
    half = (i >= CHUNK // 2) & (j < CHUNK // 2)
    return diag, pair, half


def _dot01(m, x):
    x1 = x.astype(BF16)
    rest = x - x1.astype(F32)
    x2 = rest.astype(BF16)
    x3 = (rest - x2.astype(F32)).astype(BF16)
    return _dot(m, x1, NN) + (_dot(m, x2, NN) + _dot(m, x3, NN))


def _block_rows(b, offset):
    parts = []
    for blk in range(0, CHUNK, SUB):
        r = blk + offset
        parts.append(jnp.zeros((SUB, b.shape[1]), F32) if r < 0 else jnp.broadcast_to(b[r:r + 1], (SUB, b.shape[1])))
    return jnp.concatenate(parts, axis=0)


def _hg_gates(p_ref, lb_ref, lower):
    Dm = D_MODEL
    heads = []
    for h in range(HG_HEADS):
        c0 = h * HG_DIM
        qr = p_ref[:, c0:c0 + HG_DIM]
        fz = p_ref[:, Dm + c0:Dm + c0 + HG_DIM]
        lbh = lb_ref[:, c0:c0 + HG_DIM]
        sg = _sig(fz)
        f = lbh + (1.0 - lbh) * sg
        sq = _sig(qr)
        heads.append(dict(qr=qr, v=p_ref[:, 2 * Dm + c0:2 * Dm + c0 + HG_DIM],
                          gt=p_ref[:, 3 * Dm + c0:3 * Dm + c0 + HG_DIM], lbh=lbh, sg=sg, f=f, gl=jnp.log(f),
                          kk=1.0 - f, sq=sq, q=qr * sq))
    for g in heads:
        g["b"] = _dot01(lower, g.pop("gl"))
    for g in heads:
        g.update(_hg_scalings(g["q"], g["kk"], g.pop("b")))
    return heads


def _hg_scalings(q, kk, b):
    r_mid = _block_rows(b, SUB // 2 - 1)
    r_prev = _block_rows(b, -1)
    r_end = _block_rows(b, SUB - 1)
    r_half = jnp.broadcast_to(b[CHUNK // 2 - 1:CHUNK // 2], b.shape)
    bc = b[CHUNK - 1:CHUNK]
    eqs = (jnp.exp(jnp.clip(b - r_mid, -EXP_CLAMP, EXP_CLAMP)), jnp.exp(jnp.minimum(b - r_prev, 0.0)),
           jnp.exp(jnp.minimum(b - r_half, 0.0)))
    eks = (jnp.exp(jnp.clip(r_mid - b, -EXP_CLAMP, EXP_CLAMP)), jnp.exp(jnp.minimum(r_end - b, 0.0)),
           jnp.exp(jnp.minimum(r_half - b, 0.0)))
    eb = jnp.exp(b)
    ec = jnp.exp(bc - b)
    return dict(eqs=eqs, eks=eks, eb=eb, ec=ec, e_end=jnp.exp(bc), qs=[q * e for e in eqs],
                ks=[kk * e for e in eks], qe=q * eb, ke=kk * ec)


def _scores(g, masks):
    a = None
    for qs, ks, m in zip(g["qs"], g["ks"], masks):
        part = jnp.where(m, _dot(qs.astype(BF16), ks.astype(BF16), NT), 0.0)
        a = part if a is None else a + part
    return a


def _hg_scan_fwd(p, lb, gn, name):
    T = p.shape[0]
    nc = T // CHUNK
    Dm = D_MODEL

    def body(p_ref, lb_ref, gn_ref, o_ref, og_ref, so_ref, st_ref):
        @pl.when(pl.program_id(0) == 0)
        def _():
            st_ref[...] = jnp.zeros_like(st_ref)

        masks = _score_masks()
        heads = _hg_gates(p_ref, lb_ref, _tri(True).astype(BF16))
        states = [st_ref[h] for h in range(HG_HEADS)]
        scores = [_scores(g, masks) for g in heads]
        outs = [_dot(a.astype(BF16), g["v"].astype(BF16), NN) + _dot(g["qe"], st, NT, X3)
                for g, a, st in zip(heads, scores, states)]
        new_states = [st * g["e_end"] + _dot(g["v"], g["ke"], TN, X3) for g, st in zip(heads, states)]
        for h, (g, o, st, st2) in enumerate(zip(heads, outs, states, new_states)):
            cs = slice(h * HG_DIM, (h + 1) * HG_DIM)
            so_ref[0, h] = st
            st_ref[h] = st2
            o_ref[:, cs] = o
            r = lax.rsqrt(jnp.mean(o * o, axis=-1, keepdims=True) + EPS)
            gt = g["gt"]
            og_ref[:, cs] = (((o * r) * gn_ref[:, cs]).astype(F32) * (gt * _sig(gt))).astype(BF16)

    return pl.pallas_call(
        body, name=name, grid=(nc,),
        out_shape=(jax.ShapeDtypeStruct((T, Dm), F32), jax.ShapeDtypeStruct((T, Dm), BF16),
                   jax.ShapeDtypeStruct((nc, HG_HEADS, HG_DIM, HG_DIM), F32)),
        in_specs=[_row_spec(CHUNK, 4 * Dm), _vec_spec(Dm), _vec_spec(Dm)],
        out_specs=(_row_spec(CHUNK, Dm), _row_spec(CHUNK, Dm),
                   pl.BlockSpec((1, HG_HEADS, HG_DIM, HG_DIM), lambda i: (i, 0, 0, 0))),
        scratch_shapes=[pltpu.VMEM((HG_HEADS, HG_DIM, HG_DIM), F32)],
        compiler_params=_params(("arbitrary",)),
    )(p, lb, gn)


def _hg_scan_bwd(p, lb, gn, o, dog, states, name):
    T = p.shape[0]
    nc = T // CHUNK
    Dm = D_MODEL

    def rev(i):
        return nc - 1 - i

    def body(p_ref, lb_ref, gn_ref, o_ref, dog_ref, st_in_ref, dp_ref, dlb_ref, dgn_ref, dst_ref, carry_ref):
        @pl.when(pl.program_id(0) == 0)
        def _():
            dst_ref[...] = jnp.zeros_like(dst_ref)
            carry_ref[...] = jnp.zeros_like(carry_ref)
            dlb_ref[...] = jnp.zeros_like(dlb_ref)
            dgn_ref[...] = jnp.zeros_like(dgn_ref)

        upper = _tri(False).astype(BF16)
        masks = _score_masks()
        heads = _hg_gates(p_ref, lb_ref, _tri(True).astype(BF16))
        for h, g in enumerate(heads):
            cs = slice(h * HG_DIM, (h + 1) * HG_DIM)
            oh = o_ref[:, cs]
            r = lax.rsqrt(jnp.mean(oh * oh, axis=-1, keepdims=True) + EPS)
            on = oh * r
            gt = g["gt"]
            sgt = _sig(gt)
            sil = gt * sgt
            dogh = dog_ref[:, cs]
            gnh = gn_ref[:, cs]
            don = dogh * gnh * sil
            g["dgn"] = jnp.sum(dogh * on * sil, axis=0, keepdims=True)
            g["dgate"] = dogh * on * gnh * (sgt * (1.0 + gt * (1.0 - sgt)))
            g["do"] = r * (don - on * jnp.mean(don * on, axis=-1, keepdims=True))
            g["dst"] = dst_ref[h]
            g["st"] = st_in_ref[0, h]
            g["carry"] = carry_ref[h]
        for g in heads:
            g["a"] = _scores(g, masks)
            g["dob"] = g["do"].astype(BF16)
            g["da"] = _dot(g["dob"], g["v"].astype(BF16), NT)
        for g in heads:
            g["dv"] = _dot(g["a"].astype(BF16), g["dob"], TN) + _dot(g["ke"].astype(BF16), g["dst"].astype(BF16), NT)
            g["dq"] = _dot(g["do"], g["st"], NN, X3) * g["eb"]
            g["dk"] = _dot(g["v"], g["dst"], NN, X3) * g["ec"]
            g["dst2"] = g["dst"] * g["e_end"] + _dot(g["do"], g["qe"], TN, X3)
        for lvl in range(3):
            for g in heads:
                dam = jnp.where(masks[lvl], g["da"], 0.0)
                g["dq"] = g["dq"] + _dot(dam, g["ks"][lvl], NN, X3) * g["eqs"][lvl]
                g["dk"] = g["dk"] + _dot(dam, g["qs"][lvl], TN, X3) * g["eks"][lvl]
        for g in heads:
            g["dgd"] = g["q"] * g["dq"] - g["kk"] * g["dk"]
            g["dgl"] = _dot01(upper, g["dgd"]) + g["carry"]
        for h, g in enumerate(heads):
            c0 = h * HG_DIM
            cs = slice(c0, c0 + HG_DIM)
            df = g["dgl"] / g["f"] - g["dk"]
            sg, sq, qr = g["sg"], g["sq"], g["qr"]
            dst_ref[h] = g["dst2"]
            carry_ref[h] = g["carry"] + jnp.sum(g["dgd"], axis=0, keepdims=True)
            dgn_ref[:, cs] += g["dgn"]
            dlb_ref[:, cs] += jnp.sum(df * (1.0 - sg), axis=0, keepdims=True)
            dp_ref[:, c0:c0 + HG_DIM] = (g["dq"] * (sq * (1.0 + qr * (1.0 - sq)))).astype(BF16)
            dp_ref[:, Dm + c0:Dm + c0 + HG_DIM] = (df * (1.0 - g["lbh"]) * sg * (1.0 - sg)).astype(BF16)
            dp_ref[:, 2 * Dm + c0:2 * Dm + c0 + HG_DIM] = g["dv"].astype(BF16)
            dp_ref[:, 3 * Dm + c0:3 * Dm + c0 + HG_DIM] = g["dgate"].astype(BF16)

    vshape = jax.ShapeDtypeStruct((1, Dm), F32)
    rrow = lambda w: pl.BlockSpec((CHUNK, w), lambda i: (rev(i), 0))
    return pl.pallas_call(
        body, name=name, grid=(nc,),
        out_shape=(jax.ShapeDtypeStruct((T, 4 * Dm), BF16), vshape, vshape),
        in_specs=[rrow(4 * Dm), _vec_spec(Dm), _vec_spec(Dm), rrow(Dm), rrow(Dm),
                  pl.BlockSpec((1, HG_HEADS, HG_DIM, HG_DIM), lambda i: (rev(i), 0, 0, 0))],
        out_specs=(rrow(4 * Dm), _vec_spec(Dm), _vec_spec(Dm)),
        scratch_shapes=[pltpu.VMEM((HG_HEADS, HG_DIM, HG_DIM), F32), pltpu.VMEM((HG_HEADS, 1, HG_DIM), F32)],
        compiler_params=_params(("arbitrary",)),
    )(p, lb, gn, o, dog, states)


def _lb_fwd(hg_lb, name):
    def body(a_ref, o_ref):
        a0, a1 = a_ref[0:1], a_ref[1:2]
        m = jnp.maximum(a0, a1)
        e0, e1 = jnp.exp(a0 - m), jnp.exp(a1 - m)
        p0, p1 = e0 / (e0 + e1), e1 / (e0 + e1)
        o_ref[0:1] = p0 - p0
        o_ref[1:2] = (p0 + p1) - p0

    return pl.pallas_call(body, name=name, out_shape=jax.ShapeDtypeStruct(hg_lb.shape, F32))(hg_lb)


def _lb_bwd(hg_lb, dlb_all, name):
    def body(a_ref, d_ref, o_ref):
        a0, a1 = a_ref[0:1], a_ref[1:2]
        m = jnp.maximum(a0, a1)
        e0, e1 = jnp.exp(a0 - m), jnp.exp(a1 - m)
        p0, p1 = e0 / (e0 + e1), e1 / (e0 + e1)
        d1 = d_ref[1:2]
        o_ref[0:1] = -p0 * p1 * d1
        o_ref[1:2] = p1 * (1.0 - p1) * d1

    return pl.pallas_call(body, name=name, out_shape=jax.ShapeDtypeStruct(hg_lb.shape, F32))(hg_lb, dlb_all)


CONV_COLS_FWD = 256
CONV_COLS_BWD = 128


def _conv_fwd(a, w, b, name):
    T = a.shape[0]
    Fh = FFN_HIDDEN
    tr = _pick(T, (ROW_TILE,))
    cw = CONV_COLS_FWD
    hb = tr // 8

    def body(a_ref, ap_ref, w_ref, b_ref, m_ref):
        m0 = (pl.program_id(0) > 0).astype(F32)

        def conv(cc):
            x = jnp.concatenate([ap_ref[:, pl.ds(cc, cw)] * m0, a_ref[:, pl.ds(cc, cw)]], axis=0)
            wv = w_ref[:, pl.ds(cc, cw)]
            y = b_ref[:, pl.ds(cc, cw)] + wv[2:3] * x + wv[1:2] * pltpu.roll(x, 1, axis=0) \
                + wv[0:1] * pltpu.roll(x, 2, axis=0)
            return y[8:]

        def step(c, carry):
            c0 = pl.multiple_of(c * cw, cw)
            c1 = pl.multiple_of(Fh + c * cw, cw)
            yg, yv = conv(c0), conv(c1)
            m_ref[:, pl.ds(c0, cw)] = (yg * _cdf(yg) * yv).astype(BF16)
            return carry

        lax.fori_loop(0, Fh // cw, step, 0)

    return pl.pallas_call(
        body, name=name, out_shape=jax.ShapeDtypeStruct((T, Fh), BF16), grid=(T // tr,),
        in_specs=[_row_spec(tr, 2 * Fh), pl.BlockSpec((8, 2 * Fh), lambda i: (jnp.maximum(i * hb - 1, 0), 0)),
                  _vec_spec(2 * Fh, 3), _vec_spec(2 * Fh)],
        out_specs=_row_spec(tr, Fh), compiler_params=_params(("parallel",)),
    )(a, a, w, b)


def _conv_bwd(a, dm, w, b, name):
    T = a.shape[0]
    Fh = FFN_HIDDEN
    tr = _pick(T, (ROW_TILE,))
    cw = CONV_COLS_BWD
    hb = tr // 8
    nsteps = T // tr
    n = tr + 8

    def body(a_ref, ap_ref, an_ref, dm_ref, dmn_ref, w_ref, b_ref, da_ref, dw_ref, db_ref):
        i = pl.program_id(0)
        m0 = (i > 0).astype(F32)
        m1 = (i < nsteps - 1).astype(F32)

        @pl.when(i == 0)
        def _():
            dw_ref[...] = jnp.zeros_like(dw_ref)
            db_ref[...] = jnp.zeros_like(db_ref)

        def prep(cc):
            x = jnp.concatenate([ap_ref[:, pl.ds(cc, cw)] * m0, a_ref[:, pl.ds(cc, cw)],
                                 an_ref[:, pl.ds(cc, cw)] * m1], axis=0)
            wv = w_ref[:, pl.ds(cc, cw)]
            s1 = pltpu.roll(x, 1, axis=0)
            s2 = pltpu.roll(x, 2, axis=0)
            y = b_ref[:, pl.ds(cc, cw)] + wv[2:3] * x + wv[1:2] * s1 + wv[0:1] * s2
            return wv, x[8:], s1[8:], s2[8:], y[8:]

        def back(cc, dy, wv, x0, s1, s2):
            da = wv[2:3] * dy + wv[1:2] * pltpu.roll(dy, n - 1, axis=0) + wv[0:1] * pltpu.roll(dy, n - 2, axis=0)
            da_ref[:, pl.ds(cc, cw)] = da[:tr].astype(BF16)
            d = dy[:tr]
            db_ref[:, pl.ds(cc, cw)] += jnp.sum(d, axis=0, keepdims=True)
            dw_ref[2:3, pl.ds(cc, cw)] += jnp.sum(d * x0[:tr], axis=0, keepdims=True)
            dw_ref[1:2, pl.ds(cc, cw)] += jnp.sum(d * s1[:tr], axis=0, keepdims=True)
            dw_ref[0:1, pl.ds(cc, cw)] += jnp.sum(d * s2[:tr], axis=0, keepdims=True)

        def step(c, carry):
            c0 = pl.multiple_of(c * cw, cw)
            c1 = pl.multiple_of(Fh + c * cw, cw)
            dmx = jnp.concatenate([dm_ref[:, pl.ds(c0, cw)], dmn_ref[:, pl.ds(c0, cw)] * m1], axis=0)
            wg, xg, s1g, s2g, yg = prep(c0)
            wv, xv, s1v, s2v, yv = prep(c1)
            cg = _cdf(yg)
            back(c0, dmx * yv * (cg + yg * _pdf(yg)), wg, xg, s1g, s2g)
            back(c1, dmx * (yg * cg), wv, xv, s1v, s2v)
            return carry

        lax.fori_loop(0, Fh // cw, step, 0)

    prev = lambda wd: pl.BlockSpec((8, wd), lambda i: (jnp.maximum(i * hb - 1, 0), 0))
    nxt = lambda wd: pl.BlockSpec((8, wd), lambda i: (jnp.minimum((i + 1) * hb, T // 8 - 1), 0))
    return pl.pallas_call(
        body, name=name, grid=(nsteps,),
        out_shape=(jax.ShapeDtypeStruct((T, 2 * Fh), BF16), jax.ShapeDtypeStruct((3, 2 * Fh), F32),
                   jax.ShapeDtypeStruct((1, 2 * Fh), F32)),
        in_specs=[_row_spec(tr, 2 * Fh), prev(2 * Fh), nxt(2 * Fh), _row_spec(tr, Fh), nxt(Fh),
                  _vec_spec(2 * Fh, 3), _vec_spec(2 * Fh)],
        out_specs=(_row_spec(tr, 2 * Fh), _vec_spec(2 * Fh, 3), _vec_spec(2 * Fh)),
        compiler_params=_params(("arbitrary",)),
    )(a, a, a, dm, dm, w, b)


def _ada_fwd(c_all, ada_w, ada_b, name):
    L, Dm, cols = ada_w.shape
    tn = _pick(cols, (512, 256, 128))

    def body(c_ref, w_ref, b_ref, o_ref):
        cv = c_ref[...]
        cond = (cv * _sig(cv)).astype(BF16)
        o_ref[...] = _dot(cond, w_ref[...].astype(BF16), NN) + b_ref[...]

    return pl.pallas_call(
        body, name=name, out_shape=jax.ShapeDtypeStruct((L, N_DEV, cols), F32), grid=(L, cols // tn),
        in_specs=[pl.BlockSpec((N_DEV, Dm), lambda l, j: (0, 0)), pl.BlockSpec((None, Dm, tn), lambda l, j: (l, 0, j)),
                  pl.BlockSpec((None, 1, tn), lambda l, j: (l, 0, j))],
        out_specs=pl.BlockSpec((None, N_DEV, tn), lambda l, j: (l, 0, j)),
        compiler_params=_params(("parallel", "parallel")),
    )(c_all, ada_w, ada_b.reshape(L, 1, cols))


def _ada_bwd(c_all, dmod, name):
    L, _, cols = dmod.shape
    Dm = c_all.shape[1]
    tn = _pick(cols, (512, 256, 128))

    def body(c_ref, d_ref, o_ref):
        cv = c_ref[...]
        o_ref[...] = _dot(cv * _sig(cv), d_ref[...], TN, HI)

    return pl.pallas_call(
        body, name=name, out_shape=jax.ShapeDtypeStruct((L, Dm, cols), F32), grid=(L, cols // tn),
        in_specs=[pl.BlockSpec((N_DEV, Dm), lambda l, j: (0, 0)), pl.BlockSpec((None, N_DEV, tn), lambda l, j: (l, 0, j))],
        out_specs=pl.BlockSpec((None, Dm, tn), lambda l, j: (l, 0, j)),
        compiler_params=_params(("parallel", "parallel")),
    )(c_all, dmod)


def _add_own_half(g4, rb, core, name):
    S, _, rh, cw = g4.shape
    tr = _pick(rh, (256, 128, 176, 64))

    def body(core_ref, g_ref, r_ref, o_ref):
        o_ref[...] = (g_ref[...].astype(F32) + r_ref[...].astype(F32)).astype(GRAD_WIRE)

    return pl.pallas_call(
        body, name=name, out_shape=jax.ShapeDtypeStruct((S, rh, cw), BF16),
        grid_spec=pltpu.PrefetchScalarGridSpec(
            num_scalar_prefetch=1, grid=(S, rh // tr),
            in_specs=[pl.BlockSpec((None, None, tr, cw), lambda s, i, core_ref: (s, core_ref[0], i, 0)),
                      pl.BlockSpec((None, tr, cw), lambda s, i, core_ref: (s, i, 0))],
            out_specs=pl.BlockSpec((None, tr, cw), lambda s, i, core_ref: (s, i, 0))),
        compiler_params=_params(("parallel", "parallel")),
    )(core, g4, rb)


def _sum_chips(lands, sums, chip, name):
    L = len(lands)
    _, rh, cw = lands[0].shape
    tr = _pick(rh, (256, 128, 176, 64))

    def body(chip_ref, *refs):
        ld, cs, o_ref = refs[:L], refs[L:2 * L], refs[2 * L]
        me = chip_ref[0]
        for k in range(L):
            @pl.when(pl.program_id(0) == k)
            def _(k=k):
                own = cs[k][...].astype(F32)
                got = [ld[k][j].astype(F32) for j in range(3)]
                acc = None
                for t in range(N_CHIPS):
                    d = jnp.bitwise_xor(jnp.int32(t), me)
                    term = jnp.where(d == 0, own, jnp.where(d == 2, got[0], jnp.where(d == 1, got[1], got[2])))
                    acc = term if acc is None else acc + term
                o_ref[...] = acc

    frozen = lambda l, i, k: jnp.where(l == k, i, 0)
    in_specs = [pl.BlockSpec((3, tr, cw), lambda l, i, chip_ref, k=k: (0, frozen(l, i, k), 0)) for k in range(L)]
    in_specs += [pl.BlockSpec((None, tr, cw), lambda l, i, chip_ref, k=k: (chip_ref[0], frozen(l, i, k), 0))
                 for k in range(L)]
    return pl.pallas_call(
        body, name=name, out_shape=jax.ShapeDtypeStruct((L, rh, cw), F32),
        grid_spec=pltpu.PrefetchScalarGridSpec(
            num_scalar_prefetch=1, grid=(L, rh // tr), in_specs=in_specs,
            out_specs=pl.BlockSpec((None, tr, cw), lambda l, i, chip_ref: (l, i, 0))),
        compiler_params=_params(("arbitrary", "arbitrary")),
    )(chip, *lands, *sums)


def _sum_devices(gathered, name):
    n, R, _ = gathered.shape
    tr = _pick(R, (512, 448, 384, 256, 192, 128, 64, 32, 16, 8))

    def body(g_ref, o_ref):
        acc = g_ref[0]
        for d in range(1, n):
            acc = acc + g_ref[d]
        o_ref[...] = acc

    return pl.pallas_call(
        body, name=name, out_shape=jax.ShapeDtypeStruct((R, LANES), F32), grid=(R // tr,),
        in_specs=[pl.BlockSpec((n, tr, LANES), lambda i: (0, i, 0))], out_specs=pl.BlockSpec((tr, LANES), lambda i: (i, 0)),
        compiler_params=_params(("parallel",)),
    )(gathered)


def _adamw(w, g, m, v, name):
    R, C = w.shape
    tr = _pick(R, (256, 128, 64, 32, 16, 8))
    c1 = 1.0 / (1.0 - ADAM_B1 ** ADAM_STEP)
    c2 = 1.0 / (1.0 - ADAM_B2 ** ADAM_STEP)

    def body(w_ref, g_ref, m_ref, v_ref, d_ref, mo_ref, vo_ref):
        gv = g_ref[...]
        m2 = ADAM_B1 * m_ref[...] + (1.0 - ADAM_B1) * gv
        v2 = ADAM_B2 * v_ref[...] + (1.0 - ADAM_B2) * (gv * gv)
        mo_ref[...] = m2
        vo_ref[...] = v2
        d_ref[...] = -ADAM_LR * ((m2 * c1) / (jnp.sqrt(v2 * c2) + ADAM_EPS) + ADAM_WD * w_ref[...])

    spec = pl.BlockSpec((tr, C), lambda i: (i, 0))
    shp = jax.ShapeDtypeStruct((R, C), F32)
    return pl.pallas_call(body, name=name, out_shape=(shp, shp, shp), grid=(R // tr,), in_specs=[spec] * 4,
                          out_specs=(spec, spec, spec), compiler_params=_params(("parallel",)))(w, g, m, v)


def _adamw_halves(w, own, recv, m, v, core, name):
    L, rh, cw = own.shape
    tr = _pick(rh, (256, 128, 176, 64))
    c1 = 1.0 / (1.0 - ADAM_B1 ** ADAM_STEP)
    c2 = 1.0 / (1.0 - ADAM_B2 ** ADAM_STEP)

    def body(core_ref, w_ref, own_ref, recv_ref, m_ref, v_ref, g_ref, d_ref, mo_ref, vo_ref):
        gv = jnp.where(pl.program_id(1) == core_ref[0], own_ref[...], recv_ref[...])
        g_ref[...] = gv
        m2 = ADAM_B1 * m_ref[...] + (1.0 - ADAM_B1) * gv
        v2 = ADAM_B2 * v_ref[...] + (1.0 - ADAM_B2) * (gv * gv)
        mo_ref[...] = m2
        vo_ref[...] = v2
        d_ref[...] = -ADAM_LR * ((m2 * c1) / (jnp.sqrt(v2 * c2) + ADAM_EPS) + ADAM_WD * w_ref[...])

    full = pl.BlockSpec((None, None, tr, cw), lambda l, hf, i, core_ref: (l, hf, i, 0))
    mine = pl.BlockSpec((None, tr, cw), lambda l, hf, i, core_ref: (l, jnp.where(hf == core_ref[0], i, 0), 0))
    other = pl.BlockSpec((None, tr, cw), lambda l, hf, i, core_ref: (l, jnp.where(hf == core_ref[0], 0, i), 0))
    shp = jax.ShapeDtypeStruct((L, 2, rh, cw), F32)
    view = lambda a: a.reshape(L, 2, rh, cw)
    outs = pl.pallas_call(
        body, name=name, out_shape=(shp, shp, shp, shp),
        grid_spec=pltpu.PrefetchScalarGridSpec(
            num_scalar_prefetch=1, grid=(L, 2, rh // tr), in_specs=[full, mine, other, full, full],
            out_specs=(full, full, full, full)),
        compiler_params=_params(("arbitrary", "arbitrary", "arbitrary")),
    )(core, view(w), own, recv, view(m), view(v))
    return tuple(o.reshape(L, 2 * rh, cw) for o in outs)


ANY = pl.BlockSpec(memory_space=pl.ANY)


def _position():
    x, y, c = lax.axis_index("x"), lax.axis_index("y"), lax.axis_index("c")
    return x, y, c


def _allgather(ins, out_shapes, src_fns, dst_fns, name, in_vmem):
    n = len(ins)

    def body(*refs):
        in_refs, out_refs = refs[:n], refs[n:2 * n]
        send_sems, recv_sems, local_sems = refs[2 * n:]
        x, y, c = _position()
        me, sibling = (x, y, c), (x, y, 1 - c)
        chips = [(1 - x, y), (x, 1 - y), (1 - x, 1 - y)]

        def copy(k, j, block, to, own=False):
            dst = dst_fns[k](out_refs[k], *block)
            return pltpu.make_async_remote_copy(
                src_ref=src_fns[k](in_refs[k], c) if own else dst, dst_ref=dst,
                send_sem=send_sems.at[k, j], recv_sem=recv_sems.at[k, j], device_id=to, device_id_type=MESH)

        mine = [pltpu.make_async_copy(src_fns[k](in_refs[k], c), dst_fns[k](out_refs[k], *me), local_sems.at[k])
                for k in range(n)]
        for cp in mine:
            cp.start()
        first = []
        for k in range(n):
            first.append(copy(k, 0, me, sibling, own=True))
            first += [copy(k, 1 + j, me, (*chip, c), own=True) for j, chip in enumerate(chips)]
        for cp in first:
            cp.start()
        passed = []
        for j, chip in enumerate(chips):
            for k in range(n):
                copy(k, 1 + j, (*chip, c), me).wait_recv()
                fwd = copy(k, 4 + j, (*chip, c), sibling)
                fwd.start()
                passed.append(fwd)
        for k in range(n):
            copy(k, 0, sibling, me).wait_recv()
        for j, chip in enumerate(chips):
            for k in range(n):
                copy(k, 4 + j, (*chip, 1 - c), me).wait_recv()
        for cp in first + passed:
            cp.wait_send()
        for cp in mine:
            cp.wait()

    spec = pl.BlockSpec(memory_space=pltpu.VMEM) if in_vmem else ANY
    return pl.pallas_call(
        body, name=name, out_shape=tuple(out_shapes), in_specs=[spec] * n, out_specs=tuple([spec] * n),
        scratch_shapes=[pltpu.SemaphoreType.DMA((n, 7)), pltpu.SemaphoreType.DMA((n, 7)),
                        pltpu.SemaphoreType.DMA((n,))],
        compiler_params=pltpu.CompilerParams(vmem_limit_bytes=VMEM_LIMIT_BYTES),
    )(*ins)


def _allgather_small(payload, name):
    R = payload.shape[0]
    (out,) = _allgather(
        [payload], [jax.ShapeDtypeStruct((N_DEV, R, LANES), F32)],
        [lambda ref, c: ref], [lambda ref, px, py, pc: ref.at[4 * px + 2 * py + pc]], name, in_vmem=True)
    return out


def _swap_sibling(ins, name, other_half=False):
    n = len(ins)

    def body(*refs):
        in_refs, out_refs = refs[:n], refs[n:2 * n]
        send_sems, recv_sems = refs[2 * n:]
        x, y, c = _position()
        copies = [pltpu.make_async_remote_copy(
            src_ref=in_refs[k].at[:, 1 - c] if other_half else in_refs[k], dst_ref=out_refs[k],
            send_sem=send_sems.at[k], recv_sem=recv_sems.at[k],
            device_id=(x, y, 1 - c), device_id_type=MESH) for k in range(n)]
        for cp in copies:
            cp.start()
        for cp in copies:
            cp.wait_recv()
        for cp in copies:
            cp.wait_send()

    shape = lambda a: (a.shape[0],) + a.shape[2:] if other_half else a.shape
    return pl.pallas_call(
        body, name=name, out_shape=tuple(jax.ShapeDtypeStruct(shape(a), a.dtype) for a in ins),
        in_specs=[ANY] * n, out_specs=tuple([ANY] * n),
        scratch_shapes=[pltpu.SemaphoreType.DMA((n,)), pltpu.SemaphoreType.DMA((n,))],
        compiler_params=pltpu.CompilerParams(vmem_limit_bytes=VMEM_LIMIT_BYTES),
    )(*ins)


HBM_SPEC = pl.BlockSpec(memory_space=pltpu.HBM)
SEM_SPEC = pl.BlockSpec(memory_space=pltpu.SEMAPHORE)
SPLIT_PARAMS = pltpu.CompilerParams(has_side_effects=pltpu.SideEffectType.DATAFLOW_SIDE_EFFECTING)
TOKEN = jax.ShapeDtypeStruct((8, LANES), F32)


def _hbm(a):
    return pltpu.with_memory_space_constraint(a, pltpu.HBM)


def _weight_window(ref, col, r, cw, px, py, pc):
    rh = r // 2
    if col:
        return ref.at[pl.ds(pc * rh, rh), pl.ds((2 * px + py) * cw, cw)]
    return ref.at[pl.ds((2 * px + py) * r + pc * rh, rh), :]


def _peers(x, y, c):
    return [(x, y, 1 - c), (1 - x, y, c), (x, 1 - y, c), (1 - x, 1 - y, c)]


def _place_own(shard, col, pos, name):
    r, cw = shard.shape
    rh = r // 2
    tr = _pick(rh, (256, 128, 176, 64))
    nb = rh // tr
    shape = (r, N_CHIPS * cw) if col else (N_CHIPS * r, cw)

    def body(pos_ref, x_ref, o_ref):
        o_ref[...] = x_ref[...]

    if col:
        out_idx = lambda i, pos_ref: (pos_ref[1] * nb + i, pos_ref[0])
    else:
        out_idx = lambda i, pos_ref: (pos_ref[0] * (2 * nb) + pos_ref[1] * nb + i, 0)
    return pl.pallas_call(
        body, name=name, out_shape=jax.ShapeDtypeStruct(shape, shard.dtype),
        grid_spec=pltpu.PrefetchScalarGridSpec(
            num_scalar_prefetch=1, grid=(nb,),
            in_specs=[pl.BlockSpec((tr, cw), lambda i, pos_ref: (pos_ref[1] * nb + i, 0))],
            out_specs=pl.BlockSpec((tr, cw), out_idx)),
        compiler_params=_params(("arbitrary",)),
    )(pos, shard)


def _gather_start(shards, lands, cols, per_layer, name):
    n = len(shards)
    nl = n // per_layer

    def body(*refs):
        sh, ld = refs[:n], refs[n:2 * n]
        sems, token = refs[2 * n:2 * n + 2 * nl], refs[-1]
        x, y, c = _position()
        for k in range(n):
            l, a = divmod(k, per_layer)
            r, cw = shards[k].shape
            src = sh[k].at[pl.ds(c * (r // 2), r // 2), :]
            dst = _weight_window(ld[k], cols[k], r, cw, x, y, c)
            for j, peer in enumerate(_peers(x, y, c)):
                pltpu.make_async_remote_copy(src_ref=src, dst_ref=dst, send_sem=sems[2 * l].at[4 * a + j],
                                             recv_sem=sems[2 * l + 1].at[4 * a + j], device_id=peer,
                                             device_id_type=MESH).start()
        token[...] = jnp.zeros_like(token)

    arrs = list(shards) + list(lands)
    out = pl.pallas_call(
        body, name=name,
        out_shape=tuple(pltpu.SemaphoreType.DMA((per_layer * 4,)) for _ in range(2 * nl))
        + tuple(pltpu.HBM(a.shape, a.dtype) for a in arrs) + (TOKEN,),
        in_specs=[HBM_SPEC] * (2 * n),
        out_specs=(SEM_SPEC,) * (2 * nl) + (HBM_SPEC,) * (2 * n) + (pl.BlockSpec(memory_space=pltpu.VMEM),),
        input_output_aliases={i: 2 * nl + i for i in range(2 * n)}, compiler_params=SPLIT_PARAMS,
    )(*[_hbm(a) for a in arrs])
    return out[:2 * nl], out[2 * nl:2 * nl + n], out[2 * nl + n:2 * nl + 2 * n], out[-1]


def _gather_wait(shards, lands, send, recv, after, cols, first, name):
    m = len(shards)

    def body(*refs):
        sh, ld = refs[:m], refs[m:2 * m]
        send_ref, recv_ref = refs[2 * m], refs[2 * m + 1]
        x, y, c = _position()
        for a in range(m):
            r, cw = shards[a].shape
            src = sh[a].at[pl.ds(c * (r // 2), r // 2), :]
            for j, (px, py, pc) in enumerate(_peers(x, y, c)):
                cp = pltpu.make_async_remote_copy(
                    src_ref=src, dst_ref=_weight_window(ld[a], cols[a], r, cw, px, py, pc),
                    send_sem=send_ref.at[4 * (first + a) + j], recv_sem=recv_ref.at[4 * (first + a) + j],
                    device_id=(px, py, pc),
                    device_id_type=MESH)
                cp.wait_send()
                cp.wait_recv()

    arrs = list(shards) + list(lands)
    out = pl.pallas_call(
        body, name=name, out_shape=tuple(pltpu.HBM(a.shape, a.dtype) for a in arrs),
        in_specs=[HBM_SPEC] * (2 * m) + [SEM_SPEC, SEM_SPEC, ANY], out_specs=(HBM_SPEC,) * (2 * m),
        input_output_aliases={i: i for i in range(2 * m)}, compiler_params=SPLIT_PARAMS,
    )(*arrs, send, recv, after)
    return out[m:]


def _forward_sibling(lands, cols, shard_shapes, name):
    m = len(lands)

    def body(*refs):
        ins, outs = refs[:m], refs[m:2 * m]
        send_sems, recv_sems = refs[2 * m:]
        x, y, c = _position()
        chips = [(1 - x, y), (x, 1 - y), (1 - x, 1 - y)]
        sends = []
        for a in range(m):
            r, cw = shard_shapes[a]
            for j, (px, py) in enumerate(chips):
                cp = pltpu.make_async_remote_copy(
                    src_ref=_weight_window(ins[a], cols[a], r, cw, px, py, c),
                    dst_ref=_weight_window(outs[a], cols[a], r, cw, px, py, c),
                    send_sem=send_sems.at[a, j], recv_sem=recv_sems.at[a, j], device_id=(x, y, 1 - c),
                    device_id_type=MESH)
                cp.start()
                sends.append(cp)
        for a in range(m):
            r, cw = shard_shapes[a]
            for j, (px, py) in enumerate(chips):
                pltpu.make_async_remote_copy(
                    src_ref=_weight_window(ins[a], cols[a], r, cw, px, py, c),
                    dst_ref=_weight_window(outs[a], cols[a], r, cw, px, py, 1 - c),
                    send_sem=send_sems.at[a, j], recv_sem=recv_sems.at[a, j], device_id=(x, y, 1 - c),
                    device_id_type=MESH).wait_recv()
        for cp in sends:
            cp.wait_send()

    return pl.pallas_call(
        body, name=name, out_shape=tuple(jax.ShapeDtypeStruct(a.shape, a.dtype) for a in lands),
        in_specs=[ANY] * m, out_specs=tuple([ANY] * m), input_output_aliases={i: i for i in range(m)},
        scratch_shapes=[pltpu.SemaphoreType.DMA((m, 3)), pltpu.SemaphoreType.DMA((m, 3))],
        compiler_params=pltpu.CompilerParams(vmem_limit_bytes=VMEM_LIMIT_BYTES),
    )(*lands)


def _swap_start(ins, name, other_half=False):
    n = len(ins)
    shape = lambda a: (a.shape[0],) + a.shape[2:] if other_half else a.shape
    lands = [lax.empty(shape(a), a.dtype) for a in ins]

    def body(*refs):
        src, ld = refs[:n], refs[n:2 * n]
        send_ref, recv_ref, token = refs[2 * n], refs[2 * n + 1], refs[-1]
        x, y, c = _position()
        for k in range(n):
            pltpu.make_async_remote_copy(
                src_ref=src[k].at[:, 1 - c] if other_half else src[k], dst_ref=ld[k], send_sem=send_ref.at[k],
                recv_sem=recv_ref.at[k], device_id=(x, y, 1 - c), device_id_type=MESH).start()
        token[...] = jnp.zeros_like(token)

    arrs = list(ins) + lands
    out = pl.pallas_call(
        body, name=name,
        out_shape=(pltpu.SemaphoreType.DMA((n,)), pltpu.SemaphoreType.DMA((n,)))
        + tuple(pltpu.HBM(a.shape, a.dtype) for a in arrs) + (TOKEN,),
        in_specs=[HBM_SPEC] * (2 * n),
        out_specs=(SEM_SPEC, SEM_SPEC) + (HBM_SPEC,) * (2 * n) + (pl.BlockSpec(memory_space=pltpu.VMEM),),
        input_output_aliases={i: 2 + i for i in range(2 * n)}, compiler_params=SPLIT_PARAMS,
    )(*[_hbm(a) for a in arrs])
    return out[0], out[1], out[2:2 + n], out[2 + n:2 + 2 * n], out[-1]


def _swap_wait(ins, lands, send, recv, after, name, other_half=False, first=0):
    n = len(ins)

    def body(*refs):
        src, ld = refs[:n], refs[n:2 * n]
        send_ref, recv_ref = refs[2 * n], refs[2 * n + 1]
        x, y, c = _position()
        for k in range(n):
            cp = pltpu.make_async_remote_copy(
                src_ref=src[k].at[:, 1 - c] if other_half else src[k], dst_ref=ld[k], send_sem=send_ref.at[first + k],
                recv_sem=recv_ref.at[first + k], device_id=(x, y, 1 - c), device_id_type=MESH)
            cp.wait_send()
            cp.wait_recv()

    arrs = list(ins) + list(lands)
    out = pl.pallas_call(
        body, name=name, out_shape=tuple(pltpu.HBM(a.shape, a.dtype) for a in arrs),
        in_specs=[HBM_SPEC] * (2 * n) + [SEM_SPEC, SEM_SPEC, ANY], out_specs=(HBM_SPEC,) * (2 * n),
        input_output_aliases={i: i for i in range(2 * n)}, compiler_params=SPLIT_PARAMS,
    )(*arrs, send, recv, after)
    return out[:n], out[n:]


def _forward_start(lands, cols, shard_shapes, name):
    m = len(lands)

    def body(*refs):
        ld = refs[:m]
        send_ref, recv_ref, token = refs[m], refs[m + 1], refs[-1]
        x, y, c = _position()
        for a in range(m):
            r, cw = shard_shapes[a]
            for j, (px, py) in enumerate([(1 - x, y), (x, 1 - y), (1 - x, 1 - y)]):
                win = _weight_window(ld[a], cols[a], r, cw, px, py, c)
                pltpu.make_async_remote_copy(src_ref=win, dst_ref=win, send_sem=send_ref.at[3 * a + j],
                                             recv_sem=recv_ref.at[3 * a + j], device_id=(x, y, 1 - c),
                                             device_id_type=MESH).start()
        token[...] = jnp.zeros_like(token)

    out = pl.pallas_call(
        body, name=name,
        out_shape=(pltpu.SemaphoreType.DMA((m * 3,)), pltpu.SemaphoreType.DMA((m * 3,)))
        + tuple(pltpu.HBM(a.shape, a.dtype) for a in lands) + (TOKEN,),
        in_specs=[HBM_SPEC] * m,
        out_specs=(SEM_SPEC, SEM_SPEC) + (HBM_SPEC,) * m + (pl.BlockSpec(memory_space=pltpu.VMEM),),
        input_output_aliases={i: 2 + i for i in range(m)}, compiler_params=SPLIT_PARAMS,
    )(*[_hbm(a) for a in lands])
    return out[0], out[1], out[2:2 + m], out[-1]


def _forward_wait(lands, send, recv, after, cols, shard_shapes, name):
    m = len(lands)

    def body(*refs):
        ld = refs[:m]
        send_ref, recv_ref = refs[m], refs[m + 1]
        x, y, c = _position()
        for a in range(m):
            r, cw = shard_shapes[a]
            for j, (px, py) in enumerate([(1 - x, y), (x, 1 - y), (1 - x, 1 - y)]):
                cp = pltpu.make_async_remote_copy(
                    src_ref=_weight_window(ld[a], cols[a], r, cw, px, py, c),
                    dst_ref=_weight_window(ld[a], cols[a], r, cw, px, py, 1 - c),
                    send_sem=send_ref.at[3 * a + j], recv_sem=recv_ref.at[3 * a + j], device_id=(x, y, 1 - c),
                    device_id_type=MESH)
                cp.wait_send()
                cp.wait_recv()

    return pl.pallas_call(
        body, name=name, out_shape=tuple(pltpu.HBM(a.shape, a.dtype) for a in lands),
        in_specs=[HBM_SPEC] * m + [SEM_SPEC, SEM_SPEC, ANY], out_specs=(HBM_SPEC,) * m,
        input_output_aliases={i: i for i in range(m)}, compiler_params=SPLIT_PARAMS,
    )(*lands, send, recv, after)


def _exchange_start(sums, name):
    m = len(sums)
    lands = [lax.empty((3,) + s.shape[1:], s.dtype) for s in sums]

    def body(*refs):
        cs, ld = refs[:m], refs[m:2 * m]
        send_ref, recv_ref, token = refs[2 * m], refs[2 * m + 1], refs[-1]
        x, y, c = _position()
        for a in range(m):
            for j, (px, py) in enumerate([(1 - x, y), (x, 1 - y), (1 - x, 1 - y)]):
                pltpu.make_async_remote_copy(
                    src_ref=cs[a].at[2 * px + py], dst_ref=ld[a].at[j], send_sem=send_ref.at[3 * a + j],
                    recv_sem=recv_ref.at[3 * a + j], device_id=(px, py, c), device_id_type=MESH).start()
        token[...] = jnp.zeros_like(token)

    arrs = list(sums) + lands
    out = pl.pallas_call(
        body, name=name,
        out_shape=(pltpu.SemaphoreType.DMA((m * 3,)), pltpu.SemaphoreType.DMA((m * 3,)))
        + tuple(pltpu.HBM(a.shape, a.dtype) for a in arrs) + (TOKEN,),
        in_specs=[HBM_SPEC] * (2 * m),
        out_specs=(SEM_SPEC, SEM_SPEC) + (HBM_SPEC,) * (2 * m) + (pl.BlockSpec(memory_space=pltpu.VMEM),),
        input_output_aliases={i: 2 + i for i in range(2 * m)}, compiler_params=SPLIT_PARAMS,
    )(*[_hbm(a) for a in arrs])
    return out[0], out[1], out[2:2 + m], out[2 + m:2 + 2 * m], out[-1]


def _exchange_wait(sums, lands, send, recv, after, name):
    m = len(sums)

    def body(*refs):
        cs, ld = refs[:m], refs[m:2 * m]
        send_ref, recv_ref = refs[2 * m], refs[2 * m + 1]
        x, y, c = _position()
        for a in range(m):
            for j, (px, py) in enumerate([(1 - x, y), (x, 1 - y), (1 - x, 1 - y)]):
                cp = pltpu.make_async_remote_copy(
                    src_ref=cs[a].at[2 * px + py], dst_ref=ld[a].at[j], send_sem=send_ref.at[3 * a + j],
                    recv_sem=recv_ref.at[3 * a + j], device_id=(px, py, c), device_id_type=MESH)
                cp.wait_send()
                cp.wait_recv()

    arrs = list(sums) + list(lands)
    out = pl.pallas_call(
        body, name=name, out_shape=tuple(pltpu.HBM(a.shape, a.dtype) for a in arrs),
        in_specs=[HBM_SPEC] * (2 * m) + [SEM_SPEC, SEM_SPEC, ANY], out_specs=(HBM_SPEC,) * (2 * m),
        input_output_aliases={i: i for i in range(2 * m)}, compiler_params=SPLIT_PARAMS,
    )(*arrs, send, recv, after)
    return out[:m], out[m:]


def _place_row(payload, dev, name):
    R = payload.shape[0]
    tr = _pick(R, (512, 448, 384, 256, 192, 128, 64, 32, 16, 8))

    def body(dev_ref, x_ref, o_ref):
        o_ref[...] = x_ref[...]

    return pl.pallas_call(
        body, name=name, out_shape=jax.ShapeDtypeStruct((N_DEV, R, LANES), payload.dtype),
        grid_spec=pltpu.PrefetchScalarGridSpec(
            num_scalar_prefetch=1, grid=(R // tr,),
            in_specs=[pl.BlockSpec((tr, LANES), lambda i, dev_ref: (i, 0))],
            out_specs=pl.BlockSpec((None, tr, LANES), lambda i, dev_ref: (dev_ref[0], i, 0))),
        compiler_params=_params(("arbitrary",)),
    )(dev, payload)


def _others(x, y, c):
    return [(1 - x if fx else x, 1 - y if fy else y, 1 - c if fc else c)
            for fx in (0, 1) for fy in (0, 1) for fc in (0, 1) if fx or fy or fc]


def _broadcast_start(payload, land, name):
    def body(p_ref, l_ref, send_ref, recv_ref, p_thru, l_thru, token):
        x, y, c = _position()
        for j, peer in enumerate(_others(x, y, c)):
            pltpu.make_async_remote_copy(src_ref=p_ref, dst_ref=l_ref.at[4 * x + 2 * y + c], send_sem=send_ref.at[j],
                                         recv_sem=recv_ref.at[j], device_id=peer, device_id_type=MESH).start()
        token[...] = jnp.zeros_like(token)

    n = N_DEV - 1
    return pl.pallas_call(
        body, name=name,
        out_shape=(pltpu.SemaphoreType.DMA((n,)), pltpu.SemaphoreType.DMA((n,)), pltpu.HBM(payload.shape, payload.dtype),
                   pltpu.HBM(land.shape, land.dtype), TOKEN),
        in_specs=[HBM_SPEC, HBM_SPEC],
        out_specs=(SEM_SPEC, SEM_SPEC, HBM_SPEC, HBM_SPEC, pl.BlockSpec(memory_space=pltpu.VMEM)),
        input_output_aliases={0: 2, 1: 3}, compiler_params=SPLIT_PARAMS,
    )(_hbm(payload), _hbm(land))


def _broadcast_wait(payload, land, send, recv, after, name):
    def body(p_ref, l_ref, send_ref, recv_ref, after_ref, p_thru, l_thru):
        x, y, c = _position()
        for j, (px, py, pc) in enumerate(_others(x, y, c)):
            cp = pltpu.make_async_remote_copy(src_ref=p_ref, dst_ref=l_ref.at[4 * px + 2 * py + pc],
                                              send_sem=send_ref.at[j], recv_sem=recv_ref.at[j],
                                              device_id=(px, py, pc), device_id_type=MESH)
            cp.wait_send()
            cp.wait_recv()

    out = pl.pallas_call(
        body, name=name, out_shape=(pltpu.HBM(payload.shape, payload.dtype), pltpu.HBM(land.shape, land.dtype)),
        in_specs=[HBM_SPEC, HBM_SPEC, SEM_SPEC, SEM_SPEC, ANY], out_specs=(HBM_SPEC, HBM_SPEC),
        input_output_aliases={0: 0, 1: 1}, compiler_params=SPLIT_PARAMS,
    )(payload, land, send, recv, after)
    return out[1]


def _vec(a):
    return a.reshape(1, -1)


def _local_step(x, tgt, mod, W, P, get_w=None, on_grads=None):
    Dm = D_MODEL
    G = {k: [] for k in ("gm_w_in", "gm_w_out", "hg_w_in", "hg_w_out", "ffn_w_up", "ffn_w_down")}
    lb_all = _lb_fwd(P["hg_lb"], "lb_fwd")
    saved = []
    xs = x
    y_prev = gate_prev = None
    layer_w = [None] * DEPTH

    def wmm(xa, kind, i, mode, name):
        if layer_w[i] is not None:
            return _mm(xa, layer_w[i][kind], mode, name)
        return _mm(xa, W[kind], mode, name, b_layer=i if kind.startswith("ffn") else i // 2)

    for i in range(DEPTH):
        m = [_vec(mod[i, j * Dm:(j + 1) * Dm]) for j in range(6)]
        sh1, sc1, g1, sh2, sc2, g2 = m
        j = i // 2
        if get_w is not None:
            layer_w[i] = get_w(i, 0, xs if y_prev is None else y_prev)
        xs, h = _norm_fwd(xs, y_prev, gate_prev, _vec(P["norm_g"][i, 0]), sc1, sh1, f"norm_fwd_a{i}")
        rec = dict(x1=xs, h1=h)
        if i % 2 == 0:
            a = wmm(h, "gm_w_in", i, "nn", f"gm_in{i}")
            gated = _gm_mid_fwd(a, _vec(P["gm_ln_g"][j]), _vec(P["gm_ln_b"][j]), P["gm_w_s"][j],
                                P["gm_b_s"][j].reshape(GM_HEADS, GM_BLOCK, 1), f"gm_mid_fwd{i}")
            if get_w is not None:
                layer_w[i].update(get_w(i, 1, gated))
            y1 = wmm(gated, "gm_w_out", i, "nn", f"gm_out{i}")
            rec.update(a=a, act=gated)
        else:
            p = wmm(h, "hg_w_in", i, "nn", f"hg_in{i}")
            o, og, states = _hg_scan_fwd(p, _vec(lb_all[j]), _vec(P["hg_gn_g"][j]), f"hg_scan_fwd{i}")
            if get_w is not None:
                layer_w[i].update(get_w(i, 1, og))
            y1 = wmm(og, "hg_w_out", i, "nn", f"hg_out{i}")
            rec.update(a=p, act=og, o=o, states=states)
        rec["y1"] = y1
        xs, h2 = _norm_fwd(xs, y1, g1, _vec(P["norm_g"][i, 1]), sc2, sh2, f"norm_fwd_b{i}")
        a2 = wmm(h2, "ffn_w_up", i, "nn", f"ffn_up{i}")
        mm_ = _conv_fwd(a2, P["ffn_conv_w"][i], _vec(P["ffn_conv_b"][i]), f"conv_fwd{i}")
        y2 = wmm(mm_, "ffn_w_down", i, "nn", f"ffn_down{i}")
        rec.update(x2=xs, h2=h2, a2=a2, m=mm_, y2=y2, mods=m)
        saved.append(rec)
        y_prev, gate_prev = y2, g2
    dx, dy, loss, s_fg, s_gate = _loss_head(xs, y_prev, gate_prev, _vec(P["final_g"]), tgt, "loss_head")
    small = dict(final_g=s_fg, norm_g=[None] * DEPTH, dmod=[None] * DEPTH, ffn_conv_w=[None] * DEPTH,
                 ffn_conv_b=[None] * DEPTH, gm_ln_g=[None] * 2, gm_ln_b=[None] * 2, gm_w_s=[None] * 2,
                 gm_b_s=[None] * 2, hg_gn_g=[None] * 2, dlb=[None] * 2)
    for i in reversed(range(DEPTH)):
        rec = saved[i]
        sh1, sc1, g1, sh2, sc2, g2 = rec["mods"]
        j = i // 2
        d_g2 = s_gate
        dm = wmm(dy, "ffn_w_down", i, "nt", f"ffn_down_dx{i}")
        G["ffn_w_down"].append(_mm(rec["m"], dy, "tn", f"ffn_down_dw{i}", out_dtype=GRAD_WIRE))
        da2, dcw, dcb = _conv_bwd(rec["a2"], dm, P["ffn_conv_w"][i], _vec(P["ffn_conv_b"][i]), f"conv_bwd{i}")
        small["ffn_conv_w"][i], small["ffn_conv_b"][i] = dcw, dcb
        dh2 = wmm(da2, "ffn_w_up", i, "nt", f"ffn_up_dx{i}")
        G["ffn_w_up"].append(_mm(rec["h2"], da2, "tn", f"ffn_up_dw{i}", out_dtype=GRAD_WIRE, exchange_out=True))
        ng2 = _vec(P["norm_g"][i, 1])
        if on_grads is not None:
            ng2 = ng2 + on_grads(i, {k: G[k][-1] for k in ("ffn_w_up", "ffn_w_down")})
        dx, dy, s_sh2, s_x2, d_g1 = _norm_bwd(rec["x2"], dh2, dx, ng2, sc2, rec["y1"], g1, f"norm_bwd_b{i}")
        d_sc2, d_ng2 = s_x2 * ng2, s_x2 * (1.0 + sc2)
        if i % 2 == 0:
            dgated = wmm(dy, "gm_w_out", i, "nt", f"gm_out_dx{i}")
            G["gm_w_out"].append(_mm(rec["act"], dy, "tn", f"gm_out_dw{i}", out_dtype=GRAD_WIRE))
            da, dws, dbs, dlg, dlbeta = _gm_mid_bwd(
                rec["a"], dgated, _vec(P["gm_ln_g"][j]), _vec(P["gm_ln_b"][j]), P["gm_w_s"][j],
                P["gm_b_s"][j].reshape(GM_HEADS, GM_BLOCK, 1), f"gm_mid_bwd{i}")
            small["gm_w_s"][j], small["gm_b_s"][j] = dws, dbs[:, :GM_HEADS].T
            small["gm_ln_g"][j], small["gm_ln_b"][j] = dlg, dlbeta
            dh1 = wmm(da, "gm_w_in", i, "nt", f"gm_in_dx{i}")
            G["gm_w_in"].append(_mm(rec["h1"], da, "tn", f"gm_in_dw{i}", out_dtype=GRAD_WIRE, exchange_out=True))
        else:
            dog = wmm(dy, "hg_w_out", i, "nt", f"hg_out_dx{i}")
            G["hg_w_out"].append(_mm(rec["act"], dy, "tn", f"hg_out_dw{i}", out_dtype=GRAD_WIRE))
            dp, dlb, dgn = _hg_scan_bwd(rec["a"], _vec(lb_all[j]), _vec(P["hg_gn_g"][j]), rec["o"], dog,
                                        rec["states"], f"hg_scan_bwd{i}")
            small["dlb"][j], small["hg_gn_g"][j] = dlb, dgn
            dh1 = wmm(dp, "hg_w_in", i, "nt", f"hg_in_dx{i}")
            G["hg_w_in"].append(_mm(rec["h1"], dp, "tn", f"hg_in_dw{i}", out_dtype=GRAD_WIRE, exchange_out=True))
        ng1 = _vec(P["norm_g"][i, 0])
        if on_grads is not None:
            mixer = ("gm_w_in", "gm_w_out") if i % 2 == 0 else ("hg_w_in", "hg_w_out")
            ng1 = ng1 + on_grads(i, {k: G[k][-1] for k in mixer})
        if i > 0:
            prev = saved[i - 1]
            dx, dy, s_sh1, s_x1, s_gate = _norm_bwd(rec["x1"], dh1, dx, ng1, sc1, prev["y2"], prev["mods"][5],
                                                    f"norm_bwd_a{i}")
        else:
            dx, s_sh1, s_x1 = _norm_bwd(rec["x1"], dh1, dx, ng1, sc1, None, None, f"norm_bwd_a{i}")
        d_sc1, d_ng1 = s_x1 * ng1, s_x1 * (1.0 + sc1)
        small["norm_g"][i] = jnp.concatenate([d_ng1, d_ng2], axis=0)
        small["dmod"][i] = jnp.concatenate([s_sh1, d_sc1, d_g1, s_sh2, d_sc2, d_g2], axis=1)
    for k in G:
        G[k] = G[k][::-1]
    dlb_all = jnp.concatenate(small.pop("dlb"), axis=0)
    small["hg_lb"] = _lb_bwd(P["hg_lb"], dlb_all, "lb_bwd")
    return loss, dx, G, small


BIG = ("gm_w_in", "gm_w_out", "hg_w_in", "hg_w_out", "ffn_w_up", "ffn_w_down")
COL_SHARDED = dict(gm_w_in=True, gm_w_out=False, hg_w_in=True, hg_w_out=False, ffn_w_up=True, ffn_w_down=False)
LAYER_WEIGHTS = 4


def _layer_kinds(i):
    return (("gm_w_in", "gm_w_out") if i % 2 == 0 else ("hg_w_in", "hg_w_out")) + ("ffn_w_up", "ffn_w_down")


def _pack(pieces):
    flat = [p.reshape(-1).astype(F32) for p in pieces]
    offs, tot = [], 0
    for f in flat:
        offs.append((tot, f.shape[0]))
        tot += f.shape[0]
    padded = -(-tot // (8 * LANES)) * (8 * LANES)
    if padded > tot:
        flat.append(jnp.zeros((padded - tot,), F32))
    return jnp.concatenate(flat).reshape(-1, LANES), offs


def _unpack(rows, offs, shapes):
    lead = rows.shape[:-2]
    flat = rows.reshape(lead + (-1,))
    return [flat[..., o:o + n].reshape(lead + tuple(s)) for (o, n), s in zip(offs, shapes)]


def _from_chips(per_dev, axis):
    per_chip = per_dev[0::2]
    return jnp.concatenate([per_chip[s] for s in range(N_CHIPS)], axis=axis)


def kernel(x, c, gm_w_in, gm_ln_g, gm_ln_b, gm_w_s, gm_b_s, gm_w_out, hg_w_in, hg_lb, hg_gn_g, hg_w_out, ffn_w_up, ffn_conv_w, ffn_conv_b, ffn_w_down, norm_g, ada_w, ada_b, final_g, loss_target, m_gm_w_in, m_gm_ln_g, m_gm_ln_b, m_gm_w_s, m_gm_b_s, m_gm_w_out, m_hg_w_in, m_hg_lb, m_hg_gn_g, m_hg_w_out, m_ffn_w_up, m_ffn_conv_w, m_ffn_conv_b, m_ffn_w_down, m_norm_g, m_ada_w, m_ada_b, m_final_g, v_gm_w_in, v_gm_ln_g, v_gm_ln_b, v_gm_w_s, v_gm_b_s, v_gm_w_out, v_hg_w_in, v_hg_lb, v_hg_gn_g, v_hg_w_out, v_ffn_w_up, v_ffn_conv_w, v_ffn_conv_b, v_ffn_w_down, v_norm_g, v_ada_w, v_ada_b, v_final_g):
    Dm = D_MODEL
    xi, yi, ci = _position()
    chip = 2 * xi + yi
    dev = 4 * xi + 2 * yi + ci
    weights = dict(gm_w_in=gm_w_in, gm_ln_g=gm_ln_g, gm_ln_b=gm_ln_b, gm_w_s=gm_w_s, gm_b_s=gm_b_s,
                   gm_w_out=gm_w_out, hg_w_in=hg_w_in, hg_lb=hg_lb, hg_gn_g=hg_gn_g, hg_w_out=hg_w_out,
                   ffn_w_up=ffn_w_up, ffn_conv_w=ffn_conv_w, ffn_conv_b=ffn_conv_b, ffn_w_down=ffn_w_down,
                   norm_g=norm_g, ada_w=ada_w, ada_b=ada_b, final_g=final_g)
    mom_m = dict(gm_w_in=m_gm_w_in, gm_ln_g=m_gm_ln_g, gm_ln_b=m_gm_ln_b, gm_w_s=m_gm_w_s, gm_b_s=m_gm_b_s,
                 gm_w_out=m_gm_w_out, hg_w_in=m_hg_w_in, hg_lb=m_hg_lb, hg_gn_g=m_hg_gn_g, hg_w_out=m_hg_w_out,
                 ffn_w_up=m_ffn_w_up, ffn_conv_w=m_ffn_conv_w, ffn_conv_b=m_ffn_conv_b, ffn_w_down=m_ffn_w_down,
                 norm_g=m_norm_g, ada_w=m_ada_w, ada_b=m_ada_b, final_g=m_final_g)
    mom_v = dict(gm_w_in=v_gm_w_in, gm_ln_g=v_gm_ln_g, gm_ln_b=v_gm_ln_b, gm_w_s=v_gm_w_s, gm_b_s=v_gm_b_s,
                 gm_w_out=v_gm_w_out, hg_w_in=v_hg_w_in, hg_lb=v_hg_lb, hg_gn_g=v_hg_gn_g, hg_w_out=v_hg_w_out,
                 ffn_w_up=v_ffn_w_up, ffn_conv_w=v_ffn_conv_w, ffn_conv_b=v_ffn_conv_b, ffn_w_down=v_ffn_w_down,
                 norm_g=v_norm_g, ada_w=v_ada_w, ada_b=v_ada_b, final_g=v_final_g)
    order = list(weights)

    pos = jnp.stack([chip, ci]).astype(jnp.int32)
    shards, by_col = [], []
    for i in range(DEPTH):
        for k in _layer_kinds(i):
            shards.append(weights[k][i if k.startswith("ffn") else i // 2].astype(BF16))
            by_col.append(COL_SHARDED[k])
    placed = [_place_own(sh, col, pos, f"place_own{n}") for n, (sh, col) in enumerate(zip(shards, by_col))]
    first, rest = slice(0, LAYER_WEIGHTS), slice(LAYER_WEIGHTS, None)
    gsems, sh_thru, ld_thru, _ = _gather_start(shards[first], placed[first], by_col[first], LAYER_WEIGHTS,
                                               "gather_start0")
    more = _gather_start(shards[rest], placed[rest], by_col[rest], LAYER_WEIGHTS, "gather_start")
    gsems, sh_thru, ld_thru = gsems + more[0], sh_thru + more[1], ld_thru + more[2]

    pieces = [c, hg_lb, hg_gn_g, norm_g, ffn_conv_w]
    payload, offs = _pack(pieces)
    got = _allgather_small(payload, "gather_small")
    c_g, lb_g, gn_g, ng_g, cw_g = _unpack(got, offs, [p.shape for p in pieces])
    c_all = c_g.reshape(N_DEV, Dm)
    P = dict(hg_lb=_from_chips(lb_g, 1), hg_gn_g=_from_chips(gn_g, 1), norm_g=_from_chips(ng_g, 2),
             ffn_conv_w=_from_chips(cw_g, 2), gm_ln_g=gm_ln_g, gm_ln_b=gm_ln_b, gm_w_s=gm_w_s, gm_b_s=gm_b_s,
             ffn_conv_b=ffn_conv_b, final_g=final_g)

    cols = ada_w.shape[2]
    ada_b_sh = lax.dynamic_slice_in_dim(ada_b, chip * cols, cols, axis=1)
    mod_sh = _ada_fwd(c_all, ada_w, ada_b_sh, "ada_fwd")
    mod_g = _allgather_small(mod_sh.reshape(-1, LANES), "gather_mod").reshape(N_DEV, DEPTH, N_DEV, cols)
    mod_mine = lax.dynamic_index_in_dim(mod_g[0::2], dev, axis=2, keepdims=False)
    mod = jnp.transpose(mod_mine, (1, 0, 2)).reshape(DEPTH, N_CHIPS * cols)

    core = jnp.reshape(ci, (1,)).astype(jnp.int32)
    chip_arr = jnp.reshape(chip, (1,)).astype(jnp.int32)
    pending, held, prefetched, swapping = [], {}, {}, []

    def get_w(i, group, after):
        lo, hi = LAYER_WEIGHTS * i, LAYER_WEIGHTS * (i + 1)
        shapes = lambda s: [a.shape for a in shards[s]]
        out = {}
        if i == 0:
            s = slice(lo, lo + 1) if group == 0 else slice(lo + 1, hi)
            landed = _gather_wait(sh_thru[s], ld_thru[s], gsems[0], gsems[1], after, by_col[s], s.start - lo,
                                  f"gather_wait0_{group}")
            full = _forward_sibling(landed, by_col[s], shapes(s), f"gather_forward0_{group}")
            out = dict(zip(_layer_kinds(0)[s.start - lo:s.stop - lo], full))
        elif group == 0:
            s = slice(lo, hi)
            send, recv, lands = prefetched.pop(i)
            full = _forward_wait(lands, send, recv, after, by_col[s], shapes(s), f"gather_forward_wait{i}")
            out = dict(zip(_layer_kinds(i), full))
        if group == 1 and i + 1 < DEPTH:
            s = slice(hi, hi + LAYER_WEIGHTS)
            landed = _gather_wait(sh_thru[s], ld_thru[s], gsems[2 * i + 2], gsems[2 * i + 3], after, by_col[s], 0,
                                  f"gather_wait{i + 1}")
            send, recv, lands, _ = _forward_start(landed, by_col[s], shapes(s), f"gather_forward_start{i + 1}")
            prefetched[i + 1] = (send, recv, lands)
        return out

    def on_grads(i, gdict):
        if i > 0 and "ffn_w_up" in gdict:
            held[i] = gdict
            return 0.0
        gdict = {**held.pop(i, {}), **gdict}
        kinds = [k for k in _layer_kinds(i) if k in gdict]
        tag = f"{i}_ffn" if kinds[0] == "ffn_w_up" else f"{i}"
        g4 = []
        for k in kinds:
            g = gdict[k]
            if not COL_SHARDED[k]:
                R, C = g.shape
                g = g.reshape(N_CHIPS, 2, R // (2 * N_CHIPS), C)
            g4.append(g)
        token = finish_swap(g4[0]) if swapping else 0.0
        if i == 0:
            from_sib = _swap_sibling(g4, f"reduce_swap{tag}", other_half=True)
            return token + start_exchange(tag, i, kinds, g4, from_sib)
        send, recv, g_thru, lands, tok = _swap_start(g4, f"reduce_swap_start{tag}", other_half=True)
        swapping.append((tag, i, kinds, send, recv, g_thru, lands))
        return token + tok[0, 0]

    def start_exchange(tag, i, kinds, g4, from_sib):
        sums = [_add_own_half(g, r, core, f"chip_sum_{k}{i}") for g, r, k in zip(g4, from_sib, kinds)]
        send, recv, sums_thru, lands, token = _exchange_start(sums, f"reduce_start{tag}")
        pending.append((tag, i, kinds, send, recv, sums_thru, lands))
        return token[0, 0]

    def finish_swap(after):
        tag, i, kinds, send, recv, g_thru, lands = swapping.pop()
        g4, from_sib = _swap_wait(g_thru, lands, send, recv, after, f"reduce_swap_wait{tag}", other_half=True)
        return start_exchange(tag, i, kinds, g4, from_sib)

    loss_part, dx, G, small = _local_step(x[0], loss_target[0], mod, None, P, get_w, on_grads)

    sum_pieces = [loss_part[:, :1], small["final_g"], jnp.stack(small["gm_ln_g"]), jnp.stack(small["gm_ln_b"]),
                  jnp.stack(small["gm_w_s"]), jnp.stack(small["gm_b_s"]), jnp.stack(small["ffn_conv_b"]),
                  small["hg_lb"], jnp.stack(small["hg_gn_g"]), jnp.stack(small["norm_g"]),
                  jnp.stack(small["ffn_conv_w"])]
    dmod = jnp.concatenate(small["dmod"], axis=0)
    payload2, offs2 = _pack(sum_pieces + [dmod])
    placed2 = _place_row(payload2, jnp.reshape(dev, (1,)).astype(jnp.int32), "place_grads")
    bsend, brecv, p2_thru, l2_thru, small_token = _broadcast_start(payload2, placed2, "gather_grads_start")

    landed = {}
    for tag, i, kinds, send, recv, sums_thru, lands in pending:
        sums_i, lands_i = _exchange_wait(sums_thru, lands, send, recv, small_token, f"reduce_wait{tag}")
        for k, s_, l_ in zip(kinds, sums_i, lands_i):
            landed[(k, i)] = (l_, s_)
    own_halves = []
    for k in BIG:
        layers = [landed[(k, i)] for i in range(DEPTH) if (k, i) in landed]
        own_halves.append(_sum_chips([l_ for l_, _ in layers], [s_ for _, s_ in layers], chip_arr, f"sum_chips_{k}"))
    jsend, jrecv, own_thru, jlands, after = _swap_start(own_halves, "reduce_join_start")
    grads, deltas, new_m, new_v = {}, {}, {}, {}
    for n, k in enumerate(BIG):
        (own,), (recv,) = _swap_wait([own_thru[n]], [jlands[n]], jsend, jrecv, after, f"reduce_join_wait_{k}",
                                     first=n)
        grads[k], deltas[k], new_m[k], new_v[k] = _adamw_halves(
            weights[k], own, recv, mom_m[k], mom_v[k], core, f"adamw_{k}")
        after = new_v[k]

    got2 = _broadcast_wait(p2_thru, l2_thru, bsend, brecv, new_v[BIG[-1]], "gather_grads_wait")
    dmod_all = _unpack(got2, offs2[-1:], [dmod.shape])[0]
    summed = _sum_devices(got2, "sum_devices")
    (loss_s, d_final_g, d_ln_g, d_ln_b, d_ws, d_bs, d_cb, d_lb, d_gn, d_ng, d_cw) = _unpack(
        summed, offs2[:-1], [(1,), final_g.shape, gm_ln_g.shape, gm_ln_b.shape, gm_w_s.shape, gm_b_s.shape,
                             ffn_conv_b.shape, (2, Dm), (2, Dm), (DEPTH, 2, Dm), (DEPTH, 3, 2 * FFN_HIDDEN)])
    grads.update(final_g=d_final_g, gm_ln_g=d_ln_g, gm_ln_b=d_ln_b, gm_w_s=d_ws, gm_b_s=d_bs, ffn_conv_b=d_cb)
    grads["hg_lb"] = lax.dynamic_slice_in_dim(d_lb, chip * hg_lb.shape[1], hg_lb.shape[1], axis=1)
    grads["hg_gn_g"] = lax.dynamic_slice_in_dim(d_gn, chip * hg_gn_g.shape[1], hg_gn_g.shape[1], axis=1)
    grads["norm_g"] = lax.dynamic_slice_in_dim(d_ng, chip * norm_g.shape[2], norm_g.shape[2], axis=2)
    grads["ffn_conv_w"] = lax.dynamic_slice_in_dim(d_cw, chip * ffn_conv_w.shape[2], ffn_conv_w.shape[2], axis=2)
    dmod_sh = lax.dynamic_slice_in_dim(dmod_all, chip * cols, cols, axis=2)
    grads["ada_w"] = _ada_bwd(c_all, jnp.transpose(dmod_sh, (1, 0, 2)), "ada_bwd")
    grads["ada_b"] = _sum_devices(dmod_all.reshape(N_DEV, -1, LANES), "sum_ada_b").reshape(ada_b.shape)

    for k in order:
        if k in BIG:
            continue
        w = weights[k]
        shp = w.shape
        view = (-1, shp[-1]) if w.ndim > 1 else (8, -1)
        d, m2, v2 = _adamw(w.reshape(view), grads[k].reshape(view), mom_m[k].reshape(view), mom_v[k].reshape(view),
                           f"adamw_{k}")
        deltas[k], new_m[k], new_v[k] = d.reshape(shp), m2.reshape(shp), v2.reshape(shp)
        grads[k] = grads[k].reshape(shp)

    loss = loss_s.reshape(())
    return (loss, dx[None], *[grads[k] for k in order], *[deltas[k] for k in order],
            *[new_m[k] for k in order], *[new_v[k] for k in order])
```

```python
import functools

import jax
import jax.numpy as jnp
from jax import lax
from jax.experimental import pallas as pl
from jax.experimental.pallas import tpu as pltpu

F32 = jnp.float32
BF16 = jnp.bfloat16
HI = lax.Precision.HIGHEST
X3 = lax.Precision.HIGH
GRAD_WIRE = BF16
MESH = pl.DeviceIdType.MESH

D_MODEL = 1024
DEPTH = 4
EPS = 1e-6
GM_WIDTH = 2048
GM_HEADS = 8
GM_BLOCK = 128
GM_HEAD_DIM = 256
CHUNK = 64
HG_HEADS = 8
HG_DIM = 128
FFN_HIDDEN = 2816
N_CHIPS = 4
N_DEV = 8

ADAM_LR = 0.001
ADAM_B1 = 0.9
ADAM_B2 = 0.999
ADAM_EPS = 1e-08
ADAM_WD = 0.01
ADAM_STEP = 10

VMEM_LIMIT_BYTES = 56 * 1024 * 1024
ROW_TILE = 256
NORM_ROW_TILE = 512
LANES = 128

_SQRT_HALF = 0.7071067811865476
_INV_SQRT_2PI = 0.3989422804014327


def _pick(dim, prefs):
    for p in prefs:
        if dim % p == 0:
            return p
    return dim


def _params(sem):
    return pltpu.CompilerParams(dimension_semantics=sem, vmem_limit_bytes=VMEM_LIMIT_BYTES)


def _cdf(x):
    return 0.5 * (1.0 + lax.erf(x * _SQRT_HALF))


def _pdf(x):
    return jnp.exp(-0.5 * x * x) * _INV_SQRT_2PI


def _sig(x):
    return jax.nn.sigmoid(x)


def _dot(a, b, dims, prec=None):
    return lax.dot_general(a, b, (dims, ((), ())), precision=prec, preferred_element_type=F32)


NN = ((1,), (0,))
NT = ((1,), (1,))
TN = ((0,), (0,))


MM_VMEM_BUDGET = 40 * 1024 * 1024


def _mm_tiles(mode, M, N, K, a_bytes, b_bytes, exchange_out):
    tn = _pick(N, (1408, 1024, 512, 256, 128))
    tms = [t for t in (1408, 1024, 512, 256, 128) if M % t == 0 and not (exchange_out and (M // 2) % t)] or [M]
    tks = [K] + [t for t in (2816, 2048, 1408, 1024, 512, 256, 128) if t < K and K % t == 0]

    def fits(tm, tk):
        acc = tm * tn * 4 if tk < K else 0
        return 2 * tm * tk * a_bytes + 2 * tk * tn * b_bytes + 2 * tm * tn * 4 + acc <= MM_VMEM_BUDGET

    for min_tm in (min(512, tms[0]), 0):
        for tk in tks:
            for tm in tms:
                if tm >= min_tm and fits(tm, tk):
                    return tm, tn, tk
    return tms[-1], tn, tks[-1]


def _mm(a, b, mode, name, b_layer=None, out_dtype=F32, exchange_out=False):
    b2 = b.shape[-2:]
    if mode == "nn":
        (M, K), (_, N) = a.shape, b2
    elif mode == "nt":
        (M, K), (N, _) = a.shape, b2
    else:
        (K, M), (_, N) = a.shape, b2
    tm, tn, tk = _mm_tiles(mode, M, N, K, a.dtype.itemsize, b.dtype.itemsize, exchange_out)
    nk = K // tk
    dims = {"nn": NN, "nt": NT, "tn": TN}[mode]

    def body(a_ref, b_ref, o_ref, *scratch):
        part = _dot(a_ref[...].astype(BF16), b_ref[...].astype(BF16), dims)
        if nk == 1:
            o_ref[...] = part.astype(o_ref.dtype)
            return
        (acc_ref,) = scratch
        k = pl.program_id(2)

        @pl.when(k == 0)
        def _():
            acc_ref[...] = part

        @pl.when(k > 0)
        def _():
            acc_ref[...] += part

        @pl.when(k == nk - 1)
        def _():
            o_ref[...] = acc_ref[...].astype(o_ref.dtype)

    if mode == "tn":
        a_spec = pl.BlockSpec((tk, tm), lambda i, j, k: (k, i))
    else:
        a_spec = pl.BlockSpec((tm, tk), lambda i, j, k: (i, k))
    bblk = (tk, tn) if mode in ("nn", "tn") else (tn, tk)
    bidx = (lambda i, j, k: (k, j)) if mode in ("nn", "tn") else (lambda i, j, k: (j, k))
    if b_layer is None:
        b_spec = pl.BlockSpec(bblk, bidx)
    else:
        b_spec = pl.BlockSpec((None,) + bblk, lambda i, j, k: (b_layer,) + bidx(i, j, k))
    if exchange_out:
        mh, cw = M // 2, N // N_CHIPS
        assert mh % tm == 0 and cw % tn == 0
        out_shape = jax.ShapeDtypeStruct((N_CHIPS, 2, mh, cw), out_dtype)
        o_spec = pl.BlockSpec(
            (None, None, tm, tn),
            lambda i, j, k: (j // (cw // tn), i // (mh // tm), i % (mh // tm), j % (cw // tn)))
    else:
        out_shape = jax.ShapeDtypeStruct((M, N), out_dtype)
        o_spec = pl.BlockSpec((tm, tn), lambda i, j, k: (i, j))
    return pl.pallas_call(
        body, name=name, out_shape=out_shape, grid=(M // tm, N // tn, nk),
        in_specs=[a_spec, b_spec], out_specs=o_spec,
        scratch_shapes=[] if nk == 1 else [pltpu.VMEM((tm, tn), F32)],
        compiler_params=_params(("parallel", "parallel", "arbitrary")),
    )(a, b)


def _row_spec(tr, width):
    return pl.BlockSpec((tr, width), lambda i: (i, 0))


def _vec_spec(width, rows=1):
    return pl.BlockSpec((rows, width), lambda i: (0, 0))


def _norm_fwd(x, y, gate, g, sc, sh, name):
    T, Dm = x.shape
    tr = _pick(T, (NORM_ROW_TILE, ROW_TILE))
    has_res = y is not None

    def body(*refs):
        if has_res:
            x_ref, y_ref, gate_ref, g_ref, sc_ref, sh_ref, xo_ref, h_ref = refs
            xv = x_ref[...] + gate_ref[...] * y_ref[...]
            xo_ref[...] = xv
        else:
            x_ref, g_ref, sc_ref, sh_ref, h_ref = refs
            xv = x_ref[...]
        rstd = lax.rsqrt(jnp.mean(xv * xv, axis=-1, keepdims=True) + EPS)
        h_ref[...] = ((xv * rstd * g_ref[...]) * (1.0 + sc_ref[...]) + sh_ref[...]).astype(BF16)

    row, vec = _row_spec(tr, Dm), _vec_spec(Dm)
    if has_res:
        ins, in_specs = (x, y, gate, g, sc, sh), [row, row, vec, vec, vec, vec]
        out_shape = (jax.ShapeDtypeStruct((T, Dm), F32), jax.ShapeDtypeStruct((T, Dm), BF16))
        out_specs = (row, row)
    else:
        ins, in_specs = (x, g, sc, sh), [row, vec, vec, vec]
        out_shape = jax.ShapeDtypeStruct((T, Dm), BF16)
        out_specs = row
    out = pl.pallas_call(body, name=name, out_shape=out_shape, grid=(T // tr,), in_specs=in_specs,
                         out_specs=out_specs, compiler_params=_params(("parallel",)))(*ins)
    return out if has_res else (x, out)


def _norm_bwd(x, dh, dxo, g, sc, y_prev, gate_prev, name):
    T, Dm = x.shape
    tr = _pick(T, (NORM_ROW_TILE, ROW_TILE))
    has_prev = y_prev is not None

    def body(*refs):
        if has_prev:
            x_ref, dh_ref, dxo_ref, g_ref, sc_ref, yp_ref, gp_ref, dx_ref, dyp_ref, s1_ref, s2_ref, s3_ref = refs
        else:
            x_ref, dh_ref, dxo_ref, g_ref, sc_ref, dx_ref, s1_ref, s2_ref = refs

        @pl.when(pl.program_id(0) == 0)
        def _():
            s1_ref[...] = jnp.zeros_like(s1_ref)
            s2_ref[...] = jnp.zeros_like(s2_ref)
            if has_prev:
                s3_ref[...] = jnp.zeros_like(s3_ref)

        xv = x_ref[...]
        rstd = lax.rsqrt(jnp.mean(xv * xv, axis=-1, keepdims=True) + EPS)
        xhat = xv * rstd
        dh = dh_ref[...]
        dxhat = dh * (g_ref[...] * (1.0 + sc_ref[...]))
        dx = dxo_ref[...] + rstd * (dxhat - xhat * jnp.mean(dxhat * xhat, axis=-1, keepdims=True))
        dx_ref[...] = dx
        s1_ref[...] += jnp.sum(dh, axis=0, keepdims=True)
        s2_ref[...] += jnp.sum(dh * xhat, axis=0, keepdims=True)
        if has_prev:
            dyp_ref[...] = (gp_ref[...] * dx).astype(BF16)
            s3_ref[...] += jnp.sum(dx * yp_ref[...], axis=0, keepdims=True)

    row, vec = _row_spec(tr, Dm), _vec_spec(Dm)
    vshape = jax.ShapeDtypeStruct((1, Dm), F32)
    if has_prev:
        ins, in_specs = (x, dh, dxo, g, sc, y_prev, gate_prev), [row, row, row, vec, vec, row, vec]
        out_shape = (jax.ShapeDtypeStruct((T, Dm), F32), jax.ShapeDtypeStruct((T, Dm), BF16), vshape, vshape, vshape)
        out_specs = (row, row, vec, vec, vec)
    else:
        ins, in_specs = (x, dh, dxo, g, sc), [row, row, row, vec, vec]
        out_shape = (jax.ShapeDtypeStruct((T, Dm), F32), vshape, vshape)
        out_specs = (row, vec, vec)
    return pl.pallas_call(body, name=name, out_shape=out_shape, grid=(T // tr,), in_specs=in_specs,
                          out_specs=out_specs, compiler_params=_params(("arbitrary",)))(*ins)


def _loss_head(x, y, gate, fg, tgt, name):
    T, Dm = x.shape
    tr = _pick(T, (NORM_ROW_TILE, ROW_TILE))
    nsteps = T // tr

    def body(x_ref, y_ref, gate_ref, fg_ref, t_ref, dx_ref, dy_ref, loss_ref, sfg_ref, sg_ref, acc_ref):
        i = pl.program_id(0)

        @pl.when(i == 0)
        def _():
            acc_ref[...] = jnp.zeros_like(acc_ref)
            sfg_ref[...] = jnp.zeros_like(sfg_ref)
            sg_ref[...] = jnp.zeros_like(sg_ref)

        yv = y_ref[...]
        xv = x_ref[...] + gate_ref[...] * yv
        rstd = lax.rsqrt(jnp.mean(xv * xv, axis=-1, keepdims=True) + EPS)
        xhat = xv * rstd
        err = xhat * fg_ref[...] - t_ref[...]
        acc_ref[...] += jnp.sum(err * err, axis=0, keepdims=True)
        dyn = err * (1.0 / Dm)
        sfg_ref[...] += jnp.sum(dyn * xhat, axis=0, keepdims=True)
        dxhat = dyn * fg_ref[...]
        dx = rstd * (dxhat - xhat * jnp.mean(dxhat * xhat, axis=-1, keepdims=True))
        dx_ref[...] = dx
        dy_ref[...] = (gate_ref[...] * dx).astype(BF16)
        sg_ref[...] += jnp.sum(dx * yv, axis=0, keepdims=True)

        @pl.when(i == nsteps - 1)
        def _():
            total = jnp.sum(acc_ref[...], axis=1, keepdims=True) * (0.5 / Dm)
            loss_ref[...] = jnp.broadcast_to(total, loss_ref.shape)

    row, vec = _row_spec(tr, Dm), _vec_spec(Dm)
    vshape = jax.ShapeDtypeStruct((1, Dm), F32)
    return pl.pallas_call(
        body, name=name, grid=(nsteps,),
        out_shape=(jax.ShapeDtypeStruct((T, Dm), F32), jax.ShapeDtypeStruct((T, Dm), BF16),
                   jax.ShapeDtypeStruct((1, LANES), F32), vshape, vshape),
        in_specs=[row, row, vec, vec, row], out_specs=(row, row, _vec_spec(LANES), vec, vec),
        scratch_shapes=[pltpu.VMEM((1, Dm), F32)], compiler_params=_params(("arbitrary",)),
    )(x, y, gate, fg, tgt)


def _spatial_mask():
    r = lax.broadcasted_iota(jnp.int32, (GM_BLOCK, GM_BLOCK), 0) // CHUNK
    c = lax.broadcasted_iota(jnp.int32, (GM_BLOCK, GM_BLOCK), 1) // CHUNK
    return r >= c


def _gm_specs(tr):
    return [_row_spec(tr, 2 * GM_WIDTH), _vec_spec(GM_WIDTH), _vec_spec(GM_WIDTH),
            pl.BlockSpec((GM_HEADS, GM_BLOCK, GM_BLOCK), lambda i: (0, 0, 0)),
            pl.BlockSpec((GM_HEADS, GM_BLOCK, 1), lambda i: (0, 0, 0))]


def _gm_mid_fwd(a, ln_g, ln_b, ws, bs3, name):
    T = a.shape[0]
    tr = _pick(T, (ROW_TILE,))
    W = GM_WIDTH

    def body(a_ref, lg_ref, lb_ref, ws_ref, bs_ref, o_ref, vn_scr):
        av = a_ref[:, W:]
        v = av * _cdf(av)
        vc = v - jnp.mean(v, axis=-1, keepdims=True)
        rstd = lax.rsqrt(jnp.mean(vc * vc, axis=-1, keepdims=True) + EPS)
        vn_scr[...] = (vc * rstd * lg_ref[...] + lb_ref[...]).astype(BF16)
        mask = _spatial_mask()
        for h in range(GM_HEADS):
            w = jnp.where(mask, ws_ref[h], 0.0).astype(BF16)
            cs = slice(h * GM_HEAD_DIM, (h + 1) * GM_HEAD_DIM)
            for blk in range(tr // GM_BLOCK):
                rs = slice(blk * GM_BLOCK, (blk + 1) * GM_BLOCK)
                s = _dot(w, vn_scr[rs, cs], NN) + bs_ref[h]
                au = a_ref[rs, cs]
                o_ref[rs, cs] = (au * _cdf(au) * s).astype(BF16)

    return pl.pallas_call(
        body, name=name, out_shape=jax.ShapeDtypeStruct((T, W), BF16), grid=(T // tr,),
        in_specs=_gm_specs(tr), out_specs=_row_spec(tr, W),
        scratch_shapes=[pltpu.VMEM((tr, W), BF16)], compiler_params=_params(("parallel",)),
    )(a, ln_g, ln_b, ws, bs3)


def _gm_mid_bwd(a, dgated, ln_g, ln_b, ws, bs3, name):
    T = a.shape[0]
    tr = _pick(T, (ROW_TILE,))
    W = GM_WIDTH
    nsteps = T // tr

    def body(a_ref, dg_ref, lg_ref, lb_ref, ws_ref, bs_ref, da_ref, dws_ref, dbs_ref, dlg_ref, dlb_ref,
             vn_scr, vhat_scr, dvn_scr, dsum_scr):
        i = pl.program_id(0)

        @pl.when(i == 0)
        def _():
            dws_ref[...] = jnp.zeros_like(dws_ref)
            dbs_ref[...] = jnp.zeros_like(dbs_ref)
            dlg_ref[...] = jnp.zeros_like(dlg_ref)
            dlb_ref[...] = jnp.zeros_like(dlb_ref)
            dsum_scr[...] = jnp.zeros_like(dsum_scr)

        av = a_ref[:, W:]
        cdf_v = _cdf(av)
        v = av * cdf_v
        vc = v - jnp.mean(v, axis=-1, keepdims=True)
        rstd = lax.rsqrt(jnp.mean(vc * vc, axis=-1, keepdims=True) + EPS)
        vhat_scr[...] = vc * rstd
        vn_scr[...] = (vhat_scr[...] * lg_ref[...] + lb_ref[...]).astype(BF16)
        mask = _spatial_mask()
        for h in range(GM_HEADS):
            w = jnp.where(mask, ws_ref[h], 0.0).astype(BF16)
            cs = slice(h * GM_HEAD_DIM, (h + 1) * GM_HEAD_DIM)
            for blk in range(tr // GM_BLOCK):
                rs = slice(blk * GM_BLOCK, (blk + 1) * GM_BLOCK)
                vnb = vn_scr[rs, cs]
                s = _dot(w, vnb, NN) + bs_ref[h]
                au = a_ref[rs, cs]
                cdf_u = _cdf(au)
                dg = dg_ref[rs, cs]
                ds = dg * (au * cdf_u)
                da_ref[rs, cs] = (dg * s * (cdf_u + au * _pdf(au))).astype(BF16)
                dsb = ds.astype(BF16)
                dvn_scr[rs, cs] = _dot(w, dsb, TN)
                dws_ref[h] += _dot(dsb, vnb, NT)
                dsum_scr[:, cs] += ds
        dvn = dvn_scr[...]
        vhat = vhat_scr[...]
        dlg_ref[...] += jnp.sum(dvn * vhat, axis=0, keepdims=True)
        dlb_ref[...] += jnp.sum(dvn, axis=0, keepdims=True)
        dvh = dvn * lg_ref[...]
        dv = rstd * (dvh - jnp.mean(dvh, axis=-1, keepdims=True)
                     - vhat * jnp.mean(dvh * vhat, axis=-1, keepdims=True))
        da_ref[:, W:] = (dv * (cdf_v + av * _pdf(av))).astype(BF16)

        @pl.when(i == nsteps - 1)
        def _():
            for h in range(GM_HEADS):
                dws_ref[h] = jnp.where(mask, dws_ref[h], 0.0)
            col_head = lax.broadcasted_iota(jnp.int32, (W, GM_BLOCK), 0) // GM_HEAD_DIM
            sel = (col_head == lax.broadcasted_iota(jnp.int32, (W, GM_BLOCK), 1)).astype(F32)
            dbs_ref[...] = _dot(dsum_scr[...], sel, NN, HI)

    vshape = jax.ShapeDtypeStruct((1, W), F32)
    return pl.pallas_call(
        body, name=name, grid=(nsteps,),
        out_shape=(jax.ShapeDtypeStruct((T, 2 * W), BF16), jax.ShapeDtypeStruct((GM_HEADS, GM_BLOCK, GM_BLOCK), F32),
                   jax.ShapeDtypeStruct((GM_BLOCK, GM_BLOCK), F32), vshape, vshape),
        in_specs=[_gm_specs(tr)[0], _row_spec(tr, W)] + _gm_specs(tr)[1:],
        out_specs=(_row_spec(tr, 2 * W), pl.BlockSpec((GM_HEADS, GM_BLOCK, GM_BLOCK), lambda i: (0, 0, 0)),
                   pl.BlockSpec((GM_BLOCK, GM_BLOCK), lambda i: (0, 0)), _vec_spec(W), _vec_spec(W)),
        scratch_shapes=[pltpu.VMEM((tr, W), BF16), pltpu.VMEM((tr, W), F32), pltpu.VMEM((tr, W), F32),
                        pltpu.VMEM((GM_BLOCK, W), F32)],
        compiler_params=_params(("arbitrary",)),
    )(a, dgated, ln_g, ln_b, ws, bs3)


SUB = 16
EXP_CLAMP = 80.0


def _tri(lower):
    r = lax.broadcasted_iota(jnp.int32, (CHUNK, CHUNK), 0)
    c = lax.broadcasted_iota(jnp.int32, (CHUNK, CHUNK), 1)
    return (r >= c) if lower else (c >= r)


def _score_masks():
    i = lax.broadcasted_iota(jnp.int32, (CHUNK, CHUNK), 0)
    j = lax.broadcasted_iota(jnp.int32, (CHUNK, CHUNK), 1)
    bi, bj = i // SUB, j // SUB
    diag = (bi == bj) & (i >= j)
    pair = (bi % 2 == 1) & (bj == bi - 1)
    half = (i >= CHUNK // 2) & (j < CHUNK // 2)
    return diag, pair, half


def _dot01(m, x):
    x1 = x.astype(BF16)
    rest = x - x1.astype(F32)
    x2 = rest.astype(BF16)
    x3 = (rest - x2.astype(F32)).astype(BF16)
    return _dot(m, x1, NN) + (_dot(m, x2, NN) + _dot(m, x3, NN))


def _block_rows(b, offset):
    parts = []
    for blk in range(0, CHUNK, SUB):
        r = blk + offset
        parts.append(jnp.zeros((SUB, b.shape[1]), F32) if r < 0 else jnp.broadcast_to(b[r:r + 1], (SUB, b.shape[1])))
    return jnp.concatenate(parts, axis=0)


def _hg_gates(p_ref, lb_ref, lower):
    Dm = D_MODEL
    heads = []
    for h in range(HG_HEADS):
        c0 = h * HG_DIM
        qr = p_ref[:, c0:c0 + HG_DIM]
        fz = p_ref[:, Dm + c0:Dm + c0 + HG_DIM]
        lbh = lb_ref[:, c0:c0 + HG_DIM]
        sg = _sig(fz)
        f = lbh + (1.0 - lbh) * sg
        sq = _sig(qr)
        heads.append(dict(qr=qr, v=p_ref[:, 2 * Dm + c0:2 * Dm + c0 + HG_DIM],
                          gt=p_ref[:, 3 * Dm + c0:3 * Dm + c0 + HG_DIM], lbh=lbh, sg=sg, f=f, gl=jnp.log(f),
                          kk=1.0 - f, sq=sq, q=qr * sq))
    for g in heads:
        g["b"] = _dot01(lower, g.pop("gl"))
    for g in heads:
        g.update(_hg_scalings(g["q"], g["kk"], g.pop("b")))
    return heads


def _hg_scalings(q, kk, b):
    r_mid = _block_rows(b, SUB // 2 - 1)
    r_prev = _block_rows(b, -1)
    r_end = _block_rows(b, SUB - 1)
    r_half = jnp.broadcast_to(b[CHUNK // 2 - 1:CHUNK // 2], b.shape)
    bc = b[CHUNK - 1:CHUNK]
    eqs = (jnp.exp(jnp.clip(b - r_mid, -EXP_CLAMP, EXP_CLAMP)), jnp.exp(jnp.minimum(b - r_prev, 0.0)),
           jnp.exp(jnp.minimum(b - r_half, 0.0)))
    eks = (jnp.exp(jnp.clip(r_mid - b, -EXP_CLAMP, EXP_CLAMP)), jnp.exp(jnp.minimum(r_end - b, 0.0)),
           jnp.exp(jnp.minimum(r_half - b, 0.0)))
    eb = jnp.exp(b)
    ec = jnp.exp(bc - b)
    return dict(eqs=eqs, eks=eks, eb=eb, ec=ec, e_end=jnp.exp(bc), qs=[q * e for e in eqs],
                ks=[kk * e for e in eks], qe=q * eb, ke=kk * ec)


def _scores(g, masks):
    a = None
    for qs, ks, m in zip(g["qs"], g["ks"], masks):
        part = jnp.where(m, _dot(qs.astype(BF16), ks.astype(BF16), NT), 0.0)
        a = part if a is None else a + part
    return a


def _hg_scan_fwd(p, lb, gn, name):
    T = p.shape[0]
    nc = T // CHUNK
    Dm = D_MODEL

    def body(p_ref, lb_ref, gn_ref, o_ref, og_ref, so_ref, st_ref):
        @pl.when(pl.program_id(0) == 0)
        def _():
            st_ref[...] = jnp.zeros_like(st_ref)

        masks = _score_masks()
        heads = _hg_gates(p_ref, lb_ref, _tri(True).astype(BF16))
        states = [st_ref[h] for h in range(HG_HEADS)]
        scores = [_scores(g, masks) for g in heads]
        outs = [_dot(a.astype(BF16), g["v"].astype(BF16), NN) + _dot(g["qe"], st, NT, X3)
                for g, a, st in zip(heads, scores, states)]
        new_states = [st * g["e_end"] + _dot(g["v"], g["ke"], TN, X3) for g, st in zip(heads, states)]
        for h, (g, o, st, st2) in enumerate(zip(heads, outs, states, new_states)):
            cs = slice(h * HG_DIM, (h + 1) * HG_DIM)
            so_ref[0, h] = st
            st_ref[h] = st2
            o_ref[:, cs] = o
            r = lax.rsqrt(jnp.mean(o * o, axis=-1, keepdims=True) + EPS)
            gt = g["gt"]
            og_ref[:, cs] = (((o * r) * gn_ref[:, cs]).astype(F32) * (gt * _sig(gt))).astype(BF16)

    return pl.pallas_call(
        body, name=name, grid=(nc,),
        out_shape=(jax.ShapeDtypeStruct((T, Dm), F32), jax.ShapeDtypeStruct((T, Dm), BF16),
                   jax.ShapeDtypeStruct((nc, HG_HEADS, HG_DIM, HG_DIM), F32)),
        in_specs=[_row_spec(CHUNK, 4 * Dm), _vec_spec(Dm), _vec_spec(Dm)],
        out_specs=(_row_spec(CHUNK, Dm), _row_spec(CHUNK, Dm),
                   pl.BlockSpec((1, HG_HEADS, HG_DIM, HG_DIM), lambda i: (i, 0, 0, 0))),
        scratch_shapes=[pltpu.VMEM((HG_HEADS, HG_DIM, HG_DIM), F32)],
        compiler_params=_params(("arbitrary",)),
    )(p, lb, gn)


def _hg_scan_bwd(p, lb, gn, o, dog, states, name):
    T = p.shape[0]
    nc = T // CHUNK
    Dm = D_MODEL

    def rev(i):
        return nc - 1 - i

    def body(p_ref, lb_ref, gn_ref, o_ref, dog_ref, st_in_ref, dp_ref, dlb_ref, dgn_ref, dst_ref, carry_ref):
        @pl.when(pl.program_id(0) == 0)
        def _():
            dst_ref[...] = jnp.zeros_like(dst_ref)
            carry_ref[...] = jnp.zeros_like(carry_ref)
            dlb_ref[...] = jnp.zeros_like(dlb_ref)
            dgn_ref[...] = jnp.zeros_like(dgn_ref)

        upper = _tri(False).astype(BF16)
        masks = _score_masks()
        heads = _hg_gates(p_ref, lb_ref, _tri(True).astype(BF16))
        for h, g in enumerate(heads):
            cs = slice(h * HG_DIM, (h + 1) * HG_DIM)
            oh = o_ref[:, cs]
            r = lax.rsqrt(jnp.mean(oh * oh, axis=-1, keepdims=True) + EPS)
            on = oh * r
            gt = g["gt"]
            sgt = _sig(gt)
            sil = gt * sgt
            dogh = dog_ref[:, cs]
            gnh = gn_ref[:, cs]
            don = dogh * gnh * sil
            g["dgn"] = jnp.sum(dogh * on * sil, axis=0, keepdims=True)
            g["dgate"] = dogh * on * gnh * (sgt * (1.0 + gt * (1.0 - sgt)))
            g["do"] = r * (don - on * jnp.mean(don * on, axis=-1, keepdims=True))
            g["dst"] = dst_ref[h]
            g["st"] = st_in_ref[0, h]
            g["carry"] = carry_ref[h]
        for g in heads:
            g["a"] = _scores(g, masks)
            g["dob"] = g["do"].astype(BF16)
            g["da"] = _dot(g["dob"], g["v"].astype(BF16), NT)
        for g in heads:
            g["dv"] = _dot(g["a"].astype(BF16), g["dob"], TN) + _dot(g["ke"].astype(BF16), g["dst"].astype(BF16), NT)
            g["dq"] = _dot(g["do"], g["st"], NN, X3) * g["eb"]
            g["dk"] = _dot(g["v"], g["dst"], NN, X3) * g["ec"]
            g["dst2"] = g["dst"] * g["e_end"] + _dot(g["do"], g["qe"], TN, X3)
        for lvl in range(3):
            for g in heads:
                dam = jnp.where(masks[lvl], g["da"], 0.0)
                g["dq"] = g["dq"] + _dot(dam, g["ks"][lvl], NN, X3) * g["eqs"][lvl]
                g["dk"] = g["dk"] + _dot(dam, g["qs"][lvl], TN, X3) * g["eks"][lvl]
        for g in heads:
            g["dgd"] = g["q"] * g["dq"] - g["kk"] * g["dk"]
            g["dgl"] = _dot01(upper, g["dgd"]) + g["carry"]
        for h, g in enumerate(heads):
            c0 = h * HG_DIM
            cs = slice(c0, c0 + HG_DIM)
            df = g["dgl"] / g["f"] - g["dk"]
            sg, sq, qr = g["sg"], g["sq"], g["qr"]
            dst_ref[h] = g["dst2"]
            carry_ref[h] = g["carry"] + jnp.sum(g["dgd"], axis=0, keepdims=True)
            dgn_ref[:, cs] += g["dgn"]
            dlb_ref[:, cs] += jnp.sum(df * (1.0 - sg), axis=0, keepdims=True)
            dp_ref[:, c0:c0 + HG_DIM] = (g["dq"] * (sq * (1.0 + qr * (1.0 - sq)))).astype(BF16)
            dp_ref[:, Dm + c0:Dm + c0 + HG_DIM] = (df * (1.0 - g["lbh"]) * sg * (1.0 - sg)).astype(BF16)
            dp_ref[:, 2 * Dm + c0:2 * Dm + c0 + HG_DIM] = g["dv"].astype(BF16)
            dp_ref[:, 3 * Dm + c0:3 * Dm + c0 + HG_DIM] = g["dgate"].astype(BF16)

    vshape = jax.ShapeDtypeStruct((1, Dm), F32)
    rrow = lambda w: pl.BlockSpec((CHUNK, w), lambda i: (rev(i), 0))
    return pl.pallas_call(
        body, name=name, grid=(nc,),
        out_shape=(jax.ShapeDtypeStruct((T, 4 * Dm), BF16), vshape, vshape),
        in_specs=[rrow(4 * Dm), _vec_spec(Dm), _vec_spec(Dm), rrow(Dm), rrow(Dm),
                  pl.BlockSpec((1, HG_HEADS, HG_DIM, HG_DIM), lambda i: (rev(i), 0, 0, 0))],
        out_specs=(rrow(4 * Dm), _vec_spec(Dm), _vec_spec(Dm)),
        scratch_shapes=[pltpu.VMEM((HG_HEADS, HG_DIM, HG_DIM), F32), pltpu.VMEM((HG_HEADS, 1, HG_DIM), F32)],
        compiler_params=_params(("arbitrary",)),
    )(p, lb, gn, o, dog, states)


def _lb_fwd(hg_lb, name):
    def body(a_ref, o_ref):
        a0, a1 = a_ref[0:1], a_ref[1:2]
        m = jnp.maximum(a0, a1)
        e0, e1 = jnp.exp(a0 - m), jnp.exp(a1 - m)
        p0, p1 = e0 / (e0 + e1), e1 / (e0 + e1)
        o_ref[0:1] = p0 - p0
        o_ref[1:2] = (p0 + p1) - p0

    return pl.pallas_call(body, name=name, out_shape=jax.ShapeDtypeStruct(hg_lb.shape, F32))(hg_lb)


def _lb_bwd(hg_lb, dlb_all, name):
    def body(a_ref, d_ref, o_ref):
        a0, a1 = a_ref[0:1], a_ref[1:2]
        m = jnp.maximum(a0, a1)
        e0, e1 = jnp.exp(a0 - m), jnp.exp(a1 - m)
        p0, p1 = e0 / (e0 + e1), e1 / (e0 + e1)
        d1 = d_ref[1:2]
        o_ref[0:1] = -p0 * p1 * d1
        o_ref[1:2] = p1 * (1.0 - p1) * d1

    return pl.pallas_call(body, name=name, out_shape=jax.ShapeDtypeStruct(hg_lb.shape, F32))(hg_lb, dlb_all)


CONV_COLS_FWD = 256
CONV_COLS_BWD = 128


def _conv_fwd(a, w, b, name):
    T = a.shape[0]
    Fh = FFN_HIDDEN
    tr = _pick(T, (ROW_TILE,))
    cw = CONV_COLS_FWD
    hb = tr // 8

    def body(a_ref, ap_ref, w_ref, b_ref, m_ref):
        m0 = (pl.program_id(0) > 0).astype(F32)

        def conv(cc):
            x = jnp.concatenate([ap_ref[:, pl.ds(cc, cw)] * m0, a_ref[:, pl.ds(cc, cw)]], axis=0)
            wv = w_ref[:, pl.ds(cc, cw)]
            y = b_ref[:, pl.ds(cc, cw)] + wv[2:3] * x + wv[1:2] * pltpu.roll(x, 1, axis=0) \
                + wv[0:1] * pltpu.roll(x, 2, axis=0)
            return y[8:]

        def step(c, carry):
            c0 = pl.multiple_of(c * cw, cw)
            c1 = pl.multiple_of(Fh + c * cw, cw)
            yg, yv = conv(c0), conv(c1)
            m_ref[:, pl.ds(c0, cw)] = (yg * _cdf(yg) * yv).astype(BF16)
            return carry

        lax.fori_loop(0, Fh // cw, step, 0)

    return pl.pallas_call(
        body, name=name, out_shape=jax.ShapeDtypeStruct((T, Fh), BF16), grid=(T // tr,),
        in_specs=[_row_spec(tr, 2 * Fh), pl.BlockSpec((8, 2 * Fh), lambda i: (jnp.maximum(i * hb - 1, 0), 0)),
                  _vec_spec(2 * Fh, 3), _vec_spec(2 * Fh)],
        out_specs=_row_spec(tr, Fh), compiler_params=_params(("parallel",)),
    )(a, a, w, b)


def _conv_bwd(a, dm, w, b, name):
    T = a.shape[0]
    Fh = FFN_HIDDEN
    tr = _pick(T, (ROW_TILE,))
    cw = CONV_COLS_BWD
    hb = tr // 8
    nsteps = T // tr
    n = tr + 8

    def body(a_ref, ap_ref, an_ref, dm_ref, dmn_ref, w_ref, b_ref, da_ref, dw_ref, db_ref):
        i = pl.program_id(0)
        m0 = (i > 0).astype(F32)
        m1 = (i < nsteps - 1).astype(F32)

        @pl.when(i == 0)
        def _():
            dw_ref[...] = jnp.zeros_like(dw_ref)
            db_ref[...] = jnp.zeros_like(db_ref)

        def prep(cc):
            x = jnp.concatenate([ap_ref[:, pl.ds(cc, cw)] * m0, a_ref[:, pl.ds(cc, cw)],
                                 an_ref[:, pl.ds(cc, cw)] * m1], axis=0)
            wv = w_ref[:, pl.ds(cc, cw)]
            s1 = pltpu.roll(x, 1, axis=0)
            s2 = pltpu.roll(x, 2, axis=0)
            y = b_ref[:, pl.ds(cc, cw)] + wv[2:3] * x + wv[1:2] * s1 + wv[0:1] * s2
            return wv, x[8:], s1[8:], s2[8:], y[8:]

        def back(cc, dy, wv, x0, s1, s2):
            da = wv[2:3] * dy + wv[1:2] * pltpu.roll(dy, n - 1, axis=0) + wv[0:1] * pltpu.roll(dy, n - 2, axis=0)
            da_ref[:, pl.ds(cc, cw)] = da[:tr].astype(BF16)
            d = dy[:tr]
            db_ref[:, pl.ds(cc, cw)] += jnp.sum(d, axis=0, keepdims=True)
            dw_ref[2:3, pl.ds(cc, cw)] += jnp.sum(d * x0[:tr], axis=0, keepdims=True)
            dw_ref[1:2, pl.ds(cc, cw)] += jnp.sum(d * s1[:tr], axis=0, keepdims=True)
            dw_ref[0:1, pl.ds(cc, cw)] += jnp.sum(d * s2[:tr], axis=0, keepdims=True)

        def step(c, carry):
            c0 = pl.multiple_of(c * cw, cw)
            c1 = pl.multiple_of(Fh + c * cw, cw)
            dmx = jnp.concatenate([dm_ref[:, pl.ds(c0, cw)], dmn_ref[:, pl.ds(c0, cw)] * m1], axis=0)
            wg, xg, s1g, s2g, yg = prep(c0)
            wv, xv, s1v, s2v, yv = prep(c1)
            cg = _cdf(yg)
            back(c0, dmx * yv * (cg + yg * _pdf(yg)), wg, xg, s1g, s2g)
            back(c1, dmx * (yg * cg), wv, xv, s1v, s2v)
            return carry

        lax.fori_loop(0, Fh // cw, step, 0)

    prev = lambda wd: pl.BlockSpec((8, wd), lambda i: (jnp.maximum(i * hb - 1, 0), 0))
    nxt = lambda wd: pl.BlockSpec((8, wd), lambda i: (jnp.minimum((i + 1) * hb, T // 8 - 1), 0))
    return pl.pallas_call(
        body, name=name, grid=(nsteps,),
        out_shape=(jax.ShapeDtypeStruct((T, 2 * Fh), BF16), jax.ShapeDtypeStruct((3, 2 * Fh), F32),
                   jax.ShapeDtypeStruct((1, 2 * Fh), F32)),
        in_specs=[_row_spec(tr, 2 * Fh), prev(2 * Fh), nxt(2 * Fh), _row_spec(tr, Fh), nxt(Fh),
                  _vec_spec(2 * Fh, 3), _vec_spec(2 * Fh)],
        out_specs=(_row_spec(tr, 2 * Fh), _vec_spec(2 * Fh, 3), _vec_spec(2 * Fh)),
        compiler_params=_params(("arbitrary",)),
    )(a, a, a, dm, dm, w, b)


def _ada_fwd(c_all, ada_w, ada_b, name):
    L, Dm, cols = ada_w.shape
    tn = _pick(cols, (512, 256, 128))

    def body(c_ref, w_ref, b_ref, o_ref):
        cv = c_ref[...]
        cond = (cv * _sig(cv)).astype(BF16)
        o_ref[...] = _dot(cond, w_ref[...].astype(BF16), NN) + b_ref[...]

    return pl.pallas_call(
        body, name=name, out_shape=jax.ShapeDtypeStruct((L, N_DEV, cols), F32), grid=(L, cols // tn),
        in_specs=[pl.BlockSpec((N_DEV, Dm), lambda l, j: (0, 0)), pl.BlockSpec((None, Dm, tn), lambda l, j: (l, 0, j)),
                  pl.BlockSpec((None, 1, tn), lambda l, j: (l, 0, j))],
        out_specs=pl.BlockSpec((None, N_DEV, tn), lambda l, j: (l, 0, j)),
        compiler_params=_params(("parallel", "parallel")),
    )(c_all, ada_w, ada_b.reshape(L, 1, cols))


def _ada_bwd(c_all, dmod, name):
    L, _, cols = dmod.shape
    Dm = c_all.shape[1]
    tn = _pick(cols, (512, 256, 128))

    def body(c_ref, d_ref, o_ref):
        cv = c_ref[...]
        o_ref[...] = _dot(cv * _sig(cv), d_ref[...], TN, HI)

    return pl.pallas_call(
        body, name=name, out_shape=jax.ShapeDtypeStruct((L, Dm, cols), F32), grid=(L, cols // tn),
        in_specs=[pl.BlockSpec((N_DEV, Dm), lambda l, j: (0, 0)), pl.BlockSpec((None, N_DEV, tn), lambda l, j: (l, 0, j))],
        out_specs=pl.BlockSpec((None, Dm, tn), lambda l, j: (l, 0, j)),
        compiler_params=_params(("parallel", "parallel")),
    )(c_all, dmod)


def _add_own_half(g4, rb, core, name):
    S, _, rh, cw = g4.shape
    tr = _pick(rh, (256, 128, 176, 64))

    def body(core_ref, g_ref, r_ref, o_ref):
        o_ref[...] = (g_ref[...].astype(F32) + r_ref[...].astype(F32)).astype(GRAD_WIRE)

    return pl.pallas_call(
        body, name=name, out_shape=jax.ShapeDtypeStruct((S, rh, cw), BF16),
        grid_spec=pltpu.PrefetchScalarGridSpec(
            num_scalar_prefetch=1, grid=(S, rh // tr),
            in_specs=[pl.BlockSpec((None, None, tr, cw), lambda s, i, core_ref: (s, core_ref[0], i, 0)),
                      pl.BlockSpec((None, tr, cw), lambda s, i, core_ref: (s, i, 0))],
            out_specs=pl.BlockSpec((None, tr, cw), lambda s, i, core_ref: (s, i, 0))),
        compiler_params=_params(("parallel", "parallel")),
    )(core, g4, rb)


def _sum_chips(lands, sums, chip, name):
    L = len(lands)
    _, rh, cw = lands[0].shape
    tr = _pick(rh, (256, 128, 176, 64))

    def body(chip_ref, *refs):
        ld, cs, o_ref = refs[:L], refs[L:2 * L], refs[2 * L]
        me = chip_ref[0]
        for k in range(L):
            @pl.when(pl.program_id(0) == k)
            def _(k=k):
                own = cs[k][...].astype(F32)
                got = [ld[k][j].astype(F32) for j in range(3)]
                acc = None
                for t in range(N_CHIPS):
                    d = jnp.bitwise_xor(jnp.int32(t), me)
                    term = jnp.where(d == 0, own, jnp.where(d == 2, got[0], jnp.where(d == 1, got[1], got[2])))
                    acc = term if acc is None else acc + term
                o_ref[...] = acc

    frozen = lambda l, i, k: jnp.where(l == k, i, 0)
    in_specs = [pl.BlockSpec((3, tr, cw), lambda l, i, chip_ref, k=k: (0, frozen(l, i, k), 0)) for k in range(L)]
    in_specs += [pl.BlockSpec((None, tr, cw), lambda l, i, chip_ref, k=k: (chip_ref[0], frozen(l, i, k), 0))
                 for k in range(L)]
    return pl.pallas_call(
        body, name=name, out_shape=jax.ShapeDtypeStruct((L, rh, cw), F32),
        grid_spec=pltpu.PrefetchScalarGridSpec(
            num_scalar_prefetch=1, grid=(L, rh // tr), in_specs=in_specs,
            out_specs=pl.BlockSpec((None, tr, cw), lambda l, i, chip_ref: (l, i, 0))),
        compiler_params=_params(("arbitrary", "arbitrary")),
    )(chip, *lands, *sums)


def _sum_devices(gathered, name):
    n, R, _ = gathered.shape
    tr = _pick(R, (512, 448, 384, 256, 192, 128, 64, 32, 16, 8))

    def body(g_ref, o_ref):
        acc = g_ref[0]
        for d in range(1, n):
            acc = acc + g_ref[d]
        o_ref[...] = acc

    return pl.pallas_call(
        body, name=name, out_shape=jax.ShapeDtypeStruct((R, LANES), F32), grid=(R // tr,),
        in_specs=[pl.BlockSpec((n, tr, LANES), lambda i: (0, i, 0))], out_specs=pl.BlockSpec((tr, LANES), lambda i: (i, 0)),
        compiler_params=_params(("parallel",)),
    )(gathered)


def _adamw(w, g, m, v, name):
    R, C = w.shape
    tr = _pick(R, (256, 128, 64, 32, 16, 8))
    c1 = 1.0 / (1.0 - ADAM_B1 ** ADAM_STEP)
    c2 = 1.0 / (1.0 - ADAM_B2 ** ADAM_STEP)

    def body(w_ref, g_ref, m_ref, v_ref, d_ref, mo_ref, vo_ref):
        gv = g_ref[...]
        m2 = ADAM_B1 * m_ref[...] + (1.0 - ADAM_B1) * gv
        v2 = ADAM_B2 * v_ref[...] + (1.0 - ADAM_B2) * (gv * gv)
        mo_ref[...] = m2
        vo_ref[...] = v2
        d_ref[...] = -ADAM_LR * ((m2 * c1) / (jnp.sqrt(v2 * c2) + ADAM_EPS) + ADAM_WD * w_ref[...])

    spec = pl.BlockSpec((tr, C), lambda i: (i, 0))
    shp = jax.ShapeDtypeStruct((R, C), F32)
    return pl.pallas_call(body, name=name, out_shape=(shp, shp, shp), grid=(R // tr,), in_specs=[spec] * 4,
                          out_specs=(spec, spec, spec), compiler_params=_params(("parallel",)))(w, g, m, v)


def _adamw_halves(w, own, recv, m, v, core, name):
    L, rh, cw = own.shape
    tr = _pick(rh, (256, 128, 176, 64))
    c1 = 1.0 / (1.0 - ADAM_B1 ** ADAM_STEP)
    c2 = 1.0 / (1.0 - ADAM_B2 ** ADAM_STEP)

    def body(core_ref, w_ref, own_ref, recv_ref, m_ref, v_ref, g_ref, d_ref, mo_ref, vo_ref):
        gv = jnp.where(pl.program_id(1) == core_ref[0], own_ref[...], recv_ref[...])
        g_ref[...] = gv
        m2 = ADAM_B1 * m_ref[...] + (1.0 - ADAM_B1) * gv
        v2 = ADAM_B2 * v_ref[...] + (1.0 - ADAM_B2) * (gv * gv)
        mo_ref[...] = m2
        vo_ref[...] = v2
        d_ref[...] = -ADAM_LR * ((m2 * c1) / (jnp.sqrt(v2 * c2) + ADAM_EPS) + ADAM_WD * w_ref[...])

    full = pl.BlockSpec((None, None, tr, cw), lambda l, hf, i, core_ref: (l, hf, i, 0))
    mine = pl.BlockSpec((None, tr, cw), lambda l, hf, i, core_ref: (l, jnp.where(hf == core_ref[0], i, 0), 0))
    other = pl.BlockSpec((None, tr, cw), lambda l, hf, i, core_ref: (l, jnp.where(hf == core_ref[0], 0, i), 0))
    shp = jax.ShapeDtypeStruct((L, 2, rh, cw), F32)
    view = lambda a: a.reshape(L, 2, rh, cw)
    outs = pl.pallas_call(
        body, name=name, out_shape=(shp, shp, shp, shp),
        grid_spec=pltpu.PrefetchScalarGridSpec(
            num_scalar_prefetch=1, grid=(L, 2, rh // tr), in_specs=[full, mine, other, full, full],
            out_specs=(full, full, full, full)),
        compiler_params=_params(("arbitrary", "arbitrary", "arbitrary")),
    )(core, view(w), own, recv, view(m), view(v))
    return tuple(o.reshape(L, 2 * rh, cw) for o in outs)


ANY = pl.BlockSpec(memory_space=pl.ANY)


def _position():
    x, y, c = lax.axis_index("x"), lax.axis_index("y"), lax.axis_index("c")
    return x, y, c


def _allgather(ins, out_shapes, src_fns, dst_fns, name, in_vmem):
    n = len(ins)

    def body(*refs):
        in_refs, out_refs = refs[:n], refs[n:2 * n]
        send_sems, recv_sems, local_sems = refs[2 * n:]
        x, y, c = _position()
        me, sibling = (x, y, c), (x, y, 1 - c)
        chips = [(1 - x, y), (x, 1 - y), (1 - x, 1 - y)]

        def copy(k, j, block, to, own=False):
            dst = dst_fns[k](out_refs[k], *block)
            return pltpu.make_async_remote_copy(
                src_ref=src_fns[k](in_refs[k], c) if own else dst, dst_ref=dst,
                send_sem=send_sems.at[k, j], recv_sem=recv_sems.at[k, j], device_id=to, device_id_type=MESH)

        mine = [pltpu.make_async_copy(src_fns[k](in_refs[k], c), dst_fns[k](out_refs[k], *me), local_sems.at[k])
                for k in range(n)]
        for cp in mine:
            cp.start()
        first = []
        for k in range(n):
            first.append(copy(k, 0, me, sibling, own=True))
            first += [copy(k, 1 + j, me, (*chip, c), own=True) for j, chip in enumerate(chips)]
        for cp in first:
            cp.start()
        passed = []
        for j, chip in enumerate(chips):
            for k in range(n):
                copy(k, 1 + j, (*chip, c), me).wait_recv()
                fwd = copy(k, 4 + j, (*chip, c), sibling)
                fwd.start()
                passed.append(fwd)
        for k in range(n):
            copy(k, 0, sibling, me).wait_recv()
        for j, chip in enumerate(chips):
            for k in range(n):
                copy(k, 4 + j, (*chip, 1 - c), me).wait_recv()
        for cp in first + passed:
            cp.wait_send()
        for cp in mine:
            cp.wait()

    spec = pl.BlockSpec(memory_space=pltpu.VMEM) if in_vmem else ANY
    return pl.pallas_call(
        body, name=name, out_shape=tuple(out_shapes), in_specs=[spec] * n, out_specs=tuple([spec] * n),
        scratch_shapes=[pltpu.SemaphoreType.DMA((n, 7)), pltpu.SemaphoreType.DMA((n, 7)),
                        pltpu.SemaphoreType.DMA((n,))],
        compiler_params=pltpu.CompilerParams(vmem_limit_bytes=VMEM_LIMIT_BYTES),
    )(*ins)


def _allgather_small(payload, name):
    R = payload.shape[0]
    (out,) = _allgather(
        [payload], [jax.ShapeDtypeStruct((N_DEV, R, LANES), F32)],
        [lambda ref, c: ref], [lambda ref, px, py, pc: ref.at[4 * px + 2 * py + pc]], name, in_vmem=True)
    return out


def _swap_sibling(ins, name, other_half=False):
    n = len(ins)

    def body(*refs):
        in_refs, out_refs = refs[:n], refs[n:2 * n]
        send_sems, recv_sems = refs[2 * n:]
        x, y, c = _position()
        copies = [pltpu.make_async_remote_copy(
            src_ref=in_refs[k].at[:, 1 - c] if other_half else in_refs[k], dst_ref=out_refs[k],
            send_sem=send_sems.at[k], recv_sem=recv_sems.at[k],
            device_id=(x, y, 1 - c), device_id_type=MESH) for k in range(n)]
        for cp in copies:
            cp.start()
        for cp in copies:
            cp.wait_recv()
        for cp in copies:
            cp.wait_send()

    shape = lambda a: (a.shape[0],) + a.shape[2:] if other_half else a.shape
    return pl.pallas_call(
        body, name=name, out_shape=tuple(jax.ShapeDtypeStruct(shape(a), a.dtype) for a in ins),
        in_specs=[ANY] * n, out_specs=tuple([ANY] * n),
        scratch_shapes=[pltpu.SemaphoreType.DMA((n,)), pltpu.SemaphoreType.DMA((n,))],
        compiler_params=pltpu.CompilerParams(vmem_limit_bytes=VMEM_LIMIT_BYTES),
    )(*ins)


HBM_SPEC = pl.BlockSpec(memory_space=pltpu.HBM)
SEM_SPEC = pl.BlockSpec(memory_space=pltpu.SEMAPHORE)
SPLIT_PARAMS = pltpu.CompilerParams(has_side_effects=pltpu.SideEffectType.DATAFLOW_SIDE_EFFECTING)
TOKEN = jax.ShapeDtypeStruct((8, LANES), F32)


def _hbm(a):
    return pltpu.with_memory_space_constraint(a, pltpu.HBM)


def _weight_window(ref, col, r, cw, px, py, pc):
    rh = r // 2
    if col:
        return ref.at[pl.ds(pc * rh, rh), pl.ds((2 * px + py) * cw, cw)]
    return ref.at[pl.ds((2 * px + py) * r + pc * rh, rh), :]


def _peers(x, y, c):
    return [(x, y, 1 - c), (1 - x, y, c), (x, 1 - y, c), (1 - x, 1 - y, c)]


def _place_own(shard, col, pos, name):
    r, cw = shard.shape
    rh = r // 2
    tr = _pick(rh, (256, 128, 176, 64))
    nb = rh // tr
    shape = (r, N_CHIPS * cw) if col else (N_CHIPS * r, cw)

    def body(pos_ref, x_ref, o_ref):
        o_ref[...] = x_ref[...]

    if col:
        out_idx = lambda i, pos_ref: (pos_ref[1] * nb + i, pos_ref[0])
    else:
        out_idx = lambda i, pos_ref: (pos_ref[0] * (2 * nb) + pos_ref[1] * nb + i, 0)
    return pl.pallas_call(
        body, name=name, out_shape=jax.ShapeDtypeStruct(shape, shard.dtype),
        grid_spec=pltpu.PrefetchScalarGridSpec(
            num_scalar_prefetch=1, grid=(nb,),
            in_specs=[pl.BlockSpec((tr, cw), lambda i, pos_ref: (pos_ref[1] * nb + i, 0))],
            out_specs=pl.BlockSpec((tr, cw), out_idx)),
        compiler_params=_params(("arbitrary",)),
    )(pos, shard)


def _gather_start(shards, lands, cols, per_layer, name):
    n = len(shards)
    nl = n // per_layer

    def body(*refs):
        sh, ld = refs[:n], refs[n:2 * n]
        sems, token = refs[2 * n:2 * n + 2 * nl], refs[-1]
        x, y, c = _position()
        for k in range(n):
            l, a = divmod(k, per_layer)
            r, cw = shards[k].shape
            src = sh[k].at[pl.ds(c * (r // 2), r // 2), :]
            dst = _weight_window(ld[k], cols[k], r, cw, x, y, c)
            for j, peer in enumerate(_peers(x, y, c)):
                pltpu.make_async_remote_copy(src_ref=src, dst_ref=dst, send_sem=sems[2 * l].at[4 * a + j],
                                             recv_sem=sems[2 * l + 1].at[4 * a + j], device_id=peer,
                                             device_id_type=MESH).start()
        token[...] = jnp.zeros_like(token)

    arrs = list(shards) + list(lands)
    out = pl.pallas_call(
        body, name=name,
        out_shape=tuple(pltpu.SemaphoreType.DMA((per_layer * 4,)) for _ in range(2 * nl))
        + tuple(pltpu.HBM(a.shape, a.dtype) for a in arrs) + (TOKEN,),
        in_specs=[HBM_SPEC] * (2 * n),
        out_specs=(SEM_SPEC,) * (2 * nl) + (HBM_SPEC,) * (2 * n) + (pl.BlockSpec(memory_space=pltpu.VMEM),),
        input_output_aliases={i: 2 * nl + i for i in range(2 * n)}, compiler_params=SPLIT_PARAMS,
    )(*[_hbm(a) for a in arrs])
    return out[:2 * nl], out[2 * nl:2 * nl + n], out[2 * nl + n:2 * nl + 2 * n], out[-1]


def _gather_wait(shards, lands, send, recv, after, cols, first, name):
    m = len(shards)

    def body(*refs):
        sh, ld = refs[:m], refs[m:2 * m]
        send_ref, recv_ref = refs[2 * m], refs[2 * m + 1]
        x, y, c = _position()
        for a in range(m):
            r, cw = shards[a].shape
            src = sh[a].at[pl.ds(c * (r // 2), r // 2), :]
            for j, (px, py, pc) in enumerate(_peers(x, y, c)):
                cp = pltpu.make_async_remote_copy(
                    src_ref=src, dst_ref=_weight_window(ld[a], cols[a], r, cw, px, py, pc),
                    send_sem=send_ref.at[4 * (first + a) + j], recv_sem=recv_ref.at[4 * (first + a) + j],
                    device_id=(px, py, pc),
                    device_id_type=MESH)
                cp.wait_send()
                cp.wait_recv()

    arrs = list(shards) + list(lands)
    out = pl.pallas_call(
        body, name=name, out_shape=tuple(pltpu.HBM(a.shape, a.dtype) for a in arrs),
        in_specs=[HBM_SPEC] * (2 * m) + [SEM_SPEC, SEM_SPEC, ANY], out_specs=(HBM_SPEC,) * (2 * m),
        input_output_aliases={i: i for i in range(2 * m)}, compiler_params=SPLIT_PARAMS,
    )(*arrs, send, recv, after)
    return out[m:]


def _forward_sibling(lands, cols, shard_shapes, name):
    m = len(lands)

    def body(*refs):
        ins, outs = refs[:m], refs[m:2 * m]
        send_sems, recv_sems = refs[2 * m:]
        x, y, c = _position()
        chips = [(1 - x, y), (x, 1 - y), (1 - x, 1 - y)]
        sends = []
        for a in range(m):
            r, cw = shard_shapes[a]
            for j, (px, py) in enumerate(chips):
                cp = pltpu.make_async_remote_copy(
                    src_ref=_weight_window(ins[a], cols[a], r, cw, px, py, c),
                    dst_ref=_weight_window(outs[a], cols[a], r, cw, px, py, c),
                    send_sem=send_sems.at[a, j], recv_sem=recv_sems.at[a, j], device_id=(x, y, 1 - c),
                    device_id_type=MESH)
                cp.start()
                sends.append(cp)
        for a in range(m):
            r, cw = shard_shapes[a]
            for j, (px, py) in enumerate(chips):
                pltpu.make_async_remote_copy(
                    src_ref=_weight_window(ins[a], cols[a], r, cw, px, py, c),
                    dst_ref=_weight_window(outs[a], cols[a], r, cw, px, py, 1 - c),
                    send_sem=send_sems.at[a, j], recv_sem=recv_sems.at[a, j], device_id=(x, y, 1 - c),
                    device_id_type=MESH).wait_recv()
        for cp in sends:
            cp.wait_send()

    return pl.pallas_call(
        body, name=name, out_shape=tuple(jax.ShapeDtypeStruct(a.shape, a.dtype) for a in lands),
        in_specs=[ANY] * m, out_specs=tuple([ANY] * m), input_output_aliases={i: i for i in range(m)},
        scratch_shapes=[pltpu.SemaphoreType.DMA((m, 3)), pltpu.SemaphoreType.DMA((m, 3))],
        compiler_params=pltpu.CompilerParams(vmem_limit_bytes=VMEM_LIMIT_BYTES),
    )(*lands)


def _swap_start(ins, name, other_half=False):
    n = len(ins)
    shape = lambda a: (a.shape[0],) + a.shape[2:] if other_half else a.shape
    lands = [lax.empty(shape(a), a.dtype) for a in ins]

    def body(*refs):
        src, ld = refs[:n], refs[n:2 * n]
        send_ref, recv_ref, token = refs[2 * n], refs[2 * n + 1], refs[-1]
        x, y, c = _position()
        for k in range(n):
            pltpu.make_async_remote_copy(
                src_ref=src[k].at[:, 1 - c] if other_half else src[k], dst_ref=ld[k], send_sem=send_ref.at[k],
                recv_sem=recv_ref.at[k], device_id=(x, y, 1 - c), device_id_type=MESH).start()
        token[...] = jnp.zeros_like(token)

    arrs = list(ins) + lands
    out = pl.pallas_call(
        body, name=name,
        out_shape=(pltpu.SemaphoreType.DMA((n,)), pltpu.SemaphoreType.DMA((n,)))
        + tuple(pltpu.HBM(a.shape, a.dtype) for a in arrs) + (TOKEN,),
        in_specs=[HBM_SPEC] * (2 * n),
        out_specs=(SEM_SPEC, SEM_SPEC) + (HBM_SPEC,) * (2 * n) + (pl.BlockSpec(memory_space=pltpu.VMEM),),
        input_output_aliases={i: 2 + i for i in range(2 * n)}, compiler_params=SPLIT_PARAMS,
    )(*[_hbm(a) for a in arrs])
    return out[0], out[1], out[2:2 + n], out[2 + n:2 + 2 * n], out[-1]


def _swap_wait(ins, lands, send, recv, after, name, other_half=False, first=0):
    n = len(ins)

    def body(*refs):
        src, ld = refs[:n], refs[n:2 * n]
        send_ref, recv_ref = refs[2 * n], refs[2 * n + 1]
        x, y, c = _position()
        for k in range(n):
            cp = pltpu.make_async_remote_copy(
                src_ref=src[k].at[:, 1 - c] if other_half else src[k], dst_ref=ld[k], send_sem=send_ref.at[first + k],
                recv_sem=recv_ref.at[first + k], device_id=(x, y, 1 - c), device_id_type=MESH)
            cp.wait_send()
            cp.wait_recv()

    arrs = list(ins) + list(lands)
    out = pl.pallas_call(
        body, name=name, out_shape=tuple(pltpu.HBM(a.shape, a.dtype) for a in arrs),
        in_specs=[HBM_SPEC] * (2 * n) + [SEM_SPEC, SEM_SPEC, ANY], out_specs=(HBM_SPEC,) * (2 * n),
        input_output_aliases={i: i for i in range(2 * n)}, compiler_params=SPLIT_PARAMS,
    )(*arrs, send, recv, after)
    return out[:n], out[n:]


def _forward_start(lands, cols, shard_shapes, name):
    m = len(lands)

    def body(*refs):
        ld = refs[:m]
        send_ref, recv_ref, token = refs[m], refs[m + 1], refs[-1]
        x, y, c = _position()
        for a in range(m):
            r, cw = shard_shapes[a]
            for j, (px, py) in enumerate([(1 - x, y), (x, 1 - y), (1 - x, 1 - y)]):
                win = _weight_window(ld[a], cols[a], r, cw, px, py, c)
                pltpu.make_async_remote_copy(src_ref=win, dst_ref=win, send_sem=send_ref.at[3 * a + j],
                                             recv_sem=recv_ref.at[3 * a + j], device_id=(x, y, 1 - c),
                                             device_id_type=MESH).start()
        token[...] = jnp.zeros_like(token)

    out = pl.pallas_call(
        body, name=name,
        out_shape=(pltpu.SemaphoreType.DMA((m * 3,)), pltpu.SemaphoreType.DMA((m * 3,)))
        + tuple(pltpu.HBM(a.shape, a.dtype) for a in lands) + (TOKEN,),
        in_specs=[HBM_SPEC] * m,
        out_specs=(SEM_SPEC, SEM_SPEC) + (HBM_SPEC,) * m + (pl.BlockSpec(memory_space=pltpu.VMEM),),
        input_output_aliases={i: 2 + i for i in range(m)}, compiler_params=SPLIT_PARAMS,
    )(*[_hbm(a) for a in lands])
    return out[0], out[1], out[2:2 + m], out[-1]


def _forward_wait(lands, send, recv, after, cols, shard_shapes, name):
    m = len(lands)

    def body(*refs):
        ld = refs[:m]
        send_ref, recv_ref = refs[m], refs[m + 1]
        x, y, c = _position()
        for a in range(m):
            r, cw = shard_shapes[a]
            for j, (px, py) in enumerate([(1 - x, y), (x, 1 - y), (1 - x, 1 - y)]):
                cp = pltpu.make_async_remote_copy(
                    src_ref=_weight_window(ld[a], cols[a], r, cw, px, py, c),
                    dst_ref=_weight_window(ld[a], cols[a], r, cw, px, py, 1 - c),
                    send_sem=send_ref.at[3 * a + j], recv_sem=recv_ref.at[3 * a + j], device_id=(x, y, 1 - c),
                    device_id_type=MESH)
                cp.wait_send()
                cp.wait_recv()

    return pl.pallas_call(
        body, name=name, out_shape=tuple(pltpu.HBM(a.shape, a.dtype) for a in lands),
        in_specs=[HBM_SPEC] * m + [SEM_SPEC, SEM_SPEC, ANY], out_specs=(HBM_SPEC,) * m,
        input_output_aliases={i: i for i in range(m)}, compiler_params=SPLIT_PARAMS,
    )(*lands, send, recv, after)


def _exchange_start(sums, name):
    m = len(sums)
    lands = [lax.empty((3,) + s.shape[1:], s.dtype) for s in sums]

    def body(*refs):
        cs, ld = refs[:m], refs[m:2 * m]
        send_ref, recv_ref, token = refs[2 * m], refs[2 * m + 1], refs[-1]
        x, y, c = _position()
        for a in range(m):
            for j, (px, py) in enumerate([(1 - x, y), (x, 1 - y), (1 - x, 1 - y)]):
                pltpu.make_async_remote_copy(
                    src_ref=cs[a].at[2 * px + py], dst_ref=ld[a].at[j], send_sem=send_ref.at[3 * a + j],
                    recv_sem=recv_ref.at[3 * a + j], device_id=(px, py, c), device_id_type=MESH).start()
        token[...] = jnp.zeros_like(token)

    arrs = list(sums) + lands
    out = pl.pallas_call(
        body, name=name,
        out_shape=(pltpu.SemaphoreType.DMA((m * 3,)), pltpu.SemaphoreType.DMA((m * 3,)))
        + tuple(pltpu.HBM(a.shape, a.dtype) for a in arrs) + (TOKEN,),
        in_specs=[HBM_SPEC] * (2 * m),
        out_specs=(SEM_SPEC, SEM_SPEC) + (HBM_SPEC,) * (2 * m) + (pl.BlockSpec(memory_space=pltpu.VMEM),),
        input_output_aliases={i: 2 + i for i in range(2 * m)}, compiler_params=SPLIT_PARAMS,
    )(*[_hbm(a) for a in arrs])
    return out[0], out[1], out[2:2 + m], out[2 + m:2 + 2 * m], out[-1]


def _exchange_wait(sums, lands, send, recv, after, name):
    m = len(sums)

    def body(*refs):
        cs, ld = refs[:m], refs[m:2 * m]
        send_ref, recv_ref = refs[2 * m], refs[2 * m + 1]
        x, y, c = _position()
        for a in range(m):
            for j, (px, py) in enumerate([(1 - x, y), (x, 1 - y), (1 - x, 1 - y)]):
                cp = pltpu.make_async_remote_copy(
                    src_ref=cs[a].at[2 * px + py], dst_ref=ld[a].at[j], send_sem=send_ref.at[3 * a + j],
                    recv_sem=recv_ref.at[3 * a + j], device_id=(px, py, c), device_id_type=MESH)
                cp.wait_send()
                cp.wait_recv()

    arrs = list(sums) + list(lands)
    out = pl.pallas_call(
        body, name=name, out_shape=tuple(pltpu.HBM(a.shape, a.dtype) for a in arrs),
        in_specs=[HBM_SPEC] * (2 * m) + [SEM_SPEC, SEM_SPEC, ANY], out_specs=(HBM_SPEC,) * (2 * m),
        input_output_aliases={i: i for i in range(2 * m)}, compiler_params=SPLIT_PARAMS,
    )(*arrs, send, recv, after)
    return out[:m], out[m:]


def _place_row(payload, dev, name):
    R = payload.shape[0]
    tr = _pick(R, (512, 448, 384, 256, 192, 128, 64, 32, 16, 8))

    def body(dev_ref, x_ref, o_ref):
        o_ref[...] = x_ref[...]

    return pl.pallas_call(
        body, name=name, out_shape=jax.ShapeDtypeStruct((N_DEV, R, LANES), payload.dtype),
        grid_spec=pltpu.PrefetchScalarGridSpec(
            num_scalar_prefetch=1, grid=(R // tr,),
            in_specs=[pl.BlockSpec((tr, LANES), lambda i, dev_ref: (i, 0))],
            out_specs=pl.BlockSpec((None, tr, LANES), lambda i, dev_ref: (dev_ref[0], i, 0))),
        compiler_params=_params(("arbitrary",)),
    )(dev, payload)


def _others(x, y, c):
    return [(1 - x if fx else x, 1 - y if fy else y, 1 - c if fc else c)
            for fx in (0, 1) for fy in (0, 1) for fc in (0, 1) if fx or fy or fc]


def _broadcast_start(payload, land, name):
    def body(p_ref, l_ref, send_ref, recv_ref, p_thru, l_thru, token):
        x, y, c = _position()
        for j, peer in enumerate(_others(x, y, c)):
            pltpu.make_async_remote_copy(src_ref=p_ref, dst_ref=l_ref.at[4 * x + 2 * y + c], send_sem=send_ref.at[j],
                                         recv_sem=recv_ref.at[j], device_id=peer, device_id_type=MESH).start()
        token[...] = jnp.zeros_like(token)

    n = N_DEV - 1
    return pl.pallas_call(
        body, name=name,
        out_shape=(pltpu.SemaphoreType.DMA((n,)), pltpu.SemaphoreType.DMA((n,)), pltpu.HBM(payload.shape, payload.dtype),
                   pltpu.HBM(land.shape, land.dtype), TOKEN),
        in_specs=[HBM_SPEC, HBM_SPEC],
        out_specs=(SEM_SPEC, SEM_SPEC, HBM_SPEC, HBM_SPEC, pl.BlockSpec(memory_space=pltpu.VMEM)),
        input_output_aliases={0: 2, 1: 3}, compiler_params=SPLIT_PARAMS,
    )(_hbm(payload), _hbm(land))


def _broadcast_wait(payload, land, send, recv, after, name):
    def body(p_ref, l_ref, send_ref, recv_ref, after_ref, p_thru, l_thru):
        x, y, c = _position()
        for j, (px, py, pc) in enumerate(_others(x, y, c)):
            cp = pltpu.make_async_remote_copy(src_ref=p_ref, dst_ref=l_ref.at[4 * px + 2 * py + pc],
                                              send_sem=send_ref.at[j], recv_sem=recv_ref.at[j],
                                              device_id=(px, py, pc), device_id_type=MESH)
            cp.wait_send()
            cp.wait_recv()

    out = pl.pallas_call(
        body, name=name, out_shape=(pltpu.HBM(payload.shape, payload.dtype), pltpu.HBM(land.shape, land.dtype)),
        in_specs=[HBM_SPEC, HBM_SPEC, SEM_SPEC, SEM_SPEC, ANY], out_specs=(HBM_SPEC, HBM_SPEC),
        input_output_aliases={0: 0, 1: 1}, compiler_params=SPLIT_PARAMS,
    )(payload, land, send, recv, after)
    return out[1]


def _vec(a):
    return a.reshape(1, -1)


def _local_step(x, tgt, mod, W, P, get_w=None, on_grads=None):
    Dm = D_MODEL
    G = {k: [] for k in ("gm_w_in", "gm_w_out", "hg_w_in", "hg_w_out", "ffn_w_up", "ffn_w_down")}
    lb_all = _lb_fwd(P["hg_lb"], "lb_fwd")
    saved = []
    xs = x
    y_prev = gate_prev = None
    layer_w = [None] * DEPTH

    def wmm(xa, kind, i, mode, name):
        if layer_w[i] is not None:
            return _mm(xa, layer_w[i][kind], mode, name)
        return _mm(xa, W[kind], mode, name, b_layer=i if kind.startswith("ffn") else i // 2)

    for i in range(DEPTH):
        m = [_vec(mod[i, j * Dm:(j + 1) * Dm]) for j in range(6)]
        sh1, sc1, g1, sh2, sc2, g2 = m
        j = i // 2
        if get_w is not None:
            layer_w[i] = get_w(i, 0, xs if y_prev is None else y_prev)
        xs, h = _norm_fwd(xs, y_prev, gate_prev, _vec(P["norm_g"][i, 0]), sc1, sh1, f"norm_fwd_a{i}")
        rec = dict(x1=xs, h1=h)
        if i % 2 == 0:
            a = wmm(h, "gm_w_in", i, "nn", f"gm_in{i}")
            gated = _gm_mid_fwd(a, _vec(P["gm_ln_g"][j]), _vec(P["gm_ln_b"][j]), P["gm_w_s"][j],
                                P["gm_b_s"][j].reshape(GM_HEADS, GM_BLOCK, 1), f"gm_mid_fwd{i}")
            if get_w is not None:
                layer_w[i].update(get_w(i, 1, gated))
            y1 = wmm(gated, "gm_w_out", i, "nn", f"gm_out{i}")
            rec.update(a=a, act=gated)
        else:
            p = wmm(h, "hg_w_in", i, "nn", f"hg_in{i}")
            o, og, states = _hg_scan_fwd(p, _vec(lb_all[j]), _vec(P["hg_gn_g"][j]), f"hg_scan_fwd{i}")
            if get_w is not None:
                layer_w[i].update(get_w(i, 1, og))
            y1 = wmm(og, "hg_w_out", i, "nn", f"hg_out{i}")
            rec.update(a=p, act=og, o=o, states=states)
        rec["y1"] = y1
        xs, h2 = _norm_fwd(xs, y1, g1, _vec(P["norm_g"][i, 1]), sc2, sh2, f"norm_fwd_b{i}")
        a2 = wmm(h2, "ffn_w_up", i, "nn", f"ffn_up{i}")
        mm_ = _conv_fwd(a2, P["ffn_conv_w"][i], _vec(P["ffn_conv_b"][i]), f"conv_fwd{i}")
        y2 = wmm(mm_, "ffn_w_down", i, "nn", f"ffn_down{i}")
        rec.update(x2=xs, h2=h2, a2=a2, m=mm_, y2=y2, mods=m)
        saved.append(rec)
        y_prev, gate_prev = y2, g2
    dx, dy, loss, s_fg, s_gate = _loss_head(xs, y_prev, gate_prev, _vec(P["final_g"]), tgt, "loss_head")
    small = dict(final_g=s_fg, norm_g=[None] * DEPTH, dmod=[None] * DEPTH, ffn_conv_w=[None] * DEPTH,
                 ffn_conv_b=[None] * DEPTH, gm_ln_g=[None] * 2, gm_ln_b=[None] * 2, gm_w_s=[None] * 2,
                 gm_b_s=[None] * 2, hg_gn_g=[None] * 2, dlb=[None] * 2)
    for i in reversed(range(DEPTH)):
        rec = saved[i]
        sh1, sc1, g1, sh2, sc2, g2 = rec["mods"]
        j = i // 2
        d_g2 = s_gate
        dm = wmm(dy, "ffn_w_down", i, "nt", f"ffn_down_dx{i}")
        G["ffn_w_down"].append(_mm(rec["m"], dy, "tn", f"ffn_down_dw{i}", out_dtype=GRAD_WIRE))
        da2, dcw, dcb = _conv_bwd(rec["a2"], dm, P["ffn_conv_w"][i], _vec(P["ffn_conv_b"][i]), f"conv_bwd{i}")
        small["ffn_conv_w"][i], small["ffn_conv_b"][i] = dcw, dcb
        dh2 = wmm(da2, "ffn_w_up", i, "nt", f"ffn_up_dx{i}")
        G["ffn_w_up"].append(_mm(rec["h2"], da2, "tn", f"ffn_up_dw{i}", out_dtype=GRAD_WIRE, exchange_out=True))
        ng2 = _vec(P["norm_g"][i, 1])
        if on_grads is not None:
            ng2 = ng2 + on_grads(i, {k: G[k][-1] for k in ("ffn_w_up", "ffn_w_down")})
        dx, dy, s_sh2, s_x2, d_g1 = _norm_bwd(rec["x2"], dh2, dx, ng2, sc2, rec["y1"], g1, f"norm_bwd_b{i}")
        d_sc2, d_ng2 = s_x2 * ng2, s_x2 * (1.0 + sc2)
        if i % 2 == 0:
            dgated = wmm(dy, "gm_w_out", i, "nt", f"gm_out_dx{i}")
            G["gm_w_out"].append(_mm(rec["act"], dy, "tn", f"gm_out_dw{i}", out_dtype=GRAD_WIRE))
            da, dws, dbs, dlg, dlbeta = _gm_mid_bwd(
                rec["a"], dgated, _vec(P["gm_ln_g"][j]), _vec(P["gm_ln_b"][j]), P["gm_w_s"][j],
                P["gm_b_s"][j].reshape(GM_HEADS, GM_BLOCK, 1), f"gm_mid_bwd{i}")
            small["gm_w_s"][j], small["gm_b_s"][j] = dws, dbs[:, :GM_HEADS].T
            small["gm_ln_g"][j], small["gm_ln_b"][j] = dlg, dlbeta
            dh1 = wmm(da, "gm_w_in", i, "nt", f"gm_in_dx{i}")
            G["gm_w_in"].append(_mm(rec["h1"], da, "tn", f"gm_in_dw{i}", out_dtype=GRAD_WIRE, exchange_out=True))
        else:
            dog = wmm(dy, "hg_w_out", i, "nt", f"hg_out_dx{i}")
            G["hg_w_out"].append(_mm(rec["act"], dy, "tn", f"hg_out_dw{i}", out_dtype=GRAD_WIRE))
            dp, dlb, dgn = _hg_scan_bwd(rec["a"], _vec(lb_all[j]), _vec(P["hg_gn_g"][j]), rec["o"], dog,
                                        rec["states"], f"hg_scan_bwd{i}")
            small["dlb"][j], small["hg_gn_g"][j] = dlb, dgn
            dh1 = wmm(dp, "hg_w_in", i, "nt", f"hg_in_dx{i}")
            G["hg_w_in"].append(_mm(rec["h1"], dp, "tn", f"hg_in_dw{i}", out_dtype=GRAD_WIRE, exchange_out=True))
        ng1 = _vec(P["norm_g"][i, 0])
        if on_grads is not None:
            mixer = ("gm_w_in", "gm_w_out") if i % 2 == 0 else ("hg_w_in", "hg_w_out")
            ng1 = ng1 + on_grads(i, {k: G[k][-1] for k in mixer})
        if i > 0:
            prev = saved[i - 1]
            dx, dy, s_sh1, s_x1, s_gate = _norm_bwd(rec["x1"], dh1, dx, ng1, sc1, prev["y2"], prev["mods"][5],
                                                    f"norm_bwd_a{i}")
        else:
            dx, s_sh1, s_x1 = _norm_bwd(rec["x1"], dh1, dx, ng1, sc1, None, None, f"norm_bwd_a{i}")
        d_sc1, d_ng1 = s_x1 * ng1, s_x1 * (1.0 + sc1)
        small["norm_g"][i] = jnp.concatenate([d_ng1, d_ng2], axis=0)
        small["dmod"][i] = jnp.concatenate([s_sh1, d_sc1, d_g1, s_sh2, d_sc2, d_g2], axis=1)
    for k in G:
        G[k] = G[k][::-1]
    dlb_all = jnp.concatenate(small.pop("dlb"), axis=0)
    small["hg_lb"] = _lb_bwd(P["hg_lb"], dlb_all, "lb_bwd")
    return loss, dx, G, small


BIG = ("gm_w_in", "gm_w_out", "hg_w_in", "hg_w_out", "ffn_w_up", "ffn_w_down")
COL_SHARDED = dict(gm_w_in=True, gm_w_out=False, hg_w_in=True, hg_w_out=False, ffn_w_up=True, ffn_w_down=False)
LAYER_WEIGHTS = 4


def _layer_kinds(i):
    return (("gm_w_in", "gm_w_out") if i % 2 == 0 else ("hg_w_in", "hg_w_out")) + ("ffn_w_up", "ffn_w_down")


def _pack(pieces):
    flat = [p.reshape(-1).astype(F32) for p in pieces]
    offs, tot = [], 0
    for f in flat:
        offs.append((tot, f.shape[0]))
        tot += f.shape[0]
    padded = -(-tot // (8 * LANES)) * (8 * LANES)
    if padded > tot:
        flat.append(jnp.zeros((padded - tot,), F32))
    return jnp.concatenate(flat).reshape(-1, LANES), offs


def _unpack(rows, offs, shapes):
    lead = rows.shape[:-2]
    flat = rows.reshape(lead + (-1,))
    return [flat[..., o:o + n].reshape(lead + tuple(s)) for (o, n), s in zip(offs, shapes)]


def _from_chips(per_dev, axis):
    per_chip = per_dev[0::2]
    return jnp.concatenate([per_chip[s] for s in range(N_CHIPS)], axis=axis)


def kernel(x, c, gm_w_in, gm_ln_g, gm_ln_b, gm_w_s, gm_b_s, gm_w_out, hg_w_in, hg_lb, hg_gn_g, hg_w_out, ffn_w_up, ffn_conv_w, ffn_conv_b, ffn_w_down, norm_g, ada_w, ada_b, final_g, loss_target, m_gm_w_in, m_gm_ln_g, m_gm_ln_b, m_gm_w_s, m_gm_b_s, m_gm_w_out, m_hg_w_in, m_hg_lb, m_hg_gn_g, m_hg_w_out, m_ffn_w_up, m_ffn_conv_w, m_ffn_conv_b, m_ffn_w_down, m_norm_g, m_ada_w, m_ada_b, m_final_g, v_gm_w_in, v_gm_ln_g, v_gm_ln_b, v_gm_w_s, v_gm_b_s, v_gm_w_out, v_hg_w_in, v_hg_lb, v_hg_gn_g, v_hg_w_out, v_ffn_w_up, v_ffn_conv_w, v_ffn_conv_b, v_ffn_w_down, v_norm_g, v_ada_w, v_ada_b, v_final_g):
    Dm = D_MODEL
    xi, yi, ci = _position()
    chip = 2 * xi + yi
    dev = 4 * xi + 2 * yi + ci
    weights = dict(gm_w_in=gm_w_in, gm_ln_g=gm_ln_g, gm_ln_b=gm_ln_b, gm_w_s=gm_w_s, gm_b_s=gm_b_s,
                   gm_w_out=gm_w_out, hg_w_in=hg_w_in, hg_lb=hg_lb, hg_gn_g=hg_gn_g, hg_w_out=hg_w_out,
                   ffn_w_up=ffn_w_up, ffn_conv_w=ffn_conv_w, ffn_conv_b=ffn_conv_b, ffn_w_down=ffn_w_down,
                   norm_g=norm_g, ada_w=ada_w, ada_b=ada_b, final_g=final_g)
    mom_m = dict(gm_w_in=m_gm_w_in, gm_ln_g=m_gm_ln_g, gm_ln_b=m_gm_ln_b, gm_w_s=m_gm_w_s, gm_b_s=m_gm_b_s,
                 gm_w_out=m_gm_w_out, hg_w_in=m_hg_w_in, hg_lb=m_hg_lb, hg_gn_g=m_hg_gn_g, hg_w_out=m_hg_w_out,
                 ffn_w_up=m_ffn_w_up, ffn_conv_w=m_ffn_conv_w, ffn_conv_b=m_ffn_conv_b, ffn_w_down=m_ffn_w_down,
                 norm_g=m_norm_g, ada_w=m_ada_w, ada_b=m_ada_b, final_g=m_final_g)
    mom_v = dict(gm_w_in=v_gm_w_in, gm_ln_g=v_gm_ln_g, gm_ln_b=v_gm_ln_b, gm_w_s=v_gm_w_s, gm_b_s=v_gm_b_s,
                 gm_w_out=v_gm_w_out, hg_w_in=v_hg_w_in, hg_lb=v_hg_lb, hg_gn_g=v_hg_gn_g, hg_w_out=v_hg_w_out,
                 ffn_w_up=v_ffn_w_up, ffn_conv_w=v_ffn_conv_w, ffn_conv_b=v_ffn_conv_b, ffn_w_down=v_ffn_w_down,
                 norm_g=v_norm_g, ada_w=v_ada_w, ada_b=v_ada_b, final_g=v_final_g)
    order = list(weights)

    pos = jnp.stack([chip, ci]).astype(jnp.int32)
    shards, by_col = [], []
    for i in range(DEPTH):
        for k in _layer_kinds(i):
            shards.append(weights[k][i if k.startswith("ffn") else i // 2].astype(BF16))
            by_col.append(COL_SHARDED[k])
    placed = [_place_own(sh, col, pos, f"place_own{n}") for n, (sh, col) in enumerate(zip(shards, by_col))]
    gsems, sh_thru, ld_thru, _ = _gather_start(shards, placed, by_col, LAYER_WEIGHTS, "gather_start")

    pieces = [c, hg_lb, hg_gn_g, norm_g, ffn_conv_w]
    payload, offs = _pack(pieces)
    got = _allgather_small(payload, "gather_small")
    c_g, lb_g, gn_g, ng_g, cw_g = _unpack(got, offs, [p.shape for p in pieces])
    c_all = c_g.reshape(N_DEV, Dm)
    P = dict(hg_lb=_from_chips(lb_g, 1), hg_gn_g=_from_chips(gn_g, 1), norm_g=_from_chips(ng_g, 2),
             ffn_conv_w=_from_chips(cw_g, 2), gm_ln_g=gm_ln_g, gm_ln_b=gm_ln_b, gm_w_s=gm_w_s, gm_b_s=gm_b_s,
             ffn_conv_b=ffn_conv_b, final_g=final_g)

    cols = ada_w.shape[2]
    ada_b_sh = lax.dynamic_slice_in_dim(ada_b, chip * cols, cols, axis=1)
    mod_sh = _ada_fwd(c_all, ada_w, ada_b_sh, "ada_fwd")
    mod_g = _allgather_small(mod_sh.reshape(-1, LANES), "gather_mod").reshape(N_DEV, DEPTH, N_DEV, cols)
    mod_mine = lax.dynamic_index_in_dim(mod_g[0::2], dev, axis=2, keepdims=False)
    mod = jnp.transpose(mod_mine, (1, 0, 2)).reshape(DEPTH, N_CHIPS * cols)

    core = jnp.reshape(ci, (1,)).astype(jnp.int32)
    chip_arr = jnp.reshape(chip, (1,)).astype(jnp.int32)
    pending, held, prefetched, swapping = [], {}, {}, []

    def get_w(i, group, after):
        lo, hi = LAYER_WEIGHTS * i, LAYER_WEIGHTS * (i + 1)
        shapes = lambda s: [a.shape for a in shards[s]]
        out = {}
        if i == 0:
            s = slice(lo, lo + 1) if group == 0 else slice(lo + 1, hi)
            landed = _gather_wait(sh_thru[s], ld_thru[s], gsems[0], gsems[1], after, by_col[s], s.start - lo,
                                  f"gather_wait0_{group}")
            full = _forward_sibling(landed, by_col[s], shapes(s), f"gather_forward0_{group}")
            out = dict(zip(_layer_kinds(0)[s.start - lo:s.stop - lo], full))
        elif group == 0:
            s = slice(lo, hi)
            send, recv, lands = prefetched.pop(i)
            full = _forward_wait(lands, send, recv, after, by_col[s], shapes(s), f"gather_forward_wait{i}")
            out = dict(zip(_layer_kinds(i), full))
        if group == 1 and i + 1 < DEPTH:
            s = slice(hi, hi + LAYER_WEIGHTS)
            landed = _gather_wait(sh_thru[s], ld_thru[s], gsems[2 * i + 2], gsems[2 * i + 3], after, by_col[s], 0,
                                  f"gather_wait{i + 1}")
            send, recv, lands, _ = _forward_start(landed, by_col[s], shapes(s), f"gather_forward_start{i + 1}")
            prefetched[i + 1] = (send, recv, lands)
        return out

    def on_grads(i, gdict):
        if i > 0 and "ffn_w_up" in gdict:
            held[i] = gdict
            return 0.0
        gdict = {**held.pop(i, {}), **gdict}
        kinds = [k for k in _layer_kinds(i) if k in gdict]
        tag = f"{i}_ffn" if kinds[0] == "ffn_w_up" else f"{i}"
        g4 = []
        for k in kinds:
            g = gdict[k]
            if not COL_SHARDED[k]:
                R, C = g.shape
                g = g.reshape(N_CHIPS, 2, R // (2 * N_CHIPS), C)
            g4.append(g)
        token = finish_swap(g4[0]) if swapping else 0.0
        if i == 0:
            from_sib = _swap_sibling(g4, f"reduce_swap{tag}", other_half=True)
            return token + start_exchange(tag, i, kinds, g4, from_sib)
        send, recv, g_thru, lands, tok = _swap_start(g4, f"reduce_swap_start{tag}", other_half=True)
        swapping.append((tag, i, kinds, send, recv, g_thru, lands))
        return token + tok[0, 0]

    def start_exchange(tag, i, kinds, g4, from_sib):
        sums = [_add_own_half(g, r, core, f"chip_sum_{k}{i}") for g, r, k in zip(g4, from_sib, kinds)]
        send, recv, sums_thru, lands, token = _exchange_start(sums, f"reduce_start{tag}")
        pending.append((tag, i, kinds, send, recv, sums_thru, lands))
        return token[0, 0]

    def finish_swap(after):
        tag, i, kinds, send, recv, g_thru, lands = swapping.pop()
        g4, from_sib = _swap_wait(g_thru, lands, send, recv, after, f"reduce_swap_wait{tag}", other_half=True)
        return start_exchange(tag, i, kinds, g4, from_sib)

    loss_part, dx, G, small = _local_step(x[0], loss_target[0], mod, None, P, get_w, on_grads)

    sum_pieces = [loss_part[:, :1], small["final_g"], jnp.stack(small["gm_ln_g"]), jnp.stack(small["gm_ln_b"]),
                  jnp.stack(small["gm_w_s"]), jnp.stack(small["gm_b_s"]), jnp.stack(small["ffn_conv_b"]),
                  small["hg_lb"], jnp.stack(small["hg_gn_g"]), jnp.stack(small["norm_g"]),
                  jnp.stack(small["ffn_conv_w"])]
    dmod = jnp.concatenate(small["dmod"], axis=0)
    payload2, offs2 = _pack(sum_pieces + [dmod])
    placed2 = _place_row(payload2, jnp.reshape(dev, (1,)).astype(jnp.int32), "place_grads")
    bsend, brecv, p2_thru, l2_thru, small_token = _broadcast_start(payload2, placed2, "gather_grads_start")

    landed = {}
    for tag, i, kinds, send, recv, sums_thru, lands in pending:
        sums_i, lands_i = _exchange_wait(sums_thru, lands, send, recv, small_token, f"reduce_wait{tag}")
        for k, s_, l_ in zip(kinds, sums_i, lands_i):
            landed[(k, i)] = (l_, s_)
    own_halves = []
    for k in BIG:
        layers = [landed[(k, i)] for i in range(DEPTH) if (k, i) in landed]
        own_halves.append(_sum_chips([l_ for l_, _ in layers], [s_ for _, s_ in layers], chip_arr, f"sum_chips_{k}"))
    jsend, jrecv, own_thru, jlands, after = _swap_start(own_halves, "reduce_join_start")
    grads, deltas, new_m, new_v = {}, {}, {}, {}
    for n, k in enumerate(BIG):
        (own,), (recv,) = _swap_wait([own_thru[n]], [jlands[n]], jsend, jrecv, after, f"reduce_join_wait_{k}",
                                     first=n)
        grads[k], deltas[k], new_m[k], new_v[k] = _adamw_halves(
            weights[k], own, recv, mom_m[k], mom_v[k], core, f"adamw_{k}")
        after = new_v[k]

    got2 = _broadcast_wait(p2_thru, l2_thru, bsend, brecv, new_v[BIG[-1]], "gather_grads_wait")
    dmod_all = _unpack(got2, offs2[-1:], [dmod.shape])[0]
    summed = _sum_devices(got2, "sum_devices")
    (loss_s, d_final_g, d_ln_g, d_ln_b, d_ws, d_bs, d_cb, d_lb, d_gn, d_ng, d_cw) = _unpack(
        summed, offs2[:-1], [(1,), final_g.shape, gm_ln_g.shape, gm_ln_b.shape, gm_w_s.shape, gm_b_s.shape,
                             ffn_conv_b.shape, (2, Dm), (2, Dm), (DEPTH, 2, Dm), (DEPTH, 3, 2 * FFN_HIDDEN)])
    grads.update(final_g=d_final_g, gm_ln_g=d_ln_g, gm_ln_b=d_ln_b, gm_w_s=d_ws, gm_b_s=d_bs, ffn_conv_b=d_cb)
    grads["hg_lb"] = lax.dynamic_slice_in_dim(d_lb, chip * hg_lb.shape[1], hg_lb.shape[1], axis=1)
    grads["hg_gn_g"] = lax.dynamic_slice_in_dim(d_gn, chip * hg_gn_g.shape[1], hg_gn_g.shape[1], axis=1)
    grads["norm_g"] = lax.dynamic_slice_in_dim(d_ng, chip * norm_g.shape[2], norm_g.shape[2], axis=2)
    grads["ffn_conv_w"] = lax.dynamic_slice_in_dim(d_cw, chip * ffn_conv_w.shape[2], ffn_conv_w.shape[2], axis=2)
    dmod_sh = lax.dynamic_slice_in_dim(dmod_all, chip * cols, cols, axis=2)
    grads["ada_w"] = _ada_bwd(c_all, jnp.transpose(dmod_sh, (1, 0, 2)), "ada_bwd")
    grads["ada_b"] = _sum_devices(dmod_all.reshape(N_DEV, -1, LANES), "sum_ada_b").reshape(ada_b.shape)

    for k in order:
        if k in BIG:
            continue
        w = weights[k]
        shp = w.shape
        view = (-1, shp[-1]) if w.ndim > 1 else (8, -1)
        d, m2, v2 = _adamw(w.reshape(view), grads[k].reshape(view), mom_m[k].reshape(view), mom_v[k].reshape(view),
                           f"adamw_{k}")
        deltas[k], new_m[k], new_v[k] = d.reshape(shp), m2.reshape(shp), v2.reshape(shp)
        grads[k] = grads[k].reshape(shp)

    loss = loss_s.reshape(())
    return (loss, dx[None], *[grads[k] for k in order], *[deltas[k] for k in order],
            *[new_m[k] for k in order], *[new_v[k] for k in order])
```

```python
import functools

import jax
import jax.numpy as jnp
from jax import lax
from jax.experimental import pallas as pl
from jax.experimental.pallas import tpu as pltpu

F32 = jnp.float32
BF16 = jnp.bfloat16
HI = lax.Precision.HIGHEST
X3 = lax.Precision.HIGH
GRAD_WIRE = BF16
MESH = pl.DeviceIdType.MESH

D_MODEL = 1024
DEPTH = 4
EPS = 1e-6
GM_WIDTH = 2048
GM_HEADS = 8
GM_BLOCK = 128
GM_HEAD_DIM = 256
CHUNK = 64
HG_HEADS = 8
HG_DIM = 128
FFN_HIDDEN = 2816
N_CHIPS = 4
N_DEV = 8

ADAM_LR = 0.001
ADAM_B1 = 0.9
ADAM_B2 = 0.999
ADAM_EPS = 1e-08
ADAM_WD = 0.01
ADAM_STEP = 10

VMEM_LIMIT_BYTES = 56 * 1024 * 1024
ROW_TILE = 256
NORM_ROW_TILE = 512
HALF_TILES = (512, 352, 256, 128, 64)
LANES = 128

_SQRT_HALF = 0.7071067811865476
_INV_SQRT_2PI = 0.3989422804014327


def _pick(dim, prefs):
    for p in prefs:
        if dim % p == 0:
            return p
    return dim


def _params(sem):
    return pltpu.CompilerParams(dimension_semantics=sem, vmem_limit_bytes=VMEM_LIMIT_BYTES)


def _cdf(x):
    return 0.5 * (1.0 + lax.erf(x * _SQRT_HALF))


def _pdf(x):
    return jnp.exp(-0.5 * x * x) * _INV_SQRT_2PI


def _sig(x):
    return jax.nn.sigmoid(x)


def _dot(a, b, dims, prec=None):
    return lax.dot_general(a, b, (dims, ((), ())), precision=prec, preferred_element_type=F32)


NN = ((1,), (0,))
NT = ((1,), (1,))
TN = ((0,), (0,))


MM_VMEM_BUDGET = 40 * 1024 * 1024


def _mm_tiles(mode, M, N, K, a_bytes, b_bytes, exchange_out):
    tn = _pick(N, (1408, 1024, 512, 256, 128))
    tms = [t for t in (1408, 1024, 512, 256, 128) if M % t == 0 and not (exchange_out and (M // 2) % t)] or [M]
    tks = [K] + [t for t in (2816, 2048, 1408, 1024, 512, 256, 128) if t < K and K % t == 0]

    def fits(tm, tk):
        acc = tm * tn * 4 if tk < K else 0
        return 2 * tm * tk * a_bytes + 2 * tk * tn * b_bytes + 2 * tm * tn * 4 + acc <= MM_VMEM_BUDGET

    for min_tm in (min(512, tms[0]), 0):
        for tk in tks:
            for tm in tms:
                if tm >= min_tm and fits(tm, tk):
                    return tm, tn, tk
    return tms[-1], tn, tks[-1]


def _mm(a, b, mode, name, b_layer=None, out_dtype=F32, exchange_out=False):
    b2 = b.shape[-2:]
    if mode == "nn":
        (M, K), (_, N) = a.shape, b2
    elif mode == "nt":
        (M, K), (N, _) = a.shape, b2
    else:
        (K, M), (_, N) = a.shape, b2
    tm, tn, tk = _mm_tiles(mode, M, N, K, a.dtype.itemsize, b.dtype.itemsize, exchange_out)
    nk = K // tk
    dims = {"nn": NN, "nt": NT, "tn": TN}[mode]

    def body(a_ref, b_ref, o_ref, *scratch):
        part = _dot(a_ref[...].astype(BF16), b_ref[...].astype(BF16), dims)
        if nk == 1:
            o_ref[...] = part.astype(o_ref.dtype)
            return
        (acc_ref,) = scratch
        k = pl.program_id(2)

        @pl.when(k == 0)
        def _():
            acc_ref[...] = part

        @pl.when(k > 0)
        def _():
            acc_ref[...] += part

        @pl.when(k == nk - 1)
        def _():
            o_ref[...] = acc_ref[...].astype(o_ref.dtype)

    if mode == "tn":
        a_spec = pl.BlockSpec((tk, tm), lambda i, j, k: (k, i))
    else:
        a_spec = pl.BlockSpec((tm, tk), lambda i, j, k: (i, k))
    bblk = (tk, tn) if mode in ("nn", "tn") else (tn, tk)
    bidx = (lambda i, j, k: (k, j)) if mode in ("nn", "tn") else (lambda i, j, k: (j, k))
    if b_layer is None:
        b_spec = pl.BlockSpec(bblk, bidx)
    else:
        b_spec = pl.BlockSpec((None,) + bblk, lambda i, j, k: (b_layer,) + bidx(i, j, k))
    if exchange_out:
        mh, cw = M // 2, N // N_CHIPS
        assert mh % tm == 0 and cw % tn == 0
        out_shape = jax.ShapeDtypeStruct((N_CHIPS, 2, mh, cw), out_dtype)
        o_spec = pl.BlockSpec(
            (None, None, tm, tn),
            lambda i, j, k: (j // (cw // tn), i // (mh // tm), i % (mh // tm), j % (cw // tn)))
    else:
        out_shape = jax.ShapeDtypeStruct((M, N), out_dtype)
        o_spec = pl.BlockSpec((tm, tn), lambda i, j, k: (i, j))
    return pl.pallas_call(
        body, name=name, out_shape=out_shape, grid=(M // tm, N // tn, nk),
        in_specs=[a_spec, b_spec], out_specs=o_spec,
        scratch_shapes=[] if nk == 1 else [pltpu.VMEM((tm, tn), F32)],
        compiler_params=_params(("parallel", "parallel", "arbitrary")),
    )(a, b)


def _row_spec(tr, width):
    return pl.BlockSpec((tr, width), lambda i: (i, 0))


def _vec_spec(width, rows=1):
    return pl.BlockSpec((rows, width), lambda i: (0, 0))


def _norm_fwd(x, y, gate, g, sc, sh, name):
    T, Dm = x.shape
    tr = _pick(T, (NORM_ROW_TILE, ROW_TILE))
    has_res = y is not None

    def body(*refs):
        if has_res:
            x_ref, y_ref, gate_ref, g_ref, sc_ref, sh_ref, xo_ref, h_ref = refs
            xv = x_ref[...] + gate_ref[...] * y_ref[...]
            xo_ref[...] = xv
        else:
            x_ref, g_ref, sc_ref, sh_ref, h_ref = refs
            xv = x_ref[...]
        rstd = lax.rsqrt(jnp.mean(xv * xv, axis=-1, keepdims=True) + EPS)
        h_ref[...] = ((xv * rstd * g_ref[...]) * (1.0 + sc_ref[...]) + sh_ref[...]).astype(BF16)

    row, vec = _row_spec(tr, Dm), _vec_spec(Dm)
    if has_res:
        ins, in_specs = (x, y, gate, g, sc, sh), [row, row, vec, vec, vec, vec]
        out_shape = (jax.ShapeDtypeStruct((T, Dm), F32), jax.ShapeDtypeStruct((T, Dm), BF16))
        out_specs = (row, row)
    else:
        ins, in_specs = (x, g, sc, sh), [row, vec, vec, vec]
        out_shape = jax.ShapeDtypeStruct((T, Dm), BF16)
        out_specs = row
    out = pl.pallas_call(body, name=name, out_shape=out_shape, grid=(T // tr,), in_specs=in_specs,
                         out_specs=out_specs, compiler_params=_params(("parallel",)))(*ins)
    return out if has_res else (x, out)


def _norm_bwd(x, dh, dxo, g, sc, y_prev, gate_prev, name):
    T, Dm = x.shape
    tr = _pick(T, (NORM_ROW_TILE, ROW_TILE))
    has_prev = y_prev is not None

    def body(*refs):
        if has_prev:
            x_ref, dh_ref, dxo_ref, g_ref, sc_ref, yp_ref, gp_ref, dx_ref, dyp_ref, s1_ref, s2_ref, s3_ref = refs
        else:
            x_ref, dh_ref, dxo_ref, g_ref, sc_ref, dx_ref, s1_ref, s2_ref = refs

        @pl.when(pl.program_id(0) == 0)
        def _():
            s1_ref[...] = jnp.zeros_like(s1_ref)
            s2_ref[...] = jnp.zeros_like(s2_ref)
            if has_prev:
                s3_ref[...] = jnp.zeros_like(s3_ref)

        xv = x_ref[...]
        rstd = lax.rsqrt(jnp.mean(xv * xv, axis=-1, keepdims=True) + EPS)
        xhat = xv * rstd
        dh = dh_ref[...]
        dxhat = dh * (g_ref[...] * (1.0 + sc_ref[...]))
        dx = dxo_ref[...] + rstd * (dxhat - xhat * jnp.mean(dxhat * xhat, axis=-1, keepdims=True))
        dx_ref[...] = dx
        s1_ref[...] += jnp.sum(dh, axis=0, keepdims=True)
        s2_ref[...] += jnp.sum(dh * xhat, axis=0, keepdims=True)
        if has_prev:
            dyp_ref[...] = (gp_ref[...] * dx).astype(BF16)
            s3_ref[...] += jnp.sum(dx * yp_ref[...], axis=0, keepdims=True)

    row, vec = _row_spec(tr, Dm), _vec_spec(Dm)
    vshape = jax.ShapeDtypeStruct((1, Dm), F32)
    if has_prev:
        ins, in_specs = (x, dh, dxo, g, sc, y_prev, gate_prev), [row, row, row, vec, vec, row, vec]
        out_shape = (jax.ShapeDtypeStruct((T, Dm), F32), jax.ShapeDtypeStruct((T, Dm), BF16), vshape, vshape, vshape)
        out_specs = (row, row, vec, vec, vec)
    else:
        ins, in_specs = (x, dh, dxo, g, sc), [row, row, row, vec, vec]
        out_shape = (jax.ShapeDtypeStruct((T, Dm), F32), vshape, vshape)
        out_specs = (row, vec, vec)
    return pl.pallas_call(body, name=name, out_shape=out_shape, grid=(T // tr,), in_specs=in_specs,
                          out_specs=out_specs, compiler_params=_params(("arbitrary",)))(*ins)


def _loss_head(x, y, gate, fg, tgt, name):
    T, Dm = x.shape
    tr = _pick(T, (NORM_ROW_TILE, ROW_TILE))
    nsteps = T // tr

    def body(x_ref, y_ref, gate_ref, fg_ref, t_ref, dx_ref, dy_ref, loss_ref, sfg_ref, sg_ref, acc_ref):
        i = pl.program_id(0)

        @pl.when(i == 0)
        def _():
            acc_ref[...] = jnp.zeros_like(acc_ref)
            sfg_ref[...] = jnp.zeros_like(sfg_ref)
            sg_ref[...] = jnp.zeros_like(sg_ref)

        yv = y_ref[...]
        xv = x_ref[...] + gate_ref[...] * yv
        rstd = lax.rsqrt(jnp.mean(xv * xv, axis=-1, keepdims=True) + EPS)
        xhat = xv * rstd
        err = xhat * fg_ref[...] - t_ref[...]
        acc_ref[...] += jnp.sum(err * err, axis=0, keepdims=True)
        dyn = err * (1.0 / Dm)
        sfg_ref[...] += jnp.sum(dyn * xhat, axis=0, keepdims=True)
        dxhat = dyn * fg_ref[...]
        dx = rstd * (dxhat - xhat * jnp.mean(dxhat * xhat, axis=-1, keepdims=True))
        dx_ref[...] = dx
        dy_ref[...] = (gate_ref[...] * dx).astype(BF16)
        sg_ref[...] += jnp.sum(dx * yv, axis=0, keepdims=True)

        @pl.when(i == nsteps - 1)
        def _():
            total = jnp.sum(acc_ref[...], axis=1, keepdims=True) * (0.5 / Dm)
            loss_ref[...] = jnp.broadcast_to(total, loss_ref.shape)

    row, vec = _row_spec(tr, Dm), _vec_spec(Dm)
    vshape = jax.ShapeDtypeStruct((1, Dm), F32)
    return pl.pallas_call(
        body, name=name, grid=(nsteps,),
        out_shape=(jax.ShapeDtypeStruct((T, Dm), F32), jax.ShapeDtypeStruct((T, Dm), BF16),
                   jax.ShapeDtypeStruct((1, LANES), F32), vshape, vshape),
        in_specs=[row, row, vec, vec, row], out_specs=(row, row, _vec_spec(LANES), vec, vec),
        scratch_shapes=[pltpu.VMEM((1, Dm), F32)], compiler_params=_params(("arbitrary",)),
    )(x, y, gate, fg, tgt)


def _spatial_mask():
    r = lax.broadcasted_iota(jnp.int32, (GM_BLOCK, GM_BLOCK), 0) // CHUNK
    c = lax.broadcasted_iota(jnp.int32, (GM_BLOCK, GM_BLOCK), 1) // CHUNK
    return r >= c


def _gm_specs(tr):
    return [_row_spec(tr, 2 * GM_WIDTH), _vec_spec(GM_WIDTH), _vec_spec(GM_WIDTH),
            pl.BlockSpec((GM_HEADS, GM_BLOCK, GM_BLOCK), lambda i: (0, 0, 0)),
            pl.BlockSpec((GM_HEADS, GM_BLOCK, 1), lambda i: (0, 0, 0))]


def _gm_mid_fwd(a, ln_g, ln_b, ws, bs3, name):
    T = a.shape[0]
    tr = _pick(T, (ROW_TILE,))
    W = GM_WIDTH

    def body(a_ref, lg_ref, lb_ref, ws_ref, bs_ref, o_ref, vn_scr):
        av = a_ref[:, W:]
        v = av * _cdf(av)
        vc = v - jnp.mean(v, axis=-1, keepdims=True)
        rstd = lax.rsqrt(jnp.mean(vc * vc, axis=-1, keepdims=True) + EPS)
        vn_scr[...] = (vc * rstd * lg_ref[...] + lb_ref[...]).astype(BF16)
        mask = _spatial_mask()
        for h in range(GM_HEADS):
            w = jnp.where(mask, ws_ref[h], 0.0).astype(BF16)
            cs = slice(h * GM_HEAD_DIM, (h + 1) * GM_HEAD_DIM)
            for blk in range(tr // GM_BLOCK):
                rs = slice(blk * GM_BLOCK, (blk + 1) * GM_BLOCK)
                s = _dot(w, vn_scr[rs, cs], NN) + bs_ref[h]
                au = a_ref[rs, cs]
                o_ref[rs, cs] = (au * _cdf(au) * s).astype(BF16)

    return pl.pallas_call(
        body, name=name, out_shape=jax.ShapeDtypeStruct((T, W), BF16), grid=(T // tr,),
        in_specs=_gm_specs(tr), out_specs=_row_spec(tr, W),
        scratch_shapes=[pltpu.VMEM((tr, W), BF16)], compiler_params=_params(("parallel",)),
    )(a, ln_g, ln_b, ws, bs3)


def _gm_mid_bwd(a, dgated, ln_g, ln_b, ws, bs3, name):
    T = a.shape[0]
    tr = _pick(T, (ROW_TILE,))
    W = GM_WIDTH
    nsteps = T // tr

    def body(a_ref, dg_ref, lg_ref, lb_ref, ws_ref, bs_ref, da_ref, dws_ref, dbs_ref, dlg_ref, dlb_ref,
             vn_scr, vhat_scr, dvn_scr, dsum_scr):
        i = pl.program_id(0)

        @pl.when(i == 0)
        def _():
            dws_ref[...] = jnp.zeros_like(dws_ref)
            dbs_ref[...] = jnp.zeros_like(dbs_ref)
            dlg_ref[...] = jnp.zeros_like(dlg_ref)
            dlb_ref[...] = jnp.zeros_like(dlb_ref)
            dsum_scr[...] = jnp.zeros_like(dsum_scr)

        av = a_ref[:, W:]
        cdf_v = _cdf(av)
        v = av * cdf_v
        vc = v - jnp.mean(v, axis=-1, keepdims=True)
        rstd = lax.rsqrt(jnp.mean(vc * vc, axis=-1, keepdims=True) + EPS)
        vhat_scr[...] = vc * rstd
        vn_scr[...] = (vhat_scr[...] * lg_ref[...] + lb_ref[...]).astype(BF16)
        mask = _spatial_mask()
        for h in range(GM_HEADS):
            w = jnp.where(mask, ws_ref[h], 0.0).astype(BF16)
            cs = slice(h * GM_HEAD_DIM, (h + 1) * GM_HEAD_DIM)
            for blk in range(tr // GM_BLOCK):
                rs = slice(blk * GM_BLOCK, (blk + 1) * GM_BLOCK)
                vnb = vn_scr[rs, cs]
                s = _dot(w, vnb, NN) + bs_ref[h]
                au = a_ref[rs, cs]
                cdf_u = _cdf(au)
                dg = dg_ref[rs, cs]
                ds = dg * (au * cdf_u)
                da_ref[rs, cs] = (dg * s * (cdf_u + au * _pdf(au))).astype(BF16)
                dsb = ds.astype(BF16)
                dvn_scr[rs, cs] = _dot(w, dsb, TN)
                dws_ref[h] += _dot(dsb, vnb, NT)
                dsum_scr[:, cs] += ds
        dvn = dvn_scr[...]
        vhat = vhat_scr[...]
        dlg_ref[...] += jnp.sum(dvn * vhat, axis=0, keepdims=True)
        dlb_ref[...] += jnp.sum(dvn, axis=0, keepdims=True)
        dvh = dvn * lg_ref[...]
        dv = rstd * (dvh - jnp.mean(dvh, axis=-1, keepdims=True)
                     - vhat * jnp.mean(dvh * vhat, axis=-1, keepdims=True))
        da_ref[:, W:] = (dv * (cdf_v + av * _pdf(av))).astype(BF16)

        @pl.when(i == nsteps - 1)
        def _():
            for h in range(GM_HEADS):
                dws_ref[h] = jnp.where(mask, dws_ref[h], 0.0)
            col_head = lax.broadcasted_iota(jnp.int32, (W, GM_BLOCK), 0) // GM_HEAD_DIM
            sel = (col_head == lax.broadcasted_iota(jnp.int32, (W, GM_BLOCK), 1)).astype(F32)
            dbs_ref[...] = _dot(dsum_scr[...], sel, NN, HI)

    vshape = jax.ShapeDtypeStruct((1, W), F32)
    return pl.pallas_call(
        body, name=name, grid=(nsteps,),
        out_shape=(jax.ShapeDtypeStruct((T, 2 * W), BF16), jax.ShapeDtypeStruct((GM_HEADS, GM_BLOCK, GM_BLOCK), F32),
                   jax.ShapeDtypeStruct((GM_BLOCK, GM_BLOCK), F32), vshape, vshape),
        in_specs=[_gm_specs(tr)[0], _row_spec(tr, W)] + _gm_specs(tr)[1:],
        out_specs=(_row_spec(tr, 2 * W), pl.BlockSpec((GM_HEADS, GM_BLOCK, GM_BLOCK), lambda i: (0, 0, 0)),
                   pl.BlockSpec((GM_BLOCK, GM_BLOCK), lambda i: (0, 0)), _vec_spec(W), _vec_spec(W)),
        scratch_shapes=[pltpu.VMEM((tr, W), BF16), pltpu.VMEM((tr, W), F32), pltpu.VMEM((tr, W), F32),
                        pltpu.VMEM((GM_BLOCK, W), F32)],
        compiler_params=_params(("arbitrary",)),
    )(a, dgated, ln_g, ln_b, ws, bs3)


SUB = 16
EXP_CLAMP = 80.0


def _tri(lower):
    r = lax.broadcasted_iota(jnp.int32, (CHUNK, CHUNK), 0)
    c = lax.broadcasted_iota(jnp.int32, (CHUNK, CHUNK), 1)
    return (r >= c) if lower else (c >= r)


def _score_masks():
    i = lax.broadcasted_iota(jnp.int32, (CHUNK, CHUNK), 0)
    j = lax.broadcasted_iota(jnp.int32, (CHUNK, CHUNK), 1)
    bi, bj = i // SUB, j // SUB
    diag = (bi == bj) & (i >= j)
    pair = (bi % 2 == 1) & (bj == bi - 1)
    half = (i >= CHUNK // 2) & (j < CHUNK // 2)
    return diag, pair, half


def _dot01(m, x):
    x1 = x.astype(BF16)
    rest = x - x1.astype(F32)
    x2 = rest.astype(BF16)
    x3 = (rest - x2.astype(F32)).astype(BF16)
    return _dot(m, x1, NN) + (_dot(m, x2, NN) + _dot(m, x3, NN))


def _block_rows(b, offset):
    parts = []
    for blk in range(0, CHUNK, SUB):
        r = blk + offset
        parts.append(jnp.zeros((SUB, b.shape[1]), F32) if r < 0 else jnp.broadcast_to(b[r:r + 1], (SUB, b.shape[1])))
    return jnp.concatenate(parts, axis=0)


def _hg_gates(p_ref, lb_ref, lower):
    Dm = D_MODEL
    heads = []
    for h in range(HG_HEADS):
        c0 = h * HG_DIM
        qr = p_ref[:, c0:c0 + HG_DIM]
        fz = p_ref[:, Dm + c0:Dm + c0 + HG_DIM]
        lbh = lb_ref[:, c0:c0 + HG_DIM]
        sg = _sig(fz)
        f = lbh + (1.0 - lbh) * sg
        sq = _sig(qr)
        heads.append(dict(qr=qr, v=p_ref[:, 2 * Dm + c0:2 * Dm + c0 + HG_DIM],
                          gt=p_ref[:, 3 * Dm + c0:3 * Dm + c0 + HG_DIM], lbh=lbh, sg=sg, f=f, gl=jnp.log(f),
                          kk=1.0 - f, sq=sq, q=qr * sq))
    for g in heads:
        g["b"] = _dot01(lower, g.pop("gl"))
    for g in heads:
        g.update(_hg_scalings(g["q"], g["kk"], g.pop("b")))
    return heads


def _hg_scalings(q, kk, b):
    r_mid = _block_rows(b, SUB // 2 - 1)
    r_prev = _block_rows(b, -1)
    r_end = _block_rows(b, SUB - 1)
    r_half = jnp.broadcast_to(b[CHUNK // 2 - 1:CHUNK // 2], b.shape)
    bc = b[CHUNK - 1:CHUNK]
    eqs = (jnp.exp(jnp.clip(b - r_mid, -EXP_CLAMP, EXP_CLAMP)), jnp.exp(jnp.minimum(b - r_prev, 0.0)),
           jnp.exp(jnp.minimum(b - r_half, 0.0)))
    eks = (jnp.exp(jnp.clip(r_mid - b, -EXP_CLAMP, EXP_CLAMP)), jnp.exp(jnp.minimum(r_end - b, 0.0)),
           jnp.exp(jnp.minimum(r_half - b, 0.0)))
    eb = jnp.exp(b)
    ec = jnp.exp(bc - b)
    return dict(eqs=eqs, eks=eks, eb=eb, ec=ec, e_end=jnp.exp(bc), qs=[q * e for e in eqs],
                ks=[kk * e for e in eks], qe=q * eb, ke=kk * ec)


def _scores(g, masks):
    a = None
    for qs, ks, m in zip(g["qs"], g["ks"], masks):
        part = jnp.where(m, _dot(qs.astype(BF16), ks.astype(BF16), NT), 0.0)
        a = part if a is None else a + part
    return a


def _hg_scan_fwd(p, lb, gn, name):
    T = p.shape[0]
    nc = T // CHUNK
    Dm = D_MODEL

    def body(p_ref, lb_ref, gn_ref, o_ref, og_ref, so_ref, st_ref):
        @pl.when(pl.program_id(0) == 0)
        def _():
            st_ref[...] = jnp.zeros_like(st_ref)

        masks = _score_masks()
        heads = _hg_gates(p_ref, lb_ref, _tri(True).astype(BF16))
        states = [st_ref[h] for h in range(HG_HEADS)]
        scores = [_scores(g, masks) for g in heads]
        outs = [_dot(a.astype(BF16), g["v"].astype(BF16), NN) + _dot(g["qe"], st, NT, X3)
                for g, a, st in zip(heads, scores, states)]
        new_states = [st * g["e_end"] + _dot(g["v"], g["ke"], TN, X3) for g, st in zip(heads, states)]
        for h, (g, o, st, st2) in enumerate(zip(heads, outs, states, new_states)):
            cs = slice(h * HG_DIM, (h + 1) * HG_DIM)
            so_ref[0, h] = st
            st_ref[h] = st2
            o_ref[:, cs] = o
            r = lax.rsqrt(jnp.mean(o * o, axis=-1, keepdims=True) + EPS)
            gt = g["gt"]
            og_ref[:, cs] = (((o * r) * gn_ref[:, cs]).astype(F32) * (gt * _sig(gt))).astype(BF16)

    return pl.pallas_call(
        body, name=name, grid=(nc,),
        out_shape=(jax.ShapeDtypeStruct((T, Dm), F32), jax.ShapeDtypeStruct((T, Dm), BF16),
                   jax.ShapeDtypeStruct((nc, HG_HEADS, HG_DIM, HG_DIM), F32)),
        in_specs=[_row_spec(CHUNK, 4 * Dm), _vec_spec(Dm), _vec_spec(Dm)],
        out_specs=(_row_spec(CHUNK, Dm), _row_spec(CHUNK, Dm),
                   pl.BlockSpec((1, HG_HEADS, HG_DIM, HG_DIM), lambda i: (i, 0, 0, 0))),
        scratch_shapes=[pltpu.VMEM((HG_HEADS, HG_DIM, HG_DIM), F32)],
        compiler_params=_params(("arbitrary",)),
    )(p, lb, gn)


def _hg_scan_bwd(p, lb, gn, o, dog, states, name):
    T = p.shape[0]
    nc = T // CHUNK
    Dm = D_MODEL

    def rev(i):
        return nc - 1 - i

    def body(p_ref, lb_ref, gn_ref, o_ref, dog_ref, st_in_ref, dp_ref, dlb_ref, dgn_ref, dst_ref, carry_ref):
        @pl.when(pl.program_id(0) == 0)
        def _():
            dst_ref[...] = jnp.zeros_like(dst_ref)
            carry_ref[...] = jnp.zeros_like(carry_ref)
            dlb_ref[...] = jnp.zeros_like(dlb_ref)
            dgn_ref[...] = jnp.zeros_like(dgn_ref)

        upper = _tri(False).astype(BF16)
        masks = _score_masks()
        heads = _hg_gates(p_ref, lb_ref, _tri(True).astype(BF16))
        for h, g in enumerate(heads):
            cs = slice(h * HG_DIM, (h + 1) * HG_DIM)
            oh = o_ref[:, cs]
            r = lax.rsqrt(jnp.mean(oh * oh, axis=-1, keepdims=True) + EPS)
            on = oh * r
            gt = g["gt"]
            sgt = _sig(gt)
            sil = gt * sgt
            dogh = dog_ref[:, cs]
            gnh = gn_ref[:, cs]
            don = dogh * gnh * sil
            g["dgn"] = jnp.sum(dogh * on * sil, axis=0, keepdims=True)
            g["dgate"] = dogh * on * gnh * (sgt * (1.0 + gt * (1.0 - sgt)))
            g["do"] = r * (don - on * jnp.mean(don * on, axis=-1, keepdims=True))
            g["dst"] = dst_ref[h]
            g["st"] = st_in_ref[0, h]
            g["carry"] = carry_ref[h]
        for g in heads:
            g["a"] = _scores(g, masks)
            g["dob"] = g["do"].astype(BF16)
            g["da"] = _dot(g["dob"], g["v"].astype(BF16), NT)
        for g in heads:
            g["dv"] = _dot(g["a"].astype(BF16), g["dob"], TN) + _dot(g["ke"].astype(BF16), g["dst"].astype(BF16), NT)
            g["dq"] = _dot(g["do"], g["st"], NN, X3) * g["eb"]
            g["dk"] = _dot(g["v"], g["dst"], NN, X3) * g["ec"]
            g["dst2"] = g["dst"] * g["e_end"] + _dot(g["do"], g["qe"], TN, X3)
        for lvl in range(3):
            for g in heads:
                dam = jnp.where(masks[lvl], g["da"], 0.0)
                g["dq"] = g["dq"] + _dot(dam, g["ks"][lvl], NN, X3) * g["eqs"][lvl]
                g["dk"] = g["dk"] + _dot(dam, g["qs"][lvl], TN, X3) * g["eks"][lvl]
        for g in heads:
            g["dgd"] = g["q"] * g["dq"] - g["kk"] * g["dk"]
            g["dgl"] = _dot01(upper, g["dgd"]) + g["carry"]
        for h, g in enumerate(heads):
            c0 = h * HG_DIM
            cs = slice(c0, c0 + HG_DIM)
            df = g["dgl"] / g["f"] - g["dk"]
            sg, sq, qr = g["sg"], g["sq"], g["qr"]
            dst_ref[h] = g["dst2"]
            carry_ref[h] = g["carry"] + jnp.sum(g["dgd"], axis=0, keepdims=True)
            dgn_ref[:, cs] += g["dgn"]
            dlb_ref[:, cs] += jnp.sum(df * (1.0 - sg), axis=0, keepdims=True)
            dp_ref[:, c0:c0 + HG_DIM] = (g["dq"] * (sq * (1.0 + qr * (1.0 - sq)))).astype(BF16)
            dp_ref[:, Dm + c0:Dm + c0 + HG_DIM] = (df * (1.0 - g["lbh"]) * sg * (1.0 - sg)).astype(BF16)
            dp_ref[:, 2 * Dm + c0:2 * Dm + c0 + HG_DIM] = g["dv"].astype(BF16)
            dp_ref[:, 3 * Dm + c0:3 * Dm + c0 + HG_DIM] = g["dgate"].astype(BF16)

    vshape = jax.ShapeDtypeStruct((1, Dm), F32)
    rrow = lambda w: pl.BlockSpec((CHUNK, w), lambda i: (rev(i), 0))
    return pl.pallas_call(
        body, name=name, grid=(nc,),
        out_shape=(jax.ShapeDtypeStruct((T, 4 * Dm), BF16), vshape, vshape),
        in_specs=[rrow(4 * Dm), _vec_spec(Dm), _vec_spec(Dm), rrow(Dm), rrow(Dm),
                  pl.BlockSpec((1, HG_HEADS, HG_DIM, HG_DIM), lambda i: (rev(i), 0, 0, 0))],
        out_specs=(rrow(4 * Dm), _vec_spec(Dm), _vec_spec(Dm)),
        scratch_shapes=[pltpu.VMEM((HG_HEADS, HG_DIM, HG_DIM), F32), pltpu.VMEM((HG_HEADS, 1, HG_DIM), F32)],
        compiler_params=_params(("arbitrary",)),
    )(p, lb, gn, o, dog, states)


def _lb_fwd(hg_lb, name):
    def body(a_ref, o_ref):
        a0, a1 = a_ref[0:1], a_ref[1:2]
        m = jnp.maximum(a0, a1)
        e0, e1 = jnp.exp(a0 - m), jnp.exp(a1 - m)
        p0, p1 = e0 / (e0 + e1), e1 / (e0 + e1)
        o_ref[0:1] = p0 - p0
        o_ref[1:2] = (p0 + p1) - p0

    return pl.pallas_call(body, name=name, out_shape=jax.ShapeDtypeStruct(hg_lb.shape, F32))(hg_lb)


def _lb_bwd(hg_lb, dlb_all, name):
    def body(a_ref, d_ref, o_ref):
        a0, a1 = a_ref[0:1], a_ref[1:2]
        m = jnp.maximum(a0, a1)
        e0, e1 = jnp.exp(a0 - m), jnp.exp(a1 - m)
        p0, p1 = e0 / (e0 + e1), e1 / (e0 + e1)
        d1 = d_ref[1:2]
        o_ref[0:1] = -p0 * p1 * d1
        o_ref[1:2] = p1 * (1.0 - p1) * d1

    return pl.pallas_call(body, name=name, out_shape=jax.ShapeDtypeStruct(hg_lb.shape, F32))(hg_lb, dlb_all)


CONV_COLS_FWD = 256
CONV_COLS_BWD = 128


def _conv_fwd(a, w, b, name):
    T = a.shape[0]
    Fh = FFN_HIDDEN
    tr = _pick(T, (ROW_TILE,))
    cw = CONV_COLS_FWD
    hb = tr // 8

    def body(a_ref, ap_ref, w_ref, b_ref, m_ref):
        m0 = (pl.program_id(0) > 0).astype(F32)

        def conv(cc):
            x = jnp.concatenate([ap_ref[:, pl.ds(cc, cw)] * m0, a_ref[:, pl.ds(cc, cw)]], axis=0)
            wv = w_ref[:, pl.ds(cc, cw)]
            y = b_ref[:, pl.ds(cc, cw)] + wv[2:3] * x + wv[1:2] * pltpu.roll(x, 1, axis=0) \
                + wv[0:1] * pltpu.roll(x, 2, axis=0)
            return y[8:]

        def step(c, carry):
            c0 = pl.multiple_of(c * cw, cw)
            c1 = pl.multiple_of(Fh + c * cw, cw)
            yg, yv = conv(c0), conv(c1)
            m_ref[:, pl.ds(c0, cw)] = (yg * _cdf(yg) * yv).astype(BF16)
            return carry

        lax.fori_loop(0, Fh // cw, step, 0)

    return pl.pallas_call(
        body, name=name, out_shape=jax.ShapeDtypeStruct((T, Fh), BF16), grid=(T // tr,),
        in_specs=[_row_spec(tr, 2 * Fh), pl.BlockSpec((8, 2 * Fh), lambda i: (jnp.maximum(i * hb - 1, 0), 0)),
                  _vec_spec(2 * Fh, 3), _vec_spec(2 * Fh)],
        out_specs=_row_spec(tr, Fh), compiler_params=_params(("parallel",)),
    )(a, a, w, b)


def _conv_bwd(a, dm, w, b, name):
    T = a.shape[0]
    Fh = FFN_HIDDEN
    tr = _pick(T, (ROW_TILE,))
    cw = CONV_COLS_BWD
    hb = tr // 8
    nsteps = T // tr
    n = tr + 8

    def body(a_ref, ap_ref, an_ref, dm_ref, dmn_ref, w_ref, b_ref, da_ref, dw_ref, db_ref):
        i = pl.program_id(0)
        m0 = (i > 0).astype(F32)
        m1 = (i < nsteps - 1).astype(F32)

        @pl.when(i == 0)
        def _():
            dw_ref[...] = jnp.zeros_like(dw_ref)
            db_ref[...] = jnp.zeros_like(db_ref)

        def prep(cc):
            x = jnp.concatenate([ap_ref[:, pl.ds(cc, cw)] * m0, a_ref[:, pl.ds(cc, cw)],
                                 an_ref[:, pl.ds(cc, cw)] * m1], axis=0)
            wv = w_ref[:, pl.ds(cc, cw)]
            s1 = pltpu.roll(x, 1, axis=0)
            s2 = pltpu.roll(x, 2, axis=0)
            y = b_ref[:, pl.ds(cc, cw)] + wv[2:3] * x + wv[1:2] * s1 + wv[0:1] * s2
            return wv, x[8:], s1[8:], s2[8:], y[8:]

        def back(cc, dy, wv, x0, s1, s2):
            da = wv[2:3] * dy + wv[1:2] * pltpu.roll(dy, n - 1, axis=0) + wv[0:1] * pltpu.roll(dy, n - 2, axis=0)
            da_ref[:, pl.ds(cc, cw)] = da[:tr].astype(BF16)
            d = dy[:tr]
            db_ref[:, pl.ds(cc, cw)] += jnp.sum(d, axis=0, keepdims=True)
            dw_ref[2:3, pl.ds(cc, cw)] += jnp.sum(d * x0[:tr], axis=0, keepdims=True)
            dw_ref[1:2, pl.ds(cc, cw)] += jnp.sum(d * s1[:tr], axis=0, keepdims=True)
            dw_ref[0:1, pl.ds(cc, cw)] += jnp.sum(d * s2[:tr], axis=0, keepdims=True)

        def step(c, carry):
            c0 = pl.multiple_of(c * cw, cw)
            c1 = pl.multiple_of(Fh + c * cw, cw)
            dmx = jnp.concatenate([dm_ref[:, pl.ds(c0, cw)], dmn_ref[:, pl.ds(c0, cw)] * m1], axis=0)
            wg, xg, s1g, s2g, yg = prep(c0)
            wv, xv, s1v, s2v, yv = prep(c1)
            cg = _cdf(yg)
            back(c0, dmx * yv * (cg + yg * _pdf(yg)), wg, xg, s1g, s2g)
            back(c1, dmx * (yg * cg), wv, xv, s1v, s2v)
            return carry

        lax.fori_loop(0, Fh // cw, step, 0)

    prev = lambda wd: pl.BlockSpec((8, wd), lambda i: (jnp.maximum(i * hb - 1, 0), 0))
    nxt = lambda wd: pl.BlockSpec((8, wd), lambda i: (jnp.minimum((i + 1) * hb, T // 8 - 1), 0))
    return pl.pallas_call(
        body, name=name, grid=(nsteps,),
        out_shape=(jax.ShapeDtypeStruct((T, 2 * Fh), BF16), jax.ShapeDtypeStruct((3, 2 * Fh), F32),
                   jax.ShapeDtypeStruct((1, 2 * Fh), F32)),
        in_specs=[_row_spec(tr, 2 * Fh), prev(2 * Fh), nxt(2 * Fh), _row_spec(tr, Fh), nxt(Fh),
                  _vec_spec(2 * Fh, 3), _vec_spec(2 * Fh)],
        out_specs=(_row_spec(tr, 2 * Fh), _vec_spec(2 * Fh, 3), _vec_spec(2 * Fh)),
        compiler_params=_params(("arbitrary",)),
    )(a, a, a, dm, dm, w, b)


def _ada_fwd(c_all, ada_w, ada_b, name):
    L, Dm, cols = ada_w.shape
    tn = _pick(cols, (512, 256, 128))

    def body(c_ref, w_ref, b_ref, o_ref):
        cv = c_ref[...]
        cond = (cv * _sig(cv)).astype(BF16)
        o_ref[...] = _dot(cond, w_ref[...].astype(BF16), NN) + b_ref[...]

    return pl.pallas_call(
        body, name=name, out_shape=jax.ShapeDtypeStruct((L, N_DEV, cols), F32), grid=(L, cols // tn),
        in_specs=[pl.BlockSpec((N_DEV, Dm), lambda l, j: (0, 0)), pl.BlockSpec((None, Dm, tn), lambda l, j: (l, 0, j)),
                  pl.BlockSpec((None, 1, tn), lambda l, j: (l, 0, j))],
        out_specs=pl.BlockSpec((None, N_DEV, tn), lambda l, j: (l, 0, j)),
        compiler_params=_params(("parallel", "parallel")),
    )(c_all, ada_w, ada_b.reshape(L, 1, cols))


def _ada_bwd(c_all, dmod, name):
    L, _, cols = dmod.shape
    Dm = c_all.shape[1]
    tn = _pick(cols, (512, 256, 128))

    def body(c_ref, d_ref, o_ref):
        cv = c_ref[...]
        o_ref[...] = _dot(cv * _sig(cv), d_ref[...], TN, HI)

    return pl.pallas_call(
        body, name=name, out_shape=jax.ShapeDtypeStruct((L, Dm, cols), F32), grid=(L, cols // tn),
        in_specs=[pl.BlockSpec((N_DEV, Dm), lambda l, j: (0, 0)), pl.BlockSpec((None, N_DEV, tn), lambda l, j: (l, 0, j))],
        out_specs=pl.BlockSpec((None, Dm, tn), lambda l, j: (l, 0, j)),
        compiler_params=_params(("parallel", "parallel")),
    )(c_all, dmod)


def _add_own_half(g4, rb, core, name):
    S, _, rh, cw = g4.shape
    tr = _pick(rh, HALF_TILES)

    def body(core_ref, g_ref, r_ref, o_ref):
        o_ref[...] = (g_ref[...].astype(F32) + r_ref[...].astype(F32)).astype(GRAD_WIRE)

    return pl.pallas_call(
        body, name=name, out_shape=jax.ShapeDtypeStruct((S, rh, cw), BF16),
        grid_spec=pltpu.PrefetchScalarGridSpec(
            num_scalar_prefetch=1, grid=(S, rh // tr),
            in_specs=[pl.BlockSpec((None, None, tr, cw), lambda s, i, core_ref: (s, core_ref[0], i, 0)),
                      pl.BlockSpec((None, tr, cw), lambda s, i, core_ref: (s, i, 0))],
            out_specs=pl.BlockSpec((None, tr, cw), lambda s, i, core_ref: (s, i, 0))),
        compiler_params=_params(("parallel", "parallel")),
    )(core, g4, rb)


def _sum_chips(lands, sums, chip, name):
    L = len(lands)
    _, rh, cw = lands[0].shape
    tr = _pick(rh, (256, 128, 176, 64))

    def body(chip_ref, *refs):
        ld, cs, o_ref = refs[:L], refs[L:2 * L], refs[2 * L]
        me = chip_ref[0]
        for k in range(L):
            @pl.when(pl.program_id(0) == k)
            def _(k=k):
                own = cs[k][...].astype(F32)
                got = [ld[k][j].astype(F32) for j in range(3)]
                acc = None
                for t in range(N_CHIPS):
                    d = jnp.bitwise_xor(jnp.int32(t), me)
                    term = jnp.where(d == 0, own, jnp.where(d == 2, got[0], jnp.where(d == 1, got[1], got[2])))
                    acc = term if acc is None else acc + term
                o_ref[...] = acc

    frozen = lambda l, i, k: jnp.where(l == k, i, 0)
    in_specs = [pl.BlockSpec((3, tr, cw), lambda l, i, chip_ref, k=k: (0, frozen(l, i, k), 0)) for k in range(L)]
    in_specs += [pl.BlockSpec((None, tr, cw), lambda l, i, chip_ref, k=k: (chip_ref[0], frozen(l, i, k), 0))
                 for k in range(L)]
    return pl.pallas_call(
        body, name=name, out_shape=jax.ShapeDtypeStruct((L, rh, cw), F32),
        grid_spec=pltpu.PrefetchScalarGridSpec(
            num_scalar_prefetch=1, grid=(L, rh // tr), in_specs=in_specs,
            out_specs=pl.BlockSpec((None, tr, cw), lambda l, i, chip_ref: (l, i, 0))),
        compiler_params=_params(("arbitrary", "arbitrary")),
    )(chip, *lands, *sums)


def _sum_devices(gathered, name):
    n, R, _ = gathered.shape
    tr = _pick(R, (512, 448, 384, 256, 192, 128, 64, 32, 16, 8))

    def body(g_ref, o_ref):
        acc = g_ref[0]
        for d in range(1, n):
            acc = acc + g_ref[d]
        o_ref[...] = acc

    return pl.pallas_call(
        body, name=name, out_shape=jax.ShapeDtypeStruct((R, LANES), F32), grid=(R // tr,),
        in_specs=[pl.BlockSpec((n, tr, LANES), lambda i: (0, i, 0))], out_specs=pl.BlockSpec((tr, LANES), lambda i: (i, 0)),
        compiler_params=_params(("parallel",)),
    )(gathered)


def _adamw(w, g, m, v, name):
    R, C = w.shape
    tr = _pick(R, (256, 128, 64, 32, 16, 8))
    c1 = 1.0 / (1.0 - ADAM_B1 ** ADAM_STEP)
    c2 = 1.0 / (1.0 - ADAM_B2 ** ADAM_STEP)

    def body(w_ref, g_ref, m_ref, v_ref, d_ref, mo_ref, vo_ref):
        gv = g_ref[...]
        m2 = ADAM_B1 * m_ref[...] + (1.0 - ADAM_B1) * gv
        v2 = ADAM_B2 * v_ref[...] + (1.0 - ADAM_B2) * (gv * gv)
        mo_ref[...] = m2
        vo_ref[...] = v2
        d_ref[...] = -ADAM_LR * ((m2 * c1) / (jnp.sqrt(v2 * c2) + ADAM_EPS) + ADAM_WD * w_ref[...])

    spec = pl.BlockSpec((tr, C), lambda i: (i, 0))
    shp = jax.ShapeDtypeStruct((R, C), F32)
    return pl.pallas_call(body, name=name, out_shape=(shp, shp, shp), grid=(R // tr,), in_specs=[spec] * 4,
                          out_specs=(spec, spec, spec), compiler_params=_params(("parallel",)))(w, g, m, v)


def _adamw_halves(w, own, recv, m, v, core, name):
    L, rh, cw = own.shape
    tr = _pick(rh, (256, 128, 176, 64))
    c1 = 1.0 / (1.0 - ADAM_B1 ** ADAM_STEP)
    c2 = 1.0 / (1.0 - ADAM_B2 ** ADAM_STEP)

    def body(core_ref, w_ref, own_ref, recv_ref, m_ref, v_ref, g_ref, d_ref, mo_ref, vo_ref):
        gv = jnp.where(pl.program_id(1) == core_ref[0], own_ref[...], recv_ref[...])
        g_ref[...] = gv
        m2 = ADAM_B1 * m_ref[...] + (1.0 - ADAM_B1) * gv
        v2 = ADAM_B2 * v_ref[...] + (1.0 - ADAM_B2) * (gv * gv)
        mo_ref[...] = m2
        vo_ref[...] = v2
        d_ref[...] = -ADAM_LR * ((m2 * c1) / (jnp.sqrt(v2 * c2) + ADAM_EPS) + ADAM_WD * w_ref[...])

    full = pl.BlockSpec((None, None, tr, cw), lambda l, hf, i, core_ref: (l, hf, i, 0))
    mine = pl.BlockSpec((None, tr, cw), lambda l, hf, i, core_ref: (l, jnp.where(hf == core_ref[0], i, 0), 0))
    other = pl.BlockSpec((None, tr, cw), lambda l, hf, i, core_ref: (l, jnp.where(hf == core_ref[0], 0, i), 0))
    shp = jax.ShapeDtypeStruct((L, 2, rh, cw), F32)
    view = lambda a: a.reshape(L, 2, rh, cw)
    outs = pl.pallas_call(
        body, name=name, out_shape=(shp, shp, shp, shp),
        grid_spec=pltpu.PrefetchScalarGridSpec(
            num_scalar_prefetch=1, grid=(L, 2, rh // tr), in_specs=[full, mine, other, full, full],
            out_specs=(full, full, full, full)),
        compiler_params=_params(("arbitrary", "arbitrary", "arbitrary")),
    )(core, view(w), own, recv, view(m), view(v))
    return tuple(o.reshape(L, 2 * rh, cw) for o in outs)


ANY = pl.BlockSpec(memory_space=pl.ANY)


def _position():
    x, y, c = lax.axis_index("x"), lax.axis_index("y"), lax.axis_index("c")
    return x, y, c


def _allgather(ins, out_shapes, src_fns, dst_fns, name):
    n = len(ins)

    def body(*refs):
        in_refs, out_refs = refs[:n], refs[n:2 * n]
        send_sems, recv_sems, local_sems = refs[2 * n:]
        x, y, c = _position()
        me, sibling = (x, y, c), (x, y, 1 - c)
        chips = [(1 - x, y), (x, 1 - y), (1 - x, 1 - y)]

        def copy(k, j, block, to, own=False):
            dst = dst_fns[k](out_refs[k], *block)
            return pltpu.make_async_remote_copy(
                src_ref=src_fns[k](in_refs[k], c) if own else dst, dst_ref=dst,
                send_sem=send_sems.at[k, j], recv_sem=recv_sems.at[k, j], device_id=to, device_id_type=MESH)

        mine = [pltpu.make_async_copy(src_fns[k](in_refs[k], c), dst_fns[k](out_refs[k], *me), local_sems.at[k])
                for k in range(n)]
        for cp in mine:
            cp.start()
        first = []
        for k in range(n):
            first.append(copy(k, 0, me, sibling, own=True))
            first += [copy(k, 1 + j, me, (*chip, c), own=True) for j, chip in enumerate(chips)]
        for cp in first:
            cp.start()
        passed = []
        for j, chip in enumerate(chips):
            for k in range(n):
                copy(k, 1 + j, (*chip, c), me).wait_recv()
                fwd = copy(k, 4 + j, (*chip, c), sibling)
                fwd.start()
                passed.append(fwd)
        for k in range(n):
            copy(k, 0, sibling, me).wait_recv()
        for j, chip in enumerate(chips):
            for k in range(n):
                copy(k, 4 + j, (*chip, 1 - c), me).wait_recv()
        for cp in first + passed:
            cp.wait_send()
        for cp in mine:
            cp.wait()

    spec = pl.BlockSpec(memory_space=pltpu.VMEM)
    return pl.pallas_call(
        body, name=name, out_shape=tuple(out_shapes), in_specs=[spec] * n, out_specs=tuple([spec] * n),
        scratch_shapes=[pltpu.SemaphoreType.DMA((n, 7)), pltpu.SemaphoreType.DMA((n, 7)),
                        pltpu.SemaphoreType.DMA((n,))],
        compiler_params=pltpu.CompilerParams(vmem_limit_bytes=VMEM_LIMIT_BYTES),
    )(*ins)


def _allgather_small(payload, name):
    R = payload.shape[0]
    (out,) = _allgather(
        [payload], [jax.ShapeDtypeStruct((N_DEV, R, LANES), F32)],
        [lambda ref, c: ref], [lambda ref, px, py, pc: ref.at[4 * px + 2 * py + pc]], name)
    return out


def _swap_sibling(ins, name, other_half=False):
    n = len(ins)

    def body(*refs):
        in_refs, out_refs = refs[:n], refs[n:2 * n]
        send_sems, recv_sems = refs[2 * n:]
        x, y, c = _position()
        copies = [pltpu.make_async_remote_copy(
            src_ref=in_refs[k].at[:, 1 - c] if other_half else in_refs[k], dst_ref=out_refs[k],
            send_sem=send_sems.at[k], recv_sem=recv_sems.at[k],
            device_id=(x, y, 1 - c), device_id_type=MESH) for k in range(n)]
        for cp in copies:
            cp.start()
        for cp in copies:
            cp.wait_recv()
        for cp in copies:
            cp.wait_send()

    shape = lambda a: (a.shape[0],) + a.shape[2:] if other_half else a.shape
    return pl.pallas_call(
        body, name=name, out_shape=tuple(jax.ShapeDtypeStruct(shape(a), a.dtype) for a in ins),
        in_specs=[ANY] * n, out_specs=tuple([ANY] * n),
        scratch_shapes=[pltpu.SemaphoreType.DMA((n,)), pltpu.SemaphoreType.DMA((n,))],
        compiler_params=pltpu.CompilerParams(vmem_limit_bytes=VMEM_LIMIT_BYTES),
    )(*ins)


HBM_SPEC = pl.BlockSpec(memory_space=pltpu.HBM)
SEM_SPEC = pl.BlockSpec(memory_space=pltpu.SEMAPHORE)
SPLIT_PARAMS = pltpu.CompilerParams(has_side_effects=pltpu.SideEffectType.DATAFLOW_SIDE_EFFECTING)
TOKEN = jax.ShapeDtypeStruct((8, LANES), F32)


def _hbm(a):
    return pltpu.with_memory_space_constraint(a, pltpu.HBM)


def _weight_window(ref, col, r, cw, px, py, pc):
    rh = r // 2
    if col:
        return ref.at[pl.ds(pc * rh, rh), pl.ds((2 * px + py) * cw, cw)]
    return ref.at[pl.ds((2 * px + py) * r + pc * rh, rh), :]


def _peers(x, y, c):
    return [(x, y, 1 - c), (1 - x, y, c), (x, 1 - y, c), (1 - x, 1 - y, c)]


def _place_own(shard, col, pos, name):
    r, cw = shard.shape
    rh = r // 2
    tr = _pick(rh, HALF_TILES)
    nb = rh // tr
    shape = (r, N_CHIPS * cw) if col else (N_CHIPS * r, cw)

    def body(pos_ref, x_ref, o_ref):
        o_ref[...] = x_ref[...]

    if col:
        out_idx = lambda i, pos_ref: (pos_ref[1] * nb + i, pos_ref[0])
    else:
        out_idx = lambda i, pos_ref: (pos_ref[0] * (2 * nb) + pos_ref[1] * nb + i, 0)
    return pl.pallas_call(
        body, name=name, out_shape=jax.ShapeDtypeStruct(shape, shard.dtype),
        grid_spec=pltpu.PrefetchScalarGridSpec(
            num_scalar_prefetch=1, grid=(nb,),
            in_specs=[pl.BlockSpec((tr, cw), lambda i, pos_ref: (pos_ref[1] * nb + i, 0))],
            out_specs=pl.BlockSpec((tr, cw), out_idx)),
        compiler_params=_params(("arbitrary",)),
    )(pos, shard)


def _gather_start(shards, lands, cols, per_layer, name):
    n = len(shards)
    nl = n // per_layer

    def body(*refs):
        sh, ld = refs[:n], refs[n:2 * n]
        sems, token = refs[2 * n:2 * n + 2 * nl], refs[-1]
        x, y, c = _position()
        for k in range(n):
            l, a = divmod(k, per_layer)
            r, cw = shards[k].shape
            src = sh[k].at[pl.ds(c * (r // 2), r // 2), :]
            dst = _weight_window(ld[k], cols[k], r, cw, x, y, c)
            for j, peer in enumerate(_peers(x, y, c)):
                pltpu.make_async_remote_copy(src_ref=src, dst_ref=dst, send_sem=sems[2 * l].at[4 * a + j],
                                             recv_sem=sems[2 * l + 1].at[4 * a + j], device_id=peer,
                                             device_id_type=MESH).start()
        token[...] = jnp.zeros_like(token)

    arrs = list(shards) + list(lands)
    out = pl.pallas_call(
        body, name=name,
        out_shape=tuple(pltpu.SemaphoreType.DMA((per_layer * 4,)) for _ in range(2 * nl))
        + tuple(pltpu.HBM(a.shape, a.dtype) for a in arrs) + (TOKEN,),
        in_specs=[HBM_SPEC] * (2 * n),
        out_specs=(SEM_SPEC,) * (2 * nl) + (HBM_SPEC,) * (2 * n) + (pl.BlockSpec(memory_space=pltpu.VMEM),),
        input_output_aliases={i: 2 * nl + i for i in range(2 * n)}, compiler_params=SPLIT_PARAMS,
    )(*[_hbm(a) for a in arrs])
    return out[:2 * nl], out[2 * nl:2 * nl + n], out[2 * nl + n:2 * nl + 2 * n], out[-1]


def _gather_wait(shards, lands, send, recv, after, cols, first, name):
    m = len(shards)

    def body(*refs):
        sh, ld = refs[:m], refs[m:2 * m]
        send_ref, recv_ref = refs[2 * m], refs[2 * m + 1]
        x, y, c = _position()
        for a in range(m):
            r, cw = shards[a].shape
            src = sh[a].at[pl.ds(c * (r // 2), r // 2), :]
            for j, (px, py, pc) in enumerate(_peers(x, y, c)):
                cp = pltpu.make_async_remote_copy(
                    src_ref=src, dst_ref=_weight_window(ld[a], cols[a], r, cw, px, py, pc),
                    send_sem=send_ref.at[4 * (first + a) + j], recv_sem=recv_ref.at[4 * (first + a) + j],
                    device_id=(px, py, pc),
                    device_id_type=MESH)
                cp.wait_send()
                cp.wait_recv()

    arrs = list(shards) + list(lands)
    out = pl.pallas_call(
        body, name=name, out_shape=tuple(pltpu.HBM(a.shape, a.dtype) for a in arrs),
        in_specs=[HBM_SPEC] * (2 * m) + [SEM_SPEC, SEM_SPEC, ANY], out_specs=(HBM_SPEC,) * (2 * m),
        input_output_aliases={i: i for i in range(2 * m)}, compiler_params=SPLIT_PARAMS,
    )(*arrs, send, recv, after)
    return out[m:]


def _forward_sibling(lands, cols, shard_shapes, name):
    m = len(lands)

    def body(*refs):
        ins, outs = refs[:m], refs[m:2 * m]
        send_sems, recv_sems = refs[2 * m:]
        x, y, c = _position()
        chips = [(1 - x, y), (x, 1 - y), (1 - x, 1 - y)]
        sends = []
        for a in range(m):
            r, cw = shard_shapes[a]
            for j, (px, py) in enumerate(chips):
                cp = pltpu.make_async_remote_copy(
                    src_ref=_weight_window(ins[a], cols[a], r, cw, px, py, c),
                    dst_ref=_weight_window(outs[a], cols[a], r, cw, px, py, c),
                    send_sem=send_sems.at[a, j], recv_sem=recv_sems.at[a, j], device_id=(x, y, 1 - c),
                    device_id_type=MESH)
                cp.start()
                sends.append(cp)
        for a in range(m):
            r, cw = shard_shapes[a]
            for j, (px, py) in enumerate(chips):
                pltpu.make_async_remote_copy(
                    src_ref=_weight_window(ins[a], cols[a], r, cw, px, py, c),
                    dst_ref=_weight_window(outs[a], cols[a], r, cw, px, py, 1 - c),
                    send_sem=send_sems.at[a, j], recv_sem=recv_sems.at[a, j], device_id=(x, y, 1 - c),
                    device_id_type=MESH).wait_recv()
        for cp in sends:
            cp.wait_send()

    return pl.pallas_call(
        body, name=name, out_shape=tuple(jax.ShapeDtypeStruct(a.shape, a.dtype) for a in lands),
        in_specs=[ANY] * m, out_specs=tuple([ANY] * m), input_output_aliases={i: i for i in range(m)},
        scratch_shapes=[pltpu.SemaphoreType.DMA((m, 3)), pltpu.SemaphoreType.DMA((m, 3))],
        compiler_params=pltpu.CompilerParams(vmem_limit_bytes=VMEM_LIMIT_BYTES),
    )(*lands)


def _swap_start(ins, name, other_half=False):
    n = len(ins)
    shape = lambda a: (a.shape[0],) + a.shape[2:] if other_half else a.shape
    lands = [lax.empty(shape(a), a.dtype) for a in ins]

    def body(*refs):
        src, ld = refs[:n], refs[n:2 * n]
        send_ref, recv_ref, token = refs[2 * n], refs[2 * n + 1], refs[-1]
        x, y, c = _position()
        for k in range(n):
            pltpu.make_async_remote_copy(
                src_ref=src[k].at[:, 1 - c] if other_half else src[k], dst_ref=ld[k], send_sem=send_ref.at[k],
                recv_sem=recv_ref.at[k], device_id=(x, y, 1 - c), device_id_type=MESH).start()
        token[...] = jnp.zeros_like(token)

    arrs = list(ins) + lands
    out = pl.pallas_call(
        body, name=name,
        out_shape=(pltpu.SemaphoreType.DMA((n,)), pltpu.SemaphoreType.DMA((n,)))
        + tuple(pltpu.HBM(a.shape, a.dtype) for a in arrs) + (TOKEN,),
        in_specs=[HBM_SPEC] * (2 * n),
        out_specs=(SEM_SPEC, SEM_SPEC) + (HBM_SPEC,) * (2 * n) + (pl.BlockSpec(memory_space=pltpu.VMEM),),
        input_output_aliases={i: 2 + i for i in range(2 * n)}, compiler_params=SPLIT_PARAMS,
    )(*[_hbm(a) for a in arrs])
    return out[0], out[1], out[2:2 + n], out[2 + n:2 + 2 * n], out[-1]


def _swap_wait(ins, lands, send, recv, after, name, other_half=False, first=0):
    n = len(ins)

    def body(*refs):
        src, ld = refs[:n], refs[n:2 * n]
        send_ref, recv_ref = refs[2 * n], refs[2 * n + 1]
        x, y, c = _position()
        for k in range(n):
            cp = pltpu.make_async_remote_copy(
                src_ref=src[k].at[:, 1 - c] if other_half else src[k], dst_ref=ld[k], send_sem=send_ref.at[first + k],
                recv_sem=recv_ref.at[first + k], device_id=(x, y, 1 - c), device_id_type=MESH)
            cp.wait_send()
            cp.wait_recv()

    arrs = list(ins) + list(lands)
    out = pl.pallas_call(
        body, name=name, out_shape=tuple(pltpu.HBM(a.shape, a.dtype) for a in arrs),
        in_specs=[HBM_SPEC] * (2 * n) + [SEM_SPEC, SEM_SPEC, ANY], out_specs=(HBM_SPEC,) * (2 * n),
        input_output_aliases={i: i for i in range(2 * n)}, compiler_params=SPLIT_PARAMS,
    )(*arrs, send, recv, after)
    return out[:n], out[n:]


def _forward_start(lands, cols, shard_shapes, name):
    m = len(lands)

    def body(*refs):
        ld = refs[:m]
        send_ref, recv_ref, token = refs[m], refs[m + 1], refs[-1]
        x, y, c = _position()
        for a in range(m):
            r, cw = shard_shapes[a]
            for j, (px, py) in enumerate([(1 - x, y), (x, 1 - y), (1 - x, 1 - y)]):
                win = _weight_window(ld[a], cols[a], r, cw, px, py, c)
                pltpu.make_async_remote_copy(src_ref=win, dst_ref=win, send_sem=send_ref.at[3 * a + j],
                                             recv_sem=recv_ref.at[3 * a + j], device_id=(x, y, 1 - c),
                                             device_id_type=MESH).start()
        token[...] = jnp.zeros_like(token)

    out = pl.pallas_call(
        body, name=name,
        out_shape=(pltpu.SemaphoreType.DMA((m * 3,)), pltpu.SemaphoreType.DMA((m * 3,)))
        + tuple(pltpu.HBM(a.shape, a.dtype) for a in lands) + (TOKEN,),
        in_specs=[HBM_SPEC] * m,
        out_specs=(SEM_SPEC, SEM_SPEC) + (HBM_SPEC,) * m + (pl.BlockSpec(memory_space=pltpu.VMEM),),
        input_output_aliases={i: 2 + i for i in range(m)}, compiler_params=SPLIT_PARAMS,
    )(*[_hbm(a) for a in lands])
    return out[0], out[1], out[2:2 + m], out[-1]


def _forward_wait(lands, send, recv, after, cols, shard_shapes, name):
    m = len(lands)

    def body(*refs):
        ld = refs[:m]
        send_ref, recv_ref = refs[m], refs[m + 1]
        x, y, c = _position()
        for a in range(m):
            r, cw = shard_shapes[a]
            for j, (px, py) in enumerate([(1 - x, y), (x, 1 - y), (1 - x, 1 - y)]):
                cp = pltpu.make_async_remote_copy(
                    src_ref=_weight_window(ld[a], cols[a], r, cw, px, py, c),
                    dst_ref=_weight_window(ld[a], cols[a], r, cw, px, py, 1 - c),
                    send_sem=send_ref.at[3 * a + j], recv_sem=recv_ref.at[3 * a + j], device_id=(x, y, 1 - c),
                    device_id_type=MESH)
                cp.wait_send()
                cp.wait_recv()

    return pl.pallas_call(
        body, name=name, out_shape=tuple(pltpu.HBM(a.shape, a.dtype) for a in lands),
        in_specs=[HBM_SPEC] * m + [SEM_SPEC, SEM_SPEC, ANY], out_specs=(HBM_SPEC,) * m,
        input_output_aliases={i: i for i in range(m)}, compiler_params=SPLIT_PARAMS,
    )(*lands, send, recv, after)


def _exchange_start(sums, name):
    m = len(sums)
    lands = [lax.empty((3,) + s.shape[1:], s.dtype) for s in sums]

    def body(*refs):
        cs, ld = refs[:m], refs[m:2 * m]
        send_ref, recv_ref, token = refs[2 * m], refs[2 * m + 1], refs[-1]
        x, y, c = _position()
        for a in range(m):
            for j, (px, py) in enumerate([(1 - x, y), (x, 1 - y), (1 - x, 1 - y)]):
                pltpu.make_async_remote_copy(
                    src_ref=cs[a].at[2 * px + py], dst_ref=ld[a].at[j], send_sem=send_ref.at[3 * a + j],
                    recv_sem=recv_ref.at[3 * a + j], device_id=(px, py, c), device_id_type=MESH).start()
        token[...] = jnp.zeros_like(token)

    arrs = list(sums) + lands
    out = pl.pallas_call(
        body, name=name,
        out_shape=(pltpu.SemaphoreType.DMA((m * 3,)), pltpu.SemaphoreType.DMA((m * 3,)))
        + tuple(pltpu.HBM(a.shape, a.dtype) for a in arrs) + (TOKEN,),
        in_specs=[HBM_SPEC] * (2 * m),
        out_specs=(SEM_SPEC, SEM_SPEC) + (HBM_SPEC,) * (2 * m) + (pl.BlockSpec(memory_space=pltpu.VMEM),),
        input_output_aliases={i: 2 + i for i in range(2 * m)}, compiler_params=SPLIT_PARAMS,
    )(*[_hbm(a) for a in arrs])
    return out[0], out[1], out[2:2 + m], out[2 + m:2 + 2 * m], out[-1]


def _exchange_wait(sums, lands, send, recv, after, name):
    m = len(sums)

    def body(*refs):
        cs, ld = refs[:m], refs[m:2 * m]
        send_ref, recv_ref = refs[2 * m], refs[2 * m + 1]
        x, y, c = _position()
        for a in range(m):
            for j, (px, py) in enumerate([(1 - x, y), (x, 1 - y), (1 - x, 1 - y)]):
                cp = pltpu.make_async_remote_copy(
                    src_ref=cs[a].at[2 * px + py], dst_ref=ld[a].at[j], send_sem=send_ref.at[3 * a + j],
                    recv_sem=recv_ref.at[3 * a + j], device_id=(px, py, c), device_id_type=MESH)
                cp.wait_send()
                cp.wait_recv()

    arrs = list(sums) + list(lands)
    out = pl.pallas_call(
        body, name=name, out_shape=tuple(pltpu.HBM(a.shape, a.dtype) for a in arrs),
        in_specs=[HBM_SPEC] * (2 * m) + [SEM_SPEC, SEM_SPEC, ANY], out_specs=(HBM_SPEC,) * (2 * m),
        input_output_aliases={i: i for i in range(2 * m)}, compiler_params=SPLIT_PARAMS,
    )(*arrs, send, recv, after)
    return out[:m], out[m:]


def _place_row(payload, dev, name):
    R = payload.shape[0]
    tr = _pick(R, (512, 448, 384, 256, 192, 128, 64, 32, 16, 8))

    def body(dev_ref, x_ref, o_ref):
        o_ref[...] = x_ref[...]

    return pl.pallas_call(
        body, name=name, out_shape=jax.ShapeDtypeStruct((N_DEV, R, LANES), payload.dtype),
        grid_spec=pltpu.PrefetchScalarGridSpec(
            num_scalar_prefetch=1, grid=(R // tr,),
            in_specs=[pl.BlockSpec((tr, LANES), lambda i, dev_ref: (i, 0))],
            out_specs=pl.BlockSpec((None, tr, LANES), lambda i, dev_ref: (dev_ref[0], i, 0))),
        compiler_params=_params(("arbitrary",)),
    )(dev, payload)


def _others(x, y, c):
    return [(1 - x if fx else x, 1 - y if fy else y, 1 - c if fc else c)
            for fx in (0, 1) for fy in (0, 1) for fc in (0, 1) if fx or fy or fc]


def _broadcast_start(payload, land, name):
    def body(p_ref, l_ref, send_ref, recv_ref, p_thru, l_thru, token):
        x, y, c = _position()
        for j, peer in enumerate(_others(x, y, c)):
            pltpu.make_async_remote_copy(src_ref=p_ref, dst_ref=l_ref.at[4 * x + 2 * y + c], send_sem=send_ref.at[j],
                                         recv_sem=recv_ref.at[j], device_id=peer, device_id_type=MESH).start()
        token[...] = jnp.zeros_like(token)

    n = N_DEV - 1
    return pl.pallas_call(
        body, name=name,
        out_shape=(pltpu.SemaphoreType.DMA((n,)), pltpu.SemaphoreType.DMA((n,)), pltpu.HBM(payload.shape, payload.dtype),
                   pltpu.HBM(land.shape, land.dtype), TOKEN),
        in_specs=[HBM_SPEC, HBM_SPEC],
        out_specs=(SEM_SPEC, SEM_SPEC, HBM_SPEC, HBM_SPEC, pl.BlockSpec(memory_space=pltpu.VMEM)),
        input_output_aliases={0: 2, 1: 3}, compiler_params=SPLIT_PARAMS,
    )(_hbm(payload), _hbm(land))


def _broadcast_wait(payload, land, send, recv, after, name):
    def body(p_ref, l_ref, send_ref, recv_ref, after_ref, p_thru, l_thru):
        x, y, c = _position()
        for j, (px, py, pc) in enumerate(_others(x, y, c)):
            cp = pltpu.make_async_remote_copy(src_ref=p_ref, dst_ref=l_ref.at[4 * px + 2 * py + pc],
                                              send_sem=send_ref.at[j], recv_sem=recv_ref.at[j],
                                              device_id=(px, py, pc), device_id_type=MESH)
            cp.wait_send()
            cp.wait_recv()

    out = pl.pallas_call(
        body, name=name, out_shape=(pltpu.HBM(payload.shape, payload.dtype), pltpu.HBM(land.shape, land.dtype)),
        in_specs=[HBM_SPEC, HBM_SPEC, SEM_SPEC, SEM_SPEC, ANY], out_specs=(HBM_SPEC, HBM_SPEC),
        input_output_aliases={0: 0, 1: 1}, compiler_params=SPLIT_PARAMS,
    )(payload, land, send, recv, after)
    return out[1]


def _vec(a):
    return a.reshape(1, -1)


def _local_step(x, tgt, mod, W, P, get_w=None, on_grads=None):
    Dm = D_MODEL
    G = {k: [] for k in ("gm_w_in", "gm_w_out", "hg_w_in", "hg_w_out", "ffn_w_up", "ffn_w_down")}
    lb_all = _lb_fwd(P["hg_lb"], "lb_fwd")
    saved = []
    xs = x
    y_prev = gate_prev = None
    layer_w = [None] * DEPTH

    def wmm(xa, kind, i, mode, name):
        if layer_w[i] is not None:
            return _mm(xa, layer_w[i][kind], mode, name)
        return _mm(xa, W[kind], mode, name, b_layer=i if kind.startswith("ffn") else i // 2)

    for i in range(DEPTH):
        m = [_vec(mod[i, j * Dm:(j + 1) * Dm]) for j in range(6)]
        sh1, sc1, g1, sh2, sc2, g2 = m
        j = i // 2
        if get_w is not None:
            layer_w[i] = get_w(i, 0, xs if y_prev is None else y_prev)
        xs, h = _norm_fwd(xs, y_prev, gate_prev, _vec(P["norm_g"][i, 0]), sc1, sh1, f"norm_fwd_a{i}")
        rec = dict(x1=xs, h1=h)
        if i % 2 == 0:
            a = wmm(h, "gm_w_in", i, "nn", f"gm_in{i}")
            gated = _gm_mid_fwd(a, _vec(P["gm_ln_g"][j]), _vec(P["gm_ln_b"][j]), P["gm_w_s"][j],
                                P["gm_b_s"][j].reshape(GM_HEADS, GM_BLOCK, 1), f"gm_mid_fwd{i}")
            if get_w is not None:
                layer_w[i].update(get_w(i, 1, gated))
            y1 = wmm(gated, "gm_w_out", i, "nn", f"gm_out{i}")
            rec.update(a=a, act=gated)
        else:
            p = wmm(h, "hg_w_in", i, "nn", f"hg_in{i}")
            o, og, states = _hg_scan_fwd(p, _vec(lb_all[j]), _vec(P["hg_gn_g"][j]), f"hg_scan_fwd{i}")
            if get_w is not None:
                layer_w[i].update(get_w(i, 1, og))
            y1 = wmm(og, "hg_w_out", i, "nn", f"hg_out{i}")
            rec.update(a=p, act=og, o=o, states=states)
        rec["y1"] = y1
        xs, h2 = _norm_fwd(xs, y1, g1, _vec(P["norm_g"][i, 1]), sc2, sh2, f"norm_fwd_b{i}")
        a2 = wmm(h2, "ffn_w_up", i, "nn", f"ffn_up{i}")
        mm_ = _conv_fwd(a2, P["ffn_conv_w"][i], _vec(P["ffn_conv_b"][i]), f"conv_fwd{i}")
        y2 = wmm(mm_, "ffn_w_down", i, "nn", f"ffn_down{i}")
        rec.update(x2=xs, h2=h2, a2=a2, m=mm_, y2=y2, mods=m)
        saved.append(rec)
        y_prev, gate_prev = y2, g2
    dx, dy, loss, s_fg, s_gate = _loss_head(xs, y_prev, gate_prev, _vec(P["final_g"]), tgt, "loss_head")
    small = dict(final_g=s_fg, norm_g=[None] * DEPTH, dmod=[None] * DEPTH, ffn_conv_w=[None] * DEPTH,
                 ffn_conv_b=[None] * DEPTH, gm_ln_g=[None] * 2, gm_ln_b=[None] * 2, gm_w_s=[None] * 2,
                 gm_b_s=[None] * 2, hg_gn_g=[None] * 2, dlb=[None] * 2)
    for i in reversed(range(DEPTH)):
        rec = saved[i]
        sh1, sc1, g1, sh2, sc2, g2 = rec["mods"]
        j = i // 2
        d_g2 = s_gate
        dm = wmm(dy, "ffn_w_down", i, "nt", f"ffn_down_dx{i}")
        G["ffn_w_down"].append(_mm(rec["m"], dy, "tn", f"ffn_down_dw{i}", out_dtype=GRAD_WIRE))
        da2, dcw, dcb = _conv_bwd(rec["a2"], dm, P["ffn_conv_w"][i], _vec(P["ffn_conv_b"][i]), f"conv_bwd{i}")
        small["ffn_conv_w"][i], small["ffn_conv_b"][i] = dcw, dcb
        dh2 = wmm(da2, "ffn_w_up", i, "nt", f"ffn_up_dx{i}")
        G["ffn_w_up"].append(_mm(rec["h2"], da2, "tn", f"ffn_up_dw{i}", out_dtype=GRAD_WIRE, exchange_out=True))
        ng2 = _vec(P["norm_g"][i, 1])
        if on_grads is not None:
            ng2 = ng2 + on_grads(i, {k: G[k][-1] for k in ("ffn_w_up", "ffn_w_down")})
        dx, dy, s_sh2, s_x2, d_g1 = _norm_bwd(rec["x2"], dh2, dx, ng2, sc2, rec["y1"], g1, f"norm_bwd_b{i}")
        d_sc2, d_ng2 = s_x2 * ng2, s_x2 * (1.0 + sc2)
        if i % 2 == 0:
            dgated = wmm(dy, "gm_w_out", i, "nt", f"gm_out_dx{i}")
            G["gm_w_out"].append(_mm(rec["act"], dy, "tn", f"gm_out_dw{i}", out_dtype=GRAD_WIRE))
            da, dws, dbs, dlg, dlbeta = _gm_mid_bwd(
                rec["a"], dgated, _vec(P["gm_ln_g"][j]), _vec(P["gm_ln_b"][j]), P["gm_w_s"][j],
                P["gm_b_s"][j].reshape(GM_HEADS, GM_BLOCK, 1), f"gm_mid_bwd{i}")
            small["gm_w_s"][j], small["gm_b_s"][j] = dws, dbs[:, :GM_HEADS].T
            small["gm_ln_g"][j], small["gm_ln_b"][j] = dlg, dlbeta
            dh1 = wmm(da, "gm_w_in", i, "nt", f"gm_in_dx{i}")
            G["gm_w_in"].append(_mm(rec["h1"], da, "tn", f"gm_in_dw{i}", out_dtype=GRAD_WIRE, exchange_out=True))
        else:
            dog = wmm(dy, "hg_w_out", i, "nt", f"hg_out_dx{i}")
            G["hg_w_out"].append(_mm(rec["act"], dy, "tn", f"hg_out_dw{i}", out_dtype=GRAD_WIRE))
            dp, dlb, dgn = _hg_scan_bwd(rec["a"], _vec(lb_all[j]), _vec(P["hg_gn_g"][j]), rec["o"], dog,
                                        rec["states"], f"hg_scan_bwd{i}")
            small["dlb"][j], small["hg_gn_g"][j] = dlb, dgn
            dh1 = wmm(dp, "hg_w_in", i, "nt", f"hg_in_dx{i}")
            G["hg_w_in"].append(_mm(rec["h1"], dp, "tn", f"hg_in_dw{i}", out_dtype=GRAD_WIRE, exchange_out=True))
        ng1 = _vec(P["norm_g"][i, 0])
        if on_grads is not None:
            mixer = ("gm_w_in", "gm_w_out") if i % 2 == 0 else ("hg_w_in", "hg_w_out")
            ng1 = ng1 + on_grads(i, {k: G[k][-1] for k in mixer})
        if i > 0:
            prev = saved[i - 1]
            dx, dy, s_sh1, s_x1, s_gate = _norm_bwd(rec["x1"], dh1, dx, ng1, sc1, prev["y2"], prev["mods"][5],
                                                    f"norm_bwd_a{i}")
        else:
            dx, s_sh1, s_x1 = _norm_bwd(rec["x1"], dh1, dx, ng1, sc1, None, None, f"norm_bwd_a{i}")
        d_sc1, d_ng1 = s_x1 * ng1, s_x1 * (1.0 + sc1)
        small["norm_g"][i] = jnp.concatenate([d_ng1, d_ng2], axis=0)
        small["dmod"][i] = jnp.concatenate([s_sh1, d_sc1, d_g1, s_sh2, d_sc2, d_g2], axis=1)
    for k in G:
        G[k] = G[k][::-1]
    dlb_all = jnp.concatenate(small.pop("dlb"), axis=0)
    small["hg_lb"] = _lb_bwd(P["hg_lb"], dlb_all, "lb_bwd")
    return loss, dx, G, small


BIG = ("gm_w_in", "gm_w_out", "hg_w_in", "hg_w_out", "ffn_w_up", "ffn_w_down")
COL_SHARDED = dict(gm_w_in=True, gm_w_out=False, hg_w_in=True, hg_w_out=False, ffn_w_up=True, ffn_w_down=False)
LAYER_WEIGHTS = 4


def _layer_kinds(i):
    return (("gm_w_in", "gm_w_out") if i % 2 == 0 else ("hg_w_in", "hg_w_out")) + ("ffn_w_up", "ffn_w_down")


def _pack(pieces):
    flat = [p.reshape(-1).astype(F32) for p in pieces]
    offs, tot = [], 0
    for f in flat:
        offs.append((tot, f.shape[0]))
        tot += f.shape[0]
    padded = -(-tot // (8 * LANES)) * (8 * LANES)
    if padded > tot:
        flat.append(jnp.zeros((padded - tot,), F32))
    return jnp.concatenate(flat).reshape(-1, LANES), offs


def _unpack(rows, offs, shapes):
    lead = rows.shape[:-2]
    flat = rows.reshape(lead + (-1,))
    return [flat[..., o:o + n].reshape(lead + tuple(s)) for (o, n), s in zip(offs, shapes)]


def _from_chips(per_dev, axis):
    per_chip = per_dev[0::2]
    return jnp.concatenate([per_chip[s] for s in range(N_CHIPS)], axis=axis)


def kernel(x, c, gm_w_in, gm_ln_g, gm_ln_b, gm_w_s, gm_b_s, gm_w_out, hg_w_in, hg_lb, hg_gn_g, hg_w_out, ffn_w_up, ffn_conv_w, ffn_conv_b, ffn_w_down, norm_g, ada_w, ada_b, final_g, loss_target, m_gm_w_in, m_gm_ln_g, m_gm_ln_b, m_gm_w_s, m_gm_b_s, m_gm_w_out, m_hg_w_in, m_hg_lb, m_hg_gn_g, m_hg_w_out, m_ffn_w_up, m_ffn_conv_w, m_ffn_conv_b, m_ffn_w_down, m_norm_g, m_ada_w, m_ada_b, m_final_g, v_gm_w_in, v_gm_ln_g, v_gm_ln_b, v_gm_w_s, v_gm_b_s, v_gm_w_out, v_hg_w_in, v_hg_lb, v_hg_gn_g, v_hg_w_out, v_ffn_w_up, v_ffn_conv_w, v_ffn_conv_b, v_ffn_w_down, v_norm_g, v_ada_w, v_ada_b, v_final_g):
    Dm = D_MODEL
    xi, yi, ci = _position()
    chip = 2 * xi + yi
    dev = 4 * xi + 2 * yi + ci
    weights = dict(gm_w_in=gm_w_in, gm_ln_g=gm_ln_g, gm_ln_b=gm_ln_b, gm_w_s=gm_w_s, gm_b_s=gm_b_s,
                   gm_w_out=gm_w_out, hg_w_in=hg_w_in, hg_lb=hg_lb, hg_gn_g=hg_gn_g, hg_w_out=hg_w_out,
                   ffn_w_up=ffn_w_up, ffn_conv_w=ffn_conv_w, ffn_conv_b=ffn_conv_b, ffn_w_down=ffn_w_down,
                   norm_g=norm_g, ada_w=ada_w, ada_b=ada_b, final_g=final_g)
    mom_m = dict(gm_w_in=m_gm_w_in, gm_ln_g=m_gm_ln_g, gm_ln_b=m_gm_ln_b, gm_w_s=m_gm_w_s, gm_b_s=m_gm_b_s,
                 gm_w_out=m_gm_w_out, hg_w_in=m_hg_w_in, hg_lb=m_hg_lb, hg_gn_g=m_hg_gn_g, hg_w_out=m_hg_w_out,
                 ffn_w_up=m_ffn_w_up, ffn_conv_w=m_ffn_conv_w, ffn_conv_b=m_ffn_conv_b, ffn_w_down=m_ffn_w_down,
                 norm_g=m_norm_g, ada_w=m_ada_w, ada_b=m_ada_b, final_g=m_final_g)
    mom_v = dict(gm_w_in=v_gm_w_in, gm_ln_g=v_gm_ln_g, gm_ln_b=v_gm_ln_b, gm_w_s=v_gm_w_s, gm_b_s=v_gm_b_s,
                 gm_w_out=v_gm_w_out, hg_w_in=v_hg_w_in, hg_lb=v_hg_lb, hg_gn_g=v_hg_gn_g, hg_w_out=v_hg_w_out,
                 ffn_w_up=v_ffn_w_up, ffn_conv_w=v_ffn_conv_w, ffn_conv_b=v_ffn_conv_b, ffn_w_down=v_ffn_w_down,
                 norm_g=v_norm_g, ada_w=v_ada_w, ada_b=v_ada_b, final_g=v_final_g)
    order = list(weights)

    pos = jnp.stack([chip, ci]).astype(jnp.int32)
    shards, by_col = [], []
    for i in range(DEPTH):
        for k in _layer_kinds(i):
            shards.append(weights[k][i if k.startswith("ffn") else i // 2].astype(BF16))
            by_col.append(COL_SHARDED[k])
    placed = [_place_own(sh, col, pos, f"place_own{n}") for n, (sh, col) in enumerate(zip(shards, by_col))]
    gsems, sh_thru, ld_thru, _ = _gather_start(shards, placed, by_col, LAYER_WEIGHTS, "gather_start")

    pieces = [c, hg_lb, hg_gn_g, norm_g, ffn_conv_w]
    payload, offs = _pack(pieces)
    got = _allgather_small(payload, "gather_small")
    c_g, lb_g, gn_g, ng_g, cw_g = _unpack(got, offs, [p.shape for p in pieces])
    c_all = c_g.reshape(N_DEV, Dm)
    P = dict(hg_lb=_from_chips(lb_g, 1), hg_gn_g=_from_chips(gn_g, 1), norm_g=_from_chips(ng_g, 2),
             ffn_conv_w=_from_chips(cw_g, 2), gm_ln_g=gm_ln_g, gm_ln_b=gm_ln_b, gm_w_s=gm_w_s, gm_b_s=gm_b_s,
             ffn_conv_b=ffn_conv_b, final_g=final_g)

    cols = ada_w.shape[2]
    ada_b_sh = lax.dynamic_slice_in_dim(ada_b, chip * cols, cols, axis=1)
    mod_sh = _ada_fwd(c_all, ada_w, ada_b_sh, "ada_fwd")
    mod_g = _allgather_small(mod_sh.reshape(-1, LANES), "gather_mod").reshape(N_DEV, DEPTH, N_DEV, cols)
    mod_mine = lax.dynamic_index_in_dim(mod_g[0::2], dev, axis=2, keepdims=False)
    mod = jnp.transpose(mod_mine, (1, 0, 2)).reshape(DEPTH, N_CHIPS * cols)

    core = jnp.reshape(ci, (1,)).astype(jnp.int32)
    chip_arr = jnp.reshape(chip, (1,)).astype(jnp.int32)
    pending, held, prefetched, swapping = [], {}, {}, []

    def get_w(i, group, after):
        lo, hi = LAYER_WEIGHTS * i, LAYER_WEIGHTS * (i + 1)
        shapes = lambda s: [a.shape for a in shards[s]]
        out = {}
        if i == 0:
            s = slice(lo, lo + 1) if group == 0 else slice(lo + 1, hi)
            landed = _gather_wait(sh_thru[s], ld_thru[s], gsems[0], gsems[1], after, by_col[s], s.start - lo,
                                  f"gather_wait0_{group}")
            full = _forward_sibling(landed, by_col[s], shapes(s), f"gather_forward0_{group}")
            out = dict(zip(_layer_kinds(0)[s.start - lo:s.stop - lo], full))
        elif group == 0:
            s = slice(lo, hi)
            send, recv, lands = prefetched.pop(i)
            full = _forward_wait(lands, send, recv, after, by_col[s], shapes(s), f"gather_forward_wait{i}")
            out = dict(zip(_layer_kinds(i), full))
        if group == 1 and i + 1 < DEPTH:
            s = slice(hi, hi + LAYER_WEIGHTS)
            landed = _gather_wait(sh_thru[s], ld_thru[s], gsems[2 * i + 2], gsems[2 * i + 3], after, by_col[s], 0,
                                  f"gather_wait{i + 1}")
            send, recv, lands, _ = _forward_start(landed, by_col[s], shapes(s), f"gather_forward_start{i + 1}")
            prefetched[i + 1] = (send, recv, lands)
        return out

    def on_grads(i, gdict):
        if i > 0 and "ffn_w_up" in gdict:
            held[i] = gdict
            return 0.0
        gdict = {**held.pop(i, {}), **gdict}
        kinds = [k for k in _layer_kinds(i) if k in gdict]
        tag = f"{i}_ffn" if kinds[0] == "ffn_w_up" else f"{i}"
        g4 = []
        for k in kinds:
            g = gdict[k]
            if not COL_SHARDED[k]:
                R, C = g.shape
                g = g.reshape(N_CHIPS, 2, R // (2 * N_CHIPS), C)
            g4.append(g)
        token = finish_swap(g4[0]) if swapping else 0.0
        if i == 0:
            from_sib = _swap_sibling(g4, f"reduce_swap{tag}", other_half=True)
            return token + start_exchange(tag, i, kinds, g4, from_sib)
        send, recv, g_thru, lands, tok = _swap_start(g4, f"reduce_swap_start{tag}", other_half=True)
        swapping.append((tag, i, kinds, send, recv, g_thru, lands))
        return token + tok[0, 0]

    def start_exchange(tag, i, kinds, g4, from_sib):
        sums = [_add_own_half(g, r, core, f"chip_sum_{k}{i}") for g, r, k in zip(g4, from_sib, kinds)]
        send, recv, sums_thru, lands, token = _exchange_start(sums, f"reduce_start{tag}")
        pending.append((tag, i, kinds, send, recv, sums_thru, lands))
        return token[0, 0]

    def finish_swap(after):
        tag, i, kinds, send, recv, g_thru, lands = swapping.pop()
        g4, from_sib = _swap_wait(g_thru, lands, send, recv, after, f"reduce_swap_wait{tag}", other_half=True)
        return start_exchange(tag, i, kinds, g4, from_sib)

    loss_part, dx, G, small = _local_step(x[0], loss_target[0], mod, None, P, get_w, on_grads)

    sum_pieces = [loss_part[:, :1], small["final_g"], jnp.stack(small["gm_ln_g"]), jnp.stack(small["gm_ln_b"]),
                  jnp.stack(small["gm_w_s"]), jnp.stack(small["gm_b_s"]), jnp.stack(small["ffn_conv_b"]),
                  small["hg_lb"], jnp.stack(small["hg_gn_g"]), jnp.stack(small["norm_g"]),
                  jnp.stack(small["ffn_conv_w"])]
    dmod = jnp.concatenate(small["dmod"], axis=0)
    payload2, offs2 = _pack(sum_pieces + [dmod])
    placed2 = _place_row(payload2, jnp.reshape(dev, (1,)).astype(jnp.int32), "place_grads")
    bsend, brecv, p2_thru, l2_thru, small_token = _broadcast_start(payload2, placed2, "gather_grads_start")

    landed = {}
    for tag, i, kinds, send, recv, sums_thru, lands in pending:
        sums_i, lands_i = _exchange_wait(sums_thru, lands, send, recv, small_token, f"reduce_wait{tag}")
        for k, s_, l_ in zip(kinds, sums_i, lands_i):
            landed[(k, i)] = (l_, s_)
    own_halves = []
    for k in BIG:
        layers = [landed[(k, i)] for i in range(DEPTH) if (k, i) in landed]
        own_halves.append(_sum_chips([l_ for l_, _ in layers], [s_ for _, s_ in layers], chip_arr, f"sum_chips_{k}"))
    jsend, jrecv, own_thru, jlands, after = _swap_start(own_halves, "reduce_join_start")
    grads, deltas, new_m, new_v = {}, {}, {}, {}
    for n, k in enumerate(BIG):
        (own,), (recv,) = _swap_wait([own_thru[n]], [jlands[n]], jsend, jrecv, after, f"reduce_join_wait_{k}",
                                     first=n)
        grads[k], deltas[k], new_m[k], new_v[k] = _adamw_halves(
            weights[k], own, recv, mom_m[k], mom_v[k], core, f"adamw_{k}")
        after = new_v[k]

    got2 = _broadcast_wait(p2_thru, l2_thru, bsend, brecv, new_v[BIG[-1]], "gather_grads_wait")
    dmod_all = _unpack(got2, offs2[-1:], [dmod.shape])[0]
    summed = _sum_devices(got2, "sum_devices")
    (loss_s, d_final_g, d_ln_g, d_ln_b, d_ws, d_bs, d_cb, d_lb, d_gn, d_ng, d_cw) = _unpack(
        summed, offs2[:-1], [(1,), final_g.shape, gm_ln_g.shape, gm_ln_b.shape, gm_w_s.shape, gm_b_s.shape,
                             ffn_conv_b.shape, (2, Dm), (2, Dm), (DEPTH, 2, Dm), (DEPTH, 3, 2 * FFN_HIDDEN)])
    grads.update(final_g=d_final_g, gm_ln_g=d_ln_g, gm_ln_b=d_ln_b, gm_w_s=d_ws, gm_b_s=d_bs, ffn_conv_b=d_cb)
    grads["hg_lb"] = lax.dynamic_slice_in_dim(d_lb, chip * hg_lb.shape[1], hg_lb.shape[1], axis=1)
    grads["hg_gn_g"] = lax.dynamic_slice_in_dim(d_gn, chip * hg_gn_g.shape[1], hg_gn_g.shape[1], axis=1)
    grads["norm_g"] = lax.dynamic_slice_in_dim(d_ng, chip * norm_g.shape[2], norm_g.shape[2], axis=2)
    grads["ffn_conv_w"] = lax.dynamic_slice_in_dim(d_cw, chip * ffn_conv_w.shape[2], ffn_conv_w.shape[2], axis=2)
    dmod_sh = lax.dynamic_slice_in_dim(dmod_all, chip * cols, cols, axis=2)
    grads["ada_w"] = _ada_bwd(c_all, jnp.transpose(dmod_sh, (1, 0, 2)), "ada_bwd")
    grads["ada_b"] = _sum_devices(dmod_all.reshape(N_DEV, -1, LANES), "sum_ada_b").reshape(ada_b.shape)

    for k in order:
        if k in BIG:
            continue
        w = weights[k]
        shp = w.shape
        view = (-1, shp[-1]) if w.ndim > 1 else (8, -1)
        d, m2, v2 = _adamw(w.reshape(view), grads[k].reshape(view), mom_m[k].reshape(view), mom_v[k].reshape(view),
                           f"adamw_{k}")
        deltas[k], new_m[k], new_v[k] = d.reshape(shp), m2.reshape(shp), v2.reshape(shp)
        grads[k] = grads[k].reshape(shp)

    loss = loss_s.reshape(())
    return (loss, dx[None], *[grads[k] for k in order], *[deltas[k] for k in order],
            *[new_m[k] for k in order], *[new_v[k] for k in order])
```

```python
import functools

import jax
import jax.numpy as jnp
from jax import lax
from jax.experimental import pallas as pl
from jax.experimental.pallas import tpu as pltpu

F32 = jnp.float32
BF16 = jnp.bfloat16
HI = lax.Precision.HIGHEST
X3 = lax.Precision.HIGH
GRAD_WIRE = BF16
MESH = pl.DeviceIdType.MESH

D_MODEL = 1024
DEPTH = 4
EPS = 1e-6
GM_WIDTH = 2048
GM_HEADS = 8
GM_BLOCK = 128
GM_HEAD_DIM = 256
CHUNK = 64
HG_HEADS = 8
HG_DIM = 128
FFN_HIDDEN = 2816
N_CHIPS = 4
N_DEV = 8

ADAM_LR = 0.001
ADAM_B1 = 0.9
ADAM_B2 = 0.999
ADAM_EPS = 1e-08
ADAM_WD = 0.01
ADAM_STEP = 10

VMEM_LIMIT_BYTES = 56 * 1024 * 1024
ROW_TILE = 256
NORM_ROW_TILE = 512
HALF_TILES = (512, 352, 256, 128, 64)
LANES = 128

_SQRT_HALF = 0.7071067811865476
_INV_SQRT_2PI = 0.3989422804014327


def _pick(dim, prefs):
    for p in prefs:
        if dim % p == 0:
            return p
    return dim


def _params(sem):
    return pltpu.CompilerParams(dimension_semantics=sem, vmem_limit_bytes=VMEM_LIMIT_BYTES)


def _cdf(x):
    return 0.5 * (1.0 + lax.erf(x * _SQRT_HALF))


def _pdf(x):
    return jnp.exp(-0.5 * x * x) * _INV_SQRT_2PI


def _sig(x):
    return jax.nn.sigmoid(x)


def _dot(a, b, dims, prec=None):
    return lax.dot_general(a, b, (dims, ((), ())), precision=prec, preferred_element_type=F32)


NN = ((1,), (0,))
NT = ((1,), (1,))
TN = ((0,), (0,))


MM_VMEM_BUDGET = 40 * 1024 * 1024


def _mm_tiles(mode, M, N, K, a_bytes, b_bytes, exchange_out):
    tn = _pick(N, (1408, 1024, 512, 256, 128))
    tms = [t for t in (1408, 1024, 512, 256, 128) if M % t == 0 and not (exchange_out and (M // 2) % t)] or [M]
    tks = [K] + [t for t in (2816, 2048, 1408, 1024, 512, 256, 128) if t < K and K % t == 0]

    def fits(tm, tk):
        acc = tm * tn * 4 if tk < K else 0
        return 2 * tm * tk * a_bytes + 2 * tk * tn * b_bytes + 2 * tm * tn * 4 + acc <= MM_VMEM_BUDGET

    for min_tm in (min(512, tms[0]), 0):
        for tk in tks:
            for tm in tms:
                if tm >= min_tm and fits(tm, tk):
                    return tm, tn, tk
    return tms[-1], tn, tks[-1]


def _mm(a, b, mode, name, b_layer=None, out_dtype=F32, exchange_out=False):
    b2 = b.shape[-2:]
    if mode == "nn":
        (M, K), (_, N) = a.shape, b2
    elif mode == "nt":
        (M, K), (N, _) = a.shape, b2
    else:
        (K, M), (_, N) = a.shape, b2
    tm, tn, tk = _mm_tiles(mode, M, N, K, a.dtype.itemsize, b.dtype.itemsize, exchange_out)
    nk = K // tk
    dims = {"nn": NN, "nt": NT, "tn": TN}[mode]

    def body(a_ref, b_ref, o_ref, *scratch):
        part = _dot(a_ref[...].astype(BF16), b_ref[...].astype(BF16), dims)
        if nk == 1:
            o_ref[...] = part.astype(o_ref.dtype)
            return
        (acc_ref,) = scratch
        k = pl.program_id(2)

        @pl.when(k == 0)
        def _():
            acc_ref[...] = part

        @pl.when(k > 0)
        def _():
            acc_ref[...] += part

        @pl.when(k == nk - 1)
        def _():
            o_ref[...] = acc_ref[...].astype(o_ref.dtype)

    if mode == "tn":
        a_spec = pl.BlockSpec((tk, tm), lambda i, j, k: (k, i))
    else:
        a_spec = pl.BlockSpec((tm, tk), lambda i, j, k: (i, k))
    bblk = (tk, tn) if mode in ("nn", "tn") else (tn, tk)
    bidx = (lambda i, j, k: (k, j)) if mode in ("nn", "tn") else (lambda i, j, k: (j, k))
    if b_layer is None:
        b_spec = pl.BlockSpec(bblk, bidx)
    else:
        b_spec = pl.BlockSpec((None,) + bblk, lambda i, j, k: (b_layer,) + bidx(i, j, k))
    if exchange_out:
        mh, cw = M // 2, N // N_CHIPS
        assert mh % tm == 0 and cw % tn == 0
        out_shape = jax.ShapeDtypeStruct((N_CHIPS, 2, mh, cw), out_dtype)
        o_spec = pl.BlockSpec(
            (None, None, tm, tn),
            lambda i, j, k: (j // (cw // tn), i // (mh // tm), i % (mh // tm), j % (cw // tn)))
    else:
        out_shape = jax.ShapeDtypeStruct((M, N), out_dtype)
        o_spec = pl.BlockSpec((tm, tn), lambda i, j, k: (i, j))
    return pl.pallas_call(
        body, name=name, out_shape=out_shape, grid=(M // tm, N // tn, nk),
        in_specs=[a_spec, b_spec], out_specs=o_spec,
        scratch_shapes=[] if nk == 1 else [pltpu.VMEM((tm, tn), F32)],
        compiler_params=_params(("parallel", "parallel", "arbitrary")),
    )(a, b)


def _row_spec(tr, width):
    return pl.BlockSpec((tr, width), lambda i: (i, 0))


def _vec_spec(width, rows=1):
    return pl.BlockSpec((rows, width), lambda i: (0, 0))


def _norm_fwd(x, y, gate, g, sc, sh, name):
    T, Dm = x.shape
    tr = _pick(T, (NORM_ROW_TILE, ROW_TILE))
    has_res = y is not None

    def body(*refs):
        if has_res:
            x_ref, y_ref, gate_ref, g_ref, sc_ref, sh_ref, xo_ref, h_ref = refs
            xv = x_ref[...] + gate_ref[...] * y_ref[...]
            xo_ref[...] = xv
        else:
            x_ref, g_ref, sc_ref, sh_ref, h_ref = refs
            xv = x_ref[...]
        rstd = lax.rsqrt(jnp.mean(xv * xv, axis=-1, keepdims=True) + EPS)
        h_ref[...] = ((xv * rstd * g_ref[...]) * (1.0 + sc_ref[...]) + sh_ref[...]).astype(BF16)

    row, vec = _row_spec(tr, Dm), _vec_spec(Dm)
    if has_res:
        ins, in_specs = (x, y, gate, g, sc, sh), [row, row, vec, vec, vec, vec]
        out_shape = (jax.ShapeDtypeStruct((T, Dm), F32), jax.ShapeDtypeStruct((T, Dm), BF16))
        out_specs = (row, row)
    else:
        ins, in_specs = (x, g, sc, sh), [row, vec, vec, vec]
        out_shape = jax.ShapeDtypeStruct((T, Dm), BF16)
        out_specs = row
    out = pl.pallas_call(body, name=name, out_shape=out_shape, grid=(T // tr,), in_specs=in_specs,
                         out_specs=out_specs, compiler_params=_params(("parallel",)))(*ins)
    return out if has_res else (x, out)


def _norm_bwd(x, dh, dxo, g, sc, y_prev, gate_prev, name):
    T, Dm = x.shape
    tr = _pick(T, (NORM_ROW_TILE, ROW_TILE))
    has_prev = y_prev is not None

    def body(*refs):
        if has_prev:
            x_ref, dh_ref, dxo_ref, g_ref, sc_ref, yp_ref, gp_ref, dx_ref, dyp_ref, s1_ref, s2_ref, s3_ref = refs
        else:
            x_ref, dh_ref, dxo_ref, g_ref, sc_ref, dx_ref, s1_ref, s2_ref = refs

        @pl.when(pl.program_id(0) == 0)
        def _():
            s1_ref[...] = jnp.zeros_like(s1_ref)
            s2_ref[...] = jnp.zeros_like(s2_ref)
            if has_prev:
                s3_ref[...] = jnp.zeros_like(s3_ref)

        xv = x_ref[...]
        rstd = lax.rsqrt(jnp.mean(xv * xv, axis=-1, keepdims=True) + EPS)
        xhat = xv * rstd
        dh = dh_ref[...]
        dxhat = dh * (g_ref[...] * (1.0 + sc_ref[...]))
        dx = dxo_ref[...] + rstd * (dxhat - xhat * jnp.mean(dxhat * xhat, axis=-1, keepdims=True))
        dx_ref[...] = dx
        s1_ref[...] += jnp.sum(dh, axis=0, keepdims=True)
        s2_ref[...] += jnp.sum(dh * xhat, axis=0, keepdims=True)
        if has_prev:
            dyp_ref[...] = (gp_ref[...] * dx).astype(BF16)
            s3_ref[...] += jnp.sum(dx * yp_ref[...], axis=0, keepdims=True)

    row, vec = _row_spec(tr, Dm), _vec_spec(Dm)
    vshape = jax.ShapeDtypeStruct((1, Dm), F32)
    if has_prev:
        ins, in_specs = (x, dh, dxo, g, sc, y_prev, gate_prev), [row, row, row, vec, vec, row, vec]
        out_shape = (jax.ShapeDtypeStruct((T, Dm), F32), jax.ShapeDtypeStruct((T, Dm), BF16), vshape, vshape, vshape)
        out_specs = (row, row, vec, vec, vec)
    else:
        ins, in_specs = (x, dh, dxo, g, sc), [row, row, row, vec, vec]
        out_shape = (jax.ShapeDtypeStruct((T, Dm), F32), vshape, vshape)
        out_specs = (row, vec, vec)
    return pl.pallas_call(body, name=name, out_shape=out_shape, grid=(T // tr,), in_specs=in_specs,
                          out_specs=out_specs, compiler_params=_params(("arbitrary",)))(*ins)


def _loss_head(x, y, gate, fg, tgt, name):
    T, Dm = x.shape
    tr = _pick(T, (NORM_ROW_TILE, ROW_TILE))
    nsteps = T // tr

    def body(x_ref, y_ref, gate_ref, fg_ref, t_ref, dx_ref, dy_ref, loss_ref, sfg_ref, sg_ref, acc_ref):
        i = pl.program_id(0)

        @pl.when(i == 0)
        def _():
            acc_ref[...] = jnp.zeros_like(acc_ref)
            sfg_ref[...] = jnp.zeros_like(sfg_ref)
            sg_ref[...] = jnp.zeros_like(sg_ref)

        yv = y_ref[...]
        xv = x_ref[...] + gate_ref[...] * yv
        rstd = lax.rsqrt(jnp.mean(xv * xv, axis=-1, keepdims=True) + EPS)
        xhat = xv * rstd
        err = xhat * fg_ref[...] - t_ref[...]
        acc_ref[...] += jnp.sum(err * err, axis=0, keepdims=True)
        dyn = err * (1.0 / Dm)
        sfg_ref[...] += jnp.sum(dyn * xhat, axis=0, keepdims=True)
        dxhat = dyn * fg_ref[...]
        dx = rstd * (dxhat - xhat * jnp.mean(dxhat * xhat, axis=-1, keepdims=True))
        dx_ref[...] = dx
        dy_ref[...] = (gate_ref[...] * dx).astype(BF16)
        sg_ref[...] += jnp.sum(dx * yv, axis=0, keepdims=True)

        @pl.when(i == nsteps - 1)
        def _():
            total = jnp.sum(acc_ref[...], axis=1, keepdims=True) * (0.5 / Dm)
            loss_ref[...] = jnp.broadcast_to(total, loss_ref.shape)

    row, vec = _row_spec(tr, Dm), _vec_spec(Dm)
    vshape = jax.ShapeDtypeStruct((1, Dm), F32)
    return pl.pallas_call(
        body, name=name, grid=(nsteps,),
        out_shape=(jax.ShapeDtypeStruct((T, Dm), F32), jax.ShapeDtypeStruct((T, Dm), BF16),
                   jax.ShapeDtypeStruct((1, LANES), F32), vshape, vshape),
        in_specs=[row, row, vec, vec, row], out_specs=(row, row, _vec_spec(LANES), vec, vec),
        scratch_shapes=[pltpu.VMEM((1, Dm), F32)], compiler_params=_params(("arbitrary",)),
    )(x, y, gate, fg, tgt)


def _spatial_mask():
    r = lax.broadcasted_iota(jnp.int32, (GM_BLOCK, GM_BLOCK), 0) // CHUNK
    c = lax.broadcasted_iota(jnp.int32, (GM_BLOCK, GM_BLOCK), 1) // CHUNK
    return r >= c


def _gm_specs(tr):
    return [_row_spec(tr, 2 * GM_WIDTH), _vec_spec(GM_WIDTH), _vec_spec(GM_WIDTH),
            pl.BlockSpec((GM_HEADS, GM_BLOCK, GM_BLOCK), lambda i: (0, 0, 0)),
            pl.BlockSpec((GM_HEADS, GM_BLOCK, 1), lambda i: (0, 0, 0))]


def _gm_mid_fwd(a, ln_g, ln_b, ws, bs3, name):
    T = a.shape[0]
    tr = _pick(T, (ROW_TILE,))
    W = GM_WIDTH

    def body(a_ref, lg_ref, lb_ref, ws_ref, bs_ref, o_ref, vn_scr):
        av = a_ref[:, W:]
        v = av * _cdf(av)
        vc = v - jnp.mean(v, axis=-1, keepdims=True)
        rstd = lax.rsqrt(jnp.mean(vc * vc, axis=-1, keepdims=True) + EPS)
        vn_scr[...] = (vc * rstd * lg_ref[...] + lb_ref[...]).astype(BF16)
        mask = _spatial_mask()
        for h in range(GM_HEADS):
            w = jnp.where(mask, ws_ref[h], 0.0).astype(BF16)
            cs = slice(h * GM_HEAD_DIM, (h + 1) * GM_HEAD_DIM)
            for blk in range(tr // GM_BLOCK):
                rs = slice(blk * GM_BLOCK, (blk + 1) * GM_BLOCK)
                s = _dot(w, vn_scr[rs, cs], NN) + bs_ref[h]
                au = a_ref[rs, cs]
                o_ref[rs, cs] = (au * _cdf(au) * s).astype(BF16)

    return pl.pallas_call(
        body, name=name, out_shape=jax.ShapeDtypeStruct((T, W), BF16), grid=(T // tr,),
        in_specs=_gm_specs(tr), out_specs=_row_spec(tr, W),
        scratch_shapes=[pltpu.VMEM((tr, W), BF16)], compiler_params=_params(("parallel",)),
    )(a, ln_g, ln_b, ws, bs3)


def _gm_mid_bwd(a, dgated, ln_g, ln_b, ws, bs3, name):
    T = a.shape[0]
    tr = _pick(T, (ROW_TILE,))
    W = GM_WIDTH
    nsteps = T // tr

    def body(a_ref, dg_ref, lg_ref, lb_ref, ws_ref, bs_ref, da_ref, dws_ref, dbs_ref, dlg_ref, dlb_ref,
             vn_scr, vhat_scr, dvn_scr, dsum_scr):
        i = pl.program_id(0)

        @pl.when(i == 0)
        def _():
            dws_ref[...] = jnp.zeros_like(dws_ref)
            dbs_ref[...] = jnp.zeros_like(dbs_ref)
            dlg_ref[...] = jnp.zeros_like(dlg_ref)
            dlb_ref[...] = jnp.zeros_like(dlb_ref)
            dsum_scr[...] = jnp.zeros_like(dsum_scr)

        av = a_ref[:, W:]
        cdf_v = _cdf(av)
        v = av * cdf_v
        vc = v - jnp.mean(v, axis=-1, keepdims=True)
        rstd = lax.rsqrt(jnp.mean(vc * vc, axis=-1, keepdims=True) + EPS)
        vhat_scr[...] = vc * rstd
        vn_scr[...] = (vhat_scr[...] * lg_ref[...] + lb_ref[...]).astype(BF16)
        mask = _spatial_mask()
        for h in range(GM_HEADS):
            w = jnp.where(mask, ws_ref[h], 0.0).astype(BF16)
            cs = slice(h * GM_HEAD_DIM, (h + 1) * GM_HEAD_DIM)
            for blk in range(tr // GM_BLOCK):
                rs = slice(blk * GM_BLOCK, (blk + 1) * GM_BLOCK)
                vnb = vn_scr[rs, cs]
                s = _dot(w, vnb, NN) + bs_ref[h]
                au = a_ref[rs, cs]
                cdf_u = _cdf(au)
                dg = dg_ref[rs, cs]
                ds = dg * (au * cdf_u)
                da_ref[rs, cs] = (dg * s * (cdf_u + au * _pdf(au))).astype(BF16)
                dsb = ds.astype(BF16)
                dvn_scr[rs, cs] = _dot(w, dsb, TN)
                dws_ref[h] += _dot(dsb, vnb, NT)
                dsum_scr[:, cs] += ds
        dvn = dvn_scr[...]
        vhat = vhat_scr[...]
        dlg_ref[...] += jnp.sum(dvn * vhat, axis=0, keepdims=True)
        dlb_ref[...] += jnp.sum(dvn, axis=0, keepdims=True)
        dvh = dvn * lg_ref[...]
        dv = rstd * (dvh - jnp.mean(dvh, axis=-1, keepdims=True)
                     - vhat * jnp.mean(dvh * vhat, axis=-1, keepdims=True))
        da_ref[:, W:] = (dv * (cdf_v + av * _pdf(av))).astype(BF16)

        @pl.when(i == nsteps - 1)
        def _():
            for h in range(GM_HEADS):
                dws_ref[h] = jnp.where(mask, dws_ref[h], 0.0)
            col_head = lax.broadcasted_iota(jnp.int32, (W, GM_BLOCK), 0) // GM_HEAD_DIM
            sel = (col_head == lax.broadcasted_iota(jnp.int32, (W, GM_BLOCK), 1)).astype(F32)
            dbs_ref[...] = _dot(dsum_scr[...], sel, NN, HI)

    vshape = jax.ShapeDtypeStruct((1, W), F32)
    return pl.pallas_call(
        body, name=name, grid=(nsteps,),
        out_shape=(jax.ShapeDtypeStruct((T, 2 * W), BF16), jax.ShapeDtypeStruct((GM_HEADS, GM_BLOCK, GM_BLOCK), F32),
                   jax.ShapeDtypeStruct((GM_BLOCK, GM_BLOCK), F32), vshape, vshape),
        in_specs=[_gm_specs(tr)[0], _row_spec(tr, W)] + _gm_specs(tr)[1:],
        out_specs=(_row_spec(tr, 2 * W), pl.BlockSpec((GM_HEADS, GM_BLOCK, GM_BLOCK), lambda i: (0, 0, 0)),
                   pl.BlockSpec((GM_BLOCK, GM_BLOCK), lambda i: (0, 0)), _vec_spec(W), _vec_spec(W)),
        scratch_shapes=[pltpu.VMEM((tr, W), BF16), pltpu.VMEM((tr, W), F32), pltpu.VMEM((tr, W), F32),
                        pltpu.VMEM((GM_BLOCK, W), F32)],
        compiler_params=_params(("arbitrary",)),
    )(a, dgated, ln_g, ln_b, ws, bs3)


SUB = 16
EXP_CLAMP = 80.0


def _tri(lower):
    r = lax.broadcasted_iota(jnp.int32, (CHUNK, CHUNK), 0)
    c = lax.broadcasted_iota(jnp.int32, (CHUNK, CHUNK), 1)
    return (r >= c) if lower else (c >= r)


def _score_masks():
    i = lax.broadcasted_iota(jnp.int32, (CHUNK, CHUNK), 0)
    j = lax.broadcasted_iota(jnp.int32, (CHUNK, CHUNK), 1)
    bi, bj = i // SUB, j // SUB
    diag = (bi == bj) & (i >= j)
    pair = (bi % 2 == 1) & (bj == bi - 1)
    half = (i >= CHUNK // 2) & (j < CHUNK // 2)
    return diag, pair, half


def _dot01(m, x):
    x1 = x.astype(BF16)
    rest = x - x1.astype(F32)
    x2 = rest.astype(BF16)
    x3 = (rest - x2.astype(F32)).astype(BF16)
    return _dot(m, x1, NN) + (_dot(m, x2, NN) + _dot(m, x3, NN))


def _block_rows(b, offset):
    parts = []
    for blk in range(0, CHUNK, SUB):
        r = blk + offset
        parts.append(jnp.zeros((SUB, b.shape[1]), F32) if r < 0 else jnp.broadcast_to(b[r:r + 1], (SUB, b.shape[1])))
    return jnp.concatenate(parts, axis=0)


def _hg_gates(p_ref, lb_ref, lower):
    Dm = D_MODEL
    heads = []
    for h in range(HG_HEADS):
        c0 = h * HG_DIM
        qr = p_ref[:, c0:c0 + HG_DIM]
        fz = p_ref[:, Dm + c0:Dm + c0 + HG_DIM]
        lbh = lb_ref[:, c0:c0 + HG_DIM]
        sg = _sig(fz)
        f = lbh + (1.0 - lbh) * sg
        sq = _sig(qr)
        heads.append(dict(qr=qr, v=p_ref[:, 2 * Dm + c0:2 * Dm + c0 + HG_DIM],
                          gt=p_ref[:, 3 * Dm + c0:3 * Dm + c0 + HG_DIM], lbh=lbh, sg=sg, f=f, gl=jnp.log(f),
                          kk=1.0 - f, sq=sq, q=qr * sq))
    for g in heads:
        g["b"] = _dot01(lower, g.pop("gl"))
    for g in heads:
        g.update(_hg_scalings(g["q"], g["kk"], g.pop("b")))
    return heads


def _hg_scalings(q, kk, b):
    r_mid = _block_rows(b, SUB // 2 - 1)
    r_prev = _block_rows(b, -1)
    r_end = _block_rows(b, SUB - 1)
    r_half = jnp.broadcast_to(b[CHUNK // 2 - 1:CHUNK // 2], b.shape)
    bc = b[CHUNK - 1:CHUNK]
    eqs = (jnp.exp(jnp.clip(b - r_mid, -EXP_CLAMP, EXP_CLAMP)), jnp.exp(jnp.minimum(b - r_prev, 0.0)),
           jnp.exp(jnp.minimum(b - r_half, 0.0)))
    eks = (jnp.exp(jnp.clip(r_mid - b, -EXP_CLAMP, EXP_CLAMP)), jnp.exp(jnp.minimum(r_end - b, 0.0)),
           jnp.exp(jnp.minimum(r_half - b, 0.0)))
    eb = jnp.exp(b)
    ec = jnp.exp(bc - b)
    return dict(eqs=eqs, eks=eks, eb=eb, ec=ec, e_end=jnp.exp(bc), qs=[q * e for e in eqs],
                ks=[kk * e for e in eks], qe=q * eb, ke=kk * ec)


def _scores(g, masks):
    a = None
    for qs, ks, m in zip(g["qs"], g["ks"], masks):
        part = jnp.where(m, _dot(qs.astype(BF16), ks.astype(BF16), NT), 0.0)
        a = part if a is None else a + part
    return a


def _hg_scan_fwd(p, lb, gn, name):
    T = p.shape[0]
    nc = T // CHUNK
    Dm = D_MODEL

    def body(p_ref, lb_ref, gn_ref, o_ref, og_ref, so_ref, st_ref):
        @pl.when(pl.program_id(0) == 0)
        def _():
            st_ref[...] = jnp.zeros_like(st_ref)

        masks = _score_masks()
        heads = _hg_gates(p_ref, lb_ref, _tri(True).astype(BF16))
        states = [st_ref[h] for h in range(HG_HEADS)]
        scores = [_scores(g, masks) for g in heads]
        outs = [_dot(a.astype(BF16), g["v"].astype(BF16), NN) + _dot(g["qe"], st, NT, X3)
                for g, a, st in zip(heads, scores, states)]
        new_states = [st * g["e_end"] + _dot(g["v"], g["ke"], TN, X3) for g, st in zip(heads, states)]
        for h, (g, o, st, st2) in enumerate(zip(heads, outs, states, new_states)):
            cs = slice(h * HG_DIM, (h + 1) * HG_DIM)
            so_ref[0, h] = st
            st_ref[h] = st2
            o_ref[:, cs] = o
            r = lax.rsqrt(jnp.mean(o * o, axis=-1, keepdims=True) + EPS)
            gt = g["gt"]
            og_ref[:, cs] = (((o * r) * gn_ref[:, cs]).astype(F32) * (gt * _sig(gt))).astype(BF16)

    return pl.pallas_call(
        body, name=name, grid=(nc,),
        out_shape=(jax.ShapeDtypeStruct((T, Dm), F32), jax.ShapeDtypeStruct((T, Dm), BF16),
                   jax.ShapeDtypeStruct((nc, HG_HEADS, HG_DIM, HG_DIM), F32)),
        in_specs=[_row_spec(CHUNK, 4 * Dm), _vec_spec(Dm), _vec_spec(Dm)],
        out_specs=(_row_spec(CHUNK, Dm), _row_spec(CHUNK, Dm),
                   pl.BlockSpec((1, HG_HEADS, HG_DIM, HG_DIM), lambda i: (i, 0, 0, 0))),
        scratch_shapes=[pltpu.VMEM((HG_HEADS, HG_DIM, HG_DIM), F32)],
        compiler_params=_params(("arbitrary",)),
    )(p, lb, gn)


def _hg_scan_bwd(p, lb, gn, o, dog, states, name):
    T = p.shape[0]
    nc = T // CHUNK
    Dm = D_MODEL

    def rev(i):
        return nc - 1 - i

    def body(p_ref, lb_ref, gn_ref, o_ref, dog_ref, st_in_ref, dp_ref, dlb_ref, dgn_ref, dst_ref, carry_ref):
        @pl.when(pl.program_id(0) == 0)
        def _():
            dst_ref[...] = jnp.zeros_like(dst_ref)
            carry_ref[...] = jnp.zeros_like(carry_ref)
            dlb_ref[...] = jnp.zeros_like(dlb_ref)
            dgn_ref[...] = jnp.zeros_like(dgn_ref)

        upper = _tri(False).astype(BF16)
        masks = _score_masks()
        heads = _hg_gates(p_ref, lb_ref, _tri(True).astype(BF16))
        for h, g in enumerate(heads):
            cs = slice(h * HG_DIM, (h + 1) * HG_DIM)
            oh = o_ref[:, cs]
            r = lax.rsqrt(jnp.mean(oh * oh, axis=-1, keepdims=True) + EPS)
            on = oh * r
            gt = g["gt"]
            sgt = _sig(gt)
            sil = gt * sgt
            dogh = dog_ref[:, cs]
            gnh = gn_ref[:, cs]
            don = dogh * gnh * sil
            g["dgn"] = jnp.sum(dogh * on * sil, axis=0, keepdims=True)
            g["dgate"] = dogh * on * gnh * (sgt * (1.0 + gt * (1.0 - sgt)))
            g["do"] = r * (don - on * jnp.mean(don * on, axis=-1, keepdims=True))
            g["dst"] = dst_ref[h]
            g["st"] = st_in_ref[0, h]
            g["carry"] = carry_ref[h]
        for g in heads:
            g["a"] = _scores(g, masks)
            g["dob"] = g["do"].astype(BF16)
            g["da"] = _dot(g["dob"], g["v"].astype(BF16), NT)
        for g in heads:
            g["dv"] = _dot(g["a"].astype(BF16), g["dob"], TN) + _dot(g["ke"].astype(BF16), g["dst"].astype(BF16), NT)
            g["dq"] = _dot(g["do"], g["st"], NN, X3) * g["eb"]
            g["dk"] = _dot(g["v"], g["dst"], NN, X3) * g["ec"]
            g["dst2"] = g["dst"] * g["e_end"] + _dot(g["do"], g["qe"], TN, X3)
        for lvl in range(3):
            for g in heads:
                dam = jnp.where(masks[lvl], g["da"], 0.0)
                g["dq"] = g["dq"] + _dot(dam, g["ks"][lvl], NN, X3) * g["eqs"][lvl]
                g["dk"] = g["dk"] + _dot(dam, g["qs"][lvl], TN, X3) * g["eks"][lvl]
        for g in heads:
            g["dgd"] = g["q"] * g["dq"] - g["kk"] * g["dk"]
            g["dgl"] = _dot01(upper, g["dgd"]) + g["carry"]
        for h, g in enumerate(heads):
            c0 = h * HG_DIM
            cs = slice(c0, c0 + HG_DIM)
            df = g["dgl"] / g["f"] - g["dk"]
            sg, sq, qr = g["sg"], g["sq"], g["qr"]
            dst_ref[h] = g["dst2"]
            carry_ref[h] = g["carry"] + jnp.sum(g["dgd"], axis=0, keepdims=True)
            dgn_ref[:, cs] += g["dgn"]
            dlb_ref[:, cs] += jnp.sum(df * (1.0 - sg), axis=0, keepdims=True)
            dp_ref[:, c0:c0 + HG_DIM] = (g["dq"] * (sq * (1.0 + qr * (1.0 - sq)))).astype(BF16)
            dp_ref[:, Dm + c0:Dm + c0 + HG_DIM] = (df * (1.0 - g["lbh"]) * sg * (1.0 - sg)).astype(BF16)
            dp_ref[:, 2 * Dm + c0:2 * Dm + c0 + HG_DIM] = g["dv"].astype(BF16)
            dp_ref[:, 3 * Dm + c0:3 * Dm + c0 + HG_DIM] = g["dgate"].astype(BF16)

    vshape = jax.ShapeDtypeStruct((1, Dm), F32)
    rrow = lambda w: pl.BlockSpec((CHUNK, w), lambda i: (rev(i), 0))
    return pl.pallas_call(
        body, name=name, grid=(nc,),
        out_shape=(jax.ShapeDtypeStruct((T, 4 * Dm), BF16), vshape, vshape),
        in_specs=[rrow(4 * Dm), _vec_spec(Dm), _vec_spec(Dm), rrow(Dm), rrow(Dm),
                  pl.BlockSpec((1, HG_HEADS, HG_DIM, HG_DIM), lambda i: (rev(i), 0, 0, 0))],
        out_specs=(rrow(4 * Dm), _vec_spec(Dm), _vec_spec(Dm)),
        scratch_shapes=[pltpu.VMEM((HG_HEADS, HG_DIM, HG_DIM), F32), pltpu.VMEM((HG_HEADS, 1, HG_DIM), F32)],
        compiler_params=_params(("arbitrary",)),
    )(p, lb, gn, o, dog, states)


def _lb_fwd(hg_lb, name):
    def body(a_ref, o_ref):
        a0, a1 = a_ref[0:1], a_ref[1:2]
        m = jnp.maximum(a0, a1)
        e0, e1 = jnp.exp(a0 - m), jnp.exp(a1 - m)
        p0, p1 = e0 / (e0 + e1), e1 / (e0 + e1)
        o_ref[0:1] = p0 - p0
        o_ref[1:2] = (p0 + p1) - p0

    return pl.pallas_call(body, name=name, out_shape=jax.ShapeDtypeStruct(hg_lb.shape, F32))(hg_lb)


def _lb_bwd(hg_lb, dlb_all, name):
    def body(a_ref, d_ref, o_ref):
        a0, a1 = a_ref[0:1], a_ref[1:2]
        m = jnp.maximum(a0, a1)
        e0, e1 = jnp.exp(a0 - m), jnp.exp(a1 - m)
        p0, p1 = e0 / (e0 + e1), e1 / (e0 + e1)
        d1 = d_ref[1:2]
        o_ref[0:1] = -p0 * p1 * d1
        o_ref[1:2] = p1 * (1.0 - p1) * d1

    return pl.pallas_call(body, name=name, out_shape=jax.ShapeDtypeStruct(hg_lb.shape, F32))(hg_lb, dlb_all)


CONV_COLS_FWD = 256
CONV_COLS_BWD = 128
CONV_ROWS_BWD = 128
CONV_ROWS_FWD = 512


def _conv_fwd(a, w, b, name):
    T = a.shape[0]
    Fh = FFN_HIDDEN
    tr = _pick(T, (CONV_ROWS_FWD, ROW_TILE))
    cw = CONV_COLS_FWD
    hb = tr // 8

    def body(a_ref, ap_ref, w_ref, b_ref, m_ref):
        m0 = (pl.program_id(0) > 0).astype(F32)

        def conv(cc):
            x = jnp.concatenate([ap_ref[:, pl.ds(cc, cw)] * m0, a_ref[:, pl.ds(cc, cw)]], axis=0)
            wv = w_ref[:, pl.ds(cc, cw)]
            y = b_ref[:, pl.ds(cc, cw)] + wv[2:3] * x + wv[1:2] * pltpu.roll(x, 1, axis=0) \
                + wv[0:1] * pltpu.roll(x, 2, axis=0)
            return y[8:]

        def step(c, carry):
            c0 = pl.multiple_of(c * cw, cw)
            c1 = pl.multiple_of(Fh + c * cw, cw)
            yg, yv = conv(c0), conv(c1)
            m_ref[:, pl.ds(c0, cw)] = (yg * _cdf(yg) * yv).astype(BF16)
            return carry

        lax.fori_loop(0, Fh // cw, step, 0)

    return pl.pallas_call(
        body, name=name, out_shape=jax.ShapeDtypeStruct((T, Fh), BF16), grid=(T // tr,),
        in_specs=[_row_spec(tr, 2 * Fh), pl.BlockSpec((8, 2 * Fh), lambda i: (jnp.maximum(i * hb - 1, 0), 0)),
                  _vec_spec(2 * Fh, 3), _vec_spec(2 * Fh)],
        out_specs=_row_spec(tr, Fh), compiler_params=_params(("parallel",)),
    )(a, a, w, b)


def _conv_bwd(a, dm, w, b, name):
    T = a.shape[0]
    Fh = FFN_HIDDEN
    tr = _pick(T, (CONV_ROWS_BWD, ROW_TILE))
    cw = CONV_COLS_BWD
    hb = tr // 8
    nsteps = T // tr
    n = tr + 8

    def body(a_ref, ap_ref, an_ref, dm_ref, dmn_ref, w_ref, b_ref, da_ref, dw_ref, db_ref):
        i = pl.program_id(0)
        m0 = (i > 0).astype(F32)
        m1 = (i < nsteps - 1).astype(F32)

        @pl.when(i == 0)
        def _():
            dw_ref[...] = jnp.zeros_like(dw_ref)
            db_ref[...] = jnp.zeros_like(db_ref)

        def prep(cc):
            x = jnp.concatenate([ap_ref[:, pl.ds(cc, cw)] * m0, a_ref[:, pl.ds(cc, cw)],
                                 an_ref[:, pl.ds(cc, cw)] * m1], axis=0)
            wv = w_ref[:, pl.ds(cc, cw)]
            s1 = pltpu.roll(x, 1, axis=0)
            s2 = pltpu.roll(x, 2, axis=0)
            y = b_ref[:, pl.ds(cc, cw)] + wv[2:3] * x + wv[1:2] * s1 + wv[0:1] * s2
            return wv, x[8:], s1[8:], s2[8:], y[8:]

        def back(cc, dy, wv, x0, s1, s2):
            da = wv[2:3] * dy + wv[1:2] * pltpu.roll(dy, n - 1, axis=0) + wv[0:1] * pltpu.roll(dy, n - 2, axis=0)
            da_ref[:, pl.ds(cc, cw)] = da[:tr].astype(BF16)
            d = dy[:tr]
            db_ref[:, pl.ds(cc, cw)] += jnp.sum(d, axis=0, keepdims=True)
            dw_ref[2:3, pl.ds(cc, cw)] += jnp.sum(d * x0[:tr], axis=0, keepdims=True)
            dw_ref[1:2, pl.ds(cc, cw)] += jnp.sum(d * s1[:tr], axis=0, keepdims=True)
            dw_ref[0:1, pl.ds(cc, cw)] += jnp.sum(d * s2[:tr], axis=0, keepdims=True)

        def step(c, carry):
            c0 = pl.multiple_of(c * cw, cw)
            c1 = pl.multiple_of(Fh + c * cw, cw)
            dmx = jnp.concatenate([dm_ref[:, pl.ds(c0, cw)], dmn_ref[:, pl.ds(c0, cw)] * m1], axis=0)
            wg, xg, s1g, s2g, yg = prep(c0)
            wv, xv, s1v, s2v, yv = prep(c1)
            cg = _cdf(yg)
            back(c0, dmx * yv * (cg + yg * _pdf(yg)), wg, xg, s1g, s2g)
            back(c1, dmx * (yg * cg), wv, xv, s1v, s2v)
            return carry

        lax.fori_loop(0, Fh // cw, step, 0)

    prev = lambda wd: pl.BlockSpec((8, wd), lambda i: (jnp.maximum(i * hb - 1, 0), 0))
    nxt = lambda wd: pl.BlockSpec((8, wd), lambda i: (jnp.minimum((i + 1) * hb, T // 8 - 1), 0))
    return pl.pallas_call(
        body, name=name, grid=(nsteps,),
        out_shape=(jax.ShapeDtypeStruct((T, 2 * Fh), BF16), jax.ShapeDtypeStruct((3, 2 * Fh), F32),
                   jax.ShapeDtypeStruct((1, 2 * Fh), F32)),
        in_specs=[_row_spec(tr, 2 * Fh), prev(2 * Fh), nxt(2 * Fh), _row_spec(tr, Fh), nxt(Fh),
                  _vec_spec(2 * Fh, 3), _vec_spec(2 * Fh)],
        out_specs=(_row_spec(tr, 2 * Fh), _vec_spec(2 * Fh, 3), _vec_spec(2 * Fh)),
        compiler_params=_params(("arbitrary",)),
    )(a, a, a, dm, dm, w, b)


def _ada_fwd(c_all, ada_w, ada_b, name):
    L, Dm, cols = ada_w.shape
    tn = _pick(cols, (512, 256, 128))

    def body(c_ref, w_ref, b_ref, o_ref):
        cv = c_ref[...]
        cond = (cv * _sig(cv)).astype(BF16)
        o_ref[...] = _dot(cond, w_ref[...].astype(BF16), NN) + b_ref[...]

    return pl.pallas_call(
        body, name=name, out_shape=jax.ShapeDtypeStruct((L, N_DEV, cols), F32), grid=(L, cols // tn),
        in_specs=[pl.BlockSpec((N_DEV, Dm), lambda l, j: (0, 0)), pl.BlockSpec((None, Dm, tn), lambda l, j: (l, 0, j)),
                  pl.BlockSpec((None, 1, tn), lambda l, j: (l, 0, j))],
        out_specs=pl.BlockSpec((None, N_DEV, tn), lambda l, j: (l, 0, j)),
        compiler_params=_params(("parallel", "parallel")),
    )(c_all, ada_w, ada_b.reshape(L, 1, cols))


def _ada_bwd(c_all, dmod, name):
    L, _, cols = dmod.shape
    Dm = c_all.shape[1]
    tn = _pick(cols, (512, 256, 128))

    def body(c_ref, d_ref, o_ref):
        cv = c_ref[...]
        o_ref[...] = _dot(cv * _sig(cv), d_ref[...], TN, HI)

    return pl.pallas_call(
        body, name=name, out_shape=jax.ShapeDtypeStruct((L, Dm, cols), F32), grid=(L, cols // tn),
        in_specs=[pl.BlockSpec((N_DEV, Dm), lambda l, j: (0, 0)), pl.BlockSpec((None, N_DEV, tn), lambda l, j: (l, 0, j))],
        out_specs=pl.BlockSpec((None, Dm, tn), lambda l, j: (l, 0, j)),
        compiler_params=_params(("parallel", "parallel")),
    )(c_all, dmod)


def _add_own_half(g4, rb, core, name):
    S, _, rh, cw = g4.shape
    tr = _pick(rh, HALF_TILES)

    def body(core_ref, g_ref, r_ref, o_ref):
        o_ref[...] = (g_ref[...].astype(F32) + r_ref[...].astype(F32)).astype(GRAD_WIRE)

    return pl.pallas_call(
        body, name=name, out_shape=jax.ShapeDtypeStruct((S, rh, cw), BF16),
        grid_spec=pltpu.PrefetchScalarGridSpec(
            num_scalar_prefetch=1, grid=(S, rh // tr),
            in_specs=[pl.BlockSpec((None, None, tr, cw), lambda s, i, core_ref: (s, core_ref[0], i, 0)),
                      pl.BlockSpec((None, tr, cw), lambda s, i, core_ref: (s, i, 0))],
            out_specs=pl.BlockSpec((None, tr, cw), lambda s, i, core_ref: (s, i, 0))),
        compiler_params=_params(("parallel", "parallel")),
    )(core, g4, rb)


def _sum_chips(lands, sums, chip, name):
    L = len(lands)
    _, rh, cw = lands[0].shape
    tr = _pick(rh, (256, 128, 176, 64))

    def body(chip_ref, *refs):
        ld, cs, o_ref = refs[:L], refs[L:2 * L], refs[2 * L]
        me = chip_ref[0]
        for k in range(L):
            @pl.when(pl.program_id(0) == k)
            def _(k=k):
                own = cs[k][...].astype(F32)
                got = [ld[k][j].astype(F32) for j in range(3)]
                acc = None
                for t in range(N_CHIPS):
                    d = jnp.bitwise_xor(jnp.int32(t), me)
                    term = jnp.where(d == 0, own, jnp.where(d == 2, got[0], jnp.where(d == 1, got[1], got[2])))
                    acc = term if acc is None else acc + term
                o_ref[...] = acc

    frozen = lambda l, i, k: jnp.where(l == k, i, 0)
    in_specs = [pl.BlockSpec((3, tr, cw), lambda l, i, chip_ref, k=k: (0, frozen(l, i, k), 0)) for k in range(L)]
    in_specs += [pl.BlockSpec((None, tr, cw), lambda l, i, chip_ref, k=k: (chip_ref[0], frozen(l, i, k), 0))
                 for k in range(L)]
    return pl.pallas_call(
        body, name=name, out_shape=jax.ShapeDtypeStruct((L, rh, cw), F32),
        grid_spec=pltpu.PrefetchScalarGridSpec(
            num_scalar_prefetch=1, grid=(L, rh // tr), in_specs=in_specs,
            out_specs=pl.BlockSpec((None, tr, cw), lambda l, i, chip_ref: (l, i, 0))),
        compiler_params=_params(("arbitrary", "arbitrary")),
    )(chip, *lands, *sums)


def _sum_devices(gathered, name):
    n, R, _ = gathered.shape
    tr = _pick(R, (512, 448, 384, 256, 192, 128, 64, 32, 16, 8))

    def body(g_ref, o_ref):
        acc = g_ref[0]
        for d in range(1, n):
            acc = acc + g_ref[d]
        o_ref[...] = acc

    return pl.pallas_call(
        body, name=name, out_shape=jax.ShapeDtypeStruct((R, LANES), F32), grid=(R // tr,),
        in_specs=[pl.BlockSpec((n, tr, LANES), lambda i: (0, i, 0))], out_specs=pl.BlockSpec((tr, LANES), lambda i: (i, 0)),
        compiler_params=_params(("parallel",)),
    )(gathered)


def _adamw(w, g, m, v, name):
    R, C = w.shape
    tr = _pick(R, (256, 128, 64, 32, 16, 8))
    c1 = 1.0 / (1.0 - ADAM_B1 ** ADAM_STEP)
    c2 = 1.0 / (1.0 - ADAM_B2 ** ADAM_STEP)

    def body(w_ref, g_ref, m_ref, v_ref, d_ref, mo_ref, vo_ref):
        gv = g_ref[...]
        m2 = ADAM_B1 * m_ref[...] + (1.0 - ADAM_B1) * gv
        v2 = ADAM_B2 * v_ref[...] + (1.0 - ADAM_B2) * (gv * gv)
        mo_ref[...] = m2
        vo_ref[...] = v2
        d_ref[...] = -ADAM_LR * ((m2 * c1) / (jnp.sqrt(v2 * c2) + ADAM_EPS) + ADAM_WD * w_ref[...])

    spec = pl.BlockSpec((tr, C), lambda i: (i, 0))
    shp = jax.ShapeDtypeStruct((R, C), F32)
    return pl.pallas_call(body, name=name, out_shape=(shp, shp, shp), grid=(R // tr,), in_specs=[spec] * 4,
                          out_specs=(spec, spec, spec), compiler_params=_params(("parallel",)))(w, g, m, v)


def _adamw_halves(w, own, recv, m, v, core, name):
    L, rh, cw = own.shape
    tr = _pick(rh, (256, 128, 176, 64))
    c1 = 1.0 / (1.0 - ADAM_B1 ** ADAM_STEP)
    c2 = 1.0 / (1.0 - ADAM_B2 ** ADAM_STEP)

    def body(core_ref, w_ref, own_ref, recv_ref, m_ref, v_ref, g_ref, d_ref, mo_ref, vo_ref):
        gv = jnp.where(pl.program_id(1) == core_ref[0], own_ref[...], recv_ref[...])
        g_ref[...] = gv
        m2 = ADAM_B1 * m_ref[...] + (1.0 - ADAM_B1) * gv
        v2 = ADAM_B2 * v_ref[...] + (1.0 - ADAM_B2) * (gv * gv)
        mo_ref[...] = m2
        vo_ref[...] = v2
        d_ref[...] = -ADAM_LR * ((m2 * c1) / (jnp.sqrt(v2 * c2) + ADAM_EPS) + ADAM_WD * w_ref[...])

    full = pl.BlockSpec((None, None, tr, cw), lambda l, hf, i, core_ref: (l, hf, i, 0))
    mine = pl.BlockSpec((None, tr, cw), lambda l, hf, i, core_ref: (l, jnp.where(hf == core_ref[0], i, 0), 0))
    other = pl.BlockSpec((None, tr, cw), lambda l, hf, i, core_ref: (l, jnp.where(hf == core_ref[0], 0, i), 0))
    shp = jax.ShapeDtypeStruct((L, 2, rh, cw), F32)
    view = lambda a: a.reshape(L, 2, rh, cw)
    outs = pl.pallas_call(
        body, name=name, out_shape=(shp, shp, shp, shp),
        grid_spec=pltpu.PrefetchScalarGridSpec(
            num_scalar_prefetch=1, grid=(L, 2, rh // tr), in_specs=[full, mine, other, full, full],
            out_specs=(full, full, full, full)),
        compiler_params=_params(("arbitrary", "arbitrary", "arbitrary")),
    )(core, view(w), own, recv, view(m), view(v))
    return tuple(o.reshape(L, 2 * rh, cw) for o in outs)


ANY = pl.BlockSpec(memory_space=pl.ANY)


def _position():
    x, y, c = lax.axis_index("x"), lax.axis_index("y"), lax.axis_index("c")
    return x, y, c


def _allgather(ins, out_shapes, src_fns, dst_fns, name):
    n = len(ins)

    def body(*refs):
        in_refs, out_refs = refs[:n], refs[n:2 * n]
        send_sems, recv_sems, local_sems = refs[2 * n:]
        x, y, c = _position()
        me, sibling = (x, y, c), (x, y, 1 - c)
        chips = [(1 - x, y), (x, 1 - y), (1 - x, 1 - y)]

        def copy(k, j, block, to, own=False):
            dst = dst_fns[k](out_refs[k], *block)
            return pltpu.make_async_remote_copy(
                src_ref=src_fns[k](in_refs[k], c) if own else dst, dst_ref=dst,
                send_sem=send_sems.at[k, j], recv_sem=recv_sems.at[k, j], device_id=to, device_id_type=MESH)

        mine = [pltpu.make_async_copy(src_fns[k](in_refs[k], c), dst_fns[k](out_refs[k], *me), local_sems.at[k])
                for k in range(n)]
        for cp in mine:
            cp.start()
        first = []
        for k in range(n):
            first.append(copy(k, 0, me, sibling, own=True))
            first += [copy(k, 1 + j, me, (*chip, c), own=True) for j, chip in enumerate(chips)]
        for cp in first:
            cp.start()
        passed = []
        for j, chip in enumerate(chips):
            for k in range(n):
                copy(k, 1 + j, (*chip, c), me).wait_recv()
                fwd = copy(k, 4 + j, (*chip, c), sibling)
                fwd.start()
                passed.append(fwd)
        for k in range(n):
            copy(k, 0, sibling, me).wait_recv()
        for j, chip in enumerate(chips):
            for k in range(n):
                copy(k, 4 + j, (*chip, 1 - c), me).wait_recv()
        for cp in first + passed:
            cp.wait_send()
        for cp in mine:
            cp.wait()

    spec = pl.BlockSpec(memory_space=pltpu.VMEM)
    return pl.pallas_call(
        body, name=name, out_shape=tuple(out_shapes), in_specs=[spec] * n, out_specs=tuple([spec] * n),
        scratch_shapes=[pltpu.SemaphoreType.DMA((n, 7)), pltpu.SemaphoreType.DMA((n, 7)),
                        pltpu.SemaphoreType.DMA((n,))],
        compiler_params=pltpu.CompilerParams(vmem_limit_bytes=VMEM_LIMIT_BYTES),
    )(*ins)


def _allgather_small(payload, name):
    R = payload.shape[0]
    (out,) = _allgather(
        [payload], [jax.ShapeDtypeStruct((N_DEV, R, LANES), F32)],
        [lambda ref, c: ref], [lambda ref, px, py, pc: ref.at[4 * px + 2 * py + pc]], name)
    return out


def _swap_sibling(ins, name, other_half=False):
    n = len(ins)

    def body(*refs):
        in_refs, out_refs = refs[:n], refs[n:2 * n]
        send_sems, recv_sems = refs[2 * n:]
        x, y, c = _position()
        copies = [pltpu.make_async_remote_copy(
            src_ref=in_refs[k].at[:, 1 - c] if other_half else in_refs[k], dst_ref=out_refs[k],
            send_sem=send_sems.at[k], recv_sem=recv_sems.at[k],
            device_id=(x, y, 1 - c), device_id_type=MESH) for k in range(n)]
        for cp in copies:
            cp.start()
        for cp in copies:
            cp.wait_recv()
        for cp in copies:
            cp.wait_send()

    shape = lambda a: (a.shape[0],) + a.shape[2:] if other_half else a.shape
    return pl.pallas_call(
        body, name=name, out_shape=tuple(jax.ShapeDtypeStruct(shape(a), a.dtype) for a in ins),
        in_specs=[ANY] * n, out_specs=tuple([ANY] * n),
        scratch_shapes=[pltpu.SemaphoreType.DMA((n,)), pltpu.SemaphoreType.DMA((n,))],
        compiler_params=pltpu.CompilerParams(vmem_limit_bytes=VMEM_LIMIT_BYTES),
    )(*ins)


HBM_SPEC = pl.BlockSpec(memory_space=pltpu.HBM)
SEM_SPEC = pl.BlockSpec(memory_space=pltpu.SEMAPHORE)
SPLIT_PARAMS = pltpu.CompilerParams(has_side_effects=pltpu.SideEffectType.DATAFLOW_SIDE_EFFECTING)
TOKEN = jax.ShapeDtypeStruct((8, LANES), F32)


def _hbm(a):
    return pltpu.with_memory_space_constraint(a, pltpu.HBM)


def _weight_window(ref, col, r, cw, px, py, pc):
    rh = r // 2
    if col:
        return ref.at[pl.ds(pc * rh, rh), pl.ds((2 * px + py) * cw, cw)]
    return ref.at[pl.ds((2 * px + py) * r + pc * rh, rh), :]


def _peers(x, y, c):
    return [(x, y, 1 - c), (1 - x, y, c), (x, 1 - y, c), (1 - x, 1 - y, c)]


def _place_own(shard, col, pos, name):
    r, cw = shard.shape
    rh = r // 2
    tr = _pick(rh, HALF_TILES)
    nb = rh // tr
    shape = (r, N_CHIPS * cw) if col else (N_CHIPS * r, cw)

    def body(pos_ref, x_ref, o_ref):
        o_ref[...] = x_ref[...]

    if col:
        out_idx = lambda i, pos_ref: (pos_ref[1] * nb + i, pos_ref[0])
    else:
        out_idx = lambda i, pos_ref: (pos_ref[0] * (2 * nb) + pos_ref[1] * nb + i, 0)
    return pl.pallas_call(
        body, name=name, out_shape=jax.ShapeDtypeStruct(shape, shard.dtype),
        grid_spec=pltpu.PrefetchScalarGridSpec(
            num_scalar_prefetch=1, grid=(nb,),
            in_specs=[pl.BlockSpec((tr, cw), lambda i, pos_ref: (pos_ref[1] * nb + i, 0))],
            out_specs=pl.BlockSpec((tr, cw), out_idx)),
        compiler_params=_params(("arbitrary",)),
    )(pos, shard)


def _gather_start(shards, lands, cols, per_layer, name):
    n = len(shards)
    nl = n // per_layer

    def body(*refs):
        sh, ld = refs[:n], refs[n:2 * n]
        sems, token = refs[2 * n:2 * n + 2 * nl], refs[-1]
        x, y, c = _position()
        for k in range(n):
            l, a = divmod(k, per_layer)
            r, cw = shards[k].shape
            src = sh[k].at[pl.ds(c * (r // 2), r // 2), :]
            dst = _weight_window(ld[k], cols[k], r, cw, x, y, c)
            for j, peer in enumerate(_peers(x, y, c)):
                pltpu.make_async_remote_copy(src_ref=src, dst_ref=dst, send_sem=sems[2 * l].at[4 * a + j],
                                             recv_sem=sems[2 * l + 1].at[4 * a + j], device_id=peer,
                                             device_id_type=MESH).start()
        token[...] = jnp.zeros_like(token)

    arrs = list(shards) + list(lands)
    out = pl.pallas_call(
        body, name=name,
        out_shape=tuple(pltpu.SemaphoreType.DMA((per_layer * 4,)) for _ in range(2 * nl))
        + tuple(pltpu.HBM(a.shape, a.dtype) for a in arrs) + (TOKEN,),
        in_specs=[HBM_SPEC] * (2 * n),
        out_specs=(SEM_SPEC,) * (2 * nl) + (HBM_SPEC,) * (2 * n) + (pl.BlockSpec(memory_space=pltpu.VMEM),),
        input_output_aliases={i: 2 * nl + i for i in range(2 * n)}, compiler_params=SPLIT_PARAMS,
    )(*[_hbm(a) for a in arrs])
    return out[:2 * nl], out[2 * nl:2 * nl + n], out[2 * nl + n:2 * nl + 2 * n], out[-1]


def _gather_wait(shards, lands, send, recv, after, cols, first, name):
    m = len(shards)

    def body(*refs):
        sh, ld = refs[:m], refs[m:2 * m]
        send_ref, recv_ref = refs[2 * m], refs[2 * m + 1]
        x, y, c = _position()
        for a in range(m):
            r, cw = shards[a].shape
            src = sh[a].at[pl.ds(c * (r // 2), r // 2), :]
            for j, (px, py, pc) in enumerate(_peers(x, y, c)):
                cp = pltpu.make_async_remote_copy(
                    src_ref=src, dst_ref=_weight_window(ld[a], cols[a], r, cw, px, py, pc),
                    send_sem=send_ref.at[4 * (first + a) + j], recv_sem=recv_ref.at[4 * (first + a) + j],
                    device_id=(px, py, pc),
                    device_id_type=MESH)
                cp.wait_send()
                cp.wait_recv()

    arrs = list(shards) + list(lands)
    out = pl.pallas_call(
        body, name=name, out_shape=tuple(pltpu.HBM(a.shape, a.dtype) for a in arrs),
        in_specs=[HBM_SPEC] * (2 * m) + [SEM_SPEC, SEM_SPEC, ANY], out_specs=(HBM_SPEC,) * (2 * m),
        input_output_aliases={i: i for i in range(2 * m)}, compiler_params=SPLIT_PARAMS,
    )(*arrs, send, recv, after)
    return out[m:]


def _forward_sibling(lands, cols, shard_shapes, name):
    m = len(lands)

    def body(*refs):
        ins, outs = refs[:m], refs[m:2 * m]
        send_sems, recv_sems = refs[2 * m:]
        x, y, c = _position()
        chips = [(1 - x, y), (x, 1 - y), (1 - x, 1 - y)]
        sends = []
        for a in range(m):
            r, cw = shard_shapes[a]
            for j, (px, py) in enumerate(chips):
                cp = pltpu.make_async_remote_copy(
                    src_ref=_weight_window(ins[a], cols[a], r, cw, px, py, c),
                    dst_ref=_weight_window(outs[a], cols[a], r, cw, px, py, c),
                    send_sem=send_sems.at[a, j], recv_sem=recv_sems.at[a, j], device_id=(x, y, 1 - c),
                    device_id_type=MESH)
                cp.start()
                sends.append(cp)
        for a in range(m):
            r, cw = shard_shapes[a]
            for j, (px, py) in enumerate(chips):
                pltpu.make_async_remote_copy(
                    src_ref=_weight_window(ins[a], cols[a], r, cw, px, py, c),
                    dst_ref=_weight_window(outs[a], cols[a], r, cw, px, py, 1 - c),
                    send_sem=send_sems.at[a, j], recv_sem=recv_sems.at[a, j], device_id=(x, y, 1 - c),
                    device_id_type=MESH).wait_recv()
        for cp in sends:
            cp.wait_send()

    return pl.pallas_call(
        body, name=name, out_shape=tuple(jax.ShapeDtypeStruct(a.shape, a.dtype) for a in lands),
        in_specs=[ANY] * m, out_specs=tuple([ANY] * m), input_output_aliases={i: i for i in range(m)},
        scratch_shapes=[pltpu.SemaphoreType.DMA((m, 3)), pltpu.SemaphoreType.DMA((m, 3))],
        compiler_params=pltpu.CompilerParams(vmem_limit_bytes=VMEM_LIMIT_BYTES),
    )(*lands)


def _swap_start(ins, name, other_half=False):
    n = len(ins)
    shape = lambda a: (a.shape[0],) + a.shape[2:] if other_half else a.shape
    lands = [lax.empty(shape(a), a.dtype) for a in ins]

    def body(*refs):
        src, ld = refs[:n], refs[n:2 * n]
        send_ref, recv_ref, token = refs[2 * n], refs[2 * n + 1], refs[-1]
        x, y, c = _position()
        for k in range(n):
            pltpu.make_async_remote_copy(
                src_ref=src[k].at[:, 1 - c] if other_half else src[k], dst_ref=ld[k], send_sem=send_ref.at[k],
                recv_sem=recv_ref.at[k], device_id=(x, y, 1 - c), device_id_type=MESH).start()
        token[...] = jnp.zeros_like(token)

    arrs = list(ins) + lands
    out = pl.pallas_call(
        body, name=name,
        out_shape=(pltpu.SemaphoreType.DMA((n,)), pltpu.SemaphoreType.DMA((n,)))
        + tuple(pltpu.HBM(a.shape, a.dtype) for a in arrs) + (TOKEN,),
        in_specs=[HBM_SPEC] * (2 * n),
        out_specs=(SEM_SPEC, SEM_SPEC) + (HBM_SPEC,) * (2 * n) + (pl.BlockSpec(memory_space=pltpu.VMEM),),
        input_output_aliases={i: 2 + i for i in range(2 * n)}, compiler_params=SPLIT_PARAMS,
    )(*[_hbm(a) for a in arrs])
    return out[0], out[1], out[2:2 + n], out[2 + n:2 + 2 * n], out[-1]


def _swap_wait(ins, lands, send, recv, after, name, other_half=False, first=0):
    n = len(ins)

    def body(*refs):
        src, ld = refs[:n], refs[n:2 * n]
        send_ref, recv_ref = refs[2 * n], refs[2 * n + 1]
        x, y, c = _position()
        for k in range(n):
            cp = pltpu.make_async_remote_copy(
                src_ref=src[k].at[:, 1 - c] if other_half else src[k], dst_ref=ld[k], send_sem=send_ref.at[first + k],
                recv_sem=recv_ref.at[first + k], device_id=(x, y, 1 - c), device_id_type=MESH)
            cp.wait_send()
            cp.wait_recv()

    arrs = list(ins) + list(lands)
    out = pl.pallas_call(
        body, name=name, out_shape=tuple(pltpu.HBM(a.shape, a.dtype) for a in arrs),
        in_specs=[HBM_SPEC] * (2 * n) + [SEM_SPEC, SEM_SPEC, ANY], out_specs=(HBM_SPEC,) * (2 * n),
        input_output_aliases={i: i for i in range(2 * n)}, compiler_params=SPLIT_PARAMS,
    )(*arrs, send, recv, after)
    return out[:n], out[n:]


def _forward_start(lands, cols, shard_shapes, name):
    m = len(lands)

    def body(*refs):
        ld = refs[:m]
        send_ref, recv_ref, token = refs[m], refs[m + 1], refs[-1]
        x, y, c = _position()
        for a in range(m):
            r, cw = shard_shapes[a]
            for j, (px, py) in enumerate([(1 - x, y), (x, 1 - y), (1 - x, 1 - y)]):
                win = _weight_window(ld[a], cols[a], r, cw, px, py, c)
                pltpu.make_async_remote_copy(src_ref=win, dst_ref=win, send_sem=send_ref.at[3 * a + j],
                                             recv_sem=recv_ref.at[3 * a + j], device_id=(x, y, 1 - c),
                                             device_id_type=MESH).start()
        token[...] = jnp.zeros_like(token)

    out = pl.pallas_call(
        body, name=name,
        out_shape=(pltpu.SemaphoreType.DMA((m * 3,)), pltpu.SemaphoreType.DMA((m * 3,)))
        + tuple(pltpu.HBM(a.shape, a.dtype) for a in lands) + (TOKEN,),
        in_specs=[HBM_SPEC] * m,
        out_specs=(SEM_SPEC, SEM_SPEC) + (HBM_SPEC,) * m + (pl.BlockSpec(memory_space=pltpu.VMEM),),
        input_output_aliases={i: 2 + i for i in range(m)}, compiler_params=SPLIT_PARAMS,
    )(*[_hbm(a) for a in lands])
    return out[0], out[1], out[2:2 + m], out[-1]


def _forward_wait(lands, send, recv, after, cols, shard_shapes, name):
    m = len(lands)

    def body(*refs):
        ld = refs[:m]
        send_ref, recv_ref = refs[m], refs[m + 1]
        x, y, c = _position()
        for a in range(m):
            r, cw = shard_shapes[a]
            for j, (px, py) in enumerate([(1 - x, y), (x, 1 - y), (1 - x, 1 - y)]):
                cp = pltpu.make_async_remote_copy(
                    src_ref=_weight_window(ld[a], cols[a], r, cw, px, py, c),
                    dst_ref=_weight_window(ld[a], cols[a], r, cw, px, py, 1 - c),
                    send_sem=send_ref.at[3 * a + j], recv_sem=recv_ref.at[3 * a + j], device_id=(x, y, 1 - c),
                    device_id_type=MESH)
                cp.wait_send()
                cp.wait_recv()

    return pl.pallas_call(
        body, name=name, out_shape=tuple(pltpu.HBM(a.shape, a.dtype) for a in lands),
        in_specs=[HBM_SPEC] * m + [SEM_SPEC, SEM_SPEC, ANY], out_specs=(HBM_SPEC,) * m,
        input_output_aliases={i: i for i in range(m)}, compiler_params=SPLIT_PARAMS,
    )(*lands, send, recv, after)


def _exchange_start(sums, name):
    m = len(sums)
    lands = [lax.empty((3,) + s.shape[1:], s.dtype) for s in sums]

    def body(*refs):
        cs, ld = refs[:m], refs[m:2 * m]
        send_ref, recv_ref, token = refs[2 * m], refs[2 * m + 1], refs[-1]
        x, y, c = _position()
        for a in range(m):
            for j, (px, py) in enumerate([(1 - x, y), (x, 1 - y), (1 - x, 1 - y)]):
                pltpu.make_async_remote_copy(
                    src_ref=cs[a].at[2 * px + py], dst_ref=ld[a].at[j], send_sem=send_ref.at[3 * a + j],
                    recv_sem=recv_ref.at[3 * a + j], device_id=(px, py, c), device_id_type=MESH).start()
        token[...] = jnp.zeros_like(token)

    arrs = list(sums) + lands
    out = pl.pallas_call(
        body, name=name,
        out_shape=(pltpu.SemaphoreType.DMA((m * 3,)), pltpu.SemaphoreType.DMA((m * 3,)))
        + tuple(pltpu.HBM(a.shape, a.dtype) for a in arrs) + (TOKEN,),
        in_specs=[HBM_SPEC] * (2 * m),
        out_specs=(SEM_SPEC, SEM_SPEC) + (HBM_SPEC,) * (2 * m) + (pl.BlockSpec(memory_space=pltpu.VMEM),),
        input_output_aliases={i: 2 + i for i in range(2 * m)}, compiler_params=SPLIT_PARAMS,
    )(*[_hbm(a) for a in arrs])
    return out[0], out[1], out[2:2 + m], out[2 + m:2 + 2 * m], out[-1]


def _exchange_wait(sums, lands, send, recv, after, name):
    m = len(sums)

    def body(*refs):
        cs, ld = refs[:m], refs[m:2 * m]
        send_ref, recv_ref = refs[2 * m], refs[2 * m + 1]
        x, y, c = _position()
        for a in range(m):
            for j, (px, py) in enumerate([(1 - x, y), (x, 1 - y), (1 - x, 1 - y)]):
                cp = pltpu.make_async_remote_copy(
                    src_ref=cs[a].at[2 * px + py], dst_ref=ld[a].at[j], send_sem=send_ref.at[3 * a + j],
                    recv_sem=recv_ref.at[3 * a + j], device_id=(px, py, c), device_id_type=MESH)
                cp.wait_send()
                cp.wait_recv()

    arrs = list(sums) + list(lands)
    out = pl.pallas_call(
        body, name=name, out_shape=tuple(pltpu.HBM(a.shape, a.dtype) for a in arrs),
        in_specs=[HBM_SPEC] * (2 * m) + [SEM_SPEC, SEM_SPEC, ANY], out_specs=(HBM_SPEC,) * (2 * m),
        input_output_aliases={i: i for i in range(2 * m)}, compiler_params=SPLIT_PARAMS,
    )(*arrs, send, recv, after)
    return out[:m], out[m:]


def _place_row(payload, dev, name):
    R = payload.shape[0]
    tr = _pick(R, (512, 448, 384, 256, 192, 128, 64, 32, 16, 8))

    def body(dev_ref, x_ref, o_ref):
        o_ref[...] = x_ref[...]

    return pl.pallas_call(
        body, name=name, out_shape=jax.ShapeDtypeStruct((N_DEV, R, LANES), payload.dtype),
        grid_spec=pltpu.PrefetchScalarGridSpec(
            num_scalar_prefetch=1, grid=(R // tr,),
            in_specs=[pl.BlockSpec((tr, LANES), lambda i, dev_ref: (i, 0))],
            out_specs=pl.BlockSpec((None, tr, LANES), lambda i, dev_ref: (dev_ref[0], i, 0))),
        compiler_params=_params(("arbitrary",)),
    )(dev, payload)


def _others(x, y, c):
    return [(1 - x if fx else x, 1 - y if fy else y, 1 - c if fc else c)
            for fx in (0, 1) for fy in (0, 1) for fc in (0, 1) if fx or fy or fc]


def _broadcast_start(payload, land, name):
    def body(p_ref, l_ref, send_ref, recv_ref, p_thru, l_thru, token):
        x, y, c = _position()
        for j, peer in enumerate(_others(x, y, c)):
            pltpu.make_async_remote_copy(src_ref=p_ref, dst_ref=l_ref.at[4 * x + 2 * y + c], send_sem=send_ref.at[j],
                                         recv_sem=recv_ref.at[j], device_id=peer, device_id_type=MESH).start()
        token[...] = jnp.zeros_like(token)

    n = N_DEV - 1
    return pl.pallas_call(
        body, name=name,
        out_shape=(pltpu.SemaphoreType.DMA((n,)), pltpu.SemaphoreType.DMA((n,)), pltpu.HBM(payload.shape, payload.dtype),
                   pltpu.HBM(land.shape, land.dtype), TOKEN),
        in_specs=[HBM_SPEC, HBM_SPEC],
        out_specs=(SEM_SPEC, SEM_SPEC, HBM_SPEC, HBM_SPEC, pl.BlockSpec(memory_space=pltpu.VMEM)),
        input_output_aliases={0: 2, 1: 3}, compiler_params=SPLIT_PARAMS,
    )(_hbm(payload), _hbm(land))


def _broadcast_wait(payload, land, send, recv, after, name):
    def body(p_ref, l_ref, send_ref, recv_ref, after_ref, p_thru, l_thru):
        x, y, c = _position()
        for j, (px, py, pc) in enumerate(_others(x, y, c)):
            cp = pltpu.make_async_remote_copy(src_ref=p_ref, dst_ref=l_ref.at[4 * px + 2 * py + pc],
                                              send_sem=send_ref.at[j], recv_sem=recv_ref.at[j],
                                              device_id=(px, py, pc), device_id_type=MESH)
            cp.wait_send()
            cp.wait_recv()

    out = pl.pallas_call(
        body, name=name, out_shape=(pltpu.HBM(payload.shape, payload.dtype), pltpu.HBM(land.shape, land.dtype)),
        in_specs=[HBM_SPEC, HBM_SPEC, SEM_SPEC, SEM_SPEC, ANY], out_specs=(HBM_SPEC, HBM_SPEC),
        input_output_aliases={0: 0, 1: 1}, compiler_params=SPLIT_PARAMS,
    )(payload, land, send, recv, after)
    return out[1]


def _vec(a):
    return a.reshape(1, -1)


def _local_step(x, tgt, mod, W, P, get_w=None, on_grads=None):
    Dm = D_MODEL
    G = {k: [] for k in ("gm_w_in", "gm_w_out", "hg_w_in", "hg_w_out", "ffn_w_up", "ffn_w_down")}
    lb_all = _lb_fwd(P["hg_lb"], "lb_fwd")
    saved = []
    xs = x
    y_prev = gate_prev = None
    layer_w = [None] * DEPTH

    def wmm(xa, kind, i, mode, name):
        if layer_w[i] is not None:
            return _mm(xa, layer_w[i][kind], mode, name)
        return _mm(xa, W[kind], mode, name, b_layer=i if kind.startswith("ffn") else i // 2)

    for i in range(DEPTH):
        m = [_vec(mod[i, j * Dm:(j + 1) * Dm]) for j in range(6)]
        sh1, sc1, g1, sh2, sc2, g2 = m
        j = i // 2
        if get_w is not None:
            layer_w[i] = get_w(i, 0, xs if y_prev is None else y_prev)
        xs, h = _norm_fwd(xs, y_prev, gate_prev, _vec(P["norm_g"][i, 0]), sc1, sh1, f"norm_fwd_a{i}")
        rec = dict(x1=xs, h1=h)
        if i % 2 == 0:
            a = wmm(h, "gm_w_in", i, "nn", f"gm_in{i}")
            gated = _gm_mid_fwd(a, _vec(P["gm_ln_g"][j]), _vec(P["gm_ln_b"][j]), P["gm_w_s"][j],
                                P["gm_b_s"][j].reshape(GM_HEADS, GM_BLOCK, 1), f"gm_mid_fwd{i}")
            if get_w is not None:
                layer_w[i].update(get_w(i, 1, gated))
            y1 = wmm(gated, "gm_w_out", i, "nn", f"gm_out{i}")
            rec.update(a=a, act=gated)
        else:
            p = wmm(h, "hg_w_in", i, "nn", f"hg_in{i}")
            o, og, states = _hg_scan_fwd(p, _vec(lb_all[j]), _vec(P["hg_gn_g"][j]), f"hg_scan_fwd{i}")
            if get_w is not None:
                layer_w[i].update(get_w(i, 1, og))
            y1 = wmm(og, "hg_w_out", i, "nn", f"hg_out{i}")
            rec.update(a=p, act=og, o=o, states=states)
        rec["y1"] = y1
        xs, h2 = _norm_fwd(xs, y1, g1, _vec(P["norm_g"][i, 1]), sc2, sh2, f"norm_fwd_b{i}")
        a2 = wmm(h2, "ffn_w_up", i, "nn", f"ffn_up{i}")
        mm_ = _conv_fwd(a2, P["ffn_conv_w"][i], _vec(P["ffn_conv_b"][i]), f"conv_fwd{i}")
        y2 = wmm(mm_, "ffn_w_down", i, "nn", f"ffn_down{i}")
        rec.update(x2=xs, h2=h2, a2=a2, m=mm_, y2=y2, mods=m)
        saved.append(rec)
        y_prev, gate_prev = y2, g2
    dx, dy, loss, s_fg, s_gate = _loss_head(xs, y_prev, gate_prev, _vec(P["final_g"]), tgt, "loss_head")
    small = dict(final_g=s_fg, norm_g=[None] * DEPTH, dmod=[None] * DEPTH, ffn_conv_w=[None] * DEPTH,
                 ffn_conv_b=[None] * DEPTH, gm_ln_g=[None] * 2, gm_ln_b=[None] * 2, gm_w_s=[None] * 2,
                 gm_b_s=[None] * 2, hg_gn_g=[None] * 2, dlb=[None] * 2)
    for i in reversed(range(DEPTH)):
        rec = saved[i]
        sh1, sc1, g1, sh2, sc2, g2 = rec["mods"]
        j = i // 2
        d_g2 = s_gate
        dm = wmm(dy, "ffn_w_down", i, "nt", f"ffn_down_dx{i}")
        G["ffn_w_down"].append(_mm(rec["m"], dy, "tn", f"ffn_down_dw{i}", out_dtype=GRAD_WIRE))
        da2, dcw, dcb = _conv_bwd(rec["a2"], dm, P["ffn_conv_w"][i], _vec(P["ffn_conv_b"][i]), f"conv_bwd{i}")
        small["ffn_conv_w"][i], small["ffn_conv_b"][i] = dcw, dcb
        dh2 = wmm(da2, "ffn_w_up", i, "nt", f"ffn_up_dx{i}")
        G["ffn_w_up"].append(_mm(rec["h2"], da2, "tn", f"ffn_up_dw{i}", out_dtype=GRAD_WIRE, exchange_out=True))
        ng2 = _vec(P["norm_g"][i, 1])
        if on_grads is not None:
            ng2 = ng2 + on_grads(i, {k: G[k][-1] for k in ("ffn_w_up", "ffn_w_down")})
        dx, dy, s_sh2, s_x2, d_g1 = _norm_bwd(rec["x2"], dh2, dx, ng2, sc2, rec["y1"], g1, f"norm_bwd_b{i}")
        d_sc2, d_ng2 = s_x2 * ng2, s_x2 * (1.0 + sc2)
        if i % 2 == 0:
            dgated = wmm(dy, "gm_w_out", i, "nt", f"gm_out_dx{i}")
            G["gm_w_out"].append(_mm(rec["act"], dy, "tn", f"gm_out_dw{i}", out_dtype=GRAD_WIRE))
            da, dws, dbs, dlg, dlbeta = _gm_mid_bwd(
                rec["a"], dgated, _vec(P["gm_ln_g"][j]), _vec(P["gm_ln_b"][j]), P["gm_w_s"][j],
                P["gm_b_s"][j].reshape(GM_HEADS, GM_BLOCK, 1), f"gm_mid_bwd{i}")
            small["gm_w_s"][j], small["gm_b_s"][j] = dws, dbs[:, :GM_HEADS].T
            small["gm_ln_g"][j], small["gm_ln_b"][j] = dlg, dlbeta
            dh1 = wmm(da, "gm_w_in", i, "nt", f"gm_in_dx{i}")
            G["gm_w_in"].append(_mm(rec["h1"], da, "tn", f"gm_in_dw{i}", out_dtype=GRAD_WIRE, exchange_out=True))
        else:
            dog = wmm(dy, "hg_w_out", i, "nt", f"hg_out_dx{i}")
            G["hg_w_out"].append(_mm(rec["act"], dy, "tn", f"hg_out_dw{i}", out_dtype=GRAD_WIRE))
            dp, dlb, dgn = _hg_scan_bwd(rec["a"], _vec(lb_all[j]), _vec(P["hg_gn_g"][j]), rec["o"], dog,
                                        rec["states"], f"hg_scan_bwd{i}")
            small["dlb"][j], small["hg_gn_g"][j] = dlb, dgn
            dh1 = wmm(dp, "hg_w_in", i, "nt", f"hg_in_dx{i}")
            G["hg_w_in"].append(_mm(rec["h1"], dp, "tn", f"hg_in_dw{i}", out_dtype=GRAD_WIRE, exchange_out=True))
        ng1 = _vec(P["norm_g"][i, 0])
        if on_grads is not None:
            mixer = ("gm_w_in", "gm_w_out") if i % 2 == 0 else ("hg_w_in", "hg_w_out")
            ng1 = ng1 + on_grads(i, {k: G[k][-1] for k in mixer})
        if i > 0:
            prev = saved[i - 1]
            dx, dy, s_sh1, s_x1, s_gate = _norm_bwd(rec["x1"], dh1, dx, ng1, sc1, prev["y2"], prev["mods"][5],
                                                    f"norm_bwd_a{i}")
        else:
            dx, s_sh1, s_x1 = _norm_bwd(rec["x1"], dh1, dx, ng1, sc1, None, None, f"norm_bwd_a{i}")
        d_sc1, d_ng1 = s_x1 * ng1, s_x1 * (1.0 + sc1)
        small["norm_g"][i] = jnp.concatenate([d_ng1, d_ng2], axis=0)
        small["dmod"][i] = jnp.concatenate([s_sh1, d_sc1, d_g1, s_sh2, d_sc2, d_g2], axis=1)
    for k in G:
        G[k] = G[k][::-1]
    dlb_all = jnp.concatenate(small.pop("dlb"), axis=0)
    small["hg_lb"] = _lb_bwd(P["hg_lb"], dlb_all, "lb_bwd")
    return loss, dx, G, small


BIG = ("gm_w_in", "gm_w_out", "hg_w_in", "hg_w_out", "ffn_w_up", "ffn_w_down")
COL_SHARDED = dict(gm_w_in=True, gm_w_out=False, hg_w_in=True, hg_w_out=False, ffn_w_up=True, ffn_w_down=False)
LAYER_WEIGHTS = 4


def _layer_kinds(i):
    return (("gm_w_in", "gm_w_out") if i % 2 == 0 else ("hg_w_in", "hg_w_out")) + ("ffn_w_up", "ffn_w_down")


def _pack(pieces):
    flat = [p.reshape(-1).astype(F32) for p in pieces]
    offs, tot = [], 0
    for f in flat:
        offs.append((tot, f.shape[0]))
        tot += f.shape[0]
    padded = -(-tot // (8 * LANES)) * (8 * LANES)
    if padded > tot:
        flat.append(jnp.zeros((padded - tot,), F32))
    return jnp.concatenate(flat).reshape(-1, LANES), offs


def _unpack(rows, offs, shapes):
    lead = rows.shape[:-2]
    flat = rows.reshape(lead + (-1,))
    return [flat[..., o:o + n].reshape(lead + tuple(s)) for (o, n), s in zip(offs, shapes)]


def _from_chips(per_dev, axis):
    per_chip = per_dev[0::2]
    return jnp.concatenate([per_chip[s] for s in range(N_CHIPS)], axis=axis)


def kernel(x, c, gm_w_in, gm_ln_g, gm_ln_b, gm_w_s, gm_b_s, gm_w_out, hg_w_in, hg_lb, hg_gn_g, hg_w_out, ffn_w_up, ffn_conv_w, ffn_conv_b, ffn_w_down, norm_g, ada_w, ada_b, final_g, loss_target, m_gm_w_in, m_gm_ln_g, m_gm_ln_b, m_gm_w_s, m_gm_b_s, m_gm_w_out, m_hg_w_in, m_hg_lb, m_hg_gn_g, m_hg_w_out, m_ffn_w_up, m_ffn_conv_w, m_ffn_conv_b, m_ffn_w_down, m_norm_g, m_ada_w, m_ada_b, m_final_g, v_gm_w_in, v_gm_ln_g, v_gm_ln_b, v_gm_w_s, v_gm_b_s, v_gm_w_out, v_hg_w_in, v_hg_lb, v_hg_gn_g, v_hg_w_out, v_ffn_w_up, v_ffn_conv_w, v_ffn_conv_b, v_ffn_w_down, v_norm_g, v_ada_w, v_ada_b, v_final_g):
    Dm = D_MODEL
    xi, yi, ci = _position()
    chip = 2 * xi + yi
    dev = 4 * xi + 2 * yi + ci
    weights = dict(gm_w_in=gm_w_in, gm_ln_g=gm_ln_g, gm_ln_b=gm_ln_b, gm_w_s=gm_w_s, gm_b_s=gm_b_s,
                   gm_w_out=gm_w_out, hg_w_in=hg_w_in, hg_lb=hg_lb, hg_gn_g=hg_gn_g, hg_w_out=hg_w_out,
                   ffn_w_up=ffn_w_up, ffn_conv_w=ffn_conv_w, ffn_conv_b=ffn_conv_b, ffn_w_down=ffn_w_down,
                   norm_g=norm_g, ada_w=ada_w, ada_b=ada_b, final_g=final_g)
    mom_m = dict(gm_w_in=m_gm_w_in, gm_ln_g=m_gm_ln_g, gm_ln_b=m_gm_ln_b, gm_w_s=m_gm_w_s, gm_b_s=m_gm_b_s,
                 gm_w_out=m_gm_w_out, hg_w_in=m_hg_w_in, hg_lb=m_hg_lb, hg_gn_g=m_hg_gn_g, hg_w_out=m_hg_w_out,
                 ffn_w_up=m_ffn_w_up, ffn_conv_w=m_ffn_conv_w, ffn_conv_b=m_ffn_conv_b, ffn_w_down=m_ffn_w_down,
                 norm_g=m_norm_g, ada_w=m_ada_w, ada_b=m_ada_b, final_g=m_final_g)
    mom_v = dict(gm_w_in=v_gm_w_in, gm_ln_g=v_gm_ln_g, gm_ln_b=v_gm_ln_b, gm_w_s=v_gm_w_s, gm_b_s=v_gm_b_s,
                 gm_w_out=v_gm_w_out, hg_w_in=v_hg_w_in, hg_lb=v_hg_lb, hg_gn_g=v_hg_gn_g, hg_w_out=v_hg_w_out,
                 ffn_w_up=v_ffn_w_up, ffn_conv_w=v_ffn_conv_w, ffn_conv_b=v_ffn_conv_b, ffn_w_down=v_ffn_w_down,
                 norm_g=v_norm_g, ada_w=v_ada_w, ada_b=v_ada_b, final_g=v_final_g)
    order = list(weights)

    pos = jnp.stack([chip, ci]).astype(jnp.int32)
    shards, by_col = [], []
    for i in range(DEPTH):
        for k in _layer_kinds(i):
            shards.append(weights[k][i if k.startswith("ffn") else i // 2].astype(BF16))
            by_col.append(COL_SHARDED[k])
    placed = [_place_own(sh, col, pos, f"place_own{n}") for n, (sh, col) in enumerate(zip(shards, by_col))]
    gsems, sh_thru, ld_thru, _ = _gather_start(shards, placed, by_col, LAYER_WEIGHTS, "gather_start")

    pieces = [c, hg_lb, hg_gn_g, norm_g, ffn_conv_w]
    payload, offs = _pack(pieces)
    got = _allgather_small(payload, "gather_small")
    c_g, lb_g, gn_g, ng_g, cw_g = _unpack(got, offs, [p.shape for p in pieces])
    c_all = c_g.reshape(N_DEV, Dm)
    P = dict(hg_lb=_from_chips(lb_g, 1), hg_gn_g=_from_chips(gn_g, 1), norm_g=_from_chips(ng_g, 2),
             ffn_conv_w=_from_chips(cw_g, 2), gm_ln_g=gm_ln_g, gm_ln_b=gm_ln_b, gm_w_s=gm_w_s, gm_b_s=gm_b_s,
             ffn_conv_b=ffn_conv_b, final_g=final_g)

    cols = ada_w.shape[2]
    ada_b_sh = lax.dynamic_slice_in_dim(ada_b, chip * cols, cols, axis=1)
    mod_sh = _ada_fwd(c_all, ada_w, ada_b_sh, "ada_fwd")
    mod_g = _allgather_small(mod_sh.reshape(-1, LANES), "gather_mod").reshape(N_DEV, DEPTH, N_DEV, cols)
    mod_mine = lax.dynamic_index_in_dim(mod_g[0::2], dev, axis=2, keepdims=False)
    mod = jnp.transpose(mod_mine, (1, 0, 2)).reshape(DEPTH, N_CHIPS * cols)

    core = jnp.reshape(ci, (1,)).astype(jnp.int32)
    chip_arr = jnp.reshape(chip, (1,)).astype(jnp.int32)
    pending, held, prefetched, swapping = [], {}, {}, []

    def get_w(i, group, after):
        lo, hi = LAYER_WEIGHTS * i, LAYER_WEIGHTS * (i + 1)
        shapes = lambda s: [a.shape for a in shards[s]]
        out = {}
        if i == 0:
            s = slice(lo, lo + 1) if group == 0 else slice(lo + 1, hi)
            landed = _gather_wait(sh_thru[s], ld_thru[s], gsems[0], gsems[1], after, by_col[s], s.start - lo,
                                  f"gather_wait0_{group}")
            full = _forward_sibling(landed, by_col[s], shapes(s), f"gather_forward0_{group}")
            out = dict(zip(_layer_kinds(0)[s.start - lo:s.stop - lo], full))
        elif group == 0:
            s = slice(lo, hi)
            send, recv, lands = prefetched.pop(i)
            full = _forward_wait(lands, send, recv, after, by_col[s], shapes(s), f"gather_forward_wait{i}")
            out = dict(zip(_layer_kinds(i), full))
        if group == 1 and i + 1 < DEPTH:
            s = slice(hi, hi + LAYER_WEIGHTS)
            landed = _gather_wait(sh_thru[s], ld_thru[s], gsems[2 * i + 2], gsems[2 * i + 3], after, by_col[s], 0,
                                  f"gather_wait{i + 1}")
            send, recv, lands, _ = _forward_start(landed, by_col[s], shapes(s), f"gather_forward_start{i + 1}")
            prefetched[i + 1] = (send, recv, lands)
        return out

    def on_grads(i, gdict):
        if i > 0 and "ffn_w_up" in gdict:
            held[i] = gdict
            return 0.0
        gdict = {**held.pop(i, {}), **gdict}
        kinds = [k for k in _layer_kinds(i) if k in gdict]
        tag = f"{i}_ffn" if kinds[0] == "ffn_w_up" else f"{i}"
        g4 = []
        for k in kinds:
            g = gdict[k]
            if not COL_SHARDED[k]:
                R, C = g.shape
                g = g.reshape(N_CHIPS, 2, R // (2 * N_CHIPS), C)
            g4.append(g)
        token = finish_swap(g4[0]) if swapping else 0.0
        if i == 0:
            from_sib = _swap_sibling(g4, f"reduce_swap{tag}", other_half=True)
            return token + start_exchange(tag, i, kinds, g4, from_sib)
        send, recv, g_thru, lands, tok = _swap_start(g4, f"reduce_swap_start{tag}", other_half=True)
        swapping.append((tag, i, kinds, send, recv, g_thru, lands))
        return token + tok[0, 0]

    def start_exchange(tag, i, kinds, g4, from_sib):
        sums = [_add_own_half(g, r, core, f"chip_sum_{k}{i}") for g, r, k in zip(g4, from_sib, kinds)]
        send, recv, sums_thru, lands, token = _exchange_start(sums, f"reduce_start{tag}")
        pending.append((tag, i, kinds, send, recv, sums_thru, lands))
        return token[0, 0]

    def finish_swap(after):
        tag, i, kinds, send, recv, g_thru, lands = swapping.pop()
        g4, from_sib = _swap_wait(g_thru, lands, send, recv, after, f"reduce_swap_wait{tag}", other_half=True)
        return start_exchange(tag, i, kinds, g4, from_sib)

    loss_part, dx, G, small = _local_step(x[0], loss_target[0], mod, None, P, get_w, on_grads)

    sum_pieces = [loss_part[:, :1], small["final_g"], jnp.stack(small["gm_ln_g"]), jnp.stack(small["gm_ln_b"]),
                  jnp.stack(small["gm_w_s"]), jnp.stack(small["gm_b_s"]), jnp.stack(small["ffn_conv_b"]),
                  small["hg_lb"], jnp.stack(small["hg_gn_g"]), jnp.stack(small["norm_g"]),
                  jnp.stack(small["ffn_conv_w"])]
    dmod = jnp.concatenate(small["dmod"], axis=0)
    payload2, offs2 = _pack(sum_pieces + [dmod])
    placed2 = _place_row(payload2, jnp.reshape(dev, (1,)).astype(jnp.int32), "place_grads")
    bsend, brecv, p2_thru, l2_thru, small_token = _broadcast_start(payload2, placed2, "gather_grads_start")

    landed = {}
    for tag, i, kinds, send, recv, sums_thru, lands in pending:
        sums_i, lands_i = _exchange_wait(sums_thru, lands, send, recv, small_token, f"reduce_wait{tag}")
        for k, s_, l_ in zip(kinds, sums_i, lands_i):
            landed[(k, i)] = (l_, s_)
    own_halves = []
    for k in BIG:
        layers = [landed[(k, i)] for i in range(DEPTH) if (k, i) in landed]
        own_halves.append(_sum_chips([l_ for l_, _ in layers], [s_ for _, s_ in layers], chip_arr, f"sum_chips_{k}"))
    jsend, jrecv, own_thru, jlands, after = _swap_start(own_halves, "reduce_join_start")
    grads, deltas, new_m, new_v = {}, {}, {}, {}
    for n, k in enumerate(BIG):
        (own,), (recv,) = _swap_wait([own_thru[n]], [jlands[n]], jsend, jrecv, after, f"reduce_join_wait_{k}",
                                     first=n)
        grads[k], deltas[k], new_m[k], new_v[k] = _adamw_halves(
            weights[k], own, recv, mom_m[k], mom_v[k], core, f"adamw_{k}")
        after = new_v[k]

    got2 = _broadcast_wait(p2_thru, l2_thru, bsend, brecv, new_v[BIG[-1]], "gather_grads_wait")
    dmod_all = _unpack(got2, offs2[-1:], [dmod.shape])[0]
    summed = _sum_devices(got2, "sum_devices")
    (loss_s, d_final_g, d_ln_g, d_ln_b, d_ws, d_bs, d_cb, d_lb, d_gn, d_ng, d_cw) = _unpack(
        summed, offs2[:-1], [(1,), final_g.shape, gm_ln_g.shape, gm_ln_b.shape, gm_w_s.shape, gm_b_s.shape,
                             ffn_conv_b.shape, (2, Dm), (2, Dm), (DEPTH, 2, Dm), (DEPTH, 3, 2 * FFN_HIDDEN)])
    grads.update(final_g=d_final_g, gm_ln_g=d_ln_g, gm_ln_b=d_ln_b, gm_w_s=d_ws, gm_b_s=d_bs, ffn_conv_b=d_cb)
    grads["hg_lb"] = lax.dynamic_slice_in_dim(d_lb, chip * hg_lb.shape[1], hg_lb.shape[1], axis=1)
    grads["hg_gn_g"] = lax.dynamic_slice_in_dim(d_gn, chip * hg_gn_g.shape[1], hg_gn_g.shape[1], axis=1)
    grads["norm_g"] = lax.dynamic_slice_in_dim(d_ng, chip * norm_g.shape[2], norm_g.shape[2], axis=2)
    grads["ffn_conv_w"] = lax.dynamic_slice_in_dim(d_cw, chip * ffn_conv_w.shape[2], ffn_conv_w.shape[2], axis=2)
    dmod_sh = lax.dynamic_slice_in_dim(dmod_all, chip * cols, cols, axis=2)
    grads["ada_w"] = _ada_bwd(c_all, jnp.transpose(dmod_sh, (1, 0, 2)), "ada_bwd")
    grads["ada_b"] = _sum_devices(dmod_all.reshape(N_DEV, -1, LANES), "sum_ada_b").reshape(ada_b.shape)

    for k in order:
        if k in BIG:
            continue
        w = weights[k]
        shp = w.shape
        view = (-1, shp[-1]) if w.ndim > 1 else (8, -1)
        d, m2, v2 = _adamw(w.reshape(view), grads[k].reshape(view), mom_m[k].reshape(view), mom_v[k].reshape(view),
                           f"adamw_{k}")
        deltas[k], new_m[k], new_v[k] = d.reshape(shp), m2.reshape(shp), v2.reshape(shp)
        grads[k] = grads[k].reshape(shp)

    loss = loss_s.reshape(())
    return (loss, dx[None], *[grads[k] for k in order], *[deltas[k] for k in order],
            *[new_m[k] for k in order], *[new_v[k] for k in order])
```

```python
import functools

import jax
import jax.numpy as jnp
from jax import lax
from jax.experimental import pallas as pl
from jax.experimental.pallas import tpu as pltpu

F32 = jnp.float32
BF16 = jnp.bfloat16
HI = lax.Precision.HIGHEST
X3 = lax.Precision.HIGH
GRAD_WIRE = BF16
MESH = pl.DeviceIdType.MESH

D_MODEL = 1024
DEPTH = 4
EPS = 1e-6
GM_WIDTH = 2048
GM_HEADS = 8
GM_BLOCK = 128
GM_HEAD_DIM = 256
CHUNK = 64
HG_HEADS = 8
HG_DIM = 128
FFN_HIDDEN = 2816
N_CHIPS = 4
N_DEV = 8

ADAM_LR = 0.001
ADAM_B1 = 0.9
ADAM_B2 = 0.999
ADAM_EPS = 1e-08
ADAM_WD = 0.01
ADAM_STEP = 10

VMEM_LIMIT_BYTES = 56 * 1024 * 1024
ROW_TILE = 256
NORM_ROW_TILE = 512
LANES = 128

_SQRT_HALF = 0.7071067811865476
_INV_SQRT_2PI = 0.3989422804014327


def _pick(dim, prefs):
    for p in prefs:
        if dim % p == 0:
            return p
    return dim


def _params(sem):
    return pltpu.CompilerParams(dimension_semantics=sem, vmem_limit_bytes=VMEM_LIMIT_BYTES)


def _cdf(x):
    return 0.5 * (1.0 + lax.erf(x * _SQRT_HALF))


def _pdf(x):
    return jnp.exp(-0.5 * x * x) * _INV_SQRT_2PI


def _sig(x):
    return jax.nn.sigmoid(x)


def _dot(a, b, dims, prec=None):
    return lax.dot_general(a, b, (dims, ((), ())), precision=prec, preferred_element_type=F32)


NN = ((1,), (0,))
NT = ((1,), (1,))
TN = ((0,), (0,))


MM_VMEM_BUDGET = 40 * 1024 * 1024


def _mm_tiles(mode, M, N, K, a_bytes, b_bytes, exchange_out):
    tn = _pick(N, (1408, 1024, 512, 256, 128))
    tms = [t for t in (1408, 1024, 512, 256, 128) if M % t == 0 and not (exchange_out and (M // 2) % t)] or [M]
    tks = [K] + [t for t in (2816, 2048, 1408, 1024, 512, 256, 128) if t < K and K % t == 0]

    def fits(tm, tk):
        acc = tm * tn * 4 if tk < K else 0
        return 2 * tm * tk * a_bytes + 2 * tk * tn * b_bytes + 2 * tm * tn * 4 + acc <= MM_VMEM_BUDGET

    for min_tm in (min(512, tms[0]), 0):
        for tk in tks:
            for tm in tms:
                if tm >= min_tm and fits(tm, tk):
                    return tm, tn, tk
    return tms[-1], tn, tks[-1]


def _mm(a, b, mode, name, b_layer=None, out_dtype=F32, exchange_out=False):
    b2 = b.shape[-2:]
    if mode == "nn":
        (M, K), (_, N) = a.shape, b2
    elif mode == "nt":
        (M, K), (N, _) = a.shape, b2
    else:
        (K, M), (_, N) = a.shape, b2
    tm, tn, tk = _mm_tiles(mode, M, N, K, a.dtype.itemsize, b.dtype.itemsize, exchange_out)
    nk = K // tk
    dims = {"nn": NN, "nt": NT, "tn": TN}[mode]

    def body(a_ref, b_ref, o_ref, *scratch):
        part = _dot(a_ref[...].astype(BF16), b_ref[...].astype(BF16), dims)
        if nk == 1:
            o_ref[...] = part.astype(o_ref.dtype)
            return
        (acc_ref,) = scratch
        k = pl.program_id(2)

        @pl.when(k == 0)
        def _():
            acc_ref[...] = part

        @pl.when(k > 0)
        def _():
            acc_ref[...] += part

        @pl.when(k == nk - 1)
        def _():
            o_ref[...] = acc_ref[...].astype(o_ref.dtype)

    if mode == "tn":
        a_spec = pl.BlockSpec((tk, tm), lambda i, j, k: (k, i))
    else:
        a_spec = pl.BlockSpec((tm, tk), lambda i, j, k: (i, k))
    bblk = (tk, tn) if mode in ("nn", "tn") else (tn, tk)
    bidx = (lambda i, j, k: (k, j)) if mode in ("nn", "tn") else (lambda i, j, k: (j, k))
    if b_layer is None:
        b_spec = pl.BlockSpec(bblk, bidx)
    else:
        b_spec = pl.BlockSpec((None,) + bblk, lambda i, j, k: (b_layer,) + bidx(i, j, k))
    if exchange_out:
        mh, cw = M // 2, N // N_CHIPS
        assert mh % tm == 0 and cw % tn == 0
        out_shape = jax.ShapeDtypeStruct((N_CHIPS, 2, mh, cw), out_dtype)
        o_spec = pl.BlockSpec(
            (None, None, tm, tn),
            lambda i, j, k: (j // (cw // tn), i // (mh // tm), i % (mh // tm), j % (cw // tn)))
    else:
        out_shape = jax.ShapeDtypeStruct((M, N), out_dtype)
        o_spec = pl.BlockSpec((tm, tn), lambda i, j, k: (i, j))
    return pl.pallas_call(
        body, name=name, out_shape=out_shape, grid=(M // tm, N // tn, nk),
        in_specs=[a_spec, b_spec], out_specs=o_spec,
        scratch_shapes=[] if nk == 1 else [pltpu.VMEM((tm, tn), F32)],
        compiler_params=_params(("parallel", "parallel", "arbitrary")),
    )(a, b)


def _row_spec(tr, width):
    return pl.BlockSpec((tr, width), lambda i: (i, 0))


def _vec_spec(width, rows=1):
    return pl.BlockSpec((rows, width), lambda i: (0, 0))


def _norm_fwd(x, y, gate, g, sc, sh, name):
    T, Dm = x.shape
    tr = _pick(T, (NORM_ROW_TILE, ROW_TILE))
    has_res = y is not None

    def body(*refs):
        if has_res:
            x_ref, y_ref, gate_ref, g_ref, sc_ref, sh_ref, xo_ref, h_ref = refs
            xv = x_ref[...] + gate_ref[...] * y_ref[...]
            xo_ref[...] = xv
        else:
            x_ref, g_ref, sc_ref, sh_ref, h_ref = refs
            xv = x_ref[...]
        rstd = lax.rsqrt(jnp.mean(xv * xv, axis=-1, keepdims=True) + EPS)
        h_ref[...] = ((xv * rstd * g_ref[...]) * (1.0 + sc_ref[...]) + sh_ref[...]).astype(BF16)

    row, vec = _row_spec(tr, Dm), _vec_spec(Dm)
    if has_res:
        ins, in_specs = (x, y, gate, g, sc, sh), [row, row, vec, vec, vec, vec]
        out_shape = (jax.ShapeDtypeStruct((T, Dm), F32), jax.ShapeDtypeStruct((T, Dm), BF16))
        out_specs = (row, row)
    else:
        ins, in_specs = (x, g, sc, sh), [row, vec, vec, vec]
        out_shape = jax.ShapeDtypeStruct((T, Dm), BF16)
        out_specs = row
    out = pl.pallas_call(body, name=name, out_shape=out_shape, grid=(T // tr,), in_specs=in_specs,
                         out_specs=out_specs, compiler_params=_params(("parallel",)))(*ins)
    return out if has_res else (x, out)


def _norm_bwd(x, dh, dxo, g, sc, y_prev, gate_prev, name):
    T, Dm = x.shape
    tr = _pick(T, (NORM_ROW_TILE, ROW_TILE))
    has_prev = y_prev is not None

    def body(*refs):
        if has_prev:
            x_ref, dh_ref, dxo_ref, g_ref, sc_ref, yp_ref, gp_ref, dx_ref, dyp_ref, s1_ref, s2_ref, s3_ref = refs
        else:
            x_ref, dh_ref, dxo_ref, g_ref, sc_ref, dx_ref, s1_ref, s2_ref = refs

        @pl.when(pl.program_id(0) == 0)
        def _():
            s1_ref[...] = jnp.zeros_like(s1_ref)
            s2_ref[...] = jnp.zeros_like(s2_ref)
            if has_prev:
                s3_ref[...] = jnp.zeros_like(s3_ref)

        xv = x_ref[...]
        rstd = lax.rsqrt(jnp.mean(xv * xv, axis=-1, keepdims=True) + EPS)
        xhat = xv * rstd
        dh = dh_ref[...]
        dxhat = dh * (g_ref[...] * (1.0 + sc_ref[...]))
        dx = dxo_ref[...] + rstd * (dxhat - xhat * jnp.mean(dxhat * xhat, axis=-1, keepdims=True))
        dx_ref[...] = dx
        s1_ref[...] += jnp.sum(dh, axis=0, keepdims=True)
        s2_ref[...] += jnp.sum(dh * xhat, axis=0, keepdims=True)
        if has_prev:
            dyp_ref[...] = (gp_ref[...] * dx).astype(BF16)
            s3_ref[...] += jnp.sum(dx * yp_ref[...], axis=0, keepdims=True)

    row, vec = _row_spec(tr, Dm), _vec_spec(Dm)
    vshape = jax.ShapeDtypeStruct((1, Dm), F32)
    if has_prev:
        ins, in_specs = (x, dh, dxo, g, sc, y_prev, gate_prev), [row, row, row, vec, vec, row, vec]
        out_shape = (jax.ShapeDtypeStruct((T, Dm), F32), jax.ShapeDtypeStruct((T, Dm), BF16), vshape, vshape, vshape)
        out_specs = (row, row, vec, vec, vec)
    else:
        ins, in_specs = (x, dh, dxo, g, sc), [row, row, row, vec, vec]
        out_shape = (jax.ShapeDtypeStruct((T, Dm), F32), vshape, vshape)
        out_specs = (row, vec, vec)
    return pl.pallas_call(body, name=name, out_shape=out_shape, grid=(T // tr,), in_specs=in_specs,
                          out_specs=out_specs, compiler_params=_params(("arbitrary",)))(*ins)


def _loss_head(x, y, gate, fg, tgt, name):
    T, Dm = x.shape
    tr = _pick(T, (NORM_ROW_TILE, ROW_TILE))
    nsteps = T // tr

    def body(x_ref, y_ref, gate_ref, fg_ref, t_ref, dx_ref, dy_ref, loss_ref, sfg_ref, sg_ref, acc_ref):
        i = pl.program_id(0)

        @pl.when(i == 0)
        def _():
            acc_ref[...] = jnp.zeros_like(acc_ref)
            sfg_ref[...] = jnp.zeros_like(sfg_ref)
            sg_ref[...] = jnp.zeros_like(sg_ref)

        yv = y_ref[...]
        xv = x_ref[...] + gate_ref[...] * yv
        rstd = lax.rsqrt(jnp.mean(xv * xv, axis=-1, keepdims=True) + EPS)
        xhat = xv * rstd
        err = xhat * fg_ref[...] - t_ref[...]
        acc_ref[...] += jnp.sum(err * err, axis=0, keepdims=True)
        dyn = err * (1.0 / Dm)
        sfg_ref[...] += jnp.sum(dyn * xhat, axis=0, keepdims=True)
        dxhat = dyn * fg_ref[...]
        dx = rstd * (dxhat - xhat * jnp.mean(dxhat * xhat, axis=-1, keepdims=True))
        dx_ref[...] = dx
        dy_ref[...] = (gate_ref[...] * dx).astype(BF16)
        sg_ref[...] += jnp.sum(dx * yv, axis=0, keepdims=True)

        @pl.when(i == nsteps - 1)
        def _():
            total = jnp.sum(acc_ref[...], axis=1, keepdims=True) * (0.5 / Dm)
            loss_ref[...] = jnp.broadcast_to(total, loss_ref.shape)

    row, vec = _row_spec(tr, Dm), _vec_spec(Dm)
    vshape = jax.ShapeDtypeStruct((1, Dm), F32)
    return pl.pallas_call(
        body, name=name, grid=(nsteps,),
        out_shape=(jax.ShapeDtypeStruct((T, Dm), F32), jax.ShapeDtypeStruct((T, Dm), BF16),
                   jax.ShapeDtypeStruct((1, LANES), F32), vshape, vshape),
        in_specs=[row, row, vec, vec, row], out_specs=(row, row, _vec_spec(LANES), vec, vec),
        scratch_shapes=[pltpu.VMEM((1, Dm), F32)], compiler_params=_params(("arbitrary",)),
    )(x, y, gate, fg, tgt)


def _spatial_mask():
    r = lax.broadcasted_iota(jnp.int32, (GM_BLOCK, GM_BLOCK), 0) // CHUNK
    c = lax.broadcasted_iota(jnp.int32, (GM_BLOCK, GM_BLOCK), 1) // CHUNK
    return r >= c


def _gm_specs(tr):
    return [_row_spec(tr, 2 * GM_WIDTH), _vec_spec(GM_WIDTH), _vec_spec(GM_WIDTH),
            pl.BlockSpec((GM_HEADS, GM_BLOCK, GM_BLOCK), lambda i: (0, 0, 0)),
            pl.BlockSpec((GM_HEADS, GM_BLOCK, 1), lambda i: (0, 0, 0))]


def _gm_mid_fwd(a, ln_g, ln_b, ws, bs3, name):
    T = a.shape[0]
    tr = _pick(T, (ROW_TILE,))
    W = GM_WIDTH

    def body(a_ref, lg_ref, lb_ref, ws_ref, bs_ref, o_ref, vn_scr):
        av = a_ref[:, W:]
        v = av * _cdf(av)
        vc = v - jnp.mean(v, axis=-1, keepdims=True)
        rstd = lax.rsqrt(jnp.mean(vc * vc, axis=-1, keepdims=True) + EPS)
        vn_scr[...] = (vc * rstd * lg_ref[...] + lb_ref[...]).astype(BF16)
        mask = _spatial_mask()
        for h in range(GM_HEADS):
            w = jnp.where(mask, ws_ref[h], 0.0).astype(BF16)
            cs = slice(h * GM_HEAD_DIM, (h + 1) * GM_HEAD_DIM)
            for blk in range(tr // GM_BLOCK):
                rs = slice(blk * GM_BLOCK, (blk + 1) * GM_BLOCK)
                s = _dot(w, vn_scr[rs, cs], NN) + bs_ref[h]
                au = a_ref[rs, cs]
                o_ref[rs, cs] = (au * _cdf(au) * s).astype(BF16)

    return pl.pallas_call(
        body, name=name, out_shape=jax.ShapeDtypeStruct((T, W), BF16), grid=(T // tr,),
        in_specs=_gm_specs(tr), out_specs=_row_spec(tr, W),
        scratch_shapes=[pltpu.VMEM((tr, W), BF16)], compiler_params=_params(("parallel",)),
    )(a, ln_g, ln_b, ws, bs3)


def _gm_mid_bwd(a, dgated, ln_g, ln_b, ws, bs3, name):
    T = a.shape[0]
    tr = _pick(T, (ROW_TILE,))
    W = GM_WIDTH
    nsteps = T // tr

    def body(a_ref, dg_ref, lg_ref, lb_ref, ws_ref, bs_ref, da_ref, dws_ref, dbs_ref, dlg_ref, dlb_ref,
             vn_scr, vhat_scr, dvn_scr, dsum_scr):
        i = pl.program_id(0)

        @pl.when(i == 0)
        def _():
            dws_ref[...] = jnp.zeros_like(dws_ref)
            dbs_ref[...] = jnp.zeros_like(dbs_ref)
            dlg_ref[...] = jnp.zeros_like(dlg_ref)
            dlb_ref[...] = jnp.zeros_like(dlb_ref)
            dsum_scr[...] = jnp.zeros_like(dsum_scr)

        av = a_ref[:, W:]
        cdf_v = _cdf(av)
        v = av * cdf_v
        vc = v - jnp.mean(v, axis=-1, keepdims=True)
        rstd = lax.rsqrt(jnp.mean(vc * vc, axis=-1, keepdims=True) + EPS)
        vhat_scr[...] = vc * rstd
        vn_scr[...] = (vhat_scr[...] * lg_ref[...] + lb_ref[...]).astype(BF16)
        mask = _spatial_mask()
        for h in range(GM_HEADS):
            w = jnp.where(mask, ws_ref[h], 0.0).astype(BF16)
            cs = slice(h * GM_HEAD_DIM, (h + 1) * GM_HEAD_DIM)
            for blk in range(tr // GM_BLOCK):
                rs = slice(blk * GM_BLOCK, (blk + 1) * GM_BLOCK)
                vnb = vn_scr[rs, cs]
                s = _dot(w, vnb, NN) + bs_ref[h]
                au = a_ref[rs, cs]
                cdf_u = _cdf(au)
                dg = dg_ref[rs, cs]
                ds = dg * (au * cdf_u)
                da_ref[rs, cs] = (dg * s * (cdf_u + au * _pdf(au))).astype(BF16)
                dsb = ds.astype(BF16)
                dvn_scr[rs, cs] = _dot(w, dsb, TN)
                dws_ref[h] += _dot(dsb, vnb, NT)
                dsum_scr[:, cs] += ds
        dvn = dvn_scr[...]
        vhat = vhat_scr[...]
        dlg_ref[...] += jnp.sum(dvn * vhat, axis=0, keepdims=True)
        dlb_ref[...] += jnp.sum(dvn, axis=0, keepdims=True)
        dvh = dvn * lg_ref[...]
        dv = rstd * (dvh - jnp.mean(dvh, axis=-1, keepdims=True)
                     - vhat * jnp.mean(dvh * vhat, axis=-1, keepdims=True))
        da_ref[:, W:] = (dv * (cdf_v + av * _pdf(av))).astype(BF16)

        @pl.when(i == nsteps - 1)
        def _():
            for h in range(GM_HEADS):
                dws_ref[h] = jnp.where(mask, dws_ref[h], 0.0)
            col_head = lax.broadcasted_iota(jnp.int32, (W, GM_BLOCK), 0) // GM_HEAD_DIM
            sel = (col_head == lax.broadcasted_iota(jnp.int32, (W, GM_BLOCK), 1)).astype(F32)
            dbs_ref[...] = _dot(dsum_scr[...], sel, NN, HI)

    vshape = jax.ShapeDtypeStruct((1, W), F32)
    return pl.pallas_call(
        body, name=name, grid=(nsteps,),
        out_shape=(jax.ShapeDtypeStruct((T, 2 * W), BF16), jax.ShapeDtypeStruct((GM_HEADS, GM_BLOCK, GM_BLOCK), F32),
                   jax.ShapeDtypeStruct((GM_BLOCK, GM_BLOCK), F32), vshape, vshape),
        in_specs=[_gm_specs(tr)[0], _row_spec(tr, W)] + _gm_specs(tr)[1:],
        out_specs=(_row_spec(tr, 2 * W), pl.BlockSpec((GM_HEADS, GM_BLOCK, GM_BLOCK), lambda i: (0, 0, 0)),
                   pl.BlockSpec((GM_BLOCK, GM_BLOCK), lambda i: (0, 0)), _vec_spec(W), _vec_spec(W)),
        scratch_shapes=[pltpu.VMEM((tr, W), BF16), pltpu.VMEM((tr, W), F32), pltpu.VMEM((tr, W), F32),
                        pltpu.VMEM((GM_BLOCK, W), F32)],
        compiler_params=_params(("arbitrary",)),
    )(a, dgated, ln_g, ln_b, ws, bs3)


SUB = 16
EXP_CLAMP = 80.0


def _tri(lower):
    r = lax.broadcasted_iota(jnp.int32, (CHUNK, CHUNK), 0)
    c = lax.broadcasted_iota(jnp.int32, (CHUNK, CHUNK), 1)
    return (r >= c) if lower else (c >= r)


def _score_masks():
    i = lax.broadcasted_iota(jnp.int32, (CHUNK, CHUNK), 0)
    j = lax.broadcasted_iota(jnp.int32, (CHUNK, CHUNK), 1)
    bi, bj = i // SUB, j // SUB
    diag = (bi == bj) & (i >= j)
    pair = (bi % 2 == 1) & (bj == bi - 1)
    half = (i >= CHUNK // 2) & (j < CHUNK // 2)
    return diag, pair, half


def _dot01(m, x):
    x1 = x.astype(BF16)
    rest = x - x1.astype(F32)
    x2 = rest.astype(BF16)
    x3 = (rest - x2.astype(F32)).astype(BF16)
    return _dot(m, x1, NN) + (_dot(m, x2, NN) + _dot(m, x3, NN))


def _block_rows(b, offset):
    parts = []
    for blk in range(0, CHUNK, SUB):
        r = blk + offset
        parts.append(jnp.zeros((SUB, b.shape[1]), F32) if r < 0 else jnp.broadcast_to(b[r:r + 1], (SUB, b.shape[1])))
    return jnp.concatenate(parts, axis=0)


def _hg_gates(p_ref, lb_ref, lower):
    Dm = D_MODEL
    heads = []
    for h in range(HG_HEADS):
        c0 = h * HG_DIM
        qr = p_ref[:, c0:c0 + HG_DIM]
        fz = p_ref[:, Dm + c0:Dm + c0 + HG_DIM]
        lbh = lb_ref[:, c0:c0 + HG_DIM]
        sg = _sig(fz)
        f = lbh + (1.0 - lbh) * sg
        sq = _sig(qr)
        heads.append(dict(qr=qr, v=p_ref[:, 2 * Dm + c0:2 * Dm + c0 + HG_DIM],
                          gt=p_ref[:, 3 * Dm + c0:3 * Dm + c0 + HG_DIM], lbh=lbh, sg=sg, f=f, gl=jnp.log(f),
                          kk=1.0 - f, sq=sq, q=qr * sq))
    for g in heads:
        g["b"] = _dot01(lower, g.pop("gl"))
    for g in heads:
        g.update(_hg_scalings(g["q"], g["kk"], g.pop("b")))
    return heads


def _hg_scalings(q, kk, b):
    r_mid = _block_rows(b, SUB // 2 - 1)
    r_prev = _block_rows(b, -1)
    r_end = _block_rows(b, SUB - 1)
    r_half = jnp.broadcast_to(b[CHUNK // 2 - 1:CHUNK // 2], b.shape)
    bc = b[CHUNK - 1:CHUNK]
    eqs = (jnp.exp(jnp.clip(b - r_mid, -EXP_CLAMP, EXP_CLAMP)), jnp.exp(jnp.minimum(b - r_prev, 0.0)),
           jnp.exp(jnp.minimum(b - r_half, 0.0)))
    eks = (jnp.exp(jnp.clip(r_mid - b, -EXP_CLAMP, EXP_CLAMP)), jnp.exp(jnp.minimum(r_end - b, 0.0)),
           jnp.exp(jnp.minimum(r_half - b, 0.0)))
    eb = jnp.exp(b)
    ec = jnp.exp(bc - b)
    return dict(eqs=eqs, eks=eks, eb=eb, ec=ec, e_end=jnp.exp(bc), qs=[q * e for e in eqs],
                ks=[kk * e for e in eks], qe=q * eb, ke=kk * ec)


def _scores(g, masks):
    a = None
    for qs, ks, m in zip(g["qs"], g["ks"], masks):
        part = jnp.where(m, _dot(qs.astype(BF16), ks.astype(BF16), NT), 0.0)
        a = part if a is None else a + part
    return a


def _hg_scan_fwd(p, lb, gn, name):
    T = p.shape[0]
    nc = T // CHUNK
    Dm = D_MODEL

    def body(p_ref, lb_ref, gn_ref, o_ref, og_ref, so_ref, st_ref):
        @pl.when(pl.program_id(0) == 0)
        def _():
            st_ref[...] = jnp.zeros_like(st_ref)

        masks = _score_masks()
        heads = _hg_gates(p_ref, lb_ref, _tri(True).astype(BF16))
        states = [st_ref[h] for h in range(HG_HEADS)]
        scores = [_scores(g, masks) for g in heads]
        outs = [_dot(a.astype(BF16), g["v"].astype(BF16), NN) + _dot(g["qe"], st, NT, X3)
                for g, a, st in zip(heads, scores, states)]
        new_states = [st * g["e_end"] + _dot(g["v"], g["ke"], TN, X3) for g, st in zip(heads, states)]
        for h, (g, o, st, st2) in enumerate(zip(heads, outs, states, new_states)):
            cs = slice(h * HG_DIM, (h + 1) * HG_DIM)
            so_ref[0, h] = st
            st_ref[h] = st2
            o_ref[:, cs] = o
            r = lax.rsqrt(jnp.mean(o * o, axis=-1, keepdims=True) + EPS)
            gt = g["gt"]
            og_ref[:, cs] = (((o * r) * gn_ref[:, cs]).astype(F32) * (gt * _sig(gt))).astype(BF16)

    return pl.pallas_call(
        body, name=name, grid=(nc,),
        out_shape=(jax.ShapeDtypeStruct((T, Dm), F32), jax.ShapeDtypeStruct((T, Dm), BF16),
                   jax.ShapeDtypeStruct((nc, HG_HEADS, HG_DIM, HG_DIM), F32)),
        in_specs=[_row_spec(CHUNK, 4 * Dm), _vec_spec(Dm), _vec_spec(Dm)],
        out_specs=(_row_spec(CHUNK, Dm), _row_spec(CHUNK, Dm),
                   pl.BlockSpec((1, HG_HEADS, HG_DIM, HG_DIM), lambda i: (i, 0, 0, 0))),
        scratch_shapes=[pltpu.VMEM((HG_HEADS, HG_DIM, HG_DIM), F32)],
        compiler_params=_params(("arbitrary",)),
    )(p, lb, gn)


def _hg_scan_bwd(p, lb, gn, o, dog, states, name):
    T = p.shape[0]
    nc = T // CHUNK
    Dm = D_MODEL

    def rev(i):
        return nc - 1 - i

    def body(p_ref, lb_ref, gn_ref, o_ref, dog_ref, st_in_ref, dp_ref, dlb_ref, dgn_ref, dst_ref, carry_ref):
        @pl.when(pl.program_id(0) == 0)
        def _():
            dst_ref[...] = jnp.zeros_like(dst_ref)
            carry_ref[...] = jnp.zeros_like(carry_ref)
            dlb_ref[...] = jnp.zeros_like(dlb_ref)
            dgn_ref[...] = jnp.zeros_like(dgn_ref)

        upper = _tri(False).astype(BF16)
        masks = _score_masks()
        heads = _hg_gates(p_ref, lb_ref, _tri(True).astype(BF16))
        for h, g in enumerate(heads):
            cs = slice(h * HG_DIM, (h + 1) * HG_DIM)
            oh = o_ref[:, cs]
            r = lax.rsqrt(jnp.mean(oh * oh, axis=-1, keepdims=True) + EPS)
            on = oh * r
            gt = g["gt"]
            sgt = _sig(gt)
            sil = gt * sgt
            dogh = dog_ref[:, cs]
            gnh = gn_ref[:, cs]
            don = dogh * gnh * sil
            g["dgn"] = jnp.sum(dogh * on * sil, axis=0, keepdims=True)
            g["dgate"] = dogh * on * gnh * (sgt * (1.0 + gt * (1.0 - sgt)))
            g["do"] = r * (don - on * jnp.mean(don * on, axis=-1, keepdims=True))
            g["dst"] = dst_ref[h]
            g["st"] = st_in_ref[0, h]
            g["carry"] = carry_ref[h]
        for g in heads:
            g["a"] = _scores(g, masks)
            g["dob"] = g["do"].astype(BF16)
            g["da"] = _dot(g["dob"], g["v"].astype(BF16), NT)
        for g in heads:
            g["dv"] = _dot(g["a"].astype(BF16), g["dob"], TN) + _dot(g["ke"].astype(BF16), g["dst"].astype(BF16), NT)
            g["dq"] = _dot(g["do"], g["st"], NN, X3) * g["eb"]
            g["dk"] = _dot(g["v"], g["dst"], NN, X3) * g["ec"]
            g["dst2"] = g["dst"] * g["e_end"] + _dot(g["do"], g["qe"], TN, X3)
        for lvl in range(3):
            for g in heads:
                dam = jnp.where(masks[lvl], g["da"], 0.0)
                g["dq"] = g["dq"] + _dot(dam, g["ks"][lvl], NN, X3) * g["eqs"][lvl]
                g["dk"] = g["dk"] + _dot(dam, g["qs"][lvl], TN, X3) * g["eks"][lvl]
        for g in heads:
            g["dgd"] = g["q"] * g["dq"] - g["kk"] * g["dk"]
            g["dgl"] = _dot01(upper, g["dgd"]) + g["carry"]
        for h, g in enumerate(heads):
            c0 = h * HG_DIM
            cs = slice(c0, c0 + HG_DIM)
            df = g["dgl"] / g["f"] - g["dk"]
            sg, sq, qr = g["sg"], g["sq"], g["qr"]
            dst_ref[h] = g["dst2"]
            carry_ref[h] = g["carry"] + jnp.sum(g["dgd"], axis=0, keepdims=True)
            dgn_ref[:, cs] += g["dgn"]
            dlb_ref[:, cs] += jnp.sum(df * (1.0 - sg), axis=0, keepdims=True)
            dp_ref[:, c0:c0 + HG_DIM] = (g["dq"] * (sq * (1.0 + qr * (1.0 - sq)))).astype(BF16)
            dp_ref[:, Dm + c0:Dm + c0 + HG_DIM] = (df * (1.0 - g["lbh"]) * sg * (1.0 - sg)).astype(BF16)
            dp_ref[:, 2 * Dm + c0:2 * Dm + c0 + HG_DIM] = g["dv"].astype(BF16)
            dp_ref[:, 3 * Dm + c0:3 * Dm + c0 + HG_DIM] = g["dgate"].astype(BF16)

    vshape = jax.ShapeDtypeStruct((1, Dm), F32)
    rrow = lambda w: pl.BlockSpec((CHUNK, w), lambda i: (rev(i), 0))
    return pl.pallas_call(
        body, name=name, grid=(nc,),
        out_shape=(jax.ShapeDtypeStruct((T, 4 * Dm), BF16), vshape, vshape),
        in_specs=[rrow(4 * Dm), _vec_spec(Dm), _vec_spec(Dm), rrow(Dm), rrow(Dm),
                  pl.BlockSpec((1, HG_HEADS, HG_DIM, HG_DIM), lambda i: (rev(i), 0, 0, 0))],
        out_specs=(rrow(4 * Dm), _vec_spec(Dm), _vec_spec(Dm)),
        scratch_shapes=[pltpu.VMEM((HG_HEADS, HG_DIM, HG_DIM), F32), pltpu.VMEM((HG_HEADS, 1, HG_DIM), F32)],
        compiler_params=_params(("arbitrary",)),
    )(p, lb, gn, o, dog, states)


def _lb_fwd(hg_lb, name):
    def body(a_ref, o_ref):
        a0, a1 = a_ref[0:1], a_ref[1:2]
        m = jnp.maximum(a0, a1)
        e0, e1 = jnp.exp(a0 - m), jnp.exp(a1 - m)
        p0, p1 = e0 / (e0 + e1), e1 / (e0 + e1)
        o_ref[0:1] = p0 - p0
        o_ref[1:2] = (p0 + p1) - p0

    return pl.pallas_call(body, name=name, out_shape=jax.ShapeDtypeStruct(hg_lb.shape, F32))(hg_lb)


def _lb_bwd(hg_lb, dlb_all, name):
    def body(a_ref, d_ref, o_ref):
        a0, a1 = a_ref[0:1], a_ref[1:2]
        m = jnp.maximum(a0, a1)
        e0, e1 = jnp.exp(a0 - m), jnp.exp(a1 - m)
        p0, p1 = e0 / (e0 + e1), e1 / (e0 + e1)
        d1 = d_ref[1:2]
        o_ref[0:1] = -p0 * p1 * d1
        o_ref[1:2] = p1 * (1.0 - p1) * d1

    return pl.pallas_call(body, name=name, out_shape=jax.ShapeDtypeStruct(hg_lb.shape, F32))(hg_lb, dlb_all)


CONV_COLS_FWD = 256
CONV_COLS_BWD = 128
CONV_ROWS_BWD = 128
CONV_ROWS_FWD = 512


def _conv_fwd(a, w, b, name):
    T = a.shape[0]
    Fh = FFN_HIDDEN
    tr = _pick(T, (CONV_ROWS_FWD, ROW_TILE))
    cw = CONV_COLS_FWD
    hb = tr // 8

    def body(a_ref, ap_ref, w_ref, b_ref, m_ref):
        m0 = (pl.program_id(0) > 0).astype(F32)

        def conv(cc):
            x = jnp.concatenate([ap_ref[:, pl.ds(cc, cw)] * m0, a_ref[:, pl.ds(cc, cw)]], axis=0)
            wv = w_ref[:, pl.ds(cc, cw)]
            y = b_ref[:, pl.ds(cc, cw)] + wv[2:3] * x + wv[1:2] * pltpu.roll(x, 1, axis=0) \
                + wv[0:1] * pltpu.roll(x, 2, axis=0)
            return y[8:]

        def step(c, carry):
            c0 = pl.multiple_of(c * cw, cw)
            c1 = pl.multiple_of(Fh + c * cw, cw)
            yg, yv = conv(c0), conv(c1)
            m_ref[:, pl.ds(c0, cw)] = (yg * _cdf(yg) * yv).astype(BF16)
            return carry

        lax.fori_loop(0, Fh // cw, step, 0)

    return pl.pallas_call(
        body, name=name, out_shape=jax.ShapeDtypeStruct((T, Fh), BF16), grid=(T // tr,),
        in_specs=[_row_spec(tr, 2 * Fh), pl.BlockSpec((8, 2 * Fh), lambda i: (jnp.maximum(i * hb - 1, 0), 0)),
                  _vec_spec(2 * Fh, 3), _vec_spec(2 * Fh)],
        out_specs=_row_spec(tr, Fh), compiler_params=_params(("parallel",)),
    )(a, a, w, b)


def _conv_bwd(a, dm, w, b, name):
    T = a.shape[0]
    Fh = FFN_HIDDEN
    tr = _pick(T, (CONV_ROWS_BWD, ROW_TILE))
    cw = CONV_COLS_BWD
    hb = tr // 8
    nsteps = T // tr
    n = tr + 8

    def body(a_ref, ap_ref, an_ref, dm_ref, dmn_ref, w_ref, b_ref, da_ref, dw_ref, db_ref):
        i = pl.program_id(0)
        m0 = (i > 0).astype(F32)
        m1 = (i < nsteps - 1).astype(F32)

        @pl.when(i == 0)
        def _():
            dw_ref[...] = jnp.zeros_like(dw_ref)
            db_ref[...] = jnp.zeros_like(db_ref)

        def prep(cc):
            x = jnp.concatenate([ap_ref[:, pl.ds(cc, cw)] * m0, a_ref[:, pl.ds(cc, cw)],
                                 an_ref[:, pl.ds(cc, cw)] * m1], axis=0)
            wv = w_ref[:, pl.ds(cc, cw)]
            s1 = pltpu.roll(x, 1, axis=0)
            s2 = pltpu.roll(x, 2, axis=0)
            y = b_ref[:, pl.ds(cc, cw)] + wv[2:3] * x + wv[1:2] * s1 + wv[0:1] * s2
            return wv, x[8:], s1[8:], s2[8:], y[8:]

        def back(cc, dy, wv, x0, s1, s2):
            da = wv[2:3] * dy + wv[1:2] * pltpu.roll(dy, n - 1, axis=0) + wv[0:1] * pltpu.roll(dy, n - 2, axis=0)
            da_ref[:, pl.ds(cc, cw)] = da[:tr].astype(BF16)
            d = dy[:tr]
            db_ref[:, pl.ds(cc, cw)] += jnp.sum(d, axis=0, keepdims=True)
            dw_ref[2:3, pl.ds(cc, cw)] += jnp.sum(d * x0[:tr], axis=0, keepdims=True)
            dw_ref[1:2, pl.ds(cc, cw)] += jnp.sum(d * s1[:tr], axis=0, keepdims=True)
            dw_ref[0:1, pl.ds(cc, cw)] += jnp.sum(d * s2[:tr], axis=0, keepdims=True)

        def step(c, carry):
            c0 = pl.multiple_of(c * cw, cw)
            c1 = pl.multiple_of(Fh + c * cw, cw)
            dmx = jnp.concatenate([dm_ref[:, pl.ds(c0, cw)], dmn_ref[:, pl.ds(c0, cw)] * m1], axis=0)
            wg, xg, s1g, s2g, yg = prep(c0)
            wv, xv, s1v, s2v, yv = prep(c1)
            cg = _cdf(yg)
            back(c0, dmx * yv * (cg + yg * _pdf(yg)), wg, xg, s1g, s2g)
            back(c1, dmx * (yg * cg), wv, xv, s1v, s2v)
            return carry

        lax.fori_loop(0, Fh // cw, step, 0)

    prev = lambda wd: pl.BlockSpec((8, wd), lambda i: (jnp.maximum(i * hb - 1, 0), 0))
    nxt = lambda wd: pl.BlockSpec((8, wd), lambda i: (jnp.minimum((i + 1) * hb, T // 8 - 1), 0))
    return pl.pallas_call(
        body, name=name, grid=(nsteps,),
        out_shape=(jax.ShapeDtypeStruct((T, 2 * Fh), BF16), jax.ShapeDtypeStruct((3, 2 * Fh), F32),
                   jax.ShapeDtypeStruct((1, 2 * Fh), F32)),
        in_specs=[_row_spec(tr, 2 * Fh), prev(2 * Fh), nxt(2 * Fh), _row_spec(tr, Fh), nxt(Fh),
                  _vec_spec(2 * Fh, 3), _vec_spec(2 * Fh)],
        out_specs=(_row_spec(tr, 2 * Fh), _vec_spec(2 * Fh, 3), _vec_spec(2 * Fh)),
        compiler_params=_params(("arbitrary",)),
    )(a, a, a, dm, dm, w, b)


def _ada_fwd(c_all, ada_w, ada_b, name):
    L, Dm, cols = ada_w.shape
    tn = _pick(cols, (512, 256, 128))

    def body(c_ref, w_ref, b_ref, o_ref):
        cv = c_ref[...]
        cond = (cv * _sig(cv)).astype(BF16)
        o_ref[...] = _dot(cond, w_ref[...].astype(BF16), NN) + b_ref[...]

    return pl.pallas_call(
        body, name=name, out_shape=jax.ShapeDtypeStruct((L, N_DEV, cols), F32), grid=(L, cols // tn),
        in_specs=[pl.BlockSpec((N_DEV, Dm), lambda l, j: (0, 0)), pl.BlockSpec((None, Dm, tn), lambda l, j: (l, 0, j)),
                  pl.BlockSpec((None, 1, tn), lambda l, j: (l, 0, j))],
        out_specs=pl.BlockSpec((None, N_DEV, tn), lambda l, j: (l, 0, j)),
        compiler_params=_params(("parallel", "parallel")),
    )(c_all, ada_w, ada_b.reshape(L, 1, cols))


def _ada_bwd(c_all, dmod, name):
    L, _, cols = dmod.shape
    Dm = c_all.shape[1]
    tn = _pick(cols, (512, 256, 128))

    def body(c_ref, d_ref, o_ref):
        cv = c_ref[...]
        o_ref[...] = _dot(cv * _sig(cv), d_ref[...], TN, HI)

    return pl.pallas_call(
        body, name=name, out_shape=jax.ShapeDtypeStruct((L, Dm, cols), F32), grid=(L, cols // tn),
        in_specs=[pl.BlockSpec((N_DEV, Dm), lambda l, j: (0, 0)), pl.BlockSpec((None, N_DEV, tn), lambda l, j: (l, 0, j))],
        out_specs=pl.BlockSpec((None, Dm, tn), lambda l, j: (l, 0, j)),
        compiler_params=_params(("parallel", "parallel")),
    )(c_all, dmod)


def _add_own_halves(g4s, rbs, core, name):
    n = len(g4s)

    def body(core_ref, *refs):
        for g_ref, r_ref, o_ref in zip(refs[:n], refs[n:2 * n], refs[2 * n:]):
            o_ref[...] = (g_ref[...].astype(F32) + r_ref[...].astype(F32)).astype(GRAD_WIRE)

    shapes = [g.shape for g in g4s]
    return pl.pallas_call(
        body, name=name, out_shape=tuple(jax.ShapeDtypeStruct((S, rh, cw), GRAD_WIRE) for S, _, rh, cw in shapes),
        grid_spec=pltpu.PrefetchScalarGridSpec(
            num_scalar_prefetch=1, grid=(N_CHIPS,),
            in_specs=[pl.BlockSpec((None, None, rh, cw), lambda s, core_ref: (s, core_ref[0], 0, 0))
                      for _, _, rh, cw in shapes]
            + [pl.BlockSpec((None, rh, cw), lambda s, core_ref: (s, 0, 0)) for _, _, rh, cw in shapes],
            out_specs=tuple(pl.BlockSpec((None, rh, cw), lambda s, core_ref: (s, 0, 0)) for _, _, rh, cw in shapes)),
        compiler_params=_params(("parallel",)),
    )(core, *g4s, *rbs)


def _sum_chips(lands, sums, chip, name):
    L = len(lands)
    _, rh, cw = lands[0].shape
    tr = _pick(rh, (256, 128, 176, 64))

    def body(chip_ref, *refs):
        ld, cs, o_ref = refs[:L], refs[L:2 * L], refs[2 * L]
        me = chip_ref[0]
        for k in range(L):
            @pl.when(pl.program_id(0) == k)
            def _(k=k):
                own = cs[k][...].astype(F32)
                got = [ld[k][j].astype(F32) for j in range(3)]
                acc = None
                for t in range(N_CHIPS):
                    d = jnp.bitwise_xor(jnp.int32(t), me)
                    term = jnp.where(d == 0, own, jnp.where(d == 2, got[0], jnp.where(d == 1, got[1], got[2])))
                    acc = term if acc is None else acc + term
                o_ref[...] = acc

    frozen = lambda l, i, k: jnp.where(l == k, i, 0)
    in_specs = [pl.BlockSpec((3, tr, cw), lambda l, i, chip_ref, k=k: (0, frozen(l, i, k), 0)) for k in range(L)]
    in_specs += [pl.BlockSpec((None, tr, cw), lambda l, i, chip_ref, k=k: (chip_ref[0], frozen(l, i, k), 0))
                 for k in range(L)]
    return pl.pallas_call(
        body, name=name, out_shape=jax.ShapeDtypeStruct((L, rh, cw), F32),
        grid_spec=pltpu.PrefetchScalarGridSpec(
            num_scalar_prefetch=1, grid=(L, rh // tr), in_specs=in_specs,
            out_specs=pl.BlockSpec((None, tr, cw), lambda l, i, chip_ref: (l, i, 0))),
        compiler_params=_params(("arbitrary", "arbitrary")),
    )(chip, *lands, *sums)


def _sum_devices(gathered, name):
    n, R, _ = gathered.shape
    tr = _pick(R, (512, 448, 384, 256, 192, 128, 64, 32, 16, 8))

    def body(g_ref, o_ref):
        acc = g_ref[0]
        for d in range(1, n):
            acc = acc + g_ref[d]
        o_ref[...] = acc

    return pl.pallas_call(
        body, name=name, out_shape=jax.ShapeDtypeStruct((R, LANES), F32), grid=(R // tr,),
        in_specs=[pl.BlockSpec((n, tr, LANES), lambda i: (0, i, 0))], out_specs=pl.BlockSpec((tr, LANES), lambda i: (i, 0)),
        compiler_params=_params(("parallel",)),
    )(gathered)


def _adamw(w, g, m, v, name):
    R, C = w.shape
    tr = _pick(R, (256, 128, 64, 32, 16, 8))
    c1 = 1.0 / (1.0 - ADAM_B1 ** ADAM_STEP)
    c2 = 1.0 / (1.0 - ADAM_B2 ** ADAM_STEP)

    def body(w_ref, g_ref, m_ref, v_ref, d_ref, mo_ref, vo_ref):
        gv = g_ref[...]
        m2 = ADAM_B1 * m_ref[...] + (1.0 - ADAM_B1) * gv
        v2 = ADAM_B2 * v_ref[...] + (1.0 - ADAM_B2) * (gv * gv)
        mo_ref[...] = m2
        vo_ref[...] = v2
        d_ref[...] = -ADAM_LR * ((m2 * c1) / (jnp.sqrt(v2 * c2) + ADAM_EPS) + ADAM_WD * w_ref[...])

    spec = pl.BlockSpec((tr, C), lambda i: (i, 0))
    shp = jax.ShapeDtypeStruct((R, C), F32)
    return pl.pallas_call(body, name=name, out_shape=(shp, shp, shp), grid=(R // tr,), in_specs=[spec] * 4,
                          out_specs=(spec, spec, spec), compiler_params=_params(("parallel",)))(w, g, m, v)


def _adamw_halves(w, own, recv, m, v, core, name):
    L, rh, cw = own.shape
    tr = _pick(rh, (256, 128, 176, 64))
    c1 = 1.0 / (1.0 - ADAM_B1 ** ADAM_STEP)
    c2 = 1.0 / (1.0 - ADAM_B2 ** ADAM_STEP)

    def body(core_ref, w_ref, own_ref, recv_ref, m_ref, v_ref, g_ref, d_ref, mo_ref, vo_ref):
        gv = jnp.where(pl.program_id(1) == core_ref[0], own_ref[...], recv_ref[...])
        g_ref[...] = gv
        m2 = ADAM_B1 * m_ref[...] + (1.0 - ADAM_B1) * gv
        v2 = ADAM_B2 * v_ref[...] + (1.0 - ADAM_B2) * (gv * gv)
        mo_ref[...] = m2
        vo_ref[...] = v2
        d_ref[...] = -ADAM_LR * ((m2 * c1) / (jnp.sqrt(v2 * c2) + ADAM_EPS) + ADAM_WD * w_ref[...])

    full = pl.BlockSpec((None, None, tr, cw), lambda l, hf, i, core_ref: (l, hf, i, 0))
    mine = pl.BlockSpec((None, tr, cw), lambda l, hf, i, core_ref: (l, jnp.where(hf == core_ref[0], i, 0), 0))
    other = pl.BlockSpec((None, tr, cw), lambda l, hf, i, core_ref: (l, jnp.where(hf == core_ref[0], 0, i), 0))
    shp = jax.ShapeDtypeStruct((L, 2, rh, cw), F32)
    view = lambda a: a.reshape(L, 2, rh, cw)
    outs = pl.pallas_call(
        body, name=name, out_shape=(shp, shp, shp, shp),
        grid_spec=pltpu.PrefetchScalarGridSpec(
            num_scalar_prefetch=1, grid=(L, 2, rh // tr), in_specs=[full, mine, other, full, full],
            out_specs=(full, full, full, full)),
        compiler_params=_params(("arbitrary", "arbitrary", "arbitrary")),
    )(core, view(w), own, recv, view(m), view(v))
    return tuple(o.reshape(L, 2 * rh, cw) for o in outs)


ANY = pl.BlockSpec(memory_space=pl.ANY)


def _position():
    x, y, c = lax.axis_index("x"), lax.axis_index("y"), lax.axis_index("c")
    return x, y, c


def _allgather(ins, out_shapes, src_fns, dst_fns, name):
    n = len(ins)

    def body(*refs):
        in_refs, out_refs = refs[:n], refs[n:2 * n]
        send_sems, recv_sems, local_sems = refs[2 * n:]
        x, y, c = _position()
        me, sibling = (x, y, c), (x, y, 1 - c)
        chips = [(1 - x, y), (x, 1 - y), (1 - x, 1 - y)]

        def copy(k, j, block, to, own=False):
            dst = dst_fns[k](out_refs[k], *block)
            return pltpu.make_async_remote_copy(
                src_ref=src_fns[k](in_refs[k], c) if own else dst, dst_ref=dst,
                send_sem=send_sems.at[k, j], recv_sem=recv_sems.at[k, j], device_id=to, device_id_type=MESH)

        mine = [pltpu.make_async_copy(src_fns[k](in_refs[k], c), dst_fns[k](out_refs[k], *me), local_sems.at[k])
                for k in range(n)]
        for cp in mine:
            cp.start()
        first = []
        for k in range(n):
            first.append(copy(k, 0, me, sibling, own=True))
            first += [copy(k, 1 + j, me, (*chip, c), own=True) for j, chip in enumerate(chips)]
        for cp in first:
            cp.start()
        passed = []
        for j, chip in enumerate(chips):
            for k in range(n):
                copy(k, 1 + j, (*chip, c), me).wait_recv()
                fwd = copy(k, 4 + j, (*chip, c), sibling)
                fwd.start()
                passed.append(fwd)
        for k in range(n):
            copy(k, 0, sibling, me).wait_recv()
        for j, chip in enumerate(chips):
            for k in range(n):
                copy(k, 4 + j, (*chip, 1 - c), me).wait_recv()
        for cp in first + passed:
            cp.wait_send()
        for cp in mine:
            cp.wait()

    spec = pl.BlockSpec(memory_space=pltpu.VMEM)
    return pl.pallas_call(
        body, name=name, out_shape=tuple(out_shapes), in_specs=[spec] * n, out_specs=tuple([spec] * n),
        scratch_shapes=[pltpu.SemaphoreType.DMA((n, 7)), pltpu.SemaphoreType.DMA((n, 7)),
                        pltpu.SemaphoreType.DMA((n,))],
        compiler_params=pltpu.CompilerParams(vmem_limit_bytes=VMEM_LIMIT_BYTES),
    )(*ins)


def _allgather_small(payload, name):
    R = payload.shape[0]
    (out,) = _allgather(
        [payload], [jax.ShapeDtypeStruct((N_DEV, R, LANES), F32)],
        [lambda ref, c: ref], [lambda ref, px, py, pc: ref.at[4 * px + 2 * py + pc]], name)
    return out


def _swap_sibling(ins, name, other_half=False):
    n = len(ins)

    def body(*refs):
        in_refs, out_refs = refs[:n], refs[n:2 * n]
        send_sems, recv_sems = refs[2 * n:]
        x, y, c = _position()
        copies = [pltpu.make_async_remote_copy(
            src_ref=in_refs[k].at[:, 1 - c] if other_half else in_refs[k], dst_ref=out_refs[k],
            send_sem=send_sems.at[k], recv_sem=recv_sems.at[k],
            device_id=(x, y, 1 - c), device_id_type=MESH) for k in range(n)]
        for cp in copies:
            cp.start()
        for cp in copies:
            cp.wait_recv()
        for cp in copies:
            cp.wait_send()

    shape = lambda a: (a.shape[0],) + a.shape[2:] if other_half else a.shape
    return pl.pallas_call(
        body, name=name, out_shape=tuple(jax.ShapeDtypeStruct(shape(a), a.dtype) for a in ins),
        in_specs=[ANY] * n, out_specs=tuple([ANY] * n),
        scratch_shapes=[pltpu.SemaphoreType.DMA((n,)), pltpu.SemaphoreType.DMA((n,))],
        compiler_params=pltpu.CompilerParams(vmem_limit_bytes=VMEM_LIMIT_BYTES),
    )(*ins)


HBM_SPEC = pl.BlockSpec(memory_space=pltpu.HBM)
SEM_SPEC = pl.BlockSpec(memory_space=pltpu.SEMAPHORE)
SPLIT_PARAMS = pltpu.CompilerParams(has_side_effects=pltpu.SideEffectType.DATAFLOW_SIDE_EFFECTING)
TOKEN = jax.ShapeDtypeStruct((8, LANES), F32)


def _hbm(a):
    return pltpu.with_memory_space_constraint(a, pltpu.HBM)


def _weight_window(ref, col, r, cw, px, py, pc):
    rh = r // 2
    if col:
        return ref.at[pl.ds(pc * rh, rh), pl.ds((2 * px + py) * cw, cw)]
    return ref.at[pl.ds((2 * px + py) * r + pc * rh, rh), :]


def _peers(x, y, c):
    return [(x, y, 1 - c), (1 - x, y, c), (x, 1 - y, c), (1 - x, 1 - y, c)]


def _place_own(shards, cols, pos, name):
    n = len(shards)

    def body(pos_ref, *refs):
        for x_ref, o_ref in zip(refs[:n], refs[n:]):
            o_ref[...] = x_ref[...]

    in_specs, out_specs, out_shape = [], [], []
    for sh, col in zip(shards, cols):
        r, cw = sh.shape
        rh = r // 2
        in_specs.append(pl.BlockSpec((rh, cw), lambda i, pos_ref: (pos_ref[1], 0)))
        if col:
            out_specs.append(pl.BlockSpec((rh, cw), lambda i, pos_ref: (pos_ref[1], pos_ref[0])))
            out_shape.append(jax.ShapeDtypeStruct((r, N_CHIPS * cw), sh.dtype))
        else:
            out_specs.append(pl.BlockSpec((rh, cw), lambda i, pos_ref: (2 * pos_ref[0] + pos_ref[1], 0)))
            out_shape.append(jax.ShapeDtypeStruct((N_CHIPS * r, cw), sh.dtype))
    return pl.pallas_call(
        body, name=name, out_shape=tuple(out_shape),
        grid_spec=pltpu.PrefetchScalarGridSpec(num_scalar_prefetch=1, grid=(1,), in_specs=in_specs,
                                               out_specs=tuple(out_specs)),
        compiler_params=_params(("arbitrary",)),
    )(pos, *shards)


def _gather_start(shards, lands, cols, per_layer, name):
    n = len(shards)
    nl = n // per_layer

    def body(*refs):
        sh, ld = refs[:n], refs[n:2 * n]
        sems, token = refs[2 * n:2 * n + 2 * nl], refs[-1]
        x, y, c = _position()
        for k in range(n):
            l, a = divmod(k, per_layer)
            r, cw = shards[k].shape
            src = sh[k].at[pl.ds(c * (r // 2), r // 2), :]
            dst = _weight_window(ld[k], cols[k], r, cw, x, y, c)
            for j, peer in enumerate(_peers(x, y, c)):
                pltpu.make_async_remote_copy(src_ref=src, dst_ref=dst, send_sem=sems[2 * l].at[4 * a + j],
                                             recv_sem=sems[2 * l + 1].at[4 * a + j], device_id=peer,
                                             device_id_type=MESH).start()
        token[...] = jnp.zeros_like(token)

    arrs = list(shards) + list(lands)
    out = pl.pallas_call(
        body, name=name,
        out_shape=tuple(pltpu.SemaphoreType.DMA((per_layer * 4,)) for _ in range(2 * nl))
        + tuple(pltpu.HBM(a.shape, a.dtype) for a in arrs) + (TOKEN,),
        in_specs=[HBM_SPEC] * (2 * n),
        out_specs=(SEM_SPEC,) * (2 * nl) + (HBM_SPEC,) * (2 * n) + (pl.BlockSpec(memory_space=pltpu.VMEM),),
        input_output_aliases={i: 2 * nl + i for i in range(2 * n)}, compiler_params=SPLIT_PARAMS,
    )(*[_hbm(a) for a in arrs])
    return out[:2 * nl], out[2 * nl:2 * nl + n], out[2 * nl + n:2 * nl + 2 * n], out[-1]


def _gather_wait(shards, lands, send, recv, after, cols, first, name):
    m = len(shards)

    def body(*refs):
        sh, ld = refs[:m], refs[m:2 * m]
        send_ref, recv_ref = refs[2 * m], refs[2 * m + 1]
        x, y, c = _position()
        for a in range(m):
            r, cw = shards[a].shape
            src = sh[a].at[pl.ds(c * (r // 2), r // 2), :]
            for j, (px, py, pc) in enumerate(_peers(x, y, c)):
                cp = pltpu.make_async_remote_copy(
                    src_ref=src, dst_ref=_weight_window(ld[a], cols[a], r, cw, px, py, pc),
                    send_sem=send_ref.at[4 * (first + a) + j], recv_sem=recv_ref.at[4 * (first + a) + j],
                    device_id=(px, py, pc),
                    device_id_type=MESH)
                cp.wait_send()
                cp.wait_recv()

    arrs = list(shards) + list(lands)
    out = pl.pallas_call(
        body, name=name, out_shape=tuple(pltpu.HBM(a.shape, a.dtype) for a in arrs),
        in_specs=[HBM_SPEC] * (2 * m) + [SEM_SPEC, SEM_SPEC, ANY], out_specs=(HBM_SPEC,) * (2 * m),
        input_output_aliases={i: i for i in range(2 * m)}, compiler_params=SPLIT_PARAMS,
    )(*arrs, send, recv, after)
    return out[m:]


def _forward_sibling(lands, cols, shard_shapes, name):
    m = len(lands)

    def body(*refs):
        ins, outs = refs[:m], refs[m:2 * m]
        send_sems, recv_sems = refs[2 * m:]
        x, y, c = _position()
        chips = [(1 - x, y), (x, 1 - y), (1 - x, 1 - y)]
        sends = []
        for a in range(m):
            r, cw = shard_shapes[a]
            for j, (px, py) in enumerate(chips):
                cp = pltpu.make_async_remote_copy(
                    src_ref=_weight_window(ins[a], cols[a], r, cw, px, py, c),
                    dst_ref=_weight_window(outs[a], cols[a], r, cw, px, py, c),
                    send_sem=send_sems.at[a, j], recv_sem=recv_sems.at[a, j], device_id=(x, y, 1 - c),
                    device_id_type=MESH)
                cp.start()
                sends.append(cp)
        for a in range(m):
            r, cw = shard_shapes[a]
            for j, (px, py) in enumerate(chips):
                pltpu.make_async_remote_copy(
                    src_ref=_weight_window(ins[a], cols[a], r, cw, px, py, c),
                    dst_ref=_weight_window(outs[a], cols[a], r, cw, px, py, 1 - c),
                    send_sem=send_sems.at[a, j], recv_sem=recv_sems.at[a, j], device_id=(x, y, 1 - c),
                    device_id_type=MESH).wait_recv()
        for cp in sends:
            cp.wait_send()

    return pl.pallas_call(
        body, name=name, out_shape=tuple(jax.ShapeDtypeStruct(a.shape, a.dtype) for a in lands),
        in_specs=[ANY] * m, out_specs=tuple([ANY] * m), input_output_aliases={i: i for i in range(m)},
        scratch_shapes=[pltpu.SemaphoreType.DMA((m, 3)), pltpu.SemaphoreType.DMA((m, 3))],
        compiler_params=pltpu.CompilerParams(vmem_limit_bytes=VMEM_LIMIT_BYTES),
    )(*lands)


def _swap_start(ins, name, other_half=False):
    n = len(ins)
    shape = lambda a: (a.shape[0],) + a.shape[2:] if other_half else a.shape
    lands = [lax.empty(shape(a), a.dtype) for a in ins]

    def body(*refs):
        src, ld = refs[:n], refs[n:2 * n]
        send_ref, recv_ref, token = refs[2 * n], refs[2 * n + 1], refs[-1]
        x, y, c = _position()
        for k in range(n):
            pltpu.make_async_remote_copy(
                src_ref=src[k].at[:, 1 - c] if other_half else src[k], dst_ref=ld[k], send_sem=send_ref.at[k],
                recv_sem=recv_ref.at[k], device_id=(x, y, 1 - c), device_id_type=MESH).start()
        token[...] = jnp.zeros_like(token)

    arrs = list(ins) + lands
    out = pl.pallas_call(
        body, name=name,
        out_shape=(pltpu.SemaphoreType.DMA((n,)), pltpu.SemaphoreType.DMA((n,)))
        + tuple(pltpu.HBM(a.shape, a.dtype) for a in arrs) + (TOKEN,),
        in_specs=[HBM_SPEC] * (2 * n),
        out_specs=(SEM_SPEC, SEM_SPEC) + (HBM_SPEC,) * (2 * n) + (pl.BlockSpec(memory_space=pltpu.VMEM),),
        input_output_aliases={i: 2 + i for i in range(2 * n)}, compiler_params=SPLIT_PARAMS,
    )(*[_hbm(a) for a in arrs])
    return out[0], out[1], out[2:2 + n], out[2 + n:2 + 2 * n], out[-1]


def _swap_wait(ins, lands, send, recv, after, name, other_half=False, first=0):
    n = len(ins)

    def body(*refs):
        src, ld = refs[:n], refs[n:2 * n]
        send_ref, recv_ref = refs[2 * n], refs[2 * n + 1]
        x, y, c = _position()
        for k in range(n):
            cp = pltpu.make_async_remote_copy(
                src_ref=src[k].at[:, 1 - c] if other_half else src[k], dst_ref=ld[k], send_sem=send_ref.at[first + k],
                recv_sem=recv_ref.at[first + k], device_id=(x, y, 1 - c), device_id_type=MESH)
            cp.wait_send()
            cp.wait_recv()

    arrs = list(ins) + list(lands)
    out = pl.pallas_call(
        body, name=name, out_shape=tuple(pltpu.HBM(a.shape, a.dtype) for a in arrs),
        in_specs=[HBM_SPEC] * (2 * n) + [SEM_SPEC, SEM_SPEC, ANY], out_specs=(HBM_SPEC,) * (2 * n),
        input_output_aliases={i: i for i in range(2 * n)}, compiler_params=SPLIT_PARAMS,
    )(*arrs, send, recv, after)
    return out[:n], out[n:]


def _forward_start(lands, cols, shard_shapes, name):
    m = len(lands)

    def body(*refs):
        ld = refs[:m]
        send_ref, recv_ref, token = refs[m], refs[m + 1], refs[-1]
        x, y, c = _position()
        for a in range(m):
            r, cw = shard_shapes[a]
            for j, (px, py) in enumerate([(1 - x, y), (x, 1 - y), (1 - x, 1 - y)]):
                win = _weight_window(ld[a], cols[a], r, cw, px, py, c)
                pltpu.make_async_remote_copy(src_ref=win, dst_ref=win, send_sem=send_ref.at[3 * a + j],
                                             recv_sem=recv_ref.at[3 * a + j], device_id=(x, y, 1 - c),
                                             device_id_type=MESH).start()
        token[...] = jnp.zeros_like(token)

    out = pl.pallas_call(
        body, name=name,
        out_shape=(pltpu.SemaphoreType.DMA((m * 3,)), pltpu.SemaphoreType.DMA((m * 3,)))
        + tuple(pltpu.HBM(a.shape, a.dtype) for a in lands) + (TOKEN,),
        in_specs=[HBM_SPEC] * m,
        out_specs=(SEM_SPEC, SEM_SPEC) + (HBM_SPEC,) * m + (pl.BlockSpec(memory_space=pltpu.VMEM),),
        input_output_aliases={i: 2 + i for i in range(m)}, compiler_params=SPLIT_PARAMS,
    )(*[_hbm(a) for a in lands])
    return out[0], out[1], out[2:2 + m], out[-1]


def _forward_wait(lands, send, recv, after, cols, shard_shapes, name):
    m = len(lands)

    def body(*refs):
        ld = refs[:m]
        send_ref, recv_ref = refs[m], refs[m + 1]
        x, y, c = _position()
        for a in range(m):
            r, cw = shard_shapes[a]
            for j, (px, py) in enumerate([(1 - x, y), (x, 1 - y), (1 - x, 1 - y)]):
                cp = pltpu.make_async_remote_copy(
                    src_ref=_weight_window(ld[a], cols[a], r, cw, px, py, c),
                    dst_ref=_weight_window(ld[a], cols[a], r, cw, px, py, 1 - c),
                    send_sem=send_ref.at[3 * a + j], recv_sem=recv_ref.at[3 * a + j], device_id=(x, y, 1 - c),
                    device_id_type=MESH)
                cp.wait_send()
                cp.wait_recv()

    return pl.pallas_call(
        body, name=name, out_shape=tuple(pltpu.HBM(a.shape, a.dtype) for a in lands),
        in_specs=[HBM_SPEC] * m + [SEM_SPEC, SEM_SPEC, ANY], out_specs=(HBM_SPEC,) * m,
        input_output_aliases={i: i for i in range(m)}, compiler_params=SPLIT_PARAMS,
    )(*lands, send, recv, after)


def _exchange_start(sums, name):
    m = len(sums)
    lands = [lax.empty((3,) + s.shape[1:], s.dtype) for s in sums]

    def body(*refs):
        cs, ld = refs[:m], refs[m:2 * m]
        send_ref, recv_ref, token = refs[2 * m], refs[2 * m + 1], refs[-1]
        x, y, c = _position()
        for a in range(m):
            for j, (px, py) in enumerate([(1 - x, y), (x, 1 - y), (1 - x, 1 - y)]):
                pltpu.make_async_remote_copy(
                    src_ref=cs[a].at[2 * px + py], dst_ref=ld[a].at[j], send_sem=send_ref.at[3 * a + j],
                    recv_sem=recv_ref.at[3 * a + j], device_id=(px, py, c), device_id_type=MESH).start()
        token[...] = jnp.zeros_like(token)

    arrs = list(sums) + lands
    out = pl.pallas_call(
        body, name=name,
        out_shape=(pltpu.SemaphoreType.DMA((m * 3,)), pltpu.SemaphoreType.DMA((m * 3,)))
        + tuple(pltpu.HBM(a.shape, a.dtype) for a in arrs) + (TOKEN,),
        in_specs=[HBM_SPEC] * (2 * m),
        out_specs=(SEM_SPEC, SEM_SPEC) + (HBM_SPEC,) * (2 * m) + (pl.BlockSpec(memory_space=pltpu.VMEM),),
        input_output_aliases={i: 2 + i for i in range(2 * m)}, compiler_params=SPLIT_PARAMS,
    )(*[_hbm(a) for a in arrs])
    return out[0], out[1], out[2:2 + m], out[2 + m:2 + 2 * m], out[-1]


def _exchange_wait(sums, lands, send, recv, after, name):
    m = len(sums)

    def body(*refs):
        cs, ld = refs[:m], refs[m:2 * m]
        send_ref, recv_ref = refs[2 * m], refs[2 * m + 1]
        x, y, c = _position()
        for a in range(m):
            for j, (px, py) in enumerate([(1 - x, y), (x, 1 - y), (1 - x, 1 - y)]):
                cp = pltpu.make_async_remote_copy(
                    src_ref=cs[a].at[2 * px + py], dst_ref=ld[a].at[j], send_sem=send_ref.at[3 * a + j],
                    recv_sem=recv_ref.at[3 * a + j], device_id=(px, py, c), device_id_type=MESH)
                cp.wait_send()
                cp.wait_recv()

    arrs = list(sums) + list(lands)
    out = pl.pallas_call(
        body, name=name, out_shape=tuple(pltpu.HBM(a.shape, a.dtype) for a in arrs),
        in_specs=[HBM_SPEC] * (2 * m) + [SEM_SPEC, SEM_SPEC, ANY], out_specs=(HBM_SPEC,) * (2 * m),
        input_output_aliases={i: i for i in range(2 * m)}, compiler_params=SPLIT_PARAMS,
    )(*arrs, send, recv, after)
    return out[:m], out[m:]


def _place_row(payload, dev, name):
    R = payload.shape[0]
    tr = _pick(R, (512, 448, 384, 256, 192, 128, 64, 32, 16, 8))

    def body(dev_ref, x_ref, o_ref):
        o_ref[...] = x_ref[...]

    return pl.pallas_call(
        body, name=name, out_shape=jax.ShapeDtypeStruct((N_DEV, R, LANES), payload.dtype),
        grid_spec=pltpu.PrefetchScalarGridSpec(
            num_scalar_prefetch=1, grid=(R // tr,),
            in_specs=[pl.BlockSpec((tr, LANES), lambda i, dev_ref: (i, 0))],
            out_specs=pl.BlockSpec((None, tr, LANES), lambda i, dev_ref: (dev_ref[0], i, 0))),
        compiler_params=_params(("arbitrary",)),
    )(dev, payload)


def _others(x, y, c):
    return [(1 - x if fx else x, 1 - y if fy else y, 1 - c if fc else c)
            for fx in (0, 1) for fy in (0, 1) for fc in (0, 1) if fx or fy or fc]


def _broadcast_start(payload, land, name):
    def body(p_ref, l_ref, send_ref, recv_ref, p_thru, l_thru, token):
        x, y, c = _position()
        for j, peer in enumerate(_others(x, y, c)):
            pltpu.make_async_remote_copy(src_ref=p_ref, dst_ref=l_ref.at[4 * x + 2 * y + c], send_sem=send_ref.at[j],
                                         recv_sem=recv_ref.at[j], device_id=peer, device_id_type=MESH).start()
        token[...] = jnp.zeros_like(token)

    n = N_DEV - 1
    return pl.pallas_call(
        body, name=name,
        out_shape=(pltpu.SemaphoreType.DMA((n,)), pltpu.SemaphoreType.DMA((n,)), pltpu.HBM(payload.shape, payload.dtype),
                   pltpu.HBM(land.shape, land.dtype), TOKEN),
        in_specs=[HBM_SPEC, HBM_SPEC],
        out_specs=(SEM_SPEC, SEM_SPEC, HBM_SPEC, HBM_SPEC, pl.BlockSpec(memory_space=pltpu.VMEM)),
        input_output_aliases={0: 2, 1: 3}, compiler_params=SPLIT_PARAMS,
    )(_hbm(payload), _hbm(land))


def _broadcast_wait(payload, land, send, recv, after, name):
    def body(p_ref, l_ref, send_ref, recv_ref, after_ref, p_thru, l_thru):
        x, y, c = _position()
        for j, (px, py, pc) in enumerate(_others(x, y, c)):
            cp = pltpu.make_async_remote_copy(src_ref=p_ref, dst_ref=l_ref.at[4 * px + 2 * py + pc],
                                              send_sem=send_ref.at[j], recv_sem=recv_ref.at[j],
                                              device_id=(px, py, pc), device_id_type=MESH)
            cp.wait_send()
            cp.wait_recv()

    out = pl.pallas_call(
        body, name=name, out_shape=(pltpu.HBM(payload.shape, payload.dtype), pltpu.HBM(land.shape, land.dtype)),
        in_specs=[HBM_SPEC, HBM_SPEC, SEM_SPEC, SEM_SPEC, ANY], out_specs=(HBM_SPEC, HBM_SPEC),
        input_output_aliases={0: 0, 1: 1}, compiler_params=SPLIT_PARAMS,
    )(payload, land, send, recv, after)
    return out[1]


def _vec(a):
    return a.reshape(1, -1)


def _local_step(x, tgt, mod, W, P, get_w=None, on_grads=None):
    Dm = D_MODEL
    G = {k: [] for k in ("gm_w_in", "gm_w_out", "hg_w_in", "hg_w_out", "ffn_w_up", "ffn_w_down")}
    lb_all = _lb_fwd(P["hg_lb"], "lb_fwd")
    saved = []
    xs = x
    y_prev = gate_prev = None
    layer_w = [None] * DEPTH

    def wmm(xa, kind, i, mode, name):
        if layer_w[i] is not None:
            return _mm(xa, layer_w[i][kind], mode, name)
        return _mm(xa, W[kind], mode, name, b_layer=i if kind.startswith("ffn") else i // 2)

    for i in range(DEPTH):
        m = [_vec(mod[i, j * Dm:(j + 1) * Dm]) for j in range(6)]
        sh1, sc1, g1, sh2, sc2, g2 = m
        j = i // 2
        if get_w is not None:
            layer_w[i] = get_w(i, 0, xs if y_prev is None else y_prev)
        xs, h = _norm_fwd(xs, y_prev, gate_prev, _vec(P["norm_g"][i, 0]), sc1, sh1, f"norm_fwd_a{i}")
        rec = dict(x1=xs, h1=h)
        if i % 2 == 0:
            a = wmm(h, "gm_w_in", i, "nn", f"gm_in{i}")
            gated = _gm_mid_fwd(a, _vec(P["gm_ln_g"][j]), _vec(P["gm_ln_b"][j]), P["gm_w_s"][j],
                                P["gm_b_s"][j].reshape(GM_HEADS, GM_BLOCK, 1), f"gm_mid_fwd{i}")
            if get_w is not None:
                layer_w[i].update(get_w(i, 1, gated))
            y1 = wmm(gated, "gm_w_out", i, "nn", f"gm_out{i}")
            rec.update(a=a, act=gated)
        else:
            p = wmm(h, "hg_w_in", i, "nn", f"hg_in{i}")
            o, og, states = _hg_scan_fwd(p, _vec(lb_all[j]), _vec(P["hg_gn_g"][j]), f"hg_scan_fwd{i}")
            if get_w is not None:
                layer_w[i].update(get_w(i, 1, og))
            y1 = wmm(og, "hg_w_out", i, "nn", f"hg_out{i}")
            rec.update(a=p, act=og, o=o, states=states)
        rec["y1"] = y1
        xs, h2 = _norm_fwd(xs, y1, g1, _vec(P["norm_g"][i, 1]), sc2, sh2, f"norm_fwd_b{i}")
        a2 = wmm(h2, "ffn_w_up", i, "nn", f"ffn_up{i}")
        mm_ = _conv_fwd(a2, P["ffn_conv_w"][i], _vec(P["ffn_conv_b"][i]), f"conv_fwd{i}")
        y2 = wmm(mm_, "ffn_w_down", i, "nn", f"ffn_down{i}")
        rec.update(x2=xs, h2=h2, a2=a2, m=mm_, y2=y2, mods=m)
        saved.append(rec)
        y_prev, gate_prev = y2, g2
    dx, dy, loss, s_fg, s_gate = _loss_head(xs, y_prev, gate_prev, _vec(P["final_g"]), tgt, "loss_head")
    small = dict(final_g=s_fg, norm_g=[None] * DEPTH, dmod=[None] * DEPTH, ffn_conv_w=[None] * DEPTH,
                 ffn_conv_b=[None] * DEPTH, gm_ln_g=[None] * 2, gm_ln_b=[None] * 2, gm_w_s=[None] * 2,
                 gm_b_s=[None] * 2, hg_gn_g=[None] * 2, dlb=[None] * 2)
    for i in reversed(range(DEPTH)):
        rec = saved[i]
        sh1, sc1, g1, sh2, sc2, g2 = rec["mods"]
        j = i // 2
        d_g2 = s_gate
        dm = wmm(dy, "ffn_w_down", i, "nt", f"ffn_down_dx{i}")
        G["ffn_w_down"].append(_mm(rec["m"], dy, "tn", f"ffn_down_dw{i}", out_dtype=GRAD_WIRE))
        da2, dcw, dcb = _conv_bwd(rec["a2"], dm, P["ffn_conv_w"][i], _vec(P["ffn_conv_b"][i]), f"conv_bwd{i}")
        small["ffn_conv_w"][i], small["ffn_conv_b"][i] = dcw, dcb
        dh2 = wmm(da2, "ffn_w_up", i, "nt", f"ffn_up_dx{i}")
        G["ffn_w_up"].append(_mm(rec["h2"], da2, "tn", f"ffn_up_dw{i}", out_dtype=GRAD_WIRE, exchange_out=True))
        ng2 = _vec(P["norm_g"][i, 1])
        if on_grads is not None:
            ng2 = ng2 + on_grads(i, {k: G[k][-1] for k in ("ffn_w_up", "ffn_w_down")})
        dx, dy, s_sh2, s_x2, d_g1 = _norm_bwd(rec["x2"], dh2, dx, ng2, sc2, rec["y1"], g1, f"norm_bwd_b{i}")
        d_sc2, d_ng2 = s_x2 * ng2, s_x2 * (1.0 + sc2)
        if i % 2 == 0:
            dgated = wmm(dy, "gm_w_out", i, "nt", f"gm_out_dx{i}")
            G["gm_w_out"].append(_mm(rec["act"], dy, "tn", f"gm_out_dw{i}", out_dtype=GRAD_WIRE))
            da, dws, dbs, dlg, dlbeta = _gm_mid_bwd(
                rec["a"], dgated, _vec(P["gm_ln_g"][j]), _vec(P["gm_ln_b"][j]), P["gm_w_s"][j],
                P["gm_b_s"][j].reshape(GM_HEADS, GM_BLOCK, 1), f"gm_mid_bwd{i}")
            small["gm_w_s"][j], small["gm_b_s"][j] = dws, dbs[:, :GM_HEADS].T
            small["gm_ln_g"][j], small["gm_ln_b"][j] = dlg, dlbeta
            dh1 = wmm(da, "gm_w_in", i, "nt", f"gm_in_dx{i}")
            G["gm_w_in"].append(_mm(rec["h1"], da, "tn", f"gm_in_dw{i}", out_dtype=GRAD_WIRE, exchange_out=True))
        else:
            dog = wmm(dy, "hg_w_out", i, "nt", f"hg_out_dx{i}")
            G["hg_w_out"].append(_mm(rec["act"], dy, "tn", f"hg_out_dw{i}", out_dtype=GRAD_WIRE))
            dp, dlb, dgn = _hg_scan_bwd(rec["a"], _vec(lb_all[j]), _vec(P["hg_gn_g"][j]), rec["o"], dog,
                                        rec["states"], f"hg_scan_bwd{i}")
            small["dlb"][j], small["hg_gn_g"][j] = dlb, dgn
            dh1 = wmm(dp, "hg_w_in", i, "nt", f"hg_in_dx{i}")
            G["hg_w_in"].append(_mm(rec["h1"], dp, "tn", f"hg_in_dw{i}", out_dtype=GRAD_WIRE, exchange_out=True))
        ng1 = _vec(P["norm_g"][i, 0])
        if on_grads is not None:
            mixer = ("gm_w_in", "gm_w_out") if i % 2 == 0 else ("hg_w_in", "hg_w_out")
            ng1 = ng1 + on_grads(i, {k: G[k][-1] for k in mixer})
        if i > 0:
            prev = saved[i - 1]
            dx, dy, s_sh1, s_x1, s_gate = _norm_bwd(rec["x1"], dh1, dx, ng1, sc1, prev["y2"], prev["mods"][5],
                                                    f"norm_bwd_a{i}")
        else:
            dx, s_sh1, s_x1 = _norm_bwd(rec["x1"], dh1, dx, ng1, sc1, None, None, f"norm_bwd_a{i}")
        d_sc1, d_ng1 = s_x1 * ng1, s_x1 * (1.0 + sc1)
        small["norm_g"][i] = jnp.concatenate([d_ng1, d_ng2], axis=0)
        small["dmod"][i] = jnp.concatenate([s_sh1, d_sc1, d_g1, s_sh2, d_sc2, d_g2], axis=1)
    for k in G:
        G[k] = G[k][::-1]
    dlb_all = jnp.concatenate(small.pop("dlb"), axis=0)
    small["hg_lb"] = _lb_bwd(P["hg_lb"], dlb_all, "lb_bwd")
    return loss, dx, G, small


BIG = ("gm_w_in", "gm_w_out", "hg_w_in", "hg_w_out", "ffn_w_up", "ffn_w_down")
COL_SHARDED = dict(gm_w_in=True, gm_w_out=False, hg_w_in=True, hg_w_out=False, ffn_w_up=True, ffn_w_down=False)
LAYER_WEIGHTS = 4


def _layer_kinds(i):
    return (("gm_w_in", "gm_w_out") if i % 2 == 0 else ("hg_w_in", "hg_w_out")) + ("ffn_w_up", "ffn_w_down")


def _pack(pieces):
    flat = [p.reshape(-1).astype(F32) for p in pieces]
    offs, tot = [], 0
    for f in flat:
        offs.append((tot, f.shape[0]))
        tot += f.shape[0]
    padded = -(-tot // (8 * LANES)) * (8 * LANES)
    if padded > tot:
        flat.append(jnp.zeros((padded - tot,), F32))
    return jnp.concatenate(flat).reshape(-1, LANES), offs


def _unpack(rows, offs, shapes):
    lead = rows.shape[:-2]
    flat = rows.reshape(lead + (-1,))
    return [flat[..., o:o + n].reshape(lead + tuple(s)) for (o, n), s in zip(offs, shapes)]


def _from_chips(per_dev, axis):
    per_chip = per_dev[0::2]
    return jnp.concatenate([per_chip[s] for s in range(N_CHIPS)], axis=axis)


def kernel(x, c, gm_w_in, gm_ln_g, gm_ln_b, gm_w_s, gm_b_s, gm_w_out, hg_w_in, hg_lb, hg_gn_g, hg_w_out, ffn_w_up, ffn_conv_w, ffn_conv_b, ffn_w_down, norm_g, ada_w, ada_b, final_g, loss_target, m_gm_w_in, m_gm_ln_g, m_gm_ln_b, m_gm_w_s, m_gm_b_s, m_gm_w_out, m_hg_w_in, m_hg_lb, m_hg_gn_g, m_hg_w_out, m_ffn_w_up, m_ffn_conv_w, m_ffn_conv_b, m_ffn_w_down, m_norm_g, m_ada_w, m_ada_b, m_final_g, v_gm_w_in, v_gm_ln_g, v_gm_ln_b, v_gm_w_s, v_gm_b_s, v_gm_w_out, v_hg_w_in, v_hg_lb, v_hg_gn_g, v_hg_w_out, v_ffn_w_up, v_ffn_conv_w, v_ffn_conv_b, v_ffn_w_down, v_norm_g, v_ada_w, v_ada_b, v_final_g):
    Dm = D_MODEL
    xi, yi, ci = _position()
    chip = 2 * xi + yi
    dev = 4 * xi + 2 * yi + ci
    weights = dict(gm_w_in=gm_w_in, gm_ln_g=gm_ln_g, gm_ln_b=gm_ln_b, gm_w_s=gm_w_s, gm_b_s=gm_b_s,
                   gm_w_out=gm_w_out, hg_w_in=hg_w_in, hg_lb=hg_lb, hg_gn_g=hg_gn_g, hg_w_out=hg_w_out,
                   ffn_w_up=ffn_w_up, ffn_conv_w=ffn_conv_w, ffn_conv_b=ffn_conv_b, ffn_w_down=ffn_w_down,
                   norm_g=norm_g, ada_w=ada_w, ada_b=ada_b, final_g=final_g)
    mom_m = dict(gm_w_in=m_gm_w_in, gm_ln_g=m_gm_ln_g, gm_ln_b=m_gm_ln_b, gm_w_s=m_gm_w_s, gm_b_s=m_gm_b_s,
                 gm_w_out=m_gm_w_out, hg_w_in=m_hg_w_in, hg_lb=m_hg_lb, hg_gn_g=m_hg_gn_g, hg_w_out=m_hg_w_out,
                 ffn_w_up=m_ffn_w_up, ffn_conv_w=m_ffn_conv_w, ffn_conv_b=m_ffn_conv_b, ffn_w_down=m_ffn_w_down,
                 norm_g=m_norm_g, ada_w=m_ada_w, ada_b=m_ada_b, final_g=m_final_g)
    mom_v = dict(gm_w_in=v_gm_w_in, gm_ln_g=v_gm_ln_g, gm_ln_b=v_gm_ln_b, gm_w_s=v_gm_w_s, gm_b_s=v_gm_b_s,
                 gm_w_out=v_gm_w_out, hg_w_in=v_hg_w_in, hg_lb=v_hg_lb, hg_gn_g=v_hg_gn_g, hg_w_out=v_hg_w_out,
                 ffn_w_up=v_ffn_w_up, ffn_conv_w=v_ffn_conv_w, ffn_conv_b=v_ffn_conv_b, ffn_w_down=v_ffn_w_down,
                 norm_g=v_norm_g, ada_w=v_ada_w, ada_b=v_ada_b, final_g=v_final_g)
    order = list(weights)

    pos = jnp.stack([chip, ci]).astype(jnp.int32)
    shards, by_col = [], []
    for i in range(DEPTH):
        for k in _layer_kinds(i):
            shards.append(weights[k][i if k.startswith("ffn") else i // 2].astype(BF16))
            by_col.append(COL_SHARDED[k])
    placed = []
    for i in range(DEPTH):
        s = slice(LAYER_WEIGHTS * i, LAYER_WEIGHTS * (i + 1))
        placed += list(_place_own(shards[s], by_col[s], pos, f"place_own{i}"))
    gsems, sh_thru, ld_thru, _ = _gather_start(shards, placed, by_col, LAYER_WEIGHTS, "gather_start")

    pieces = [c, hg_lb, hg_gn_g, norm_g, ffn_conv_w]
    payload, offs = _pack(pieces)
    got = _allgather_small(payload, "gather_small")
    c_g, lb_g, gn_g, ng_g, cw_g = _unpack(got, offs, [p.shape for p in pieces])
    c_all = c_g.reshape(N_DEV, Dm)
    P = dict(hg_lb=_from_chips(lb_g, 1), hg_gn_g=_from_chips(gn_g, 1), norm_g=_from_chips(ng_g, 2),
             ffn_conv_w=_from_chips(cw_g, 2), gm_ln_g=gm_ln_g, gm_ln_b=gm_ln_b, gm_w_s=gm_w_s, gm_b_s=gm_b_s,
             ffn_conv_b=ffn_conv_b, final_g=final_g)

    cols = ada_w.shape[2]
    ada_b_sh = lax.dynamic_slice_in_dim(ada_b, chip * cols, cols, axis=1)
    mod_sh = _ada_fwd(c_all, ada_w, ada_b_sh, "ada_fwd")
    mod_g = _allgather_small(mod_sh.reshape(-1, LANES), "gather_mod").reshape(N_DEV, DEPTH, N_DEV, cols)
    mod_mine = lax.dynamic_index_in_dim(mod_g[0::2], dev, axis=2, keepdims=False)
    mod = jnp.transpose(mod_mine, (1, 0, 2)).reshape(DEPTH, N_CHIPS * cols)

    core = jnp.reshape(ci, (1,)).astype(jnp.int32)
    chip_arr = jnp.reshape(chip, (1,)).astype(jnp.int32)
    pending, held, prefetched, swapping = [], {}, {}, []

    def get_w(i, group, after):
        lo, hi = LAYER_WEIGHTS * i, LAYER_WEIGHTS * (i + 1)
        shapes = lambda s: [a.shape for a in shards[s]]
        out = {}
        if i == 0:
            s = slice(lo, lo + 1) if group == 0 else slice(lo + 1, hi)
            landed = _gather_wait(sh_thru[s], ld_thru[s], gsems[0], gsems[1], after, by_col[s], s.start - lo,
                                  f"gather_wait0_{group}")
            full = _forward_sibling(landed, by_col[s], shapes(s), f"gather_forward0_{group}")
            out = dict(zip(_layer_kinds(0)[s.start - lo:s.stop - lo], full))
        elif group == 0:
            s = slice(lo, hi)
            send, recv, lands = prefetched.pop(i)
            full = _forward_wait(lands, send, recv, after, by_col[s], shapes(s), f"gather_forward_wait{i}")
            out = dict(zip(_layer_kinds(i), full))
        if group == 1 and i + 1 < DEPTH:
            s = slice(hi, hi + LAYER_WEIGHTS)
            landed = _gather_wait(sh_thru[s], ld_thru[s], gsems[2 * i + 2], gsems[2 * i + 3], after, by_col[s], 0,
                                  f"gather_wait{i + 1}")
            send, recv, lands, _ = _forward_start(landed, by_col[s], shapes(s), f"gather_forward_start{i + 1}")
            prefetched[i + 1] = (send, recv, lands)
        return out

    def on_grads(i, gdict):
        if i > 0 and "ffn_w_up" in gdict:
            held[i] = gdict
            return 0.0
        gdict = {**held.pop(i, {}), **gdict}
        kinds = [k for k in _layer_kinds(i) if k in gdict]
        tag = f"{i}_ffn" if kinds[0] == "ffn_w_up" else f"{i}"
        g4 = []
        for k in kinds:
            g = gdict[k]
            if not COL_SHARDED[k]:
                R, C = g.shape
                g = g.reshape(N_CHIPS, 2, R // (2 * N_CHIPS), C)
            g4.append(g)
        token = finish_swap(g4[0]) if swapping else 0.0
        if i == 0:
            from_sib = _swap_sibling(g4, f"reduce_swap{tag}", other_half=True)
            return token + start_exchange(tag, i, kinds, g4, from_sib)
        send, recv, g_thru, lands, tok = _swap_start(g4, f"reduce_swap_start{tag}", other_half=True)
        swapping.append((tag, i, kinds, send, recv, g_thru, lands))
        return token + tok[0, 0]

    def start_exchange(tag, i, kinds, g4, from_sib):
        sums = list(_add_own_halves(list(g4), list(from_sib), core, f"chip_sum{tag}"))
        send, recv, sums_thru, lands, token = _exchange_start(sums, f"reduce_start{tag}")
        pending.append((tag, i, kinds, send, recv, sums_thru, lands))
        return token[0, 0]

    def finish_swap(after):
        tag, i, kinds, send, recv, g_thru, lands = swapping.pop()
        g4, from_sib = _swap_wait(g_thru, lands, send, recv, after, f"reduce_swap_wait{tag}", other_half=True)
        return start_exchange(tag, i, kinds, g4, from_sib)

    loss_part, dx, G, small = _local_step(x[0], loss_target[0], mod, None, P, get_w, on_grads)

    sum_pieces = [loss_part[:, :1], small["final_g"], jnp.stack(small["gm_ln_g"]), jnp.stack(small["gm_ln_b"]),
                  jnp.stack(small["gm_w_s"]), jnp.stack(small["gm_b_s"]), jnp.stack(small["ffn_conv_b"]),
                  small["hg_lb"], jnp.stack(small["hg_gn_g"]), jnp.stack(small["norm_g"]),
                  jnp.stack(small["ffn_conv_w"])]
    dmod = jnp.concatenate(small["dmod"], axis=0)
    payload2, offs2 = _pack(sum_pieces + [dmod])
    placed2 = _place_row(payload2, jnp.reshape(dev, (1,)).astype(jnp.int32), "place_grads")
    bsend, brecv, p2_thru, l2_thru, small_token = _broadcast_start(payload2, placed2, "gather_grads_start")

    landed = {}
    for tag, i, kinds, send, recv, sums_thru, lands in pending:
        sums_i, lands_i = _exchange_wait(sums_thru, lands, send, recv, small_token, f"reduce_wait{tag}")
        for k, s_, l_ in zip(kinds, sums_i, lands_i):
            landed[(k, i)] = (l_, s_)
    own_halves = []
    for k in BIG:
        layers = [landed[(k, i)] for i in range(DEPTH) if (k, i) in landed]
        own_halves.append(_sum_chips([l_ for l_, _ in layers], [s_ for _, s_ in layers], chip_arr, f"sum_chips_{k}"))
    jsend, jrecv, own_thru, jlands, after = _swap_start(own_halves, "reduce_join_start")
    grads, deltas, new_m, new_v = {}, {}, {}, {}
    for n, k in enumerate(BIG):
        (own,), (recv,) = _swap_wait([own_thru[n]], [jlands[n]], jsend, jrecv, after, f"reduce_join_wait_{k}",
                                     first=n)
        grads[k], deltas[k], new_m[k], new_v[k] = _adamw_halves(
            weights[k], own, recv, mom_m[k], mom_v[k], core, f"adamw_{k}")
        after = new_v[k]

    got2 = _broadcast_wait(p2_thru, l2_thru, bsend, brecv, new_v[BIG[-1]], "gather_grads_wait")
    dmod_all = _unpack(got2, offs2[-1:], [dmod.shape])[0]
    summed = _sum_devices(got2, "sum_devices")
    (loss_s, d_final_g, d_ln_g, d_ln_b, d_ws, d_bs, d_cb, d_lb, d_gn, d_ng, d_cw) = _unpack(
        summed, offs2[:-1], [(1,), final_g.shape, gm_ln_g.shape, gm_ln_b.shape, gm_w_s.shape, gm_b_s.shape,
                             ffn_conv_b.shape, (2, Dm), (2, Dm), (DEPTH, 2, Dm), (DEPTH, 3, 2 * FFN_HIDDEN)])
    grads.update(final_g=d_final_g, gm_ln_g=d_ln_g, gm_ln_b=d_ln_b, gm_w_s=d_ws, gm_b_s=d_bs, ffn_conv_b=d_cb)
    grads["hg_lb"] = lax.dynamic_slice_in_dim(d_lb, chip * hg_lb.shape[1], hg_lb.shape[1], axis=1)
    grads["hg_gn_g"] = lax.dynamic_slice_in_dim(d_gn, chip * hg_gn_g.shape[1], hg_gn_g.shape[1], axis=1)
    grads["norm_g"] = lax.dynamic_slice_in_dim(d_ng, chip * norm_g.shape[2], norm_g.shape[2], axis=2)
    grads["ffn_conv_w"] = lax.dynamic_slice_in_dim(d_cw, chip * ffn_conv_w.shape[2], ffn_conv_w.shape[2], axis=2)
    dmod_sh = lax.dynamic_slice_in_dim(dmod_all, chip * cols, cols, axis=2)
    grads["ada_w"] = _ada_bwd(c_all, jnp.transpose(dmod_sh, (1, 0, 2)), "ada_bwd")
    grads["ada_b"] = _sum_devices(dmod_all.reshape(N_DEV, -1, LANES), "sum_ada_b").reshape(ada_b.shape)

    for k in order:
        if k in BIG:
            continue
        w = weights[k]
        shp = w.shape
        view = (-1, shp[-1]) if w.ndim > 1 else (8, -1)
        d, m2, v2 = _adamw(w.reshape(view), grads[k].reshape(view), mom_m[k].reshape(view), mom_v[k].reshape(view),
                           f"adamw_{k}")
        deltas[k], new_m[k], new_v[k] = d.reshape(shp), m2.reshape(shp), v2.reshape(shp)
        grads[k] = grads[k].reshape(shp)

    loss = loss_s.reshape(())
    return (loss, dx[None], *[grads[k] for k in order], *[deltas[k] for k in order],
            *[new_m[k] for k in order], *[new_v[k] for k in order])
```

```python
import functools

import jax
import jax.numpy as jnp
from jax import lax
from jax.experimental import pallas as pl
from jax.experimental.pallas import tpu as pltpu

F32 = jnp.float32
BF16 = jnp.bfloat16
HI = lax.Precision.HIGHEST
X3 = lax.Precision.HIGH
GRAD_WIRE = BF16
MESH = pl.DeviceIdType.MESH

D_MODEL = 1024
DEPTH = 4
EPS = 1e-6
GM_WIDTH = 2048
GM_HEADS = 8
GM_BLOCK = 128
GM_HEAD_DIM = 256
CHUNK = 64
HG_HEADS = 8
HG_DIM = 128
FFN_HIDDEN = 2816
N_CHIPS = 4
N_DEV = 8

ADAM_LR = 0.001
ADAM_B1 = 0.9
ADAM_B2 = 0.999
ADAM_EPS = 1e-08
ADAM_WD = 0.01
ADAM_STEP = 10

VMEM_LIMIT_BYTES = 56 * 1024 * 1024
ROW_TILE = 256
NORM_ROW_TILE = 512
LANES = 128

_SQRT_HALF = 0.7071067811865476
_INV_SQRT_2PI = 0.3989422804014327


def _pick(dim, prefs):
    for p in prefs:
        if dim % p == 0:
            return p
    return dim


def _params(sem):
    return pltpu.CompilerParams(dimension_semantics=sem, vmem_limit_bytes=VMEM_LIMIT_BYTES)


def _cdf(x):
    return 0.5 * (1.0 + lax.erf(x * _SQRT_HALF))


def _pdf(x):
    return jnp.exp(-0.5 * x * x) * _INV_SQRT_2PI


def _sig(x):
    return jax.nn.sigmoid(x)


def _dot(a, b, dims, prec=None):
    return lax.dot_general(a, b, (dims, ((), ())), precision=prec, preferred_element_type=F32)


NN = ((1,), (0,))
NT = ((1,), (1,))
TN = ((0,), (0,))


MM_VMEM_BUDGET = 40 * 1024 * 1024


def _mm_tiles(mode, M, N, K, a_bytes, b_bytes, exchange_out):
    tn = _pick(N, (1408, 1024, 512, 256, 128))
    tms = [t for t in (1408, 1024, 512, 256, 128) if M % t == 0 and not (exchange_out and (M // 2) % t)] or [M]
    tks = [K] + [t for t in (2816, 2048, 1408, 1024, 512, 256, 128) if t < K and K % t == 0]

    def fits(tm, tk):
        acc = tm * tn * 4 if tk < K else 0
        return 2 * tm * tk * a_bytes + 2 * tk * tn * b_bytes + 2 * tm * tn * 4 + acc <= MM_VMEM_BUDGET

    for min_tm in (min(512, tms[0]), 0):
        for tk in tks:
            for tm in tms:
                if tm >= min_tm and fits(tm, tk):
                    return tm, tn, tk
    return tms[-1], tn, tks[-1]


def _mm(a, b, mode, name, b_layer=None, out_dtype=F32, exchange_out=False):
    b2 = b.shape[-2:]
    if mode == "nn":
        (M, K), (_, N) = a.shape, b2
    elif mode == "nt":
        (M, K), (N, _) = a.shape, b2
    else:
        (K, M), (_, N) = a.shape, b2
    tm, tn, tk = _mm_tiles(mode, M, N, K, a.dtype.itemsize, b.dtype.itemsize, exchange_out)
    nk = K // tk
    dims = {"nn": NN, "nt": NT, "tn": TN}[mode]

    def body(a_ref, b_ref, o_ref, *scratch):
        part = _dot(a_ref[...].astype(BF16), b_ref[...].astype(BF16), dims)
        if nk == 1:
            o_ref[...] = part.astype(o_ref.dtype)
            return
        (acc_ref,) = scratch
        k = pl.program_id(2)

        @pl.when(k == 0)
        def _():
            acc_ref[...] = part

        @pl.when(k > 0)
        def _():
            acc_ref[...] += part

        @pl.when(k == nk - 1)
        def _():
            o_ref[...] = acc_ref[...].astype(o_ref.dtype)

    if mode == "tn":
        a_spec = pl.BlockSpec((tk, tm), lambda i, j, k: (k, i))
    else:
        a_spec = pl.BlockSpec((tm, tk), lambda i, j, k: (i, k))
    bblk = (tk, tn) if mode in ("nn", "tn") else (tn, tk)
    bidx = (lambda i, j, k: (k, j)) if mode in ("nn", "tn") else (lambda i, j, k: (j, k))
    if b_layer is None:
        b_spec = pl.BlockSpec(bblk, bidx)
    else:
        b_spec = pl.BlockSpec((None,) + bblk, lambda i, j, k: (b_layer,) + bidx(i, j, k))
    if exchange_out:
        mh, cw = M // 2, N // N_CHIPS
        assert mh % tm == 0 and cw % tn == 0
        out_shape = jax.ShapeDtypeStruct((N_CHIPS, 2, mh, cw), out_dtype)
        o_spec = pl.BlockSpec(
            (None, None, tm, tn),
            lambda i, j, k: (j // (cw // tn), i // (mh // tm), i % (mh // tm), j % (cw // tn)))
    else:
        out_shape = jax.ShapeDtypeStruct((M, N), out_dtype)
        o_spec = pl.BlockSpec((tm, tn), lambda i, j, k: (i, j))
    return pl.pallas_call(
        body, name=name, out_shape=out_shape, grid=(M // tm, N // tn, nk),
        in_specs=[a_spec, b_spec], out_specs=o_spec,
        scratch_shapes=[] if nk == 1 else [pltpu.VMEM((tm, tn), F32)],
        compiler_params=_params(("parallel", "parallel", "arbitrary")),
    )(a, b)


def _row_spec(tr, width):
    return pl.BlockSpec((tr, width), lambda i: (i, 0))


def _vec_spec(width, rows=1):
    return pl.BlockSpec((rows, width), lambda i: (0, 0))


def _norm_fwd(x, y, gate, g, sc, sh, name):
    T, Dm = x.shape
    tr = _pick(T, (2 * NORM_ROW_TILE, NORM_ROW_TILE, ROW_TILE))
    has_res = y is not None

    def body(*refs):
        if has_res:
            x_ref, y_ref, gate_ref, g_ref, sc_ref, sh_ref, xo_ref, h_ref = refs
            xv = x_ref[...] + gate_ref[...] * y_ref[...]
            xo_ref[...] = xv
        else:
            x_ref, g_ref, sc_ref, sh_ref, h_ref = refs
            xv = x_ref[...]
        rstd = lax.rsqrt(jnp.mean(xv * xv, axis=-1, keepdims=True) + EPS)
        h_ref[...] = ((xv * rstd * g_ref[...]) * (1.0 + sc_ref[...]) + sh_ref[...]).astype(BF16)

    row, vec = _row_spec(tr, Dm), _vec_spec(Dm)
    if has_res:
        ins, in_specs = (x, y, gate, g, sc, sh), [row, row, vec, vec, vec, vec]
        out_shape = (jax.ShapeDtypeStruct((T, Dm), F32), jax.ShapeDtypeStruct((T, Dm), BF16))
        out_specs = (row, row)
    else:
        ins, in_specs = (x, g, sc, sh), [row, vec, vec, vec]
        out_shape = jax.ShapeDtypeStruct((T, Dm), BF16)
        out_specs = row
    out = pl.pallas_call(body, name=name, out_shape=out_shape, grid=(T // tr,), in_specs=in_specs,
                         out_specs=out_specs, compiler_params=_params(("parallel",)))(*ins)
    return out if has_res else (x, out)


def _norm_bwd(x, dh, dxo, g, sc, y_prev, gate_prev, name):
    T, Dm = x.shape
    tr = _pick(T, (NORM_ROW_TILE, ROW_TILE))
    has_prev = y_prev is not None

    def body(*refs):
        if has_prev:
            x_ref, dh_ref, dxo_ref, g_ref, sc_ref, yp_ref, gp_ref, dx_ref, dyp_ref, s1_ref, s2_ref, s3_ref = refs
        else:
            x_ref, dh_ref, dxo_ref, g_ref, sc_ref, dx_ref, s1_ref, s2_ref = refs

        @pl.when(pl.program_id(0) == 0)
        def _():
            s1_ref[...] = jnp.zeros_like(s1_ref)
            s2_ref[...] = jnp.zeros_like(s2_ref)
            if has_prev:
                s3_ref[...] = jnp.zeros_like(s3_ref)

        xv = x_ref[...]
        rstd = lax.rsqrt(jnp.mean(xv * xv, axis=-1, keepdims=True) + EPS)
        xhat = xv * rstd
        dh = dh_ref[...]
        dxhat = dh * (g_ref[...] * (1.0 + sc_ref[...]))
        dx = dxo_ref[...] + rstd * (dxhat - xhat * jnp.mean(dxhat * xhat, axis=-1, keepdims=True))
        dx_ref[...] = dx
        s1_ref[...] += jnp.sum(dh, axis=0, keepdims=True)
        s2_ref[...] += jnp.sum(dh * xhat, axis=0, keepdims=True)
        if has_prev:
            dyp_ref[...] = (gp_ref[...] * dx).astype(BF16)
            s3_ref[...] += jnp.sum(dx * yp_ref[...], axis=0, keepdims=True)

    row, vec = _row_spec(tr, Dm), _vec_spec(Dm)
    vshape = jax.ShapeDtypeStruct((1, Dm), F32)
    if has_prev:
        ins, in_specs = (x, dh, dxo, g, sc, y_prev, gate_prev), [row, row, row, vec, vec, row, vec]
        out_shape = (jax.ShapeDtypeStruct((T, Dm), F32), jax.ShapeDtypeStruct((T, Dm), BF16), vshape, vshape, vshape)
        out_specs = (row, row, vec, vec, vec)
    else:
        ins, in_specs = (x, dh, dxo, g, sc), [row, row, row, vec, vec]
        out_shape = (jax.ShapeDtypeStruct((T, Dm), F32), vshape, vshape)
        out_specs = (row, vec, vec)
    return pl.pallas_call(body, name=name, out_shape=out_shape, grid=(T // tr,), in_specs=in_specs,
                          out_specs=out_specs, compiler_params=_params(("arbitrary",)))(*ins)


def _loss_head(x, y, gate, fg, tgt, name):
    T, Dm = x.shape
    tr = _pick(T, (NORM_ROW_TILE, ROW_TILE))
    nsteps = T // tr

    def body(x_ref, y_ref, gate_ref, fg_ref, t_ref, dx_ref, dy_ref, loss_ref, sfg_ref, sg_ref, acc_ref):
        i = pl.program_id(0)

        @pl.when(i == 0)
        def _():
            acc_ref[...] = jnp.zeros_like(acc_ref)
            sfg_ref[...] = jnp.zeros_like(sfg_ref)
            sg_ref[...] = jnp.zeros_like(sg_ref)

        yv = y_ref[...]
        xv = x_ref[...] + gate_ref[...] * yv
        rstd = lax.rsqrt(jnp.mean(xv * xv, axis=-1, keepdims=True) + EPS)
        xhat = xv * rstd
        err = xhat * fg_ref[...] - t_ref[...]
        acc_ref[...] += jnp.sum(err * err, axis=0, keepdims=True)
        dyn = err * (1.0 / Dm)
        sfg_ref[...] += jnp.sum(dyn * xhat, axis=0, keepdims=True)
        dxhat = dyn * fg_ref[...]
        dx = rstd * (dxhat - xhat * jnp.mean(dxhat * xhat, axis=-1, keepdims=True))
        dx_ref[...] = dx
        dy_ref[...] = (gate_ref[...] * dx).astype(BF16)
        sg_ref[...] += jnp.sum(dx * yv, axis=0, keepdims=True)

        @pl.when(i == nsteps - 1)
        def _():
            total = jnp.sum(acc_ref[...], axis=1, keepdims=True) * (0.5 / Dm)
            loss_ref[...] = jnp.broadcast_to(total, loss_ref.shape)

    row, vec = _row_spec(tr, Dm), _vec_spec(Dm)
    vshape = jax.ShapeDtypeStruct((1, Dm), F32)
    return pl.pallas_call(
        body, name=name, grid=(nsteps,),
        out_shape=(jax.ShapeDtypeStruct((T, Dm), F32), jax.ShapeDtypeStruct((T, Dm), BF16),
                   jax.ShapeDtypeStruct((1, LANES), F32), vshape, vshape),
        in_specs=[row, row, vec, vec, row], out_specs=(row, row, _vec_spec(LANES), vec, vec),
        scratch_shapes=[pltpu.VMEM((1, Dm), F32)], compiler_params=_params(("arbitrary",)),
    )(x, y, gate, fg, tgt)


def _spatial_mask():
    r = lax.broadcasted_iota(jnp.int32, (GM_BLOCK, GM_BLOCK), 0) // CHUNK
    c = lax.broadcasted_iota(jnp.int32, (GM_BLOCK, GM_BLOCK), 1) // CHUNK
    return r >= c


def _gm_specs(tr):
    return [_row_spec(tr, 2 * GM_WIDTH), _vec_spec(GM_WIDTH), _vec_spec(GM_WIDTH),
            pl.BlockSpec((GM_HEADS, GM_BLOCK, GM_BLOCK), lambda i: (0, 0, 0)),
            pl.BlockSpec((GM_HEADS, GM_BLOCK, 1), lambda i: (0, 0, 0))]


def _gm_mid_fwd(a, ln_g, ln_b, ws, bs3, name):
    T = a.shape[0]
    tr = _pick(T, (ROW_TILE,))
    W = GM_WIDTH

    def body(a_ref, lg_ref, lb_ref, ws_ref, bs_ref, o_ref, vn_scr):
        av = a_ref[:, W:]
        v = av * _cdf(av)
        vc = v - jnp.mean(v, axis=-1, keepdims=True)
        rstd = lax.rsqrt(jnp.mean(vc * vc, axis=-1, keepdims=True) + EPS)
        vn_scr[...] = (vc * rstd * lg_ref[...] + lb_ref[...]).astype(BF16)
        mask = _spatial_mask()
        for h in range(GM_HEADS):
            w = jnp.where(mask, ws_ref[h], 0.0).astype(BF16)
            cs = slice(h * GM_HEAD_DIM, (h + 1) * GM_HEAD_DIM)
            for blk in range(tr // GM_BLOCK):
                rs = slice(blk * GM_BLOCK, (blk + 1) * GM_BLOCK)
                s = _dot(w, vn_scr[rs, cs], NN) + bs_ref[h]
                au = a_ref[rs, cs]
                o_ref[rs, cs] = (au * _cdf(au) * s).astype(BF16)

    return pl.pallas_call(
        body, name=name, out_shape=jax.ShapeDtypeStruct((T, W), BF16), grid=(T // tr,),
        in_specs=_gm_specs(tr), out_specs=_row_spec(tr, W),
        scratch_shapes=[pltpu.VMEM((tr, W), BF16)], compiler_params=_params(("parallel",)),
    )(a, ln_g, ln_b, ws, bs3)


def _gm_mid_bwd(a, dgated, ln_g, ln_b, ws, bs3, name):
    T = a.shape[0]
    tr = _pick(T, (ROW_TILE,))
    W = GM_WIDTH
    nsteps = T // tr

    def body(a_ref, dg_ref, lg_ref, lb_ref, ws_ref, bs_ref, da_ref, dws_ref, dbs_ref, dlg_ref, dlb_ref,
             vn_scr, vhat_scr, dvn_scr, dsum_scr):
        i = pl.program_id(0)

        @pl.when(i == 0)
        def _():
            dws_ref[...] = jnp.zeros_like(dws_ref)
            dbs_ref[...] = jnp.zeros_like(dbs_ref)
            dlg_ref[...] = jnp.zeros_like(dlg_ref)
            dlb_ref[...] = jnp.zeros_like(dlb_ref)
            dsum_scr[...] = jnp.zeros_like(dsum_scr)

        av = a_ref[:, W:]
        cdf_v = _cdf(av)
        v = av * cdf_v
        vc = v - jnp.mean(v, axis=-1, keepdims=True)
        rstd = lax.rsqrt(jnp.mean(vc * vc, axis=-1, keepdims=True) + EPS)
        vhat_scr[...] = vc * rstd
        vn_scr[...] = (vhat_scr[...] * lg_ref[...] + lb_ref[...]).astype(BF16)
        mask = _spatial_mask()
        for h in range(GM_HEADS):
            w = jnp.where(mask, ws_ref[h], 0.0).astype(BF16)
            cs = slice(h * GM_HEAD_DIM, (h + 1) * GM_HEAD_DIM)
            for blk in range(tr // GM_BLOCK):
                rs = slice(blk * GM_BLOCK, (blk + 1) * GM_BLOCK)
                vnb = vn_scr[rs, cs]
                s = _dot(w, vnb, NN) + bs_ref[h]
                au = a_ref[rs, cs]
                cdf_u = _cdf(au)
                dg = dg_ref[rs, cs]
                ds = dg * (au * cdf_u)
                da_ref[rs, cs] = (dg * s * (cdf_u + au * _pdf(au))).astype(BF16)
                dsb = ds.astype(BF16)
                dvn_scr[rs, cs] = _dot(w, dsb, TN)
                dws_ref[h] += _dot(dsb, vnb, NT)
                dsum_scr[:, cs] += ds
        dvn = dvn_scr[...]
        vhat = vhat_scr[...]
        dlg_ref[...] += jnp.sum(dvn * vhat, axis=0, keepdims=True)
        dlb_ref[...] += jnp.sum(dvn, axis=0, keepdims=True)
        dvh = dvn * lg_ref[...]
        dv = rstd * (dvh - jnp.mean(dvh, axis=-1, keepdims=True)
                     - vhat * jnp.mean(dvh * vhat, axis=-1, keepdims=True))
        da_ref[:, W:] = (dv * (cdf_v + av * _pdf(av))).astype(BF16)

        @pl.when(i == nsteps - 1)
        def _():
            for h in range(GM_HEADS):
                dws_ref[h] = jnp.where(mask, dws_ref[h], 0.0)
            col_head = lax.broadcasted_iota(jnp.int32, (W, GM_BLOCK), 0) // GM_HEAD_DIM
            sel = (col_head == lax.broadcasted_iota(jnp.int32, (W, GM_BLOCK), 1)).astype(F32)
            dbs_ref[...] = _dot(dsum_scr[...], sel, NN, HI)

    vshape = jax.ShapeDtypeStruct((1, W), F32)
    return pl.pallas_call(
        body, name=name, grid=(nsteps,),
        out_shape=(jax.ShapeDtypeStruct((T, 2 * W), BF16), jax.ShapeDtypeStruct((GM_HEADS, GM_BLOCK, GM_BLOCK), F32),
                   jax.ShapeDtypeStruct((GM_BLOCK, GM_BLOCK), F32), vshape, vshape),
        in_specs=[_gm_specs(tr)[0], _row_spec(tr, W)] + _gm_specs(tr)[1:],
        out_specs=(_row_spec(tr, 2 * W), pl.BlockSpec((GM_HEADS, GM_BLOCK, GM_BLOCK), lambda i: (0, 0, 0)),
                   pl.BlockSpec((GM_BLOCK, GM_BLOCK), lambda i: (0, 0)), _vec_spec(W), _vec_spec(W)),
        scratch_shapes=[pltpu.VMEM((tr, W), BF16), pltpu.VMEM((tr, W), F32), pltpu.VMEM((tr, W), F32),
                        pltpu.VMEM((GM_BLOCK, W), F32)],
        compiler_params=_params(("arbitrary",)),
    )(a, dgated, ln_g, ln_b, ws, bs3)


SUB = 16
EXP_CLAMP = 80.0


def _tri(lower):
    r = lax.broadcasted_iota(jnp.int32, (CHUNK, CHUNK), 0)
    c = lax.broadcasted_iota(jnp.int32, (CHUNK, CHUNK), 1)
    return (r >= c) if lower else (c >= r)


def _score_masks():
    i = lax.broadcasted_iota(jnp.int32, (CHUNK, CHUNK), 0)
    j = lax.broadcasted_iota(jnp.int32, (CHUNK, CHUNK), 1)
    bi, bj = i // SUB, j // SUB
    diag = (bi == bj) & (i >= j)
    pair = (bi % 2 == 1) & (bj == bi - 1)
    half = (i >= CHUNK // 2) & (j < CHUNK // 2)
    return diag, pair, half


def _dot01(m, x):
    x1 = x.astype(BF16)
    rest = x - x1.astype(F32)
    x2 = rest.astype(BF16)
    x3 = (rest - x2.astype(F32)).astype(BF16)
    return _dot(m, x1, NN) + (_dot(m, x2, NN) + _dot(m, x3, NN))


def _block_rows(b, offset):
    parts = []
    for blk in range(0, CHUNK, SUB):
        r = blk + offset
        parts.append(jnp.zeros((SUB, b.shape[1]), F32) if r < 0 else jnp.broadcast_to(b[r:r + 1], (SUB, b.shape[1])))
    return jnp.concatenate(parts, axis=0)


def _hg_gates(p_ref, lb_ref, lower):
    Dm = D_MODEL
    heads = []
    for h in range(HG_HEADS):
        c0 = h * HG_DIM
        qr = p_ref[:, c0:c0 + HG_DIM]
        fz = p_ref[:, Dm + c0:Dm + c0 + HG_DIM]
        lbh = lb_ref[:, c0:c0 + HG_DIM]
        sg = _sig(fz)
        f = lbh + (1.0 - lbh) * sg
        sq = _sig(qr)
        heads.append(dict(qr=qr, v=p_ref[:, 2 * Dm + c0:2 * Dm + c0 + HG_DIM],
                          gt=p_ref[:, 3 * Dm + c0:3 * Dm + c0 + HG_DIM], lbh=lbh, sg=sg, f=f, gl=jnp.log(f),
                          kk=1.0 - f, sq=sq, q=qr * sq))
    for g in heads:
        g["b"] = _dot01(lower, g.pop("gl"))
    for g in heads:
        g.update(_hg_scalings(g["q"], g["kk"], g.pop("b")))
    return heads


def _hg_scalings(q, kk, b):
    r_mid = _block_rows(b, SUB // 2 - 1)
    r_prev = _block_rows(b, -1)
    r_end = _block_rows(b, SUB - 1)
    r_half = jnp.broadcast_to(b[CHUNK // 2 - 1:CHUNK // 2], b.shape)
    bc = b[CHUNK - 1:CHUNK]
    eqs = (jnp.exp(jnp.clip(b - r_mid, -EXP_CLAMP, EXP_CLAMP)), jnp.exp(jnp.minimum(b - r_prev, 0.0)),
           jnp.exp(jnp.minimum(b - r_half, 0.0)))
    eks = (jnp.exp(jnp.clip(r_mid - b, -EXP_CLAMP, EXP_CLAMP)), jnp.exp(jnp.minimum(r_end - b, 0.0)),
           jnp.exp(jnp.minimum(r_half - b, 0.0)))
    eb = jnp.exp(b)
    ec = jnp.exp(bc - b)
    return dict(eqs=eqs, eks=eks, eb=eb, ec=ec, e_end=jnp.exp(bc), qs=[q * e for e in eqs],
                ks=[kk * e for e in eks], qe=q * eb, ke=kk * ec)


def _scores(g, masks):
    a = None
    for qs, ks, m in zip(g["qs"], g["ks"], masks):
        part = jnp.where(m, _dot(qs.astype(BF16), ks.astype(BF16), NT), 0.0)
        a = part if a is None else a + part
    return a


def _hg_scan_fwd(p, lb, gn, name):
    T = p.shape[0]
    nc = T // CHUNK
    Dm = D_MODEL

    def body(p_ref, lb_ref, gn_ref, o_ref, og_ref, so_ref, st_ref):
        @pl.when(pl.program_id(0) == 0)
        def _():
            st_ref[...] = jnp.zeros_like(st_ref)

        masks = _score_masks()
        heads = _hg_gates(p_ref, lb_ref, _tri(True).astype(BF16))
        states = [st_ref[h] for h in range(HG_HEADS)]
        scores = [_scores(g, masks) for g in heads]
        outs = [_dot(a.astype(BF16), g["v"].astype(BF16), NN) + _dot(g["qe"], st, NT, X3)
                for g, a, st in zip(heads, scores, states)]
        new_states = [st * g["e_end"] + _dot(g["v"], g["ke"], TN, X3) for g, st in zip(heads, states)]
        for h, (g, o, st, st2) in enumerate(zip(heads, outs, states, new_states)):
            cs = slice(h * HG_DIM, (h + 1) * HG_DIM)
            so_ref[0, h] = st
            st_ref[h] = st2
            o_ref[:, cs] = o
            r = lax.rsqrt(jnp.mean(o * o, axis=-1, keepdims=True) + EPS)
            gt = g["gt"]
            og_ref[:, cs] = (((o * r) * gn_ref[:, cs]).astype(F32) * (gt * _sig(gt))).astype(BF16)

    return pl.pallas_call(
        body, name=name, grid=(nc,),
        out_shape=(jax.ShapeDtypeStruct((T, Dm), F32), jax.ShapeDtypeStruct((T, Dm), BF16),
                   jax.ShapeDtypeStruct((nc, HG_HEADS, HG_DIM, HG_DIM), F32)),
        in_specs=[_row_spec(CHUNK, 4 * Dm), _vec_spec(Dm), _vec_spec(Dm)],
        out_specs=(_row_spec(CHUNK, Dm), _row_spec(CHUNK, Dm),
                   pl.BlockSpec((1, HG_HEADS, HG_DIM, HG_DIM), lambda i: (i, 0, 0, 0))),
        scratch_shapes=[pltpu.VMEM((HG_HEADS, HG_DIM, HG_DIM), F32)],
        compiler_params=_params(("arbitrary",)),
    )(p, lb, gn)


def _hg_scan_bwd(p, lb, gn, o, dog, states, name):
    T = p.shape[0]
    nc = T // CHUNK
    Dm = D_MODEL

    def rev(i):
        return nc - 1 - i

    def body(p_ref, lb_ref, gn_ref, o_ref, dog_ref, st_in_ref, dp_ref, dlb_ref, dgn_ref, dst_ref, carry_ref):
        @pl.when(pl.program_id(0) == 0)
        def _():
            dst_ref[...] = jnp.zeros_like(dst_ref)
            carry_ref[...] = jnp.zeros_like(carry_ref)
            dlb_ref[...] = jnp.zeros_like(dlb_ref)
            dgn_ref[...] = jnp.zeros_like(dgn_ref)

        upper = _tri(False).astype(BF16)
        masks = _score_masks()
        heads = _hg_gates(p_ref, lb_ref, _tri(True).astype(BF16))
        for h, g in enumerate(heads):
            cs = slice(h * HG_DIM, (h + 1) * HG_DIM)
            oh = o_ref[:, cs]
            r = lax.rsqrt(jnp.mean(oh * oh, axis=-1, keepdims=True) + EPS)
            on = oh * r
            gt = g["gt"]
            sgt = _sig(gt)
            sil = gt * sgt
            dogh = dog_ref[:, cs]
            gnh = gn_ref[:, cs]
            don = dogh * gnh * sil
            g["dgn"] = jnp.sum(dogh * on * sil, axis=0, keepdims=True)
            g["dgate"] = dogh * on * gnh * (sgt * (1.0 + gt * (1.0 - sgt)))
            g["do"] = r * (don - on * jnp.mean(don * on, axis=-1, keepdims=True))
            g["dst"] = dst_ref[h]
            g["st"] = st_in_ref[0, h]
            g["carry"] = carry_ref[h]
        for g in heads:
            g["a"] = _scores(g, masks)
            g["dob"] = g["do"].astype(BF16)
            g["da"] = _dot(g["dob"], g["v"].astype(BF16), NT)
        for g in heads:
            g["dv"] = _dot(g["a"].astype(BF16), g["dob"], TN) + _dot(g["ke"].astype(BF16), g["dst"].astype(BF16), NT)
            g["dq"] = _dot(g["do"], g["st"], NN, X3) * g["eb"]
            g["dk"] = _dot(g["v"], g["dst"], NN, X3) * g["ec"]
            g["dst2"] = g["dst"] * g["e_end"] + _dot(g["do"], g["qe"], TN, X3)
        for lvl in range(3):
            for g in heads:
                dam = jnp.where(masks[lvl], g["da"], 0.0)
                g["dq"] = g["dq"] + _dot(dam, g["ks"][lvl], NN, X3) * g["eqs"][lvl]
                g["dk"] = g["dk"] + _dot(dam, g["qs"][lvl], TN, X3) * g["eks"][lvl]
        for g in heads:
            g["dgd"] = g["q"] * g["dq"] - g["kk"] * g["dk"]
            g["dgl"] = _dot01(upper, g["dgd"]) + g["carry"]
        for h, g in enumerate(heads):
            c0 = h * HG_DIM
            cs = slice(c0, c0 + HG_DIM)
            df = g["dgl"] / g["f"] - g["dk"]
            sg, sq, qr = g["sg"], g["sq"], g["qr"]
            dst_ref[h] = g["dst2"]
            carry_ref[h] = g["carry"] + jnp.sum(g["dgd"], axis=0, keepdims=True)
            dgn_ref[:, cs] += g["dgn"]
            dlb_ref[:, cs] += jnp.sum(df * (1.0 - sg), axis=0, keepdims=True)
            dp_ref[:, c0:c0 + HG_DIM] = (g["dq"] * (sq * (1.0 + qr * (1.0 - sq)))).astype(BF16)
            dp_ref[:, Dm + c0:Dm + c0 + HG_DIM] = (df * (1.0 - g["lbh"]) * sg * (1.0 - sg)).astype(BF16)
            dp_ref[:, 2 * Dm + c0:2 * Dm + c0 + HG_DIM] = g["dv"].astype(BF16)
            dp_ref[:, 3 * Dm + c0:3 * Dm + c0 + HG_DIM] = g["dgate"].astype(BF16)

    vshape = jax.ShapeDtypeStruct((1, Dm), F32)
    rrow = lambda w: pl.BlockSpec((CHUNK, w), lambda i: (rev(i), 0))
    return pl.pallas_call(
        body, name=name, grid=(nc,),
        out_shape=(jax.ShapeDtypeStruct((T, 4 * Dm), BF16), vshape, vshape),
        in_specs=[rrow(4 * Dm), _vec_spec(Dm), _vec_spec(Dm), rrow(Dm), rrow(Dm),
                  pl.BlockSpec((1, HG_HEADS, HG_DIM, HG_DIM), lambda i: (rev(i), 0, 0, 0))],
        out_specs=(rrow(4 * Dm), _vec_spec(Dm), _vec_spec(Dm)),
        scratch_shapes=[pltpu.VMEM((HG_HEADS, HG_DIM, HG_DIM), F32), pltpu.VMEM((HG_HEADS, 1, HG_DIM), F32)],
        compiler_params=_params(("arbitrary",)),
    )(p, lb, gn, o, dog, states)


def _lb_fwd(hg_lb, name):
    def body(a_ref, o_ref):
        a0, a1 = a_ref[0:1], a_ref[1:2]
        m = jnp.maximum(a0, a1)
        e0, e1 = jnp.exp(a0 - m), jnp.exp(a1 - m)
        p0, p1 = e0 / (e0 + e1), e1 / (e0 + e1)
        o_ref[0:1] = p0 - p0
        o_ref[1:2] = (p0 + p1) - p0

    return pl.pallas_call(body, name=name, out_shape=jax.ShapeDtypeStruct(hg_lb.shape, F32))(hg_lb)


def _lb_bwd(hg_lb, dlb_all, name):
    def body(a_ref, d_ref, o_ref):
        a0, a1 = a_ref[0:1], a_ref[1:2]
        m = jnp.maximum(a0, a1)
        e0, e1 = jnp.exp(a0 - m), jnp.exp(a1 - m)
        p0, p1 = e0 / (e0 + e1), e1 / (e0 + e1)
        d1 = d_ref[1:2]
        o_ref[0:1] = -p0 * p1 * d1
        o_ref[1:2] = p1 * (1.0 - p1) * d1

    return pl.pallas_call(body, name=name, out_shape=jax.ShapeDtypeStruct(hg_lb.shape, F32))(hg_lb, dlb_all)


CONV_COLS_FWD = 256
CONV_COLS_BWD = 128
CONV_ROWS_BWD = 128
CONV_ROWS_FWD = 512


def _conv_fwd(a, w, b, name):
    T = a.shape[0]
    Fh = FFN_HIDDEN
    tr = _pick(T, (CONV_ROWS_FWD, ROW_TILE))
    cw = CONV_COLS_FWD
    hb = tr // 8

    def body(a_ref, ap_ref, w_ref, b_ref, m_ref):
        m0 = (pl.program_id(0) > 0).astype(F32)

        def conv(cc):
            x = jnp.concatenate([ap_ref[:, pl.ds(cc, cw)] * m0, a_ref[:, pl.ds(cc, cw)]], axis=0)
            wv = w_ref[:, pl.ds(cc, cw)]
            y = b_ref[:, pl.ds(cc, cw)] + wv[2:3] * x + wv[1:2] * pltpu.roll(x, 1, axis=0) \
                + wv[0:1] * pltpu.roll(x, 2, axis=0)
            return y[8:]

        def step(c, carry):
            c0 = pl.multiple_of(c * cw, cw)
            c1 = pl.multiple_of(Fh + c * cw, cw)
            yg, yv = conv(c0), conv(c1)
            m_ref[:, pl.ds(c0, cw)] = (yg * _cdf(yg) * yv).astype(BF16)
            return carry

        lax.fori_loop(0, Fh // cw, step, 0)

    return pl.pallas_call(
        body, name=name, out_shape=jax.ShapeDtypeStruct((T, Fh), BF16), grid=(T // tr,),
        in_specs=[_row_spec(tr, 2 * Fh), pl.BlockSpec((8, 2 * Fh), lambda i: (jnp.maximum(i * hb - 1, 0), 0)),
                  _vec_spec(2 * Fh, 3), _vec_spec(2 * Fh)],
        out_specs=_row_spec(tr, Fh), compiler_params=_params(("parallel",)),
    )(a, a, w, b)


def _conv_bwd(a, dm, w, b, name):
    T = a.shape[0]
    Fh = FFN_HIDDEN
    tr = _pick(T, (CONV_ROWS_BWD, ROW_TILE))
    cw = CONV_COLS_BWD
    hb = tr // 8
    nsteps = T // tr
    n = tr + 8

    def body(a_ref, ap_ref, an_ref, dm_ref, dmn_ref, w_ref, b_ref, da_ref, dw_ref, db_ref):
        i = pl.program_id(0)
        m0 = (i > 0).astype(F32)
        m1 = (i < nsteps - 1).astype(F32)

        @pl.when(i == 0)
        def _():
            dw_ref[...] = jnp.zeros_like(dw_ref)
            db_ref[...] = jnp.zeros_like(db_ref)

        def prep(cc):
            x = jnp.concatenate([ap_ref[:, pl.ds(cc, cw)] * m0, a_ref[:, pl.ds(cc, cw)],
                                 an_ref[:, pl.ds(cc, cw)] * m1], axis=0)
            wv = w_ref[:, pl.ds(cc, cw)]
            s1 = pltpu.roll(x, 1, axis=0)
            s2 = pltpu.roll(x, 2, axis=0)
            y = b_ref[:, pl.ds(cc, cw)] + wv[2:3] * x + wv[1:2] * s1 + wv[0:1] * s2
            return wv, x[8:], s1[8:], s2[8:], y[8:]

        def back(cc, dy, wv, x0, s1, s2):
            da = wv[2:3] * dy + wv[1:2] * pltpu.roll(dy, n - 1, axis=0) + wv[0:1] * pltpu.roll(dy, n - 2, axis=0)
            da_ref[:, pl.ds(cc, cw)] = da[:tr].astype(BF16)
            d = dy[:tr]
            db_ref[:, pl.ds(cc, cw)] += jnp.sum(d, axis=0, keepdims=True)
            dw_ref[2:3, pl.ds(cc, cw)] += jnp.sum(d * x0[:tr], axis=0, keepdims=True)
            dw_ref[1:2, pl.ds(cc, cw)] += jnp.sum(d * s1[:tr], axis=0, keepdims=True)
            dw_ref[0:1, pl.ds(cc, cw)] += jnp.sum(d * s2[:tr], axis=0, keepdims=True)

        def step(c, carry):
            c0 = pl.multiple_of(c * cw, cw)
            c1 = pl.multiple_of(Fh + c * cw, cw)
            dmx = jnp.concatenate([dm_ref[:, pl.ds(c0, cw)], dmn_ref[:, pl.ds(c0, cw)] * m1], axis=0)
            wg, xg, s1g, s2g, yg = prep(c0)
            wv, xv, s1v, s2v, yv = prep(c1)
            cg = _cdf(yg)
            back(c0, dmx * yv * (cg + yg * _pdf(yg)), wg, xg, s1g, s2g)
            back(c1, dmx * (yg * cg), wv, xv, s1v, s2v)
            return carry

        lax.fori_loop(0, Fh // cw, step, 0)

    prev = lambda wd: pl.BlockSpec((8, wd), lambda i: (jnp.maximum(i * hb - 1, 0), 0))
    nxt = lambda wd: pl.BlockSpec((8, wd), lambda i: (jnp.minimum((i + 1) * hb, T // 8 - 1), 0))
    return pl.pallas_call(
        body, name=name, grid=(nsteps,),
        out_shape=(jax.ShapeDtypeStruct((T, 2 * Fh), BF16), jax.ShapeDtypeStruct((3, 2 * Fh), F32),
                   jax.ShapeDtypeStruct((1, 2 * Fh), F32)),
        in_specs=[_row_spec(tr, 2 * Fh), prev(2 * Fh), nxt(2 * Fh), _row_spec(tr, Fh), nxt(Fh),
                  _vec_spec(2 * Fh, 3), _vec_spec(2 * Fh)],
        out_specs=(_row_spec(tr, 2 * Fh), _vec_spec(2 * Fh, 3), _vec_spec(2 * Fh)),
        compiler_params=_params(("arbitrary",)),
    )(a, a, a, dm, dm, w, b)


def _ada_fwd(c_all, ada_w, ada_b, name):
    L, Dm, cols = ada_w.shape
    tn = _pick(cols, (512, 256, 128))

    def body(c_ref, w_ref, b_ref, o_ref):
        cv = c_ref[...]
        cond = (cv * _sig(cv)).astype(BF16)
        o_ref[...] = _dot(cond, w_ref[...].astype(BF16), NN) + b_ref[...]

    return pl.pallas_call(
        body, name=name, out_shape=jax.ShapeDtypeStruct((L, N_DEV, cols), F32), grid=(L, cols // tn),
        in_specs=[pl.BlockSpec((N_DEV, Dm), lambda l, j: (0, 0)), pl.BlockSpec((None, Dm, tn), lambda l, j: (l, 0, j)),
                  pl.BlockSpec((None, 1, tn), lambda l, j: (l, 0, j))],
        out_specs=pl.BlockSpec((None, N_DEV, tn), lambda l, j: (l, 0, j)),
        compiler_params=_params(("parallel", "parallel")),
    )(c_all, ada_w, ada_b.reshape(L, 1, cols))


def _ada_bwd(c_all, dmod, name):
    L, _, cols = dmod.shape
    Dm = c_all.shape[1]
    tn = _pick(cols, (512, 256, 128))

    def body(c_ref, d_ref, o_ref):
        cv = c_ref[...]
        o_ref[...] = _dot(cv * _sig(cv), d_ref[...], TN, HI)

    return pl.pallas_call(
        body, name=name, out_shape=jax.ShapeDtypeStruct((L, Dm, cols), F32), grid=(L, cols // tn),
        in_specs=[pl.BlockSpec((N_DEV, Dm), lambda l, j: (0, 0)), pl.BlockSpec((None, N_DEV, tn), lambda l, j: (l, 0, j))],
        out_specs=pl.BlockSpec((None, Dm, tn), lambda l, j: (l, 0, j)),
        compiler_params=_params(("parallel", "parallel")),
    )(c_all, dmod)


def _add_own_halves(g4s, rbs, core, name):
    n = len(g4s)

    def body(core_ref, *refs):
        for g_ref, r_ref, o_ref in zip(refs[:n], refs[n:2 * n], refs[2 * n:]):
            o_ref[...] = (g_ref[...].astype(F32) + r_ref[...].astype(F32)).astype(GRAD_WIRE)

    shapes = [g.shape for g in g4s]
    return pl.pallas_call(
        body, name=name, out_shape=tuple(jax.ShapeDtypeStruct((S, rh, cw), GRAD_WIRE) for S, _, rh, cw in shapes),
        grid_spec=pltpu.PrefetchScalarGridSpec(
            num_scalar_prefetch=1, grid=(N_CHIPS,),
            in_specs=[pl.BlockSpec((None, None, rh, cw), lambda s, core_ref: (s, core_ref[0], 0, 0))
                      for _, _, rh, cw in shapes]
            + [pl.BlockSpec((None, rh, cw), lambda s, core_ref: (s, 0, 0)) for _, _, rh, cw in shapes],
            out_specs=tuple(pl.BlockSpec((None, rh, cw), lambda s, core_ref: (s, 0, 0)) for _, _, rh, cw in shapes)),
        compiler_params=_params(("parallel",)),
    )(core, *g4s, *rbs)


def _sum_chips(lands, sums, chip, name):
    L = len(lands)
    _, rh, cw = lands[0].shape
    tr = _pick(rh, (256, 128, 176, 64))

    def body(chip_ref, *refs):
        ld, cs, o_ref = refs[:L], refs[L:2 * L], refs[2 * L]
        me = chip_ref[0]
        for k in range(L):
            @pl.when(pl.program_id(0) == k)
            def _(k=k):
                own = cs[k][...].astype(F32)
                got = [ld[k][j].astype(F32) for j in range(3)]
                acc = None
                for t in range(N_CHIPS):
                    d = jnp.bitwise_xor(jnp.int32(t), me)
                    term = jnp.where(d == 0, own, jnp.where(d == 2, got[0], jnp.where(d == 1, got[1], got[2])))
                    acc = term if acc is None else acc + term
                o_ref[...] = acc

    frozen = lambda l, i, k: jnp.where(l == k, i, 0)
    in_specs = [pl.BlockSpec((3, tr, cw), lambda l, i, chip_ref, k=k: (0, frozen(l, i, k), 0)) for k in range(L)]
    in_specs += [pl.BlockSpec((None, tr, cw), lambda l, i, chip_ref, k=k: (chip_ref[0], frozen(l, i, k), 0))
                 for k in range(L)]
    return pl.pallas_call(
        body, name=name, out_shape=jax.ShapeDtypeStruct((L, rh, cw), F32),
        grid_spec=pltpu.PrefetchScalarGridSpec(
            num_scalar_prefetch=1, grid=(L, rh // tr), in_specs=in_specs,
            out_specs=pl.BlockSpec((None, tr, cw), lambda l, i, chip_ref: (l, i, 0))),
        compiler_params=_params(("arbitrary", "arbitrary")),
    )(chip, *lands, *sums)


def _sum_devices(gathered, name):
    n, R, _ = gathered.shape
    tr = _pick(R, (512, 448, 384, 256, 192, 128, 64, 32, 16, 8))

    def body(g_ref, o_ref):
        acc = g_ref[0]
        for d in range(1, n):
            acc = acc + g_ref[d]
        o_ref[...] = acc

    return pl.pallas_call(
        body, name=name, out_shape=jax.ShapeDtypeStruct((R, LANES), F32), grid=(R // tr,),
        in_specs=[pl.BlockSpec((n, tr, LANES), lambda i: (0, i, 0))], out_specs=pl.BlockSpec((tr, LANES), lambda i: (i, 0)),
        compiler_params=_params(("parallel",)),
    )(gathered)


def _adamw(w, g, m, v, name):
    R, C = w.shape
    tr = _pick(R, (256, 128, 64, 32, 16, 8))
    c1 = 1.0 / (1.0 - ADAM_B1 ** ADAM_STEP)
    c2 = 1.0 / (1.0 - ADAM_B2 ** ADAM_STEP)

    def body(w_ref, g_ref, m_ref, v_ref, d_ref, mo_ref, vo_ref):
        gv = g_ref[...]
        m2 = ADAM_B1 * m_ref[...] + (1.0 - ADAM_B1) * gv
        v2 = ADAM_B2 * v_ref[...] + (1.0 - ADAM_B2) * (gv * gv)
        mo_ref[...] = m2
        vo_ref[...] = v2
        d_ref[...] = -ADAM_LR * ((m2 * c1) / (jnp.sqrt(v2 * c2) + ADAM_EPS) + ADAM_WD * w_ref[...])

    spec = pl.BlockSpec((tr, C), lambda i: (i, 0))
    shp = jax.ShapeDtypeStruct((R, C), F32)
    return pl.pallas_call(body, name=name, out_shape=(shp, shp, shp), grid=(R // tr,), in_specs=[spec] * 4,
                          out_specs=(spec, spec, spec), compiler_params=_params(("parallel",)))(w, g, m, v)


def _adamw_halves(w, own, recv, m, v, core, name):
    L, rh, cw = own.shape
    tr = _pick(rh, (256, 128, 176, 64))
    c1 = 1.0 / (1.0 - ADAM_B1 ** ADAM_STEP)
    c2 = 1.0 / (1.0 - ADAM_B2 ** ADAM_STEP)

    def body(core_ref, w_ref, own_ref, recv_ref, m_ref, v_ref, g_ref, d_ref, mo_ref, vo_ref):
        gv = jnp.where(pl.program_id(1) == core_ref[0], own_ref[...], recv_ref[...])
        g_ref[...] = gv
        m2 = ADAM_B1 * m_ref[...] + (1.0 - ADAM_B1) * gv
        v2 = ADAM_B2 * v_ref[...] + (1.0 - ADAM_B2) * (gv * gv)
        mo_ref[...] = m2
        vo_ref[...] = v2
        d_ref[...] = -ADAM_LR * ((m2 * c1) / (jnp.sqrt(v2 * c2) + ADAM_EPS) + ADAM_WD * w_ref[...])

    full = pl.BlockSpec((None, None, tr, cw), lambda l, hf, i, core_ref: (l, hf, i, 0))
    mine = pl.BlockSpec((None, tr, cw), lambda l, hf, i, core_ref: (l, jnp.where(hf == core_ref[0], i, 0), 0))
    other = pl.BlockSpec((None, tr, cw), lambda l, hf, i, core_ref: (l, jnp.where(hf == core_ref[0], 0, i), 0))
    shp = jax.ShapeDtypeStruct((L, 2, rh, cw), F32)
    view = lambda a: a.reshape(L, 2, rh, cw)
    outs = pl.pallas_call(
        body, name=name, out_shape=(shp, shp, shp, shp),
        grid_spec=pltpu.PrefetchScalarGridSpec(
            num_scalar_prefetch=1, grid=(L, 2, rh // tr), in_specs=[full, mine, other, full, full],
            out_specs=(full, full, full, full)),
        compiler_params=_params(("arbitrary", "arbitrary", "arbitrary")),
    )(core, view(w), own, recv, view(m), view(v))
    return tuple(o.reshape(L, 2 * rh, cw) for o in outs)


ANY = pl.BlockSpec(memory_space=pl.ANY)


def _position():
    x, y, c = lax.axis_index("x"), lax.axis_index("y"), lax.axis_index("c")
    return x, y, c


def _allgather(ins, out_shapes, src_fns, dst_fns, name):
    n = len(ins)

    def body(*refs):
        in_refs, out_refs = refs[:n], refs[n:2 * n]
        send_sems, recv_sems, local_sems = refs[2 * n:]
        x, y, c = _position()
        me, sibling = (x, y, c), (x, y, 1 - c)
        chips = [(1 - x, y), (x, 1 - y), (1 - x, 1 - y)]

        def copy(k, j, block, to, own=False):
            dst = dst_fns[k](out_refs[k], *block)
            return pltpu.make_async_remote_copy(
                src_ref=src_fns[k](in_refs[k], c) if own else dst, dst_ref=dst,
                send_sem=send_sems.at[k, j], recv_sem=recv_sems.at[k, j], device_id=to, device_id_type=MESH)

        mine = [pltpu.make_async_copy(src_fns[k](in_refs[k], c), dst_fns[k](out_refs[k], *me), local_sems.at[k])
                for k in range(n)]
        for cp in mine:
            cp.start()
        first = []
        for k in range(n):
            first.append(copy(k, 0, me, sibling, own=True))
            first += [copy(k, 1 + j, me, (*chip, c), own=True) for j, chip in enumerate(chips)]
        for cp in first:
            cp.start()
        passed = []
        for j, chip in enumerate(chips):
            for k in range(n):
                copy(k, 1 + j, (*chip, c), me).wait_recv()
                fwd = copy(k, 4 + j, (*chip, c), sibling)
                fwd.start()
                passed.append(fwd)
        for k in range(n):
            copy(k, 0, sibling, me).wait_recv()
        for j, chip in enumerate(chips):
            for k in range(n):
                copy(k, 4 + j, (*chip, 1 - c), me).wait_recv()
        for cp in first + passed:
            cp.wait_send()
        for cp in mine:
            cp.wait()

    spec = pl.BlockSpec(memory_space=pltpu.VMEM)
    return pl.pallas_call(
        body, name=name, out_shape=tuple(out_shapes), in_specs=[spec] * n, out_specs=tuple([spec] * n),
        scratch_shapes=[pltpu.SemaphoreType.DMA((n, 7)), pltpu.SemaphoreType.DMA((n, 7)),
                        pltpu.SemaphoreType.DMA((n,))],
        compiler_params=pltpu.CompilerParams(vmem_limit_bytes=VMEM_LIMIT_BYTES),
    )(*ins)


def _allgather_small(payload, name):
    R = payload.shape[0]
    (out,) = _allgather(
        [payload], [jax.ShapeDtypeStruct((N_DEV, R, LANES), F32)],
        [lambda ref, c: ref], [lambda ref, px, py, pc: ref.at[4 * px + 2 * py + pc]], name)
    return out


def _swap_sibling(ins, name, other_half=False):
    n = len(ins)

    def body(*refs):
        in_refs, out_refs = refs[:n], refs[n:2 * n]
        send_sems, recv_sems = refs[2 * n:]
        x, y, c = _position()
        copies = [pltpu.make_async_remote_copy(
            src_ref=in_refs[k].at[:, 1 - c] if other_half else in_refs[k], dst_ref=out_refs[k],
            send_sem=send_sems.at[k], recv_sem=recv_sems.at[k],
            device_id=(x, y, 1 - c), device_id_type=MESH) for k in range(n)]
        for cp in copies:
            cp.start()
        for cp in copies:
            cp.wait_recv()
        for cp in copies:
            cp.wait_send()

    shape = lambda a: (a.shape[0],) + a.shape[2:] if other_half else a.shape
    return pl.pallas_call(
        body, name=name, out_shape=tuple(jax.ShapeDtypeStruct(shape(a), a.dtype) for a in ins),
        in_specs=[ANY] * n, out_specs=tuple([ANY] * n),
        scratch_shapes=[pltpu.SemaphoreType.DMA((n,)), pltpu.SemaphoreType.DMA((n,))],
        compiler_params=pltpu.CompilerParams(vmem_limit_bytes=VMEM_LIMIT_BYTES),
    )(*ins)


HBM_SPEC = pl.BlockSpec(memory_space=pltpu.HBM)
SEM_SPEC = pl.BlockSpec(memory_space=pltpu.SEMAPHORE)
SPLIT_PARAMS = pltpu.CompilerParams(has_side_effects=pltpu.SideEffectType.DATAFLOW_SIDE_EFFECTING)
TOKEN = jax.ShapeDtypeStruct((8, LANES), F32)


def _hbm(a):
    return pltpu.with_memory_space_constraint(a, pltpu.HBM)


def _weight_window(ref, col, r, cw, px, py, pc):
    rh = r // 2
    if col:
        return ref.at[pl.ds(pc * rh, rh), pl.ds((2 * px + py) * cw, cw)]
    return ref.at[pl.ds((2 * px + py) * r + pc * rh, rh), :]


def _peers(x, y, c):
    return [(x, y, 1 - c), (1 - x, y, c), (x, 1 - y, c), (1 - x, 1 - y, c)]


def _place_own(shards, cols, pos, name):
    n = len(shards)

    def body(pos_ref, *refs):
        for x_ref, o_ref in zip(refs[:n], refs[n:]):
            o_ref[...] = x_ref[...]

    in_specs, out_specs, out_shape = [], [], []
    for sh, col in zip(shards, cols):
        r, cw = sh.shape
        rh = r // 2
        in_specs.append(pl.BlockSpec((rh, cw), lambda i, pos_ref: (pos_ref[1], 0)))
        if col:
            out_specs.append(pl.BlockSpec((rh, cw), lambda i, pos_ref: (pos_ref[1], pos_ref[0])))
            out_shape.append(jax.ShapeDtypeStruct((r, N_CHIPS * cw), sh.dtype))
        else:
            out_specs.append(pl.BlockSpec((rh, cw), lambda i, pos_ref: (2 * pos_ref[0] + pos_ref[1], 0)))
            out_shape.append(jax.ShapeDtypeStruct((N_CHIPS * r, cw), sh.dtype))
    return pl.pallas_call(
        body, name=name, out_shape=tuple(out_shape),
        grid_spec=pltpu.PrefetchScalarGridSpec(num_scalar_prefetch=1, grid=(1,), in_specs=in_specs,
                                               out_specs=tuple(out_specs)),
        compiler_params=_params(("arbitrary",)),
    )(pos, *shards)


def _gather_start(shards, lands, cols, per_layer, name):
    n = len(shards)
    nl = n // per_layer

    def body(*refs):
        sh, ld = refs[:n], refs[n:2 * n]
        sems, token = refs[2 * n:2 * n + 2 * nl], refs[-1]
        x, y, c = _position()
        for k in range(n):
            l, a = divmod(k, per_layer)
            r, cw = shards[k].shape
            src = sh[k].at[pl.ds(c * (r // 2), r // 2), :]
            dst = _weight_window(ld[k], cols[k], r, cw, x, y, c)
            for j, peer in enumerate(_peers(x, y, c)):
                pltpu.make_async_remote_copy(src_ref=src, dst_ref=dst, send_sem=sems[2 * l].at[4 * a + j],
                                             recv_sem=sems[2 * l + 1].at[4 * a + j], device_id=peer,
                                             device_id_type=MESH).start()
        token[...] = jnp.zeros_like(token)

    arrs = list(shards) + list(lands)
    out = pl.pallas_call(
        body, name=name,
        out_shape=tuple(pltpu.SemaphoreType.DMA((per_layer * 4,)) for _ in range(2 * nl))
        + tuple(pltpu.HBM(a.shape, a.dtype) for a in arrs) + (TOKEN,),
        in_specs=[HBM_SPEC] * (2 * n),
        out_specs=(SEM_SPEC,) * (2 * nl) + (HBM_SPEC,) * (2 * n) + (pl.BlockSpec(memory_space=pltpu.VMEM),),
        input_output_aliases={i: 2 * nl + i for i in range(2 * n)}, compiler_params=SPLIT_PARAMS,
    )(*[_hbm(a) for a in arrs])
    return out[:2 * nl], out[2 * nl:2 * nl + n], out[2 * nl + n:2 * nl + 2 * n], out[-1]


def _gather_wait(shards, lands, send, recv, after, cols, first, name):
    m = len(shards)

    def body(*refs):
        sh, ld = refs[:m], refs[m:2 * m]
        send_ref, recv_ref = refs[2 * m], refs[2 * m + 1]
        x, y, c = _position()
        for a in range(m):
            r, cw = shards[a].shape
            src = sh[a].at[pl.ds(c * (r // 2), r // 2), :]
            for j, (px, py, pc) in enumerate(_peers(x, y, c)):
                cp = pltpu.make_async_remote_copy(
                    src_ref=src, dst_ref=_weight_window(ld[a], cols[a], r, cw, px, py, pc),
                    send_sem=send_ref.at[4 * (first + a) + j], recv_sem=recv_ref.at[4 * (first + a) + j],
                    device_id=(px, py, pc),
                    device_id_type=MESH)
                cp.wait_send()
                cp.wait_recv()

    arrs = list(shards) + list(lands)
    out = pl.pallas_call(
        body, name=name, out_shape=tuple(pltpu.HBM(a.shape, a.dtype) for a in arrs),
        in_specs=[HBM_SPEC] * (2 * m) + [SEM_SPEC, SEM_SPEC, ANY], out_specs=(HBM_SPEC,) * (2 * m),
        input_output_aliases={i: i for i in range(2 * m)}, compiler_params=SPLIT_PARAMS,
    )(*arrs, send, recv, after)
    return out[m:]


def _forward_sibling(lands, cols, shard_shapes, name):
    m = len(lands)

    def body(*refs):
        ins, outs = refs[:m], refs[m:2 * m]
        send_sems, recv_sems = refs[2 * m:]
        x, y, c = _position()
        chips = [(1 - x, y), (x, 1 - y), (1 - x, 1 - y)]
        sends = []
        for a in range(m):
            r, cw = shard_shapes[a]
            for j, (px, py) in enumerate(chips):
                cp = pltpu.make_async_remote_copy(
                    src_ref=_weight_window(ins[a], cols[a], r, cw, px, py, c),
                    dst_ref=_weight_window(outs[a], cols[a], r, cw, px, py, c),
                    send_sem=send_sems.at[a, j], recv_sem=recv_sems.at[a, j], device_id=(x, y, 1 - c),
                    device_id_type=MESH)
                cp.start()
                sends.append(cp)
        for a in range(m):
            r, cw = shard_shapes[a]
            for j, (px, py) in enumerate(chips):
                pltpu.make_async_remote_copy(
                    src_ref=_weight_window(ins[a], cols[a], r, cw, px, py, c),
                    dst_ref=_weight_window(outs[a], cols[a], r, cw, px, py, 1 - c),
                    send_sem=send_sems.at[a, j], recv_sem=recv_sems.at[a, j], device_id=(x, y, 1 - c),
                    device_id_type=MESH).wait_recv()
        for cp in sends:
            cp.wait_send()

    return pl.pallas_call(
        body, name=name, out_shape=tuple(jax.ShapeDtypeStruct(a.shape, a.dtype) for a in lands),
        in_specs=[ANY] * m, out_specs=tuple([ANY] * m), input_output_aliases={i: i for i in range(m)},
        scratch_shapes=[pltpu.SemaphoreType.DMA((m, 3)), pltpu.SemaphoreType.DMA((m, 3))],
        compiler_params=pltpu.CompilerParams(vmem_limit_bytes=VMEM_LIMIT_BYTES),
    )(*lands)


def _swap_start(ins, name, other_half=False):
    n = len(ins)
    shape = lambda a: (a.shape[0],) + a.shape[2:] if other_half else a.shape
    lands = [lax.empty(shape(a), a.dtype) for a in ins]

    def body(*refs):
        src, ld = refs[:n], refs[n:2 * n]
        send_ref, recv_ref, token = refs[2 * n], refs[2 * n + 1], refs[-1]
        x, y, c = _position()
        for k in range(n):
            pltpu.make_async_remote_copy(
                src_ref=src[k].at[:, 1 - c] if other_half else src[k], dst_ref=ld[k], send_sem=send_ref.at[k],
                recv_sem=recv_ref.at[k], device_id=(x, y, 1 - c), device_id_type=MESH).start()
        token[...] = jnp.zeros_like(token)

    arrs = list(ins) + lands
    out = pl.pallas_call(
        body, name=name,
        out_shape=(pltpu.SemaphoreType.DMA((n,)), pltpu.SemaphoreType.DMA((n,)))
        + tuple(pltpu.HBM(a.shape, a.dtype) for a in arrs) + (TOKEN,),
        in_specs=[HBM_SPEC] * (2 * n),
        out_specs=(SEM_SPEC, SEM_SPEC) + (HBM_SPEC,) * (2 * n) + (pl.BlockSpec(memory_space=pltpu.VMEM),),
        input_output_aliases={i: 2 + i for i in range(2 * n)}, compiler_params=SPLIT_PARAMS,
    )(*[_hbm(a) for a in arrs])
    return out[0], out[1], out[2:2 + n], out[2 + n:2 + 2 * n], out[-1]


def _swap_wait(ins, lands, send, recv, after, name, other_half=False, first=0):
    n = len(ins)

    def body(*refs):
        src, ld = refs[:n], refs[n:2 * n]
        send_ref, recv_ref = refs[2 * n], refs[2 * n + 1]
        x, y, c = _position()
        for k in range(n):
            cp = pltpu.make_async_remote_copy(
                src_ref=src[k].at[:, 1 - c] if other_half else src[k], dst_ref=ld[k], send_sem=send_ref.at[first + k],
                recv_sem=recv_ref.at[first + k], device_id=(x, y, 1 - c), device_id_type=MESH)
            cp.wait_send()
            cp.wait_recv()

    arrs = list(ins) + list(lands)
    out = pl.pallas_call(
        body, name=name, out_shape=tuple(pltpu.HBM(a.shape, a.dtype) for a in arrs),
        in_specs=[HBM_SPEC] * (2 * n) + [SEM_SPEC, SEM_SPEC, ANY], out_specs=(HBM_SPEC,) * (2 * n),
        input_output_aliases={i: i for i in range(2 * n)}, compiler_params=SPLIT_PARAMS,
    )(*arrs, send, recv, after)
    return out[:n], out[n:]


def _forward_start(lands, cols, shard_shapes, name):
    m = len(lands)

    def body(*refs):
        ld = refs[:m]
        send_ref, recv_ref, token = refs[m], refs[m + 1], refs[-1]
        x, y, c = _position()
        for a in range(m):
            r, cw = shard_shapes[a]
            for j, (px, py) in enumerate([(1 - x, y), (x, 1 - y), (1 - x, 1 - y)]):
                win = _weight_window(ld[a], cols[a], r, cw, px, py, c)
                pltpu.make_async_remote_copy(src_ref=win, dst_ref=win, send_sem=send_ref.at[3 * a + j],
                                             recv_sem=recv_ref.at[3 * a + j], device_id=(x, y, 1 - c),
                                             device_id_type=MESH).start()
        token[...] = jnp.zeros_like(token)

    out = pl.pallas_call(
        body, name=name,
        out_shape=(pltpu.SemaphoreType.DMA((m * 3,)), pltpu.SemaphoreType.DMA((m * 3,)))
        + tuple(pltpu.HBM(a.shape, a.dtype) for a in lands) + (TOKEN,),
        in_specs=[HBM_SPEC] * m,
        out_specs=(SEM_SPEC, SEM_SPEC) + (HBM_SPEC,) * m + (pl.BlockSpec(memory_space=pltpu.VMEM),),
        input_output_aliases={i: 2 + i for i in range(m)}, compiler_params=SPLIT_PARAMS,
    )(*[_hbm(a) for a in lands])
    return out[0], out[1], out[2:2 + m], out[-1]


def _forward_wait(lands, send, recv, after, cols, shard_shapes, name):
    m = len(lands)

    def body(*refs):
        ld = refs[:m]
        send_ref, recv_ref = refs[m], refs[m + 1]
        x, y, c = _position()
        for a in range(m):
            r, cw = shard_shapes[a]
            for j, (px, py) in enumerate([(1 - x, y), (x, 1 - y), (1 - x, 1 - y)]):
                cp = pltpu.make_async_remote_copy(
                    src_ref=_weight_window(ld[a], cols[a], r, cw, px, py, c),
                    dst_ref=_weight_window(ld[a], cols[a], r, cw, px, py, 1 - c),
                    send_sem=send_ref.at[3 * a + j], recv_sem=recv_ref.at[3 * a + j], device_id=(x, y, 1 - c),
                    device_id_type=MESH)
                cp.wait_send()
                cp.wait_recv()

    return pl.pallas_call(
        body, name=name, out_shape=tuple(pltpu.HBM(a.shape, a.dtype) for a in lands),
        in_specs=[HBM_SPEC] * m + [SEM_SPEC, SEM_SPEC, ANY], out_specs=(HBM_SPEC,) * m,
        input_output_aliases={i: i for i in range(m)}, compiler_params=SPLIT_PARAMS,
    )(*lands, send, recv, after)


def _exchange_start(sums, name):
    m = len(sums)
    lands = [lax.empty((3,) + s.shape[1:], s.dtype) for s in sums]

    def body(*refs):
        cs, ld = refs[:m], refs[m:2 * m]
        send_ref, recv_ref, token = refs[2 * m], refs[2 * m + 1], refs[-1]
        x, y, c = _position()
        for a in range(m):
            for j, (px, py) in enumerate([(1 - x, y), (x, 1 - y), (1 - x, 1 - y)]):
                pltpu.make_async_remote_copy(
                    src_ref=cs[a].at[2 * px + py], dst_ref=ld[a].at[j], send_sem=send_ref.at[3 * a + j],
                    recv_sem=recv_ref.at[3 * a + j], device_id=(px, py, c), device_id_type=MESH).start()
        token[...] = jnp.zeros_like(token)

    arrs = list(sums) + lands
    out = pl.pallas_call(
        body, name=name,
        out_shape=(pltpu.SemaphoreType.DMA((m * 3,)), pltpu.SemaphoreType.DMA((m * 3,)))
        + tuple(pltpu.HBM(a.shape, a.dtype) for a in arrs) + (TOKEN,),
        in_specs=[HBM_SPEC] * (2 * m),
        out_specs=(SEM_SPEC, SEM_SPEC) + (HBM_SPEC,) * (2 * m) + (pl.BlockSpec(memory_space=pltpu.VMEM),),
        input_output_aliases={i: 2 + i for i in range(2 * m)}, compiler_params=SPLIT_PARAMS,
    )(*[_hbm(a) for a in arrs])
    return out[0], out[1], out[2:2 + m], out[2 + m:2 + 2 * m], out[-1]


def _exchange_wait(sums, lands, send, recv, after, name):
    m = len(sums)

    def body(*refs):
        cs, ld = refs[:m], refs[m:2 * m]
        send_ref, recv_ref = refs[2 * m], refs[2 * m + 1]
        x, y, c = _position()
        for a in range(m):
            for j, (px, py) in enumerate([(1 - x, y), (x, 1 - y), (1 - x, 1 - y)]):
                cp = pltpu.make_async_remote_copy(
                    src_ref=cs[a].at[2 * px + py], dst_ref=ld[a].at[j], send_sem=send_ref.at[3 * a + j],
                    recv_sem=recv_ref.at[3 * a + j], device_id=(px, py, c), device_id_type=MESH)
                cp.wait_send()
                cp.wait_recv()

    arrs = list(sums) + list(lands)
    out = pl.pallas_call(
        body, name=name, out_shape=tuple(pltpu.HBM(a.shape, a.dtype) for a in arrs),
        in_specs=[HBM_SPEC] * (2 * m) + [SEM_SPEC, SEM_SPEC, ANY], out_specs=(HBM_SPEC,) * (2 * m),
        input_output_aliases={i: i for i in range(2 * m)}, compiler_params=SPLIT_PARAMS,
    )(*arrs, send, recv, after)
    return out[:m], out[m:]


def _place_row(payload, dev, name):
    R = payload.shape[0]
    tr = _pick(R, (512, 448, 384, 256, 192, 128, 64, 32, 16, 8))

    def body(dev_ref, x_ref, o_ref):
        o_ref[...] = x_ref[...]

    return pl.pallas_call(
        body, name=name, out_shape=jax.ShapeDtypeStruct((N_DEV, R, LANES), payload.dtype),
        grid_spec=pltpu.PrefetchScalarGridSpec(
            num_scalar_prefetch=1, grid=(R // tr,),
            in_specs=[pl.BlockSpec((tr, LANES), lambda i, dev_ref: (i, 0))],
            out_specs=pl.BlockSpec((None, tr, LANES), lambda i, dev_ref: (dev_ref[0], i, 0))),
        compiler_params=_params(("arbitrary",)),
    )(dev, payload)


def _others(x, y, c):
    return [(1 - x if fx else x, 1 - y if fy else y, 1 - c if fc else c)
            for fx in (0, 1) for fy in (0, 1) for fc in (0, 1) if fx or fy or fc]


def _broadcast_start(payload, land, name):
    def body(p_ref, l_ref, send_ref, recv_ref, p_thru, l_thru, token):
        x, y, c = _position()
        for j, peer in enumerate(_others(x, y, c)):
            pltpu.make_async_remote_copy(src_ref=p_ref, dst_ref=l_ref.at[4 * x + 2 * y + c], send_sem=send_ref.at[j],
                                         recv_sem=recv_ref.at[j], device_id=peer, device_id_type=MESH).start()
        token[...] = jnp.zeros_like(token)

    n = N_DEV - 1
    return pl.pallas_call(
        body, name=name,
        out_shape=(pltpu.SemaphoreType.DMA((n,)), pltpu.SemaphoreType.DMA((n,)), pltpu.HBM(payload.shape, payload.dtype),
                   pltpu.HBM(land.shape, land.dtype), TOKEN),
        in_specs=[HBM_SPEC, HBM_SPEC],
        out_specs=(SEM_SPEC, SEM_SPEC, HBM_SPEC, HBM_SPEC, pl.BlockSpec(memory_space=pltpu.VMEM)),
        input_output_aliases={0: 2, 1: 3}, compiler_params=SPLIT_PARAMS,
    )(_hbm(payload), _hbm(land))


def _broadcast_wait(payload, land, send, recv, after, name):
    def body(p_ref, l_ref, send_ref, recv_ref, after_ref, p_thru, l_thru):
        x, y, c = _position()
        for j, (px, py, pc) in enumerate(_others(x, y, c)):
            cp = pltpu.make_async_remote_copy(src_ref=p_ref, dst_ref=l_ref.at[4 * px + 2 * py + pc],
                                              send_sem=send_ref.at[j], recv_sem=recv_ref.at[j],
                                              device_id=(px, py, pc), device_id_type=MESH)
            cp.wait_send()
            cp.wait_recv()

    out = pl.pallas_call(
        body, name=name, out_shape=(pltpu.HBM(payload.shape, payload.dtype), pltpu.HBM(land.shape, land.dtype)),
        in_specs=[HBM_SPEC, HBM_SPEC, SEM_SPEC, SEM_SPEC, ANY], out_specs=(HBM_SPEC, HBM_SPEC),
        input_output_aliases={0: 0, 1: 1}, compiler_params=SPLIT_PARAMS,
    )(payload, land, send, recv, after)
    return out[1]


def _vec(a):
    return a.reshape(1, -1)


def _local_step(x, tgt, mod, W, P, get_w=None, on_grads=None):
    Dm = D_MODEL
    G = {k: [] for k in ("gm_w_in", "gm_w_out", "hg_w_in", "hg_w_out", "ffn_w_up", "ffn_w_down")}
    lb_all = _lb_fwd(P["hg_lb"], "lb_fwd")
    saved = []
    xs = x
    y_prev = gate_prev = None
    layer_w = [None] * DEPTH

    def wmm(xa, kind, i, mode, name):
        if layer_w[i] is not None:
            return _mm(xa, layer_w[i][kind], mode, name)
        return _mm(xa, W[kind], mode, name, b_layer=i if kind.startswith("ffn") else i // 2)

    for i in range(DEPTH):
        m = [_vec(mod[i, j * Dm:(j + 1) * Dm]) for j in range(6)]
        sh1, sc1, g1, sh2, sc2, g2 = m
        j = i // 2
        if get_w is not None:
            layer_w[i] = get_w(i, 0, xs if y_prev is None else y_prev)
        xs, h = _norm_fwd(xs, y_prev, gate_prev, _vec(P["norm_g"][i, 0]), sc1, sh1, f"norm_fwd_a{i}")
        rec = dict(x1=xs, h1=h)
        if i % 2 == 0:
            a = wmm(h, "gm_w_in", i, "nn", f"gm_in{i}")
            gated = _gm_mid_fwd(a, _vec(P["gm_ln_g"][j]), _vec(P["gm_ln_b"][j]), P["gm_w_s"][j],
                                P["gm_b_s"][j].reshape(GM_HEADS, GM_BLOCK, 1), f"gm_mid_fwd{i}")
            if get_w is not None:
                layer_w[i].update(get_w(i, 1, gated))
            y1 = wmm(gated, "gm_w_out", i, "nn", f"gm_out{i}")
            rec.update(a=a, act=gated)
        else:
            p = wmm(h, "hg_w_in", i, "nn", f"hg_in{i}")
            o, og, states = _hg_scan_fwd(p, _vec(lb_all[j]), _vec(P["hg_gn_g"][j]), f"hg_scan_fwd{i}")
            if get_w is not None:
                layer_w[i].update(get_w(i, 1, og))
            y1 = wmm(og, "hg_w_out", i, "nn", f"hg_out{i}")
            rec.update(a=p, act=og, o=o, states=states)
        rec["y1"] = y1
        xs, h2 = _norm_fwd(xs, y1, g1, _vec(P["norm_g"][i, 1]), sc2, sh2, f"norm_fwd_b{i}")
        a2 = wmm(h2, "ffn_w_up", i, "nn", f"ffn_up{i}")
        mm_ = _conv_fwd(a2, P["ffn_conv_w"][i], _vec(P["ffn_conv_b"][i]), f"conv_fwd{i}")
        y2 = wmm(mm_, "ffn_w_down", i, "nn", f"ffn_down{i}")
        rec.update(x2=xs, h2=h2, a2=a2, m=mm_, y2=y2, mods=m)
        saved.append(rec)
        y_prev, gate_prev = y2, g2
    dx, dy, loss, s_fg, s_gate = _loss_head(xs, y_prev, gate_prev, _vec(P["final_g"]), tgt, "loss_head")
    small = dict(final_g=s_fg, norm_g=[None] * DEPTH, dmod=[None] * DEPTH, ffn_conv_w=[None] * DEPTH,
                 ffn_conv_b=[None] * DEPTH, gm_ln_g=[None] * 2, gm_ln_b=[None] * 2, gm_w_s=[None] * 2,
                 gm_b_s=[None] * 2, hg_gn_g=[None] * 2, dlb=[None] * 2)
    for i in reversed(range(DEPTH)):
        rec = saved[i]
        sh1, sc1, g1, sh2, sc2, g2 = rec["mods"]
        j = i // 2
        d_g2 = s_gate
        dm = wmm(dy, "ffn_w_down", i, "nt", f"ffn_down_dx{i}")
        G["ffn_w_down"].append(_mm(rec["m"], dy, "tn", f"ffn_down_dw{i}", out_dtype=GRAD_WIRE))
        da2, dcw, dcb = _conv_bwd(rec["a2"], dm, P["ffn_conv_w"][i], _vec(P["ffn_conv_b"][i]), f"conv_bwd{i}")
        small["ffn_conv_w"][i], small["ffn_conv_b"][i] = dcw, dcb
        dh2 = wmm(da2, "ffn_w_up", i, "nt", f"ffn_up_dx{i}")
        G["ffn_w_up"].append(_mm(rec["h2"], da2, "tn", f"ffn_up_dw{i}", out_dtype=GRAD_WIRE, exchange_out=True))
        ng2 = _vec(P["norm_g"][i, 1])
        if on_grads is not None:
            ng2 = ng2 + on_grads(i, {k: G[k][-1] for k in ("ffn_w_up", "ffn_w_down")})
        dx, dy, s_sh2, s_x2, d_g1 = _norm_bwd(rec["x2"], dh2, dx, ng2, sc2, rec["y1"], g1, f"norm_bwd_b{i}")
        d_sc2, d_ng2 = s_x2 * ng2, s_x2 * (1.0 + sc2)
        if i % 2 == 0:
            dgated = wmm(dy, "gm_w_out", i, "nt", f"gm_out_dx{i}")
            G["gm_w_out"].append(_mm(rec["act"], dy, "tn", f"gm_out_dw{i}", out_dtype=GRAD_WIRE))
            da, dws, dbs, dlg, dlbeta = _gm_mid_bwd(
                rec["a"], dgated, _vec(P["gm_ln_g"][j]), _vec(P["gm_ln_b"][j]), P["gm_w_s"][j],
                P["gm_b_s"][j].reshape(GM_HEADS, GM_BLOCK, 1), f"gm_mid_bwd{i}")
            small["gm_w_s"][j], small["gm_b_s"][j] = dws, dbs[:, :GM_HEADS].T
            small["gm_ln_g"][j], small["gm_ln_b"][j] = dlg, dlbeta
            dh1 = wmm(da, "gm_w_in", i, "nt", f"gm_in_dx{i}")
            G["gm_w_in"].append(_mm(rec["h1"], da, "tn", f"gm_in_dw{i}", out_dtype=GRAD_WIRE, exchange_out=True))
        else:
            dog = wmm(dy, "hg_w_out", i, "nt", f"hg_out_dx{i}")
            G["hg_w_out"].append(_mm(rec["act"], dy, "tn", f"hg_out_dw{i}", out_dtype=GRAD_WIRE))
            dp, dlb, dgn = _hg_scan_bwd(rec["a"], _vec(lb_all[j]), _vec(P["hg_gn_g"][j]), rec["o"], dog,
                                        rec["states"], f"hg_scan_bwd{i}")
            small["dlb"][j], small["hg_gn_g"][j] = dlb, dgn
            dh1 = wmm(dp, "hg_w_in", i, "nt", f"hg_in_dx{i}")
            G["hg_w_in"].append(_mm(rec["h1"], dp, "tn", f"hg_in_dw{i}", out_dtype=GRAD_WIRE, exchange_out=True))
        ng1 = _vec(P["norm_g"][i, 0])
        if on_grads is not None:
            mixer = ("gm_w_in", "gm_w_out") if i % 2 == 0 else ("hg_w_in", "hg_w_out")
            ng1 = ng1 + on_grads(i, {k: G[k][-1] for k in mixer})
        if i > 0:
            prev = saved[i - 1]
            dx, dy, s_sh1, s_x1, s_gate = _norm_bwd(rec["x1"], dh1, dx, ng1, sc1, prev["y2"], prev["mods"][5],
                                                    f"norm_bwd_a{i}")
        else:
            dx, s_sh1, s_x1 = _norm_bwd(rec["x1"], dh1, dx, ng1, sc1, None, None, f"norm_bwd_a{i}")
        d_sc1, d_ng1 = s_x1 * ng1, s_x1 * (1.0 + sc1)
        small["norm_g"][i] = jnp.concatenate([d_ng1, d_ng2], axis=0)
        small["dmod"][i] = jnp.concatenate([s_sh1, d_sc1, d_g1, s_sh2, d_sc2, d_g2], axis=1)
    for k in G:
        G[k] = G[k][::-1]
    dlb_all = jnp.concatenate(small.pop("dlb"), axis=0)
    small["hg_lb"] = _lb_bwd(P["hg_lb"], dlb_all, "lb_bwd")
    return loss, dx, G, small


BIG = ("gm_w_in", "gm_w_out", "hg_w_in", "hg_w_out", "ffn_w_up", "ffn_w_down")
COL_SHARDED = dict(gm_w_in=True, gm_w_out=False, hg_w_in=True, hg_w_out=False, ffn_w_up=True, ffn_w_down=False)
LAYER_WEIGHTS = 4


def _layer_kinds(i):
    return (("gm_w_in", "gm_w_out") if i % 2 == 0 else ("hg_w_in", "hg_w_out")) + ("ffn_w_up", "ffn_w_down")


def _pack(pieces):
    tile = 8 * LANES
    parts, offs, tot = [], [], 0
    for p in pieces:
        f = p.reshape(-1).astype(F32)
        offs.append((tot, f.shape[0]))
        pad = -f.shape[0] % tile
        parts.append(jnp.pad(f, (0, pad)).reshape(-1, LANES))
        tot += f.shape[0] + pad
    return jnp.concatenate(parts, axis=0), offs


def _unpack(rows, offs, shapes):
    lead = rows.shape[:-2]
    flat = rows.reshape(lead + (-1,))
    return [flat[..., o:o + n].reshape(lead + tuple(s)) for (o, n), s in zip(offs, shapes)]


def _from_chips(per_dev, axis):
    per_chip = per_dev[0::2]
    return jnp.concatenate([per_chip[s] for s in range(N_CHIPS)], axis=axis)


def kernel(x, c, gm_w_in, gm_ln_g, gm_ln_b, gm_w_s, gm_b_s, gm_w_out, hg_w_in, hg_lb, hg_gn_g, hg_w_out, ffn_w_up, ffn_conv_w, ffn_conv_b, ffn_w_down, norm_g, ada_w, ada_b, final_g, loss_target, m_gm_w_in, m_gm_ln_g, m_gm_ln_b, m_gm_w_s, m_gm_b_s, m_gm_w_out, m_hg_w_in, m_hg_lb, m_hg_gn_g, m_hg_w_out, m_ffn_w_up, m_ffn_conv_w, m_ffn_conv_b, m_ffn_w_down, m_norm_g, m_ada_w, m_ada_b, m_final_g, v_gm_w_in, v_gm_ln_g, v_gm_ln_b, v_gm_w_s, v_gm_b_s, v_gm_w_out, v_hg_w_in, v_hg_lb, v_hg_gn_g, v_hg_w_out, v_ffn_w_up, v_ffn_conv_w, v_ffn_conv_b, v_ffn_w_down, v_norm_g, v_ada_w, v_ada_b, v_final_g):
    Dm = D_MODEL
    xi, yi, ci = _position()
    chip = 2 * xi + yi
    dev = 4 * xi + 2 * yi + ci
    weights = dict(gm_w_in=gm_w_in, gm_ln_g=gm_ln_g, gm_ln_b=gm_ln_b, gm_w_s=gm_w_s, gm_b_s=gm_b_s,
                   gm_w_out=gm_w_out, hg_w_in=hg_w_in, hg_lb=hg_lb, hg_gn_g=hg_gn_g, hg_w_out=hg_w_out,
                   ffn_w_up=ffn_w_up, ffn_conv_w=ffn_conv_w, ffn_conv_b=ffn_conv_b, ffn_w_down=ffn_w_down,
                   norm_g=norm_g, ada_w=ada_w, ada_b=ada_b, final_g=final_g)
    mom_m = dict(gm_w_in=m_gm_w_in, gm_ln_g=m_gm_ln_g, gm_ln_b=m_gm_ln_b, gm_w_s=m_gm_w_s, gm_b_s=m_gm_b_s,
                 gm_w_out=m_gm_w_out, hg_w_in=m_hg_w_in, hg_lb=m_hg_lb, hg_gn_g=m_hg_gn_g, hg_w_out=m_hg_w_out,
                 ffn_w_up=m_ffn_w_up, ffn_conv_w=m_ffn_conv_w, ffn_conv_b=m_ffn_conv_b, ffn_w_down=m_ffn_w_down,
                 norm_g=m_norm_g, ada_w=m_ada_w, ada_b=m_ada_b, final_g=m_final_g)
    mom_v = dict(gm_w_in=v_gm_w_in, gm_ln_g=v_gm_ln_g, gm_ln_b=v_gm_ln_b, gm_w_s=v_gm_w_s, gm_b_s=v_gm_b_s,
                 gm_w_out=v_gm_w_out, hg_w_in=v_hg_w_in, hg_lb=v_hg_lb, hg_gn_g=v_hg_gn_g, hg_w_out=v_hg_w_out,
                 ffn_w_up=v_ffn_w_up, ffn_conv_w=v_ffn_conv_w, ffn_conv_b=v_ffn_conv_b, ffn_w_down=v_ffn_w_down,
                 norm_g=v_norm_g, ada_w=v_ada_w, ada_b=v_ada_b, final_g=v_final_g)
    order = list(weights)

    pos = jnp.stack([chip, ci]).astype(jnp.int32)
    shards, by_col = [], []
    for i in range(DEPTH):
        for k in _layer_kinds(i):
            shards.append(weights[k][i if k.startswith("ffn") else i // 2].astype(BF16))
            by_col.append(COL_SHARDED[k])
    placed = []
    for i in range(DEPTH):
        s = slice(LAYER_WEIGHTS * i, LAYER_WEIGHTS * (i + 1))
        placed += list(_place_own(shards[s], by_col[s], pos, f"place_own{i}"))
    gsems, sh_thru, ld_thru, _ = _gather_start(shards, placed, by_col, LAYER_WEIGHTS, "gather_start")

    pieces = [c, hg_lb, hg_gn_g, norm_g, ffn_conv_w]
    payload, offs = _pack(pieces)
    got = _allgather_small(payload, "gather_small")
    c_g, lb_g, gn_g, ng_g, cw_g = _unpack(got, offs, [p.shape for p in pieces])
    c_all = c_g.reshape(N_DEV, Dm)
    P = dict(hg_lb=_from_chips(lb_g, 1), hg_gn_g=_from_chips(gn_g, 1), norm_g=_from_chips(ng_g, 2),
             ffn_conv_w=_from_chips(cw_g, 2), gm_ln_g=gm_ln_g, gm_ln_b=gm_ln_b, gm_w_s=gm_w_s, gm_b_s=gm_b_s,
             ffn_conv_b=ffn_conv_b, final_g=final_g)

    cols = ada_w.shape[2]
    ada_b_sh = lax.dynamic_slice_in_dim(ada_b, chip * cols, cols, axis=1)
    mod_sh = _ada_fwd(c_all, ada_w, ada_b_sh, "ada_fwd")
    mod_g = _allgather_small(mod_sh.reshape(-1, LANES), "gather_mod").reshape(N_DEV, DEPTH, N_DEV, cols)
    mod_mine = lax.dynamic_index_in_dim(mod_g[0::2], dev, axis=2, keepdims=False)
    mod = jnp.transpose(mod_mine, (1, 0, 2)).reshape(DEPTH, N_CHIPS * cols)

    core = jnp.reshape(ci, (1,)).astype(jnp.int32)
    chip_arr = jnp.reshape(chip, (1,)).astype(jnp.int32)
    pending, held, prefetched, swapping = [], {}, {}, []

    def get_w(i, group, after):
        lo, hi = LAYER_WEIGHTS * i, LAYER_WEIGHTS * (i + 1)
        shapes = lambda s: [a.shape for a in shards[s]]
        out = {}
        if i == 0:
            s = slice(lo, lo + 1) if group == 0 else slice(lo + 1, hi)
            landed = _gather_wait(sh_thru[s], ld_thru[s], gsems[0], gsems[1], after, by_col[s], s.start - lo,
                                  f"gather_wait0_{group}")
            full = _forward_sibling(landed, by_col[s], shapes(s), f"gather_forward0_{group}")
            out = dict(zip(_layer_kinds(0)[s.start - lo:s.stop - lo], full))
        elif group == 0:
            s = slice(lo, hi)
            send, recv, lands = prefetched.pop(i)
            full = _forward_wait(lands, send, recv, after, by_col[s], shapes(s), f"gather_forward_wait{i}")
            out = dict(zip(_layer_kinds(i), full))
        if group == 1 and i + 1 < DEPTH:
            s = slice(hi, hi + LAYER_WEIGHTS)
            landed = _gather_wait(sh_thru[s], ld_thru[s], gsems[2 * i + 2], gsems[2 * i + 3], after, by_col[s], 0,
                                  f"gather_wait{i + 1}")
            send, recv, lands, _ = _forward_start(landed, by_col[s], shapes(s), f"gather_forward_start{i + 1}")
            prefetched[i + 1] = (send, recv, lands)
        return out

    def on_grads(i, gdict):
        if i > 0 and "ffn_w_up" in gdict:
            held[i] = gdict
            return 0.0
        gdict = {**held.pop(i, {}), **gdict}
        kinds = [k for k in _layer_kinds(i) if k in gdict]
        tag = f"{i}_ffn" if kinds[0] == "ffn_w_up" else f"{i}"
        g4 = []
        for k in kinds:
            g = gdict[k]
            if not COL_SHARDED[k]:
                R, C = g.shape
                g = g.reshape(N_CHIPS, 2, R // (2 * N_CHIPS), C)
            g4.append(g)
        token = finish_swap(g4[0]) if swapping else 0.0
        if i == 0:
            from_sib = _swap_sibling(g4, f"reduce_swap{tag}", other_half=True)
            return token + start_exchange(tag, i, kinds, g4, from_sib)
        send, recv, g_thru, lands, tok = _swap_start(g4, f"reduce_swap_start{tag}", other_half=True)
        swapping.append((tag, i, kinds, send, recv, g_thru, lands))
        return token + tok[0, 0]

    def start_exchange(tag, i, kinds, g4, from_sib):
        sums = list(_add_own_halves(list(g4), list(from_sib), core, f"chip_sum{tag}"))
        send, recv, sums_thru, lands, token = _exchange_start(sums, f"reduce_start{tag}")
        pending.append((tag, i, kinds, send, recv, sums_thru, lands))
        return token[0, 0]

    def finish_swap(after):
        tag, i, kinds, send, recv, g_thru, lands = swapping.pop()
        g4, from_sib = _swap_wait(g_thru, lands, send, recv, after, f"reduce_swap_wait{tag}", other_half=True)
        return start_exchange(tag, i, kinds, g4, from_sib)

    loss_part, dx, G, small = _local_step(x[0], loss_target[0], mod, None, P, get_w, on_grads)

    sum_pieces = [loss_part[:, :1], small["final_g"], jnp.stack(small["gm_ln_g"]), jnp.stack(small["gm_ln_b"]),
                  jnp.stack(small["gm_w_s"]), jnp.stack(small["gm_b_s"]), jnp.stack(small["ffn_conv_b"]),
                  small["hg_lb"], jnp.stack(small["hg_gn_g"]), jnp.stack(small["norm_g"]),
                  jnp.stack(small["ffn_conv_w"])]
    dmod = jnp.concatenate(small["dmod"], axis=0)
    payload2, offs2 = _pack(sum_pieces + [dmod])
    placed2 = _place_row(payload2, jnp.reshape(dev, (1,)).astype(jnp.int32), "place_grads")
    bsend, brecv, p2_thru, l2_thru, small_token = _broadcast_start(payload2, placed2, "gather_grads_start")

    landed = {}
    for tag, i, kinds, send, recv, sums_thru, lands in pending:
        sums_i, lands_i = _exchange_wait(sums_thru, lands, send, recv, small_token, f"reduce_wait{tag}")
        for k, s_, l_ in zip(kinds, sums_i, lands_i):
            landed[(k, i)] = (l_, s_)
    own_halves = []
    for k in BIG:
        layers = [landed[(k, i)] for i in range(DEPTH) if (k, i) in landed]
        own_halves.append(_sum_chips([l_ for l_, _ in layers], [s_ for _, s_ in layers], chip_arr, f"sum_chips_{k}"))
    jsend, jrecv, own_thru, jlands, after = _swap_start(own_halves, "reduce_join_start")
    grads, deltas, new_m, new_v = {}, {}, {}, {}
    for n, k in enumerate(BIG):
        (own,), (recv,) = _swap_wait([own_thru[n]], [jlands[n]], jsend, jrecv, after, f"reduce_join_wait_{k}",
                                     first=n)
        grads[k], deltas[k], new_m[k], new_v[k] = _adamw_halves(
            weights[k], own, recv, mom_m[k], mom_v[k], core, f"adamw_{k}")
        after = new_v[k]

    got2 = _broadcast_wait(p2_thru, l2_thru, bsend, brecv, new_v[BIG[-1]], "gather_grads_wait")
    dmod_all = _unpack(got2, offs2[-1:], [dmod.shape])[0]
    summed = _sum_devices(got2, "sum_devices")
    (loss_s, d_final_g, d_ln_g, d_ln_b, d_ws, d_bs, d_cb, d_lb, d_gn, d_ng, d_cw) = _unpack(
        summed, offs2[:-1], [(1,), final_g.shape, gm_ln_g.shape, gm_ln_b.shape, gm_w_s.shape, gm_b_s.shape,
                             ffn_conv_b.shape, (2, Dm), (2, Dm), (DEPTH, 2, Dm), (DEPTH, 3, 2 * FFN_HIDDEN)])
    grads.update(final_g=d_final_g, gm_ln_g=d_ln_g, gm_ln_b=d_ln_b, gm_w_s=d_ws, gm_b_s=d_bs, ffn_conv_b=d_cb)
    grads["hg_lb"] = lax.dynamic_slice_in_dim(d_lb, chip * hg_lb.shape[1], hg_lb.shape[1], axis=1)
    grads["hg_gn_g"] = lax.dynamic_slice_in_dim(d_gn, chip * hg_gn_g.shape[1], hg_gn_g.shape[1], axis=1)
    grads["norm_g"] = lax.dynamic_slice_in_dim(d_ng, chip * norm_g.shape[2], norm_g.shape[2], axis=2)
    grads["ffn_conv_w"] = lax.dynamic_slice_in_dim(d_cw, chip * ffn_conv_w.shape[2], ffn_conv_w.shape[2], axis=2)
    dmod_sh = lax.dynamic_slice_in_dim(dmod_all, chip * cols, cols, axis=2)
    grads["ada_w"] = _ada_bwd(c_all, jnp.transpose(dmod_sh, (1, 0, 2)), "ada_bwd")
    grads["ada_b"] = _sum_devices(dmod_all.reshape(N_DEV, -1, LANES), "sum_ada_b").reshape(ada_b.shape)

    for k in order:
        if k in BIG:
            continue
        w = weights[k]
        shp = w.shape
        view = (-1, shp[-1]) if w.ndim > 1 else (8, -1)
        d, m2, v2 = _adamw(w.reshape(view), grads[k].reshape(view), mom_m[k].reshape(view), mom_v[k].reshape(view),
                           f"adamw_{k}")
        deltas[k], new_m[k], new_v[k] = d.reshape(shp), m2.reshape(shp), v2.reshape(shp)
        grads[k] = grads[k].reshape(shp)

    loss = loss_s.reshape(())
    return (loss, dx[None], *[grads[k] for k in order], *[deltas[k] for k in order],
            *[new_m[k] for k in order], *[new_v[k] for k in order])
```

```python
import functools

import jax
import jax.numpy as jnp
from jax import lax
from jax.experimental import pallas as pl
from jax.experimental.pallas import tpu as pltpu

F32 = jnp.float32
BF16 = jnp.bfloat16
HI = lax.Precision.HIGHEST
X3 = lax.Precision.HIGH
GRAD_WIRE = BF16
MESH = pl.DeviceIdType.MESH

D_MODEL = 1024
DEPTH = 4
EPS = 1e-6
GM_WIDTH = 2048
GM_HEADS = 8
GM_BLOCK = 128
GM_HEAD_DIM = 256
CHUNK = 64
HG_HEADS = 8
HG_DIM = 128
FFN_HIDDEN = 2816
N_CHIPS = 4
N_DEV = 8

ADAM_LR = 0.001
ADAM_B1 = 0.9
ADAM_B2 = 0.999
ADAM_EPS = 1e-08
ADAM_WD = 0.01
ADAM_STEP = 10

VMEM_LIMIT_BYTES = 56 * 1024 * 1024
ROW_TILE = 256
NORM_ROW_TILE = 512
LANES = 128

_SQRT_HALF = 0.7071067811865476
_INV_SQRT_2PI = 0.3989422804014327


def _pick(dim, prefs):
    for p in prefs:
        if dim % p == 0:
            return p
    return dim


def _params(sem):
    return pltpu.CompilerParams(dimension_semantics=sem, vmem_limit_bytes=VMEM_LIMIT_BYTES)


def _cdf(x):
    return 0.5 * (1.0 + lax.erf(x * _SQRT_HALF))


def _pdf(x):
    return jnp.exp(-0.5 * x * x) * _INV_SQRT_2PI


def _sig(x):
    return jax.nn.sigmoid(x)


def _dot(a, b, dims, prec=None):
    return lax.dot_general(a, b, (dims, ((), ())), precision=prec, preferred_element_type=F32)


NN = ((1,), (0,))
NT = ((1,), (1,))
TN = ((0,), (0,))


MM_VMEM_BUDGET = 40 * 1024 * 1024


def _mm_tiles(mode, M, N, K, a_bytes, b_bytes, exchange_out):
    tn = _pick(N, (1408, 1024, 512, 256, 128))
    tms = [t for t in (1408, 1024, 512, 256, 128) if M % t == 0 and not (exchange_out and (M // 2) % t)] or [M]
    tks = [K] + [t for t in (2816, 2048, 1408, 1024, 512, 256, 128) if t < K and K % t == 0]

    def fits(tm, tk):
        acc = tm * tn * 4 if tk < K else 0
        return 2 * tm * tk * a_bytes + 2 * tk * tn * b_bytes + 2 * tm * tn * 4 + acc <= MM_VMEM_BUDGET

    for min_tm in (min(512, tms[0]), 0):
        for tk in tks:
            for tm in tms:
                if tm >= min_tm and fits(tm, tk):
                    return tm, tn, tk
    return tms[-1], tn, tks[-1]


def _mm(a, b, mode, name, b_layer=None, out_dtype=F32, exchange_out=False):
    b2 = b.shape[-2:]
    if mode == "nn":
        (M, K), (_, N) = a.shape, b2
    elif mode == "nt":
        (M, K), (N, _) = a.shape, b2
    else:
        (K, M), (_, N) = a.shape, b2
    tm, tn, tk = _mm_tiles(mode, M, N, K, a.dtype.itemsize, b.dtype.itemsize, exchange_out)
    nk = K // tk
    dims = {"nn": NN, "nt": NT, "tn": TN}[mode]

    def body(a_ref, b_ref, o_ref, *scratch):
        part = _dot(a_ref[...].astype(BF16), b_ref[...].astype(BF16), dims)
        if nk == 1:
            o_ref[...] = part.astype(o_ref.dtype)
            return
        (acc_ref,) = scratch
        k = pl.program_id(2)

        @pl.when(k == 0)
        def _():
            acc_ref[...] = part

        @pl.when(k > 0)
        def _():
            acc_ref[...] += part

        @pl.when(k == nk - 1)
        def _():
            o_ref[...] = acc_ref[...].astype(o_ref.dtype)

    if mode == "tn":
        a_spec = pl.BlockSpec((tk, tm), lambda i, j, k: (k, i))
    else:
        a_spec = pl.BlockSpec((tm, tk), lambda i, j, k: (i, k))
    bblk = (tk, tn) if mode in ("nn", "tn") else (tn, tk)
    bidx = (lambda i, j, k: (k, j)) if mode in ("nn", "tn") else (lambda i, j, k: (j, k))
    if b_layer is None:
        b_spec = pl.BlockSpec(bblk, bidx)
    else:
        b_spec = pl.BlockSpec((None,) + bblk, lambda i, j, k: (b_layer,) + bidx(i, j, k))
    if exchange_out:
        mh, cw = M // 2, N // N_CHIPS
        assert mh % tm == 0 and cw % tn == 0
        out_shape = jax.ShapeDtypeStruct((N_CHIPS, 2, mh, cw), out_dtype)
        o_spec = pl.BlockSpec(
            (None, None, tm, tn),
            lambda i, j, k: (j // (cw // tn), i // (mh // tm), i % (mh // tm), j % (cw // tn)))
    else:
        out_shape = jax.ShapeDtypeStruct((M, N), out_dtype)
        o_spec = pl.BlockSpec((tm, tn), lambda i, j, k: (i, j))
    return pl.pallas_call(
        body, name=name, out_shape=out_shape, grid=(M // tm, N // tn, nk),
        in_specs=[a_spec, b_spec], out_specs=o_spec,
        scratch_shapes=[] if nk == 1 else [pltpu.VMEM((tm, tn), F32)],
        compiler_params=_params(("parallel", "parallel", "arbitrary")),
    )(a, b)


def _row_spec(tr, width):
    return pl.BlockSpec((tr, width), lambda i: (i, 0))


def _vec_spec(width, rows=1):
    return pl.BlockSpec((rows, width), lambda i: (0, 0))


def _norm_fwd(x, y, gate, g, sc, sh, name):
    T, Dm = x.shape
    tr = _pick(T, (2 * NORM_ROW_TILE, NORM_ROW_TILE, ROW_TILE))
    has_res = y is not None

    def body(*refs):
        if has_res:
            x_ref, y_ref, gate_ref, g_ref, sc_ref, sh_ref, xo_ref, h_ref = refs
            xv = x_ref[...] + gate_ref[...] * y_ref[...]
            xo_ref[...] = xv
        else:
            x_ref, g_ref, sc_ref, sh_ref, h_ref = refs
            xv = x_ref[...]
        rstd = lax.rsqrt(jnp.mean(xv * xv, axis=-1, keepdims=True) + EPS)
        h_ref[...] = ((xv * rstd * g_ref[...]) * (1.0 + sc_ref[...]) + sh_ref[...]).astype(BF16)

    row, vec = _row_spec(tr, Dm), _vec_spec(Dm)
    if has_res:
        ins, in_specs = (x, y, gate, g, sc, sh), [row, row, vec, vec, vec, vec]
        out_shape = (jax.ShapeDtypeStruct((T, Dm), F32), jax.ShapeDtypeStruct((T, Dm), BF16))
        out_specs = (row, row)
    else:
        ins, in_specs = (x, g, sc, sh), [row, vec, vec, vec]
        out_shape = jax.ShapeDtypeStruct((T, Dm), BF16)
        out_specs = row
    out = pl.pallas_call(body, name=name, out_shape=out_shape, grid=(T // tr,), in_specs=in_specs,
                         out_specs=out_specs, compiler_params=_params(("parallel",)))(*ins)
    return out if has_res else (x, out)


def _norm_bwd(x, dh, dxo, g, sc, y_prev, gate_prev, name):
    T, Dm = x.shape
    tr = _pick(T, (NORM_ROW_TILE, ROW_TILE))
    has_prev = y_prev is not None

    def body(*refs):
        if has_prev:
            x_ref, dh_ref, dxo_ref, g_ref, sc_ref, yp_ref, gp_ref, dx_ref, dyp_ref, s1_ref, s2_ref, s3_ref = refs
        else:
            x_ref, dh_ref, dxo_ref, g_ref, sc_ref, dx_ref, s1_ref, s2_ref = refs

        @pl.when(pl.program_id(0) == 0)
        def _():
            s1_ref[...] = jnp.zeros_like(s1_ref)
            s2_ref[...] = jnp.zeros_like(s2_ref)
            if has_prev:
                s3_ref[...] = jnp.zeros_like(s3_ref)

        xv = x_ref[...]
        rstd = lax.rsqrt(jnp.mean(xv * xv, axis=-1, keepdims=True) + EPS)
        xhat = xv * rstd
        dh = dh_ref[...]
        dxhat = dh * (g_ref[...] * (1.0 + sc_ref[...]))
        dx = dxo_ref[...] + rstd * (dxhat - xhat * jnp.mean(dxhat * xhat, axis=-1, keepdims=True))
        dx_ref[...] = dx
        s1_ref[...] += jnp.sum(dh, axis=0, keepdims=True)
        s2_ref[...] += jnp.sum(dh * xhat, axis=0, keepdims=True)
        if has_prev:
            dyp_ref[...] = (gp_ref[...] * dx).astype(BF16)
            s3_ref[...] += jnp.sum(dx * yp_ref[...], axis=0, keepdims=True)

    row, vec = _row_spec(tr, Dm), _vec_spec(Dm)
    vshape = jax.ShapeDtypeStruct((1, Dm), F32)
    if has_prev:
        ins, in_specs = (x, dh, dxo, g, sc, y_prev, gate_prev), [row, row, row, vec, vec, row, vec]
        out_shape = (jax.ShapeDtypeStruct((T, Dm), F32), jax.ShapeDtypeStruct((T, Dm), BF16), vshape, vshape, vshape)
        out_specs = (row, row, vec, vec, vec)
    else:
        ins, in_specs = (x, dh, dxo, g, sc), [row, row, row, vec, vec]
        out_shape = (jax.ShapeDtypeStruct((T, Dm), F32), vshape, vshape)
        out_specs = (row, vec, vec)
    return pl.pallas_call(body, name=name, out_shape=out_shape, grid=(T // tr,), in_specs=in_specs,
                          out_specs=out_specs, compiler_params=_params(("arbitrary",)))(*ins)


def _loss_head(x, y, gate, fg, tgt, name):
    T, Dm = x.shape
    tr = _pick(T, (NORM_ROW_TILE, ROW_TILE))
    nsteps = T // tr

    def body(x_ref, y_ref, gate_ref, fg_ref, t_ref, dx_ref, dy_ref, loss_ref, sfg_ref, sg_ref, acc_ref):
        i = pl.program_id(0)

        @pl.when(i == 0)
        def _():
            acc_ref[...] = jnp.zeros_like(acc_ref)
            sfg_ref[...] = jnp.zeros_like(sfg_ref)
            sg_ref[...] = jnp.zeros_like(sg_ref)

        yv = y_ref[...]
        xv = x_ref[...] + gate_ref[...] * yv
        rstd = lax.rsqrt(jnp.mean(xv * xv, axis=-1, keepdims=True) + EPS)
        xhat = xv * rstd
        err = xhat * fg_ref[...] - t_ref[...]
        acc_ref[...] += jnp.sum(err * err, axis=0, keepdims=True)
        dyn = err * (1.0 / Dm)
        sfg_ref[...] += jnp.sum(dyn * xhat, axis=0, keepdims=True)
        dxhat = dyn * fg_ref[...]
        dx = rstd * (dxhat - xhat * jnp.mean(dxhat * xhat, axis=-1, keepdims=True))
        dx_ref[...] = dx
        dy_ref[...] = (gate_ref[...] * dx).astype(BF16)
        sg_ref[...] += jnp.sum(dx * yv, axis=0, keepdims=True)

        @pl.when(i == nsteps - 1)
        def _():
            total = jnp.sum(acc_ref[...], axis=1, keepdims=True) * (0.5 / Dm)
            loss_ref[...] = jnp.broadcast_to(total, loss_ref.shape)

    row, vec = _row_spec(tr, Dm), _vec_spec(Dm)
    vshape = jax.ShapeDtypeStruct((1, Dm), F32)
    return pl.pallas_call(
        body, name=name, grid=(nsteps,),
        out_shape=(jax.ShapeDtypeStruct((T, Dm), F32), jax.ShapeDtypeStruct((T, Dm), BF16),
                   jax.ShapeDtypeStruct((1, LANES), F32), vshape, vshape),
        in_specs=[row, row, vec, vec, row], out_specs=(row, row, _vec_spec(LANES), vec, vec),
        scratch_shapes=[pltpu.VMEM((1, Dm), F32)], compiler_params=_params(("arbitrary",)),
    )(x, y, gate, fg, tgt)


def _spatial_mask():
    r = lax.broadcasted_iota(jnp.int32, (GM_BLOCK, GM_BLOCK), 0) // CHUNK
    c = lax.broadcasted_iota(jnp.int32, (GM_BLOCK, GM_BLOCK), 1) // CHUNK
    return r >= c


def _gm_specs(tr):
    return [_row_spec(tr, 2 * GM_WIDTH), _vec_spec(GM_WIDTH), _vec_spec(GM_WIDTH),
            pl.BlockSpec((GM_HEADS, GM_BLOCK, GM_BLOCK), lambda i: (0, 0, 0)),
            pl.BlockSpec((GM_HEADS, GM_BLOCK, 1), lambda i: (0, 0, 0))]


def _gm_mid_fwd(a, ln_g, ln_b, ws, bs3, name):
    T = a.shape[0]
    tr = _pick(T, (ROW_TILE,))
    W = GM_WIDTH

    def body(a_ref, lg_ref, lb_ref, ws_ref, bs_ref, o_ref, vn_scr):
        av = a_ref[:, W:]
        v = av * _cdf(av)
        vc = v - jnp.mean(v, axis=-1, keepdims=True)
        rstd = lax.rsqrt(jnp.mean(vc * vc, axis=-1, keepdims=True) + EPS)
        vn_scr[...] = (vc * rstd * lg_ref[...] + lb_ref[...]).astype(BF16)
        mask = _spatial_mask()
        for h in range(GM_HEADS):
            w = jnp.where(mask, ws_ref[h], 0.0).astype(BF16)
            cs = slice(h * GM_HEAD_DIM, (h + 1) * GM_HEAD_DIM)
            for blk in range(tr // GM_BLOCK):
                rs = slice(blk * GM_BLOCK, (blk + 1) * GM_BLOCK)
                s = _dot(w, vn_scr[rs, cs], NN) + bs_ref[h]
                au = a_ref[rs, cs]
                o_ref[rs, cs] = (au * _cdf(au) * s).astype(BF16)

    return pl.pallas_call(
        body, name=name, out_shape=jax.ShapeDtypeStruct((T, W), BF16), grid=(T // tr,),
        in_specs=_gm_specs(tr), out_specs=_row_spec(tr, W),
        scratch_shapes=[pltpu.VMEM((tr, W), BF16)], compiler_params=_params(("parallel",)),
    )(a, ln_g, ln_b, ws, bs3)


def _gm_mid_bwd(a, dgated, ln_g, ln_b, ws, bs3, name):
    T = a.shape[0]
    tr = _pick(T, (ROW_TILE,))
    W = GM_WIDTH
    nsteps = T // tr

    def body(a_ref, dg_ref, lg_ref, lb_ref, ws_ref, bs_ref, da_ref, dws_ref, dbs_ref, dlg_ref, dlb_ref,
             vn_scr, vhat_scr, dvn_scr, dsum_scr):
        i = pl.program_id(0)

        @pl.when(i == 0)
        def _():
            dws_ref[...] = jnp.zeros_like(dws_ref)
            dbs_ref[...] = jnp.zeros_like(dbs_ref)
            dlg_ref[...] = jnp.zeros_like(dlg_ref)
            dlb_ref[...] = jnp.zeros_like(dlb_ref)
            dsum_scr[...] = jnp.zeros_like(dsum_scr)

        av = a_ref[:, W:]
        cdf_v = _cdf(av)
        v = av * cdf_v
        vc = v - jnp.mean(v, axis=-1, keepdims=True)
        rstd = lax.rsqrt(jnp.mean(vc * vc, axis=-1, keepdims=True) + EPS)
        vhat_scr[...] = vc * rstd
        vn_scr[...] = (vhat_scr[...] * lg_ref[...] + lb_ref[...]).astype(BF16)
        mask = _spatial_mask()
        for h in range(GM_HEADS):
            w = jnp.where(mask, ws_ref[h], 0.0).astype(BF16)
            cs = slice(h * GM_HEAD_DIM, (h + 1) * GM_HEAD_DIM)
            for blk in range(tr // GM_BLOCK):
                rs = slice(blk * GM_BLOCK, (blk + 1) * GM_BLOCK)
                vnb = vn_scr[rs, cs]
                s = _dot(w, vnb, NN) + bs_ref[h]
                au = a_ref[rs, cs]
                cdf_u = _cdf(au)
                dg = dg_ref[rs, cs]
                ds = dg * (au * cdf_u)
                da_ref[rs, cs] = (dg * s * (cdf_u + au * _pdf(au))).astype(BF16)
                dsb = ds.astype(BF16)
                dvn_scr[rs, cs] = _dot(w, dsb, TN)
                dws_ref[h] += _dot(dsb, vnb, NT)
                dsum_scr[:, cs] += ds
        dvn = dvn_scr[...]
        vhat = vhat_scr[...]
        dlg_ref[...] += jnp.sum(dvn * vhat, axis=0, keepdims=True)
        dlb_ref[...] += jnp.sum(dvn, axis=0, keepdims=True)
        dvh = dvn * lg_ref[...]
        dv = rstd * (dvh - jnp.mean(dvh, axis=-1, keepdims=True)
                     - vhat * jnp.mean(dvh * vhat, axis=-1, keepdims=True))
        da_ref[:, W:] = (dv * (cdf_v + av * _pdf(av))).astype(BF16)

        @pl.when(i == nsteps - 1)
        def _():
            for h in range(GM_HEADS):
                dws_ref[h] = jnp.where(mask, dws_ref[h], 0.0)
            col_head = lax.broadcasted_iota(jnp.int32, (W, GM_BLOCK), 0) // GM_HEAD_DIM
            sel = (col_head == lax.broadcasted_iota(jnp.int32, (W, GM_BLOCK), 1)).astype(F32)
            dbs_ref[...] = _dot(dsum_scr[...], sel, NN, HI)

    vshape = jax.ShapeDtypeStruct((1, W), F32)
    return pl.pallas_call(
        body, name=name, grid=(nsteps,),
        out_shape=(jax.ShapeDtypeStruct((T, 2 * W), BF16), jax.ShapeDtypeStruct((GM_HEADS, GM_BLOCK, GM_BLOCK), F32),
                   jax.ShapeDtypeStruct((GM_BLOCK, GM_BLOCK), F32), vshape, vshape),
        in_specs=[_gm_specs(tr)[0], _row_spec(tr, W)] + _gm_specs(tr)[1:],
        out_specs=(_row_spec(tr, 2 * W), pl.BlockSpec((GM_HEADS, GM_BLOCK, GM_BLOCK), lambda i: (0, 0, 0)),
                   pl.BlockSpec((GM_BLOCK, GM_BLOCK), lambda i: (0, 0)), _vec_spec(W), _vec_spec(W)),
        scratch_shapes=[pltpu.VMEM((tr, W), BF16), pltpu.VMEM((tr, W), F32), pltpu.VMEM((tr, W), F32),
                        pltpu.VMEM((GM_BLOCK, W), F32)],
        compiler_params=_params(("arbitrary",)),
    )(a, dgated, ln_g, ln_b, ws, bs3)


SUB = 16
EXP_CLAMP = 80.0


def _tri(lower):
    r = lax.broadcasted_iota(jnp.int32, (CHUNK, CHUNK), 0)
    c = lax.broadcasted_iota(jnp.int32, (CHUNK, CHUNK), 1)
    return (r >= c) if lower else (c >= r)


def _score_masks():
    i = lax.broadcasted_iota(jnp.int32, (CHUNK, CHUNK), 0)
    j = lax.broadcasted_iota(jnp.int32, (CHUNK, CHUNK), 1)
    bi, bj = i // SUB, j // SUB
    diag = (bi == bj) & (i >= j)
    pair = (bi % 2 == 1) & (bj == bi - 1)
    half = (i >= CHUNK // 2) & (j < CHUNK // 2)
    return diag, pair, half


def _dot01(m, x):
    x1 = x.astype(BF16)
    rest = x - x1.astype(F32)
    x2 = rest.astype(BF16)
    x3 = (rest - x2.astype(F32)).astype(BF16)
    return _dot(m, x1, NN) + (_dot(m, x2, NN) + _dot(m, x3, NN))


def _block_rows(b, offset):
    parts = []
    for blk in range(0, CHUNK, SUB):
        r = blk + offset
        parts.append(jnp.zeros((SUB, b.shape[1]), F32) if r < 0 else jnp.broadcast_to(b[r:r + 1], (SUB, b.shape[1])))
    return jnp.concatenate(parts, axis=0)


def _hg_gates(p_ref, lb_ref, lower):
    Dm = D_MODEL
    heads = []
    for h in range(HG_HEADS):
        c0 = h * HG_DIM
        qr = p_ref[:, c0:c0 + HG_DIM]
        fz = p_ref[:, Dm + c0:Dm + c0 + HG_DIM]
        lbh = lb_ref[:, c0:c0 + HG_DIM]
        sg = _sig(fz)
        f = lbh + (1.0 - lbh) * sg
        sq = _sig(qr)
        heads.append(dict(qr=qr, v=p_ref[:, 2 * Dm + c0:2 * Dm + c0 + HG_DIM],
                          gt=p_ref[:, 3 * Dm + c0:3 * Dm + c0 + HG_DIM], lbh=lbh, sg=sg, f=f, gl=jnp.log(f),
                          kk=1.0 - f, sq=sq, q=qr * sq))
    for g in heads:
        g["b"] = _dot01(lower, g.pop("gl"))
    for g in heads:
        g.update(_hg_scalings(g["q"], g["kk"], g.pop("b")))
    return heads


def _hg_scalings(q, kk, b):
    r_mid = _block_rows(b, SUB // 2 - 1)
    r_prev = _block_rows(b, -1)
    r_end = _block_rows(b, SUB - 1)
    r_half = jnp.broadcast_to(b[CHUNK // 2 - 1:CHUNK // 2], b.shape)
    bc = b[CHUNK - 1:CHUNK]
    eqs = (jnp.exp(jnp.clip(b - r_mid, -EXP_CLAMP, EXP_CLAMP)), jnp.exp(jnp.minimum(b - r_prev, 0.0)),
           jnp.exp(jnp.minimum(b - r_half, 0.0)))
    eks = (jnp.exp(jnp.clip(r_mid - b, -EXP_CLAMP, EXP_CLAMP)), jnp.exp(jnp.minimum(r_end - b, 0.0)),
           jnp.exp(jnp.minimum(r_half - b, 0.0)))
    eb = jnp.exp(b)
    ec = jnp.exp(bc - b)
    return dict(eqs=eqs, eks=eks, eb=eb, ec=ec, e_end=jnp.exp(bc), qs=[q * e for e in eqs],
                ks=[kk * e for e in eks], qe=q * eb, ke=kk * ec)


def _scores(g, masks):
    a = None
    for qs, ks, m in zip(g["qs"], g["ks"], masks):
        part = jnp.where(m, _dot(qs.astype(BF16), ks.astype(BF16), NT), 0.0)
        a = part if a is None else a + part
    return a


def _hg_scan_fwd(p, lb, gn, name):
    T = p.shape[0]
    nc = T // CHUNK
    Dm = D_MODEL

    def body(p_ref, lb_ref, gn_ref, o_ref, og_ref, so_ref, st_ref):
        @pl.when(pl.program_id(0) == 0)
        def _():
            st_ref[...] = jnp.zeros_like(st_ref)

        masks = _score_masks()
        heads = _hg_gates(p_ref, lb_ref, _tri(True).astype(BF16))
        states = [st_ref[h] for h in range(HG_HEADS)]
        scores = [_scores(g, masks) for g in heads]
        outs = [_dot(a.astype(BF16), g["v"].astype(BF16), NN) + _dot(g["qe"], st, NT, X3)
                for g, a, st in zip(heads, scores, states)]
        new_states = [st * g["e_end"] + _dot(g["v"], g["ke"], TN, X3) for g, st in zip(heads, states)]
        for h, (g, o, st, st2) in enumerate(zip(heads, outs, states, new_states)):
            cs = slice(h * HG_DIM, (h + 1) * HG_DIM)
            so_ref[0, h] = st
            st_ref[h] = st2
            o_ref[:, cs] = o
            r = lax.rsqrt(jnp.mean(o * o, axis=-1, keepdims=True) + EPS)
            gt = g["gt"]
            og_ref[:, cs] = (((o * r) * gn_ref[:, cs]).astype(F32) * (gt * _sig(gt))).astype(BF16)

    return pl.pallas_call(
        body, name=name, grid=(nc,),
        out_shape=(jax.ShapeDtypeStruct((T, Dm), F32), jax.ShapeDtypeStruct((T, Dm), BF16),
                   jax.ShapeDtypeStruct((nc, HG_HEADS, HG_DIM, HG_DIM), F32)),
        in_specs=[_row_spec(CHUNK, 4 * Dm), _vec_spec(Dm), _vec_spec(Dm)],
        out_specs=(_row_spec(CHUNK, Dm), _row_spec(CHUNK, Dm),
                   pl.BlockSpec((1, HG_HEADS, HG_DIM, HG_DIM), lambda i: (i, 0, 0, 0))),
        scratch_shapes=[pltpu.VMEM((HG_HEADS, HG_DIM, HG_DIM), F32)],
        compiler_params=_params(("arbitrary",)),
    )(p, lb, gn)


def _hg_scan_bwd(p, lb, gn, o, dog, states, name):
    T = p.shape[0]
    nc = T // CHUNK
    Dm = D_MODEL

    def rev(i):
        return nc - 1 - i

    def body(p_ref, lb_ref, gn_ref, o_ref, dog_ref, st_in_ref, dp_ref, dlb_ref, dgn_ref, dst_ref, carry_ref):
        @pl.when(pl.program_id(0) == 0)
        def _():
            dst_ref[...] = jnp.zeros_like(dst_ref)
            carry_ref[...] = jnp.zeros_like(carry_ref)
            dlb_ref[...] = jnp.zeros_like(dlb_ref)
            dgn_ref[...] = jnp.zeros_like(dgn_ref)

        upper = _tri(False).astype(BF16)
        masks = _score_masks()
        heads = _hg_gates(p_ref, lb_ref, _tri(True).astype(BF16))
        for h, g in enumerate(heads):
            cs = slice(h * HG_DIM, (h + 1) * HG_DIM)
            oh = o_ref[:, cs]
            r = lax.rsqrt(jnp.mean(oh * oh, axis=-1, keepdims=True) + EPS)
            on = oh * r
            gt = g["gt"]
            sgt = _sig(gt)
            sil = gt * sgt
            dogh = dog_ref[:, cs]
            gnh = gn_ref[:, cs]
            don = dogh * gnh * sil
            g["dgn"] = jnp.sum(dogh * on * sil, axis=0, keepdims=True)
            g["dgate"] = dogh * on * gnh * (sgt * (1.0 + gt * (1.0 - sgt)))
            g["do"] = r * (don - on * jnp.mean(don * on, axis=-1, keepdims=True))
            g["dst"] = dst_ref[h]
            g["st"] = st_in_ref[0, h]
            g["carry"] = carry_ref[h]
        for g in heads:
            g["a"] = _scores(g, masks)
            g["dob"] = g["do"].astype(BF16)
            g["da"] = _dot(g["dob"], g["v"].astype(BF16), NT)
        for g in heads:
            g["dv"] = _dot(g["a"].astype(BF16), g["dob"], TN) + _dot(g["ke"].astype(BF16), g["dst"].astype(BF16), NT)
            g["dq"] = _dot(g["do"], g["st"], NN, X3) * g["eb"]
            g["dk"] = _dot(g["v"], g["dst"], NN, X3) * g["ec"]
            g["dst2"] = g["dst"] * g["e_end"] + _dot(g["do"], g["qe"], TN, X3)
        for lvl in range(3):
            for g in heads:
                dam = jnp.where(masks[lvl], g["da"], 0.0)
                g["dq"] = g["dq"] + _dot(dam, g["ks"][lvl], NN, X3) * g["eqs"][lvl]
                g["dk"] = g["dk"] + _dot(dam, g["qs"][lvl], TN, X3) * g["eks"][lvl]
        for g in heads:
            g["dgd"] = g["q"] * g["dq"] - g["kk"] * g["dk"]
            g["dgl"] = _dot01(upper, g["dgd"]) + g["carry"]
        for h, g in enumerate(heads):
            c0 = h * HG_DIM
            cs = slice(c0, c0 + HG_DIM)
            df = g["dgl"] / g["f"] - g["dk"]
            sg, sq, qr = g["sg"], g["sq"], g["qr"]
            dst_ref[h] = g["dst2"]
            carry_ref[h] = g["carry"] + jnp.sum(g["dgd"], axis=0, keepdims=True)
            dgn_ref[:, cs] += g["dgn"]
            dlb_ref[:, cs] += jnp.sum(df * (1.0 - sg), axis=0, keepdims=True)
            dp_ref[:, c0:c0 + HG_DIM] = (g["dq"] * (sq * (1.0 + qr * (1.0 - sq)))).astype(BF16)
            dp_ref[:, Dm + c0:Dm + c0 + HG_DIM] = (df * (1.0 - g["lbh"]) * sg * (1.0 - sg)).astype(BF16)
            dp_ref[:, 2 * Dm + c0:2 * Dm + c0 + HG_DIM] = g["dv"].astype(BF16)
            dp_ref[:, 3 * Dm + c0:3 * Dm + c0 + HG_DIM] = g["dgate"].astype(BF16)

    vshape = jax.ShapeDtypeStruct((1, Dm), F32)
    rrow = lambda w: pl.BlockSpec((CHUNK, w), lambda i: (rev(i), 0))
    return pl.pallas_call(
        body, name=name, grid=(nc,),
        out_shape=(jax.ShapeDtypeStruct((T, 4 * Dm), BF16), vshape, vshape),
        in_specs=[rrow(4 * Dm), _vec_spec(Dm), _vec_spec(Dm), rrow(Dm), rrow(Dm),
                  pl.BlockSpec((1, HG_HEADS, HG_DIM, HG_DIM), lambda i: (rev(i), 0, 0, 0))],
        out_specs=(rrow(4 * Dm), _vec_spec(Dm), _vec_spec(Dm)),
        scratch_shapes=[pltpu.VMEM((HG_HEADS, HG_DIM, HG_DIM), F32), pltpu.VMEM((HG_HEADS, 1, HG_DIM), F32)],
        compiler_params=_params(("arbitrary",)),
    )(p, lb, gn, o, dog, states)


def _lb_fwd(hg_lb, name):
    def body(a_ref, o_ref):
        a0, a1 = a_ref[0:1], a_ref[1:2]
        m = jnp.maximum(a0, a1)
        e0, e1 = jnp.exp(a0 - m), jnp.exp(a1 - m)
        p0, p1 = e0 / (e0 + e1), e1 / (e0 + e1)
        o_ref[0:1] = p0 - p0
        o_ref[1:2] = (p0 + p1) - p0

    return pl.pallas_call(body, name=name, out_shape=jax.ShapeDtypeStruct(hg_lb.shape, F32))(hg_lb)


def _lb_bwd(hg_lb, dlb_all, name):
    def body(a_ref, d_ref, o_ref):
        a0, a1 = a_ref[0:1], a_ref[1:2]
        m = jnp.maximum(a0, a1)
        e0, e1 = jnp.exp(a0 - m), jnp.exp(a1 - m)
        p0, p1 = e0 / (e0 + e1), e1 / (e0 + e1)
        d1 = d_ref[1:2]
        o_ref[0:1] = -p0 * p1 * d1
        o_ref[1:2] = p1 * (1.0 - p1) * d1

    return pl.pallas_call(body, name=name, out_shape=jax.ShapeDtypeStruct(hg_lb.shape, F32))(hg_lb, dlb_all)


CONV_COLS_FWD = 256
CONV_COLS_BWD = 128
CONV_ROWS_FWD = 512


def _conv_fwd(a, w, b, name):
    T = a.shape[0]
    Fh = FFN_HIDDEN
    tr = _pick(T, (CONV_ROWS_FWD, ROW_TILE))
    cw = CONV_COLS_FWD
    hb = tr // 8

    def body(a_ref, ap_ref, w_ref, b_ref, m_ref):
        m0 = (pl.program_id(0) > 0).astype(F32)

        def conv(cc):
            x = jnp.concatenate([ap_ref[:, pl.ds(cc, cw)] * m0, a_ref[:, pl.ds(cc, cw)]], axis=0)
            wv = w_ref[:, pl.ds(cc, cw)]
            y = b_ref[:, pl.ds(cc, cw)] + wv[2:3] * x + wv[1:2] * pltpu.roll(x, 1, axis=0) \
                + wv[0:1] * pltpu.roll(x, 2, axis=0)
            return y[8:]

        def step(c, carry):
            c0 = pl.multiple_of(c * cw, cw)
            c1 = pl.multiple_of(Fh + c * cw, cw)
            yg, yv = conv(c0), conv(c1)
            m_ref[:, pl.ds(c0, cw)] = (yg * _cdf(yg) * yv).astype(BF16)
            return carry

        lax.fori_loop(0, Fh // cw, step, 0)

    return pl.pallas_call(
        body, name=name, out_shape=jax.ShapeDtypeStruct((T, Fh), BF16), grid=(T // tr,),
        in_specs=[_row_spec(tr, 2 * Fh), pl.BlockSpec((8, 2 * Fh), lambda i: (jnp.maximum(i * hb - 1, 0), 0)),
                  _vec_spec(2 * Fh, 3), _vec_spec(2 * Fh)],
        out_specs=_row_spec(tr, Fh), compiler_params=_params(("parallel",)),
    )(a, a, w, b)


def _conv_bwd(a, dm, w, b, w_up, name):
    T = a.shape[0]
    Fh = FFN_HIDDEN
    Dm = w_up.shape[0]
    tr = _pick(T, (ROW_TILE,))
    cw = CONV_COLS_BWD
    hb = tr // 8
    nsteps = T // tr
    n = tr + 8

    def body(a_ref, ap_ref, an_ref, dm_ref, dmn_ref, w_ref, b_ref, wup_hbm, da_ref, dw_ref, db_ref, dh_ref,
             wup_ref, sem):
        i = pl.program_id(0)
        m0 = (i > 0).astype(F32)
        m1 = (i < nsteps - 1).astype(F32)

        @pl.when(i == 0)
        def _():
            dw_ref[...] = jnp.zeros_like(dw_ref)
            db_ref[...] = jnp.zeros_like(db_ref)
            cp = pltpu.make_async_copy(wup_hbm, wup_ref, sem)
            cp.start()
            cp.wait()

        def prep(cc):
            x = jnp.concatenate([ap_ref[:, pl.ds(cc, cw)] * m0, a_ref[:, pl.ds(cc, cw)],
                                 an_ref[:, pl.ds(cc, cw)] * m1], axis=0)
            wv = w_ref[:, pl.ds(cc, cw)]
            s1 = pltpu.roll(x, 1, axis=0)
            s2 = pltpu.roll(x, 2, axis=0)
            y = b_ref[:, pl.ds(cc, cw)] + wv[2:3] * x + wv[1:2] * s1 + wv[0:1] * s2
            return wv, x[8:], s1[8:], s2[8:], y[8:]

        def back(cc, dy, wv, x0, s1, s2):
            da = wv[2:3] * dy + wv[1:2] * pltpu.roll(dy, n - 1, axis=0) + wv[0:1] * pltpu.roll(dy, n - 2, axis=0)
            dab = da[:tr].astype(BF16)
            da_ref[:, pl.ds(cc, cw)] = dab
            d = dy[:tr]
            db_ref[:, pl.ds(cc, cw)] += jnp.sum(d, axis=0, keepdims=True)
            dw_ref[2:3, pl.ds(cc, cw)] += jnp.sum(d * x0[:tr], axis=0, keepdims=True)
            dw_ref[1:2, pl.ds(cc, cw)] += jnp.sum(d * s1[:tr], axis=0, keepdims=True)
            dw_ref[0:1, pl.ds(cc, cw)] += jnp.sum(d * s2[:tr], axis=0, keepdims=True)
            return dab

        acc = None
        for c in range(Fh // cw):
            c0, c1 = c * cw, Fh + c * cw
            dmx = jnp.concatenate([dm_ref[:, pl.ds(c0, cw)], dmn_ref[:, pl.ds(c0, cw)] * m1], axis=0)
            wg, xg, s1g, s2g, yg = prep(c0)
            wv, xv, s1v, s2v, yv = prep(c1)
            cg = _cdf(yg)
            dag = back(c0, dmx * yv * (cg + yg * _pdf(yg)), wg, xg, s1g, s2g)
            dav = back(c1, dmx * (yg * cg), wv, xv, s1v, s2v)
            part = _dot(dag, wup_ref[:, c0:c0 + cw], NT) + _dot(dav, wup_ref[:, c1:c1 + cw], NT)
            acc = part if acc is None else acc + part
        dh_ref[...] = acc

    prev = lambda wd: pl.BlockSpec((8, wd), lambda i: (jnp.maximum(i * hb - 1, 0), 0))
    nxt = lambda wd: pl.BlockSpec((8, wd), lambda i: (jnp.minimum((i + 1) * hb, T // 8 - 1), 0))
    return pl.pallas_call(
        body, name=name, grid=(nsteps,),
        out_shape=(jax.ShapeDtypeStruct((T, 2 * Fh), BF16), jax.ShapeDtypeStruct((3, 2 * Fh), F32),
                   jax.ShapeDtypeStruct((1, 2 * Fh), F32), jax.ShapeDtypeStruct((T, Dm), F32)),
        in_specs=[_row_spec(tr, 2 * Fh), prev(2 * Fh), nxt(2 * Fh), _row_spec(tr, Fh), nxt(Fh),
                  _vec_spec(2 * Fh, 3), _vec_spec(2 * Fh), ANY],
        out_specs=(_row_spec(tr, 2 * Fh), _vec_spec(2 * Fh, 3), _vec_spec(2 * Fh), _row_spec(tr, Dm)),
        scratch_shapes=[pltpu.VMEM(w_up.shape, w_up.dtype), pltpu.SemaphoreType.DMA],
        compiler_params=_params(("arbitrary",)),
    )(a, a, a, dm, dm, w, b, w_up)


def _ada_fwd(c_all, ada_w, ada_b, name):
    L, Dm, cols = ada_w.shape
    tn = _pick(cols, (512, 256, 128))

    def body(c_ref, w_ref, b_ref, o_ref):
        cv = c_ref[...]
        cond = (cv * _sig(cv)).astype(BF16)
        o_ref[...] = _dot(cond, w_ref[...].astype(BF16), NN) + b_ref[...]

    return pl.pallas_call(
        body, name=name, out_shape=jax.ShapeDtypeStruct((L, N_DEV, cols), F32), grid=(L, cols // tn),
        in_specs=[pl.BlockSpec((N_DEV, Dm), lambda l, j: (0, 0)), pl.BlockSpec((None, Dm, tn), lambda l, j: (l, 0, j)),
                  pl.BlockSpec((None, 1, tn), lambda l, j: (l, 0, j))],
        out_specs=pl.BlockSpec((None, N_DEV, tn), lambda l, j: (l, 0, j)),
        compiler_params=_params(("parallel", "parallel")),
    )(c_all, ada_w, ada_b.reshape(L, 1, cols))


def _ada_bwd(c_all, dmod, name):
    L, _, cols = dmod.shape
    Dm = c_all.shape[1]
    tn = _pick(cols, (512, 256, 128))

    def body(c_ref, d_ref, o_ref):
        cv = c_ref[...]
        o_ref[...] = _dot(cv * _sig(cv), d_ref[...], TN, HI)

    return pl.pallas_call(
        body, name=name, out_shape=jax.ShapeDtypeStruct((L, Dm, cols), F32), grid=(L, cols // tn),
        in_specs=[pl.BlockSpec((N_DEV, Dm), lambda l, j: (0, 0)), pl.BlockSpec((None, N_DEV, tn), lambda l, j: (l, 0, j))],
        out_specs=pl.BlockSpec((None, Dm, tn), lambda l, j: (l, 0, j)),
        compiler_params=_params(("parallel", "parallel")),
    )(c_all, dmod)


def _add_own_halves(g4s, rbs, core, name):
    n = len(g4s)

    def body(core_ref, *refs):
        for g_ref, r_ref, o_ref in zip(refs[:n], refs[n:2 * n], refs[2 * n:]):
            o_ref[...] = (g_ref[...].astype(F32) + r_ref[...].astype(F32)).astype(GRAD_WIRE)

    shapes = [g.shape for g in g4s]
    return pl.pallas_call(
        body, name=name, out_shape=tuple(jax.ShapeDtypeStruct((S, rh, cw), GRAD_WIRE) for S, _, rh, cw in shapes),
        grid_spec=pltpu.PrefetchScalarGridSpec(
            num_scalar_prefetch=1, grid=(N_CHIPS,),
            in_specs=[pl.BlockSpec((None, None, rh, cw), lambda s, core_ref: (s, core_ref[0], 0, 0))
                      for _, _, rh, cw in shapes]
            + [pl.BlockSpec((None, rh, cw), lambda s, core_ref: (s, 0, 0)) for _, _, rh, cw in shapes],
            out_specs=tuple(pl.BlockSpec((None, rh, cw), lambda s, core_ref: (s, 0, 0)) for _, _, rh, cw in shapes)),
        compiler_params=_params(("parallel",)),
    )(core, *g4s, *rbs)


def _sum_chips(lands, sums, chip, name):
    L = len(lands)
    _, rh, cw = lands[0].shape
    tr = _pick(rh, (256, 128, 176, 64))

    def body(chip_ref, *refs):
        ld, cs, o_ref = refs[:L], refs[L:2 * L], refs[2 * L]
        me = chip_ref[0]
        for k in range(L):
            @pl.when(pl.program_id(0) == k)
            def _(k=k):
                own = cs[k][...].astype(F32)
                got = [ld[k][j].astype(F32) for j in range(3)]
                acc = None
                for t in range(N_CHIPS):
                    d = jnp.bitwise_xor(jnp.int32(t), me)
                    term = jnp.where(d == 0, own, jnp.where(d == 2, got[0], jnp.where(d == 1, got[1], got[2])))
                    acc = term if acc is None else acc + term
                o_ref[...] = acc

    frozen = lambda l, i, k: jnp.where(l == k, i, 0)
    in_specs = [pl.BlockSpec((3, tr, cw), lambda l, i, chip_ref, k=k: (0, frozen(l, i, k), 0)) for k in range(L)]
    in_specs += [pl.BlockSpec((None, tr, cw), lambda l, i, chip_ref, k=k: (chip_ref[0], frozen(l, i, k), 0))
                 for k in range(L)]
    return pl.pallas_call(
        body, name=name, out_shape=jax.ShapeDtypeStruct((L, rh, cw), F32),
        grid_spec=pltpu.PrefetchScalarGridSpec(
            num_scalar_prefetch=1, grid=(L, rh // tr), in_specs=in_specs,
            out_specs=pl.BlockSpec((None, tr, cw), lambda l, i, chip_ref: (l, i, 0))),
        compiler_params=_params(("arbitrary", "arbitrary")),
    )(chip, *lands, *sums)


def _sum_devices(gathered, name):
    n, R, _ = gathered.shape
    tr = _pick(R, (512, 448, 384, 256, 192, 128, 64, 32, 16, 8))

    def body(g_ref, o_ref):
        acc = g_ref[0]
        for d in range(1, n):
            acc = acc + g_ref[d]
        o_ref[...] = acc

    return pl.pallas_call(
        body, name=name, out_shape=jax.ShapeDtypeStruct((R, LANES), F32), grid=(R // tr,),
        in_specs=[pl.BlockSpec((n, tr, LANES), lambda i: (0, i, 0))], out_specs=pl.BlockSpec((tr, LANES), lambda i: (i, 0)),
        compiler_params=_params(("parallel",)),
    )(gathered)


def _adamw(w, g, m, v, name):
    R, C = w.shape
    tr = _pick(R, (256, 128, 64, 32, 16, 8))
    c1 = 1.0 / (1.0 - ADAM_B1 ** ADAM_STEP)
    c2 = 1.0 / (1.0 - ADAM_B2 ** ADAM_STEP)

    def body(w_ref, g_ref, m_ref, v_ref, d_ref, mo_ref, vo_ref):
        gv = g_ref[...]
        m2 = ADAM_B1 * m_ref[...] + (1.0 - ADAM_B1) * gv
        v2 = ADAM_B2 * v_ref[...] + (1.0 - ADAM_B2) * (gv * gv)
        mo_ref[...] = m2
        vo_ref[...] = v2
        d_ref[...] = -ADAM_LR * ((m2 * c1) / (jnp.sqrt(v2 * c2) + ADAM_EPS) + ADAM_WD * w_ref[...])

    spec = pl.BlockSpec((tr, C), lambda i: (i, 0))
    shp = jax.ShapeDtypeStruct((R, C), F32)
    return pl.pallas_call(body, name=name, out_shape=(shp, shp, shp), grid=(R // tr,), in_specs=[spec] * 4,
                          out_specs=(spec, spec, spec), compiler_params=_params(("parallel",)))(w, g, m, v)


def _adamw_halves(w, own, recv, m, v, core, name):
    L, rh, cw = own.shape
    tr = _pick(rh, (256, 128, 176, 64))
    c1 = 1.0 / (1.0 - ADAM_B1 ** ADAM_STEP)
    c2 = 1.0 / (1.0 - ADAM_B2 ** ADAM_STEP)

    def body(core_ref, w_ref, own_ref, recv_ref, m_ref, v_ref, g_ref, d_ref, mo_ref, vo_ref):
        gv = jnp.where(pl.program_id(1) == core_ref[0], own_ref[...], recv_ref[...])
        g_ref[...] = gv
        m2 = ADAM_B1 * m_ref[...] + (1.0 - ADAM_B1) * gv
        v2 = ADAM_B2 * v_ref[...] + (1.0 - ADAM_B2) * (gv * gv)
        mo_ref[...] = m2
        vo_ref[...] = v2
        d_ref[...] = -ADAM_LR * ((m2 * c1) / (jnp.sqrt(v2 * c2) + ADAM_EPS) + ADAM_WD * w_ref[...])

    full = pl.BlockSpec((None, None, tr, cw), lambda l, hf, i, core_ref: (l, hf, i, 0))
    mine = pl.BlockSpec((None, tr, cw), lambda l, hf, i, core_ref: (l, jnp.where(hf == core_ref[0], i, 0), 0))
    other = pl.BlockSpec((None, tr, cw), lambda l, hf, i, core_ref: (l, jnp.where(hf == core_ref[0], 0, i), 0))
    shp = jax.ShapeDtypeStruct((L, 2, rh, cw), F32)
    view = lambda a: a.reshape(L, 2, rh, cw)
    outs = pl.pallas_call(
        body, name=name, out_shape=(shp, shp, shp, shp),
        grid_spec=pltpu.PrefetchScalarGridSpec(
            num_scalar_prefetch=1, grid=(L, 2, rh // tr), in_specs=[full, mine, other, full, full],
            out_specs=(full, full, full, full)),
        compiler_params=_params(("arbitrary", "arbitrary", "arbitrary")),
    )(core, view(w), own, recv, view(m), view(v))
    return tuple(o.reshape(L, 2 * rh, cw) for o in outs)


ANY = pl.BlockSpec(memory_space=pl.ANY)


def _position():
    x, y, c = lax.axis_index("x"), lax.axis_index("y"), lax.axis_index("c")
    return x, y, c


def _allgather(ins, out_shapes, src_fns, dst_fns, name):
    n = len(ins)

    def body(*refs):
        in_refs, out_refs = refs[:n], refs[n:2 * n]
        send_sems, recv_sems, local_sems = refs[2 * n:]
        x, y, c = _position()
        me, sibling = (x, y, c), (x, y, 1 - c)
        chips = [(1 - x, y), (x, 1 - y), (1 - x, 1 - y)]

        def copy(k, j, block, to, own=False):
            dst = dst_fns[k](out_refs[k], *block)
            return pltpu.make_async_remote_copy(
                src_ref=src_fns[k](in_refs[k], c) if own else dst, dst_ref=dst,
                send_sem=send_sems.at[k, j], recv_sem=recv_sems.at[k, j], device_id=to, device_id_type=MESH)

        mine = [pltpu.make_async_copy(src_fns[k](in_refs[k], c), dst_fns[k](out_refs[k], *me), local_sems.at[k])
                for k in range(n)]
        for cp in mine:
            cp.start()
        first = []
        for k in range(n):
            first.append(copy(k, 0, me, sibling, own=True))
            first += [copy(k, 1 + j, me, (*chip, c), own=True) for j, chip in enumerate(chips)]
        for cp in first:
            cp.start()
        passed = []
        for j, chip in enumerate(chips):
            for k in range(n):
                copy(k, 1 + j, (*chip, c), me).wait_recv()
                fwd = copy(k, 4 + j, (*chip, c), sibling)
                fwd.start()
                passed.append(fwd)
        for k in range(n):
            copy(k, 0, sibling, me).wait_recv()
        for j, chip in enumerate(chips):
            for k in range(n):
                copy(k, 4 + j, (*chip, 1 - c), me).wait_recv()
        for cp in first + passed:
            cp.wait_send()
        for cp in mine:
            cp.wait()

    spec = pl.BlockSpec(memory_space=pltpu.VMEM)
    return pl.pallas_call(
        body, name=name, out_shape=tuple(out_shapes), in_specs=[spec] * n, out_specs=tuple([spec] * n),
        scratch_shapes=[pltpu.SemaphoreType.DMA((n, 7)), pltpu.SemaphoreType.DMA((n, 7)),
                        pltpu.SemaphoreType.DMA((n,))],
        compiler_params=pltpu.CompilerParams(vmem_limit_bytes=VMEM_LIMIT_BYTES),
    )(*ins)


def _allgather_small(payload, name):
    R = payload.shape[0]
    (out,) = _allgather(
        [payload], [jax.ShapeDtypeStruct((N_DEV, R, LANES), F32)],
        [lambda ref, c: ref], [lambda ref, px, py, pc: ref.at[4 * px + 2 * py + pc]], name)
    return out


def _swap_sibling(ins, name, other_half=False):
    n = len(ins)

    def body(*refs):
        in_refs, out_refs = refs[:n], refs[n:2 * n]
        send_sems, recv_sems = refs[2 * n:]
        x, y, c = _position()
        copies = [pltpu.make_async_remote_copy(
            src_ref=in_refs[k].at[:, 1 - c] if other_half else in_refs[k], dst_ref=out_refs[k],
            send_sem=send_sems.at[k], recv_sem=recv_sems.at[k],
            device_id=(x, y, 1 - c), device_id_type=MESH) for k in range(n)]
        for cp in copies:
            cp.start()
        for cp in copies:
            cp.wait_recv()
        for cp in copies:
            cp.wait_send()

    shape = lambda a: (a.shape[0],) + a.shape[2:] if other_half else a.shape
    return pl.pallas_call(
        body, name=name, out_shape=tuple(jax.ShapeDtypeStruct(shape(a), a.dtype) for a in ins),
        in_specs=[ANY] * n, out_specs=tuple([ANY] * n),
        scratch_shapes=[pltpu.SemaphoreType.DMA((n,)), pltpu.SemaphoreType.DMA((n,))],
        compiler_params=pltpu.CompilerParams(vmem_limit_bytes=VMEM_LIMIT_BYTES),
    )(*ins)


HBM_SPEC = pl.BlockSpec(memory_space=pltpu.HBM)
SEM_SPEC = pl.BlockSpec(memory_space=pltpu.SEMAPHORE)
SPLIT_PARAMS = pltpu.CompilerParams(has_side_effects=pltpu.SideEffectType.DATAFLOW_SIDE_EFFECTING)
TOKEN = jax.ShapeDtypeStruct((8, LANES), F32)


def _hbm(a):
    return pltpu.with_memory_space_constraint(a, pltpu.HBM)


def _weight_window(ref, col, r, cw, px, py, pc):
    rh = r // 2
    if col:
        return ref.at[pl.ds(pc * rh, rh), pl.ds((2 * px + py) * cw, cw)]
    return ref.at[pl.ds((2 * px + py) * r + pc * rh, rh), :]


def _peers(x, y, c):
    return [(x, y, 1 - c), (1 - x, y, c), (x, 1 - y, c), (1 - x, 1 - y, c)]


def _place_own(shards, cols, pos, name):
    n = len(shards)

    def body(pos_ref, *refs):
        for x_ref, o_ref in zip(refs[:n], refs[n:]):
            o_ref[...] = x_ref[...]

    in_specs, out_specs, out_shape = [], [], []
    for sh, col in zip(shards, cols):
        r, cw = sh.shape
        rh = r // 2
        in_specs.append(pl.BlockSpec((rh, cw), lambda i, pos_ref: (pos_ref[1], 0)))
        if col:
            out_specs.append(pl.BlockSpec((rh, cw), lambda i, pos_ref: (pos_ref[1], pos_ref[0])))
            out_shape.append(jax.ShapeDtypeStruct((r, N_CHIPS * cw), sh.dtype))
        else:
            out_specs.append(pl.BlockSpec((rh, cw), lambda i, pos_ref: (2 * pos_ref[0] + pos_ref[1], 0)))
            out_shape.append(jax.ShapeDtypeStruct((N_CHIPS * r, cw), sh.dtype))
    return pl.pallas_call(
        body, name=name, out_shape=tuple(out_shape),
        grid_spec=pltpu.PrefetchScalarGridSpec(num_scalar_prefetch=1, grid=(1,), in_specs=in_specs,
                                               out_specs=tuple(out_specs)),
        compiler_params=_params(("arbitrary",)),
    )(pos, *shards)


def _gather_start(shards, lands, cols, per_layer, name):
    n = len(shards)
    nl = n // per_layer

    def body(*refs):
        sh, ld = refs[:n], refs[n:2 * n]
        sems, token = refs[2 * n:2 * n + 2 * nl], refs[-1]
        x, y, c = _position()
        for k in range(n):
            l, a = divmod(k, per_layer)
            r, cw = shards[k].shape
            src = sh[k].at[pl.ds(c * (r // 2), r // 2), :]
            dst = _weight_window(ld[k], cols[k], r, cw, x, y, c)
            for j, peer in enumerate(_peers(x, y, c)):
                pltpu.make_async_remote_copy(src_ref=src, dst_ref=dst, send_sem=sems[2 * l].at[4 * a + j],
                                             recv_sem=sems[2 * l + 1].at[4 * a + j], device_id=peer,
                                             device_id_type=MESH).start()
        token[...] = jnp.zeros_like(token)

    arrs = list(shards) + list(lands)
    out = pl.pallas_call(
        body, name=name,
        out_shape=tuple(pltpu.SemaphoreType.DMA((per_layer * 4,)) for _ in range(2 * nl))
        + tuple(pltpu.HBM(a.shape, a.dtype) for a in arrs) + (TOKEN,),
        in_specs=[HBM_SPEC] * (2 * n),
        out_specs=(SEM_SPEC,) * (2 * nl) + (HBM_SPEC,) * (2 * n) + (pl.BlockSpec(memory_space=pltpu.VMEM),),
        input_output_aliases={i: 2 * nl + i for i in range(2 * n)}, compiler_params=SPLIT_PARAMS,
    )(*[_hbm(a) for a in arrs])
    return out[:2 * nl], out[2 * nl:2 * nl + n], out[2 * nl + n:2 * nl + 2 * n], out[-1]


def _gather_wait(shards, lands, send, recv, after, cols, first, name):
    m = len(shards)

    def body(*refs):
        sh, ld = refs[:m], refs[m:2 * m]
        send_ref, recv_ref = refs[2 * m], refs[2 * m + 1]
        x, y, c = _position()
        for a in range(m):
            r, cw = shards[a].shape
            src = sh[a].at[pl.ds(c * (r // 2), r // 2), :]
            for j, (px, py, pc) in enumerate(_peers(x, y, c)):
                cp = pltpu.make_async_remote_copy(
                    src_ref=src, dst_ref=_weight_window(ld[a], cols[a], r, cw, px, py, pc),
                    send_sem=send_ref.at[4 * (first + a) + j], recv_sem=recv_ref.at[4 * (first + a) + j],
                    device_id=(px, py, pc),
                    device_id_type=MESH)
                cp.wait_send()
                cp.wait_recv()

    arrs = list(shards) + list(lands)
    out = pl.pallas_call(
        body, name=name, out_shape=tuple(pltpu.HBM(a.shape, a.dtype) for a in arrs),
        in_specs=[HBM_SPEC] * (2 * m) + [SEM_SPEC, SEM_SPEC, ANY], out_specs=(HBM_SPEC,) * (2 * m),
        input_output_aliases={i: i for i in range(2 * m)}, compiler_params=SPLIT_PARAMS,
    )(*arrs, send, recv, after)
    return out[m:]


def _forward_sibling(lands, cols, shard_shapes, name):
    m = len(lands)

    def body(*refs):
        ins, outs = refs[:m], refs[m:2 * m]
        send_sems, recv_sems = refs[2 * m:]
        x, y, c = _position()
        chips = [(1 - x, y), (x, 1 - y), (1 - x, 1 - y)]
        sends = []
        for a in range(m):
            r, cw = shard_shapes[a]
            for j, (px, py) in enumerate(chips):
                cp = pltpu.make_async_remote_copy(
                    src_ref=_weight_window(ins[a], cols[a], r, cw, px, py, c),
                    dst_ref=_weight_window(outs[a], cols[a], r, cw, px, py, c),
                    send_sem=send_sems.at[a, j], recv_sem=recv_sems.at[a, j], device_id=(x, y, 1 - c),
                    device_id_type=MESH)
                cp.start()
                sends.append(cp)
        for a in range(m):
            r, cw = shard_shapes[a]
            for j, (px, py) in enumerate(chips):
                pltpu.make_async_remote_copy(
                    src_ref=_weight_window(ins[a], cols[a], r, cw, px, py, c),
                    dst_ref=_weight_window(outs[a], cols[a], r, cw, px, py, 1 - c),
                    send_sem=send_sems.at[a, j], recv_sem=recv_sems.at[a, j], device_id=(x, y, 1 - c),
                    device_id_type=MESH).wait_recv()
        for cp in sends:
            cp.wait_send()

    return pl.pallas_call(
        body, name=name, out_shape=tuple(jax.ShapeDtypeStruct(a.shape, a.dtype) for a in lands),
        in_specs=[ANY] * m, out_specs=tuple([ANY] * m), input_output_aliases={i: i for i in range(m)},
        scratch_shapes=[pltpu.SemaphoreType.DMA((m, 3)), pltpu.SemaphoreType.DMA((m, 3))],
        compiler_params=pltpu.CompilerParams(vmem_limit_bytes=VMEM_LIMIT_BYTES),
    )(*lands)


def _swap_start(ins, name, other_half=False):
    n = len(ins)
    shape = lambda a: (a.shape[0],) + a.shape[2:] if other_half else a.shape
    lands = [lax.empty(shape(a), a.dtype) for a in ins]

    def body(*refs):
        src, ld = refs[:n], refs[n:2 * n]
        send_ref, recv_ref, token = refs[2 * n], refs[2 * n + 1], refs[-1]
        x, y, c = _position()
        for k in range(n):
            pltpu.make_async_remote_copy(
                src_ref=src[k].at[:, 1 - c] if other_half else src[k], dst_ref=ld[k], send_sem=send_ref.at[k],
                recv_sem=recv_ref.at[k], device_id=(x, y, 1 - c), device_id_type=MESH).start()
        token[...] = jnp.zeros_like(token)

    arrs = list(ins) + lands
    out = pl.pallas_call(
        body, name=name,
        out_shape=(pltpu.SemaphoreType.DMA((n,)), pltpu.SemaphoreType.DMA((n,)))
        + tuple(pltpu.HBM(a.shape, a.dtype) for a in arrs) + (TOKEN,),
        in_specs=[HBM_SPEC] * (2 * n),
        out_specs=(SEM_SPEC, SEM_SPEC) + (HBM_SPEC,) * (2 * n) + (pl.BlockSpec(memory_space=pltpu.VMEM),),
        input_output_aliases={i: 2 + i for i in range(2 * n)}, compiler_params=SPLIT_PARAMS,
    )(*[_hbm(a) for a in arrs])
    return out[0], out[1], out[2:2 + n], out[2 + n:2 + 2 * n], out[-1]


def _swap_wait(ins, lands, send, recv, after, name, other_half=False, first=0):
    n = len(ins)

    def body(*refs):
        src, ld = refs[:n], refs[n:2 * n]
        send_ref, recv_ref = refs[2 * n], refs[2 * n + 1]
        x, y, c = _position()
        for k in range(n):
            cp = pltpu.make_async_remote_copy(
                src_ref=src[k].at[:, 1 - c] if other_half else src[k], dst_ref=ld[k], send_sem=send_ref.at[first + k],
                recv_sem=recv_ref.at[first + k], device_id=(x, y, 1 - c), device_id_type=MESH)
            cp.wait_send()
            cp.wait_recv()

    arrs = list(ins) + list(lands)
    out = pl.pallas_call(
        body, name=name, out_shape=tuple(pltpu.HBM(a.shape, a.dtype) for a in arrs),
        in_specs=[HBM_SPEC] * (2 * n) + [SEM_SPEC, SEM_SPEC, ANY], out_specs=(HBM_SPEC,) * (2 * n),
        input_output_aliases={i: i for i in range(2 * n)}, compiler_params=SPLIT_PARAMS,
    )(*arrs, send, recv, after)
    return out[:n], out[n:]


def _forward_start(lands, cols, shard_shapes, name):
    m = len(lands)

    def body(*refs):
        ld = refs[:m]
        send_ref, recv_ref, token = refs[m], refs[m + 1], refs[-1]
        x, y, c = _position()
        for a in range(m):
            r, cw = shard_shapes[a]
            for j, (px, py) in enumerate([(1 - x, y), (x, 1 - y), (1 - x, 1 - y)]):
                win = _weight_window(ld[a], cols[a], r, cw, px, py, c)
                pltpu.make_async_remote_copy(src_ref=win, dst_ref=win, send_sem=send_ref.at[3 * a + j],
                                             recv_sem=recv_ref.at[3 * a + j], device_id=(x, y, 1 - c),
                                             device_id_type=MESH).start()
        token[...] = jnp.zeros_like(token)

    out = pl.pallas_call(
        body, name=name,
        out_shape=(pltpu.SemaphoreType.DMA((m * 3,)), pltpu.SemaphoreType.DMA((m * 3,)))
        + tuple(pltpu.HBM(a.shape, a.dtype) for a in lands) + (TOKEN,),
        in_specs=[HBM_SPEC] * m,
        out_specs=(SEM_SPEC, SEM_SPEC) + (HBM_SPEC,) * m + (pl.BlockSpec(memory_space=pltpu.VMEM),),
        input_output_aliases={i: 2 + i for i in range(m)}, compiler_params=SPLIT_PARAMS,
    )(*[_hbm(a) for a in lands])
    return out[0], out[1], out[2:2 + m], out[-1]


def _forward_wait(lands, send, recv, after, cols, shard_shapes, name):
    m = len(lands)

    def body(*refs):
        ld = refs[:m]
        send_ref, recv_ref = refs[m], refs[m + 1]
        x, y, c = _position()
        for a in range(m):
            r, cw = shard_shapes[a]
            for j, (px, py) in enumerate([(1 - x, y), (x, 1 - y), (1 - x, 1 - y)]):
                cp = pltpu.make_async_remote_copy(
                    src_ref=_weight_window(ld[a], cols[a], r, cw, px, py, c),
                    dst_ref=_weight_window(ld[a], cols[a], r, cw, px, py, 1 - c),
                    send_sem=send_ref.at[3 * a + j], recv_sem=recv_ref.at[3 * a + j], device_id=(x, y, 1 - c),
                    device_id_type=MESH)
                cp.wait_send()
                cp.wait_recv()

    return pl.pallas_call(
        body, name=name, out_shape=tuple(pltpu.HBM(a.shape, a.dtype) for a in lands),
        in_specs=[HBM_SPEC] * m + [SEM_SPEC, SEM_SPEC, ANY], out_specs=(HBM_SPEC,) * m,
        input_output_aliases={i: i for i in range(m)}, compiler_params=SPLIT_PARAMS,
    )(*lands, send, recv, after)


def _exchange_start(sums, name):
    m = len(sums)
    lands = [lax.empty((3,) + s.shape[1:], s.dtype) for s in sums]

    def body(*refs):
        cs, ld = refs[:m], refs[m:2 * m]
        send_ref, recv_ref, token = refs[2 * m], refs[2 * m + 1], refs[-1]
        x, y, c = _position()
        for a in range(m):
            for j, (px, py) in enumerate([(1 - x, y), (x, 1 - y), (1 - x, 1 - y)]):
                pltpu.make_async_remote_copy(
                    src_ref=cs[a].at[2 * px + py], dst_ref=ld[a].at[j], send_sem=send_ref.at[3 * a + j],
                    recv_sem=recv_ref.at[3 * a + j], device_id=(px, py, c), device_id_type=MESH).start()
        token[...] = jnp.zeros_like(token)

    arrs = list(sums) + lands
    out = pl.pallas_call(
        body, name=name,
        out_shape=(pltpu.SemaphoreType.DMA((m * 3,)), pltpu.SemaphoreType.DMA((m * 3,)))
        + tuple(pltpu.HBM(a.shape, a.dtype) for a in arrs) + (TOKEN,),
        in_specs=[HBM_SPEC] * (2 * m),
        out_specs=(SEM_SPEC, SEM_SPEC) + (HBM_SPEC,) * (2 * m) + (pl.BlockSpec(memory_space=pltpu.VMEM),),
        input_output_aliases={i: 2 + i for i in range(2 * m)}, compiler_params=SPLIT_PARAMS,
    )(*[_hbm(a) for a in arrs])
    return out[0], out[1], out[2:2 + m], out[2 + m:2 + 2 * m], out[-1]


def _exchange_wait(sums, lands, send, recv, after, name):
    m = len(sums)

    def body(*refs):
        cs, ld = refs[:m], refs[m:2 * m]
        send_ref, recv_ref = refs[2 * m], refs[2 * m + 1]
        x, y, c = _position()
        for a in range(m):
            for j, (px, py) in enumerate([(1 - x, y), (x, 1 - y), (1 - x, 1 - y)]):
                cp = pltpu.make_async_remote_copy(
                    src_ref=cs[a].at[2 * px + py], dst_ref=ld[a].at[j], send_sem=send_ref.at[3 * a + j],
                    recv_sem=recv_ref.at[3 * a + j], device_id=(px, py, c), device_id_type=MESH)
                cp.wait_send()
                cp.wait_recv()

    arrs = list(sums) + list(lands)
    out = pl.pallas_call(
        body, name=name, out_shape=tuple(pltpu.HBM(a.shape, a.dtype) for a in arrs),
        in_specs=[HBM_SPEC] * (2 * m) + [SEM_SPEC, SEM_SPEC, ANY], out_specs=(HBM_SPEC,) * (2 * m),
        input_output_aliases={i: i for i in range(2 * m)}, compiler_params=SPLIT_PARAMS,
    )(*arrs, send, recv, after)
    return out[:m], out[m:]


def _place_row(payload, dev, name):
    R = payload.shape[0]
    tr = _pick(R, (512, 448, 384, 256, 192, 128, 64, 32, 16, 8))

    def body(dev_ref, x_ref, o_ref):
        o_ref[...] = x_ref[...]

    return pl.pallas_call(
        body, name=name, out_shape=jax.ShapeDtypeStruct((N_DEV, R, LANES), payload.dtype),
        grid_spec=pltpu.PrefetchScalarGridSpec(
            num_scalar_prefetch=1, grid=(R // tr,),
            in_specs=[pl.BlockSpec((tr, LANES), lambda i, dev_ref: (i, 0))],
            out_specs=pl.BlockSpec((None, tr, LANES), lambda i, dev_ref: (dev_ref[0], i, 0))),
        compiler_params=_params(("arbitrary",)),
    )(dev, payload)


def _others(x, y, c):
    return [(1 - x if fx else x, 1 - y if fy else y, 1 - c if fc else c)
            for fx in (0, 1) for fy in (0, 1) for fc in (0, 1) if fx or fy or fc]


def _broadcast_start(payload, land, name):
    def body(p_ref, l_ref, send_ref, recv_ref, p_thru, l_thru, token):
        x, y, c = _position()
        for j, peer in enumerate(_others(x, y, c)):
            pltpu.make_async_remote_copy(src_ref=p_ref, dst_ref=l_ref.at[4 * x + 2 * y + c], send_sem=send_ref.at[j],
                                         recv_sem=recv_ref.at[j], device_id=peer, device_id_type=MESH).start()
        token[...] = jnp.zeros_like(token)

    n = N_DEV - 1
    return pl.pallas_call(
        body, name=name,
        out_shape=(pltpu.SemaphoreType.DMA((n,)), pltpu.SemaphoreType.DMA((n,)), pltpu.HBM(payload.shape, payload.dtype),
                   pltpu.HBM(land.shape, land.dtype), TOKEN),
        in_specs=[HBM_SPEC, HBM_SPEC],
        out_specs=(SEM_SPEC, SEM_SPEC, HBM_SPEC, HBM_SPEC, pl.BlockSpec(memory_space=pltpu.VMEM)),
        input_output_aliases={0: 2, 1: 3}, compiler_params=SPLIT_PARAMS,
    )(_hbm(payload), _hbm(land))


def _broadcast_wait(payload, land, send, recv, after, name):
    def body(p_ref, l_ref, send_ref, recv_ref, after_ref, p_thru, l_thru):
        x, y, c = _position()
        for j, (px, py, pc) in enumerate(_others(x, y, c)):
            cp = pltpu.make_async_remote_copy(src_ref=p_ref, dst_ref=l_ref.at[4 * px + 2 * py + pc],
                                              send_sem=send_ref.at[j], recv_sem=recv_ref.at[j],
                                              device_id=(px, py, pc), device_id_type=MESH)
            cp.wait_send()
            cp.wait_recv()

    out = pl.pallas_call(
        body, name=name, out_shape=(pltpu.HBM(payload.shape, payload.dtype), pltpu.HBM(land.shape, land.dtype)),
        in_specs=[HBM_SPEC, HBM_SPEC, SEM_SPEC, SEM_SPEC, ANY], out_specs=(HBM_SPEC, HBM_SPEC),
        input_output_aliases={0: 0, 1: 1}, compiler_params=SPLIT_PARAMS,
    )(payload, land, send, recv, after)
    return out[1]


def _vec(a):
    return a.reshape(1, -1)


def _local_step(x, tgt, mod, W, P, get_w=None, on_grads=None):
    Dm = D_MODEL
    G = {k: [] for k in ("gm_w_in", "gm_w_out", "hg_w_in", "hg_w_out", "ffn_w_up", "ffn_w_down")}
    lb_all = _lb_fwd(P["hg_lb"], "lb_fwd")
    saved = []
    xs = x
    y_prev = gate_prev = None
    layer_w = [None] * DEPTH

    def wmm(xa, kind, i, mode, name):
        if layer_w[i] is not None:
            return _mm(xa, layer_w[i][kind], mode, name)
        return _mm(xa, W[kind], mode, name, b_layer=i if kind.startswith("ffn") else i // 2)

    for i in range(DEPTH):
        m = [_vec(mod[i, j * Dm:(j + 1) * Dm]) for j in range(6)]
        sh1, sc1, g1, sh2, sc2, g2 = m
        j = i // 2
        if get_w is not None:
            layer_w[i] = get_w(i, 0, xs if y_prev is None else y_prev)
        xs, h = _norm_fwd(xs, y_prev, gate_prev, _vec(P["norm_g"][i, 0]), sc1, sh1, f"norm_fwd_a{i}")
        rec = dict(x1=xs, h1=h)
        if i % 2 == 0:
            a = wmm(h, "gm_w_in", i, "nn", f"gm_in{i}")
            gated = _gm_mid_fwd(a, _vec(P["gm_ln_g"][j]), _vec(P["gm_ln_b"][j]), P["gm_w_s"][j],
                                P["gm_b_s"][j].reshape(GM_HEADS, GM_BLOCK, 1), f"gm_mid_fwd{i}")
            if get_w is not None:
                layer_w[i].update(get_w(i, 1, gated))
            y1 = wmm(gated, "gm_w_out", i, "nn", f"gm_out{i}")
            rec.update(a=a, act=gated)
        else:
            p = wmm(h, "hg_w_in", i, "nn", f"hg_in{i}")
            o, og, states = _hg_scan_fwd(p, _vec(lb_all[j]), _vec(P["hg_gn_g"][j]), f"hg_scan_fwd{i}")
            if get_w is not None:
                layer_w[i].update(get_w(i, 1, og))
            y1 = wmm(og, "hg_w_out", i, "nn", f"hg_out{i}")
            rec.update(a=p, act=og, o=o, states=states)
        rec["y1"] = y1
        xs, h2 = _norm_fwd(xs, y1, g1, _vec(P["norm_g"][i, 1]), sc2, sh2, f"norm_fwd_b{i}")
        a2 = wmm(h2, "ffn_w_up", i, "nn", f"ffn_up{i}")
        mm_ = _conv_fwd(a2, P["ffn_conv_w"][i], _vec(P["ffn_conv_b"][i]), f"conv_fwd{i}")
        y2 = wmm(mm_, "ffn_w_down", i, "nn", f"ffn_down{i}")
        rec.update(x2=xs, h2=h2, a2=a2, m=mm_, y2=y2, mods=m)
        saved.append(rec)
        y_prev, gate_prev = y2, g2
    dx, dy, loss, s_fg, s_gate = _loss_head(xs, y_prev, gate_prev, _vec(P["final_g"]), tgt, "loss_head")
    small = dict(final_g=s_fg, norm_g=[None] * DEPTH, dmod=[None] * DEPTH, ffn_conv_w=[None] * DEPTH,
                 ffn_conv_b=[None] * DEPTH, gm_ln_g=[None] * 2, gm_ln_b=[None] * 2, gm_w_s=[None] * 2,
                 gm_b_s=[None] * 2, hg_gn_g=[None] * 2, dlb=[None] * 2)
    for i in reversed(range(DEPTH)):
        rec = saved[i]
        sh1, sc1, g1, sh2, sc2, g2 = rec["mods"]
        j = i // 2
        d_g2 = s_gate
        dm = wmm(dy, "ffn_w_down", i, "nt", f"ffn_down_dx{i}")
        G["ffn_w_down"].append(_mm(rec["m"], dy, "tn", f"ffn_down_dw{i}", out_dtype=GRAD_WIRE))
        w_up = layer_w[i]["ffn_w_up"] if layer_w[i] is not None else W["ffn_w_up"][i]
        da2, dcw, dcb, dh2 = _conv_bwd(rec["a2"], dm, P["ffn_conv_w"][i], _vec(P["ffn_conv_b"][i]), w_up,
                                       f"conv_bwd{i}")
        small["ffn_conv_w"][i], small["ffn_conv_b"][i] = dcw, dcb
        G["ffn_w_up"].append(_mm(rec["h2"], da2, "tn", f"ffn_up_dw{i}", out_dtype=GRAD_WIRE, exchange_out=True))
        ng2 = _vec(P["norm_g"][i, 1])
        if on_grads is not None:
            ng2 = ng2 + on_grads(i, {k: G[k][-1] for k in ("ffn_w_up", "ffn_w_down")})
        dx, dy, s_sh2, s_x2, d_g1 = _norm_bwd(rec["x2"], dh2, dx, ng2, sc2, rec["y1"], g1, f"norm_bwd_b{i}")
        d_sc2, d_ng2 = s_x2 * ng2, s_x2 * (1.0 + sc2)
        if i % 2 == 0:
            dgated = wmm(dy, "gm_w_out", i, "nt", f"gm_out_dx{i}")
            G["gm_w_out"].append(_mm(rec["act"], dy, "tn", f"gm_out_dw{i}", out_dtype=GRAD_WIRE))
            da, dws, dbs, dlg, dlbeta = _gm_mid_bwd(
                rec["a"], dgated, _vec(P["gm_ln_g"][j]), _vec(P["gm_ln_b"][j]), P["gm_w_s"][j],
                P["gm_b_s"][j].reshape(GM_HEADS, GM_BLOCK, 1), f"gm_mid_bwd{i}")
            small["gm_w_s"][j], small["gm_b_s"][j] = dws, dbs[:, :GM_HEADS].T
            small["gm_ln_g"][j], small["gm_ln_b"][j] = dlg, dlbeta
            dh1 = wmm(da, "gm_w_in", i, "nt", f"gm_in_dx{i}")
            G["gm_w_in"].append(_mm(rec["h1"], da, "tn", f"gm_in_dw{i}", out_dtype=GRAD_WIRE, exchange_out=True))
        else:
            dog = wmm(dy, "hg_w_out", i, "nt", f"hg_out_dx{i}")
            G["hg_w_out"].append(_mm(rec["act"], dy, "tn", f"hg_out_dw{i}", out_dtype=GRAD_WIRE))
            dp, dlb, dgn = _hg_scan_bwd(rec["a"], _vec(lb_all[j]), _vec(P["hg_gn_g"][j]), rec["o"], dog,
                                        rec["states"], f"hg_scan_bwd{i}")
            small["dlb"][j], small["hg_gn_g"][j] = dlb, dgn
            dh1 = wmm(dp, "hg_w_in", i, "nt", f"hg_in_dx{i}")
            G["hg_w_in"].append(_mm(rec["h1"], dp, "tn", f"hg_in_dw{i}", out_dtype=GRAD_WIRE, exchange_out=True))
        ng1 = _vec(P["norm_g"][i, 0])
        if on_grads is not None:
            mixer = ("gm_w_in", "gm_w_out") if i % 2 == 0 else ("hg_w_in", "hg_w_out")
            ng1 = ng1 + on_grads(i, {k: G[k][-1] for k in mixer})
        if i > 0:
            prev = saved[i - 1]
            dx, dy, s_sh1, s_x1, s_gate = _norm_bwd(rec["x1"], dh1, dx, ng1, sc1, prev["y2"], prev["mods"][5],
                                                    f"norm_bwd_a{i}")
        else:
            dx, s_sh1, s_x1 = _norm_bwd(rec["x1"], dh1, dx, ng1, sc1, None, None, f"norm_bwd_a{i}")
        d_sc1, d_ng1 = s_x1 * ng1, s_x1 * (1.0 + sc1)
        small["norm_g"][i] = jnp.concatenate([d_ng1, d_ng2], axis=0)
        small["dmod"][i] = jnp.concatenate([s_sh1, d_sc1, d_g1, s_sh2, d_sc2, d_g2], axis=1)
    for k in G:
        G[k] = G[k][::-1]
    dlb_all = jnp.concatenate(small.pop("dlb"), axis=0)
    small["hg_lb"] = _lb_bwd(P["hg_lb"], dlb_all, "lb_bwd")
    return loss, dx, G, small


BIG = ("gm_w_in", "gm_w_out", "hg_w_in", "hg_w_out", "ffn_w_up", "ffn_w_down")
COL_SHARDED = dict(gm_w_in=True, gm_w_out=False, hg_w_in=True, hg_w_out=False, ffn_w_up=True, ffn_w_down=False)
LAYER_WEIGHTS = 4


def _layer_kinds(i):
    return (("gm_w_in", "gm_w_out") if i % 2 == 0 else ("hg_w_in", "hg_w_out")) + ("ffn_w_up", "ffn_w_down")


def _pack(pieces):
    tile = 8 * LANES
    parts, offs, tot = [], [], 0
    for p in pieces:
        f = p.reshape(-1).astype(F32)
        offs.append((tot, f.shape[0]))
        pad = -f.shape[0] % tile
        parts.append(jnp.pad(f, (0, pad)).reshape(-1, LANES))
        tot += f.shape[0] + pad
    return jnp.concatenate(parts, axis=0), offs


def _unpack(rows, offs, shapes):
    lead = rows.shape[:-2]
    flat = rows.reshape(lead + (-1,))
    return [flat[..., o:o + n].reshape(lead + tuple(s)) for (o, n), s in zip(offs, shapes)]


def _from_chips(per_dev, axis):
    per_chip = per_dev[0::2]
    return jnp.concatenate([per_chip[s] for s in range(N_CHIPS)], axis=axis)


def kernel(x, c, gm_w_in, gm_ln_g, gm_ln_b, gm_w_s, gm_b_s, gm_w_out, hg_w_in, hg_lb, hg_gn_g, hg_w_out, ffn_w_up, ffn_conv_w, ffn_conv_b, ffn_w_down, norm_g, ada_w, ada_b, final_g, loss_target, m_gm_w_in, m_gm_ln_g, m_gm_ln_b, m_gm_w_s, m_gm_b_s, m_gm_w_out, m_hg_w_in, m_hg_lb, m_hg_gn_g, m_hg_w_out, m_ffn_w_up, m_ffn_conv_w, m_ffn_conv_b, m_ffn_w_down, m_norm_g, m_ada_w, m_ada_b, m_final_g, v_gm_w_in, v_gm_ln_g, v_gm_ln_b, v_gm_w_s, v_gm_b_s, v_gm_w_out, v_hg_w_in, v_hg_lb, v_hg_gn_g, v_hg_w_out, v_ffn_w_up, v_ffn_conv_w, v_ffn_conv_b, v_ffn_w_down, v_norm_g, v_ada_w, v_ada_b, v_final_g):
    Dm = D_MODEL
    xi, yi, ci = _position()
    chip = 2 * xi + yi
    dev = 4 * xi + 2 * yi + ci
    weights = dict(gm_w_in=gm_w_in, gm_ln_g=gm_ln_g, gm_ln_b=gm_ln_b, gm_w_s=gm_w_s, gm_b_s=gm_b_s,
                   gm_w_out=gm_w_out, hg_w_in=hg_w_in, hg_lb=hg_lb, hg_gn_g=hg_gn_g, hg_w_out=hg_w_out,
                   ffn_w_up=ffn_w_up, ffn_conv_w=ffn_conv_w, ffn_conv_b=ffn_conv_b, ffn_w_down=ffn_w_down,
                   norm_g=norm_g, ada_w=ada_w, ada_b=ada_b, final_g=final_g)
    mom_m = dict(gm_w_in=m_gm_w_in, gm_ln_g=m_gm_ln_g, gm_ln_b=m_gm_ln_b, gm_w_s=m_gm_w_s, gm_b_s=m_gm_b_s,
                 gm_w_out=m_gm_w_out, hg_w_in=m_hg_w_in, hg_lb=m_hg_lb, hg_gn_g=m_hg_gn_g, hg_w_out=m_hg_w_out,
                 ffn_w_up=m_ffn_w_up, ffn_conv_w=m_ffn_conv_w, ffn_conv_b=m_ffn_conv_b, ffn_w_down=m_ffn_w_down,
                 norm_g=m_norm_g, ada_w=m_ada_w, ada_b=m_ada_b, final_g=m_final_g)
    mom_v = dict(gm_w_in=v_gm_w_in, gm_ln_g=v_gm_ln_g, gm_ln_b=v_gm_ln_b, gm_w_s=v_gm_w_s, gm_b_s=v_gm_b_s,
                 gm_w_out=v_gm_w_out, hg_w_in=v_hg_w_in, hg_lb=v_hg_lb, hg_gn_g=v_hg_gn_g, hg_w_out=v_hg_w_out,
                 ffn_w_up=v_ffn_w_up, ffn_conv_w=v_ffn_conv_w, ffn_conv_b=v_ffn_conv_b, ffn_w_down=v_ffn_w_down,
                 norm_g=v_norm_g, ada_w=v_ada_w, ada_b=v_ada_b, final_g=v_final_g)
    order = list(weights)

    pos = jnp.stack([chip, ci]).astype(jnp.int32)
    shards, by_col = [], []
    for i in range(DEPTH):
        for k in _layer_kinds(i):
            shards.append(weights[k][i if k.startswith("ffn") else i // 2].astype(BF16))
            by_col.append(COL_SHARDED[k])
    placed = []
    for i in range(DEPTH):
        s = slice(LAYER_WEIGHTS * i, LAYER_WEIGHTS * (i + 1))
        placed += list(_place_own(shards[s], by_col[s], pos, f"place_own{i}"))
    gsems, sh_thru, ld_thru, _ = _gather_start(shards, placed, by_col, LAYER_WEIGHTS, "gather_start")

    pieces = [c, hg_lb, hg_gn_g, norm_g, ffn_conv_w]
    payload, offs = _pack(pieces)
    got = _allgather_small(payload, "gather_small")
    c_g, lb_g, gn_g, ng_g, cw_g = _unpack(got, offs, [p.shape for p in pieces])
    c_all = c_g.reshape(N_DEV, Dm)
    P = dict(hg_lb=_from_chips(lb_g, 1), hg_gn_g=_from_chips(gn_g, 1), norm_g=_from_chips(ng_g, 2),
             ffn_conv_w=_from_chips(cw_g, 2), gm_ln_g=gm_ln_g, gm_ln_b=gm_ln_b, gm_w_s=gm_w_s, gm_b_s=gm_b_s,
             ffn_conv_b=ffn_conv_b, final_g=final_g)

    cols = ada_w.shape[2]
    ada_b_sh = lax.dynamic_slice_in_dim(ada_b, chip * cols, cols, axis=1)
    mod_sh = _ada_fwd(c_all, ada_w, ada_b_sh, "ada_fwd")
    mod_g = _allgather_small(mod_sh.reshape(-1, LANES), "gather_mod").reshape(N_DEV, DEPTH, N_DEV, cols)
    mod_mine = lax.dynamic_index_in_dim(mod_g[0::2], dev, axis=2, keepdims=False)
    mod = jnp.transpose(mod_mine, (1, 0, 2)).reshape(DEPTH, N_CHIPS * cols)

    core = jnp.reshape(ci, (1,)).astype(jnp.int32)
    chip_arr = jnp.reshape(chip, (1,)).astype(jnp.int32)
    pending, held, prefetched, swapping = [], {}, {}, []

    def get_w(i, group, after):
        lo, hi = LAYER_WEIGHTS * i, LAYER_WEIGHTS * (i + 1)
        shapes = lambda s: [a.shape for a in shards[s]]
        out = {}
        if i == 0:
            s = slice(lo, lo + 1) if group == 0 else slice(lo + 1, hi)
            landed = _gather_wait(sh_thru[s], ld_thru[s], gsems[0], gsems[1], after, by_col[s], s.start - lo,
                                  f"gather_wait0_{group}")
            full = _forward_sibling(landed, by_col[s], shapes(s), f"gather_forward0_{group}")
            out = dict(zip(_layer_kinds(0)[s.start - lo:s.stop - lo], full))
        elif group == 0:
            s = slice(lo, hi)
            send, recv, lands = prefetched.pop(i)
            full = _forward_wait(lands, send, recv, after, by_col[s], shapes(s), f"gather_forward_wait{i}")
            out = dict(zip(_layer_kinds(i), full))
        if group == 1 and i + 1 < DEPTH:
            s = slice(hi, hi + LAYER_WEIGHTS)
            landed = _gather_wait(sh_thru[s], ld_thru[s], gsems[2 * i + 2], gsems[2 * i + 3], after, by_col[s], 0,
                                  f"gather_wait{i + 1}")
            send, recv, lands, _ = _forward_start(landed, by_col[s], shapes(s), f"gather_forward_start{i + 1}")
            prefetched[i + 1] = (send, recv, lands)
        return out

    def on_grads(i, gdict):
        if i > 0 and "ffn_w_up" in gdict:
            held[i] = gdict
            return 0.0
        gdict = {**held.pop(i, {}), **gdict}
        kinds = [k for k in _layer_kinds(i) if k in gdict]
        tag = f"{i}_ffn" if kinds[0] == "ffn_w_up" else f"{i}"
        g4 = []
        for k in kinds:
            g = gdict[k]
            if not COL_SHARDED[k]:
                R, C = g.shape
                g = g.reshape(N_CHIPS, 2, R // (2 * N_CHIPS), C)
            g4.append(g)
        token = finish_swap(g4[0]) if swapping else 0.0
        if i == 0:
            from_sib = _swap_sibling(g4, f"reduce_swap{tag}", other_half=True)
            return token + start_exchange(tag, i, kinds, g4, from_sib)
        send, recv, g_thru, lands, tok = _swap_start(g4, f"reduce_swap_start{tag}", other_half=True)
        swapping.append((tag, i, kinds, send, recv, g_thru, lands))
        return token + tok[0, 0]

    def start_exchange(tag, i, kinds, g4, from_sib):
        sums = list(_add_own_halves(list(g4), list(from_sib), core, f"chip_sum{tag}"))
        send, recv, sums_thru, lands, token = _exchange_start(sums, f"reduce_start{tag}")
        pending.append((tag, i, kinds, send, recv, sums_thru, lands))
        return token[0, 0]

    def finish_swap(after):
        tag, i, kinds, send, recv, g_thru, lands = swapping.pop()
        g4, from_sib = _swap_wait(g_thru, lands, send, recv, after, f"reduce_swap_wait{tag}", other_half=True)
        return start_exchange(tag, i, kinds, g4, from_sib)

    loss_part, dx, G, small = _local_step(x[0], loss_target[0], mod, None, P, get_w, on_grads)

    sum_pieces = [loss_part[:, :1], small["final_g"], jnp.stack(small["gm_ln_g"]), jnp.stack(small["gm_ln_b"]),
                  jnp.stack(small["gm_w_s"]), jnp.stack(small["gm_b_s"]), jnp.stack(small["ffn_conv_b"]),
                  small["hg_lb"], jnp.stack(small["hg_gn_g"]), jnp.stack(small["norm_g"]),
                  jnp.stack(small["ffn_conv_w"])]
    dmod = jnp.concatenate(small["dmod"], axis=0)
    payload2, offs2 = _pack(sum_pieces + [dmod])
    placed2 = _place_row(payload2, jnp.reshape(dev, (1,)).astype(jnp.int32), "place_grads")
    bsend, brecv, p2_thru, l2_thru, small_token = _broadcast_start(payload2, placed2, "gather_grads_start")

    landed = {}
    for tag, i, kinds, send, recv, sums_thru, lands in pending:
        sums_i, lands_i = _exchange_wait(sums_thru, lands, send, recv, small_token, f"reduce_wait{tag}")
        for k, s_, l_ in zip(kinds, sums_i, lands_i):
            landed[(k, i)] = (l_, s_)
    own_halves = []
    for k in BIG:
        layers = [landed[(k, i)] for i in range(DEPTH) if (k, i) in landed]
        own_halves.append(_sum_chips([l_ for l_, _ in layers], [s_ for _, s_ in layers], chip_arr, f"sum_chips_{k}"))
    jsend, jrecv, own_thru, jlands, after = _swap_start(own_halves, "reduce_join_start")
    grads, deltas, new_m, new_v = {}, {}, {}, {}
    for n, k in enumerate(BIG):
        (own,), (recv,) = _swap_wait([own_thru[n]], [jlands[n]], jsend, jrecv, after, f"reduce_join_wait_{k}",
                                     first=n)
        grads[k], deltas[k], new_m[k], new_v[k] = _adamw_halves(
            weights[k], own, recv, mom_m[k], mom_v[k], core, f"adamw_{k}")
        after = new_v[k]

    got2 = _broadcast_wait(p2_thru, l2_thru, bsend, brecv, new_v[BIG[-1]], "gather_grads_wait")
    dmod_all = _unpack(got2, offs2[-1:], [dmod.shape])[0]
    summed = _sum_devices(got2, "sum_devices")
    (loss_s, d_final_g, d_ln_g, d_ln_b, d_ws, d_bs, d_cb, d_lb, d_gn, d_ng, d_cw) = _unpack(
        summed, offs2[:-1], [(1,), final_g.shape, gm_ln_g.shape, gm_ln_b.shape, gm_w_s.shape, gm_b_s.shape,
                             ffn_conv_b.shape, (2, Dm), (2, Dm), (DEPTH, 2, Dm), (DEPTH, 3, 2 * FFN_HIDDEN)])
    grads.update(final_g=d_final_g, gm_ln_g=d_ln_g, gm_ln_b=d_ln_b, gm_w_s=d_ws, gm_b_s=d_bs, ffn_conv_b=d_cb)
    grads["hg_lb"] = lax.dynamic_slice_in_dim(d_lb, chip * hg_lb.shape[1], hg_lb.shape[1], axis=1)
    grads["hg_gn_g"] = lax.dynamic_slice_in_dim(d_gn, chip * hg_gn_g.shape[1], hg_gn_g.shape[1], axis=1)
    grads["norm_g"] = lax.dynamic_slice_in_dim(d_ng, chip * norm_g.shape[2], norm_g.shape[2], axis=2)
    grads["ffn_conv_w"] = lax.dynamic_slice_in_dim(d_cw, chip * ffn_conv_w.shape[2], ffn_conv_w.shape[2], axis=2)
    dmod_sh = lax.dynamic_slice_in_dim(dmod_all, chip * cols, cols, axis=2)
    grads["ada_w"] = _ada_bwd(c_all, jnp.transpose(dmod_sh, (1, 0, 2)), "ada_bwd")
    grads["ada_b"] = _sum_devices(dmod_all.reshape(N_DEV, -1, LANES), "sum_ada_b").reshape(ada_b.shape)

    for k in order:
        if k in BIG:
            continue
        w = weights[k]
        shp = w.shape
        view = (-1, shp[-1]) if w.ndim > 1 else (8, -1)
        d, m2, v2 = _adamw(w.reshape(view), grads[k].reshape(view), mom_m[k].reshape(view), mom_v[k].reshape(view),
                           f"adamw_{k}")
        deltas[k], new_m[k], new_v[k] = d.reshape(shp), m2.reshape(shp), v2.reshape(shp)
        grads[k] = grads[k].reshape(shp)

    loss = loss_s.reshape(())
    return (loss, dx[None], *[grads[k] for k in order], *[deltas[k] for k in order],
            *[new_m[k] for k in order], *[new_v[k] for k in order])
```
